```python
import jax, jax.numpy as jnp
from jax import lax
import numpy as np

D_MODEL = 1024
BATCH = 8
SEQ = 8192
DEPTH = 1

HEAD_DIM = 64
N_FOX_HEADS = 8
N_SB_HEADS = 8
FOX_WIDTH = N_FOX_HEADS * HEAD_DIM
SB_WIDTH = N_SB_HEADS * HEAD_DIM
MIX_WIDTH = FOX_WIDTH + SB_WIDTH
IN_COLS = 3 * FOX_WIDTH + 3 * SB_WIDTH + N_FOX_HEADS
Q_BLOCK = 128
N_MEM = 256
N_MEM_HEADS = 4
MEM_HEAD_DIM = D_MODEL // N_MEM_HEADS
D_FF = ((8 * D_MODEL // 3 + 255) // 256) * 256
CONV_WIDTH = 3
EPS = 1e-6

kernel_name = "hybrid_fox_stickbreak_memxattn_convffn"


def _rmsnorm(x, g):
    xf = x.astype(jnp.float32)
    y = xf * lax.rsqrt(jnp.mean(xf * xf, axis=-1, keepdims=True) + EPS)
    return (y * g.astype(jnp.float32)).astype(x.dtype)


def _heads(t, n):
    b, s, w = t.shape
    return t.reshape(b, s, n, w // n).transpose(0, 2, 1, 3)


def _merge(t):
    b, n, s, d = t.shape
    return t.transpose(0, 2, 1, 3).reshape(b, s, n * d)


def _sweep(fn, q, *per_query):
    b, h, s, d = q.shape
    n = s // Q_BLOCK
    qb = q.reshape(b, h, n, Q_BLOCK, d).transpose(2, 0, 1, 3, 4)
    extras = [e.reshape(b, h, n, Q_BLOCK).transpose(2, 0, 1, 3) for e in per_query]
    out = lax.map(lambda a: fn(*a), (jnp.arange(n), qb, *extras))
    return out.transpose(1, 2, 0, 3, 4).reshape(b, h, s, d)


def _fox_attention(q, k, v, log_f):
    s_len = q.shape[2]
    scale = HEAD_DIM ** -0.5
    c = jnp.cumsum(log_f.astype(jnp.float32), axis=-1)
    kpos = jnp.arange(s_len)

    def block(i, qi, ci):
        qpos = i * Q_BLOCK + jnp.arange(Q_BLOCK)
        logits = jnp.einsum('bhqd,bhkd->bhqk', qi, k, preferred_element_type=jnp.float32) * scale
        logits = logits + ci[..., None] - c[:, :, None, :]
        logits = jnp.where(kpos[None, :] <= qpos[:, None], logits, -jnp.inf)
        p = jax.nn.softmax(logits, axis=-1)
        return jnp.einsum('bhqk,bhkd->bhqd', p.astype(v.dtype), v)

    return _sweep(block, q, c)


def _stick_breaking_attention(q, k, v):
    s_len = q.shape[2]
    scale = HEAD_DIM ** -0.5
    kpos = jnp.arange(s_len)

    def block(i, qi):
        qpos = i * Q_BLOCK + jnp.arange(Q_BLOCK)
        z = jnp.einsum('bhqd,bhkd->bhqk', qi, k, preferred_element_type=jnp.float32) * scale
        strict = kpos[None, :] < qpos[:, None]
        log_keep = jnp.where(strict, jax.nn.log_sigmoid(-z), 0.0)
        after = lax.cumsum(log_keep, axis=3, reverse=True) - log_keep
        a = jnp.where(strict, jnp.exp(jax.nn.log_sigmoid(z) + after), 0.0)
        return jnp.einsum('bhqk,bhkd->bhqd', a.astype(v.dtype), v)

    return _sweep(block, q)


def _memory_cross_attention(h, m, w_q, w_kv, w_o):
    q = _heads(jnp.einsum('bsd,dc->bsc', h, w_q), N_MEM_HEADS)
    kv = jnp.einsum('bmd,dc->bmc', m, w_kv)
    k, v = jnp.split(kv, 2, axis=-1)
    k, v = _heads(k, N_MEM_HEADS), _heads(v, N_MEM_HEADS)
    logits = jnp.einsum('bhqd,bhkd->bhqk', q, k, preferred_element_type=jnp.float32) * (MEM_HEAD_DIM ** -0.5)
    p = jax.nn.softmax(logits, axis=-1)
    o = jnp.einsum('bhqk,bhkd->bhqd', p.astype(v.dtype), v)
    return jnp.einsum('bsc,cd->bsd', _merge(o), w_o)


def _causal_dwconv(u, w, b):
    s_len = u.shape[1]
    up = jnp.pad(u, ((0, 0), (CONV_WIDTH - 1, 0), (0, 0)))
    y = b
    for i in range(CONV_WIDTH):
        y = y + up[:, i:i + s_len] * w[i]
    return y


def _fwd_setup_inputs(seed: int = 0) -> dict:
    key = jax.random.key(seed)
    ks = jax.random.split(key, 24)
    f32 = jnp.float32

    def nrm(k, shape, fan_in):
        return jax.random.normal(k, shape, f32) * (fan_in ** -0.5)

    def gain(k, shape):
        return 1.0 + 0.02 * jax.random.normal(k, shape, f32)

    return {
        "x": jax.random.normal(ks[0], (BATCH, SEQ, D_MODEL), f32),
        "mem": jax.random.normal(ks[1], (BATCH, N_MEM, D_MODEL), f32),
        "attn_norm_g": gain(ks[2], (DEPTH, D_MODEL)),
        "w_in": nrm(ks[3], (DEPTH, D_MODEL, IN_COLS), D_MODEL),
        "b_forget": jnp.linspace(1.0, 6.0, N_FOX_HEADS, dtype=f32)[None, :]
                    + 0.1 * jax.random.normal(ks[4], (DEPTH, N_FOX_HEADS), f32),
        "fox_out_g": gain(ks[5], (DEPTH, FOX_WIDTH)),
        "sb_out_g": gain(ks[6], (DEPTH, SB_WIDTH)),
        "w_out": nrm(ks[7], (DEPTH, MIX_WIDTH, D_MODEL), MIX_WIDTH),
        "xattn_norm_g": gain(ks[8], (DEPTH, D_MODEL)),
        "mem_norm_g": gain(ks[9], (DEPTH, D_MODEL)),
        "w_mq": nrm(ks[10], (DEPTH, D_MODEL, D_MODEL), D_MODEL),
        "w_mkv": nrm(ks[11], (DEPTH, D_MODEL, 2 * D_MODEL), D_MODEL),
        "w_mo": nrm(ks[12], (DEPTH, D_MODEL, D_MODEL), D_MODEL),
        "ffn_norm_g": gain(ks[13], (DEPTH, D_MODEL)),
        "w_up": nrm(ks[14], (DEPTH, D_MODEL, 2 * D_FF), D_MODEL),
        "conv_w": nrm(ks[15], (DEPTH, CONV_WIDTH, 2 * D_FF), CONV_WIDTH),
        "conv_b": 0.02 * jax.random.normal(ks[16], (DEPTH, 2 * D_FF), f32),
        "w_down": nrm(ks[17], (DEPTH, D_FF, D_MODEL), D_FF),
        "final_norm_g": gain(ks[18], (D_MODEL,)),
    }


def _fwd_reference(x, mem, attn_norm_g, w_in, b_forget, fox_out_g, sb_out_g, w_out,
              xattn_norm_g, mem_norm_g, w_mq, w_mkv, w_mo,
              ffn_norm_g, w_up, conv_w, conv_b, w_down, final_norm_g):
    split_at = np.cumsum([FOX_WIDTH, FOX_WIDTH, FOX_WIDTH, SB_WIDTH, SB_WIDTH, SB_WIDTH]).tolist()
    for l in range(DEPTH):
        h = _rmsnorm(x, attn_norm_g[l])
        proj = jnp.einsum('bsd,dc->bsc', h, w_in[l])
        fq, fk, fv, sq, sk, sv, f_logit = jnp.split(proj, split_at, axis=-1)
        log_f = jax.nn.log_sigmoid(f_logit.astype(jnp.float32) + b_forget[l].astype(jnp.float32))
        log_f = log_f.transpose(0, 2, 1)
        fox_o = _merge(_fox_attention(_heads(fq, N_FOX_HEADS), _heads(fk, N_FOX_HEADS),
                                      _heads(fv, N_FOX_HEADS), log_f))
        sb_o = _merge(_stick_breaking_attention(_heads(sq, N_SB_HEADS), _heads(sk, N_SB_HEADS),
                                                _heads(sv, N_SB_HEADS)))
        mixed = jnp.concatenate([_rmsnorm(fox_o, fox_out_g[l]), _rmsnorm(sb_o, sb_out_g[l])], axis=-1)
        x = x + jnp.einsum('bsc,cd->bsd', mixed, w_out[l])

        h = _rmsnorm(x, xattn_norm_g[l])
        m = _rmsnorm(mem, mem_norm_g[l])
        x = x + _memory_cross_attention(h, m, w_mq[l], w_mkv[l], w_mo[l])

        h = _rmsnorm(x, ffn_norm_g[l])
        u = jnp.einsum('bsd,df->bsf', h, w_up[l])
        u = _causal_dwconv(u, conv_w[l], conv_b[l])
        gate, val = jnp.split(u, 2, axis=-1)
        x = x + jnp.einsum('bsf,fd->bsd', jax.nn.silu(gate) * val, w_down[l])
    return _rmsnorm(x, final_norm_g)


import jax as _jax
import jax.numpy as _jnp

TWIN_FORMAT = 'train_step'
FWD_PARAMS = ['x', 'mem', 'attn_norm_g', 'w_in', 'b_forget', 'fox_out_g', 'sb_out_g', 'w_out', 'xattn_norm_g', 'mem_norm_g', 'w_mq', 'w_mkv', 'w_mo', 'ffn_norm_g', 'w_up', 'conv_w', 'conv_b', 'w_down', 'final_norm_g']
TWIN_WEIGHTS = ['attn_norm_g', 'w_in', 'b_forget', 'fox_out_g', 'sb_out_g', 'w_out', 'xattn_norm_g', 'mem_norm_g', 'w_mq', 'w_mkv', 'w_mo', 'ffn_norm_g', 'w_up', 'conv_w', 'conv_b', 'w_down', 'final_norm_g']
TWIN_DIFF_INPUT = 'x'
TWIN_INPUTS = ['x', 'mem', 'attn_norm_g', 'w_in', 'b_forget', 'fox_out_g', 'sb_out_g', 'w_out', 'xattn_norm_g', 'mem_norm_g', 'w_mq', 'w_mkv', 'w_mo', 'ffn_norm_g', 'w_up', 'conv_w', 'conv_b', 'w_down', 'final_norm_g', 'loss_target', 'm_attn_norm_g', 'm_w_in', 'm_b_forget', 'm_fox_out_g', 'm_sb_out_g', 'm_w_out', 'm_xattn_norm_g', 'm_mem_norm_g', 'm_w_mq', 'm_w_mkv', 'm_w_mo', 'm_ffn_norm_g', 'm_w_up', 'm_conv_w', 'm_conv_b', 'm_w_down', 'm_final_norm_g', 'v_attn_norm_g', 'v_w_in', 'v_b_forget', 'v_fox_out_g', 'v_sb_out_g', 'v_w_out', 'v_xattn_norm_g', 'v_mem_norm_g', 'v_w_mq', 'v_w_mkv', 'v_w_mo', 'v_ffn_norm_g', 'v_w_up', 'v_conv_w', 'v_conv_b', 'v_w_down', 'v_final_norm_g']
TWIN_OUTPUTS = ['loss', 'grad_x', 'grad_attn_norm_g', 'grad_w_in', 'grad_b_forget', 'grad_fox_out_g', 'grad_sb_out_g', 'grad_w_out', 'grad_xattn_norm_g', 'grad_mem_norm_g', 'grad_w_mq', 'grad_w_mkv', 'grad_w_mo', 'grad_ffn_norm_g', 'grad_w_up', 'grad_conv_w', 'grad_conv_b', 'grad_w_down', 'grad_final_norm_g', 'delta_attn_norm_g', 'delta_w_in', 'delta_b_forget', 'delta_fox_out_g', 'delta_sb_out_g', 'delta_w_out', 'delta_xattn_norm_g', 'delta_mem_norm_g', 'delta_w_mq', 'delta_w_mkv', 'delta_w_mo', 'delta_ffn_norm_g', 'delta_w_up', 'delta_conv_w', 'delta_conv_b', 'delta_w_down', 'delta_final_norm_g', 'new_m_attn_norm_g', 'new_m_w_in', 'new_m_b_forget', 'new_m_fox_out_g', 'new_m_sb_out_g', 'new_m_w_out', 'new_m_xattn_norm_g', 'new_m_mem_norm_g', 'new_m_w_mq', 'new_m_w_mkv', 'new_m_w_mo', 'new_m_ffn_norm_g', 'new_m_w_up', 'new_m_conv_w', 'new_m_conv_b', 'new_m_w_down', 'new_m_final_norm_g', 'new_v_attn_norm_g', 'new_v_w_in', 'new_v_b_forget', 'new_v_fox_out_g', 'new_v_sb_out_g', 'new_v_w_out', 'new_v_xattn_norm_g', 'new_v_mem_norm_g', 'new_v_w_mq', 'new_v_w_mkv', 'new_v_w_mo', 'new_v_ffn_norm_g', 'new_v_w_up', 'new_v_conv_w', 'new_v_conv_b', 'new_v_w_down', 'new_v_final_norm_g']
TWIN_LEAF_KINDS = {'loss': 'loss', 'grad_x': 'grad_x', 'grad_attn_norm_g': 'grad_w', 'grad_w_in': 'grad_w', 'grad_b_forget': 'grad_w', 'grad_fox_out_g': 'grad_w', 'grad_sb_out_g': 'grad_w', 'grad_w_out': 'grad_w', 'grad_xattn_norm_g': 'grad_w', 'grad_mem_norm_g': 'grad_w', 'grad_w_mq': 'grad_w', 'grad_w_mkv': 'grad_w', 'grad_w_mo': 'grad_w', 'grad_ffn_norm_g': 'grad_w', 'grad_w_up': 'grad_w', 'grad_conv_w': 'grad_w', 'grad_conv_b': 'grad_w', 'grad_w_down': 'grad_w', 'grad_final_norm_g': 'grad_w', 'delta_attn_norm_g': 'delta_w', 'delta_w_in': 'delta_w', 'delta_b_forget': 'delta_w', 'delta_fox_out_g': 'delta_w', 'delta_sb_out_g': 'delta_w', 'delta_w_out': 'delta_w', 'delta_xattn_norm_g': 'delta_w', 'delta_mem_norm_g': 'delta_w', 'delta_w_mq': 'delta_w', 'delta_w_mkv': 'delta_w', 'delta_w_mo': 'delta_w', 'delta_ffn_norm_g': 'delta_w', 'delta_w_up': 'delta_w', 'delta_conv_w': 'delta_w', 'delta_conv_b': 'delta_w', 'delta_w_down': 'delta_w', 'delta_final_norm_g': 'delta_w', 'new_m_attn_norm_g': 'new_m', 'new_m_w_in': 'new_m', 'new_m_b_forget': 'new_m', 'new_m_fox_out_g': 'new_m', 'new_m_sb_out_g': 'new_m', 'new_m_w_out': 'new_m', 'new_m_xattn_norm_g': 'new_m', 'new_m_mem_norm_g': 'new_m', 'new_m_w_mq': 'new_m', 'new_m_w_mkv': 'new_m', 'new_m_w_mo': 'new_m', 'new_m_ffn_norm_g': 'new_m', 'new_m_w_up': 'new_m', 'new_m_conv_w': 'new_m', 'new_m_conv_b': 'new_m', 'new_m_w_down': 'new_m', 'new_m_final_norm_g': 'new_m', 'new_v_attn_norm_g': 'new_v', 'new_v_w_in': 'new_v', 'new_v_b_forget': 'new_v', 'new_v_fox_out_g': 'new_v', 'new_v_sb_out_g': 'new_v', 'new_v_w_out': 'new_v', 'new_v_xattn_norm_g': 'new_v', 'new_v_mem_norm_g': 'new_v', 'new_v_w_mq': 'new_v', 'new_v_w_mkv': 'new_v', 'new_v_w_mo': 'new_v', 'new_v_ffn_norm_g': 'new_v', 'new_v_w_up': 'new_v', 'new_v_conv_w': 'new_v', 'new_v_conv_b': 'new_v', 'new_v_w_down': 'new_v', 'new_v_final_norm_g': 'new_v'}


def _forward(args):
    return _fwd_reference(*[args[k] for k in FWD_PARAMS])


def _output_shape():
    def fwd():
        inp = _fwd_setup_inputs(0)
        return _fwd_reference(*[inp[k] for k in FWD_PARAMS])
    out = _jax.eval_shape(fwd)
    return out.shape, out.dtype

N_MICROBATCH = 1
ADAM_LR = 0.001
ADAM_B1 = 0.9
ADAM_B2 = 0.999
ADAM_EPS = 1e-08
ADAM_WD = 0.01
ADAM_STEP = 10
PER_EXAMPLE_BATCH_AXIS = {'x': 0, 'mem': 0, 'loss_target': 0}
SHARED_INPUTS = []
_WEIGHT_DTYPES = {'attn_norm_g': _jnp.float32, 'w_in': _jnp.float32, 'b_forget': _jnp.float32, 'fox_out_g': _jnp.float32, 'sb_out_g': _jnp.float32, 'w_out': _jnp.float32, 'xattn_norm_g': _jnp.float32, 'mem_norm_g': _jnp.float32, 'w_mq': _jnp.float32, 'w_mkv': _jnp.float32, 'w_mo': _jnp.float32, 'ffn_norm_g': _jnp.float32, 'w_up': _jnp.float32, 'conv_w': _jnp.float32, 'conv_b': _jnp.float32, 'w_down': _jnp.float32, 'final_norm_g': _jnp.float32}
MOMENT_SCALE = {'attn_norm_g': 2.825582e-01, 'w_in': 1.498349e-01, 'b_forget': 2.624448e+00, 'fox_out_g': 2.305417e-01, 'sb_out_g': 1.957680e-01, 'w_out': 1.930168e-01, 'xattn_norm_g': 2.018592e-02, 'mem_norm_g': 3.098938e-02, 'w_mq': 1.988853e-02, 'w_mkv': 2.003081e-02, 'w_mo': 2.031934e-02, 'ffn_norm_g': 1.509050e-01, 'w_up': 6.130788e-02, 'conv_w': 6.256054e-02, 'conv_b': 6.135867e-02, 'w_down': 1.012209e-01, 'final_norm_g': 6.405889e+01}


def _to_microbatches(a, axis):
    t = _jnp.moveaxis(a, axis, 0)
    t = t.reshape((N_MICROBATCH, t.shape[0] // N_MICROBATCH) + t.shape[1:])
    return _jnp.moveaxis(t, 1, axis + 1)


def setup_inputs(seed: int = 0) -> dict:
    inp = _fwd_setup_inputs(seed)
    key = _jax.random.fold_in(_jax.random.key(seed), 7919)
    shape, _ = _output_shape()
    out = dict(inp)
    out["loss_target"] = _jax.random.normal(_jax.random.fold_in(key, 0), shape, _jnp.float32)
    for i, name in enumerate(TWIN_WEIGHTS):
        w = inp[name].astype(_jnp.float32)
        if MOMENT_SCALE is None:
            s = _jnp.sqrt(_jnp.mean(_jnp.square(w)) + 1e-30)
        else:
            s = MOMENT_SCALE[name]
        km, kv = _jax.random.split(_jax.random.fold_in(key, i + 1))
        out[name] = w
        out["m_" + name] = s * _jax.random.normal(km, w.shape, _jnp.float32)
        out["v_" + name] = (s * s) * _jax.random.uniform(kv, w.shape, _jnp.float32, 0.5, 1.5)
    if N_MICROBATCH > 1:
        for name, axis in PER_EXAMPLE_BATCH_AXIS.items():
            out[name] = _to_microbatches(out[name], axis)
    return {'x': out['x'], 'mem': out['mem'], 'attn_norm_g': out['attn_norm_g'], 'w_in': out['w_in'], 'b_forget': out['b_forget'], 'fox_out_g': out['fox_out_g'], 'sb_out_g': out['sb_out_g'], 'w_out': out['w_out'], 'xattn_norm_g': out['xattn_norm_g'], 'mem_norm_g': out['mem_norm_g'], 'w_mq': out['w_mq'], 'w_mkv': out['w_mkv'], 'w_mo': out['w_mo'], 'ffn_norm_g': out['ffn_norm_g'], 'w_up': out['w_up'], 'conv_w': out['conv_w'], 'conv_b': out['conv_b'], 'w_down': out['w_down'], 'final_norm_g': out['final_norm_g'], 'loss_target': out['loss_target'], 'm_attn_norm_g': out['m_attn_norm_g'], 'm_w_in': out['m_w_in'], 'm_b_forget': out['m_b_forget'], 'm_fox_out_g': out['m_fox_out_g'], 'm_sb_out_g': out['m_sb_out_g'], 'm_w_out': out['m_w_out'], 'm_xattn_norm_g': out['m_xattn_norm_g'], 'm_mem_norm_g': out['m_mem_norm_g'], 'm_w_mq': out['m_w_mq'], 'm_w_mkv': out['m_w_mkv'], 'm_w_mo': out['m_w_mo'], 'm_ffn_norm_g': out['m_ffn_norm_g'], 'm_w_up': out['m_w_up'], 'm_conv_w': out['m_conv_w'], 'm_conv_b': out['m_conv_b'], 'm_w_down': out['m_w_down'], 'm_final_norm_g': out['m_final_norm_g'], 'v_attn_norm_g': out['v_attn_norm_g'], 'v_w_in': out['v_w_in'], 'v_b_forget': out['v_b_forget'], 'v_fox_out_g': out['v_fox_out_g'], 'v_sb_out_g': out['v_sb_out_g'], 'v_w_out': out['v_w_out'], 'v_xattn_norm_g': out['v_xattn_norm_g'], 'v_mem_norm_g': out['v_mem_norm_g'], 'v_w_mq': out['v_w_mq'], 'v_w_mkv': out['v_w_mkv'], 'v_w_mo': out['v_w_mo'], 'v_ffn_norm_g': out['v_ffn_norm_g'], 'v_w_up': out['v_w_up'], 'v_conv_w': out['v_conv_w'], 'v_conv_b': out['v_conv_b'], 'v_w_down': out['v_w_down'], 'v_final_norm_g': out['v_final_norm_g']}


def _loss(weights, diff, rest, loss_target):
    with _jax.named_scope("forward"):
        args = {**rest, TWIN_DIFF_INPUT: diff, **{k: w.astype(_WEIGHT_DTYPES[k]) for k, w in weights.items()}}
        y = _forward(args)
    with _jax.named_scope("loss_head"):
        err = _jnp.square(y.astype(_jnp.float32) - loss_target)
        return 0.5 * _jnp.sum(_jnp.mean(err, axis=-1)) if err.ndim else 0.5 * err


def _adamw(w, g, m, v):
    m = ADAM_B1 * m + (1.0 - ADAM_B1) * g
    v = ADAM_B2 * v + (1.0 - ADAM_B2) * _jnp.square(g)
    m_hat = m / (1.0 - ADAM_B1 ** ADAM_STEP)
    v_hat = v / (1.0 - ADAM_B2 ** ADAM_STEP)
    delta = -ADAM_LR * (m_hat / (_jnp.sqrt(v_hat) + ADAM_EPS) + ADAM_WD * w)
    return delta, m, v


def reference(x, mem, attn_norm_g, w_in, b_forget, fox_out_g, sb_out_g, w_out, xattn_norm_g, mem_norm_g, w_mq, w_mkv, w_mo, ffn_norm_g, w_up, conv_w, conv_b, w_down, final_norm_g, loss_target, m_attn_norm_g, m_w_in, m_b_forget, m_fox_out_g, m_sb_out_g, m_w_out, m_xattn_norm_g, m_mem_norm_g, m_w_mq, m_w_mkv, m_w_mo, m_ffn_norm_g, m_w_up, m_conv_w, m_conv_b, m_w_down, m_final_norm_g, v_attn_norm_g, v_w_in, v_b_forget, v_fox_out_g, v_sb_out_g, v_w_out, v_xattn_norm_g, v_mem_norm_g, v_w_mq, v_w_mkv, v_w_mo, v_ffn_norm_g, v_w_up, v_conv_w, v_conv_b, v_w_down, v_final_norm_g):
    given = dict(x=x, mem=mem, attn_norm_g=attn_norm_g, w_in=w_in, b_forget=b_forget, fox_out_g=fox_out_g, sb_out_g=sb_out_g, w_out=w_out, xattn_norm_g=xattn_norm_g, mem_norm_g=mem_norm_g, w_mq=w_mq, w_mkv=w_mkv, w_mo=w_mo, ffn_norm_g=ffn_norm_g, w_up=w_up, conv_w=conv_w, conv_b=conv_b, w_down=w_down, final_norm_g=final_norm_g, loss_target=loss_target, m_attn_norm_g=m_attn_norm_g, m_w_in=m_w_in, m_b_forget=m_b_forget, m_fox_out_g=m_fox_out_g, m_sb_out_g=m_sb_out_g, m_w_out=m_w_out, m_xattn_norm_g=m_xattn_norm_g, m_mem_norm_g=m_mem_norm_g, m_w_mq=m_w_mq, m_w_mkv=m_w_mkv, m_w_mo=m_w_mo, m_ffn_norm_g=m_ffn_norm_g, m_w_up=m_w_up, m_conv_w=m_conv_w, m_conv_b=m_conv_b, m_w_down=m_w_down, m_final_norm_g=m_final_norm_g, v_attn_norm_g=v_attn_norm_g, v_w_in=v_w_in, v_b_forget=v_b_forget, v_fox_out_g=v_fox_out_g, v_sb_out_g=v_sb_out_g, v_w_out=v_w_out, v_xattn_norm_g=v_xattn_norm_g, v_mem_norm_g=v_mem_norm_g, v_w_mq=v_w_mq, v_w_mkv=v_w_mkv, v_w_mo=v_w_mo, v_ffn_norm_g=v_ffn_norm_g, v_w_up=v_w_up, v_conv_w=v_conv_w, v_conv_b=v_conv_b, v_w_down=v_w_down, v_final_norm_g=v_final_norm_g)
    weights = {n: given[n] for n in TWIN_WEIGHTS}
    shared = {n: given[n] for n in SHARED_INPUTS}
    per_example = {n: given[n] for n in ['x', 'mem']}
    grad_fn = _jax.value_and_grad(_loss, argnums=(0, 1))

    def one_microbatch(ex, loss_target):
        ex = dict(ex)
        diff = ex.pop(TWIN_DIFF_INPUT)
        return grad_fn(weights, diff, {**shared, **ex}, loss_target)

    if N_MICROBATCH == 1:
        loss, (grad_w, grad_x) = one_microbatch(per_example, given["loss_target"])
    else:
        def body(carry, xs):
            loss_sum, grad_sum = carry
            l_k, (gw_k, gx_k) = one_microbatch(xs[0], xs[1])
            with _jax.named_scope("update"):
                return (loss_sum + l_k, _jax.tree.map(_jnp.add, grad_sum, gw_k)), gx_k

        init = (_jnp.zeros((), _jnp.float32), _jax.tree.map(_jnp.zeros_like, weights))
        (loss, grad_w), grad_x = _jax.lax.scan(body, init, (per_example, given["loss_target"]))
    with _jax.named_scope("update"):
        delta_w, new_m, new_v = {}, {}, {}
        for n in TWIN_WEIGHTS:
            delta_w[n], new_m[n], new_v[n] = _adamw(weights[n], grad_w[n], given["m_" + n], given["v_" + n])
    return (loss, grad_x, *[grad_w[n] for n in TWIN_WEIGHTS], *[delta_w[n] for n in TWIN_WEIGHTS],
            *[new_m[n] for n in TWIN_WEIGHTS], *[new_v[n] for n in TWIN_WEIGHTS])
```

```python
import functools

import jax
import jax.numpy as jnp
from jax import lax
from jax.experimental import pallas as pl
from jax.experimental.pallas import tpu as pltpu

F32 = jnp.float32
BF16 = jnp.bfloat16
EPS = 1e-6
NEG = -1e30

HEAD_DIM = 64
N_FOX = 8
FOX_W = 512
QKV_W = 3072
IN_PAD = 3200
N_MEM_HEADS = 4
MEM_HD = 256
D_FF = 2816
FF_CHUNK = 256
N_DEV = 8

ADAM_LR = 0.001
ADAM_B1 = 0.9
ADAM_B2 = 0.999
ADAM_EPS = 1e-08
ADAM_WD = 0.01
ADAM_STEP = 10

VMEM_LIMIT = 56 * 1024 * 1024
MESH = pl.DeviceIdType.MESH


def _cparams(sem=None):
    return pltpu.CompilerParams(dimension_semantics=sem, vmem_limit_bytes=VMEM_LIMIT)


def _nt(a, b):
    return lax.dot_general(a, b, (((1,), (1,)), ((), ())), preferred_element_type=F32)


def _tn(a, b):
    return lax.dot_general(a, b, (((0,), (0,)), ((), ())), preferred_element_type=F32)


def _nn(a, b):
    return jnp.dot(a, b, preferred_element_type=F32)


def _split_dot(a, m01, terms):
    out = None
    r = a
    for t in range(terms):
        p = r.astype(BF16)
        d = _nn(p, m01)
        out = d if out is None else out + d
        if t + 1 < terms:
            r = r - p.astype(F32)
    return out


def _rstd(xv):
    return lax.rsqrt(jnp.mean(xv * xv, axis=-1, keepdims=True) + EPS)


def _norm_bwd(xv, g, dh):
    r = _rstd(xv)
    xhat = xv * r
    dxhat = dh * g
    dx = r * (dxhat - xhat * jnp.mean(dxhat * xhat, axis=-1, keepdims=True))
    dg = jnp.sum(dh * xhat, axis=0, keepdims=True)
    return dx, dg


def _tile_div(n, cap):
    best = None
    for d in range(128, min(n, cap) + 1, 128):
        if n % d == 0:
            best = d
    assert best is not None, n
    return best


def _inproj_fwd(x, g1, w_qkv, w_f_t, b_f, tm):
    T, D = x.shape
    N = w_qkv.shape[1]
    H = w_f_t.shape[0]

    def body(x_ref, g_ref, w_ref, wf_ref, b_ref, proj_ref, h_ref, xf_ref, c_ref, carry_ref):
        i = pl.program_id(0)

        @pl.when(i == 0)
        def _():
            carry_ref[...] = jnp.zeros_like(carry_ref)

        xv = x_ref[...]
        h = (xv * _rstd(xv) * g_ref[...]).astype(BF16)
        h_ref[...] = h
        for n0 in range(0, N, 512):
            proj_ref[:, n0:n0 + 512] = _nn(h, w_ref[:, n0:n0 + 512]).astype(BF16)
        xf = _nt(wf_ref[...], h) + b_ref[...]
        xf_ref[...] = xf
        logf = jnp.minimum(xf, 0.0) - jnp.log1p(jnp.exp(-jnp.abs(xf)))
        row = lax.broadcasted_iota(jnp.int32, (tm, tm), 0)
        col = lax.broadcasted_iota(jnp.int32, (tm, tm), 1)
        upper = jnp.where(row <= col, 1.0, 0.0).astype(BF16)
        c = _split_dot(logf, upper, 3) + carry_ref[...]
        c_ref[...] = c
        carry_ref[...] = c[:, tm - 1:tm]

    return pl.pallas_call(
        body,
        name="inproj_fwd",
        grid=(T // tm,),
        in_specs=[
            pl.BlockSpec((tm, D), lambda i: (i, 0)),
            pl.BlockSpec((1, D), lambda i: (0, 0)),
            pl.BlockSpec((D, N), lambda i: (0, 0)),
            pl.BlockSpec((H, D), lambda i: (0, 0)),
            pl.BlockSpec((H, 1), lambda i: (0, 0)),
        ],
        out_specs=[
            pl.BlockSpec((tm, N), lambda i: (i, 0)),
            pl.BlockSpec((tm, D), lambda i: (i, 0)),
            pl.BlockSpec((H, tm), lambda i: (0, i)),
            pl.BlockSpec((H, tm), lambda i: (0, i)),
        ],
        out_shape=[
            jax.ShapeDtypeStruct((T, N), BF16),
            jax.ShapeDtypeStruct((T, D), BF16),
            jax.ShapeDtypeStruct((H, T), F32),
            jax.ShapeDtypeStruct((H, T), F32),
        ],
        scratch_shapes=[pltpu.VMEM((H, 1), F32)],
        compiler_params=_cparams(("arbitrary",)),
    )(x, g1, w_qkv, w_f_t, b_f)


def _head_q(q, hh, lane):
    hmask = (lane >= HEAD_DIM * hh) & (lane < HEAD_DIM * (hh + 1))
    qh = jnp.where(hmask, q.astype(F32), 0.0) * (HEAD_DIM ** -0.5)
    return qh.astype(BF16), hmask


def _fox_fwd(proj, c_col, c_row, tq):
    T = proj.shape[0]

    def body(q_ref, k_ref, v_ref, cq_ref, ck_ref, o_ref, lse_ref):
        i = pl.program_id(1)
        lane = lax.broadcasted_iota(jnp.int32, (1, 128), 1)
        row = lax.broadcasted_iota(jnp.int32, (tq, tq), 0)
        col = lax.broadcasted_iota(jnp.int32, (tq, tq), 1)
        q = q_ref[...]
        outs = []
        for hh in range(2):
            qh, _ = _head_q(q, hh, lane)
            cq = cq_ref[hh]

            def step(j, carry, diag, qh=qh, cq=cq, hh=hh):
                m, l, acc = carry
                ks = pl.multiple_of(j * tq, tq)
                k = k_ref[pl.ds(ks, tq), :]
                v = v_ref[pl.ds(ks, tq), :]
                s = _nt(qh, k) + cq - ck_ref[hh, :, pl.ds(ks, tq)]
                if diag:
                    s = jnp.where(col <= row, s, NEG)
                m_new = jnp.maximum(m, jnp.max(s, axis=1, keepdims=True))
                alpha = jnp.exp(m - m_new)
                p = jnp.exp(s - m_new)
                l = alpha * l + jnp.sum(p, axis=1, keepdims=True)
                acc = alpha * acc + _nn(p.astype(BF16), v)
                return m_new, l, acc

            init = (jnp.full((tq, 1), NEG, F32), jnp.zeros((tq, 1), F32), jnp.zeros((tq, 128), F32))
            carry = lax.fori_loop(0, i, lambda j, c, step=step: step(j, c, False), init)
            m, l, acc = step(i, carry, True)
            outs.append((acc / l, m + jnp.log(l)))
        o_ref[...] = jnp.where(lane < HEAD_DIM, outs[0][0], outs[1][0])
        lse_ref[0] = outs[0][1]
        lse_ref[1] = outs[1][1]

    return pl.pallas_call(
        body,
        name="fox_fwd",
        grid=(4, T // tq),
        in_specs=[
            pl.BlockSpec((tq, 128), lambda p, i: (i, p)),
            pl.BlockSpec((T, 128), lambda p, i: (0, 4 + p)),
            pl.BlockSpec((T, 128), lambda p, i: (0, 8 + p)),
            pl.BlockSpec((2, tq, 1), lambda p, i: (p, i, 0)),
            pl.BlockSpec((2, 1, T), lambda p, i: (p, 0, 0)),
        ],
        out_specs=[
            pl.BlockSpec((tq, 128), lambda p, i: (i, p)),
            pl.BlockSpec((2, tq, 1), lambda p, i: (p, i, 0)),
        ],
        out_shape=[
            jax.ShapeDtypeStruct((T, FOX_W), F32),
            jax.ShapeDtypeStruct((N_FOX, T, 1), F32),
        ],
        compiler_params=_cparams(("arbitrary", "arbitrary")),
    )(proj, proj, proj, c_col, c_row)


def _sb_terms(z, strict):
    e = jnp.exp(-jnp.abs(z))
    L = -(jnp.maximum(z, 0.0) + jnp.log1p(e))
    if strict is not None:
        L = jnp.where(strict, L, 0.0)
    return L, e


def _sb_fwd(proj, tq):
    T = proj.shape[0]

    def body(q_ref, k_ref, v_ref, o_ref, ltot_ref):
        i = pl.program_id(1)
        lane = lax.broadcasted_iota(jnp.int32, (1, 128), 1)
        row = lax.broadcasted_iota(jnp.int32, (tq, tq), 0)
        col = lax.broadcasted_iota(jnp.int32, (tq, tq), 1)
        strict = col < row
        later = jnp.where(row > col, 1.0, 0.0).astype(BF16)
        q = q_ref[...]
        outs = []
        for hh in range(2):
            qh, _ = _head_q(q, hh, lane)

            def step(j, carry, diag, qh=qh):
                run, acc = carry
                ks = pl.multiple_of(j * tq, tq)
                k = k_ref[pl.ds(ks, tq), :]
                v = v_ref[pl.ds(ks, tq), :]
                z = _nt(qh, k)
                L, _ = _sb_terms(z, strict if diag else None)
                arg = z + L + _split_dot(L, later, 2) + run
                if diag:
                    arg = jnp.where(strict, arg, NEG)
                a = jnp.exp(arg)
                acc = acc + _nn(a.astype(BF16), v)
                run = run + jnp.sum(L, axis=1, keepdims=True)
                return run, acc

            carry = step(i, (jnp.zeros((tq, 1), F32), jnp.zeros((tq, 128), F32)), True)
            run, acc = lax.fori_loop(0, i, lambda jj, c, step=step: step(i - 1 - jj, c, False), carry)
            outs.append(acc)
            ltot_ref[hh] = run
        o_ref[...] = jnp.where(lane < HEAD_DIM, outs[0], outs[1])

    return pl.pallas_call(
        body,
        name="sb_fwd",
        grid=(4, T // tq),
        in_specs=[
            pl.BlockSpec((tq, 128), lambda p, i: (i, 12 + p)),
            pl.BlockSpec((T, 128), lambda p, i: (0, 16 + p)),
            pl.BlockSpec((T, 128), lambda p, i: (0, 20 + p)),
        ],
        out_specs=[
            pl.BlockSpec((tq, 128), lambda p, i: (i, p)),
            pl.BlockSpec((2, tq, 1), lambda p, i: (p, i, 0)),
        ],
        out_shape=[
            jax.ShapeDtypeStruct((T, FOX_W), F32),
            jax.ShapeDtypeStruct((N_FOX, T, 1), F32),
        ],
        compiler_params=_cparams(("arbitrary", "arbitrary")),
    )(proj, proj, proj)


def _post_attn_fwd(fox_o, sb_o, gf, gs, w_out, x, tm):
    T, D = x.shape

    def body(f_ref, s_ref, gf_ref, gs_ref, w_ref, x_ref, x1_ref, mix_ref):
        f = f_ref[...]
        s = s_ref[...]
        mix_ref[:, :FOX_W] = (f * _rstd(f) * gf_ref[...]).astype(BF16)
        mix_ref[:, FOX_W:] = (s * _rstd(s) * gs_ref[...]).astype(BF16)
        x1_ref[...] = x_ref[...] + _nn(mix_ref[...], w_ref[...])

    return pl.pallas_call(
        body,
        name="post_attn_fwd",
        grid=(T // tm,),
        in_specs=[
            pl.BlockSpec((tm, FOX_W), lambda i: (i, 0)),
            pl.BlockSpec((tm, FOX_W), lambda i: (i, 0)),
            pl.BlockSpec((1, FOX_W), lambda i: (0, 0)),
            pl.BlockSpec((1, FOX_W), lambda i: (0, 0)),
            pl.BlockSpec((D, D), lambda i: (0, 0)),
            pl.BlockSpec((tm, D), lambda i: (i, 0)),
        ],
        out_specs=[
            pl.BlockSpec((tm, D), lambda i: (i, 0)),
            pl.BlockSpec((tm, D), lambda i: (i, 0)),
        ],
        out_shape=[jax.ShapeDtypeStruct((T, D), F32), jax.ShapeDtypeStruct((T, D), BF16)],
        compiler_params=_cparams(("arbitrary",)),
    )(fox_o, sb_o, gf, gs, w_out, x)


def _mem_kv_fwd(mem, gm, w_mkv):
    M, D = mem.shape
    N = w_mkv.shape[1]

    def body(mem_ref, g_ref, w_ref, m_ref, kv_ref):
        mv = mem_ref[...]
        m = (mv * _rstd(mv) * g_ref[...]).astype(BF16)
        m_ref[...] = m
        for n0 in range(0, N, 512):
            kv_ref[:, n0:n0 + 512] = _nn(m, w_ref[:, n0:n0 + 512]).astype(BF16)

    return pl.pallas_call(
        body,
        name="mem_kv_fwd",
        out_shape=[jax.ShapeDtypeStruct((M, D), BF16), jax.ShapeDtypeStruct((M, N), BF16)],
        compiler_params=_cparams(),
    )(mem, gm, w_mkv)


def _xattn_probs(qb, kv, h):
    k = kv[:, h * MEM_HD:(h + 1) * MEM_HD]
    s = _nt(qb[:, h * MEM_HD:(h + 1) * MEM_HD], k) * (MEM_HD ** -0.5)
    s = s - jnp.max(s, axis=1, keepdims=True)
    p = jnp.exp(s)
    return p / jnp.sum(p, axis=1, keepdims=True)


def _xattn_fwd(x1, g2, w_mq, kv, w_mo, tm):
    T, D = x1.shape
    M = kv.shape[0]

    def body(x_ref, g_ref, wq_ref, kv_ref, wo_ref, x2_ref, h_ref, q_ref, om_ref):
        xv = x_ref[...]
        h = (xv * _rstd(xv) * g_ref[...]).astype(BF16)
        h_ref[...] = h
        q_ref[...] = _nn(h, wq_ref[...]).astype(BF16)
        qb = q_ref[...]
        kvv = kv_ref[...]
        for hd in range(N_MEM_HEADS):
            p = _xattn_probs(qb, kvv, hd)
            v = kvv[:, D + hd * MEM_HD:D + (hd + 1) * MEM_HD]
            om_ref[:, hd * MEM_HD:(hd + 1) * MEM_HD] = _nn(p.astype(BF16), v).astype(BF16)
        x2_ref[...] = xv + _nn(om_ref[...], wo_ref[...])

    return pl.pallas_call(
        body,
        name="xattn_fwd",
        grid=(T // tm,),
        in_specs=[
            pl.BlockSpec((tm, D), lambda i: (i, 0)),
            pl.BlockSpec((1, D), lambda i: (0, 0)),
            pl.BlockSpec((D, D), lambda i: (0, 0)),
            pl.BlockSpec((M, 2 * D), lambda i: (0, 0)),
            pl.BlockSpec((D, D), lambda i: (0, 0)),
        ],
        out_specs=[pl.BlockSpec((tm, D), lambda i: (i, 0))] * 4,
        out_shape=[jax.ShapeDtypeStruct((T, D), F32)] + [jax.ShapeDtypeStruct((T, D), BF16)] * 3,
        compiler_params=_cparams(("arbitrary",)),
    )(x1, g2, w_mq, kv, w_mo)


def _conv_taps(ext_ref, tm, back):
    if back:
        return ext_ref[pl.ds(6, tm), :], ext_ref[pl.ds(7, tm), :], ext_ref[pl.ds(8, tm), :]
    return ext_ref[pl.ds(0, tm), :], ext_ref[pl.ds(1, tm), :], ext_ref[pl.ds(2, tm), :]


def _ffn_fwd(x2, g3, w_up, conv_w, conv_b, w_down, tm):
    T, D = x2.shape
    fc = FF_CHUNK
    nj = D_FF // fc

    def body(x_ref, g_ref, wg_ref, wv_ref, cwg_ref, cwv_ref, cbg_ref, cbv_ref, wd_ref,
             x3_ref, h_ref, ug_ref, uv_ref, a_ref, acc_ref, carry_ref, ext_ref):
        i = pl.program_id(0)
        j = pl.program_id(1)

        @pl.when(j == 0)
        def _():
            xv = x_ref[...]
            h_ref[...] = (xv * _rstd(xv) * g_ref[...]).astype(BF16)
            acc_ref[...] = xv

        @pl.when(i == 0)
        def _():
            carry_ref[j] = jnp.zeros((2, 8, fc), F32)

        h = h_ref[...]
        halves = []
        for part, (w_ref, cw_ref, cb_ref, u_ref) in enumerate(
                ((wg_ref, cwg_ref, cbg_ref, ug_ref), (wv_ref, cwv_ref, cbv_ref, uv_ref))):
            u = _nn(h, w_ref[...])
            u_ref[...] = u
            ext = ext_ref.at[part]
            ext[pl.ds(0, 8), :] = carry_ref[j, part]
            ext[pl.ds(8, tm), :] = u
            carry_ref[j, part] = u[tm - 8:, :]
            u2, u1, u0 = _conv_taps(ext, tm, True)
            cw = cw_ref[...]
            halves.append(cb_ref[...] + cw[0:1] * u2 + cw[1:2] * u1 + cw[2:3] * u0)
        gate, val = halves
        a = (gate * jax.nn.sigmoid(gate) * val).astype(BF16)
        a_ref[...] = a
        acc_ref[...] += _nn(a, wd_ref[...])

        @pl.when(j == nj - 1)
        def _():
            x3_ref[...] = acc_ref[...]

    return pl.pallas_call(
        body,
        name="ffn_fwd",
        grid=(T // tm, nj),
        in_specs=[
            pl.BlockSpec((tm, D), lambda i, j: (i, 0)),
            pl.BlockSpec((1, D), lambda i, j: (0, 0)),
            pl.BlockSpec((D, fc), lambda i, j: (0, j)),
            pl.BlockSpec((D, fc), lambda i, j: (0, nj + j)),
            pl.BlockSpec((3, fc), lambda i, j: (0, j)),
            pl.BlockSpec((3, fc), lambda i, j: (0, nj + j)),
            pl.BlockSpec((1, fc), lambda i, j: (0, j)),
            pl.BlockSpec((1, fc), lambda i, j: (0, nj + j)),
            pl.BlockSpec((fc, D), lambda i, j: (j, 0)),
        ],
        out_specs=[
            pl.BlockSpec((tm, D), lambda i, j: (i, 0)),
            pl.BlockSpec((tm, D), lambda i, j: (i, 0)),
            pl.BlockSpec((tm, fc), lambda i, j: (i, j)),
            pl.BlockSpec((tm, fc), lambda i, j: (i, j)),
            pl.BlockSpec((tm, fc), lambda i, j: (i, j)),
        ],
        out_shape=[
            jax.ShapeDtypeStruct((T, D), F32),
            jax.ShapeDtypeStruct((T, D), BF16),
            jax.ShapeDtypeStruct((T, D_FF), F32),
            jax.ShapeDtypeStruct((T, D_FF), F32),
            jax.ShapeDtypeStruct((T, D_FF), BF16),
        ],
        scratch_shapes=[
            pltpu.VMEM((tm, D), F32),
            pltpu.VMEM((nj, 2, 8, fc), F32),
            pltpu.VMEM((2, tm + 8, fc), F32),
        ],
        compiler_params=_cparams(("arbitrary", "arbitrary")),
    )(x2, g3, w_up, w_up, conv_w, conv_w, conv_b, conv_b, w_down)


def _loss_head(x3, gfin, target, tm):
    T, D = x3.shape

    def body(x_ref, g_ref, t_ref, dx_ref, loss_ref, dg_ref):
        i = pl.program_id(0)

        @pl.when(i == 0)
        def _():
            loss_ref[...] = jnp.zeros_like(loss_ref)
            dg_ref[...] = jnp.zeros_like(dg_ref)

        xv = x_ref[...]
        g = g_ref[...]
        r = _rstd(xv)
        xhat = xv * r
        err = xhat * g - t_ref[...]
        part = jnp.sum(jnp.sum(err * err, axis=1, keepdims=True), axis=0, keepdims=True) * (0.5 / D)
        loss_ref[...] += jnp.broadcast_to(part, loss_ref.shape)
        dy = err * (1.0 / D)
        dg_ref[...] += jnp.sum(dy * xhat, axis=0, keepdims=True)
        dxhat = dy * g
        dx_ref[...] = r * (dxhat - xhat * jnp.mean(dxhat * xhat, axis=-1, keepdims=True))

    return pl.pallas_call(
        body,
        name="loss_head",
        grid=(T // tm,),
        in_specs=[
            pl.BlockSpec((tm, D), lambda i: (i, 0)),
            pl.BlockSpec((1, D), lambda i: (0, 0)),
            pl.BlockSpec((tm, D), lambda i: (i, 0)),
        ],
        out_specs=[
            pl.BlockSpec((tm, D), lambda i: (i, 0)),
            pl.BlockSpec((8, 128), lambda i: (0, 0)),
            pl.BlockSpec((1, D), lambda i: (0, 0)),
        ],
        out_shape=[
            jax.ShapeDtypeStruct((T, D), F32),
            jax.ShapeDtypeStruct((8, 128), F32),
            jax.ShapeDtypeStruct((1, D), F32),
        ],
        compiler_params=_cparams(("arbitrary",)),
    )(x3, gfin, target)


def _ffn_bwd(dx3, x2, g3, ug, uv, conv_w, conv_b, w_down_t, w_up_t, tm):
    T, D = x2.shape
    fc = FF_CHUNK
    nj = D_FF // fc
    nt = T // tm
    hb = tm // 8

    def rev(i):
        return nt - 1 - i

    def body(dx3_ref, x_ref, g_ref, ug_ref, uv_ref, ugh_ref, uvh_ref, cwg_ref, cwv_ref, cbg_ref, cbv_ref,
             wdt_ref, wutg_ref, wutv_ref,
             dx2_ref, dug_ref, duv_ref, dg_ref, dcg_ref, dcv_ref,
             acc_ref, carry_ref, ext_ref):
        i = pl.program_id(0)
        j = pl.program_id(1)
        first_tile = i == nt - 1
        cols = pl.ds(pl.multiple_of(j * fc, fc), fc)

        @pl.when(j == 0)
        def _():
            acc_ref[...] = jnp.zeros_like(acc_ref)

        @pl.when((i == 0) & (j == 0))
        def _():
            dg_ref[...] = jnp.zeros_like(dg_ref)
            dcg_ref[...] = jnp.zeros_like(dcg_ref)
            dcv_ref[...] = jnp.zeros_like(dcv_ref)

        @pl.when(i == 0)
        def _():
            carry_ref[j] = jnp.zeros((2, 8, fc), F32)

        da = _nn(dx3_ref[...].astype(BF16), wdt_ref[...])
        pre = []
        for part, (u_ref, uh_ref, cw_ref, cb_ref) in enumerate(
                ((ug_ref, ugh_ref, cwg_ref, cbg_ref), (uv_ref, uvh_ref, cwv_ref, cbv_ref))):
            ext = ext_ref.at[part]
            ext[pl.ds(0, 8), :] = jnp.where(first_tile, 0.0, uh_ref[...])
            ext[pl.ds(8, tm), :] = u_ref[...]
            u2, u1, u0 = _conv_taps(ext, tm, True)
            cw = cw_ref[...]
            pre.append(cb_ref[...] + cw[0:1] * u2 + cw[1:2] * u1 + cw[2:3] * u0)
        gate, val = pre
        sig = jax.nn.sigmoid(gate)
        silu = gate * sig
        dys = (da * val * (sig * (1.0 + gate * (1.0 - sig))), da * silu)
        for part, (dy, cw_ref, du_ref, wut_ref, dc_ref) in enumerate(
                ((dys[0], cwg_ref, dug_ref, wutg_ref, dcg_ref), (dys[1], cwv_ref, duv_ref, wutv_ref, dcv_ref))):
            u2, u1, u0 = _conv_taps(ext_ref.at[part], tm, True)
            upd = jnp.concatenate([
                jnp.sum(u2 * dy, axis=0, keepdims=True),
                jnp.sum(u1 * dy, axis=0, keepdims=True),
                jnp.sum(u0 * dy, axis=0, keepdims=True),
                jnp.sum(dy, axis=0, keepdims=True),
                jnp.zeros((4, fc), F32)], axis=0)
            dc_ref[:, cols] += upd
            ext = ext_ref.at[2 + part]
            ext[pl.ds(0, tm), :] = dy
            ext[pl.ds(tm, 8), :] = carry_ref[j, part]
            carry_ref[j, part] = dy[:8, :]
            d0, d1, d2 = _conv_taps(ext, tm, False)
            cw = cw_ref[...]
            du = (cw[2:3] * d0 + cw[1:2] * d1 + cw[0:1] * d2).astype(BF16)
            du_ref[...] = du
            acc_ref[...] += _nn(du, wut_ref[...])

        @pl.when(j == nj - 1)
        def _():
            dx, dg = _norm_bwd(x_ref[...], g_ref[...], acc_ref[...])
            dx2_ref[...] = dx3_ref[...] + dx
            dg_ref[...] += dg

    return pl.pallas_call(
        body,
        name="ffn_bwd",
        grid=(nt, nj),
        in_specs=[
            pl.BlockSpec((tm, D), lambda i, j: (rev(i), 0)),
            pl.BlockSpec((tm, D), lambda i, j: (rev(i), 0)),
            pl.BlockSpec((1, D), lambda i, j: (0, 0)),
            pl.BlockSpec((tm, fc), lambda i, j: (rev(i), j)),
            pl.BlockSpec((tm, fc), lambda i, j: (rev(i), j)),
            pl.BlockSpec((8, fc), lambda i, j: (jnp.maximum(rev(i) * hb - 1, 0), j)),
            pl.BlockSpec((8, fc), lambda i, j: (jnp.maximum(rev(i) * hb - 1, 0), j)),
            pl.BlockSpec((3, fc), lambda i, j: (0, j)),
            pl.BlockSpec((3, fc), lambda i, j: (0, nj + j)),
            pl.BlockSpec((1, fc), lambda i, j: (0, j)),
            pl.BlockSpec((1, fc), lambda i, j: (0, nj + j)),
            pl.BlockSpec((D, fc), lambda i, j: (0, j)),
            pl.BlockSpec((fc, D), lambda i, j: (j, 0)),
            pl.BlockSpec((fc, D), lambda i, j: (nj + j, 0)),
        ],
        out_specs=[
            pl.BlockSpec((tm, D), lambda i, j: (rev(i), 0)),
            pl.BlockSpec((tm, fc), lambda i, j: (rev(i), j)),
            pl.BlockSpec((tm, fc), lambda i, j: (rev(i), j)),
            pl.BlockSpec((1, D), lambda i, j: (0, 0)),
            pl.BlockSpec((8, D_FF), lambda i, j: (0, 0)),
            pl.BlockSpec((8, D_FF), lambda i, j: (0, 0)),
        ],
        out_shape=[
            jax.ShapeDtypeStruct((T, D), F32),
            jax.ShapeDtypeStruct((T, D_FF), BF16),
            jax.ShapeDtypeStruct((T, D_FF), BF16),
            jax.ShapeDtypeStruct((1, D), F32),
            jax.ShapeDtypeStruct((8, D_FF), F32),
            jax.ShapeDtypeStruct((8, D_FF), F32),
        ],
        scratch_shapes=[
            pltpu.VMEM((tm, D), F32),
            pltpu.VMEM((nj, 2, 8, fc), F32),
            pltpu.VMEM((4, tm + 8, fc), F32),
        ],
        compiler_params=_cparams(("arbitrary", "arbitrary")),
    )(dx3, x2, g3, ug, uv, ug, uv, conv_w, conv_w, conv_b, conv_b, w_down_t, w_up_t, w_up_t)


def _xattn_bwd(dx2, x1, g2, qb, kv, w_mo_t, w_mq_t, tm):
    T, D = x1.shape
    M = kv.shape[0]

    def body(dx2_ref, x_ref, g_ref, q_ref, kv_ref, wot_ref, wqt_ref, dx1_ref, dq_ref, dkv_ref, dg_ref):
        i = pl.program_id(0)

        @pl.when(i == 0)
        def _():
            dkv_ref[...] = jnp.zeros_like(dkv_ref)
            dg_ref[...] = jnp.zeros_like(dg_ref)

        dxv = dx2_ref[...]
        dom = _nn(dxv.astype(BF16), wot_ref[...]).astype(BF16)
        qb_ = q_ref[...]
        kvv = kv_ref[...]
        for hd in range(N_MEM_HEADS):
            sl = slice(hd * MEM_HD, (hd + 1) * MEM_HD)
            vsl = slice(D + hd * MEM_HD, D + (hd + 1) * MEM_HD)
            p = _xattn_probs(qb_, kvv, hd)
            dp = _nt(dom[:, sl], kvv[:, vsl])
            ds = (p * (dp - jnp.sum(p * dp, axis=1, keepdims=True)) * (MEM_HD ** -0.5)).astype(BF16)
            dq_ref[:, sl] = _nn(ds, kvv[:, sl]).astype(BF16)
            dkv_ref[:, sl] += _tn(ds, qb_[:, sl])
            dkv_ref[:, vsl] += _tn(p.astype(BF16), dom[:, sl])
        dh = _nn(dq_ref[...], wqt_ref[...])
        dx, dg = _norm_bwd(x_ref[...], g_ref[...], dh)
        dx1_ref[...] = dxv + dx
        dg_ref[...] += dg

    return pl.pallas_call(
        body,
        name="xattn_bwd",
        grid=(T // tm,),
        in_specs=[
            pl.BlockSpec((tm, D), lambda i: (i, 0)),
            pl.BlockSpec((tm, D), lambda i: (i, 0)),
            pl.BlockSpec((1, D), lambda i: (0, 0)),
            pl.BlockSpec((tm, D), lambda i: (i, 0)),
            pl.BlockSpec((M, 2 * D), lambda i: (0, 0)),
            pl.BlockSpec((D, D), lambda i: (0, 0)),
            pl.BlockSpec((D, D), lambda i: (0, 0)),
        ],
        out_specs=[
            pl.BlockSpec((tm, D), lambda i: (i, 0)),
            pl.BlockSpec((tm, D), lambda i: (i, 0)),
            pl.BlockSpec((M, 2 * D), lambda i: (0, 0)),
            pl.BlockSpec((1, D), lambda i: (0, 0)),
        ],
        out_shape=[
            jax.ShapeDtypeStruct((T, D), F32),
            jax.ShapeDtypeStruct((T, D), BF16),
            jax.ShapeDtypeStruct((M, 2 * D), F32),
            jax.ShapeDtypeStruct((1, D), F32),
        ],
        compiler_params=_cparams(("arbitrary",)),
    )(dx2, x1, g2, qb, kv, w_mo_t, w_mq_t)


def _mem_kv_bwd(mem, gm, mb, dkv, w_mkv_t):
    M, D = mem.shape
    N = dkv.shape[1]

    def body(mem_ref, g_ref, m_ref, dkv_ref, wt_ref, dw_ref, dg_ref):
        dkvb = dkv_ref[...].astype(BF16)
        for n0 in range(0, N, 512):
            dw_ref[:, n0:n0 + 512] = _tn(m_ref[...], dkvb[:, n0:n0 + 512])
        dm = _nn(dkvb, wt_ref[...])
        mv = mem_ref[...]
        dg_ref[...] = jnp.sum(dm * (mv * _rstd(mv)), axis=0, keepdims=True)

    return pl.pallas_call(
        body,
        name="mem_kv_bwd",
        out_shape=[jax.ShapeDtypeStruct((D, N), F32), jax.ShapeDtypeStruct((1, D), F32)],
        compiler_params=_cparams(),
    )(mem, gm, mb, dkv, w_mkv_t)


def _post_attn_bwd(dx1, fox_o, sb_o, gf, gs, w_out_t, tm):
    T, D = dx1.shape

    def body(dx_ref, f_ref, s_ref, gf_ref, gs_ref, wt_ref, df_ref, ds_ref, dgf_ref, dgs_ref):
        i = pl.program_id(0)

        @pl.when(i == 0)
        def _():
            dgf_ref[...] = jnp.zeros_like(dgf_ref)
            dgs_ref[...] = jnp.zeros_like(dgs_ref)

        dmix = _nn(dx_ref[...].astype(BF16), wt_ref[...])
        d, dg = _norm_bwd(f_ref[...], gf_ref[...], dmix[:, :FOX_W])
        df_ref[...] = d
        dgf_ref[...] += dg
        d, dg = _norm_bwd(s_ref[...], gs_ref[...], dmix[:, FOX_W:])
        ds_ref[...] = d
        dgs_ref[...] += dg

    return pl.pallas_call(
        body,
        name="post_attn_bwd",
        grid=(T // tm,),
        in_specs=[
            pl.BlockSpec((tm, D), lambda i: (i, 0)),
            pl.BlockSpec((tm, FOX_W), lambda i: (i, 0)),
            pl.BlockSpec((tm, FOX_W), lambda i: (i, 0)),
            pl.BlockSpec((1, FOX_W), lambda i: (0, 0)),
            pl.BlockSpec((1, FOX_W), lambda i: (0, 0)),
            pl.BlockSpec((D, D), lambda i: (0, 0)),
        ],
        out_specs=[
            pl.BlockSpec((tm, FOX_W), lambda i: (i, 0)),
            pl.BlockSpec((tm, FOX_W), lambda i: (i, 0)),
            pl.BlockSpec((1, FOX_W), lambda i: (0, 0)),
            pl.BlockSpec((1, FOX_W), lambda i: (0, 0)),
        ],
        out_shape=[
            jax.ShapeDtypeStruct((T, FOX_W), F32),
            jax.ShapeDtypeStruct((T, FOX_W), F32),
            jax.ShapeDtypeStruct((1, FOX_W), F32),
            jax.ShapeDtypeStruct((1, FOX_W), F32),
        ],
        compiler_params=_cparams(("arbitrary",)),
    )(dx1, fox_o, sb_o, gf, gs, w_out_t)


def _sb_bwd(proj, ltot, d_o, tq):
    T = proj.shape[0]

    def body(q_ref, k_ref, v_ref, lt_ref, do_ref, dq_ref, dk_ref, dv_ref):
        i = pl.program_id(1)

        @pl.when(i == 0)
        def _():
            dk_ref[...] = jnp.zeros_like(dk_ref)
            dv_ref[...] = jnp.zeros_like(dv_ref)

        lane = lax.broadcasted_iota(jnp.int32, (1, 128), 1)
        row = lax.broadcasted_iota(jnp.int32, (tq, tq), 0)
        col = lax.broadcasted_iota(jnp.int32, (tq, tq), 1)
        strict = col < row
        upto = jnp.where(row <= col, 1.0, 0.0).astype(BF16)
        before = jnp.where(row < col, 1.0, 0.0).astype(BF16)
        q = q_ref[...]
        dov = do_ref[...]
        dqs = []
        for hh in range(2):
            qh, hmask = _head_q(q, hh, lane)
            dohb = jnp.where(hmask, dov, 0.0).astype(BF16)
            ltot_h = lt_ref[hh]

            def step(j, carry, diag, qh=qh, dohb=dohb, ltot_h=ltot_h):
                run, run_w, dq = carry
                ks = pl.multiple_of(j * tq, tq)
                k = k_ref[pl.ds(ks, tq), :]
                v = v_ref[pl.ds(ks, tq), :]
                z = _nt(qh, k)
                L, e = _sb_terms(z, strict if diag else None)
                arg = z + L + ((ltot_h - run) - _split_dot(L, upto, 2))
                if diag:
                    arg = jnp.where(strict, arg, NEG)
                a = jnp.exp(arg)
                w = a * _nt(dohb, v)
                d_keep = _split_dot(w, before, 2) + run_w
                r = 1.0 / (1.0 + e)
                beta = jnp.where(z >= 0.0, r, e * r)
                dz = w * (1.0 - beta) - d_keep * beta
                if diag:
                    dz = jnp.where(strict, dz, 0.0)
                dzb = dz.astype(BF16)
                dq = dq + _nn(dzb, k)
                dk_ref[pl.ds(ks, tq), :] += _tn(dzb, qh)
                dv_ref[pl.ds(ks, tq), :] += _tn(a.astype(BF16), dohb)
                run = run + jnp.sum(L, axis=1, keepdims=True)
                run_w = run_w + jnp.sum(w, axis=1, keepdims=True)
                return run, run_w, dq

            zero = jnp.zeros((tq, 1), F32)
            carry = lax.fori_loop(0, i, lambda j, c, step=step: step(j, c, False),
                                  (zero, zero, jnp.zeros((tq, 128), F32)))
            _, _, dq = step(i, carry, True)
            dqs.append(dq)
        dq_ref[...] = (jnp.where(lane < HEAD_DIM, dqs[0], dqs[1]) * (HEAD_DIM ** -0.5)).astype(BF16)

    return pl.pallas_call(
        body,
        name="sb_bwd",
        grid=(4, T // tq),
        in_specs=[
            pl.BlockSpec((tq, 128), lambda p, i: (i, 12 + p)),
            pl.BlockSpec((T, 128), lambda p, i: (0, 16 + p)),
            pl.BlockSpec((T, 128), lambda p, i: (0, 20 + p)),
            pl.BlockSpec((2, tq, 1), lambda p, i: (p, i, 0)),
            pl.BlockSpec((tq, 128), lambda p, i: (i, p)),
        ],
        out_specs=[
            pl.BlockSpec((tq, 128), lambda p, i: (i, p)),
            pl.BlockSpec((T, 128), lambda p, i: (0, p)),
            pl.BlockSpec((T, 128), lambda p, i: (0, p)),
        ],
        out_shape=[
            jax.ShapeDtypeStruct((T, FOX_W), BF16),
            jax.ShapeDtypeStruct((T, FOX_W), F32),
            jax.ShapeDtypeStruct((T, FOX_W), F32),
        ],
        compiler_params=_cparams(("arbitrary", "arbitrary")),
    )(proj, proj, proj, ltot, d_o)


def _fox_bwd(proj, c_col, c_row, lse, d_o, o, tq):
    T = proj.shape[0]

    def body(q_ref, k_ref, v_ref, cq_ref, ck_ref, lse_ref, do_ref, o_ref,
             dq_ref, dk_ref, dv_ref, dck_ref, dcq_ref):
        i = pl.program_id(1)

        @pl.when(i == 0)
        def _():
            dk_ref[...] = jnp.zeros_like(dk_ref)
            dv_ref[...] = jnp.zeros_like(dv_ref)
            dck_ref[...] = jnp.zeros_like(dck_ref)

        lane = lax.broadcasted_iota(jnp.int32, (1, 128), 1)
        row = lax.broadcasted_iota(jnp.int32, (tq, tq), 0)
        col = lax.broadcasted_iota(jnp.int32, (tq, tq), 1)
        q = q_ref[...]
        dov = do_ref[...]
        ov = o_ref[...]
        dqs = []
        for hh in range(2):
            qh, hmask = _head_q(q, hh, lane)
            dohb = jnp.where(hmask, dov, 0.0).astype(BF16)
            delta = jnp.sum(dohb.astype(F32) * ov, axis=1, keepdims=True)
            shift = cq_ref[hh] - lse_ref[hh]

            def step(j, carry, diag, qh=qh, delta=delta, dohb=dohb, shift=shift, hh=hh):
                dq, rs = carry
                ks = pl.multiple_of(j * tq, tq)
                k = k_ref[pl.ds(ks, tq), :]
                v = v_ref[pl.ds(ks, tq), :]
                s = _nt(qh, k) + shift - ck_ref[hh, :, pl.ds(ks, tq)]
                if diag:
                    s = jnp.where(col <= row, s, NEG)
                p = jnp.exp(s)
                ds = p * (_nt(dohb, v) - delta)
                dsb = ds.astype(BF16)
                dk_ref[pl.ds(ks, tq), :] += _tn(dsb, qh)
                dv_ref[pl.ds(ks, tq), :] += _tn(p.astype(BF16), dohb)
                dck_ref[hh, :, pl.ds(ks, tq)] += jnp.sum(ds, axis=0, keepdims=True)
                return dq + _nn(dsb, k), rs + jnp.sum(ds, axis=1, keepdims=True)

            init = (jnp.zeros((tq, 128), F32), jnp.zeros((tq, 1), F32))
            carry = lax.fori_loop(0, i, lambda j, c, step=step: step(j, c, False), init)
            dq, rs = step(i, carry, True)
            dqs.append(dq)
            dcq_ref[hh] = rs
        dq_ref[...] = (jnp.where(lane < HEAD_DIM, dqs[0], dqs[1]) * (HEAD_DIM ** -0.5)).astype(BF16)

    return pl.pallas_call(
        body,
        name="fox_bwd",
        grid=(4, T // tq),
        in_specs=[
            pl.BlockSpec((tq, 128), lambda p, i: (i, p)),
            pl.BlockSpec((T, 128), lambda p, i: (0, 4 + p)),
            pl.BlockSpec((T, 128), lambda p, i: (0, 8 + p)),
            pl.BlockSpec((2, tq, 1), lambda p, i: (p, i, 0)),
            pl.BlockSpec((2, 1, T), lambda p, i: (p, 0, 0)),
            pl.BlockSpec((2, tq, 1), lambda p, i: (p, i, 0)),
            pl.BlockSpec((tq, 128), lambda p, i: (i, p)),
            pl.BlockSpec((tq, 128), lambda p, i: (i, p)),
        ],
        out_specs=[
            pl.BlockSpec((tq, 128), lambda p, i: (i, p)),
            pl.BlockSpec((T, 128), lambda p, i: (0, p)),
            pl.BlockSpec((T, 128), lambda p, i: (0, p)),
            pl.BlockSpec((2, 1, T), lambda p, i: (p, 0, 0)),
            pl.BlockSpec((2, tq, 1), lambda p, i: (p, i, 0)),
        ],
        out_shape=[
            jax.ShapeDtypeStruct((T, FOX_W), BF16),
            jax.ShapeDtypeStruct((T, FOX_W), F32),
            jax.ShapeDtypeStruct((T, FOX_W), F32),
            jax.ShapeDtypeStruct((N_FOX, 1, T), F32),
            jax.ShapeDtypeStruct((N_FOX, T, 1), F32),
        ],
        compiler_params=_cparams(("arbitrary", "arbitrary")),
    )(proj, proj, proj, c_col, c_row, lse, d_o, o)


def _forget_bwd(dcq, dck, xf, tc):
    H, T = xf.shape
    nc = T // tc

    def body(dcq_ref, dck_ref, xf_ref, dxf_ref, db_ref):
        row = lax.broadcasted_iota(jnp.int32, (tc, tc), 0)
        col = lax.broadcasted_iota(jnp.int32, (tc, tc), 1)
        from_here = jnp.where(row >= col, 1.0, 0.0).astype(BF16)

        def chunk(n, carry):
            run, db = carry
            cs = pl.multiple_of((nc - 1 - n) * tc, tc)
            dc = dcq_ref[:, pl.ds(cs, tc)] - dck_ref[:, pl.ds(cs, tc)]
            dlogf = _split_dot(dc, from_here, 3) + run
            xfv = xf_ref[:, pl.ds(cs, tc)]
            dxf = dlogf * jax.nn.sigmoid(-xfv)
            dxf_ref[:, pl.ds(cs, tc)] = dxf
            return dlogf[:, 0:1], db + jnp.sum(dxf, axis=1, keepdims=True)

        _, db = lax.fori_loop(0, nc, chunk, (jnp.zeros((H, 1), F32), jnp.zeros((H, 1), F32)))
        db_ref[...] = db

    return pl.pallas_call(
        body,
        name="forget_bwd",
        out_shape=[jax.ShapeDtypeStruct((H, T), F32), jax.ShapeDtypeStruct((H, 1), F32)],
        compiler_params=_cparams(),
    )(dcq, dck, xf)


def _inproj_bwd(dproj, w_in_t, x, g1, dx1, tm):
    T, D = x.shape
    N = dproj.shape[1]

    def body(dp_ref, wt_ref, x_ref, g_ref, dx1_ref, dx_ref, dg_ref):
        i = pl.program_id(0)

        @pl.when(i == 0)
        def _():
            dg_ref[...] = jnp.zeros_like(dg_ref)

        dh = _nn(dp_ref[...], wt_ref[...])
        dx, dg = _norm_bwd(x_ref[...], g_ref[...], dh)
        dx_ref[...] = dx1_ref[...] + dx
        dg_ref[...] += dg

    return pl.pallas_call(
        body,
        name="inproj_bwd",
        grid=(T // tm,),
        in_specs=[
            pl.BlockSpec((tm, N), lambda i: (i, 0)),
            pl.BlockSpec((N, D), lambda i: (0, 0)),
            pl.BlockSpec((tm, D), lambda i: (i, 0)),
            pl.BlockSpec((1, D), lambda i: (0, 0)),
            pl.BlockSpec((tm, D), lambda i: (i, 0)),
        ],
        out_specs=[
            pl.BlockSpec((tm, D), lambda i: (i, 0)),
            pl.BlockSpec((1, D), lambda i: (0, 0)),
        ],
        out_shape=[jax.ShapeDtypeStruct((T, D), F32), jax.ShapeDtypeStruct((1, D), F32)],
        compiler_params=_cparams(("arbitrary",)),
    )(dproj, w_in_t, x, g1, dx1)


def _matmul_tn(a, b, name, cast_b=False):
    T, K = a.shape
    N = b.shape[1]
    bt = min(T, 512)
    bk = _tile_div(K, 1536)
    bn = _tile_div(N, 1536)

    def body(a_ref, b_ref, o_ref):
        @pl.when(pl.program_id(2) == 0)
        def _():
            o_ref[...] = jnp.zeros_like(o_ref)

        bv = b_ref[...]
        if cast_b:
            bv = bv.astype(BF16)
        o_ref[...] += _tn(a_ref[...], bv)

    return pl.pallas_call(
        body,
        name=name,
        grid=(K // bk, N // bn, T // bt),
        in_specs=[
            pl.BlockSpec((bt, bk), lambda k, n, t: (t, k)),
            pl.BlockSpec((bt, bn), lambda k, n, t: (t, n)),
        ],
        out_specs=pl.BlockSpec((bk, bn), lambda k, n, t: (k, n)),
        out_shape=jax.ShapeDtypeStruct((K, N), F32),
        compiler_params=_cparams(("arbitrary", "arbitrary", "arbitrary")),
    )(a, b)


def _local_step(x, mem, target, p, tm, tq):
    T, D = x.shape
    w_in = p["w_in"]
    w_qkv = w_in[:, :QKV_W]
    w_f_t = w_in[:, QKV_W:].T
    w_in_t = jnp.pad(w_in, ((0, 0), (0, IN_PAD - w_in.shape[1]))).T
    b_f = p["b_forget"].reshape(N_FOX, 1)

    proj, h1, xf, c = _inproj_fwd(x, p["attn_norm_g"], w_qkv, w_f_t, b_f, tm)
    c_col = c.reshape(N_FOX, T, 1)
    c_row = c.reshape(N_FOX, 1, T)
    fox_o, lse = _fox_fwd(proj, c_col, c_row, tq)
    sb_o, sb_ltot = _sb_fwd(proj, tq)
    x1, mixed = _post_attn_fwd(fox_o, sb_o, p["fox_out_g"], p["sb_out_g"], p["w_out"], x, tm)
    mb, kv = _mem_kv_fwd(mem, p["mem_norm_g"], p["w_mkv"])
    x2, h2, qb, om = _xattn_fwd(x1, p["xattn_norm_g"], p["w_mq"], kv, p["w_mo"], tm)
    x3, h3, ug, uv, a = _ffn_fwd(x2, p["ffn_norm_g"], p["w_up"], p["conv_w"], p["conv_b"], p["w_down"], tm)
    dx3, loss_blk, d_final_g = _loss_head(x3, p["final_norm_g"], target, tm)

    g = {"final_norm_g": d_final_g}
    dx2, du_g, du_v, g["ffn_norm_g"], dc_g, dc_v = _ffn_bwd(
        dx3, x2, p["ffn_norm_g"], ug, uv, p["conv_w"], p["conv_b"], p["w_down"].T, p["w_up"].T, tm)
    g["w_down"] = _matmul_tn(a, dx3, "dw_down", cast_b=True)
    g["w_up"] = jnp.concatenate([_matmul_tn(h3, du_g, "dw_up_gate"), _matmul_tn(h3, du_v, "dw_up_val")], axis=1)
    dconv = jnp.concatenate([dc_g, dc_v], axis=1)
    g["conv_w"] = dconv[0:3]
    g["conv_b"] = dconv[3:4]
    dx1, dq_m, dkv, g["xattn_norm_g"] = _xattn_bwd(dx2, x1, p["xattn_norm_g"], qb, kv, p["w_mo"].T, p["w_mq"].T, tm)
    g["w_mo"] = _matmul_tn(om, dx2, "dw_mo", cast_b=True)
    g["w_mq"] = _matmul_tn(h2, dq_m, "dw_mq")
    g["w_mkv"], g["mem_norm_g"] = _mem_kv_bwd(mem, p["mem_norm_g"], mb, dkv, p["w_mkv"].T)
    d_fox, d_sb, g["fox_out_g"], g["sb_out_g"] = _post_attn_bwd(
        dx1, fox_o, sb_o, p["fox_out_g"], p["sb_out_g"], p["w_out"].T, tm)
    g["w_out"] = _matmul_tn(mixed, dx1, "dw_out", cast_b=True)
    dq_s, dk_s, dv_s = _sb_bwd(proj, sb_ltot, d_sb, tq)
    dq_f, dk_f, dv_f, dck, dcq = _fox_bwd(proj, c_col, c_row, lse, d_fox, fox_o, tq)
    dxf, db = _forget_bwd(dcq.reshape(N_FOX, T), dck.reshape(N_FOX, T), xf, min(T, 512))
    g["b_forget"] = db.reshape(1, N_FOX)
    dproj = jnp.concatenate([
        dq_f, dk_f.astype(BF16), dv_f.astype(BF16), dq_s, dk_s.astype(BF16), dv_s.astype(BF16),
        jnp.pad(dxf.T, ((0, 0), (0, IN_PAD - QKV_W - N_FOX))).astype(BF16)], axis=1)
    grad_x, g["attn_norm_g"] = _inproj_bwd(dproj, w_in_t, x, p["attn_norm_g"], dx1, tm)
    g["w_in"] = _matmul_tn(h1, dproj, "dw_in")[:, :w_in.shape[1]]
    return loss_blk, grad_x, g


def _mesh_pos():
    return lax.axis_index("x"), lax.axis_index("y"), lax.axis_index("c")


def _flip(pos, k):
    return tuple(1 - v if (k >> b) & 1 else v for v, b in zip(pos, (2, 1, 0)))


def _slot(pos):
    return 4 * pos[0] + 2 * pos[1] + pos[2]


def _all_gather(shards, name):
    n = len(shards)

    def body(*refs):
        ins, outs = refs[:n], refs[n:2 * n]
        send_sems, recv_sems, local_sems = refs[2 * n:]
        me = _mesh_pos()
        sibling = _flip(me, 1)
        chips = [4, 2, 6]

        def copy(a, kk, block, to, src=None):
            rows = outs[a].at[_slot(block)]
            return pltpu.make_async_remote_copy(
                src_ref=rows if src is None else src, dst_ref=rows,
                send_sem=send_sems.at[7 * a + kk], recv_sem=recv_sems.at[7 * a + kk],
                device_id=to, device_id_type=MESH)

        mine = [pltpu.make_async_copy(ins[a], outs[a].at[_slot(me)], local_sems.at[a]) for a in range(n)]
        for cp in mine:
            cp.start()
        first = []
        for a in range(n):
            first.append(copy(a, 0, me, sibling, src=ins[a]))
            first += [copy(a, 1 + j, me, _flip(me, k), src=ins[a]) for j, k in enumerate(chips)]
        for cp in first:
            cp.start()
        passed = []
        for j, k in enumerate(chips):
            for a in range(n):
                copy(a, 1 + j, _flip(me, k), me).wait_recv()
                fwd = copy(a, 4 + j, _flip(me, k), sibling)
                fwd.start()
                passed.append(fwd)
        for a in range(n):
            copy(a, 0, sibling, me).wait_recv()
            for j, k in enumerate(chips):
                copy(a, 4 + j, _flip(sibling, k), me).wait_recv()
        for cp in first + passed:
            cp.wait_send()
        for cp in mine:
            cp.wait()

    any_spec = pl.BlockSpec(memory_space=pl.ANY)
    return pl.pallas_call(
        body,
        name=name,
        in_specs=[any_spec] * n,
        out_specs=[any_spec] * n,
        out_shape=[jax.ShapeDtypeStruct((N_DEV,) + s.shape, s.dtype) for s in shards],
        scratch_shapes=[
            pltpu.SemaphoreType.DMA((7 * n,)),
            pltpu.SemaphoreType.DMA((7 * n,)),
            pltpu.SemaphoreType.DMA((n,)),
        ],
    )(*shards)


def _all_to_all(blocks, name):
    n = len(blocks)

    def body(*refs):
        ins, outs = refs[:n], refs[n:2 * n]
        send_sems, recv_sems, local_sems = refs[2 * n:]
        me = _mesh_pos()
        mine = [pltpu.make_async_copy(ins[a].at[_slot(me)], outs[a].at[_slot(me)], local_sems.at[a])
                for a in range(n)]
        for cp in mine:
            cp.start()

        def copy(a, k):
            peer = _flip(me, k)
            return pltpu.make_async_remote_copy(
                src_ref=ins[a].at[_slot(peer)], dst_ref=outs[a].at[_slot(me)],
                send_sem=send_sems.at[7 * a + k - 1], recv_sem=recv_sems.at[7 * a + k - 1],
                device_id=peer, device_id_type=MESH)

        def landed(a, k):
            peer = _flip(me, k)
            return pltpu.make_async_remote_copy(
                src_ref=ins[a].at[_slot(peer)], dst_ref=outs[a].at[_slot(peer)],
                send_sem=send_sems.at[7 * a + k - 1], recv_sem=recv_sems.at[7 * a + k - 1],
                device_id=peer, device_id_type=MESH)

        sent = [copy(a, k) for k in range(1, 8) for a in range(n)]
        for cp in sent:
            cp.start()
        for k in range(1, 8):
            for a in range(n):
                landed(a, k).wait_recv()
        for cp in sent:
            cp.wait_send()
        for cp in mine:
            cp.wait()

    any_spec = pl.BlockSpec(memory_space=pl.ANY)
    return pl.pallas_call(
        body,
        name=name,
        in_specs=[any_spec] * n,
        out_specs=[any_spec] * n,
        out_shape=[jax.ShapeDtypeStruct(b.shape, b.dtype) for b in blocks],
        scratch_shapes=[
            pltpu.SemaphoreType.DMA((7 * n,)),
            pltpu.SemaphoreType.DMA((7 * n,)),
            pltpu.SemaphoreType.DMA((n,)),
        ],
    )(*blocks)


def _adamw_math(w, g, m, v):
    m2 = ADAM_B1 * m + (1.0 - ADAM_B1) * g
    v2 = ADAM_B2 * v + (1.0 - ADAM_B2) * (g * g)
    m_hat = m2 / (1.0 - ADAM_B1 ** ADAM_STEP)
    v_hat = v2 / (1.0 - ADAM_B2 ** ADAM_STEP)
    delta = -ADAM_LR * (m_hat / (jnp.sqrt(v_hat) + ADAM_EPS) + ADAM_WD * w)
    return delta, m2, v2


def _adamw(w, parts, m, v, name):
    R, C = w.shape
    br = 128 if R % 128 == 0 else R

    def body(w_ref, p_ref, m_ref, v_ref, g_ref, d_ref, nm_ref, nv_ref):
        g = p_ref[0]
        for s in range(1, N_DEV):
            g = g + p_ref[s]
        g_ref[...] = g
        d_ref[...], nm_ref[...], nv_ref[...] = _adamw_math(w_ref[...], g, m_ref[...], v_ref[...])

    spec = pl.BlockSpec((br, C), lambda i: (i, 0))
    return pl.pallas_call(
        body,
        name=name,
        grid=(R // br,),
        in_specs=[spec, pl.BlockSpec((N_DEV, br, C), lambda i: (0, i, 0)), spec, spec],
        out_specs=[spec] * 4,
        out_shape=[jax.ShapeDtypeStruct((R, C), F32)] * 4,
        compiler_params=_cparams(("arbitrary",)),
    )(w, parts, m, v)


_SHARDED = ("w_in", "w_out", "w_mq", "w_mkv", "w_mo", "w_up", "conv_w", "w_down")
_COL_SHARDED = ("w_in", "w_mkv", "w_up", "conv_w")
_REPLICATED = ("attn_norm_g", "b_forget", "fox_out_g", "sb_out_g", "xattn_norm_g", "mem_norm_g",
               "ffn_norm_g", "conv_b", "final_norm_g")
_WEIGHTS = ("attn_norm_g", "w_in", "b_forget", "fox_out_g", "sb_out_g", "w_out", "xattn_norm_g", "mem_norm_g",
            "w_mq", "w_mkv", "w_mo", "ffn_norm_g", "w_up", "conv_w", "conv_b", "w_down", "final_norm_g")


def _pack_rows(n):
    return -(-n // 128)


def _pack(vals, rows_total):
    parts = []
    for v in vals:
        flat = v.reshape(-1)
        parts.append(jnp.pad(flat, (0, _pack_rows(flat.shape[0]) * 128 - flat.shape[0])))
    flat = jnp.concatenate(parts)
    return jnp.pad(flat, (0, rows_total * 128 - flat.shape[0])).reshape(rows_total, 128)


def _unpack(packed, shapes):
    out = []
    r = 0
    for shp in shapes:
        n = 1
        for d in shp:
            n *= d
        out.append(packed[r:r + _pack_rows(n)].reshape(-1)[:n].reshape(shp))
        r += _pack_rows(n)
    return out


def _gathered_full(name, gathered):
    if name in _COL_SHARDED:
        return jnp.transpose(gathered, (1, 0, 2)).reshape(gathered.shape[1], -1)
    return gathered.reshape(-1, gathered.shape[2])


def _to_blocks(name, full):
    if name in _COL_SHARDED:
        r = full.shape[0]
        return jnp.transpose(full.reshape(r, N_DEV, -1), (1, 0, 2))
    return full.reshape(N_DEV, -1, full.shape[1])


def _step(args, tm, tq):
    w = {n: args[n] for n in _WEIGHTS}
    mom = {n: args["m_" + n] for n in _WEIGHTS}
    var = {n: args["v_" + n] for n in _WEIGHTS}
    x = args["x"][0]
    mem = args["mem"][0]
    target = args["loss_target"][0]

    def flat2(a):
        return a.reshape(a.shape[-2], a.shape[-1]) if a.ndim == 3 else a.reshape(1, -1)

    shards = [flat2(w[n]) if n == "conv_w" else flat2(w[n]).astype(BF16) for n in _SHARDED]
    gathered = _all_gather(shards, "gather_weights")
    p = {n: _gathered_full(n, gv) for n, gv in zip(_SHARDED, gathered)}
    for n in _REPLICATED:
        p[n] = flat2(w[n])

    loss_blk, grad_x, g = _local_step(x, mem, target, p, tm, tq)

    parts = _all_to_all([_to_blocks(n, g[n]) for n in _SHARDED], "scatter_grads")
    out = {}
    for n, pr in zip(_SHARDED, parts):
        res = _adamw(flat2(w[n]), pr, flat2(mom[n]), flat2(var[n]), "adamw_" + n)
        out[n] = [r.reshape(w[n].shape) for r in res]

    shapes = [w[n].shape for n in _REPLICATED]
    rows = sum(_pack_rows(flat2(w[n]).shape[1]) for n in _REPLICATED) + 1
    rows = -(-rows // 8) * 8
    g_pack = _pack([g[n] for n in _REPLICATED] + [loss_blk[0:1, :]], rows)
    (g_all,) = _all_gather([g_pack], "gather_small")
    res = _adamw(_pack([w[n] for n in _REPLICATED], rows), g_all,
                 _pack([mom[n] for n in _REPLICATED], rows), _pack([var[n] for n in _REPLICATED], rows),
                 "adamw_small")
    n_rows_params = sum(_pack_rows(flat2(w[n]).shape[1]) for n in _REPLICATED)
    loss = res[0][n_rows_params, 0]
    unpacked = [_unpack(r, shapes) for r in res]
    for k, n in enumerate(_REPLICATED):
        out[n] = [unpacked[q][k] for q in range(4)]

    grads = [out[n][0] for n in _WEIGHTS]
    deltas = [out[n][1] for n in _WEIGHTS]
    new_m = [out[n][2] for n in _WEIGHTS]
    new_v = [out[n][3] for n in _WEIGHTS]
    return (loss, grad_x[None], *grads, *deltas, *new_m, *new_v)


def kernel(x, mem, attn_norm_g, w_in, b_forget, fox_out_g, sb_out_g, w_out, xattn_norm_g, mem_norm_g, w_mq, w_mkv, w_mo, ffn_norm_g, w_up, conv_w, conv_b, w_down, final_norm_g, loss_target, m_attn_norm_g, m_w_in, m_b_forget, m_fox_out_g, m_sb_out_g, m_w_out, m_xattn_norm_g, m_mem_norm_g, m_w_mq, m_w_mkv, m_w_mo, m_ffn_norm_g, m_w_up, m_conv_w, m_conv_b, m_w_down, m_final_norm_g, v_attn_norm_g, v_w_in, v_b_forget, v_fox_out_g, v_sb_out_g, v_w_out, v_xattn_norm_g, v_mem_norm_g, v_w_mq, v_w_mkv, v_w_mo, v_ffn_norm_g, v_w_up, v_conv_w, v_conv_b, v_w_down, v_final_norm_g):
    args = dict(locals())
    T = x.shape[1]
    return _step(args, tm=min(T, 512), tq=min(T, 256))
```

```python
import functools

import jax
import jax.numpy as jnp
from jax import lax
from jax.experimental import pallas as pl
from jax.experimental.pallas import tpu as pltpu

F32 = jnp.float32
BF16 = jnp.bfloat16
EPS = 1e-6
NEG = -1e30

HEAD_DIM = 64
N_FOX = 8
FOX_W = 512
QKV_W = 3072
IN_PAD = 3200
N_MEM_HEADS = 4
MEM_HD = 256
D_FF = 2816
FF_CHUNK = 256
N_DEV = 8

ADAM_LR = 0.001
ADAM_B1 = 0.9
ADAM_B2 = 0.999
ADAM_EPS = 1e-08
ADAM_WD = 0.01
ADAM_STEP = 10

VMEM_LIMIT = 56 * 1024 * 1024
MESH = pl.DeviceIdType.MESH


def _cparams(sem=None):
    return pltpu.CompilerParams(dimension_semantics=sem, vmem_limit_bytes=VMEM_LIMIT)


def _nt(a, b):
    return lax.dot_general(a, b, (((1,), (1,)), ((), ())), preferred_element_type=F32)


def _tn(a, b):
    return lax.dot_general(a, b, (((0,), (0,)), ((), ())), preferred_element_type=F32)


def _nn(a, b):
    return jnp.dot(a, b, preferred_element_type=F32)


def _split_dot(a, m01, terms):
    out = None
    r = a
    for t in range(terms):
        p = r.astype(BF16)
        d = _nn(p, m01)
        out = d if out is None else out + d
        if t + 1 < terms:
            r = r - p.astype(F32)
    return out


def _rstd(xv):
    return lax.rsqrt(jnp.mean(xv * xv, axis=-1, keepdims=True) + EPS)


def _norm_bwd(xv, g, dh):
    r = _rstd(xv)
    xhat = xv * r
    dxhat = dh * g
    dx = r * (dxhat - xhat * jnp.mean(dxhat * xhat, axis=-1, keepdims=True))
    dg = jnp.sum(dh * xhat, axis=0, keepdims=True)
    return dx, dg


def _tile_div(n, cap):
    best = None
    for d in range(128, min(n, cap) + 1, 128):
        if n % d == 0:
            best = d
    assert best is not None, n
    return best


def _inproj_fwd(x, g1, w_qkv, w_f_t, b_f, tm):
    T, D = x.shape
    N = w_qkv.shape[1]
    H = w_f_t.shape[0]

    def body(x_ref, g_ref, w_ref, wf_ref, b_ref, proj_ref, h_ref, xf_ref, c_ref, carry_ref):
        i = pl.program_id(0)

        @pl.when(i == 0)
        def _():
            carry_ref[...] = jnp.zeros_like(carry_ref)

        xv = x_ref[...]
        h = (xv * _rstd(xv) * g_ref[...]).astype(BF16)
        h_ref[...] = h
        for n0 in range(0, N, 512):
            proj_ref[:, n0:n0 + 512] = _nn(h, w_ref[:, n0:n0 + 512]).astype(BF16)
        xf = _nt(wf_ref[...], h) + b_ref[...]
        xf_ref[...] = xf
        logf = jnp.minimum(xf, 0.0) - jnp.log1p(jnp.exp(-jnp.abs(xf)))
        row = lax.broadcasted_iota(jnp.int32, (tm, tm), 0)
        col = lax.broadcasted_iota(jnp.int32, (tm, tm), 1)
        upper = jnp.where(row <= col, 1.0, 0.0).astype(BF16)
        c = _split_dot(logf, upper, 3) + carry_ref[...]
        c_ref[...] = c
        carry_ref[...] = c[:, tm - 1:tm]

    return pl.pallas_call(
        body,
        name="inproj_fwd",
        grid=(T // tm,),
        in_specs=[
            pl.BlockSpec((tm, D), lambda i: (i, 0)),
            pl.BlockSpec((1, D), lambda i: (0, 0)),
            pl.BlockSpec((D, N), lambda i: (0, 0)),
            pl.BlockSpec((H, D), lambda i: (0, 0)),
            pl.BlockSpec((H, 1), lambda i: (0, 0)),
        ],
        out_specs=[
            pl.BlockSpec((tm, N), lambda i: (i, 0)),
            pl.BlockSpec((tm, D), lambda i: (i, 0)),
            pl.BlockSpec((H, tm), lambda i: (0, i)),
            pl.BlockSpec((H, tm), lambda i: (0, i)),
        ],
        out_shape=[
            jax.ShapeDtypeStruct((T, N), BF16),
            jax.ShapeDtypeStruct((T, D), BF16),
            jax.ShapeDtypeStruct((H, T), F32),
            jax.ShapeDtypeStruct((H, T), F32),
        ],
        scratch_shapes=[pltpu.VMEM((H, 1), F32)],
        compiler_params=_cparams(("arbitrary",)),
    )(x, g1, w_qkv, w_f_t, b_f)


def _head_q(q, hh, lane):
    hmask = (lane >= HEAD_DIM * hh) & (lane < HEAD_DIM * (hh + 1))
    qh = jnp.where(hmask, q.astype(F32), 0.0) * (HEAD_DIM ** -0.5)
    return qh.astype(BF16), hmask


def _fox_fwd(proj, c_col, c_row, tq):
    T = proj.shape[0]

    def body(q_ref, k_ref, v_ref, cq_ref, ck_ref, o_ref, lse_ref, qh_s, z_s, p_s, al_s, m_s, l_s, acc_s):
        i = pl.program_id(1)
        n = i + 1
        lane = lax.broadcasted_iota(jnp.int32, (1, 128), 1)
        row = lax.broadcasted_iota(jnp.int32, (tq, tq), 0)
        col = lax.broadcasted_iota(jnp.int32, (tq, tq), 1)
        q = q_ref[...]
        for hh in range(2):
            qh_s[hh] = _head_q(q, hh, lane)[0]
        m_s[...] = jnp.full(m_s.shape, NEG, F32)
        l_s[...] = jnp.zeros_like(l_s)
        acc_s[...] = jnp.zeros_like(acc_s)

        def rows(t):
            return pl.ds(pl.multiple_of((i - t) * tq, tq), tq)

        def stage_a(t):
            k = k_ref[rows(t), :]
            for hh in range(2):
                z_s[hh] = _nt(qh_s[hh], k)

        def stage_b(t, diag):
            for hh in range(2):
                s = z_s[hh] + cq_ref[hh] - ck_ref[hh, :, rows(t)]
                if diag:
                    s = jnp.where(col <= row, s, NEG)
                m = m_s[hh]
                m_new = jnp.maximum(m, jnp.max(s, axis=1, keepdims=True))
                alpha = jnp.exp(m - m_new)
                p = jnp.exp(s - m_new)
                l_s[hh] = alpha * l_s[hh] + jnp.sum(p, axis=1, keepdims=True)
                m_s[hh] = m_new
                al_s[hh] = alpha
                p_s[hh] = p.astype(BF16)

        def stage_c(t):
            v = v_ref[rows(t), :]
            for hh in range(2):
                acc_s[hh] = al_s[hh] * acc_s[hh] + _nn(p_s[hh], v)

        stage_a(0)
        stage_b(0, True)

        @pl.when(n >= 2)
        def _():
            stage_a(1)

        def step(t, carry):
            stage_c(t - 2)
            stage_b(t - 1, False)
            stage_a(t)
            return carry

        lax.fori_loop(2, n, step, 0)

        @pl.when(n >= 2)
        def _():
            stage_c(n - 2)
            stage_b(n - 1, False)

        stage_c(n - 1)
        l0, l1 = l_s[0], l_s[1]
        o_ref[...] = jnp.where(lane < HEAD_DIM, acc_s[0] / l0, acc_s[1] / l1)
        lse_ref[0] = m_s[0] + jnp.log(l0)
        lse_ref[1] = m_s[1] + jnp.log(l1)

    return pl.pallas_call(
        body,
        name="fox_fwd",
        grid=(4, T // tq),
        in_specs=[
            pl.BlockSpec((tq, 128), lambda p, i: (i, p)),
            pl.BlockSpec((T, 128), lambda p, i: (0, 4 + p)),
            pl.BlockSpec((T, 128), lambda p, i: (0, 8 + p)),
            pl.BlockSpec((2, tq, 1), lambda p, i: (p, i, 0)),
            pl.BlockSpec((2, 1, T), lambda p, i: (p, 0, 0)),
        ],
        out_specs=[
            pl.BlockSpec((tq, 128), lambda p, i: (i, p)),
            pl.BlockSpec((2, tq, 1), lambda p, i: (p, i, 0)),
        ],
        out_shape=[
            jax.ShapeDtypeStruct((T, FOX_W), F32),
            jax.ShapeDtypeStruct((N_FOX, T, 1), F32),
        ],
        scratch_shapes=[
            pltpu.VMEM((2, tq, 128), BF16),
            pltpu.VMEM((2, tq, tq), F32),
            pltpu.VMEM((2, tq, tq), BF16),
            pltpu.VMEM((2, tq, 1), F32),
            pltpu.VMEM((2, tq, 1), F32),
            pltpu.VMEM((2, tq, 1), F32),
            pltpu.VMEM((2, tq, 128), F32),
        ],
        compiler_params=_cparams(("arbitrary", "arbitrary")),
    )(proj, proj, proj, c_col, c_row)


def _sb_terms(z, strict):
    e = jnp.exp(-jnp.abs(z))
    L = -(jnp.maximum(z, 0.0) + jnp.log(1.0 + e))
    if strict is not None:
        L = jnp.where(strict, L, 0.0)
    return L, e


def _sb_fwd(proj, tq):
    T = proj.shape[0]

    def body(q_ref, k_ref, v_ref, o_ref, ltot_ref, qh_s, z_s, g_s, tot_s, run_s, acc_s):
        i = pl.program_id(1)
        n = i + 1
        lane = lax.broadcasted_iota(jnp.int32, (1, 128), 1)
        row = lax.broadcasted_iota(jnp.int32, (tq, tq), 0)
        col = lax.broadcasted_iota(jnp.int32, (tq, tq), 1)
        strict = col < row
        later = jnp.where(row > col, 1.0, 0.0).astype(BF16)
        q = q_ref[...]
        for hh in range(2):
            qh_s[hh] = _head_q(q, hh, lane)[0]
        run_s[...] = jnp.zeros_like(run_s)
        acc_s[...] = jnp.zeros_like(acc_s)

        def rows(t):
            return pl.ds(pl.multiple_of((i - t) * tq, tq), tq)

        def stage_a(t):
            k = k_ref[rows(t), :]
            for hh in range(2):
                z_s[hh] = _nt(qh_s[hh], k)

        def stage_b(diag):
            for hh in range(2):
                z = z_s[hh]
                L, _ = _sb_terms(z, strict if diag else None)
                g = z + L
                if diag:
                    g = jnp.where(strict, g, NEG)
                after = _split_dot(L, later, 2)
                g_s[hh] = g + after
                tot_s[hh] = after[:, 0:1] + L[:, 0:1]

        def stage_c(t):
            v = v_ref[rows(t), :]
            for hh in range(2):
                run = run_s[hh]
                a = jnp.exp(g_s[hh] + run)
                acc_s[hh] += _nn(a.astype(BF16), v)
                run_s[hh] = run + tot_s[hh]

        stage_a(0)
        stage_b(True)

        @pl.when(n >= 2)
        def _():
            stage_a(1)

        def step(t, carry):
            stage_c(t - 2)
            stage_b(False)
            stage_a(t)
            return carry

        lax.fori_loop(2, n, step, 0)

        @pl.when(n >= 2)
        def _():
            stage_c(n - 2)
            stage_b(False)

        stage_c(n - 1)
        ltot_ref[0] = run_s[0]
        ltot_ref[1] = run_s[1]
        o_ref[...] = jnp.where(lane < HEAD_DIM, acc_s[0], acc_s[1])

    return pl.pallas_call(
        body,
        name="sb_fwd",
        grid=(4, T // tq),
        in_specs=[
            pl.BlockSpec((tq, 128), lambda p, i: (i, 12 + p)),
            pl.BlockSpec((T, 128), lambda p, i: (0, 16 + p)),
            pl.BlockSpec((T, 128), lambda p, i: (0, 20 + p)),
        ],
        out_specs=[
            pl.BlockSpec((tq, 128), lambda p, i: (i, p)),
            pl.BlockSpec((2, tq, 1), lambda p, i: (p, i, 0)),
        ],
        out_shape=[
            jax.ShapeDtypeStruct((T, FOX_W), F32),
            jax.ShapeDtypeStruct((N_FOX, T, 1), F32),
        ],
        scratch_shapes=[
            pltpu.VMEM((2, tq, 128), BF16),
            pltpu.VMEM((2, tq, tq), F32),
            pltpu.VMEM((2, tq, tq), F32),
            pltpu.VMEM((2, tq, 1), F32),
            pltpu.VMEM((2, tq, 1), F32),
            pltpu.VMEM((2, tq, 128), F32),
        ],
        compiler_params=_cparams(("arbitrary", "arbitrary")),
    )(proj, proj, proj)


def _post_attn_fwd(fox_o, sb_o, gf, gs, w_out, x, tm):
    T, D = x.shape

    def body(f_ref, s_ref, gf_ref, gs_ref, w_ref, x_ref, x1_ref, mix_ref):
        f = f_ref[...]
        s = s_ref[...]
        mix_ref[:, :FOX_W] = (f * _rstd(f) * gf_ref[...]).astype(BF16)
        mix_ref[:, FOX_W:] = (s * _rstd(s) * gs_ref[...]).astype(BF16)
        x1_ref[...] = x_ref[...] + _nn(mix_ref[...], w_ref[...])

    return pl.pallas_call(
        body,
        name="post_attn_fwd",
        grid=(T // tm,),
        in_specs=[
            pl.BlockSpec((tm, FOX_W), lambda i: (i, 0)),
            pl.BlockSpec((tm, FOX_W), lambda i: (i, 0)),
            pl.BlockSpec((1, FOX_W), lambda i: (0, 0)),
            pl.BlockSpec((1, FOX_W), lambda i: (0, 0)),
            pl.BlockSpec((D, D), lambda i: (0, 0)),
            pl.BlockSpec((tm, D), lambda i: (i, 0)),
        ],
        out_specs=[
            pl.BlockSpec((tm, D), lambda i: (i, 0)),
            pl.BlockSpec((tm, D), lambda i: (i, 0)),
        ],
        out_shape=[jax.ShapeDtypeStruct((T, D), F32), jax.ShapeDtypeStruct((T, D), BF16)],
        compiler_params=_cparams(("arbitrary",)),
    )(fox_o, sb_o, gf, gs, w_out, x)


def _mem_kv_fwd(mem, gm, w_mkv):
    M, D = mem.shape
    N = w_mkv.shape[1]

    def body(mem_ref, g_ref, w_ref, m_ref, kv_ref):
        mv = mem_ref[...]
        m = (mv * _rstd(mv) * g_ref[...]).astype(BF16)
        m_ref[...] = m
        for n0 in range(0, N, 512):
            kv_ref[:, n0:n0 + 512] = _nn(m, w_ref[:, n0:n0 + 512]).astype(BF16)

    return pl.pallas_call(
        body,
        name="mem_kv_fwd",
        out_shape=[jax.ShapeDtypeStruct((M, D), BF16), jax.ShapeDtypeStruct((M, N), BF16)],
        compiler_params=_cparams(),
    )(mem, gm, w_mkv)


def _xattn_probs(qb, kv, h):
    k = kv[:, h * MEM_HD:(h + 1) * MEM_HD]
    s = _nt(qb[:, h * MEM_HD:(h + 1) * MEM_HD], k) * (MEM_HD ** -0.5)
    s = s - jnp.max(s, axis=1, keepdims=True)
    p = jnp.exp(s)
    return p / jnp.sum(p, axis=1, keepdims=True)


def _xattn_fwd(x1, g2, w_mq, kv, w_mo, tm):
    T, D = x1.shape
    M = kv.shape[0]

    def body(x_ref, g_ref, wq_ref, kv_ref, wo_ref, x2_ref, h_ref, q_ref, om_ref):
        xv = x_ref[...]
        h = (xv * _rstd(xv) * g_ref[...]).astype(BF16)
        h_ref[...] = h
        q_ref[...] = _nn(h, wq_ref[...]).astype(BF16)
        qb = q_ref[...]
        kvv = kv_ref[...]
        for hd in range(N_MEM_HEADS):
            p = _xattn_probs(qb, kvv, hd)
            v = kvv[:, D + hd * MEM_HD:D + (hd + 1) * MEM_HD]
            om_ref[:, hd * MEM_HD:(hd + 1) * MEM_HD] = _nn(p.astype(BF16), v).astype(BF16)
        x2_ref[...] = xv + _nn(om_ref[...], wo_ref[...])

    return pl.pallas_call(
        body,
        name="xattn_fwd",
        grid=(T // tm,),
        in_specs=[
            pl.BlockSpec((tm, D), lambda i: (i, 0)),
            pl.BlockSpec((1, D), lambda i: (0, 0)),
            pl.BlockSpec((D, D), lambda i: (0, 0)),
            pl.BlockSpec((M, 2 * D), lambda i: (0, 0)),
            pl.BlockSpec((D, D), lambda i: (0, 0)),
        ],
        out_specs=[pl.BlockSpec((tm, D), lambda i: (i, 0))] * 4,
        out_shape=[jax.ShapeDtypeStruct((T, D), F32)] + [jax.ShapeDtypeStruct((T, D), BF16)] * 3,
        compiler_params=_cparams(("arbitrary",)),
    )(x1, g2, w_mq, kv, w_mo)


def _conv_taps(ext_ref, tm, back):
    if back:
        return ext_ref[pl.ds(6, tm), :], ext_ref[pl.ds(7, tm), :], ext_ref[pl.ds(8, tm), :]
    return ext_ref[pl.ds(0, tm), :], ext_ref[pl.ds(1, tm), :], ext_ref[pl.ds(2, tm), :]


def _ffn_fwd(x2, g3, w_up, conv_w, conv_b, w_down, tm):
    T, D = x2.shape
    fc = FF_CHUNK
    nj = D_FF // fc

    def body(x_ref, g_ref, wg_ref, wv_ref, cwg_ref, cwv_ref, cbg_ref, cbv_ref, wd_ref,
             x3_ref, h_ref, ug_ref, uv_ref, a_ref, acc_ref, carry_ref, ext_ref):
        i = pl.program_id(0)
        j = pl.program_id(1)

        @pl.when(j == 0)
        def _():
            xv = x_ref[...]
            h_ref[...] = (xv * _rstd(xv) * g_ref[...]).astype(BF16)
            acc_ref[...] = xv

        @pl.when(i == 0)
        def _():
            carry_ref[j] = jnp.zeros((2, 8, fc), F32)

        h = h_ref[...]
        halves = []
        for part, (w_ref, cw_ref, cb_ref, u_ref) in enumerate(
                ((wg_ref, cwg_ref, cbg_ref, ug_ref), (wv_ref, cwv_ref, cbv_ref, uv_ref))):
            u = _nn(h, w_ref[...])
            u_ref[...] = u
            ext = ext_ref.at[part]
            ext[pl.ds(0, 8), :] = carry_ref[j, part]
            ext[pl.ds(8, tm), :] = u
            carry_ref[j, part] = u[tm - 8:, :]
            u2, u1, u0 = _conv_taps(ext, tm, True)
            cw = cw_ref[...]
            halves.append(cb_ref[...] + cw[0:1] * u2 + cw[1:2] * u1 + cw[2:3] * u0)
        gate, val = halves
        a = (gate * jax.nn.sigmoid(gate) * val).astype(BF16)
        a_ref[...] = a
        acc_ref[...] += _nn(a, wd_ref[...])

        @pl.when(j == nj - 1)
        def _():
            x3_ref[...] = acc_ref[...]

    return pl.pallas_call(
        body,
        name="ffn_fwd",
        grid=(T // tm, nj),
        in_specs=[
            pl.BlockSpec((tm, D), lambda i, j: (i, 0)),
            pl.BlockSpec((1, D), lambda i, j: (0, 0)),
            pl.BlockSpec((D, fc), lambda i, j: (0, j)),
            pl.BlockSpec((D, fc), lambda i, j: (0, nj + j)),
            pl.BlockSpec((3, fc), lambda i, j: (0, j)),
            pl.BlockSpec((3, fc), lambda i, j: (0, nj + j)),
            pl.BlockSpec((1, fc), lambda i, j: (0, j)),
            pl.BlockSpec((1, fc), lambda i, j: (0, nj + j)),
            pl.BlockSpec((fc, D), lambda i, j: (j, 0)),
        ],
        out_specs=[
            pl.BlockSpec((tm, D), lambda i, j: (i, 0)),
            pl.BlockSpec((tm, D), lambda i, j: (i, 0)),
            pl.BlockSpec((tm, fc), lambda i, j: (i, j)),
            pl.BlockSpec((tm, fc), lambda i, j: (i, j)),
            pl.BlockSpec((tm, fc), lambda i, j: (i, j)),
        ],
        out_shape=[
            jax.ShapeDtypeStruct((T, D), F32),
            jax.ShapeDtypeStruct((T, D), BF16),
            jax.ShapeDtypeStruct((T, D_FF), F32),
            jax.ShapeDtypeStruct((T, D_FF), F32),
            jax.ShapeDtypeStruct((T, D_FF), BF16),
        ],
        scratch_shapes=[
            pltpu.VMEM((tm, D), F32),
            pltpu.VMEM((nj, 2, 8, fc), F32),
            pltpu.VMEM((2, tm + 8, fc), F32),
        ],
        compiler_params=_cparams(("arbitrary", "arbitrary")),
    )(x2, g3, w_up, w_up, conv_w, conv_w, conv_b, conv_b, w_down)


def _loss_head(x3, gfin, target, tm):
    T, D = x3.shape

    def body(x_ref, g_ref, t_ref, dx_ref, loss_ref, dg_ref):
        i = pl.program_id(0)

        @pl.when(i == 0)
        def _():
            loss_ref[...] = jnp.zeros_like(loss_ref)
            dg_ref[...] = jnp.zeros_like(dg_ref)

        xv = x_ref[...]
        g = g_ref[...]
        r = _rstd(xv)
        xhat = xv * r
        err = xhat * g - t_ref[...]
        part = jnp.sum(jnp.sum(err * err, axis=1, keepdims=True), axis=0, keepdims=True) * (0.5 / D)
        loss_ref[...] += jnp.broadcast_to(part, loss_ref.shape)
        dy = err * (1.0 / D)
        dg_ref[...] += jnp.sum(dy * xhat, axis=0, keepdims=True)
        dxhat = dy * g
        dx_ref[...] = r * (dxhat - xhat * jnp.mean(dxhat * xhat, axis=-1, keepdims=True))

    return pl.pallas_call(
        body,
        name="loss_head",
        grid=(T // tm,),
        in_specs=[
            pl.BlockSpec((tm, D), lambda i: (i, 0)),
            pl.BlockSpec((1, D), lambda i: (0, 0)),
            pl.BlockSpec((tm, D), lambda i: (i, 0)),
        ],
        out_specs=[
            pl.BlockSpec((tm, D), lambda i: (i, 0)),
            pl.BlockSpec((8, 128), lambda i: (0, 0)),
            pl.BlockSpec((1, D), lambda i: (0, 0)),
        ],
        out_shape=[
            jax.ShapeDtypeStruct((T, D), F32),
            jax.ShapeDtypeStruct((8, 128), F32),
            jax.ShapeDtypeStruct((1, D), F32),
        ],
        compiler_params=_cparams(("arbitrary",)),
    )(x3, gfin, target)


def _ffn_bwd(dx3, x2, g3, ug, uv, conv_w, conv_b, w_down_t, w_up_t, tm):
    T, D = x2.shape
    fc = FF_CHUNK
    nj = D_FF // fc
    nt = T // tm
    hb = tm // 8

    def rev(i):
        return nt - 1 - i

    def body(dx3_ref, x_ref, g_ref, ug_ref, uv_ref, ugh_ref, uvh_ref, cwg_ref, cwv_ref, cbg_ref, cbv_ref,
             wdt_ref, wutg_ref, wutv_ref,
             dx2_ref, dug_ref, duv_ref, dg_ref, dcg_ref, dcv_ref,
             acc_ref, carry_ref, ext_ref):
        i = pl.program_id(0)
        j = pl.program_id(1)
        first_tile = i == nt - 1
        cols = pl.ds(pl.multiple_of(j * fc, fc), fc)

        @pl.when(j == 0)
        def _():
            acc_ref[...] = jnp.zeros_like(acc_ref)

        @pl.when((i == 0) & (j == 0))
        def _():
            dg_ref[...] = jnp.zeros_like(dg_ref)
            dcg_ref[...] = jnp.zeros_like(dcg_ref)
            dcv_ref[...] = jnp.zeros_like(dcv_ref)

        @pl.when(i == 0)
        def _():
            carry_ref[j] = jnp.zeros((2, 8, fc), F32)

        da = _nn(dx3_ref[...].astype(BF16), wdt_ref[...])
        pre = []
        for part, (u_ref, uh_ref, cw_ref, cb_ref) in enumerate(
                ((ug_ref, ugh_ref, cwg_ref, cbg_ref), (uv_ref, uvh_ref, cwv_ref, cbv_ref))):
            ext = ext_ref.at[part]
            ext[pl.ds(0, 8), :] = jnp.where(first_tile, 0.0, uh_ref[...])
            ext[pl.ds(8, tm), :] = u_ref[...]
            u2, u1, u0 = _conv_taps(ext, tm, True)
            cw = cw_ref[...]
            pre.append(cb_ref[...] + cw[0:1] * u2 + cw[1:2] * u1 + cw[2:3] * u0)
        gate, val = pre
        sig = jax.nn.sigmoid(gate)
        silu = gate * sig
        dys = (da * val * (sig * (1.0 + gate * (1.0 - sig))), da * silu)
        for part, (dy, cw_ref, du_ref, wut_ref, dc_ref) in enumerate(
                ((dys[0], cwg_ref, dug_ref, wutg_ref, dcg_ref), (dys[1], cwv_ref, duv_ref, wutv_ref, dcv_ref))):
            u2, u1, u0 = _conv_taps(ext_ref.at[part], tm, True)
            upd = jnp.concatenate([
                jnp.sum(u2 * dy, axis=0, keepdims=True),
                jnp.sum(u1 * dy, axis=0, keepdims=True),
                jnp.sum(u0 * dy, axis=0, keepdims=True),
                jnp.sum(dy, axis=0, keepdims=True),
                jnp.zeros((4, fc), F32)], axis=0)
            dc_ref[:, cols] += upd
            ext = ext_ref.at[2 + part]
            ext[pl.ds(0, tm), :] = dy
            ext[pl.ds(tm, 8), :] = carry_ref[j, part]
            carry_ref[j, part] = dy[:8, :]
            d0, d1, d2 = _conv_taps(ext, tm, False)
            cw = cw_ref[...]
            du = (cw[2:3] * d0 + cw[1:2] * d1 + cw[0:1] * d2).astype(BF16)
            du_ref[...] = du
            acc_ref[...] += _nn(du, wut_ref[...])

        @pl.when(j == nj - 1)
        def _():
            dx, dg = _norm_bwd(x_ref[...], g_ref[...], acc_ref[...])
            dx2_ref[...] = dx3_ref[...] + dx
            dg_ref[...] += dg

    return pl.pallas_call(
        body,
        name="ffn_bwd",
        grid=(nt, nj),
        in_specs=[
            pl.BlockSpec((tm, D), lambda i, j: (rev(i), 0)),
            pl.BlockSpec((tm, D), lambda i, j: (rev(i), 0)),
            pl.BlockSpec((1, D), lambda i, j: (0, 0)),
            pl.BlockSpec((tm, fc), lambda i, j: (rev(i), j)),
            pl.BlockSpec((tm, fc), lambda i, j: (rev(i), j)),
            pl.BlockSpec((8, fc), lambda i, j: (jnp.maximum(rev(i) * hb - 1, 0), j)),
            pl.BlockSpec((8, fc), lambda i, j: (jnp.maximum(rev(i) * hb - 1, 0), j)),
            pl.BlockSpec((3, fc), lambda i, j: (0, j)),
            pl.BlockSpec((3, fc), lambda i, j: (0, nj + j)),
            pl.BlockSpec((1, fc), lambda i, j: (0, j)),
            pl.BlockSpec((1, fc), lambda i, j: (0, nj + j)),
            pl.BlockSpec((D, fc), lambda i, j: (0, j)),
            pl.BlockSpec((fc, D), lambda i, j: (j, 0)),
            pl.BlockSpec((fc, D), lambda i, j: (nj + j, 0)),
        ],
        out_specs=[
            pl.BlockSpec((tm, D), lambda i, j: (rev(i), 0)),
            pl.BlockSpec((tm, fc), lambda i, j: (rev(i), j)),
            pl.BlockSpec((tm, fc), lambda i, j: (rev(i), j)),
            pl.BlockSpec((1, D), lambda i, j: (0, 0)),
            pl.BlockSpec((8, D_FF), lambda i, j: (0, 0)),
            pl.BlockSpec((8, D_FF), lambda i, j: (0, 0)),
        ],
        out_shape=[
            jax.ShapeDtypeStruct((T, D), F32),
            jax.ShapeDtypeStruct((T, D_FF), BF16),
            jax.ShapeDtypeStruct((T, D_FF), BF16),
            jax.ShapeDtypeStruct((1, D), F32),
            jax.ShapeDtypeStruct((8, D_FF), F32),
            jax.ShapeDtypeStruct((8, D_FF), F32),
        ],
        scratch_shapes=[
            pltpu.VMEM((tm, D), F32),
            pltpu.VMEM((nj, 2, 8, fc), F32),
            pltpu.VMEM((4, tm + 8, fc), F32),
        ],
        compiler_params=_cparams(("arbitrary", "arbitrary")),
    )(dx3, x2, g3, ug, uv, ug, uv, conv_w, conv_w, conv_b, conv_b, w_down_t, w_up_t, w_up_t)


def _xattn_bwd(dx2, x1, g2, qb, kv, w_mo_t, w_mq_t, tm):
    T, D = x1.shape
    M = kv.shape[0]

    def body(dx2_ref, x_ref, g_ref, q_ref, kv_ref, wot_ref, wqt_ref, dx1_ref, dq_ref, dkv_ref, dg_ref):
        i = pl.program_id(0)

        @pl.when(i == 0)
        def _():
            dkv_ref[...] = jnp.zeros_like(dkv_ref)
            dg_ref[...] = jnp.zeros_like(dg_ref)

        dxv = dx2_ref[...]
        dom = _nn(dxv.astype(BF16), wot_ref[...]).astype(BF16)
        qb_ = q_ref[...]
        kvv = kv_ref[...]
        for hd in range(N_MEM_HEADS):
            sl = slice(hd * MEM_HD, (hd + 1) * MEM_HD)
            vsl = slice(D + hd * MEM_HD, D + (hd + 1) * MEM_HD)
            p = _xattn_probs(qb_, kvv, hd)
            dp = _nt(dom[:, sl], kvv[:, vsl])
            ds = (p * (dp - jnp.sum(p * dp, axis=1, keepdims=True)) * (MEM_HD ** -0.5)).astype(BF16)
            dq_ref[:, sl] = _nn(ds, kvv[:, sl]).astype(BF16)
            dkv_ref[:, sl] += _tn(ds, qb_[:, sl])
            dkv_ref[:, vsl] += _tn(p.astype(BF16), dom[:, sl])
        dh = _nn(dq_ref[...], wqt_ref[...])
        dx, dg = _norm_bwd(x_ref[...], g_ref[...], dh)
        dx1_ref[...] = dxv + dx
        dg_ref[...] += dg

    return pl.pallas_call(
        body,
        name="xattn_bwd",
        grid=(T // tm,),
        in_specs=[
            pl.BlockSpec((tm, D), lambda i: (i, 0)),
            pl.BlockSpec((tm, D), lambda i: (i, 0)),
            pl.BlockSpec((1, D), lambda i: (0, 0)),
            pl.BlockSpec((tm, D), lambda i: (i, 0)),
            pl.BlockSpec((M, 2 * D), lambda i: (0, 0)),
            pl.BlockSpec((D, D), lambda i: (0, 0)),
            pl.BlockSpec((D, D), lambda i: (0, 0)),
        ],
        out_specs=[
            pl.BlockSpec((tm, D), lambda i: (i, 0)),
            pl.BlockSpec((tm, D), lambda i: (i, 0)),
            pl.BlockSpec((M, 2 * D), lambda i: (0, 0)),
            pl.BlockSpec((1, D), lambda i: (0, 0)),
        ],
        out_shape=[
            jax.ShapeDtypeStruct((T, D), F32),
            jax.ShapeDtypeStruct((T, D), BF16),
            jax.ShapeDtypeStruct((M, 2 * D), F32),
            jax.ShapeDtypeStruct((1, D), F32),
        ],
        compiler_params=_cparams(("arbitrary",)),
    )(dx2, x1, g2, qb, kv, w_mo_t, w_mq_t)


def _mem_kv_bwd(mem, gm, mb, dkv, w_mkv_t):
    M, D = mem.shape
    N = dkv.shape[1]

    def body(mem_ref, g_ref, m_ref, dkv_ref, wt_ref, dw_ref, dg_ref):
        dkvb = dkv_ref[...].astype(BF16)
        for n0 in range(0, N, 512):
            dw_ref[:, n0:n0 + 512] = _tn(m_ref[...], dkvb[:, n0:n0 + 512])
        dm = _nn(dkvb, wt_ref[...])
        mv = mem_ref[...]
        dg_ref[...] = jnp.sum(dm * (mv * _rstd(mv)), axis=0, keepdims=True)

    return pl.pallas_call(
        body,
        name="mem_kv_bwd",
        out_shape=[jax.ShapeDtypeStruct((D, N), F32), jax.ShapeDtypeStruct((1, D), F32)],
        compiler_params=_cparams(),
    )(mem, gm, mb, dkv, w_mkv_t)


def _post_attn_bwd(dx1, fox_o, sb_o, gf, gs, w_out_t, tm):
    T, D = dx1.shape

    def body(dx_ref, f_ref, s_ref, gf_ref, gs_ref, wt_ref, df_ref, ds_ref, dgf_ref, dgs_ref):
        i = pl.program_id(0)

        @pl.when(i == 0)
        def _():
            dgf_ref[...] = jnp.zeros_like(dgf_ref)
            dgs_ref[...] = jnp.zeros_like(dgs_ref)

        dmix = _nn(dx_ref[...].astype(BF16), wt_ref[...])
        d, dg = _norm_bwd(f_ref[...], gf_ref[...], dmix[:, :FOX_W])
        df_ref[...] = d
        dgf_ref[...] += dg
        d, dg = _norm_bwd(s_ref[...], gs_ref[...], dmix[:, FOX_W:])
        ds_ref[...] = d
        dgs_ref[...] += dg

    return pl.pallas_call(
        body,
        name="post_attn_bwd",
        grid=(T // tm,),
        in_specs=[
            pl.BlockSpec((tm, D), lambda i: (i, 0)),
            pl.BlockSpec((tm, FOX_W), lambda i: (i, 0)),
            pl.BlockSpec((tm, FOX_W), lambda i: (i, 0)),
            pl.BlockSpec((1, FOX_W), lambda i: (0, 0)),
            pl.BlockSpec((1, FOX_W), lambda i: (0, 0)),
            pl.BlockSpec((D, D), lambda i: (0, 0)),
        ],
        out_specs=[
            pl.BlockSpec((tm, FOX_W), lambda i: (i, 0)),
            pl.BlockSpec((tm, FOX_W), lambda i: (i, 0)),
            pl.BlockSpec((1, FOX_W), lambda i: (0, 0)),
            pl.BlockSpec((1, FOX_W), lambda i: (0, 0)),
        ],
        out_shape=[
            jax.ShapeDtypeStruct((T, FOX_W), F32),
            jax.ShapeDtypeStruct((T, FOX_W), F32),
            jax.ShapeDtypeStruct((1, FOX_W), F32),
            jax.ShapeDtypeStruct((1, FOX_W), F32),
        ],
        compiler_params=_cparams(("arbitrary",)),
    )(dx1, fox_o, sb_o, gf, gs, w_out_t)


def _sb_bwd(proj, ltot, d_o, tq):
    T = proj.shape[0]

    def body(q_ref, k_ref, v_ref, lt_ref, do_ref, dq_ref, dk_ref, dv_ref,
             qh_s, doh_s, z_s, da_s, ab_s, dzb_s, run_s, runw_s, dq_s):
        i = pl.program_id(1)
        n = i + 1

        @pl.when(i == 0)
        def _():
            dk_ref[...] = jnp.zeros_like(dk_ref)
            dv_ref[...] = jnp.zeros_like(dv_ref)

        lane = lax.broadcasted_iota(jnp.int32, (1, 128), 1)
        row = lax.broadcasted_iota(jnp.int32, (tq, tq), 0)
        col = lax.broadcasted_iota(jnp.int32, (tq, tq), 1)
        strict = col < row
        upto = jnp.where(row <= col, 1.0, 0.0).astype(BF16)
        before = jnp.where(row < col, 1.0, 0.0).astype(BF16)
        q = q_ref[...]
        dov = do_ref[...]
        for hh in range(2):
            qh, hmask = _head_q(q, hh, lane)
            qh_s[hh] = qh
            doh_s[hh] = jnp.where(hmask, dov, 0.0).astype(BF16)
        run_s[...] = jnp.zeros_like(run_s)
        runw_s[...] = jnp.zeros_like(runw_s)
        dq_s[...] = jnp.zeros_like(dq_s)

        def rows(t):
            return pl.ds(pl.multiple_of(t * tq, tq), tq)

        def stage_a(t):
            k = k_ref[rows(t), :]
            v = v_ref[rows(t), :]
            for hh in range(2):
                z_s[hh] = _nt(qh_s[hh], k)
                da_s[hh] = _nt(doh_s[hh], v)

        def stage_b(diag):
            for hh in range(2):
                z = z_s[hh]
                L, e = _sb_terms(z, strict if diag else None)
                upto_s = _split_dot(L, upto, 2)
                run = run_s[hh]
                arg = z + L + ((lt_ref[hh] - run) - upto_s)
                if diag:
                    arg = jnp.where(strict, arg, NEG)
                a = jnp.exp(arg)
                w = a * da_s[hh]
                w_before = _split_dot(w, before, 2)
                run_w = runw_s[hh]
                d_keep = w_before + run_w
                r = 1.0 / (1.0 + e)
                beta = jnp.where(z >= 0.0, r, e * r)
                dz = w * (1.0 - beta) - d_keep * beta
                if diag:
                    dz = jnp.where(strict, dz, 0.0)
                dzb_s[hh] = dz.astype(BF16)
                ab_s[hh] = a.astype(BF16)
                run_s[hh] = run + upto_s[:, tq - 1:tq]
                runw_s[hh] = run_w + (w_before[:, tq - 1:tq] + w[:, tq - 1:tq])

        def stage_c(t):
            k = k_ref[rows(t), :]
            dk_blk = None
            dv_blk = None
            for hh in range(2):
                dzb = dzb_s[hh]
                dq_s[hh] += _nn(dzb, k)
                dk_h = _tn(dzb, qh_s[hh])
                dv_h = _tn(ab_s[hh], doh_s[hh])
                dk_blk = dk_h if dk_blk is None else dk_blk + dk_h
                dv_blk = dv_h if dv_blk is None else dv_blk + dv_h
            dk_ref[rows(t), :] += dk_blk
            dv_ref[rows(t), :] += dv_blk

        stage_a(0)

        @pl.when(n == 1)
        def _():
            stage_b(True)

        @pl.when(n >= 2)
        def _():
            stage_b(False)
            stage_a(1)

        def step(t, carry):
            stage_c(t - 2)
            stage_b(False)
            stage_a(t)
            return carry

        lax.fori_loop(2, n, step, 0)

        @pl.when(n >= 2)
        def _():
            stage_c(n - 2)
            stage_b(True)

        stage_c(n - 1)
        dq_ref[...] = (jnp.where(lane < HEAD_DIM, dq_s[0], dq_s[1]) * (HEAD_DIM ** -0.5)).astype(BF16)

    return pl.pallas_call(
        body,
        name="sb_bwd",
        grid=(4, T // tq),
        in_specs=[
            pl.BlockSpec((tq, 128), lambda p, i: (i, 12 + p)),
            pl.BlockSpec((T, 128), lambda p, i: (0, 16 + p)),
            pl.BlockSpec((T, 128), lambda p, i: (0, 20 + p)),
            pl.BlockSpec((2, tq, 1), lambda p, i: (p, i, 0)),
            pl.BlockSpec((tq, 128), lambda p, i: (i, p)),
        ],
        out_specs=[
            pl.BlockSpec((tq, 128), lambda p, i: (i, p)),
            pl.BlockSpec((T, 128), lambda p, i: (0, p)),
            pl.BlockSpec((T, 128), lambda p, i: (0, p)),
        ],
        out_shape=[
            jax.ShapeDtypeStruct((T, FOX_W), BF16),
            jax.ShapeDtypeStruct((T, FOX_W), F32),
            jax.ShapeDtypeStruct((T, FOX_W), F32),
        ],
        scratch_shapes=[
            pltpu.VMEM((2, tq, 128), BF16),
            pltpu.VMEM((2, tq, 128), BF16),
            pltpu.VMEM((2, tq, tq), F32),
            pltpu.VMEM((2, tq, tq), F32),
            pltpu.VMEM((2, tq, tq), BF16),
            pltpu.VMEM((2, tq, tq), BF16),
            pltpu.VMEM((2, tq, 1), F32),
            pltpu.VMEM((2, tq, 1), F32),
            pltpu.VMEM((2, tq, 128), F32),
        ],
        compiler_params=_cparams(("arbitrary", "arbitrary")),
    )(proj, proj, proj, ltot, d_o)


def _fox_bwd(proj, c_col, c_row, lse, d_o, o, tq):
    T = proj.shape[0]

    def body(q_ref, k_ref, v_ref, cq_ref, ck_ref, lse_ref, do_ref, o_ref,
             dq_ref, dk_ref, dv_ref, dck_ref, dcq_ref,
             qh_s, doh_s, delta_s, shift_s, z_s, dp_s, pb_s, dsb_s, rs_s, dq_s):
        i = pl.program_id(1)
        n = i + 1

        @pl.when(i == 0)
        def _():
            dk_ref[...] = jnp.zeros_like(dk_ref)
            dv_ref[...] = jnp.zeros_like(dv_ref)
            dck_ref[...] = jnp.zeros_like(dck_ref)

        lane = lax.broadcasted_iota(jnp.int32, (1, 128), 1)
        row = lax.broadcasted_iota(jnp.int32, (tq, tq), 0)
        col = lax.broadcasted_iota(jnp.int32, (tq, tq), 1)
        q = q_ref[...]
        dov = do_ref[...]
        ov = o_ref[...]
        for hh in range(2):
            qh, hmask = _head_q(q, hh, lane)
            dohb = jnp.where(hmask, dov, 0.0).astype(BF16)
            qh_s[hh] = qh
            doh_s[hh] = dohb
            delta_s[hh] = jnp.sum(dohb.astype(F32) * ov, axis=1, keepdims=True)
            shift_s[hh] = cq_ref[hh] - lse_ref[hh]
        rs_s[...] = jnp.zeros_like(rs_s)
        dq_s[...] = jnp.zeros_like(dq_s)

        def rows(t):
            return pl.ds(pl.multiple_of((i - t) * tq, tq), tq)

        def stage_a(t):
            k = k_ref[rows(t), :]
            v = v_ref[rows(t), :]
            for hh in range(2):
                z_s[hh] = _nt(qh_s[hh], k)
                dp_s[hh] = _nt(doh_s[hh], v)

        def stage_b(t, diag):
            for hh in range(2):
                s = z_s[hh] + shift_s[hh] - ck_ref[hh, :, rows(t)]
                if diag:
                    s = jnp.where(col <= row, s, NEG)
                p = jnp.exp(s)
                ds = p * (dp_s[hh] - delta_s[hh])
                pb_s[hh] = p.astype(BF16)
                dsb_s[hh] = ds.astype(BF16)
                dck_ref[hh, :, rows(t)] += jnp.sum(ds, axis=0, keepdims=True)
                rs_s[hh] += jnp.sum(ds, axis=1, keepdims=True)

        def stage_c(t):
            k = k_ref[rows(t), :]
            dk_blk = None
            dv_blk = None
            for hh in range(2):
                dsb = dsb_s[hh]
                dq_s[hh] += _nn(dsb, k)
                dk_h = _tn(dsb, qh_s[hh])
                dv_h = _tn(pb_s[hh], doh_s[hh])
                dk_blk = dk_h if dk_blk is None else dk_blk + dk_h
                dv_blk = dv_h if dv_blk is None else dv_blk + dv_h
            dk_ref[rows(t), :] += dk_blk
            dv_ref[rows(t), :] += dv_blk

        stage_a(0)
        stage_b(0, True)

        @pl.when(n >= 2)
        def _():
            stage_a(1)

        def step(t, carry):
            stage_c(t - 2)
            stage_b(t - 1, False)
            stage_a(t)
            return carry

        lax.fori_loop(2, n, step, 0)

        @pl.when(n >= 2)
        def _():
            stage_c(n - 2)
            stage_b(n - 1, False)

        stage_c(n - 1)
        dcq_ref[0] = rs_s[0]
        dcq_ref[1] = rs_s[1]
        dq_ref[...] = (jnp.where(lane < HEAD_DIM, dq_s[0], dq_s[1]) * (HEAD_DIM ** -0.5)).astype(BF16)

    return pl.pallas_call(
        body,
        name="fox_bwd",
        grid=(4, T // tq),
        in_specs=[
            pl.BlockSpec((tq, 128), lambda p, i: (i, p)),
            pl.BlockSpec((T, 128), lambda p, i: (0, 4 + p)),
            pl.BlockSpec((T, 128), lambda p, i: (0, 8 + p)),
            pl.BlockSpec((2, tq, 1), lambda p, i: (p, i, 0)),
            pl.BlockSpec((2, 1, T), lambda p, i: (p, 0, 0)),
            pl.BlockSpec((2, tq, 1), lambda p, i: (p, i, 0)),
            pl.BlockSpec((tq, 128), lambda p, i: (i, p)),
            pl.BlockSpec((tq, 128), lambda p, i: (i, p)),
        ],
        out_specs=[
            pl.BlockSpec((tq, 128), lambda p, i: (i, p)),
            pl.BlockSpec((T, 128), lambda p, i: (0, p)),
            pl.BlockSpec((T, 128), lambda p, i: (0, p)),
            pl.BlockSpec((2, 1, T), lambda p, i: (p, 0, 0)),
            pl.BlockSpec((2, tq, 1), lambda p, i: (p, i, 0)),
        ],
        out_shape=[
            jax.ShapeDtypeStruct((T, FOX_W), BF16),
            jax.ShapeDtypeStruct((T, FOX_W), F32),
            jax.ShapeDtypeStruct((T, FOX_W), F32),
            jax.ShapeDtypeStruct((N_FOX, 1, T), F32),
            jax.ShapeDtypeStruct((N_FOX, T, 1), F32),
        ],
        scratch_shapes=[
            pltpu.VMEM((2, tq, 128), BF16),
            pltpu.VMEM((2, tq, 128), BF16),
            pltpu.VMEM((2, tq, 1), F32),
            pltpu.VMEM((2, tq, 1), F32),
            pltpu.VMEM((2, tq, tq), F32),
            pltpu.VMEM((2, tq, tq), F32),
            pltpu.VMEM((2, tq, tq), BF16),
            pltpu.VMEM((2, tq, tq), BF16),
            pltpu.VMEM((2, tq, 1), F32),
            pltpu.VMEM((2, tq, 128), F32),
        ],
        compiler_params=_cparams(("arbitrary", "arbitrary")),
    )(proj, proj, proj, c_col, c_row, lse, d_o, o)


def _forget_bwd(dcq, dck, xf, tc):
    H, T = xf.shape
    nc = T // tc

    def body(dcq_ref, dck_ref, xf_ref, dxf_ref, db_ref):
        row = lax.broadcasted_iota(jnp.int32, (tc, tc), 0)
        col = lax.broadcasted_iota(jnp.int32, (tc, tc), 1)
        from_here = jnp.where(row >= col, 1.0, 0.0).astype(BF16)

        def chunk(n, carry):
            run, db = carry
            cs = pl.multiple_of((nc - 1 - n) * tc, tc)
            dc = dcq_ref[:, pl.ds(cs, tc)] - dck_ref[:, pl.ds(cs, tc)]
            dlogf = _split_dot(dc, from_here, 3) + run
            xfv = xf_ref[:, pl.ds(cs, tc)]
            dxf = dlogf * jax.nn.sigmoid(-xfv)
            dxf_ref[:, pl.ds(cs, tc)] = dxf
            return dlogf[:, 0:1], db + jnp.sum(dxf, axis=1, keepdims=True)

        _, db = lax.fori_loop(0, nc, chunk, (jnp.zeros((H, 1), F32), jnp.zeros((H, 1), F32)))
        db_ref[...] = db

    return pl.pallas_call(
        body,
        name="forget_bwd",
        out_shape=[jax.ShapeDtypeStruct((H, T), F32), jax.ShapeDtypeStruct((H, 1), F32)],
        compiler_params=_cparams(),
    )(dcq, dck, xf)


def _inproj_bwd(dproj, w_in_t, x, g1, dx1, tm):
    T, D = x.shape
    N = dproj.shape[1]

    def body(dp_ref, wt_ref, x_ref, g_ref, dx1_ref, dx_ref, dg_ref):
        i = pl.program_id(0)

        @pl.when(i == 0)
        def _():
            dg_ref[...] = jnp.zeros_like(dg_ref)

        dh = _nn(dp_ref[...], wt_ref[...])
        dx, dg = _norm_bwd(x_ref[...], g_ref[...], dh)
        dx_ref[...] = dx1_ref[...] + dx
        dg_ref[...] += dg

    return pl.pallas_call(
        body,
        name="inproj_bwd",
        grid=(T // tm,),
        in_specs=[
            pl.BlockSpec((tm, N), lambda i: (i, 0)),
            pl.BlockSpec((N, D), lambda i: (0, 0)),
            pl.BlockSpec((tm, D), lambda i: (i, 0)),
            pl.BlockSpec((1, D), lambda i: (0, 0)),
            pl.BlockSpec((tm, D), lambda i: (i, 0)),
        ],
        out_specs=[
            pl.BlockSpec((tm, D), lambda i: (i, 0)),
            pl.BlockSpec((1, D), lambda i: (0, 0)),
        ],
        out_shape=[jax.ShapeDtypeStruct((T, D), F32), jax.ShapeDtypeStruct((1, D), F32)],
        compiler_params=_cparams(("arbitrary",)),
    )(dproj, w_in_t, x, g1, dx1)


def _matmul_tn(a, b, name, cast_b=False):
    T, K = a.shape
    N = b.shape[1]
    bt = min(T, 512)
    bk = _tile_div(K, 1536)
    bn = _tile_div(N, 1536)

    def body(a_ref, b_ref, o_ref):
        @pl.when(pl.program_id(2) == 0)
        def _():
            o_ref[...] = jnp.zeros_like(o_ref)

        bv = b_ref[...]
        if cast_b:
            bv = bv.astype(BF16)
        o_ref[...] += _tn(a_ref[...], bv)

    return pl.pallas_call(
        body,
        name=name,
        grid=(K // bk, N // bn, T // bt),
        in_specs=[
            pl.BlockSpec((bt, bk), lambda k, n, t: (t, k)),
            pl.BlockSpec((bt, bn), lambda k, n, t: (t, n)),
        ],
        out_specs=pl.BlockSpec((bk, bn), lambda k, n, t: (k, n)),
        out_shape=jax.ShapeDtypeStruct((K, N), F32),
        compiler_params=_cparams(("arbitrary", "arbitrary", "arbitrary")),
    )(a, b)


def _local_step(x, mem, target, p, tm, tq):
    T, D = x.shape
    w_in = p["w_in"]
    w_qkv = w_in[:, :QKV_W]
    w_f_t = w_in[:, QKV_W:].T
    w_in_t = jnp.pad(w_in, ((0, 0), (0, IN_PAD - w_in.shape[1]))).T
    b_f = p["b_forget"].reshape(N_FOX, 1)

    proj, h1, xf, c = _inproj_fwd(x, p["attn_norm_g"], w_qkv, w_f_t, b_f, tm)
    c_col = c.reshape(N_FOX, T, 1)
    c_row = c.reshape(N_FOX, 1, T)
    fox_o, lse = _fox_fwd(proj, c_col, c_row, tq)
    sb_o, sb_ltot = _sb_fwd(proj, tq)
    x1, mixed = _post_attn_fwd(fox_o, sb_o, p["fox_out_g"], p["sb_out_g"], p["w_out"], x, tm)
    mb, kv = _mem_kv_fwd(mem, p["mem_norm_g"], p["w_mkv"])
    x2, h2, qb, om = _xattn_fwd(x1, p["xattn_norm_g"], p["w_mq"], kv, p["w_mo"], tm)
    x3, h3, ug, uv, a = _ffn_fwd(x2, p["ffn_norm_g"], p["w_up"], p["conv_w"], p["conv_b"], p["w_down"], tm)
    dx3, loss_blk, d_final_g = _loss_head(x3, p["final_norm_g"], target, tm)

    g = {"final_norm_g": d_final_g}
    dx2, du_g, du_v, g["ffn_norm_g"], dc_g, dc_v = _ffn_bwd(
        dx3, x2, p["ffn_norm_g"], ug, uv, p["conv_w"], p["conv_b"], p["w_down"].T, p["w_up"].T, tm)
    g["w_down"] = _matmul_tn(a, dx3, "dw_down", cast_b=True)
    g["w_up"] = jnp.concatenate([_matmul_tn(h3, du_g, "dw_up_gate"), _matmul_tn(h3, du_v, "dw_up_val")], axis=1)
    dconv = jnp.concatenate([dc_g, dc_v], axis=1)
    g["conv_w"] = dconv[0:3]
    g["conv_b"] = dconv[3:4]
    dx1, dq_m, dkv, g["xattn_norm_g"] = _xattn_bwd(dx2, x1, p["xattn_norm_g"], qb, kv, p["w_mo"].T, p["w_mq"].T, tm)
    g["w_mo"] = _matmul_tn(om, dx2, "dw_mo", cast_b=True)
    g["w_mq"] = _matmul_tn(h2, dq_m, "dw_mq")
    g["w_mkv"], g["mem_norm_g"] = _mem_kv_bwd(mem, p["mem_norm_g"], mb, dkv, p["w_mkv"].T)
    d_fox, d_sb, g["fox_out_g"], g["sb_out_g"] = _post_attn_bwd(
        dx1, fox_o, sb_o, p["fox_out_g"], p["sb_out_g"], p["w_out"].T, tm)
    g["w_out"] = _matmul_tn(mixed, dx1, "dw_out", cast_b=True)
    dq_s, dk_s, dv_s = _sb_bwd(proj, sb_ltot, d_sb, tq)
    dq_f, dk_f, dv_f, dck, dcq = _fox_bwd(proj, c_col, c_row, lse, d_fox, fox_o, tq)
    dxf, db = _forget_bwd(dcq.reshape(N_FOX, T), dck.reshape(N_FOX, T), xf, min(T, 512))
    g["b_forget"] = db.reshape(1, N_FOX)
    dproj = jnp.concatenate([
        dq_f, dk_f.astype(BF16), dv_f.astype(BF16), dq_s, dk_s.astype(BF16), dv_s.astype(BF16),
        jnp.pad(dxf.T, ((0, 0), (0, IN_PAD - QKV_W - N_FOX))).astype(BF16)], axis=1)
    grad_x, g["attn_norm_g"] = _inproj_bwd(dproj, w_in_t, x, p["attn_norm_g"], dx1, tm)
    g["w_in"] = _matmul_tn(h1, dproj, "dw_in")[:, :w_in.shape[1]]
    return loss_blk, grad_x, g


def _mesh_pos():
    return lax.axis_index("x"), lax.axis_index("y"), lax.axis_index("c")


def _flip(pos, k):
    return tuple(1 - v if (k >> b) & 1 else v for v, b in zip(pos, (2, 1, 0)))


def _slot(pos):
    return 4 * pos[0] + 2 * pos[1] + pos[2]


def _all_gather(shards, name):
    n = len(shards)

    def body(*refs):
        ins, outs = refs[:n], refs[n:2 * n]
        send_sems, recv_sems, local_sems = refs[2 * n:]
        me = _mesh_pos()
        sibling = _flip(me, 1)
        chips = [4, 2, 6]

        def copy(a, kk, block, to, src=None):
            rows = outs[a].at[_slot(block)]
            return pltpu.make_async_remote_copy(
                src_ref=rows if src is None else src, dst_ref=rows,
                send_sem=send_sems.at[7 * a + kk], recv_sem=recv_sems.at[7 * a + kk],
                device_id=to, device_id_type=MESH)

        mine = [pltpu.make_async_copy(ins[a], outs[a].at[_slot(me)], local_sems.at[a]) for a in range(n)]
        for cp in mine:
            cp.start()
        first = []
        for a in range(n):
            first.append(copy(a, 0, me, sibling, src=ins[a]))
            first += [copy(a, 1 + j, me, _flip(me, k), src=ins[a]) for j, k in enumerate(chips)]
        for cp in first:
            cp.start()
        passed = []
        for j, k in enumerate(chips):
            for a in range(n):
                copy(a, 1 + j, _flip(me, k), me).wait_recv()
                fwd = copy(a, 4 + j, _flip(me, k), sibling)
                fwd.start()
                passed.append(fwd)
        for a in range(n):
            copy(a, 0, sibling, me).wait_recv()
            for j, k in enumerate(chips):
                copy(a, 4 + j, _flip(sibling, k), me).wait_recv()
        for cp in first + passed:
            cp.wait_send()
        for cp in mine:
            cp.wait()

    any_spec = pl.BlockSpec(memory_space=pl.ANY)
    return pl.pallas_call(
        body,
        name=name,
        in_specs=[any_spec] * n,
        out_specs=[any_spec] * n,
        out_shape=[jax.ShapeDtypeStruct((N_DEV,) + s.shape, s.dtype) for s in shards],
        scratch_shapes=[
            pltpu.SemaphoreType.DMA((7 * n,)),
            pltpu.SemaphoreType.DMA((7 * n,)),
            pltpu.SemaphoreType.DMA((n,)),
        ],
    )(*shards)


def _all_to_all(blocks, name):
    n = len(blocks)

    def body(*refs):
        ins, outs = refs[:n], refs[n:2 * n]
        send_sems, recv_sems, local_sems = refs[2 * n:]
        me = _mesh_pos()
        mine = [pltpu.make_async_copy(ins[a].at[_slot(me)], outs[a].at[_slot(me)], local_sems.at[a])
                for a in range(n)]
        for cp in mine:
            cp.start()

        def copy(a, k):
            peer = _flip(me, k)
            return pltpu.make_async_remote_copy(
                src_ref=ins[a].at[_slot(peer)], dst_ref=outs[a].at[_slot(me)],
                send_sem=send_sems.at[7 * a + k - 1], recv_sem=recv_sems.at[7 * a + k - 1],
                device_id=peer, device_id_type=MESH)

        def landed(a, k):
            peer = _flip(me, k)
            return pltpu.make_async_remote_copy(
                src_ref=ins[a].at[_slot(peer)], dst_ref=outs[a].at[_slot(peer)],
                send_sem=send_sems.at[7 * a + k - 1], recv_sem=recv_sems.at[7 * a + k - 1],
                device_id=peer, device_id_type=MESH)

        sent = [copy(a, k) for k in range(1, 8) for a in range(n)]
        for cp in sent:
            cp.start()
        for k in range(1, 8):
            for a in range(n):
                landed(a, k).wait_recv()
        for cp in sent:
            cp.wait_send()
        for cp in mine:
            cp.wait()

    any_spec = pl.BlockSpec(memory_space=pl.ANY)
    return pl.pallas_call(
        body,
        name=name,
        in_specs=[any_spec] * n,
        out_specs=[any_spec] * n,
        out_shape=[jax.ShapeDtypeStruct(b.shape, b.dtype) for b in blocks],
        scratch_shapes=[
            pltpu.SemaphoreType.DMA((7 * n,)),
            pltpu.SemaphoreType.DMA((7 * n,)),
            pltpu.SemaphoreType.DMA((n,)),
        ],
    )(*blocks)


def _adamw_math(w, g, m, v):
    m2 = ADAM_B1 * m + (1.0 - ADAM_B1) * g
    v2 = ADAM_B2 * v + (1.0 - ADAM_B2) * (g * g)
    m_hat = m2 / (1.0 - ADAM_B1 ** ADAM_STEP)
    v_hat = v2 / (1.0 - ADAM_B2 ** ADAM_STEP)
    delta = -ADAM_LR * (m_hat / (jnp.sqrt(v_hat) + ADAM_EPS) + ADAM_WD * w)
    return delta, m2, v2


def _adamw(w, parts, m, v, name):
    R, C = w.shape
    br = 128 if R % 128 == 0 else R

    def body(w_ref, p_ref, m_ref, v_ref, g_ref, d_ref, nm_ref, nv_ref):
        g = p_ref[0]
        for s in range(1, N_DEV):
            g = g + p_ref[s]
        g_ref[...] = g
        d_ref[...], nm_ref[...], nv_ref[...] = _adamw_math(w_ref[...], g, m_ref[...], v_ref[...])

    spec = pl.BlockSpec((br, C), lambda i: (i, 0))
    return pl.pallas_call(
        body,
        name=name,
        grid=(R // br,),
        in_specs=[spec, pl.BlockSpec((N_DEV, br, C), lambda i: (0, i, 0)), spec, spec],
        out_specs=[spec] * 4,
        out_shape=[jax.ShapeDtypeStruct((R, C), F32)] * 4,
        compiler_params=_cparams(("arbitrary",)),
    )(w, parts, m, v)


_SHARDED = ("w_in", "w_out", "w_mq", "w_mkv", "w_mo", "w_up", "conv_w", "w_down")
_COL_SHARDED = ("w_in", "w_mkv", "w_up", "conv_w")
_REPLICATED = ("attn_norm_g", "b_forget", "fox_out_g", "sb_out_g", "xattn_norm_g", "mem_norm_g",
               "ffn_norm_g", "conv_b", "final_norm_g")
_WEIGHTS = ("attn_norm_g", "w_in", "b_forget", "fox_out_g", "sb_out_g", "w_out", "xattn_norm_g", "mem_norm_g",
            "w_mq", "w_mkv", "w_mo", "ffn_norm_g", "w_up", "conv_w", "conv_b", "w_down", "final_norm_g")


def _pack_rows(n):
    return -(-n // 128)


def _pack(vals, rows_total):
    parts = []
    for v in vals:
        flat = v.reshape(-1)
        parts.append(jnp.pad(flat, (0, _pack_rows(flat.shape[0]) * 128 - flat.shape[0])))
    flat = jnp.concatenate(parts)
    return jnp.pad(flat, (0, rows_total * 128 - flat.shape[0])).reshape(rows_total, 128)


def _unpack(packed, shapes):
    out = []
    r = 0
    for shp in shapes:
        n = 1
        for d in shp:
            n *= d
        out.append(packed[r:r + _pack_rows(n)].reshape(-1)[:n].reshape(shp))
        r += _pack_rows(n)
    return out


def _gathered_full(name, gathered):
    if name in _COL_SHARDED:
        return jnp.transpose(gathered, (1, 0, 2)).reshape(gathered.shape[1], -1)
    return gathered.reshape(-1, gathered.shape[2])


def _to_blocks(name, full):
    if name in _COL_SHARDED:
        r = full.shape[0]
        return jnp.transpose(full.reshape(r, N_DEV, -1), (1, 0, 2))
    return full.reshape(N_DEV, -1, full.shape[1])


def _step(args, tm, tq):
    w = {n: args[n] for n in _WEIGHTS}
    mom = {n: args["m_" + n] for n in _WEIGHTS}
    var = {n: args["v_" + n] for n in _WEIGHTS}
    x = args["x"][0]
    mem = args["mem"][0]
    target = args["loss_target"][0]

    def flat2(a):
        return a.reshape(a.shape[-2], a.shape[-1]) if a.ndim == 3 else a.reshape(1, -1)

    shards = [flat2(w[n]) if n == "conv_w" else flat2(w[n]).astype(BF16) for n in _SHARDED]
    gathered = _all_gather(shards, "gather_weights")
    p = {n: _gathered_full(n, gv) for n, gv in zip(_SHARDED, gathered)}
    for n in _REPLICATED:
        p[n] = flat2(w[n])

    loss_blk, grad_x, g = _local_step(x, mem, target, p, tm, tq)

    parts = _all_to_all([_to_blocks(n, g[n]) for n in _SHARDED], "scatter_grads")
    out = {}
    for n, pr in zip(_SHARDED, parts):
        res = _adamw(flat2(w[n]), pr, flat2(mom[n]), flat2(var[n]), "adamw_" + n)
        out[n] = [r.reshape(w[n].shape) for r in res]

    shapes = [w[n].shape for n in _REPLICATED]
    rows = sum(_pack_rows(flat2(w[n]).shape[1]) for n in _REPLICATED) + 1
    rows = -(-rows // 8) * 8
    g_pack = _pack([g[n] for n in _REPLICATED] + [loss_blk[0:1, :]], rows)
    (g_all,) = _all_gather([g_pack], "gather_small")
    res = _adamw(_pack([w[n] for n in _REPLICATED], rows), g_all,
                 _pack([mom[n] for n in _REPLICATED], rows), _pack([var[n] for n in _REPLICATED], rows),
                 "adamw_small")
    n_rows_params = sum(_pack_rows(flat2(w[n]).shape[1]) for n in _REPLICATED)
    loss = res[0][n_rows_params, 0]
    unpacked = [_unpack(r, shapes) for r in res]
    for k, n in enumerate(_REPLICATED):
        out[n] = [unpacked[q][k] for q in range(4)]

    grads = [out[n][0] for n in _WEIGHTS]
    deltas = [out[n][1] for n in _WEIGHTS]
    new_m = [out[n][2] for n in _WEIGHTS]
    new_v = [out[n][3] for n in _WEIGHTS]
    return (loss, grad_x[None], *grads, *deltas, *new_m, *new_v)


def kernel(x, mem, attn_norm_g, w_in, b_forget, fox_out_g, sb_out_g, w_out, xattn_norm_g, mem_norm_g, w_mq, w_mkv, w_mo, ffn_norm_g, w_up, conv_w, conv_b, w_down, final_norm_g, loss_target, m_attn_norm_g, m_w_in, m_b_forget, m_fox_out_g, m_sb_out_g, m_w_out, m_xattn_norm_g, m_mem_norm_g, m_w_mq, m_w_mkv, m_w_mo, m_ffn_norm_g, m_w_up, m_conv_w, m_conv_b, m_w_down, m_final_norm_g, v_attn_norm_g, v_w_in, v_b_forget, v_fox_out_g, v_sb_out_g, v_w_out, v_xattn_norm_g, v_mem_norm_g, v_w_mq, v_w_mkv, v_w_mo, v_ffn_norm_g, v_w_up, v_conv_w, v_conv_b, v_w_down, v_final_norm_g):
    args = dict(locals())
    T = x.shape[1]
    return _step(args, tm=min(T, 512), tq=min(T, 256))
```

```python
import functools

import jax
import jax.numpy as jnp
from jax import lax
from jax.experimental import pallas as pl
from jax.experimental.pallas import tpu as pltpu

F32 = jnp.float32
BF16 = jnp.bfloat16
EPS = 1e-6
NEG = -1e30

HEAD_DIM = 64
N_FOX = 8
FOX_W = 512
QKV_W = 3072
IN_PAD = 3200
N_MEM_HEADS = 4
MEM_HD = 256
D_FF = 2816
FF_CHUNK = 256
N_DEV = 8

ADAM_LR = 0.001
ADAM_B1 = 0.9
ADAM_B2 = 0.999
ADAM_EPS = 1e-08
ADAM_WD = 0.01
ADAM_STEP = 10

VMEM_LIMIT = 56 * 1024 * 1024
MESH = pl.DeviceIdType.MESH


def _cparams(sem=None):
    return pltpu.CompilerParams(dimension_semantics=sem, vmem_limit_bytes=VMEM_LIMIT)


def _nt(a, b):
    return lax.dot_general(a, b, (((1,), (1,)), ((), ())), preferred_element_type=F32)


def _tn(a, b):
    return lax.dot_general(a, b, (((0,), (0,)), ((), ())), preferred_element_type=F32)


def _nn(a, b):
    return jnp.dot(a, b, preferred_element_type=F32)


def _split_dot(a, m01, terms):
    out = None
    r = a
    for t in range(terms):
        p = r.astype(BF16)
        d = _nn(p, m01)
        out = d if out is None else out + d
        if t + 1 < terms:
            r = r - p.astype(F32)
    return out


def _rstd(xv):
    return lax.rsqrt(jnp.mean(xv * xv, axis=-1, keepdims=True) + EPS)


def _norm_bwd(xv, g, dh):
    r = _rstd(xv)
    xhat = xv * r
    dxhat = dh * g
    dx = r * (dxhat - xhat * jnp.mean(dxhat * xhat, axis=-1, keepdims=True))
    dg = jnp.sum(dh * xhat, axis=0, keepdims=True)
    return dx, dg


def _tile_div(n, cap):
    best = None
    for d in range(128, min(n, cap) + 1, 128):
        if n % d == 0:
            best = d
    assert best is not None, n
    return best


def _inproj_fwd(x, g1, w_qkv, w_f_t, b_f, tm):
    T, D = x.shape
    N = w_qkv.shape[1]
    H = w_f_t.shape[0]

    def body(x_ref, g_ref, w_ref, wf_ref, b_ref, proj_ref, h_ref, xf_ref, c_ref, carry_ref):
        i = pl.program_id(0)

        @pl.when(i == 0)
        def _():
            carry_ref[...] = jnp.zeros_like(carry_ref)

        xv = x_ref[...]
        h = (xv * _rstd(xv) * g_ref[...]).astype(BF16)
        h_ref[...] = h
        for n0 in range(0, N, 512):
            proj_ref[:, n0:n0 + 512] = _nn(h, w_ref[:, n0:n0 + 512]).astype(BF16)
        xf = _nt(wf_ref[...], h) + b_ref[...]
        xf_ref[...] = xf
        logf = jnp.minimum(xf, 0.0) - jnp.log1p(jnp.exp(-jnp.abs(xf)))
        row = lax.broadcasted_iota(jnp.int32, (tm, tm), 0)
        col = lax.broadcasted_iota(jnp.int32, (tm, tm), 1)
        upper = jnp.where(row <= col, 1.0, 0.0).astype(BF16)
        c = _split_dot(logf, upper, 3) + carry_ref[...]
        c_ref[...] = c
        carry_ref[...] = c[:, tm - 1:tm]

    return pl.pallas_call(
        body,
        name="inproj_fwd",
        grid=(T // tm,),
        in_specs=[
            pl.BlockSpec((tm, D), lambda i: (i, 0)),
            pl.BlockSpec((1, D), lambda i: (0, 0)),
            pl.BlockSpec((D, N), lambda i: (0, 0)),
            pl.BlockSpec((H, D), lambda i: (0, 0)),
            pl.BlockSpec((H, 1), lambda i: (0, 0)),
        ],
        out_specs=[
            pl.BlockSpec((tm, N), lambda i: (i, 0)),
            pl.BlockSpec((tm, D), lambda i: (i, 0)),
            pl.BlockSpec((H, tm), lambda i: (0, i)),
            pl.BlockSpec((H, tm), lambda i: (0, i)),
        ],
        out_shape=[
            jax.ShapeDtypeStruct((T, N), BF16),
            jax.ShapeDtypeStruct((T, D), BF16),
            jax.ShapeDtypeStruct((H, T), F32),
            jax.ShapeDtypeStruct((H, T), F32),
        ],
        scratch_shapes=[pltpu.VMEM((H, 1), F32)],
        compiler_params=_cparams(("arbitrary",)),
    )(x, g1, w_qkv, w_f_t, b_f)


def _head_q(q, hh, lane):
    hmask = (lane >= HEAD_DIM * hh) & (lane < HEAD_DIM * (hh + 1))
    qh = jnp.where(hmask, q.astype(F32), 0.0) * (HEAD_DIM ** -0.5)
    return qh.astype(BF16), hmask


def _pipeline3(n, stage_a, stage_b, stage_c, diag_last):
    stage_a(0, 0)
    if diag_last:
        @pl.when(n == 1)
        def _():
            stage_b(0, 0, True)

        @pl.when(n >= 2)
        def _():
            stage_b(0, 0, False)
    else:
        stage_b(0, 0, True)

    @pl.when(n >= 2)
    def _():
        stage_a(1, 1)

    def pair(m, carry):
        t = 2 + 2 * m
        stage_c(t - 2)
        stage_b(t - 1, 1, False)
        stage_a(t, 0)
        stage_c(t - 1)
        stage_b(t, 0, False)
        stage_a(t + 1, 1)
        return carry

    lax.fori_loop(0, (n - 2) // 2, pair, 0)
    odd = n % 2 == 1

    @pl.when((n >= 3) & odd)
    def _():
        stage_c(n - 3)
        stage_b(n - 2, 1, False)
        stage_a(n - 1, 0)

    @pl.when((n >= 2) & odd)
    def _():
        stage_c(n - 2)
        stage_b(n - 1, 0, diag_last)

    @pl.when((n >= 2) & jnp.logical_not(odd))
    def _():
        stage_c(n - 2)
        stage_b(n - 1, 1, diag_last)

    stage_c(n - 1)


def _lanes2(x):
    return jnp.concatenate([x, x], axis=1)


def _fox_fwd(proj, c_col, c_row, tq):
    T = proj.shape[0]
    assert tq == 256

    def body(q_ref, k_ref, v_ref, cq_ref, ck_ref, o_ref, lse_ref, qh_s, cq_s, z_s, p_s, al_s, m_s, l_s, acc_s):
        i = pl.program_id(1)
        lane = lax.broadcasted_iota(jnp.int32, (1, 128), 1)
        row = lax.broadcasted_iota(jnp.int32, (tq, tq), 0)
        col = lax.broadcasted_iota(jnp.int32, (tq, tq), 1)
        ones = jnp.ones((tq, 128), BF16)
        q = q_ref[...]
        for hh in range(2):
            qh_s[hh] = _head_q(q, hh, lane)[0]
            cq_s[hh] = jnp.broadcast_to(cq_ref[hh], (tq, tq))
        m_s[...] = jnp.full(m_s.shape, NEG, F32)
        l_s[...] = jnp.zeros_like(l_s)
        acc_s[...] = jnp.zeros_like(acc_s)

        def rows(t):
            return pl.ds(pl.multiple_of((i - t) * tq, tq), tq)

        def stage_a(t, slot):
            k = k_ref[rows(t), :]
            for hh in range(2):
                z_s[slot, hh] = _nt(qh_s[hh], k)

        def stage_b(t, slot, diag):
            for hh in range(2):
                s = z_s[slot, hh] + cq_s[hh] - ck_ref[hh, :, rows(t)]
                if diag:
                    s = jnp.where(col <= row, s, NEG)
                m = m_s[hh]
                half = jnp.maximum(s[:, :128], s[:, 128:])
                m_new = jnp.maximum(m, jnp.max(half, axis=1, keepdims=True))
                alpha = jnp.exp(m - m_new)
                p = jnp.exp(s - _lanes2(m_new)).astype(BF16)
                l_s[hh] = alpha * l_s[hh] + _nn(p, ones)
                m_s[hh] = m_new
                al_s[hh] = alpha
                p_s[hh] = p

        def stage_c(t):
            v = v_ref[rows(t), :]
            for hh in range(2):
                acc_s[hh] = al_s[hh] * acc_s[hh] + _nn(p_s[hh], v)

        _pipeline3(i + 1, stage_a, stage_b, stage_c, False)
        l0, l1 = l_s[0], l_s[1]
        o_ref[...] = jnp.where(lane < HEAD_DIM, acc_s[0] / l0, acc_s[1] / l1)
        lse_ref[0] = (m_s[0] + jnp.log(l0))[:, 0:1]
        lse_ref[1] = (m_s[1] + jnp.log(l1))[:, 0:1]

    return pl.pallas_call(
        body,
        name="fox_fwd",
        grid=(4, T // tq),
        in_specs=[
            pl.BlockSpec((tq, 128), lambda p, i: (i, p)),
            pl.BlockSpec((T, 128), lambda p, i: (0, 4 + p)),
            pl.BlockSpec((T, 128), lambda p, i: (0, 8 + p)),
            pl.BlockSpec((2, tq, 1), lambda p, i: (p, i, 0)),
            pl.BlockSpec((2, 1, T), lambda p, i: (p, 0, 0)),
        ],
        out_specs=[
            pl.BlockSpec((tq, 128), lambda p, i: (i, p)),
            pl.BlockSpec((2, tq, 1), lambda p, i: (p, i, 0)),
        ],
        out_shape=[
            jax.ShapeDtypeStruct((T, FOX_W), F32),
            jax.ShapeDtypeStruct((N_FOX, T, 1), F32),
        ],
        scratch_shapes=[
            pltpu.VMEM((2, tq, 128), BF16),
            pltpu.VMEM((2, tq, tq), F32),
            pltpu.VMEM((2, 2, tq, tq), F32),
            pltpu.VMEM((2, tq, tq), BF16),
            pltpu.VMEM((2, tq, 128), F32),
            pltpu.VMEM((2, tq, 128), F32),
            pltpu.VMEM((2, tq, 128), F32),
            pltpu.VMEM((2, tq, 128), F32),
        ],
        compiler_params=_cparams(("arbitrary", "arbitrary")),
    )(proj, proj, proj, c_col, c_row)


def _sb_terms(z, strict):
    e = jnp.exp(-jnp.abs(z))
    L = -(jnp.maximum(z, 0.0) + jnp.log(1.0 + e))
    if strict is not None:
        L = jnp.where(strict, L, 0.0)
    return L, e


def _sb_fwd(proj, tq):
    T = proj.shape[0]

    def body(q_ref, k_ref, v_ref, o_ref, ltot_ref, qh_s, z_s, g_s, tot_s, run_s, acc_s):
        i = pl.program_id(1)
        lane = lax.broadcasted_iota(jnp.int32, (1, 128), 1)
        row = lax.broadcasted_iota(jnp.int32, (tq, tq), 0)
        col = lax.broadcasted_iota(jnp.int32, (tq, tq), 1)
        strict = col < row
        later = jnp.where(row > col, 1.0, 0.0).astype(BF16)
        q = q_ref[...]
        for hh in range(2):
            qh_s[hh] = _head_q(q, hh, lane)[0]
        run_s[...] = jnp.zeros_like(run_s)
        acc_s[...] = jnp.zeros_like(acc_s)

        def rows(t):
            return pl.ds(pl.multiple_of((i - t) * tq, tq), tq)

        def stage_a(t, slot):
            k = k_ref[rows(t), :]
            for hh in range(2):
                z_s[slot, hh] = _nt(qh_s[hh], k)

        def stage_b(t, slot, diag):
            for hh in range(2):
                z = z_s[slot, hh]
                L, _ = _sb_terms(z, strict if diag else None)
                g = z + L
                if diag:
                    g = jnp.where(strict, g, NEG)
                after = _split_dot(L, later, 2)
                g_s[hh] = g + after
                tot_s[hh] = jnp.broadcast_to(after[:, 0:1] + L[:, 0:1], (tq, 128))

        def stage_c(t):
            v = v_ref[rows(t), :]
            for hh in range(2):
                run = run_s[hh]
                a = jnp.exp(g_s[hh] + _lanes2(run))
                acc_s[hh] += _nn(a.astype(BF16), v)
                run_s[hh] = run + tot_s[hh]

        _pipeline3(i + 1, stage_a, stage_b, stage_c, False)
        ltot_ref[0] = run_s[0][:, 0:1]
        ltot_ref[1] = run_s[1][:, 0:1]
        o_ref[...] = jnp.where(lane < HEAD_DIM, acc_s[0], acc_s[1])

    return pl.pallas_call(
        body,
        name="sb_fwd",
        grid=(4, T // tq),
        in_specs=[
            pl.BlockSpec((tq, 128), lambda p, i: (i, 12 + p)),
            pl.BlockSpec((T, 128), lambda p, i: (0, 16 + p)),
            pl.BlockSpec((T, 128), lambda p, i: (0, 20 + p)),
        ],
        out_specs=[
            pl.BlockSpec((tq, 128), lambda p, i: (i, p)),
            pl.BlockSpec((2, tq, 1), lambda p, i: (p, i, 0)),
        ],
        out_shape=[
            jax.ShapeDtypeStruct((T, FOX_W), F32),
            jax.ShapeDtypeStruct((N_FOX, T, 1), F32),
        ],
        scratch_shapes=[
            pltpu.VMEM((2, tq, 128), BF16),
            pltpu.VMEM((2, 2, tq, tq), F32),
            pltpu.VMEM((2, tq, tq), F32),
            pltpu.VMEM((2, tq, 128), F32),
            pltpu.VMEM((2, tq, 128), F32),
            pltpu.VMEM((2, tq, 128), F32),
        ],
        compiler_params=_cparams(("arbitrary", "arbitrary")),
    )(proj, proj, proj)


def _post_attn_fwd(fox_o, sb_o, gf, gs, w_out, x, tm):
    T, D = x.shape

    def body(f_ref, s_ref, gf_ref, gs_ref, w_ref, x_ref, x1_ref, mix_ref):
        f = f_ref[...]
        s = s_ref[...]
        mix_ref[:, :FOX_W] = (f * _rstd(f) * gf_ref[...]).astype(BF16)
        mix_ref[:, FOX_W:] = (s * _rstd(s) * gs_ref[...]).astype(BF16)
        x1_ref[...] = x_ref[...] + _nn(mix_ref[...], w_ref[...])

    return pl.pallas_call(
        body,
        name="post_attn_fwd",
        grid=(T // tm,),
        in_specs=[
            pl.BlockSpec((tm, FOX_W), lambda i: (i, 0)),
            pl.BlockSpec((tm, FOX_W), lambda i: (i, 0)),
            pl.BlockSpec((1, FOX_W), lambda i: (0, 0)),
            pl.BlockSpec((1, FOX_W), lambda i: (0, 0)),
            pl.BlockSpec((D, D), lambda i: (0, 0)),
            pl.BlockSpec((tm, D), lambda i: (i, 0)),
        ],
        out_specs=[
            pl.BlockSpec((tm, D), lambda i: (i, 0)),
            pl.BlockSpec((tm, D), lambda i: (i, 0)),
        ],
        out_shape=[jax.ShapeDtypeStruct((T, D), F32), jax.ShapeDtypeStruct((T, D), BF16)],
        compiler_params=_cparams(("arbitrary",)),
    )(fox_o, sb_o, gf, gs, w_out, x)


def _mem_kv_fwd(mem, gm, w_mkv):
    M, D = mem.shape
    N = w_mkv.shape[1]

    def body(mem_ref, g_ref, w_ref, m_ref, kv_ref):
        mv = mem_ref[...]
        m = (mv * _rstd(mv) * g_ref[...]).astype(BF16)
        m_ref[...] = m
        for n0 in range(0, N, 512):
            kv_ref[:, n0:n0 + 512] = _nn(m, w_ref[:, n0:n0 + 512]).astype(BF16)

    return pl.pallas_call(
        body,
        name="mem_kv_fwd",
        out_shape=[jax.ShapeDtypeStruct((M, D), BF16), jax.ShapeDtypeStruct((M, N), BF16)],
        compiler_params=_cparams(),
    )(mem, gm, w_mkv)


def _xattn_probs(qb, kv, h):
    k = kv[:, h * MEM_HD:(h + 1) * MEM_HD]
    s = _nt(qb[:, h * MEM_HD:(h + 1) * MEM_HD], k) * (MEM_HD ** -0.5)
    s = s - jnp.max(s, axis=1, keepdims=True)
    p = jnp.exp(s)
    return p / jnp.sum(p, axis=1, keepdims=True)


def _xattn_fwd(x1, g2, w_mq, kv, w_mo, tm):
    T, D = x1.shape
    M = kv.shape[0]

    def body(x_ref, g_ref, wq_ref, kv_ref, wo_ref, x2_ref, h_ref, q_ref, om_ref):
        xv = x_ref[...]
        h = (xv * _rstd(xv) * g_ref[...]).astype(BF16)
        h_ref[...] = h
        q_ref[...] = _nn(h, wq_ref[...]).astype(BF16)
        qb = q_ref[...]
        kvv = kv_ref[...]
        for hd in range(N_MEM_HEADS):
            p = _xattn_probs(qb, kvv, hd)
            v = kvv[:, D + hd * MEM_HD:D + (hd + 1) * MEM_HD]
            om_ref[:, hd * MEM_HD:(hd + 1) * MEM_HD] = _nn(p.astype(BF16), v).astype(BF16)
        x2_ref[...] = xv + _nn(om_ref[...], wo_ref[...])

    return pl.pallas_call(
        body,
        name="xattn_fwd",
        grid=(T // tm,),
        in_specs=[
            pl.BlockSpec((tm, D), lambda i: (i, 0)),
            pl.BlockSpec((1, D), lambda i: (0, 0)),
            pl.BlockSpec((D, D), lambda i: (0, 0)),
            pl.BlockSpec((M, 2 * D), lambda i: (0, 0)),
            pl.BlockSpec((D, D), lambda i: (0, 0)),
        ],
        out_specs=[pl.BlockSpec((tm, D), lambda i: (i, 0))] * 4,
        out_shape=[jax.ShapeDtypeStruct((T, D), F32)] + [jax.ShapeDtypeStruct((T, D), BF16)] * 3,
        compiler_params=_cparams(("arbitrary",)),
    )(x1, g2, w_mq, kv, w_mo)


def _conv_taps(ext_ref, tm, back):
    if back:
        return ext_ref[pl.ds(6, tm), :], ext_ref[pl.ds(7, tm), :], ext_ref[pl.ds(8, tm), :]
    return ext_ref[pl.ds(0, tm), :], ext_ref[pl.ds(1, tm), :], ext_ref[pl.ds(2, tm), :]


def _ffn_fwd(x2, g3, w_up, conv_w, conv_b, w_down, tm):
    T, D = x2.shape
    fc = FF_CHUNK
    nj = D_FF // fc

    def body(x_ref, g_ref, wg_ref, wv_ref, cwg_ref, cwv_ref, cbg_ref, cbv_ref, wd_ref,
             x3_ref, h_ref, ug_ref, uv_ref, a_ref, acc_ref, carry_ref, ext_ref):
        i = pl.program_id(0)
        j = pl.program_id(1)

        @pl.when(j == 0)
        def _():
            xv = x_ref[...]
            h_ref[...] = (xv * _rstd(xv) * g_ref[...]).astype(BF16)
            acc_ref[...] = xv

        @pl.when(i == 0)
        def _():
            carry_ref[j] = jnp.zeros((2, 8, fc), F32)

        h = h_ref[...]
        halves = []
        for part, (w_ref, cw_ref, cb_ref, u_ref) in enumerate(
                ((wg_ref, cwg_ref, cbg_ref, ug_ref), (wv_ref, cwv_ref, cbv_ref, uv_ref))):
            u = _nn(h, w_ref[...])
            u_ref[...] = u
            ext = ext_ref.at[part]
            ext[pl.ds(0, 8), :] = carry_ref[j, part]
            ext[pl.ds(8, tm), :] = u
            carry_ref[j, part] = u[tm - 8:, :]
            u2, u1, u0 = _conv_taps(ext, tm, True)
            cw = cw_ref[...]
            halves.append(cb_ref[...] + cw[0:1] * u2 + cw[1:2] * u1 + cw[2:3] * u0)
        gate, val = halves
        a = (gate * jax.nn.sigmoid(gate) * val).astype(BF16)
        a_ref[...] = a
        acc_ref[...] += _nn(a, wd_ref[...])

        @pl.when(j == nj - 1)
        def _():
            x3_ref[...] = acc_ref[...]

    return pl.pallas_call(
        body,
        name="ffn_fwd",
        grid=(T // tm, nj),
        in_specs=[
            pl.BlockSpec((tm, D), lambda i, j: (i, 0)),
            pl.BlockSpec((1, D), lambda i, j: (0, 0)),
            pl.BlockSpec((D, fc), lambda i, j: (0, j)),
            pl.BlockSpec((D, fc), lambda i, j: (0, nj + j)),
            pl.BlockSpec((3, fc), lambda i, j: (0, j)),
            pl.BlockSpec((3, fc), lambda i, j: (0, nj + j)),
            pl.BlockSpec((1, fc), lambda i, j: (0, j)),
            pl.BlockSpec((1, fc), lambda i, j: (0, nj + j)),
            pl.BlockSpec((fc, D), lambda i, j: (j, 0)),
        ],
        out_specs=[
            pl.BlockSpec((tm, D), lambda i, j: (i, 0)),
            pl.BlockSpec((tm, D), lambda i, j: (i, 0)),
            pl.BlockSpec((tm, fc), lambda i, j: (i, j)),
            pl.BlockSpec((tm, fc), lambda i, j: (i, j)),
            pl.BlockSpec((tm, fc), lambda i, j: (i, j)),
        ],
        out_shape=[
            jax.ShapeDtypeStruct((T, D), F32),
            jax.ShapeDtypeStruct((T, D), BF16),
            jax.ShapeDtypeStruct((T, D_FF), F32),
            jax.ShapeDtypeStruct((T, D_FF), F32),
            jax.ShapeDtypeStruct((T, D_FF), BF16),
        ],
        scratch_shapes=[
            pltpu.VMEM((tm, D), F32),
            pltpu.VMEM((nj, 2, 8, fc), F32),
            pltpu.VMEM((2, tm + 8, fc), F32),
        ],
        compiler_params=_cparams(("arbitrary", "arbitrary")),
    )(x2, g3, w_up, w_up, conv_w, conv_w, conv_b, conv_b, w_down)


def _loss_head(x3, gfin, target, tm):
    T, D = x3.shape

    def body(x_ref, g_ref, t_ref, dx_ref, loss_ref, dg_ref):
        i = pl.program_id(0)

        @pl.when(i == 0)
        def _():
            loss_ref[...] = jnp.zeros_like(loss_ref)
            dg_ref[...] = jnp.zeros_like(dg_ref)

        xv = x_ref[...]
        g = g_ref[...]
        r = _rstd(xv)
        xhat = xv * r
        err = xhat * g - t_ref[...]
        part = jnp.sum(jnp.sum(err * err, axis=1, keepdims=True), axis=0, keepdims=True) * (0.5 / D)
        loss_ref[...] += jnp.broadcast_to(part, loss_ref.shape)
        dy = err * (1.0 / D)
        dg_ref[...] += jnp.sum(dy * xhat, axis=0, keepdims=True)
        dxhat = dy * g
        dx_ref[...] = r * (dxhat - xhat * jnp.mean(dxhat * xhat, axis=-1, keepdims=True))

    return pl.pallas_call(
        body,
        name="loss_head",
        grid=(T // tm,),
        in_specs=[
            pl.BlockSpec((tm, D), lambda i: (i, 0)),
            pl.BlockSpec((1, D), lambda i: (0, 0)),
            pl.BlockSpec((tm, D), lambda i: (i, 0)),
        ],
        out_specs=[
            pl.BlockSpec((tm, D), lambda i: (i, 0)),
            pl.BlockSpec((8, 128), lambda i: (0, 0)),
            pl.BlockSpec((1, D), lambda i: (0, 0)),
        ],
        out_shape=[
            jax.ShapeDtypeStruct((T, D), F32),
            jax.ShapeDtypeStruct((8, 128), F32),
            jax.ShapeDtypeStruct((1, D), F32),
        ],
        compiler_params=_cparams(("arbitrary",)),
    )(x3, gfin, target)


def _ffn_bwd(dx3, x2, g3, ug, uv, conv_w, conv_b, w_down_t, w_up_t, tm):
    T, D = x2.shape
    fc = FF_CHUNK
    nj = D_FF // fc
    nt = T // tm
    hb = tm // 8

    def rev(i):
        return nt - 1 - i

    def body(dx3_ref, x_ref, g_ref, ug_ref, uv_ref, ugh_ref, uvh_ref, cwg_ref, cwv_ref, cbg_ref, cbv_ref,
             wdt_ref, wutg_ref, wutv_ref,
             dx2_ref, dug_ref, duv_ref, dg_ref, dcg_ref, dcv_ref,
             acc_ref, carry_ref, ext_ref):
        i = pl.program_id(0)
        j = pl.program_id(1)
        first_tile = i == nt - 1
        cols = pl.ds(pl.multiple_of(j * fc, fc), fc)

        @pl.when(j == 0)
        def _():
            acc_ref[...] = jnp.zeros_like(acc_ref)

        @pl.when((i == 0) & (j == 0))
        def _():
            dg_ref[...] = jnp.zeros_like(dg_ref)
            dcg_ref[...] = jnp.zeros_like(dcg_ref)
            dcv_ref[...] = jnp.zeros_like(dcv_ref)

        @pl.when(i == 0)
        def _():
            carry_ref[j] = jnp.zeros((2, 8, fc), F32)

        da = _nn(dx3_ref[...].astype(BF16), wdt_ref[...])
        pre = []
        for part, (u_ref, uh_ref, cw_ref, cb_ref) in enumerate(
                ((ug_ref, ugh_ref, cwg_ref, cbg_ref), (uv_ref, uvh_ref, cwv_ref, cbv_ref))):
            ext = ext_ref.at[part]
            ext[pl.ds(0, 8), :] = jnp.where(first_tile, 0.0, uh_ref[...])
            ext[pl.ds(8, tm), :] = u_ref[...]
            u2, u1, u0 = _conv_taps(ext, tm, True)
            cw = cw_ref[...]
            pre.append(cb_ref[...] + cw[0:1] * u2 + cw[1:2] * u1 + cw[2:3] * u0)
        gate, val = pre
        sig = jax.nn.sigmoid(gate)
        silu = gate * sig
        dys = (da * val * (sig * (1.0 + gate * (1.0 - sig))), da * silu)
        for part, (dy, cw_ref, du_ref, wut_ref, dc_ref) in enumerate(
                ((dys[0], cwg_ref, dug_ref, wutg_ref, dcg_ref), (dys[1], cwv_ref, duv_ref, wutv_ref, dcv_ref))):
            u2, u1, u0 = _conv_taps(ext_ref.at[part], tm, True)
            upd = jnp.concatenate([
                jnp.sum(u2 * dy, axis=0, keepdims=True),
                jnp.sum(u1 * dy, axis=0, keepdims=True),
                jnp.sum(u0 * dy, axis=0, keepdims=True),
                jnp.sum(dy, axis=0, keepdims=True),
                jnp.zeros((4, fc), F32)], axis=0)
            dc_ref[:, cols] += upd
            ext = ext_ref.at[2 + part]
            ext[pl.ds(0, tm), :] = dy
            ext[pl.ds(tm, 8), :] = carry_ref[j, part]
            carry_ref[j, part] = dy[:8, :]
            d0, d1, d2 = _conv_taps(ext, tm, False)
            cw = cw_ref[...]
            du = (cw[2:3] * d0 + cw[1:2] * d1 + cw[0:1] * d2).astype(BF16)
            du_ref[...] = du
            acc_ref[...] += _nn(du, wut_ref[...])

        @pl.when(j == nj - 1)
        def _():
            dx, dg = _norm_bwd(x_ref[...], g_ref[...], acc_ref[...])
            dx2_ref[...] = dx3_ref[...] + dx
            dg_ref[...] += dg

    return pl.pallas_call(
        body,
        name="ffn_bwd",
        grid=(nt, nj),
        in_specs=[
            pl.BlockSpec((tm, D), lambda i, j: (rev(i), 0)),
            pl.BlockSpec((tm, D), lambda i, j: (rev(i), 0)),
            pl.BlockSpec((1, D), lambda i, j: (0, 0)),
            pl.BlockSpec((tm, fc), lambda i, j: (rev(i), j)),
            pl.BlockSpec((tm, fc), lambda i, j: (rev(i), j)),
            pl.BlockSpec((8, fc), lambda i, j: (jnp.maximum(rev(i) * hb - 1, 0), j)),
            pl.BlockSpec((8, fc), lambda i, j: (jnp.maximum(rev(i) * hb - 1, 0), j)),
            pl.BlockSpec((3, fc), lambda i, j: (0, j)),
            pl.BlockSpec((3, fc), lambda i, j: (0, nj + j)),
            pl.BlockSpec((1, fc), lambda i, j: (0, j)),
            pl.BlockSpec((1, fc), lambda i, j: (0, nj + j)),
            pl.BlockSpec((D, fc), lambda i, j: (0, j)),
            pl.BlockSpec((fc, D), lambda i, j: (j, 0)),
            pl.BlockSpec((fc, D), lambda i, j: (nj + j, 0)),
        ],
        out_specs=[
            pl.BlockSpec((tm, D), lambda i, j: (rev(i), 0)),
            pl.BlockSpec((tm, fc), lambda i, j: (rev(i), j)),
            pl.BlockSpec((tm, fc), lambda i, j: (rev(i), j)),
            pl.BlockSpec((1, D), lambda i, j: (0, 0)),
            pl.BlockSpec((8, D_FF), lambda i, j: (0, 0)),
            pl.BlockSpec((8, D_FF), lambda i, j: (0, 0)),
        ],
        out_shape=[
            jax.ShapeDtypeStruct((T, D), F32),
            jax.ShapeDtypeStruct((T, D_FF), BF16),
            jax.ShapeDtypeStruct((T, D_FF), BF16),
            jax.ShapeDtypeStruct((1, D), F32),
            jax.ShapeDtypeStruct((8, D_FF), F32),
            jax.ShapeDtypeStruct((8, D_FF), F32),
        ],
        scratch_shapes=[
            pltpu.VMEM((tm, D), F32),
            pltpu.VMEM((nj, 2, 8, fc), F32),
            pltpu.VMEM((4, tm + 8, fc), F32),
        ],
        compiler_params=_cparams(("arbitrary", "arbitrary")),
    )(dx3, x2, g3, ug, uv, ug, uv, conv_w, conv_w, conv_b, conv_b, w_down_t, w_up_t, w_up_t)


def _xattn_bwd(dx2, x1, g2, qb, kv, w_mo_t, w_mq_t, tm):
    T, D = x1.shape
    M = kv.shape[0]

    def body(dx2_ref, x_ref, g_ref, q_ref, kv_ref, wot_ref, wqt_ref, dx1_ref, dq_ref, dkv_ref, dg_ref):
        i = pl.program_id(0)

        @pl.when(i == 0)
        def _():
            dkv_ref[...] = jnp.zeros_like(dkv_ref)
            dg_ref[...] = jnp.zeros_like(dg_ref)

        dxv = dx2_ref[...]
        dom = _nn(dxv.astype(BF16), wot_ref[...]).astype(BF16)
        qb_ = q_ref[...]
        kvv = kv_ref[...]
        for hd in range(N_MEM_HEADS):
            sl = slice(hd * MEM_HD, (hd + 1) * MEM_HD)
            vsl = slice(D + hd * MEM_HD, D + (hd + 1) * MEM_HD)
            p = _xattn_probs(qb_, kvv, hd)
            dp = _nt(dom[:, sl], kvv[:, vsl])
            ds = (p * (dp - jnp.sum(p * dp, axis=1, keepdims=True)) * (MEM_HD ** -0.5)).astype(BF16)
            dq_ref[:, sl] = _nn(ds, kvv[:, sl]).astype(BF16)
            dkv_ref[:, sl] += _tn(ds, qb_[:, sl])
            dkv_ref[:, vsl] += _tn(p.astype(BF16), dom[:, sl])
        dh = _nn(dq_ref[...], wqt_ref[...])
        dx, dg = _norm_bwd(x_ref[...], g_ref[...], dh)
        dx1_ref[...] = dxv + dx
        dg_ref[...] += dg

    return pl.pallas_call(
        body,
        name="xattn_bwd",
        grid=(T // tm,),
        in_specs=[
            pl.BlockSpec((tm, D), lambda i: (i, 0)),
            pl.BlockSpec((tm, D), lambda i: (i, 0)),
            pl.BlockSpec((1, D), lambda i: (0, 0)),
            pl.BlockSpec((tm, D), lambda i: (i, 0)),
            pl.BlockSpec((M, 2 * D), lambda i: (0, 0)),
            pl.BlockSpec((D, D), lambda i: (0, 0)),
            pl.BlockSpec((D, D), lambda i: (0, 0)),
        ],
        out_specs=[
            pl.BlockSpec((tm, D), lambda i: (i, 0)),
            pl.BlockSpec((tm, D), lambda i: (i, 0)),
            pl.BlockSpec((M, 2 * D), lambda i: (0, 0)),
            pl.BlockSpec((1, D), lambda i: (0, 0)),
        ],
        out_shape=[
            jax.ShapeDtypeStruct((T, D), F32),
            jax.ShapeDtypeStruct((T, D), BF16),
            jax.ShapeDtypeStruct((M, 2 * D), F32),
            jax.ShapeDtypeStruct((1, D), F32),
        ],
        compiler_params=_cparams(("arbitrary",)),
    )(dx2, x1, g2, qb, kv, w_mo_t, w_mq_t)


def _mem_kv_bwd(mem, gm, mb, dkv, w_mkv_t):
    M, D = mem.shape
    N = dkv.shape[1]

    def body(mem_ref, g_ref, m_ref, dkv_ref, wt_ref, dw_ref, dg_ref):
        dkvb = dkv_ref[...].astype(BF16)
        for n0 in range(0, N, 512):
            dw_ref[:, n0:n0 + 512] = _tn(m_ref[...], dkvb[:, n0:n0 + 512])
        dm = _nn(dkvb, wt_ref[...])
        mv = mem_ref[...]
        dg_ref[...] = jnp.sum(dm * (mv * _rstd(mv)), axis=0, keepdims=True)

    return pl.pallas_call(
        body,
        name="mem_kv_bwd",
        out_shape=[jax.ShapeDtypeStruct((D, N), F32), jax.ShapeDtypeStruct((1, D), F32)],
        compiler_params=_cparams(),
    )(mem, gm, mb, dkv, w_mkv_t)


def _post_attn_bwd(dx1, fox_o, sb_o, gf, gs, w_out_t, tm):
    T, D = dx1.shape

    def body(dx_ref, f_ref, s_ref, gf_ref, gs_ref, wt_ref, df_ref, ds_ref, dgf_ref, dgs_ref):
        i = pl.program_id(0)

        @pl.when(i == 0)
        def _():
            dgf_ref[...] = jnp.zeros_like(dgf_ref)
            dgs_ref[...] = jnp.zeros_like(dgs_ref)

        dmix = _nn(dx_ref[...].astype(BF16), wt_ref[...])
        d, dg = _norm_bwd(f_ref[...], gf_ref[...], dmix[:, :FOX_W])
        df_ref[...] = d
        dgf_ref[...] += dg
        d, dg = _norm_bwd(s_ref[...], gs_ref[...], dmix[:, FOX_W:])
        ds_ref[...] = d
        dgs_ref[...] += dg

    return pl.pallas_call(
        body,
        name="post_attn_bwd",
        grid=(T // tm,),
        in_specs=[
            pl.BlockSpec((tm, D), lambda i: (i, 0)),
            pl.BlockSpec((tm, FOX_W), lambda i: (i, 0)),
            pl.BlockSpec((tm, FOX_W), lambda i: (i, 0)),
            pl.BlockSpec((1, FOX_W), lambda i: (0, 0)),
            pl.BlockSpec((1, FOX_W), lambda i: (0, 0)),
            pl.BlockSpec((D, D), lambda i: (0, 0)),
        ],
        out_specs=[
            pl.BlockSpec((tm, FOX_W), lambda i: (i, 0)),
            pl.BlockSpec((tm, FOX_W), lambda i: (i, 0)),
            pl.BlockSpec((1, FOX_W), lambda i: (0, 0)),
            pl.BlockSpec((1, FOX_W), lambda i: (0, 0)),
        ],
        out_shape=[
            jax.ShapeDtypeStruct((T, FOX_W), F32),
            jax.ShapeDtypeStruct((T, FOX_W), F32),
            jax.ShapeDtypeStruct((1, FOX_W), F32),
            jax.ShapeDtypeStruct((1, FOX_W), F32),
        ],
        compiler_params=_cparams(("arbitrary",)),
    )(dx1, fox_o, sb_o, gf, gs, w_out_t)


def _sb_bwd(proj, ltot, d_o, tq):
    T = proj.shape[0]

    def body(q_ref, k_ref, v_ref, lt_ref, do_ref, dq_ref, dk_ref, dv_ref,
             qh_s, doh_s, lt_s, z_s, da_s, ab_s, dzb_s, run_s, runw_s, dq_s):
        i = pl.program_id(1)

        @pl.when(i == 0)
        def _():
            dk_ref[...] = jnp.zeros_like(dk_ref)
            dv_ref[...] = jnp.zeros_like(dv_ref)

        lane = lax.broadcasted_iota(jnp.int32, (1, 128), 1)
        row = lax.broadcasted_iota(jnp.int32, (tq, tq), 0)
        col = lax.broadcasted_iota(jnp.int32, (tq, tq), 1)
        strict = col < row
        upto = jnp.where(row <= col, 1.0, 0.0).astype(BF16)
        before = jnp.where(row < col, 1.0, 0.0).astype(BF16)
        q = q_ref[...]
        dov = do_ref[...]
        for hh in range(2):
            qh, hmask = _head_q(q, hh, lane)
            qh_s[hh] = qh
            doh_s[hh] = jnp.where(hmask, dov, 0.0).astype(BF16)
            lt_s[hh] = jnp.broadcast_to(lt_ref[hh], (tq, 128))
        run_s[...] = jnp.zeros_like(run_s)
        runw_s[...] = jnp.zeros_like(runw_s)
        dq_s[...] = jnp.zeros_like(dq_s)

        def rows(t):
            return pl.ds(pl.multiple_of(t * tq, tq), tq)

        def stage_a(t, slot):
            k = k_ref[rows(t), :]
            v = v_ref[rows(t), :]
            for hh in range(2):
                z_s[slot, hh] = _nt(qh_s[hh], k)
                da_s[slot, hh] = _nt(doh_s[hh], v)

        def stage_b(t, slot, diag):
            for hh in range(2):
                z = z_s[slot, hh]
                L, e = _sb_terms(z, strict if diag else None)
                upto_s = _split_dot(L, upto, 2)
                run = run_s[hh]
                arg = z + L + (_lanes2(lt_s[hh] - run) - upto_s)
                if diag:
                    arg = jnp.where(strict, arg, NEG)
                a = jnp.exp(arg)
                w = a * da_s[slot, hh]
                w_before = _split_dot(w, before, 2)
                run_w = runw_s[hh]
                d_keep = w_before + _lanes2(run_w)
                r = 1.0 / (1.0 + e)
                beta = jnp.where(z >= 0.0, r, e * r)
                dz = w * (1.0 - beta) - d_keep * beta
                if diag:
                    dz = jnp.where(strict, dz, 0.0)
                dzb_s[hh] = dz.astype(BF16)
                ab_s[hh] = a.astype(BF16)
                run_s[hh] = run + jnp.broadcast_to(upto_s[:, tq - 1:tq], (tq, 128))
                runw_s[hh] = run_w + jnp.broadcast_to(w_before[:, tq - 1:tq] + w[:, tq - 1:tq], (tq, 128))

        def stage_c(t):
            k = k_ref[rows(t), :]
            dk_blk = None
            dv_blk = None
            for hh in range(2):
                dzb = dzb_s[hh]
                dq_s[hh] += _nn(dzb, k)
                dk_h = _tn(dzb, qh_s[hh])
                dv_h = _tn(ab_s[hh], doh_s[hh])
                dk_blk = dk_h if dk_blk is None else dk_blk + dk_h
                dv_blk = dv_h if dv_blk is None else dv_blk + dv_h
            dk_ref[rows(t), :] += dk_blk
            dv_ref[rows(t), :] += dv_blk

        _pipeline3(i + 1, stage_a, stage_b, stage_c, True)
        dq_ref[...] = (jnp.where(lane < HEAD_DIM, dq_s[0], dq_s[1]) * (HEAD_DIM ** -0.5)).astype(BF16)

    return pl.pallas_call(
        body,
        name="sb_bwd",
        grid=(4, T // tq),
        in_specs=[
            pl.BlockSpec((tq, 128), lambda p, i: (i, 12 + p)),
            pl.BlockSpec((T, 128), lambda p, i: (0, 16 + p)),
            pl.BlockSpec((T, 128), lambda p, i: (0, 20 + p)),
            pl.BlockSpec((2, tq, 1), lambda p, i: (p, i, 0)),
            pl.BlockSpec((tq, 128), lambda p, i: (i, p)),
        ],
        out_specs=[
            pl.BlockSpec((tq, 128), lambda p, i: (i, p)),
            pl.BlockSpec((T, 128), lambda p, i: (0, p)),
            pl.BlockSpec((T, 128), lambda p, i: (0, p)),
        ],
        out_shape=[
            jax.ShapeDtypeStruct((T, FOX_W), BF16),
            jax.ShapeDtypeStruct((T, FOX_W), F32),
            jax.ShapeDtypeStruct((T, FOX_W), F32),
        ],
        scratch_shapes=[
            pltpu.VMEM((2, tq, 128), BF16),
            pltpu.VMEM((2, tq, 128), BF16),
            pltpu.VMEM((2, tq, 128), F32),
            pltpu.VMEM((2, 2, tq, tq), F32),
            pltpu.VMEM((2, 2, tq, tq), F32),
            pltpu.VMEM((2, tq, tq), BF16),
            pltpu.VMEM((2, tq, tq), BF16),
            pltpu.VMEM((2, tq, 128), F32),
            pltpu.VMEM((2, tq, 128), F32),
            pltpu.VMEM((2, tq, 128), F32),
        ],
        compiler_params=_cparams(("arbitrary", "arbitrary")),
    )(proj, proj, proj, ltot, d_o)


def _fox_bwd(proj, c_col, c_row, lse, d_o, o, tq):
    T = proj.shape[0]

    def body(q_ref, k_ref, v_ref, cq_ref, ck_ref, lse_ref, do_ref, o_ref,
             dq_ref, dk_ref, dv_ref, dck_ref, dcq_ref,
             qh_s, doh_s, delta_s, shift_s, z_s, dp_s, pb_s, dsb_s, rs_s, dq_s):
        i = pl.program_id(1)

        @pl.when(i == 0)
        def _():
            dk_ref[...] = jnp.zeros_like(dk_ref)
            dv_ref[...] = jnp.zeros_like(dv_ref)
            dck_ref[...] = jnp.zeros_like(dck_ref)

        lane = lax.broadcasted_iota(jnp.int32, (1, 128), 1)
        row = lax.broadcasted_iota(jnp.int32, (tq, tq), 0)
        col = lax.broadcasted_iota(jnp.int32, (tq, tq), 1)
        q = q_ref[...]
        dov = do_ref[...]
        ov = o_ref[...]
        for hh in range(2):
            qh, hmask = _head_q(q, hh, lane)
            dohb = jnp.where(hmask, dov, 0.0).astype(BF16)
            qh_s[hh] = qh
            doh_s[hh] = dohb
            delta_s[hh] = jnp.broadcast_to(jnp.sum(dohb.astype(F32) * ov, axis=1, keepdims=True), (tq, tq))
            shift_s[hh] = jnp.broadcast_to(cq_ref[hh] - lse_ref[hh], (tq, tq))
        rs_s[...] = jnp.zeros_like(rs_s)
        dq_s[...] = jnp.zeros_like(dq_s)

        def rows(t):
            return pl.ds(pl.multiple_of((i - t) * tq, tq), tq)

        def stage_a(t, slot):
            k = k_ref[rows(t), :]
            v = v_ref[rows(t), :]
            for hh in range(2):
                z_s[slot, hh] = _nt(qh_s[hh], k)
                dp_s[slot, hh] = _nt(doh_s[hh], v)

        def stage_b(t, slot, diag):
            for hh in range(2):
                s = z_s[slot, hh] + shift_s[hh] - ck_ref[hh, :, rows(t)]
                if diag:
                    s = jnp.where(col <= row, s, NEG)
                p = jnp.exp(s)
                ds = p * (dp_s[slot, hh] - delta_s[hh])
                pb_s[hh] = p.astype(BF16)
                dsb_s[hh] = ds.astype(BF16)
                dck_ref[hh, :, rows(t)] += jnp.sum(ds, axis=0, keepdims=True)
                rs_s[hh] += jnp.sum(ds, axis=1, keepdims=True)

        def stage_c(t):
            k = k_ref[rows(t), :]
            dk_blk = None
            dv_blk = None
            for hh in range(2):
                dsb = dsb_s[hh]
                dq_s[hh] += _nn(dsb, k)
                dk_h = _tn(dsb, qh_s[hh])
                dv_h = _tn(pb_s[hh], doh_s[hh])
                dk_blk = dk_h if dk_blk is None else dk_blk + dk_h
                dv_blk = dv_h if dv_blk is None else dv_blk + dv_h
            dk_ref[rows(t), :] += dk_blk
            dv_ref[rows(t), :] += dv_blk

        _pipeline3(i + 1, stage_a, stage_b, stage_c, False)
        dcq_ref[0] = rs_s[0][:, 0:1]
        dcq_ref[1] = rs_s[1][:, 0:1]
        dq_ref[...] = (jnp.where(lane < HEAD_DIM, dq_s[0], dq_s[1]) * (HEAD_DIM ** -0.5)).astype(BF16)

    return pl.pallas_call(
        body,
        name="fox_bwd",
        grid=(4, T // tq),
        in_specs=[
            pl.BlockSpec((tq, 128), lambda p, i: (i, p)),
            pl.BlockSpec((T, 128), lambda p, i: (0, 4 + p)),
            pl.BlockSpec((T, 128), lambda p, i: (0, 8 + p)),
            pl.BlockSpec((2, tq, 1), lambda p, i: (p, i, 0)),
            pl.BlockSpec((2, 1, T), lambda p, i: (p, 0, 0)),
            pl.BlockSpec((2, tq, 1), lambda p, i: (p, i, 0)),
            pl.BlockSpec((tq, 128), lambda p, i: (i, p)),
            pl.BlockSpec((tq, 128), lambda p, i: (i, p)),
        ],
        out_specs=[
            pl.BlockSpec((tq, 128), lambda p, i: (i, p)),
            pl.BlockSpec((T, 128), lambda p, i: (0, p)),
            pl.BlockSpec((T, 128), lambda p, i: (0, p)),
            pl.BlockSpec((2, 1, T), lambda p, i: (p, 0, 0)),
            pl.BlockSpec((2, tq, 1), lambda p, i: (p, i, 0)),
        ],
        out_shape=[
            jax.ShapeDtypeStruct((T, FOX_W), BF16),
            jax.ShapeDtypeStruct((T, FOX_W), F32),
            jax.ShapeDtypeStruct((T, FOX_W), F32),
            jax.ShapeDtypeStruct((N_FOX, 1, T), F32),
            jax.ShapeDtypeStruct((N_FOX, T, 1), F32),
        ],
        scratch_shapes=[
            pltpu.VMEM((2, tq, 128), BF16),
            pltpu.VMEM((2, tq, 128), BF16),
            pltpu.VMEM((2, tq, tq), F32),
            pltpu.VMEM((2, tq, tq), F32),
            pltpu.VMEM((2, 2, tq, tq), F32),
            pltpu.VMEM((2, 2, tq, tq), F32),
            pltpu.VMEM((2, tq, tq), BF16),
            pltpu.VMEM((2, tq, tq), BF16),
            pltpu.VMEM((2, tq, 128), F32),
            pltpu.VMEM((2, tq, 128), F32),
        ],
        compiler_params=_cparams(("arbitrary", "arbitrary")),
    )(proj, proj, proj, c_col, c_row, lse, d_o, o)


def _forget_bwd(dcq, dck, xf, tc):
    H, T = xf.shape
    nc = T // tc

    def body(dcq_ref, dck_ref, xf_ref, dxf_ref, db_ref):
        row = lax.broadcasted_iota(jnp.int32, (tc, tc), 0)
        col = lax.broadcasted_iota(jnp.int32, (tc, tc), 1)
        from_here = jnp.where(row >= col, 1.0, 0.0).astype(BF16)

        def chunk(n, carry):
            run, db = carry
            cs = pl.multiple_of((nc - 1 - n) * tc, tc)
            dc = dcq_ref[:, pl.ds(cs, tc)] - dck_ref[:, pl.ds(cs, tc)]
            dlogf = _split_dot(dc, from_here, 3) + run
            xfv = xf_ref[:, pl.ds(cs, tc)]
            dxf = dlogf * jax.nn.sigmoid(-xfv)
            dxf_ref[:, pl.ds(cs, tc)] = dxf
            return dlogf[:, 0:1], db + jnp.sum(dxf, axis=1, keepdims=True)

        _, db = lax.fori_loop(0, nc, chunk, (jnp.zeros((H, 1), F32), jnp.zeros((H, 1), F32)))
        db_ref[...] = db

    return pl.pallas_call(
        body,
        name="forget_bwd",
        out_shape=[jax.ShapeDtypeStruct((H, T), F32), jax.ShapeDtypeStruct((H, 1), F32)],
        compiler_params=_cparams(),
    )(dcq, dck, xf)


def _inproj_bwd(dproj, w_in_t, x, g1, dx1, tm):
    T, D = x.shape
    N = dproj.shape[1]

    def body(dp_ref, wt_ref, x_ref, g_ref, dx1_ref, dx_ref, dg_ref):
        i = pl.program_id(0)

        @pl.when(i == 0)
        def _():
            dg_ref[...] = jnp.zeros_like(dg_ref)

        dh = _nn(dp_ref[...], wt_ref[...])
        dx, dg = _norm_bwd(x_ref[...], g_ref[...], dh)
        dx_ref[...] = dx1_ref[...] + dx
        dg_ref[...] += dg

    return pl.pallas_call(
        body,
        name="inproj_bwd",
        grid=(T // tm,),
        in_specs=[
            pl.BlockSpec((tm, N), lambda i: (i, 0)),
            pl.BlockSpec((N, D), lambda i: (0, 0)),
            pl.BlockSpec((tm, D), lambda i: (i, 0)),
            pl.BlockSpec((1, D), lambda i: (0, 0)),
            pl.BlockSpec((tm, D), lambda i: (i, 0)),
        ],
        out_specs=[
            pl.BlockSpec((tm, D), lambda i: (i, 0)),
            pl.BlockSpec((1, D), lambda i: (0, 0)),
        ],
        out_shape=[jax.ShapeDtypeStruct((T, D), F32), jax.ShapeDtypeStruct((1, D), F32)],
        compiler_params=_cparams(("arbitrary",)),
    )(dproj, w_in_t, x, g1, dx1)


def _matmul_tn(a, b, name, cast_b=False):
    T, K = a.shape
    N = b.shape[1]
    bt = min(T, 512)
    bk = _tile_div(K, 1536)
    bn = _tile_div(N, 1536)

    def body(a_ref, b_ref, o_ref):
        @pl.when(pl.program_id(2) == 0)
        def _():
            o_ref[...] = jnp.zeros_like(o_ref)

        bv = b_ref[...]
        if cast_b:
            bv = bv.astype(BF16)
        o_ref[...] += _tn(a_ref[...], bv)

    return pl.pallas_call(
        body,
        name=name,
        grid=(K // bk, N // bn, T // bt),
        in_specs=[
            pl.BlockSpec((bt, bk), lambda k, n, t: (t, k)),
            pl.BlockSpec((bt, bn), lambda k, n, t: (t, n)),
        ],
        out_specs=pl.BlockSpec((bk, bn), lambda k, n, t: (k, n)),
        out_shape=jax.ShapeDtypeStruct((K, N), F32),
        compiler_params=_cparams(("arbitrary", "arbitrary", "arbitrary")),
    )(a, b)


def _local_step(x, mem, target, p, tm, tq):
    T, D = x.shape
    w_in = p["w_in"]
    w_qkv = w_in[:, :QKV_W]
    w_f_t = w_in[:, QKV_W:].T
    w_in_t = jnp.pad(w_in, ((0, 0), (0, IN_PAD - w_in.shape[1]))).T
    b_f = p["b_forget"].reshape(N_FOX, 1)

    proj, h1, xf, c = _inproj_fwd(x, p["attn_norm_g"], w_qkv, w_f_t, b_f, tm)
    c_col = c.reshape(N_FOX, T, 1)
    c_row = c.reshape(N_FOX, 1, T)
    fox_o, lse = _fox_fwd(proj, c_col, c_row, tq)
    sb_o, sb_ltot = _sb_fwd(proj, tq)
    x1, mixed = _post_attn_fwd(fox_o, sb_o, p["fox_out_g"], p["sb_out_g"], p["w_out"], x, tm)
    mb, kv = _mem_kv_fwd(mem, p["mem_norm_g"], p["w_mkv"])
    x2, h2, qb, om = _xattn_fwd(x1, p["xattn_norm_g"], p["w_mq"], kv, p["w_mo"], tm)
    x3, h3, ug, uv, a = _ffn_fwd(x2, p["ffn_norm_g"], p["w_up"], p["conv_w"], p["conv_b"], p["w_down"], tm)
    dx3, loss_blk, d_final_g = _loss_head(x3, p["final_norm_g"], target, tm)

    g = {"final_norm_g": d_final_g}
    dx2, du_g, du_v, g["ffn_norm_g"], dc_g, dc_v = _ffn_bwd(
        dx3, x2, p["ffn_norm_g"], ug, uv, p["conv_w"], p["conv_b"], p["w_down"].T, p["w_up"].T, tm)
    g["w_down"] = _matmul_tn(a, dx3, "dw_down", cast_b=True)
    g["w_up"] = jnp.concatenate([_matmul_tn(h3, du_g, "dw_up_gate"), _matmul_tn(h3, du_v, "dw_up_val")], axis=1)
    dconv = jnp.concatenate([dc_g, dc_v], axis=1)
    g["conv_w"] = dconv[0:3]
    g["conv_b"] = dconv[3:4]
    dx1, dq_m, dkv, g["xattn_norm_g"] = _xattn_bwd(dx2, x1, p["xattn_norm_g"], qb, kv, p["w_mo"].T, p["w_mq"].T, tm)
    g["w_mo"] = _matmul_tn(om, dx2, "dw_mo", cast_b=True)
    g["w_mq"] = _matmul_tn(h2, dq_m, "dw_mq")
    g["w_mkv"], g["mem_norm_g"] = _mem_kv_bwd(mem, p["mem_norm_g"], mb, dkv, p["w_mkv"].T)
    d_fox, d_sb, g["fox_out_g"], g["sb_out_g"] = _post_attn_bwd(
        dx1, fox_o, sb_o, p["fox_out_g"], p["sb_out_g"], p["w_out"].T, tm)
    g["w_out"] = _matmul_tn(mixed, dx1, "dw_out", cast_b=True)
    dq_s, dk_s, dv_s = _sb_bwd(proj, sb_ltot, d_sb, tq)
    dq_f, dk_f, dv_f, dck, dcq = _fox_bwd(proj, c_col, c_row, lse, d_fox, fox_o, tq)
    dxf, db = _forget_bwd(dcq.reshape(N_FOX, T), dck.reshape(N_FOX, T), xf, min(T, 512))
    g["b_forget"] = db.reshape(1, N_FOX)
    dproj = jnp.concatenate([
        dq_f, dk_f.astype(BF16), dv_f.astype(BF16), dq_s, dk_s.astype(BF16), dv_s.astype(BF16),
        jnp.pad(dxf.T, ((0, 0), (0, IN_PAD - QKV_W - N_FOX))).astype(BF16)], axis=1)
    grad_x, g["attn_norm_g"] = _inproj_bwd(dproj, w_in_t, x, p["attn_norm_g"], dx1, tm)
    g["w_in"] = _matmul_tn(h1, dproj, "dw_in")[:, :w_in.shape[1]]
    return loss_blk, grad_x, g


def _mesh_pos():
    return lax.axis_index("x"), lax.axis_index("y"), lax.axis_index("c")


def _flip(pos, k):
    return tuple(1 - v if (k >> b) & 1 else v for v, b in zip(pos, (2, 1, 0)))


def _slot(pos):
    return 4 * pos[0] + 2 * pos[1] + pos[2]


def _all_gather(shards, name):
    n = len(shards)

    def body(*refs):
        ins, outs = refs[:n], refs[n:2 * n]
        send_sems, recv_sems, local_sems = refs[2 * n:]
        me = _mesh_pos()
        sibling = _flip(me, 1)
        chips = [4, 2, 6]

        def copy(a, kk, block, to, src=None):
            rows = outs[a].at[_slot(block)]
            return pltpu.make_async_remote_copy(
                src_ref=rows if src is None else src, dst_ref=rows,
                send_sem=send_sems.at[7 * a + kk], recv_sem=recv_sems.at[7 * a + kk],
                device_id=to, device_id_type=MESH)

        mine = [pltpu.make_async_copy(ins[a], outs[a].at[_slot(me)], local_sems.at[a]) for a in range(n)]
        for cp in mine:
            cp.start()
        first = []
        for a in range(n):
            first.append(copy(a, 0, me, sibling, src=ins[a]))
            first += [copy(a, 1 + j, me, _flip(me, k), src=ins[a]) for j, k in enumerate(chips)]
        for cp in first:
            cp.start()
        passed = []
        for j, k in enumerate(chips):
            for a in range(n):
                copy(a, 1 + j, _flip(me, k), me).wait_recv()
                fwd = copy(a, 4 + j, _flip(me, k), sibling)
                fwd.start()
                passed.append(fwd)
        for a in range(n):
            copy(a, 0, sibling, me).wait_recv()
            for j, k in enumerate(chips):
                copy(a, 4 + j, _flip(sibling, k), me).wait_recv()
        for cp in first + passed:
            cp.wait_send()
        for cp in mine:
            cp.wait()

    any_spec = pl.BlockSpec(memory_space=pl.ANY)
    return pl.pallas_call(
        body,
        name=name,
        in_specs=[any_spec] * n,
        out_specs=[any_spec] * n,
        out_shape=[jax.ShapeDtypeStruct((N_DEV,) + s.shape, s.dtype) for s in shards],
        scratch_shapes=[
            pltpu.SemaphoreType.DMA((7 * n,)),
            pltpu.SemaphoreType.DMA((7 * n,)),
            pltpu.SemaphoreType.DMA((n,)),
        ],
    )(*shards)


def _all_to_all(blocks, name):
    n = len(blocks)

    def body(*refs):
        ins, outs = refs[:n], refs[n:2 * n]
        send_sems, recv_sems, local_sems = refs[2 * n:]
        me = _mesh_pos()
        mine = [pltpu.make_async_copy(ins[a].at[_slot(me)], outs[a].at[_slot(me)], local_sems.at[a])
                for a in range(n)]
        for cp in mine:
            cp.start()

        def copy(a, k):
            peer = _flip(me, k)
            return pltpu.make_async_remote_copy(
                src_ref=ins[a].at[_slot(peer)], dst_ref=outs[a].at[_slot(me)],
                send_sem=send_sems.at[7 * a + k - 1], recv_sem=recv_sems.at[7 * a + k - 1],
                device_id=peer, device_id_type=MESH)

        def landed(a, k):
            peer = _flip(me, k)
            return pltpu.make_async_remote_copy(
                src_ref=ins[a].at[_slot(peer)], dst_ref=outs[a].at[_slot(peer)],
                send_sem=send_sems.at[7 * a + k - 1], recv_sem=recv_sems.at[7 * a + k - 1],
                device_id=peer, device_id_type=MESH)

        sent = [copy(a, k) for k in range(1, 8) for a in range(n)]
        for cp in sent:
            cp.start()
        for k in range(1, 8):
            for a in range(n):
                landed(a, k).wait_recv()
        for cp in sent:
            cp.wait_send()
        for cp in mine:
            cp.wait()

    any_spec = pl.BlockSpec(memory_space=pl.ANY)
    return pl.pallas_call(
        body,
        name=name,
        in_specs=[any_spec] * n,
        out_specs=[any_spec] * n,
        out_shape=[jax.ShapeDtypeStruct(b.shape, b.dtype) for b in blocks],
        scratch_shapes=[
            pltpu.SemaphoreType.DMA((7 * n,)),
            pltpu.SemaphoreType.DMA((7 * n,)),
            pltpu.SemaphoreType.DMA((n,)),
        ],
    )(*blocks)


def _adamw_math(w, g, m, v):
    m2 = ADAM_B1 * m + (1.0 - ADAM_B1) * g
    v2 = ADAM_B2 * v + (1.0 - ADAM_B2) * (g * g)
    m_hat = m2 / (1.0 - ADAM_B1 ** ADAM_STEP)
    v_hat = v2 / (1.0 - ADAM_B2 ** ADAM_STEP)
    delta = -ADAM_LR * (m_hat / (jnp.sqrt(v_hat) + ADAM_EPS) + ADAM_WD * w)
    return delta, m2, v2


def _adamw(w, parts, m, v, name):
    R, C = w.shape
    br = 128 if R % 128 == 0 else R

    def body(w_ref, p_ref, m_ref, v_ref, g_ref, d_ref, nm_ref, nv_ref):
        g = p_ref[0].astype(F32)
        for s in range(1, N_DEV):
            g = g + p_ref[s].astype(F32)
        g_ref[...] = g
        d_ref[...], nm_ref[...], nv_ref[...] = _adamw_math(w_ref[...], g, m_ref[...], v_ref[...])

    spec = pl.BlockSpec((br, C), lambda i: (i, 0))
    return pl.pallas_call(
        body,
        name=name,
        grid=(R // br,),
        in_specs=[spec, pl.BlockSpec((N_DEV, br, C), lambda i: (0, i, 0)), spec, spec],
        out_specs=[spec] * 4,
        out_shape=[jax.ShapeDtypeStruct((R, C), F32)] * 4,
        compiler_params=_cparams(("arbitrary",)),
    )(w, parts, m, v)


_SHARDED = ("w_in", "w_out", "w_mq", "w_mkv", "w_mo", "w_up", "conv_w", "w_down")
_COL_SHARDED = ("w_in", "w_mkv", "w_up", "conv_w")
_REPLICATED = ("attn_norm_g", "b_forget", "fox_out_g", "sb_out_g", "xattn_norm_g", "mem_norm_g",
               "ffn_norm_g", "conv_b", "final_norm_g")
_WEIGHTS = ("attn_norm_g", "w_in", "b_forget", "fox_out_g", "sb_out_g", "w_out", "xattn_norm_g", "mem_norm_g",
            "w_mq", "w_mkv", "w_mo", "ffn_norm_g", "w_up", "conv_w", "conv_b", "w_down", "final_norm_g")


def _pack_rows(n):
    return -(-n // 128)


def _pack(vals, rows_total):
    parts = []
    for v in vals:
        flat = v.reshape(-1)
        parts.append(jnp.pad(flat, (0, _pack_rows(flat.shape[0]) * 128 - flat.shape[0])))
    flat = jnp.concatenate(parts)
    return jnp.pad(flat, (0, rows_total * 128 - flat.shape[0])).reshape(rows_total, 128)


def _unpack(packed, shapes):
    out = []
    r = 0
    for shp in shapes:
        n = 1
        for d in shp:
            n *= d
        out.append(packed[r:r + _pack_rows(n)].reshape(-1)[:n].reshape(shp))
        r += _pack_rows(n)
    return out


def _gathered_full(name, gathered):
    if name in _COL_SHARDED:
        return jnp.transpose(gathered, (1, 0, 2)).reshape(gathered.shape[1], -1)
    return gathered.reshape(-1, gathered.shape[2])


def _to_blocks(name, full):
    if name in _COL_SHARDED:
        r = full.shape[0]
        return jnp.transpose(full.reshape(r, N_DEV, -1), (1, 0, 2))
    return full.reshape(N_DEV, -1, full.shape[1])


def _step(args, tm, tq):
    w = {n: args[n] for n in _WEIGHTS}
    mom = {n: args["m_" + n] for n in _WEIGHTS}
    var = {n: args["v_" + n] for n in _WEIGHTS}
    x = args["x"][0]
    mem = args["mem"][0]
    target = args["loss_target"][0]

    def flat2(a):
        return a.reshape(a.shape[-2], a.shape[-1]) if a.ndim == 3 else a.reshape(1, -1)

    shards = [flat2(w[n]) if n == "conv_w" else flat2(w[n]).astype(BF16) for n in _SHARDED]
    gathered = _all_gather(shards, "gather_weights")
    p = {n: _gathered_full(n, gv) for n, gv in zip(_SHARDED, gathered)}
    for n in _REPLICATED:
        p[n] = flat2(w[n])

    loss_blk, grad_x, g = _local_step(x, mem, target, p, tm, tq)

    parts = _all_to_all([_to_blocks(n, g[n]) if n == "conv_w" else _to_blocks(n, g[n]).astype(BF16)
                         for n in _SHARDED], "scatter_grads")
    out = {}
    for n, pr in zip(_SHARDED, parts):
        res = _adamw(flat2(w[n]), pr, flat2(mom[n]), flat2(var[n]), "adamw_" + n)
        out[n] = [r.reshape(w[n].shape) for r in res]

    shapes = [w[n].shape for n in _REPLICATED]
    rows = sum(_pack_rows(flat2(w[n]).shape[1]) for n in _REPLICATED) + 1
    rows = -(-rows // 8) * 8
    g_pack = _pack([g[n] for n in _REPLICATED] + [loss_blk[0:1, :]], rows)
    (g_all,) = _all_gather([g_pack], "gather_small")
    res = _adamw(_pack([w[n] for n in _REPLICATED], rows), g_all,
                 _pack([mom[n] for n in _REPLICATED], rows), _pack([var[n] for n in _REPLICATED], rows),
                 "adamw_small")
    n_rows_params = sum(_pack_rows(flat2(w[n]).shape[1]) for n in _REPLICATED)
    loss = res[0][n_rows_params, 0]
    unpacked = [_unpack(r, shapes) for r in res]
    for k, n in enumerate(_REPLICATED):
        out[n] = [unpacked[q][k] for q in range(4)]

    grads = [out[n][0] for n in _WEIGHTS]
    deltas = [out[n][1] for n in _WEIGHTS]
    new_m = [out[n][2] for n in _WEIGHTS]
    new_v = [out[n][3] for n in _WEIGHTS]
    return (loss, grad_x[None], *grads, *deltas, *new_m, *new_v)


def kernel(x, mem, attn_norm_g, w_in, b_forget, fox_out_g, sb_out_g, w_out, xattn_norm_g, mem_norm_g, w_mq, w_mkv, w_mo, ffn_norm_g, w_up, conv_w, conv_b, w_down, final_norm_g, loss_target, m_attn_norm_g, m_w_in, m_b_forget, m_fox_out_g, m_sb_out_g, m_w_out, m_xattn_norm_g, m_mem_norm_g, m_w_mq, m_w_mkv, m_w_mo, m_ffn_norm_g, m_w_up, m_conv_w, m_conv_b, m_w_down, m_final_norm_g, v_attn_norm_g, v_w_in, v_b_forget, v_fox_out_g, v_sb_out_g, v_w_out, v_xattn_norm_g, v_mem_norm_g, v_w_mq, v_w_mkv, v_w_mo, v_ffn_norm_g, v_w_up, v_conv_w, v_conv_b, v_w_down, v_final_norm_g):
    args = dict(locals())
    T = x.shape[1]
    return _step(args, tm=min(T, 512), tq=min(T, 256))
```

```python
import functools

import jax
import jax.numpy as jnp
from jax import lax
from jax.experimental import pallas as pl
from jax.experimental.pallas import tpu as pltpu

F32 = jnp.float32
BF16 = jnp.bfloat16
EPS = 1e-6
NEG = -1e30

HEAD_DIM = 64
N_FOX = 8
FOX_W = 512
QKV_W = 3072
IN_PAD = 3200
N_MEM_HEADS = 4
MEM_HD = 256
D_FF = 2816
FF_CHUNK = 256
N_DEV = 8

ADAM_LR = 0.001
ADAM_B1 = 0.9
ADAM_B2 = 0.999
ADAM_EPS = 1e-08
ADAM_WD = 0.01
ADAM_STEP = 10

SB_SUM_TERMS = 1

VMEM_LIMIT = 56 * 1024 * 1024
MESH = pl.DeviceIdType.MESH


def _cparams(sem=None):
    return pltpu.CompilerParams(dimension_semantics=sem, vmem_limit_bytes=VMEM_LIMIT)


def _nt(a, b):
    return lax.dot_general(a, b, (((1,), (1,)), ((), ())), preferred_element_type=F32)


def _tn(a, b):
    return lax.dot_general(a, b, (((0,), (0,)), ((), ())), preferred_element_type=F32)


def _nn(a, b):
    return jnp.dot(a, b, preferred_element_type=F32)


def _split_dot(a, m01, terms):
    out = None
    r = a
    for t in range(terms):
        p = r.astype(BF16)
        d = _nn(p, m01)
        out = d if out is None else out + d
        if t + 1 < terms:
            r = r - p.astype(F32)
    return out


def _rstd(xv):
    return lax.rsqrt(jnp.mean(xv * xv, axis=-1, keepdims=True) + EPS)


def _norm_bwd(xv, g, dh):
    r = _rstd(xv)
    xhat = xv * r
    dxhat = dh * g
    dx = r * (dxhat - xhat * jnp.mean(dxhat * xhat, axis=-1, keepdims=True))
    dg = jnp.sum(dh * xhat, axis=0, keepdims=True)
    return dx, dg


def _tile_div(n, cap):
    best = None
    for d in range(128, min(n, cap) + 1, 128):
        if n % d == 0:
            best = d
    assert best is not None, n
    return best


def _inproj_fwd(x, g1, w_qkv, w_f_t, b_f, tm):
    T, D = x.shape
    N = w_qkv.shape[1]
    H = w_f_t.shape[0]

    def body(x_ref, g_ref, w_ref, wf_ref, b_ref, proj_ref, h_ref, xf_ref, c_ref, carry_ref):
        i = pl.program_id(0)

        @pl.when(i == 0)
        def _():
            carry_ref[...] = jnp.zeros_like(carry_ref)

        xv = x_ref[...]
        h = (xv * _rstd(xv) * g_ref[...]).astype(BF16)
        h_ref[...] = h
        for n0 in range(0, N, 512):
            proj_ref[:, n0:n0 + 512] = _nn(h, w_ref[:, n0:n0 + 512]).astype(BF16)
        xf = _nt(wf_ref[...], h) + b_ref[...]
        xf_ref[...] = xf
        logf = jnp.minimum(xf, 0.0) - jnp.log1p(jnp.exp(-jnp.abs(xf)))
        row = lax.broadcasted_iota(jnp.int32, (tm, tm), 0)
        col = lax.broadcasted_iota(jnp.int32, (tm, tm), 1)
        upper = jnp.where(row <= col, 1.0, 0.0).astype(BF16)
        c = _split_dot(logf, upper, 3) + carry_ref[...]
        c_ref[...] = c
        carry_ref[...] = c[:, tm - 1:tm]

    return pl.pallas_call(
        body,
        name="inproj_fwd",
        grid=(T // tm,),
        in_specs=[
            pl.BlockSpec((tm, D), lambda i: (i, 0)),
            pl.BlockSpec((1, D), lambda i: (0, 0)),
            pl.BlockSpec((D, N), lambda i: (0, 0)),
            pl.BlockSpec((H, D), lambda i: (0, 0)),
            pl.BlockSpec((H, 1), lambda i: (0, 0)),
        ],
        out_specs=[
            pl.BlockSpec((tm, N), lambda i: (i, 0)),
            pl.BlockSpec((tm, D), lambda i: (i, 0)),
            pl.BlockSpec((H, tm), lambda i: (0, i)),
            pl.BlockSpec((H, tm), lambda i: (0, i)),
        ],
        out_shape=[
            jax.ShapeDtypeStruct((T, N), BF16),
            jax.ShapeDtypeStruct((T, D), BF16),
            jax.ShapeDtypeStruct((H, T), F32),
            jax.ShapeDtypeStruct((H, T), F32),
        ],
        scratch_shapes=[pltpu.VMEM((H, 1), F32)],
        compiler_params=_cparams(("arbitrary",)),
    )(x, g1, w_qkv, w_f_t, b_f)


def _head_q(q, hh, lane):
    hmask = (lane >= HEAD_DIM * hh) & (lane < HEAD_DIM * (hh + 1))
    qh = jnp.where(hmask, q.astype(F32), 0.0) * (HEAD_DIM ** -0.5)
    return qh.astype(BF16), hmask


def _pipeline3(n, stage_a, stage_b, stage_c, diag_last):
    stage_a(0, 0)
    if diag_last:
        @pl.when(n == 1)
        def _():
            stage_b(0, 0, True)

        @pl.when(n >= 2)
        def _():
            stage_b(0, 0, False)
    else:
        stage_b(0, 0, True)

    @pl.when(n >= 2)
    def _():
        stage_a(1, 1)

    def pair(m, carry):
        t = 2 + 2 * m
        stage_c(t - 2)
        stage_b(t - 1, 1, False)
        stage_a(t, 0)
        stage_c(t - 1)
        stage_b(t, 0, False)
        stage_a(t + 1, 1)
        return carry

    lax.fori_loop(0, (n - 2) // 2, pair, 0)
    odd = n % 2 == 1

    @pl.when((n >= 3) & odd)
    def _():
        stage_c(n - 3)
        stage_b(n - 2, 1, False)
        stage_a(n - 1, 0)

    @pl.when((n >= 2) & odd)
    def _():
        stage_c(n - 2)
        stage_b(n - 1, 0, diag_last)

    @pl.when((n >= 2) & jnp.logical_not(odd))
    def _():
        stage_c(n - 2)
        stage_b(n - 1, 1, diag_last)

    stage_c(n - 1)


def _lanes2(x):
    return jnp.concatenate([x, x], axis=1)


def _fox_fwd(proj, c_col, c_row, tq, gather=()):
    T = proj.shape[0]
    assert tq == 256
    ng = len(gather)
    nq = T // tq

    def body(*refs):
        q_ref, k_ref, v_ref, cq_ref, ck_ref = refs[:5]
        o_ref, lse_ref = refs[5 + ng:7 + ng]
        qh_s, cq_s, z_s, p_s, al_s, m_s, l_s, acc_s = refs[7 + 2 * ng:15 + 2 * ng]
        i = pl.program_id(1)
        if ng:
            pair = pl.program_id(0)
            exchange = _Gather(refs[5:5 + ng], refs[7 + ng:7 + 2 * ng], *refs[15 + 2 * ng:])

            @pl.when((pair == 0) & (i == 0))
            def _():
                exchange.start()

            @pl.when((pair == 1) & (i == 0))
            def _():
                exchange.forward()

        lane = lax.broadcasted_iota(jnp.int32, (1, 128), 1)
        row = lax.broadcasted_iota(jnp.int32, (tq, tq), 0)
        col = lax.broadcasted_iota(jnp.int32, (tq, tq), 1)
        ones = jnp.ones((tq, 128), BF16)
        q = q_ref[...]
        for hh in range(2):
            qh_s[hh] = _head_q(q, hh, lane)[0]
            cq_s[hh] = jnp.broadcast_to(cq_ref[hh], (tq, tq))
        m_s[...] = jnp.full(m_s.shape, NEG, F32)
        l_s[...] = jnp.zeros_like(l_s)
        acc_s[...] = jnp.zeros_like(acc_s)

        def rows(t):
            return pl.ds(pl.multiple_of((i - t) * tq, tq), tq)

        def stage_a(t, slot):
            k = k_ref[rows(t), :]
            for hh in range(2):
                z_s[slot, hh] = _nt(qh_s[hh], k)

        def stage_b(t, slot, diag):
            for hh in range(2):
                s = z_s[slot, hh] + cq_s[hh] - ck_ref[hh, :, rows(t)]
                if diag:
                    s = jnp.where(col <= row, s, NEG)
                m = m_s[hh]
                half = jnp.maximum(s[:, :128], s[:, 128:])
                m_new = jnp.maximum(m, jnp.max(half, axis=1, keepdims=True))
                alpha = jnp.exp(m - m_new)
                p = jnp.exp(s - _lanes2(m_new)).astype(BF16)
                l_s[hh] = alpha * l_s[hh] + _nn(p, ones)
                m_s[hh] = m_new
                al_s[hh] = alpha
                p_s[hh] = p

        def stage_c(t):
            v = v_ref[rows(t), :]
            for hh in range(2):
                acc_s[hh] = al_s[hh] * acc_s[hh] + _nn(p_s[hh], v)

        _pipeline3(i + 1, stage_a, stage_b, stage_c, False)
        l0, l1 = l_s[0], l_s[1]
        o_ref[...] = jnp.where(lane < HEAD_DIM, acc_s[0] / l0, acc_s[1] / l1)
        lse_ref[0] = (m_s[0] + jnp.log(l0))[:, 0:1]
        lse_ref[1] = (m_s[1] + jnp.log(l1))[:, 0:1]
        if ng:
            @pl.when((pair == 3) & (i == nq - 1))
            def _():
                exchange.finish()

    res = pl.pallas_call(
        body,
        name="fox_fwd",
        grid=(4, nq),
        in_specs=[
            pl.BlockSpec((tq, 128), lambda p, i: (i, p)),
            pl.BlockSpec((T, 128), lambda p, i: (0, 4 + p)),
            pl.BlockSpec((T, 128), lambda p, i: (0, 8 + p)),
            pl.BlockSpec((2, tq, 1), lambda p, i: (p, i, 0)),
            pl.BlockSpec((2, 1, T), lambda p, i: (p, 0, 0)),
        ] + [_ANY] * ng,
        out_specs=[
            pl.BlockSpec((tq, 128), lambda p, i: (i, p)),
            pl.BlockSpec((2, tq, 1), lambda p, i: (p, i, 0)),
        ] + [_ANY] * ng,
        out_shape=[
            jax.ShapeDtypeStruct((T, FOX_W), F32),
            jax.ShapeDtypeStruct((N_FOX, T, 1), F32),
        ] + _gathered_shapes(gather),
        scratch_shapes=[
            pltpu.VMEM((2, tq, 128), BF16),
            pltpu.VMEM((2, tq, tq), F32),
            pltpu.VMEM((2, 2, tq, tq), F32),
            pltpu.VMEM((2, tq, tq), BF16),
            pltpu.VMEM((2, tq, 128), F32),
            pltpu.VMEM((2, tq, 128), F32),
            pltpu.VMEM((2, tq, 128), F32),
            pltpu.VMEM((2, tq, 128), F32),
        ] + (_comm_sems(ng) if ng else []),
        compiler_params=_cparams(("arbitrary", "arbitrary")),
    )(proj, proj, proj, c_col, c_row, *gather)
    res = list(res)
    return res[0], res[1], res[2:]


def _sb_terms(z, strict):
    e = jnp.exp(-jnp.abs(z))
    L = -(jnp.maximum(z, 0.0) + jnp.log(1.0 + e))
    if strict is not None:
        L = jnp.where(strict, L, 0.0)
    return L, e


def _sb_fwd(proj, tq):
    T = proj.shape[0]

    def body(q_ref, k_ref, v_ref, o_ref, ltot_ref, qh_s, z_s, g_s, tot_s, run_s, acc_s):
        i = pl.program_id(1)
        lane = lax.broadcasted_iota(jnp.int32, (1, 128), 1)
        row = lax.broadcasted_iota(jnp.int32, (tq, tq), 0)
        col = lax.broadcasted_iota(jnp.int32, (tq, tq), 1)
        strict = col < row
        later = jnp.where(row > col, 1.0, 0.0).astype(BF16)
        q = q_ref[...]
        for hh in range(2):
            qh_s[hh] = _head_q(q, hh, lane)[0]
        run_s[...] = jnp.zeros_like(run_s)
        acc_s[...] = jnp.zeros_like(acc_s)

        def rows(t):
            return pl.ds(pl.multiple_of((i - t) * tq, tq), tq)

        def stage_a(t, slot):
            k = k_ref[rows(t), :]
            for hh in range(2):
                z_s[slot, hh] = _nt(qh_s[hh], k)

        def stage_b(t, slot, diag):
            for hh in range(2):
                z = z_s[slot, hh]
                L, _ = _sb_terms(z, strict if diag else None)
                g = z + L
                if diag:
                    g = jnp.where(strict, g, NEG)
                after = _split_dot(L, later, SB_SUM_TERMS)
                g_s[hh] = g + after
                first = L[:, 0:1]
                if SB_SUM_TERMS == 1:
                    first = first.astype(BF16).astype(F32)
                tot_s[hh] = jnp.broadcast_to(after[:, 0:1] + first, (tq, 128))

        def stage_c(t):
            v = v_ref[rows(t), :]
            for hh in range(2):
                run = run_s[hh]
                a = jnp.exp(g_s[hh] + _lanes2(run))
                acc_s[hh] += _nn(a.astype(BF16), v)
                run_s[hh] = run + tot_s[hh]

        _pipeline3(i + 1, stage_a, stage_b, stage_c, False)
        ltot_ref[0] = run_s[0][:, 0:1]
        ltot_ref[1] = run_s[1][:, 0:1]
        o_ref[...] = jnp.where(lane < HEAD_DIM, acc_s[0], acc_s[1])

    return pl.pallas_call(
        body,
        name="sb_fwd",
        grid=(4, T // tq),
        in_specs=[
            pl.BlockSpec((tq, 128), lambda p, i: (i, 12 + p)),
            pl.BlockSpec((T, 128), lambda p, i: (0, 16 + p)),
            pl.BlockSpec((T, 128), lambda p, i: (0, 20 + p)),
        ],
        out_specs=[
            pl.BlockSpec((tq, 128), lambda p, i: (i, p)),
            pl.BlockSpec((2, tq, 1), lambda p, i: (p, i, 0)),
        ],
        out_shape=[
            jax.ShapeDtypeStruct((T, FOX_W), F32),
            jax.ShapeDtypeStruct((N_FOX, T, 1), F32),
        ],
        scratch_shapes=[
            pltpu.VMEM((2, tq, 128), BF16),
            pltpu.VMEM((2, 2, tq, tq), F32),
            pltpu.VMEM((2, tq, tq), F32),
            pltpu.VMEM((2, tq, 128), F32),
            pltpu.VMEM((2, tq, 128), F32),
            pltpu.VMEM((2, tq, 128), F32),
        ],
        compiler_params=_cparams(("arbitrary", "arbitrary")),
    )(proj, proj, proj)


def _post_attn_fwd(fox_o, sb_o, gf, gs, w_out, x, tm):
    T, D = x.shape

    def body(f_ref, s_ref, gf_ref, gs_ref, w_ref, x_ref, x1_ref, mix_ref):
        f = f_ref[...]
        s = s_ref[...]
        mix_ref[:, :FOX_W] = (f * _rstd(f) * gf_ref[...]).astype(BF16)
        mix_ref[:, FOX_W:] = (s * _rstd(s) * gs_ref[...]).astype(BF16)
        x1_ref[...] = x_ref[...] + _nn(mix_ref[...], w_ref[...])

    return pl.pallas_call(
        body,
        name="post_attn_fwd",
        grid=(T // tm,),
        in_specs=[
            pl.BlockSpec((tm, FOX_W), lambda i: (i, 0)),
            pl.BlockSpec((tm, FOX_W), lambda i: (i, 0)),
            pl.BlockSpec((1, FOX_W), lambda i: (0, 0)),
            pl.BlockSpec((1, FOX_W), lambda i: (0, 0)),
            pl.BlockSpec((D, D), lambda i: (0, 0)),
            pl.BlockSpec((tm, D), lambda i: (i, 0)),
        ],
        out_specs=[
            pl.BlockSpec((tm, D), lambda i: (i, 0)),
            pl.BlockSpec((tm, D), lambda i: (i, 0)),
        ],
        out_shape=[jax.ShapeDtypeStruct((T, D), F32), jax.ShapeDtypeStruct((T, D), BF16)],
        compiler_params=_cparams(("arbitrary",)),
    )(fox_o, sb_o, gf, gs, w_out, x)


def _mem_kv_fwd(mem, gm, w_mkv):
    M, D = mem.shape
    N = w_mkv.shape[1]

    def body(mem_ref, g_ref, w_ref, m_ref, kv_ref):
        mv = mem_ref[...]
        m = (mv * _rstd(mv) * g_ref[...]).astype(BF16)
        m_ref[...] = m
        for n0 in range(0, N, 512):
            kv_ref[:, n0:n0 + 512] = _nn(m, w_ref[:, n0:n0 + 512]).astype(BF16)

    return pl.pallas_call(
        body,
        name="mem_kv_fwd",
        out_shape=[jax.ShapeDtypeStruct((M, D), BF16), jax.ShapeDtypeStruct((M, N), BF16)],
        compiler_params=_cparams(),
    )(mem, gm, w_mkv)


def _xattn_probs(qb, kv, h):
    k = kv[:, h * MEM_HD:(h + 1) * MEM_HD]
    s = _nt(qb[:, h * MEM_HD:(h + 1) * MEM_HD], k) * (MEM_HD ** -0.5)
    s = s - jnp.max(s, axis=1, keepdims=True)
    p = jnp.exp(s)
    return p / jnp.sum(p, axis=1, keepdims=True)


def _xattn_fwd(x1, g2, w_mq, kv, w_mo, tm):
    T, D = x1.shape
    M = kv.shape[0]

    def body(x_ref, g_ref, wq_ref, kv_ref, wo_ref, x2_ref, h_ref, q_ref, om_ref):
        xv = x_ref[...]
        h = (xv * _rstd(xv) * g_ref[...]).astype(BF16)
        h_ref[...] = h
        q_ref[...] = _nn(h, wq_ref[...]).astype(BF16)
        qb = q_ref[...]
        kvv = kv_ref[...]
        for hd in range(N_MEM_HEADS):
            p = _xattn_probs(qb, kvv, hd)
            v = kvv[:, D + hd * MEM_HD:D + (hd + 1) * MEM_HD]
            om_ref[:, hd * MEM_HD:(hd + 1) * MEM_HD] = _nn(p.astype(BF16), v).astype(BF16)
        x2_ref[...] = xv + _nn(om_ref[...], wo_ref[...])

    return pl.pallas_call(
        body,
        name="xattn_fwd",
        grid=(T // tm,),
        in_specs=[
            pl.BlockSpec((tm, D), lambda i: (i, 0)),
            pl.BlockSpec((1, D), lambda i: (0, 0)),
            pl.BlockSpec((D, D), lambda i: (0, 0)),
            pl.BlockSpec((M, 2 * D), lambda i: (0, 0)),
            pl.BlockSpec((D, D), lambda i: (0, 0)),
        ],
        out_specs=[pl.BlockSpec((tm, D), lambda i: (i, 0))] * 4,
        out_shape=[jax.ShapeDtypeStruct((T, D), F32)] + [jax.ShapeDtypeStruct((T, D), BF16)] * 3,
        compiler_params=_cparams(("arbitrary",)),
    )(x1, g2, w_mq, kv, w_mo)


def _conv_taps(ext_ref, tm, back):
    if back:
        return ext_ref[pl.ds(6, tm), :], ext_ref[pl.ds(7, tm), :], ext_ref[pl.ds(8, tm), :]
    return ext_ref[pl.ds(0, tm), :], ext_ref[pl.ds(1, tm), :], ext_ref[pl.ds(2, tm), :]


def _ffn_fwd(x2, g3, w_up, conv_w, conv_b, w_down, tm):
    T, D = x2.shape
    fc = FF_CHUNK
    nj = D_FF // fc

    def body(x_ref, g_ref, wg_ref, wv_ref, cwg_ref, cwv_ref, cbg_ref, cbv_ref, wd_ref,
             x3_ref, h_ref, ug_ref, uv_ref, a_ref, acc_ref, carry_ref, ext_ref):
        i = pl.program_id(0)
        j = pl.program_id(1)

        @pl.when(j == 0)
        def _():
            xv = x_ref[...]
            h_ref[...] = (xv * _rstd(xv) * g_ref[...]).astype(BF16)
            acc_ref[...] = xv

        @pl.when(i == 0)
        def _():
            carry_ref[j] = jnp.zeros((2, 8, fc), F32)

        h = h_ref[...]
        halves = []
        for part, (w_ref, cw_ref, cb_ref, u_ref) in enumerate(
                ((wg_ref, cwg_ref, cbg_ref, ug_ref), (wv_ref, cwv_ref, cbv_ref, uv_ref))):
            u = _nn(h, w_ref[...])
            u_ref[...] = u
            ext = ext_ref.at[part]
            ext[pl.ds(0, 8), :] = carry_ref[j, part]
            ext[pl.ds(8, tm), :] = u
            carry_ref[j, part] = u[tm - 8:, :]
            u2, u1, u0 = _conv_taps(ext, tm, True)
            cw = cw_ref[...]
            halves.append(cb_ref[...] + cw[0:1] * u2 + cw[1:2] * u1 + cw[2:3] * u0)
        gate, val = halves
        a = (gate * jax.nn.sigmoid(gate) * val).astype(BF16)
        a_ref[...] = a
        acc_ref[...] += _nn(a, wd_ref[...])

        @pl.when(j == nj - 1)
        def _():
            x3_ref[...] = acc_ref[...]

    return pl.pallas_call(
        body,
        name="ffn_fwd",
        grid=(T // tm, nj),
        in_specs=[
            pl.BlockSpec((tm, D), lambda i, j: (i, 0)),
            pl.BlockSpec((1, D), lambda i, j: (0, 0)),
            pl.BlockSpec((D, fc), lambda i, j: (0, j)),
            pl.BlockSpec((D, fc), lambda i, j: (0, nj + j)),
            pl.BlockSpec((3, fc), lambda i, j: (0, j)),
            pl.BlockSpec((3, fc), lambda i, j: (0, nj + j)),
            pl.BlockSpec((1, fc), lambda i, j: (0, j)),
            pl.BlockSpec((1, fc), lambda i, j: (0, nj + j)),
            pl.BlockSpec((fc, D), lambda i, j: (j, 0)),
        ],
        out_specs=[
            pl.BlockSpec((tm, D), lambda i, j: (i, 0)),
            pl.BlockSpec((tm, D), lambda i, j: (i, 0)),
            pl.BlockSpec((tm, fc), lambda i, j: (i, j)),
            pl.BlockSpec((tm, fc), lambda i, j: (i, j)),
            pl.BlockSpec((tm, fc), lambda i, j: (i, j)),
        ],
        out_shape=[
            jax.ShapeDtypeStruct((T, D), F32),
            jax.ShapeDtypeStruct((T, D), BF16),
            jax.ShapeDtypeStruct((T, D_FF), F32),
            jax.ShapeDtypeStruct((T, D_FF), F32),
            jax.ShapeDtypeStruct((T, D_FF), BF16),
        ],
        scratch_shapes=[
            pltpu.VMEM((tm, D), F32),
            pltpu.VMEM((nj, 2, 8, fc), F32),
            pltpu.VMEM((2, tm + 8, fc), F32),
        ],
        compiler_params=_cparams(("arbitrary", "arbitrary")),
    )(x2, g3, w_up, w_up, conv_w, conv_w, conv_b, conv_b, w_down)


def _loss_head(x3, gfin, target, tm):
    T, D = x3.shape

    def body(x_ref, g_ref, t_ref, dx_ref, loss_ref, dg_ref):
        i = pl.program_id(0)

        @pl.when(i == 0)
        def _():
            loss_ref[...] = jnp.zeros_like(loss_ref)
            dg_ref[...] = jnp.zeros_like(dg_ref)

        xv = x_ref[...]
        g = g_ref[...]
        r = _rstd(xv)
        xhat = xv * r
        err = xhat * g - t_ref[...]
        part = jnp.sum(jnp.sum(err * err, axis=1, keepdims=True), axis=0, keepdims=True) * (0.5 / D)
        loss_ref[...] += jnp.broadcast_to(part, loss_ref.shape)
        dy = err * (1.0 / D)
        dg_ref[...] += jnp.sum(dy * xhat, axis=0, keepdims=True)
        dxhat = dy * g
        dx_ref[...] = r * (dxhat - xhat * jnp.mean(dxhat * xhat, axis=-1, keepdims=True))

    return pl.pallas_call(
        body,
        name="loss_head",
        grid=(T // tm,),
        in_specs=[
            pl.BlockSpec((tm, D), lambda i: (i, 0)),
            pl.BlockSpec((1, D), lambda i: (0, 0)),
            pl.BlockSpec((tm, D), lambda i: (i, 0)),
        ],
        out_specs=[
            pl.BlockSpec((tm, D), lambda i: (i, 0)),
            pl.BlockSpec((8, 128), lambda i: (0, 0)),
            pl.BlockSpec((1, D), lambda i: (0, 0)),
        ],
        out_shape=[
            jax.ShapeDtypeStruct((T, D), F32),
            jax.ShapeDtypeStruct((8, 128), F32),
            jax.ShapeDtypeStruct((1, D), F32),
        ],
        compiler_params=_cparams(("arbitrary",)),
    )(x3, gfin, target)


def _ffn_bwd(dx3, x2, g3, ug, uv, conv_w, conv_b, w_down_t, w_up_t, tm):
    T, D = x2.shape
    fc = FF_CHUNK
    nj = D_FF // fc
    nt = T // tm
    hb = tm // 8

    def rev(i):
        return nt - 1 - i

    def body(dx3_ref, x_ref, g_ref, ug_ref, uv_ref, ugh_ref, uvh_ref, cwg_ref, cwv_ref, cbg_ref, cbv_ref,
             wdt_ref, wutg_ref, wutv_ref,
             dx2_ref, dug_ref, duv_ref, dg_ref, dcg_ref, dcv_ref,
             acc_ref, carry_ref, ext_ref):
        i = pl.program_id(0)
        j = pl.program_id(1)
        first_tile = i == nt - 1
        cols = pl.ds(pl.multiple_of(j * fc, fc), fc)

        @pl.when(j == 0)
        def _():
            acc_ref[...] = jnp.zeros_like(acc_ref)

        @pl.when((i == 0) & (j == 0))
        def _():
            dg_ref[...] = jnp.zeros_like(dg_ref)
            dcg_ref[...] = jnp.zeros_like(dcg_ref)
            dcv_ref[...] = jnp.zeros_like(dcv_ref)

        @pl.when(i == 0)
        def _():
            carry_ref[j] = jnp.zeros((2, 8, fc), F32)

        da = _nn(dx3_ref[...].astype(BF16), wdt_ref[...])
        pre = []
        for part, (u_ref, uh_ref, cw_ref, cb_ref) in enumerate(
                ((ug_ref, ugh_ref, cwg_ref, cbg_ref), (uv_ref, uvh_ref, cwv_ref, cbv_ref))):
            ext = ext_ref.at[part]
            ext[pl.ds(0, 8), :] = jnp.where(first_tile, 0.0, uh_ref[...])
            ext[pl.ds(8, tm), :] = u_ref[...]
            u2, u1, u0 = _conv_taps(ext, tm, True)
            cw = cw_ref[...]
            pre.append(cb_ref[...] + cw[0:1] * u2 + cw[1:2] * u1 + cw[2:3] * u0)
        gate, val = pre
        sig = jax.nn.sigmoid(gate)
        silu = gate * sig
        dys = (da * val * (sig * (1.0 + gate * (1.0 - sig))), da * silu)
        for part, (dy, cw_ref, du_ref, wut_ref, dc_ref) in enumerate(
                ((dys[0], cwg_ref, dug_ref, wutg_ref, dcg_ref), (dys[1], cwv_ref, duv_ref, wutv_ref, dcv_ref))):
            u2, u1, u0 = _conv_taps(ext_ref.at[part], tm, True)
            upd = jnp.concatenate([
                jnp.sum(u2 * dy, axis=0, keepdims=True),
                jnp.sum(u1 * dy, axis=0, keepdims=True),
                jnp.sum(u0 * dy, axis=0, keepdims=True),
                jnp.sum(dy, axis=0, keepdims=True),
                jnp.zeros((4, fc), F32)], axis=0)
            dc_ref[:, cols] += upd
            ext = ext_ref.at[2 + part]
            ext[pl.ds(0, tm), :] = dy
            ext[pl.ds(tm, 8), :] = carry_ref[j, part]
            carry_ref[j, part] = dy[:8, :]
            d0, d1, d2 = _conv_taps(ext, tm, False)
            cw = cw_ref[...]
            du = (cw[2:3] * d0 + cw[1:2] * d1 + cw[0:1] * d2).astype(BF16)
            du_ref[...] = du
            acc_ref[...] += _nn(du, wut_ref[...])

        @pl.when(j == nj - 1)
        def _():
            dx, dg = _norm_bwd(x_ref[...], g_ref[...], acc_ref[...])
            dx2_ref[...] = dx3_ref[...] + dx
            dg_ref[...] += dg

    return pl.pallas_call(
        body,
        name="ffn_bwd",
        grid=(nt, nj),
        in_specs=[
            pl.BlockSpec((tm, D), lambda i, j: (rev(i), 0)),
            pl.BlockSpec((tm, D), lambda i, j: (rev(i), 0)),
            pl.BlockSpec((1, D), lambda i, j: (0, 0)),
            pl.BlockSpec((tm, fc), lambda i, j: (rev(i), j)),
            pl.BlockSpec((tm, fc), lambda i, j: (rev(i), j)),
            pl.BlockSpec((8, fc), lambda i, j: (jnp.maximum(rev(i) * hb - 1, 0), j)),
            pl.BlockSpec((8, fc), lambda i, j: (jnp.maximum(rev(i) * hb - 1, 0), j)),
            pl.BlockSpec((3, fc), lambda i, j: (0, j)),
            pl.BlockSpec((3, fc), lambda i, j: (0, nj + j)),
            pl.BlockSpec((1, fc), lambda i, j: (0, j)),
            pl.BlockSpec((1, fc), lambda i, j: (0, nj + j)),
            pl.BlockSpec((D, fc), lambda i, j: (0, j)),
            pl.BlockSpec((fc, D), lambda i, j: (j, 0)),
            pl.BlockSpec((fc, D), lambda i, j: (nj + j, 0)),
        ],
        out_specs=[
            pl.BlockSpec((tm, D), lambda i, j: (rev(i), 0)),
            pl.BlockSpec((tm, fc), lambda i, j: (rev(i), j)),
            pl.BlockSpec((tm, fc), lambda i, j: (rev(i), j)),
            pl.BlockSpec((1, D), lambda i, j: (0, 0)),
            pl.BlockSpec((8, D_FF), lambda i, j: (0, 0)),
            pl.BlockSpec((8, D_FF), lambda i, j: (0, 0)),
        ],
        out_shape=[
            jax.ShapeDtypeStruct((T, D), F32),
            jax.ShapeDtypeStruct((T, D_FF), BF16),
            jax.ShapeDtypeStruct((T, D_FF), BF16),
            jax.ShapeDtypeStruct((1, D), F32),
            jax.ShapeDtypeStruct((8, D_FF), F32),
            jax.ShapeDtypeStruct((8, D_FF), F32),
        ],
        scratch_shapes=[
            pltpu.VMEM((tm, D), F32),
            pltpu.VMEM((nj, 2, 8, fc), F32),
            pltpu.VMEM((4, tm + 8, fc), F32),
        ],
        compiler_params=_cparams(("arbitrary", "arbitrary")),
    )(dx3, x2, g3, ug, uv, ug, uv, conv_w, conv_w, conv_b, conv_b, w_down_t, w_up_t, w_up_t)


def _xattn_bwd(dx2, x1, g2, qb, kv, w_mo_t, w_mq_t, tm):
    T, D = x1.shape
    M = kv.shape[0]

    def body(dx2_ref, x_ref, g_ref, q_ref, kv_ref, wot_ref, wqt_ref, dx1_ref, dq_ref, dkv_ref, dg_ref):
        i = pl.program_id(0)

        @pl.when(i == 0)
        def _():
            dkv_ref[...] = jnp.zeros_like(dkv_ref)
            dg_ref[...] = jnp.zeros_like(dg_ref)

        dxv = dx2_ref[...]
        dom = _nn(dxv.astype(BF16), wot_ref[...]).astype(BF16)
        qb_ = q_ref[...]
        kvv = kv_ref[...]
        for hd in range(N_MEM_HEADS):
            sl = slice(hd * MEM_HD, (hd + 1) * MEM_HD)
            vsl = slice(D + hd * MEM_HD, D + (hd + 1) * MEM_HD)
            p = _xattn_probs(qb_, kvv, hd)
            dp = _nt(dom[:, sl], kvv[:, vsl])
            ds = (p * (dp - jnp.sum(p * dp, axis=1, keepdims=True)) * (MEM_HD ** -0.5)).astype(BF16)
            dq_ref[:, sl] = _nn(ds, kvv[:, sl]).astype(BF16)
            dkv_ref[:, sl] += _tn(ds, qb_[:, sl])
            dkv_ref[:, vsl] += _tn(p.astype(BF16), dom[:, sl])
        dh = _nn(dq_ref[...], wqt_ref[...])
        dx, dg = _norm_bwd(x_ref[...], g_ref[...], dh)
        dx1_ref[...] = dxv + dx
        dg_ref[...] += dg

    return pl.pallas_call(
        body,
        name="xattn_bwd",
        grid=(T // tm,),
        in_specs=[
            pl.BlockSpec((tm, D), lambda i: (i, 0)),
            pl.BlockSpec((tm, D), lambda i: (i, 0)),
            pl.BlockSpec((1, D), lambda i: (0, 0)),
            pl.BlockSpec((tm, D), lambda i: (i, 0)),
            pl.BlockSpec((M, 2 * D), lambda i: (0, 0)),
            pl.BlockSpec((D, D), lambda i: (0, 0)),
            pl.BlockSpec((D, D), lambda i: (0, 0)),
        ],
        out_specs=[
            pl.BlockSpec((tm, D), lambda i: (i, 0)),
            pl.BlockSpec((tm, D), lambda i: (i, 0)),
            pl.BlockSpec((M, 2 * D), lambda i: (0, 0)),
            pl.BlockSpec((1, D), lambda i: (0, 0)),
        ],
        out_shape=[
            jax.ShapeDtypeStruct((T, D), F32),
            jax.ShapeDtypeStruct((T, D), BF16),
            jax.ShapeDtypeStruct((M, 2 * D), F32),
            jax.ShapeDtypeStruct((1, D), F32),
        ],
        compiler_params=_cparams(("arbitrary",)),
    )(dx2, x1, g2, qb, kv, w_mo_t, w_mq_t)


def _mem_kv_bwd(mem, gm, mb, dkv, w_mkv_t):
    M, D = mem.shape
    N = dkv.shape[1]

    def body(mem_ref, g_ref, m_ref, dkv_ref, wt_ref, dw_ref, dg_ref):
        dkvb = dkv_ref[...].astype(BF16)
        for n0 in range(0, N, 512):
            dw_ref[:, n0:n0 + 512] = _tn(m_ref[...], dkvb[:, n0:n0 + 512])
        dm = _nn(dkvb, wt_ref[...])
        mv = mem_ref[...]
        dg_ref[...] = jnp.sum(dm * (mv * _rstd(mv)), axis=0, keepdims=True)

    return pl.pallas_call(
        body,
        name="mem_kv_bwd",
        out_shape=[jax.ShapeDtypeStruct((D, N), F32), jax.ShapeDtypeStruct((1, D), F32)],
        compiler_params=_cparams(),
    )(mem, gm, mb, dkv, w_mkv_t)


def _post_attn_bwd(dx1, fox_o, sb_o, gf, gs, w_out_t, tm):
    T, D = dx1.shape

    def body(dx_ref, f_ref, s_ref, gf_ref, gs_ref, wt_ref, df_ref, ds_ref, dgf_ref, dgs_ref):
        i = pl.program_id(0)

        @pl.when(i == 0)
        def _():
            dgf_ref[...] = jnp.zeros_like(dgf_ref)
            dgs_ref[...] = jnp.zeros_like(dgs_ref)

        dmix = _nn(dx_ref[...].astype(BF16), wt_ref[...])
        d, dg = _norm_bwd(f_ref[...], gf_ref[...], dmix[:, :FOX_W])
        df_ref[...] = d
        dgf_ref[...] += dg
        d, dg = _norm_bwd(s_ref[...], gs_ref[...], dmix[:, FOX_W:])
        ds_ref[...] = d
        dgs_ref[...] += dg

    return pl.pallas_call(
        body,
        name="post_attn_bwd",
        grid=(T // tm,),
        in_specs=[
            pl.BlockSpec((tm, D), lambda i: (i, 0)),
            pl.BlockSpec((tm, FOX_W), lambda i: (i, 0)),
            pl.BlockSpec((tm, FOX_W), lambda i: (i, 0)),
            pl.BlockSpec((1, FOX_W), lambda i: (0, 0)),
            pl.BlockSpec((1, FOX_W), lambda i: (0, 0)),
            pl.BlockSpec((D, D), lambda i: (0, 0)),
        ],
        out_specs=[
            pl.BlockSpec((tm, FOX_W), lambda i: (i, 0)),
            pl.BlockSpec((tm, FOX_W), lambda i: (i, 0)),
            pl.BlockSpec((1, FOX_W), lambda i: (0, 0)),
            pl.BlockSpec((1, FOX_W), lambda i: (0, 0)),
        ],
        out_shape=[
            jax.ShapeDtypeStruct((T, FOX_W), F32),
            jax.ShapeDtypeStruct((T, FOX_W), F32),
            jax.ShapeDtypeStruct((1, FOX_W), F32),
            jax.ShapeDtypeStruct((1, FOX_W), F32),
        ],
        compiler_params=_cparams(("arbitrary",)),
    )(dx1, fox_o, sb_o, gf, gs, w_out_t)


def _sb_bwd(proj, ltot, d_o, tq):
    T = proj.shape[0]

    def body(q_ref, k_ref, v_ref, lt_ref, do_ref, dq_ref, dk_ref, dv_ref,
             qh_s, doh_s, lt_s, z_s, da_s, ab_s, dzb_s, run_s, runw_s, dq_s):
        i = pl.program_id(1)

        @pl.when(i == 0)
        def _():
            dk_ref[...] = jnp.zeros_like(dk_ref)
            dv_ref[...] = jnp.zeros_like(dv_ref)

        lane = lax.broadcasted_iota(jnp.int32, (1, 128), 1)
        row = lax.broadcasted_iota(jnp.int32, (tq, tq), 0)
        col = lax.broadcasted_iota(jnp.int32, (tq, tq), 1)
        strict = col < row
        upto = jnp.where(row <= col, 1.0, 0.0).astype(BF16)
        before = jnp.where(row < col, 1.0, 0.0).astype(BF16)
        q = q_ref[...]
        dov = do_ref[...]
        for hh in range(2):
            qh, hmask = _head_q(q, hh, lane)
            qh_s[hh] = qh
            doh_s[hh] = jnp.where(hmask, dov, 0.0).astype(BF16)
            lt_s[hh] = jnp.broadcast_to(lt_ref[hh], (tq, 128))
        run_s[...] = jnp.zeros_like(run_s)
        runw_s[...] = jnp.zeros_like(runw_s)
        dq_s[...] = jnp.zeros_like(dq_s)

        def rows(t):
            return pl.ds(pl.multiple_of(t * tq, tq), tq)

        def stage_a(t, slot):
            k = k_ref[rows(t), :]
            v = v_ref[rows(t), :]
            for hh in range(2):
                z_s[slot, hh] = _nt(qh_s[hh], k)
                da_s[slot, hh] = _nt(doh_s[hh], v)

        def stage_b(t, slot, diag):
            for hh in range(2):
                z = z_s[slot, hh]
                L, e = _sb_terms(z, strict if diag else None)
                upto_s = _split_dot(L, upto, SB_SUM_TERMS)
                run = run_s[hh]
                arg = z + L + (_lanes2(lt_s[hh] - run) - upto_s)
                if diag:
                    arg = jnp.where(strict, arg, NEG)
                a = jnp.exp(arg)
                w = a * da_s[slot, hh]
                w_before = _split_dot(w, before, SB_SUM_TERMS)
                run_w = runw_s[hh]
                d_keep = w_before + _lanes2(run_w)
                r = 1.0 / (1.0 + e)
                beta = jnp.where(z >= 0.0, r, e * r)
                dz = w * (1.0 - beta) - d_keep * beta
                if diag:
                    dz = jnp.where(strict, dz, 0.0)
                dzb_s[hh] = dz.astype(BF16)
                ab_s[hh] = a.astype(BF16)
                run_s[hh] = run + jnp.broadcast_to(upto_s[:, tq - 1:tq], (tq, 128))
                runw_s[hh] = run_w + jnp.broadcast_to(w_before[:, tq - 1:tq] + w[:, tq - 1:tq], (tq, 128))

        def stage_c(t):
            k = k_ref[rows(t), :]
            dk_blk = None
            dv_blk = None
            for hh in range(2):
                dzb = dzb_s[hh]
                dq_s[hh] += _nn(dzb, k)
                dk_h = _tn(dzb, qh_s[hh])
                dv_h = _tn(ab_s[hh], doh_s[hh])
                dk_blk = dk_h if dk_blk is None else dk_blk + dk_h
                dv_blk = dv_h if dv_blk is None else dv_blk + dv_h
            dk_ref[rows(t), :] += dk_blk
            dv_ref[rows(t), :] += dv_blk

        _pipeline3(i + 1, stage_a, stage_b, stage_c, True)
        dq_ref[...] = (jnp.where(lane < HEAD_DIM, dq_s[0], dq_s[1]) * (HEAD_DIM ** -0.5)).astype(BF16)

    return pl.pallas_call(
        body,
        name="sb_bwd",
        grid=(4, T // tq),
        in_specs=[
            pl.BlockSpec((tq, 128), lambda p, i: (i, 12 + p)),
            pl.BlockSpec((T, 128), lambda p, i: (0, 16 + p)),
            pl.BlockSpec((T, 128), lambda p, i: (0, 20 + p)),
            pl.BlockSpec((2, tq, 1), lambda p, i: (p, i, 0)),
            pl.BlockSpec((tq, 128), lambda p, i: (i, p)),
        ],
        out_specs=[
            pl.BlockSpec((tq, 128), lambda p, i: (i, p)),
            pl.BlockSpec((T, 128), lambda p, i: (0, p)),
            pl.BlockSpec((T, 128), lambda p, i: (0, p)),
        ],
        out_shape=[
            jax.ShapeDtypeStruct((T, FOX_W), BF16),
            jax.ShapeDtypeStruct((T, FOX_W), F32),
            jax.ShapeDtypeStruct((T, FOX_W), F32),
        ],
        scratch_shapes=[
            pltpu.VMEM((2, tq, 128), BF16),
            pltpu.VMEM((2, tq, 128), BF16),
            pltpu.VMEM((2, tq, 128), F32),
            pltpu.VMEM((2, 2, tq, tq), F32),
            pltpu.VMEM((2, 2, tq, tq), F32),
            pltpu.VMEM((2, tq, tq), BF16),
            pltpu.VMEM((2, tq, tq), BF16),
            pltpu.VMEM((2, tq, 128), F32),
            pltpu.VMEM((2, tq, 128), F32),
            pltpu.VMEM((2, tq, 128), F32),
        ],
        compiler_params=_cparams(("arbitrary", "arbitrary")),
    )(proj, proj, proj, ltot, d_o)


def _fox_bwd(proj, c_col, c_row, lse, d_o, o, tq, scatter=()):
    T = proj.shape[0]
    ns = len(scatter)
    nq = T // tq

    def body(*refs):
        q_ref, k_ref, v_ref, cq_ref, ck_ref, lse_ref, do_ref, o_ref = refs[:8]
        dq_ref, dk_ref, dv_ref, dck_ref, dcq_ref = refs[8 + ns:13 + ns]
        qh_s, doh_s, delta_s, shift_s, z_s, dp_s, pb_s, dsb_s, rs_s, dq_s = refs[13 + 2 * ns:23 + 2 * ns]
        i = pl.program_id(1)
        if ns:
            pair = pl.program_id(0)
            exchange = _Scatter(refs[8:8 + ns], refs[13 + ns:13 + 2 * ns], *refs[23 + 2 * ns:])

            @pl.when((pair == 0) & (i == 0))
            def _():
                exchange.start()

        @pl.when(i == 0)
        def _():
            dk_ref[...] = jnp.zeros_like(dk_ref)
            dv_ref[...] = jnp.zeros_like(dv_ref)
            dck_ref[...] = jnp.zeros_like(dck_ref)

        lane = lax.broadcasted_iota(jnp.int32, (1, 128), 1)
        row = lax.broadcasted_iota(jnp.int32, (tq, tq), 0)
        col = lax.broadcasted_iota(jnp.int32, (tq, tq), 1)
        q = q_ref[...]
        dov = do_ref[...]
        ov = o_ref[...]
        for hh in range(2):
            qh, hmask = _head_q(q, hh, lane)
            dohb = jnp.where(hmask, dov, 0.0).astype(BF16)
            qh_s[hh] = qh
            doh_s[hh] = dohb
            delta_s[hh] = jnp.broadcast_to(jnp.sum(dohb.astype(F32) * ov, axis=1, keepdims=True), (tq, tq))
            shift_s[hh] = jnp.broadcast_to(cq_ref[hh] - lse_ref[hh], (tq, tq))
        rs_s[...] = jnp.zeros_like(rs_s)
        dq_s[...] = jnp.zeros_like(dq_s)

        def rows(t):
            return pl.ds(pl.multiple_of((i - t) * tq, tq), tq)

        def stage_a(t, slot):
            k = k_ref[rows(t), :]
            v = v_ref[rows(t), :]
            for hh in range(2):
                z_s[slot, hh] = _nt(qh_s[hh], k)
                dp_s[slot, hh] = _nt(doh_s[hh], v)

        def stage_b(t, slot, diag):
            for hh in range(2):
                s = z_s[slot, hh] + shift_s[hh] - ck_ref[hh, :, rows(t)]
                if diag:
                    s = jnp.where(col <= row, s, NEG)
                p = jnp.exp(s)
                ds = p * (dp_s[slot, hh] - delta_s[hh])
                pb_s[hh] = p.astype(BF16)
                dsb_s[hh] = ds.astype(BF16)
                dck_ref[hh, :, rows(t)] += jnp.sum(ds, axis=0, keepdims=True)
                rs_s[hh] += jnp.sum(ds, axis=1, keepdims=True)

        def stage_c(t):
            k = k_ref[rows(t), :]
            dk_blk = None
            dv_blk = None
            for hh in range(2):
                dsb = dsb_s[hh]
                dq_s[hh] += _nn(dsb, k)
                dk_h = _tn(dsb, qh_s[hh])
                dv_h = _tn(pb_s[hh], doh_s[hh])
                dk_blk = dk_h if dk_blk is None else dk_blk + dk_h
                dv_blk = dv_h if dv_blk is None else dv_blk + dv_h
            dk_ref[rows(t), :] += dk_blk
            dv_ref[rows(t), :] += dv_blk

        _pipeline3(i + 1, stage_a, stage_b, stage_c, False)
        dcq_ref[0] = rs_s[0][:, 0:1]
        dcq_ref[1] = rs_s[1][:, 0:1]
        dq_ref[...] = (jnp.where(lane < HEAD_DIM, dq_s[0], dq_s[1]) * (HEAD_DIM ** -0.5)).astype(BF16)
        if ns:
            @pl.when((pair == 3) & (i == nq - 1))
            def _():
                exchange.finish()

    res = pl.pallas_call(
        body,
        name="fox_bwd",
        grid=(4, nq),
        in_specs=[
            pl.BlockSpec((tq, 128), lambda p, i: (i, p)),
            pl.BlockSpec((T, 128), lambda p, i: (0, 4 + p)),
            pl.BlockSpec((T, 128), lambda p, i: (0, 8 + p)),
            pl.BlockSpec((2, tq, 1), lambda p, i: (p, i, 0)),
            pl.BlockSpec((2, 1, T), lambda p, i: (p, 0, 0)),
            pl.BlockSpec((2, tq, 1), lambda p, i: (p, i, 0)),
            pl.BlockSpec((tq, 128), lambda p, i: (i, p)),
            pl.BlockSpec((tq, 128), lambda p, i: (i, p)),
        ] + [_ANY] * ns,
        out_specs=[
            pl.BlockSpec((tq, 128), lambda p, i: (i, p)),
            pl.BlockSpec((T, 128), lambda p, i: (0, p)),
            pl.BlockSpec((T, 128), lambda p, i: (0, p)),
            pl.BlockSpec((2, 1, T), lambda p, i: (p, 0, 0)),
            pl.BlockSpec((2, tq, 1), lambda p, i: (p, i, 0)),
        ] + [_ANY] * ns,
        out_shape=[
            jax.ShapeDtypeStruct((T, FOX_W), BF16),
            jax.ShapeDtypeStruct((T, FOX_W), F32),
            jax.ShapeDtypeStruct((T, FOX_W), F32),
            jax.ShapeDtypeStruct((N_FOX, 1, T), F32),
            jax.ShapeDtypeStruct((N_FOX, T, 1), F32),
        ] + [jax.ShapeDtypeStruct(b.shape, b.dtype) for b in scatter],
        scratch_shapes=[
            pltpu.VMEM((2, tq, 128), BF16),
            pltpu.VMEM((2, tq, 128), BF16),
            pltpu.VMEM((2, tq, tq), F32),
            pltpu.VMEM((2, tq, tq), F32),
            pltpu.VMEM((2, 2, tq, tq), F32),
            pltpu.VMEM((2, 2, tq, tq), F32),
            pltpu.VMEM((2, tq, tq), BF16),
            pltpu.VMEM((2, tq, tq), BF16),
            pltpu.VMEM((2, tq, 128), F32),
            pltpu.VMEM((2, tq, 128), F32),
        ] + (_comm_sems(ns) if ns else []),
        compiler_params=_cparams(("arbitrary", "arbitrary")),
    )(proj, proj, proj, c_col, c_row, lse, d_o, o, *scatter)
    res = list(res)
    return (*res[:5], res[5:])


def _forget_bwd(dcq, dck, xf, tc):
    H, T = xf.shape
    nc = T // tc

    def body(dcq_ref, dck_ref, xf_ref, dxf_ref, db_ref):
        row = lax.broadcasted_iota(jnp.int32, (tc, tc), 0)
        col = lax.broadcasted_iota(jnp.int32, (tc, tc), 1)
        from_here = jnp.where(row >= col, 1.0, 0.0).astype(BF16)

        def chunk(n, carry):
            run, db = carry
            cs = pl.multiple_of((nc - 1 - n) * tc, tc)
            dc = dcq_ref[:, pl.ds(cs, tc)] - dck_ref[:, pl.ds(cs, tc)]
            dlogf = _split_dot(dc, from_here, 3) + run
            xfv = xf_ref[:, pl.ds(cs, tc)]
            dxf = dlogf * jax.nn.sigmoid(-xfv)
            dxf_ref[:, pl.ds(cs, tc)] = dxf
            return dlogf[:, 0:1], db + jnp.sum(dxf, axis=1, keepdims=True)

        _, db = lax.fori_loop(0, nc, chunk, (jnp.zeros((H, 1), F32), jnp.zeros((H, 1), F32)))
        db_ref[...] = db

    return pl.pallas_call(
        body,
        name="forget_bwd",
        out_shape=[jax.ShapeDtypeStruct((H, T), F32), jax.ShapeDtypeStruct((H, 1), F32)],
        compiler_params=_cparams(),
    )(dcq, dck, xf)


def _inproj_bwd(dproj, w_in_t, x, g1, dx1, tm):
    T, D = x.shape
    N = dproj.shape[1]

    def body(dp_ref, wt_ref, x_ref, g_ref, dx1_ref, dx_ref, dg_ref):
        i = pl.program_id(0)

        @pl.when(i == 0)
        def _():
            dg_ref[...] = jnp.zeros_like(dg_ref)

        dh = _nn(dp_ref[...], wt_ref[...])
        dx, dg = _norm_bwd(x_ref[...], g_ref[...], dh)
        dx_ref[...] = dx1_ref[...] + dx
        dg_ref[...] += dg

    return pl.pallas_call(
        body,
        name="inproj_bwd",
        grid=(T // tm,),
        in_specs=[
            pl.BlockSpec((tm, N), lambda i: (i, 0)),
            pl.BlockSpec((N, D), lambda i: (0, 0)),
            pl.BlockSpec((tm, D), lambda i: (i, 0)),
            pl.BlockSpec((1, D), lambda i: (0, 0)),
            pl.BlockSpec((tm, D), lambda i: (i, 0)),
        ],
        out_specs=[
            pl.BlockSpec((tm, D), lambda i: (i, 0)),
            pl.BlockSpec((1, D), lambda i: (0, 0)),
        ],
        out_shape=[jax.ShapeDtypeStruct((T, D), F32), jax.ShapeDtypeStruct((1, D), F32)],
        compiler_params=_cparams(("arbitrary",)),
    )(dproj, w_in_t, x, g1, dx1)


def _matmul_tn(a, b, name, cast_b=False):
    T, K = a.shape
    N = b.shape[1]
    bt = min(T, 512)
    bk = _tile_div(K, 1536)
    bn = _tile_div(N, 1536)

    def body(a_ref, b_ref, o_ref):
        @pl.when(pl.program_id(2) == 0)
        def _():
            o_ref[...] = jnp.zeros_like(o_ref)

        bv = b_ref[...]
        if cast_b:
            bv = bv.astype(BF16)
        o_ref[...] += _tn(a_ref[...], bv)

    return pl.pallas_call(
        body,
        name=name,
        grid=(K // bk, N // bn, T // bt),
        in_specs=[
            pl.BlockSpec((bt, bk), lambda k, n, t: (t, k)),
            pl.BlockSpec((bt, bn), lambda k, n, t: (t, n)),
        ],
        out_specs=pl.BlockSpec((bk, bn), lambda k, n, t: (k, n)),
        out_shape=jax.ShapeDtypeStruct((K, N), F32),
        compiler_params=_cparams(("arbitrary", "arbitrary", "arbitrary")),
    )(a, b)


def _local_step(x, mem, target, p, tm, tq, late=None):
    T, D = x.shape
    w_in = p["w_in"]
    w_qkv = w_in[:, :QKV_W]
    w_f_t = w_in[:, QKV_W:].T
    w_in_t = jnp.pad(w_in, ((0, 0), (0, IN_PAD - w_in.shape[1]))).T
    b_f = p["b_forget"].reshape(N_FOX, 1)

    proj, h1, xf, c = _inproj_fwd(x, p["attn_norm_g"], w_qkv, w_f_t, b_f, tm)
    c_col = c.reshape(N_FOX, T, 1)
    c_row = c.reshape(N_FOX, 1, T)
    if late:
        fox_o, lse, gathered = _fox_fwd(proj, c_col, c_row, tq, gather=[late[n] for n in _LATE])
        p = dict(p, **{n: _gathered_full(n, gv) for n, gv in zip(_LATE, gathered)})
    else:
        fox_o, lse, _ = _fox_fwd(proj, c_col, c_row, tq)
    sb_o, sb_ltot = _sb_fwd(proj, tq)
    x1, mixed = _post_attn_fwd(fox_o, sb_o, p["fox_out_g"], p["sb_out_g"], p["w_out"], x, tm)
    mb, kv = _mem_kv_fwd(mem, p["mem_norm_g"], p["w_mkv"])
    x2, h2, qb, om = _xattn_fwd(x1, p["xattn_norm_g"], p["w_mq"], kv, p["w_mo"], tm)
    x3, h3, ug, uv, a = _ffn_fwd(x2, p["ffn_norm_g"], p["w_up"], p["conv_w"], p["conv_b"], p["w_down"], tm)
    dx3, loss_blk, d_final_g = _loss_head(x3, p["final_norm_g"], target, tm)

    g = {"final_norm_g": d_final_g}
    dx2, du_g, du_v, g["ffn_norm_g"], dc_g, dc_v = _ffn_bwd(
        dx3, x2, p["ffn_norm_g"], ug, uv, p["conv_w"], p["conv_b"], p["w_down"].T, p["w_up"].T, tm)
    g["w_down"] = _matmul_tn(a, dx3, "dw_down", cast_b=True)
    g["w_up"] = jnp.concatenate([_matmul_tn(h3, du_g, "dw_up_gate"), _matmul_tn(h3, du_v, "dw_up_val")], axis=1)
    dconv = jnp.concatenate([dc_g, dc_v], axis=1)
    g["conv_w"] = dconv[0:3]
    g["conv_b"] = dconv[3:4]
    dx1, dq_m, dkv, g["xattn_norm_g"] = _xattn_bwd(dx2, x1, p["xattn_norm_g"], qb, kv, p["w_mo"].T, p["w_mq"].T, tm)
    g["w_mo"] = _matmul_tn(om, dx2, "dw_mo", cast_b=True)
    g["w_mq"] = _matmul_tn(h2, dq_m, "dw_mq")
    g["w_mkv"], g["mem_norm_g"] = _mem_kv_bwd(mem, p["mem_norm_g"], mb, dkv, p["w_mkv"].T)
    d_fox, d_sb, g["fox_out_g"], g["sb_out_g"] = _post_attn_bwd(
        dx1, fox_o, sb_o, p["fox_out_g"], p["sb_out_g"], p["w_out"].T, tm)
    g["w_out"] = _matmul_tn(mixed, dx1, "dw_out", cast_b=True)
    dq_s, dk_s, dv_s = _sb_bwd(proj, sb_ltot, d_sb, tq)
    if late:
        dq_f, dk_f, dv_f, dck, dcq, parts = _fox_bwd(proj, c_col, c_row, lse, d_fox, fox_o, tq,
                                                      scatter=[_grad_blocks(n, g[n]) for n in _LATE])
        g["parts"] = dict(zip(_LATE, parts))
    else:
        dq_f, dk_f, dv_f, dck, dcq, _ = _fox_bwd(proj, c_col, c_row, lse, d_fox, fox_o, tq)
    dxf, db = _forget_bwd(dcq.reshape(N_FOX, T), dck.reshape(N_FOX, T), xf, min(T, 512))
    g["b_forget"] = db.reshape(1, N_FOX)
    dproj = jnp.concatenate([
        dq_f, dk_f.astype(BF16), dv_f.astype(BF16), dq_s, dk_s.astype(BF16), dv_s.astype(BF16),
        jnp.pad(dxf.T, ((0, 0), (0, IN_PAD - QKV_W - N_FOX))).astype(BF16)], axis=1)
    grad_x, g["attn_norm_g"] = _inproj_bwd(dproj, w_in_t, x, p["attn_norm_g"], dx1, tm)
    g["w_in"] = _matmul_tn(h1, dproj, "dw_in")[:, :w_in.shape[1]]
    return loss_blk, grad_x, g


def _mesh_pos():
    return lax.axis_index("x"), lax.axis_index("y"), lax.axis_index("c")


def _flip(pos, k):
    return tuple(1 - v if (k >> b) & 1 else v for v, b in zip(pos, (2, 1, 0)))


def _slot(pos):
    return 4 * pos[0] + 2 * pos[1] + pos[2]


_CHIPS = (4, 2, 6)


def _comm_sems(n):
    return [pltpu.SemaphoreType.DMA((7 * n,)), pltpu.SemaphoreType.DMA((7 * n,)), pltpu.SemaphoreType.DMA((n,))]


class _Gather:
    def __init__(self, ins, outs, send_sems, recv_sems, local_sems):
        self.ins, self.outs, self.n = ins, outs, len(ins)
        self.send_sems, self.recv_sems, self.local_sems = send_sems, recv_sems, local_sems
        self.me = _mesh_pos()
        self.sibling = _flip(self.me, 1)

    def _copy(self, a, kk, block, to, src=None):
        rows = self.outs[a].at[_slot(block)]
        return pltpu.make_async_remote_copy(
            src_ref=rows if src is None else src, dst_ref=rows,
            send_sem=self.send_sems.at[7 * a + kk], recv_sem=self.recv_sems.at[7 * a + kk],
            device_id=to, device_id_type=MESH)

    def _mine(self):
        return [pltpu.make_async_copy(self.ins[a], self.outs[a].at[_slot(self.me)], self.local_sems.at[a])
                for a in range(self.n)]

    def _first(self):
        out = []
        for a in range(self.n):
            out.append(self._copy(a, 0, self.me, self.sibling, src=self.ins[a]))
            out += [self._copy(a, 1 + j, self.me, _flip(self.me, k), src=self.ins[a]) for j, k in enumerate(_CHIPS)]
        return out

    def _passed(self):
        return [self._copy(a, 4 + j, _flip(self.me, k), self.sibling)
                for j, k in enumerate(_CHIPS) for a in range(self.n)]

    def start(self):
        for cp in self._mine() + self._first():
            cp.start()

    def forward(self):
        for j, k in enumerate(_CHIPS):
            for a in range(self.n):
                self._copy(a, 1 + j, _flip(self.me, k), self.me).wait_recv()
                self._copy(a, 4 + j, _flip(self.me, k), self.sibling).start()

    def finish(self):
        for a in range(self.n):
            self._copy(a, 0, self.sibling, self.me).wait_recv()
            for j, k in enumerate(_CHIPS):
                self._copy(a, 4 + j, _flip(self.sibling, k), self.me).wait_recv()
        for cp in self._first() + self._passed():
            cp.wait_send()
        for cp in self._mine():
            cp.wait()


class _Scatter:
    def __init__(self, ins, outs, send_sems, recv_sems, local_sems):
        self.ins, self.outs, self.n = ins, outs, len(ins)
        self.send_sems, self.recv_sems, self.local_sems = send_sems, recv_sems, local_sems
        self.me = _mesh_pos()

    def _copy(self, a, k, landed=False):
        peer = _flip(self.me, k)
        return pltpu.make_async_remote_copy(
            src_ref=self.ins[a].at[_slot(peer)], dst_ref=self.outs[a].at[_slot(peer if landed else self.me)],
            send_sem=self.send_sems.at[7 * a + k - 1], recv_sem=self.recv_sems.at[7 * a + k - 1],
            device_id=peer, device_id_type=MESH)

    def _mine(self):
        s = _slot(self.me)
        return [pltpu.make_async_copy(self.ins[a].at[s], self.outs[a].at[s], self.local_sems.at[a])
                for a in range(self.n)]

    def start(self):
        for cp in self._mine() + [self._copy(a, k) for k in range(1, 8) for a in range(self.n)]:
            cp.start()

    def finish(self):
        for k in range(1, 8):
            for a in range(self.n):
                self._copy(a, k, landed=True).wait_recv()
        for k in range(1, 8):
            for a in range(self.n):
                self._copy(a, k).wait_send()
        for cp in self._mine():
            cp.wait()


_ANY = pl.BlockSpec(memory_space=pl.ANY)


def _gathered_shapes(shards):
    return [jax.ShapeDtypeStruct((N_DEV,) + s.shape, s.dtype) for s in shards]


def _all_gather(shards, name):
    n = len(shards)

    def body(*refs):
        g = _Gather(refs[:n], refs[n:2 * n], *refs[2 * n:])
        g.start()
        g.forward()
        g.finish()

    return pl.pallas_call(
        body, name=name, in_specs=[_ANY] * n, out_specs=[_ANY] * n,
        out_shape=_gathered_shapes(shards), scratch_shapes=_comm_sems(n),
    )(*shards)


def _all_to_all(blocks, name):
    n = len(blocks)

    def body(*refs):
        s = _Scatter(refs[:n], refs[n:2 * n], *refs[2 * n:])
        s.start()
        s.finish()

    return pl.pallas_call(
        body, name=name, in_specs=[_ANY] * n, out_specs=[_ANY] * n,
        out_shape=[jax.ShapeDtypeStruct(b.shape, b.dtype) for b in blocks], scratch_shapes=_comm_sems(n),
    )(*blocks)


def _adamw_math(w, g, m, v):
    m2 = ADAM_B1 * m + (1.0 - ADAM_B1) * g
    v2 = ADAM_B2 * v + (1.0 - ADAM_B2) * (g * g)
    m_hat = m2 / (1.0 - ADAM_B1 ** ADAM_STEP)
    v_hat = v2 / (1.0 - ADAM_B2 ** ADAM_STEP)
    delta = -ADAM_LR * (m_hat / (jnp.sqrt(v_hat) + ADAM_EPS) + ADAM_WD * w)
    return delta, m2, v2


def _adamw(w, parts, m, v, name):
    R, C = w.shape
    br = 128 if R % 128 == 0 else R

    def body(w_ref, p_ref, m_ref, v_ref, g_ref, d_ref, nm_ref, nv_ref):
        g = p_ref[0].astype(F32)
        for s in range(1, N_DEV):
            g = g + p_ref[s].astype(F32)
        g_ref[...] = g
        d_ref[...], nm_ref[...], nv_ref[...] = _adamw_math(w_ref[...], g, m_ref[...], v_ref[...])

    spec = pl.BlockSpec((br, C), lambda i: (i, 0))
    return pl.pallas_call(
        body,
        name=name,
        grid=(R // br,),
        in_specs=[spec, pl.BlockSpec((N_DEV, br, C), lambda i: (0, i, 0)), spec, spec],
        out_specs=[spec] * 4,
        out_shape=[jax.ShapeDtypeStruct((R, C), F32)] * 4,
        compiler_params=_cparams(("arbitrary",)),
    )(w, parts, m, v)


_SHARDED = ("w_in", "w_out", "w_mq", "w_mkv", "w_mo", "w_up", "conv_w", "w_down")
_LATE = _SHARDED[1:]
_COL_SHARDED = ("w_in", "w_mkv", "w_up", "conv_w")
_REPLICATED = ("attn_norm_g", "b_forget", "fox_out_g", "sb_out_g", "xattn_norm_g", "mem_norm_g",
               "ffn_norm_g", "conv_b", "final_norm_g")
_WEIGHTS = ("attn_norm_g", "w_in", "b_forget", "fox_out_g", "sb_out_g", "w_out", "xattn_norm_g", "mem_norm_g",
            "w_mq", "w_mkv", "w_mo", "ffn_norm_g", "w_up", "conv_w", "conv_b", "w_down", "final_norm_g")


def _pack_rows(n):
    return -(-n // 128)


def _pack(vals, rows_total):
    parts = []
    for v in vals:
        flat = v.reshape(-1)
        parts.append(jnp.pad(flat, (0, _pack_rows(flat.shape[0]) * 128 - flat.shape[0])))
    flat = jnp.concatenate(parts)
    return jnp.pad(flat, (0, rows_total * 128 - flat.shape[0])).reshape(rows_total, 128)


def _unpack(packed, shapes):
    out = []
    r = 0
    for shp in shapes:
        n = 1
        for d in shp:
            n *= d
        out.append(packed[r:r + _pack_rows(n)].reshape(-1)[:n].reshape(shp))
        r += _pack_rows(n)
    return out


def _gathered_full(name, gathered):
    if name in _COL_SHARDED:
        return jnp.transpose(gathered, (1, 0, 2)).reshape(gathered.shape[1], -1)
    return gathered.reshape(-1, gathered.shape[2])


def _to_blocks(name, full):
    if name in _COL_SHARDED:
        r = full.shape[0]
        return jnp.transpose(full.reshape(r, N_DEV, -1), (1, 0, 2))
    return full.reshape(N_DEV, -1, full.shape[1])


def _grad_blocks(name, full):
    blocks = _to_blocks(name, full)
    return blocks if name == "conv_w" else blocks.astype(BF16)


def _step(args, tm, tq):
    w = {n: args[n] for n in _WEIGHTS}
    mom = {n: args["m_" + n] for n in _WEIGHTS}
    var = {n: args["v_" + n] for n in _WEIGHTS}
    x = args["x"][0]
    mem = args["mem"][0]
    target = args["loss_target"][0]

    def flat2(a):
        return a.reshape(a.shape[-2], a.shape[-1]) if a.ndim == 3 else a.reshape(1, -1)

    shards = {n: flat2(w[n]) if n == "conv_w" else flat2(w[n]).astype(BF16) for n in _SHARDED}
    (w_in_all,) = _all_gather([shards["w_in"]], "gather_w_in")
    p = {"w_in": _gathered_full("w_in", w_in_all)}
    for n in _REPLICATED:
        p[n] = flat2(w[n])

    loss_blk, grad_x, g = _local_step(x, mem, target, p, tm, tq, late={n: shards[n] for n in _LATE})

    parts = dict(g["parts"])
    (parts["w_in"],) = _all_to_all([_grad_blocks("w_in", g["w_in"])], "scatter_dw_in")
    out = {}
    for n in _SHARDED:
        res = _adamw(flat2(w[n]), parts[n], flat2(mom[n]), flat2(var[n]), "adamw_" + n)
        out[n] = [r.reshape(w[n].shape) for r in res]

    shapes = [w[n].shape for n in _REPLICATED]
    rows = sum(_pack_rows(flat2(w[n]).shape[1]) for n in _REPLICATED) + 1
    rows = -(-rows // 8) * 8
    g_pack = _pack([g[n] for n in _REPLICATED] + [loss_blk[0:1, :]], rows)
    (g_all,) = _all_gather([g_pack], "gather_small")
    res = _adamw(_pack([w[n] for n in _REPLICATED], rows), g_all,
                 _pack([mom[n] for n in _REPLICATED], rows), _pack([var[n] for n in _REPLICATED], rows),
                 "adamw_small")
    n_rows_params = sum(_pack_rows(flat2(w[n]).shape[1]) for n in _REPLICATED)
    loss = res[0][n_rows_params, 0]
    unpacked = [_unpack(r, shapes) for r in res]
    for k, n in enumerate(_REPLICATED):
        out[n] = [unpacked[q][k] for q in range(4)]

    grads = [out[n][0] for n in _WEIGHTS]
    deltas = [out[n][1] for n in _WEIGHTS]
    new_m = [out[n][2] for n in _WEIGHTS]
    new_v = [out[n][3] for n in _WEIGHTS]
    return (loss, grad_x[None], *grads, *deltas, *new_m, *new_v)


def kernel(x, mem, attn_norm_g, w_in, b_forget, fox_out_g, sb_out_g, w_out, xattn_norm_g, mem_norm_g, w_mq, w_mkv, w_mo, ffn_norm_g, w_up, conv_w, conv_b, w_down, final_norm_g, loss_target, m_attn_norm_g, m_w_in, m_b_forget, m_fox_out_g, m_sb_out_g, m_w_out, m_xattn_norm_g, m_mem_norm_g, m_w_mq, m_w_mkv, m_w_mo, m_ffn_norm_g, m_w_up, m_conv_w, m_conv_b, m_w_down, m_final_norm_g, v_attn_norm_g, v_w_in, v_b_forget, v_fox_out_g, v_sb_out_g, v_w_out, v_xattn_norm_g, v_mem_norm_g, v_w_mq, v_w_mkv, v_w_mo, v_ffn_norm_g, v_w_up, v_conv_w, v_conv_b, v_w_down, v_final_norm_g):
    args = dict(locals())
    T = x.shape[1]
    return _step(args, tm=min(T, 512), tq=min(T, 256))
```

```python
import functools

import jax
import jax.numpy as jnp
from jax import lax
from jax.experimental import pallas as pl
from jax.experimental.pallas import tpu as pltpu

F32 = jnp.float32
BF16 = jnp.bfloat16
EPS = 1e-6
NEG = -1e30
LOG2E = 1.4426950408889634

HEAD_DIM = 64
N_FOX = 8
FOX_W = 512
QKV_W = 3072
IN_PAD = 3200
N_MEM_HEADS = 4
MEM_HD = 256
D_FF = 2816
FF_CHUNK = 256
N_DEV = 8

ADAM_LR = 0.001
ADAM_B1 = 0.9
ADAM_B2 = 0.999
ADAM_EPS = 1e-08
ADAM_WD = 0.01
ADAM_STEP = 10

SB_SUM_TERMS = 1

VMEM_LIMIT = 56 * 1024 * 1024
MESH = pl.DeviceIdType.MESH


def _cparams(sem=None):
    return pltpu.CompilerParams(dimension_semantics=sem, vmem_limit_bytes=VMEM_LIMIT)


def _nt(a, b):
    return lax.dot_general(a, b, (((1,), (1,)), ((), ())), preferred_element_type=F32)


def _tn(a, b):
    return lax.dot_general(a, b, (((0,), (0,)), ((), ())), preferred_element_type=F32)


def _nn(a, b):
    return jnp.dot(a, b, preferred_element_type=F32)


def _split_dot(a, m01, terms):
    out = None
    r = a
    for t in range(terms):
        p = r.astype(BF16)
        d = _nn(p, m01)
        out = d if out is None else out + d
        if t + 1 < terms:
            r = r - p.astype(F32)
    return out


def _rstd(xv):
    return lax.rsqrt(jnp.mean(xv * xv, axis=-1, keepdims=True) + EPS)


def _norm_bwd(xv, g, dh):
    r = _rstd(xv)
    xhat = xv * r
    dxhat = dh * g
    dx = r * (dxhat - xhat * jnp.mean(dxhat * xhat, axis=-1, keepdims=True))
    dg = jnp.sum(dh * xhat, axis=0, keepdims=True)
    return dx, dg


def _tile_div(n, cap):
    best = None
    for d in range(128, min(n, cap) + 1, 128):
        if n % d == 0:
            best = d
    assert best is not None, n
    return best


def _inproj_fwd(x, g1, w_qkv, w_f_t, b_f, tm):
    T, D = x.shape
    N = w_qkv.shape[1]
    H = w_f_t.shape[0]

    def body(x_ref, g_ref, w_ref, wf_ref, b_ref, proj_ref, h_ref, xf_ref, c_ref, carry_ref):
        i = pl.program_id(0)

        @pl.when(i == 0)
        def _():
            carry_ref[...] = jnp.zeros_like(carry_ref)

        xv = x_ref[...]
        h = (xv * _rstd(xv) * g_ref[...]).astype(BF16)
        h_ref[...] = h
        for n0 in range(0, N, 512):
            proj_ref[:, n0:n0 + 512] = _nn(h, w_ref[:, n0:n0 + 512]).astype(BF16)
        xf = _nt(wf_ref[...], h) + b_ref[...]
        xf_ref[...] = xf
        logf = jnp.minimum(xf, 0.0) - jnp.log1p(jnp.exp(-jnp.abs(xf)))
        row = lax.broadcasted_iota(jnp.int32, (tm, tm), 0)
        col = lax.broadcasted_iota(jnp.int32, (tm, tm), 1)
        upper = jnp.where(row <= col, 1.0, 0.0).astype(BF16)
        c = _split_dot(logf, upper, 3) + carry_ref[...]
        c_ref[...] = c
        carry_ref[...] = c[:, tm - 1:tm]

    return pl.pallas_call(
        body,
        name="inproj_fwd",
        grid=(T // tm,),
        in_specs=[
            pl.BlockSpec((tm, D), lambda i: (i, 0)),
            pl.BlockSpec((1, D), lambda i: (0, 0)),
            pl.BlockSpec((D, N), lambda i: (0, 0)),
            pl.BlockSpec((H, D), lambda i: (0, 0)),
            pl.BlockSpec((H, 1), lambda i: (0, 0)),
        ],
        out_specs=[
            pl.BlockSpec((tm, N), lambda i: (i, 0)),
            pl.BlockSpec((tm, D), lambda i: (i, 0)),
            pl.BlockSpec((H, tm), lambda i: (0, i)),
            pl.BlockSpec((H, tm), lambda i: (0, i)),
        ],
        out_shape=[
            jax.ShapeDtypeStruct((T, N), BF16),
            jax.ShapeDtypeStruct((T, D), BF16),
            jax.ShapeDtypeStruct((H, T), F32),
            jax.ShapeDtypeStruct((H, T), F32),
        ],
        scratch_shapes=[pltpu.VMEM((H, 1), F32)],
        compiler_params=_cparams(("arbitrary",)),
    )(x, g1, w_qkv, w_f_t, b_f)


def _head_q(q, hh, lane):
    hmask = (lane >= HEAD_DIM * hh) & (lane < HEAD_DIM * (hh + 1))
    qh = jnp.where(hmask, q.astype(F32), 0.0) * (HEAD_DIM ** -0.5)
    return qh.astype(BF16), hmask


def _pipeline3(n, stage_a, stage_b, stage_c, diag_last):
    stage_a(0, 0)
    if diag_last:
        @pl.when(n == 1)
        def _():
            stage_b(0, 0, True)

        @pl.when(n >= 2)
        def _():
            stage_b(0, 0, False)
    else:
        stage_b(0, 0, True)

    @pl.when(n >= 2)
    def _():
        stage_a(1, 1)

    def pair(m, carry):
        t = 2 + 2 * m
        stage_c(t - 2)
        stage_b(t - 1, 1, False)
        stage_a(t, 0)
        stage_c(t - 1)
        stage_b(t, 0, False)
        stage_a(t + 1, 1)
        return carry

    lax.fori_loop(0, (n - 2) // 2, pair, 0)
    odd = n % 2 == 1

    @pl.when((n >= 3) & odd)
    def _():
        stage_c(n - 3)
        stage_b(n - 2, 1, False)
        stage_a(n - 1, 0)

    @pl.when((n >= 2) & odd)
    def _():
        stage_c(n - 2)
        stage_b(n - 1, 0, diag_last)

    @pl.when((n >= 2) & jnp.logical_not(odd))
    def _():
        stage_c(n - 2)
        stage_b(n - 1, 1, diag_last)

    stage_c(n - 1)


def _lanes2(x):
    return jnp.concatenate([x, x], axis=1)


def _fox_fwd(proj, c_col, c_row, tq, gather=()):
    T = proj.shape[0]
    assert tq == 256
    ng = len(gather)
    nq = T // tq

    def body(*refs):
        q_ref, k_ref, v_ref, cq_ref, ck_ref = refs[:5]
        o_ref, lse_ref = refs[5 + ng:7 + ng]
        qh_s, cq_s, z_s, p_s, al_s, m_s, l_s, acc_s = refs[7 + 2 * ng:15 + 2 * ng]
        i = pl.program_id(1)
        if ng:
            pair = pl.program_id(0)
            exchange = _Gather(refs[5:5 + ng], refs[7 + ng:7 + 2 * ng], *refs[15 + 2 * ng:])

            @pl.when((pair == 0) & (i == 0))
            def _():
                exchange.start()

            @pl.when((pair == 1) & (i == 0))
            def _():
                exchange.forward()

        lane = lax.broadcasted_iota(jnp.int32, (1, 128), 1)
        row = lax.broadcasted_iota(jnp.int32, (tq, tq), 0)
        col = lax.broadcasted_iota(jnp.int32, (tq, tq), 1)
        ones = jnp.ones((tq, 128), BF16)
        q = q_ref[...]
        for hh in range(2):
            qh_s[hh] = _head_q(q, hh, lane)[0]
            cq_s[hh] = jnp.broadcast_to(cq_ref[hh], (tq, tq))
        m_s[...] = jnp.full(m_s.shape, NEG, F32)
        l_s[...] = jnp.zeros_like(l_s)
        acc_s[...] = jnp.zeros_like(acc_s)

        def rows(t):
            return pl.ds(pl.multiple_of((i - t) * tq, tq), tq)

        def stage_a(t, slot):
            k = k_ref[rows(t), :]
            for hh in range(2):
                z_s[slot, hh] = _nt(qh_s[hh], k)

        def stage_b(t, slot, diag):
            for hh in range(2):
                s = z_s[slot, hh] + cq_s[hh] - ck_ref[hh, :, rows(t)]
                if diag:
                    s = jnp.where(col <= row, s, NEG)
                m = m_s[hh]
                half = jnp.maximum(s[:, :128], s[:, 128:])
                m_new = jnp.maximum(m, jnp.max(half, axis=1, keepdims=True))
                alpha = jnp.exp(m - m_new)
                p = jnp.exp(s - _lanes2(m_new)).astype(BF16)
                l_s[hh] = alpha * l_s[hh] + _nn(p, ones)
                m_s[hh] = m_new
                al_s[hh] = alpha
                p_s[hh] = p

        def stage_c(t):
            v = v_ref[rows(t), :]
            for hh in range(2):
                acc_s[hh] = al_s[hh] * acc_s[hh] + _nn(p_s[hh], v)

        _pipeline3(i + 1, stage_a, stage_b, stage_c, False)
        l0, l1 = l_s[0], l_s[1]
        o_ref[...] = jnp.where(lane < HEAD_DIM, acc_s[0] / l0, acc_s[1] / l1)
        lse_ref[0] = (m_s[0] + jnp.log(l0))[:, 0:1]
        lse_ref[1] = (m_s[1] + jnp.log(l1))[:, 0:1]
        if ng:
            @pl.when((pair == 3) & (i == nq - 1))
            def _():
                exchange.finish()

    res = pl.pallas_call(
        body,
        name="fox_fwd",
        grid=(4, nq),
        in_specs=[
            pl.BlockSpec((tq, 128), lambda p, i: (i, p)),
            pl.BlockSpec((T, 128), lambda p, i: (0, 4 + p)),
            pl.BlockSpec((T, 128), lambda p, i: (0, 8 + p)),
            pl.BlockSpec((2, tq, 1), lambda p, i: (p, i, 0)),
            pl.BlockSpec((2, 1, T), lambda p, i: (p, 0, 0)),
        ] + [_ANY] * ng,
        out_specs=[
            pl.BlockSpec((tq, 128), lambda p, i: (i, p)),
            pl.BlockSpec((2, tq, 1), lambda p, i: (p, i, 0)),
        ] + [_ANY] * ng,
        out_shape=[
            jax.ShapeDtypeStruct((T, FOX_W), F32),
            jax.ShapeDtypeStruct((N_FOX, T, 1), F32),
        ] + _gathered_shapes(gather),
        scratch_shapes=[
            pltpu.VMEM((2, tq, 128), BF16),
            pltpu.VMEM((2, tq, tq), F32),
            pltpu.VMEM((2, 2, tq, tq), F32),
            pltpu.VMEM((2, tq, tq), BF16),
            pltpu.VMEM((2, tq, 128), F32),
            pltpu.VMEM((2, tq, 128), F32),
            pltpu.VMEM((2, tq, 128), F32),
            pltpu.VMEM((2, tq, 128), F32),
        ] + (_comm_sems(ng) if ng else []),
        compiler_params=_cparams(("arbitrary", "arbitrary")),
    )(proj, proj, proj, c_col, c_row, *gather)
    res = list(res)
    return res[0], res[1], res[2:]


def _sb_logs(zn, strict):
    e = jnp.exp2(jnp.abs(zn) * (-LOG2E))
    L = jnp.minimum(zn, 0.0) - jnp.log(1.0 + e)
    G = L - zn
    if strict is not None:
        L = jnp.where(strict, L, 0.0)
    return L, G


def _sb_fwd(proj, tq):
    T = proj.shape[0]

    def body(q_ref, k_ref, v_ref, o_ref, ltot_ref, qh_s, z_s, g_s, tot_s, run_s, acc_s):
        i = pl.program_id(1)
        lane = lax.broadcasted_iota(jnp.int32, (1, 128), 1)
        row = lax.broadcasted_iota(jnp.int32, (tq, tq), 0)
        col = lax.broadcasted_iota(jnp.int32, (tq, tq), 1)
        strict = col < row
        later = jnp.where(row > col, 1.0, 0.0).astype(BF16)
        q = q_ref[...]
        for hh in range(2):
            qh_s[hh] = -_head_q(q, hh, lane)[0]
        run_s[...] = jnp.zeros_like(run_s)
        acc_s[...] = jnp.zeros_like(acc_s)

        def rows(t):
            return pl.ds(pl.multiple_of((i - t) * tq, tq), tq)

        def stage_a(t, slot):
            k = k_ref[rows(t), :]
            for hh in range(2):
                z_s[slot, hh] = _nt(qh_s[hh], k)

        def stage_b(t, slot, diag):
            for hh in range(2):
                L, g = _sb_logs(z_s[slot, hh], strict if diag else None)
                if diag:
                    g = jnp.where(strict, g, NEG)
                after = _split_dot(L, later, SB_SUM_TERMS)
                g_s[hh] = g + after
                first = L[:, 0:1]
                if SB_SUM_TERMS == 1:
                    first = first.astype(BF16).astype(F32)
                tot_s[hh] = jnp.broadcast_to(after[:, 0:1] + first, (tq, 128))

        def stage_c(t):
            v = v_ref[rows(t), :]
            for hh in range(2):
                run = run_s[hh]
                a = jnp.exp(g_s[hh] + _lanes2(run))
                acc_s[hh] += _nn(a.astype(BF16), v)
                run_s[hh] = run + tot_s[hh]

        _pipeline3(i + 1, stage_a, stage_b, stage_c, False)
        ltot_ref[0] = run_s[0][:, 0:1]
        ltot_ref[1] = run_s[1][:, 0:1]
        o_ref[...] = jnp.where(lane < HEAD_DIM, acc_s[0], acc_s[1])

    return pl.pallas_call(
        body,
        name="sb_fwd",
        grid=(4, T // tq),
        in_specs=[
            pl.BlockSpec((tq, 128), lambda p, i: (i, 12 + p)),
            pl.BlockSpec((T, 128), lambda p, i: (0, 16 + p)),
            pl.BlockSpec((T, 128), lambda p, i: (0, 20 + p)),
        ],
        out_specs=[
            pl.BlockSpec((tq, 128), lambda p, i: (i, p)),
            pl.BlockSpec((2, tq, 1), lambda p, i: (p, i, 0)),
        ],
        out_shape=[
            jax.ShapeDtypeStruct((T, FOX_W), F32),
            jax.ShapeDtypeStruct((N_FOX, T, 1), F32),
        ],
        scratch_shapes=[
            pltpu.VMEM((2, tq, 128), BF16),
            pltpu.VMEM((2, 2, tq, tq), F32),
            pltpu.VMEM((2, tq, tq), F32),
            pltpu.VMEM((2, tq, 128), F32),
            pltpu.VMEM((2, tq, 128), F32),
            pltpu.VMEM((2, tq, 128), F32),
        ],
        compiler_params=_cparams(("arbitrary", "arbitrary")),
    )(proj, proj, proj)


def _post_attn_fwd(fox_o, sb_o, gf, gs, w_out, x, tm):
    T, D = x.shape

    def body(f_ref, s_ref, gf_ref, gs_ref, w_ref, x_ref, x1_ref, mix_ref):
        f = f_ref[...]
        s = s_ref[...]
        mix_ref[:, :FOX_W] = (f * _rstd(f) * gf_ref[...]).astype(BF16)
        mix_ref[:, FOX_W:] = (s * _rstd(s) * gs_ref[...]).astype(BF16)
        x1_ref[...] = x_ref[...] + _nn(mix_ref[...], w_ref[...])

    return pl.pallas_call(
        body,
        name="post_attn_fwd",
        grid=(T // tm,),
        in_specs=[
            pl.BlockSpec((tm, FOX_W), lambda i: (i, 0)),
            pl.BlockSpec((tm, FOX_W), lambda i: (i, 0)),
            pl.BlockSpec((1, FOX_W), lambda i: (0, 0)),
            pl.BlockSpec((1, FOX_W), lambda i: (0, 0)),
            pl.BlockSpec((D, D), lambda i: (0, 0)),
            pl.BlockSpec((tm, D), lambda i: (i, 0)),
        ],
        out_specs=[
            pl.BlockSpec((tm, D), lambda i: (i, 0)),
            pl.BlockSpec((tm, D), lambda i: (i, 0)),
        ],
        out_shape=[jax.ShapeDtypeStruct((T, D), F32), jax.ShapeDtypeStruct((T, D), BF16)],
        compiler_params=_cparams(("arbitrary",)),
    )(fox_o, sb_o, gf, gs, w_out, x)


def _mem_kv_fwd(mem, gm, w_mkv):
    M, D = mem.shape
    N = w_mkv.shape[1]

    def body(mem_ref, g_ref, w_ref, m_ref, kv_ref):
        mv = mem_ref[...]
        m = (mv * _rstd(mv) * g_ref[...]).astype(BF16)
        m_ref[...] = m
        for n0 in range(0, N, 512):
            kv_ref[:, n0:n0 + 512] = _nn(m, w_ref[:, n0:n0 + 512]).astype(BF16)

    return pl.pallas_call(
        body,
        name="mem_kv_fwd",
        out_shape=[jax.ShapeDtypeStruct((M, D), BF16), jax.ShapeDtypeStruct((M, N), BF16)],
        compiler_params=_cparams(),
    )(mem, gm, w_mkv)


def _xattn_probs(qb, kv, h):
    k = kv[:, h * MEM_HD:(h + 1) * MEM_HD]
    s = _nt(qb[:, h * MEM_HD:(h + 1) * MEM_HD], k) * (MEM_HD ** -0.5)
    s = s - jnp.max(s, axis=1, keepdims=True)
    p = jnp.exp(s)
    return p / jnp.sum(p, axis=1, keepdims=True)


def _xattn_fwd(x1, g2, w_mq, kv, w_mo, tm):
    T, D = x1.shape
    M = kv.shape[0]

    def body(x_ref, g_ref, wq_ref, kv_ref, wo_ref, x2_ref, h_ref, q_ref, om_ref):
        xv = x_ref[...]
        h = (xv * _rstd(xv) * g_ref[...]).astype(BF16)
        h_ref[...] = h
        q_ref[...] = _nn(h, wq_ref[...]).astype(BF16)
        qb = q_ref[...]
        kvv = kv_ref[...]
        for hd in range(N_MEM_HEADS):
            p = _xattn_probs(qb, kvv, hd)
            v = kvv[:, D + hd * MEM_HD:D + (hd + 1) * MEM_HD]
            om_ref[:, hd * MEM_HD:(hd + 1) * MEM_HD] = _nn(p.astype(BF16), v).astype(BF16)
        x2_ref[...] = xv + _nn(om_ref[...], wo_ref[...])

    return pl.pallas_call(
        body,
        name="xattn_fwd",
        grid=(T // tm,),
        in_specs=[
            pl.BlockSpec((tm, D), lambda i: (i, 0)),
            pl.BlockSpec((1, D), lambda i: (0, 0)),
            pl.BlockSpec((D, D), lambda i: (0, 0)),
            pl.BlockSpec((M, 2 * D), lambda i: (0, 0)),
            pl.BlockSpec((D, D), lambda i: (0, 0)),
        ],
        out_specs=[pl.BlockSpec((tm, D), lambda i: (i, 0))] * 4,
        out_shape=[jax.ShapeDtypeStruct((T, D), F32)] + [jax.ShapeDtypeStruct((T, D), BF16)] * 3,
        compiler_params=_cparams(("arbitrary",)),
    )(x1, g2, w_mq, kv, w_mo)


def _conv_taps(ext_ref, tm, back):
    if back:
        return ext_ref[pl.ds(6, tm), :], ext_ref[pl.ds(7, tm), :], ext_ref[pl.ds(8, tm), :]
    return ext_ref[pl.ds(0, tm), :], ext_ref[pl.ds(1, tm), :], ext_ref[pl.ds(2, tm), :]


def _ffn_fwd(x2, g3, w_up, conv_w, conv_b, w_down, tm):
    T, D = x2.shape
    fc = FF_CHUNK
    nj = D_FF // fc

    def body(x_ref, g_ref, wg_ref, wv_ref, cwg_ref, cwv_ref, cbg_ref, cbv_ref, wd_ref,
             x3_ref, h_ref, ug_ref, uv_ref, a_ref, acc_ref, carry_ref, ext_ref):
        i = pl.program_id(0)
        j = pl.program_id(1)

        @pl.when(j == 0)
        def _():
            xv = x_ref[...]
            h_ref[...] = (xv * _rstd(xv) * g_ref[...]).astype(BF16)
            acc_ref[...] = xv

        @pl.when(i == 0)
        def _():
            carry_ref[j] = jnp.zeros((2, 8, fc), F32)

        h = h_ref[...]
        halves = []
        for part, (w_ref, cw_ref, cb_ref, u_ref) in enumerate(
                ((wg_ref, cwg_ref, cbg_ref, ug_ref), (wv_ref, cwv_ref, cbv_ref, uv_ref))):
            u = _nn(h, w_ref[...])
            u_ref[...] = u
            ext = ext_ref.at[part]
            ext[pl.ds(0, 8), :] = carry_ref[j, part]
            ext[pl.ds(8, tm), :] = u
            carry_ref[j, part] = u[tm - 8:, :]
            u2, u1, u0 = _conv_taps(ext, tm, True)
            cw = cw_ref[...]
            halves.append(cb_ref[...] + cw[0:1] * u2 + cw[1:2] * u1 + cw[2:3] * u0)
        gate, val = halves
        a = (gate * jax.nn.sigmoid(gate) * val).astype(BF16)
        a_ref[...] = a
        acc_ref[...] += _nn(a, wd_ref[...])

        @pl.when(j == nj - 1)
        def _():
            x3_ref[...] = acc_ref[...]

    return pl.pallas_call(
        body,
        name="ffn_fwd",
        grid=(T // tm, nj),
        in_specs=[
            pl.BlockSpec((tm, D), lambda i, j: (i, 0)),
            pl.BlockSpec((1, D), lambda i, j: (0, 0)),
            pl.BlockSpec((D, fc), lambda i, j: (0, j)),
            pl.BlockSpec((D, fc), lambda i, j: (0, nj + j)),
            pl.BlockSpec((3, fc), lambda i, j: (0, j)),
            pl.BlockSpec((3, fc), lambda i, j: (0, nj + j)),
            pl.BlockSpec((1, fc), lambda i, j: (0, j)),
            pl.BlockSpec((1, fc), lambda i, j: (0, nj + j)),
            pl.BlockSpec((fc, D), lambda i, j: (j, 0)),
        ],
        out_specs=[
            pl.BlockSpec((tm, D), lambda i, j: (i, 0)),
            pl.BlockSpec((tm, D), lambda i, j: (i, 0)),
            pl.BlockSpec((tm, fc), lambda i, j: (i, j)),
            pl.BlockSpec((tm, fc), lambda i, j: (i, j)),
            pl.BlockSpec((tm, fc), lambda i, j: (i, j)),
        ],
        out_shape=[
            jax.ShapeDtypeStruct((T, D), F32),
            jax.ShapeDtypeStruct((T, D), BF16),
            jax.ShapeDtypeStruct((T, D_FF), F32),
            jax.ShapeDtypeStruct((T, D_FF), F32),
            jax.ShapeDtypeStruct((T, D_FF), BF16),
        ],
        scratch_shapes=[
            pltpu.VMEM((tm, D), F32),
            pltpu.VMEM((nj, 2, 8, fc), F32),
            pltpu.VMEM((2, tm + 8, fc), F32),
        ],
        compiler_params=_cparams(("arbitrary", "arbitrary")),
    )(x2, g3, w_up, w_up, conv_w, conv_w, conv_b, conv_b, w_down)


def _loss_head(x3, gfin, target, tm):
    T, D = x3.shape

    def body(x_ref, g_ref, t_ref, dx_ref, loss_ref, dg_ref):
        i = pl.program_id(0)

        @pl.when(i == 0)
        def _():
            loss_ref[...] = jnp.zeros_like(loss_ref)
            dg_ref[...] = jnp.zeros_like(dg_ref)

        xv = x_ref[...]
        g = g_ref[...]
        r = _rstd(xv)
        xhat = xv * r
        err = xhat * g - t_ref[...]
        part = jnp.sum(jnp.sum(err * err, axis=1, keepdims=True), axis=0, keepdims=True) * (0.5 / D)
        loss_ref[...] += jnp.broadcast_to(part, loss_ref.shape)
        dy = err * (1.0 / D)
        dg_ref[...] += jnp.sum(dy * xhat, axis=0, keepdims=True)
        dxhat = dy * g
        dx_ref[...] = r * (dxhat - xhat * jnp.mean(dxhat * xhat, axis=-1, keepdims=True))

    return pl.pallas_call(
        body,
        name="loss_head",
        grid=(T // tm,),
        in_specs=[
            pl.BlockSpec((tm, D), lambda i: (i, 0)),
            pl.BlockSpec((1, D), lambda i: (0, 0)),
            pl.BlockSpec((tm, D), lambda i: (i, 0)),
        ],
        out_specs=[
            pl.BlockSpec((tm, D), lambda i: (i, 0)),
            pl.BlockSpec((8, 128), lambda i: (0, 0)),
            pl.BlockSpec((1, D), lambda i: (0, 0)),
        ],
        out_shape=[
            jax.ShapeDtypeStruct((T, D), F32),
            jax.ShapeDtypeStruct((8, 128), F32),
            jax.ShapeDtypeStruct((1, D), F32),
        ],
        compiler_params=_cparams(("arbitrary",)),
    )(x3, gfin, target)


def _ffn_bwd(dx3, x2, g3, ug, uv, conv_w, conv_b, w_down_t, w_up_t, tm):
    T, D = x2.shape
    fc = FF_CHUNK
    nj = D_FF // fc
    nt = T // tm
    hb = tm // 8

    def rev(i):
        return nt - 1 - i

    def body(dx3_ref, x_ref, g_ref, ug_ref, uv_ref, ugh_ref, uvh_ref, cwg_ref, cwv_ref, cbg_ref, cbv_ref,
             wdt_ref, wutg_ref, wutv_ref,
             dx2_ref, dug_ref, duv_ref, dg_ref, dcg_ref, dcv_ref,
             acc_ref, carry_ref, ext_ref):
        i = pl.program_id(0)
        j = pl.program_id(1)
        first_tile = i == nt - 1
        cols = pl.ds(pl.multiple_of(j * fc, fc), fc)

        @pl.when(j == 0)
        def _():
            acc_ref[...] = jnp.zeros_like(acc_ref)

        @pl.when((i == 0) & (j == 0))
        def _():
            dg_ref[...] = jnp.zeros_like(dg_ref)
            dcg_ref[...] = jnp.zeros_like(dcg_ref)
            dcv_ref[...] = jnp.zeros_like(dcv_ref)

        @pl.when(i == 0)
        def _():
            carry_ref[j] = jnp.zeros((2, 8, fc), F32)

        da = _nn(dx3_ref[...].astype(BF16), wdt_ref[...])
        pre = []
        for part, (u_ref, uh_ref, cw_ref, cb_ref) in enumerate(
                ((ug_ref, ugh_ref, cwg_ref, cbg_ref), (uv_ref, uvh_ref, cwv_ref, cbv_ref))):
            ext = ext_ref.at[part]
            ext[pl.ds(0, 8), :] = jnp.where(first_tile, 0.0, uh_ref[...])
            ext[pl.ds(8, tm), :] = u_ref[...]
            u2, u1, u0 = _conv_taps(ext, tm, True)
            cw = cw_ref[...]
            pre.append(cb_ref[...] + cw[0:1] * u2 + cw[1:2] * u1 + cw[2:3] * u0)
        gate, val = pre
        sig = jax.nn.sigmoid(gate)
        silu = gate * sig
        dys = (da * val * (sig * (1.0 + gate * (1.0 - sig))), da * silu)
        for part, (dy, cw_ref, du_ref, wut_ref, dc_ref) in enumerate(
                ((dys[0], cwg_ref, dug_ref, wutg_ref, dcg_ref), (dys[1], cwv_ref, duv_ref, wutv_ref, dcv_ref))):
            u2, u1, u0 = _conv_taps(ext_ref.at[part], tm, True)
            upd = jnp.concatenate([
                jnp.sum(u2 * dy, axis=0, keepdims=True),
                jnp.sum(u1 * dy, axis=0, keepdims=True),
                jnp.sum(u0 * dy, axis=0, keepdims=True),
                jnp.sum(dy, axis=0, keepdims=True),
                jnp.zeros((4, fc), F32)], axis=0)
            dc_ref[:, cols] += upd
            ext = ext_ref.at[2 + part]
            ext[pl.ds(0, tm), :] = dy
            ext[pl.ds(tm, 8), :] = carry_ref[j, part]
            carry_ref[j, part] = dy[:8, :]
            d0, d1, d2 = _conv_taps(ext, tm, False)
            cw = cw_ref[...]
            du = (cw[2:3] * d0 + cw[1:2] * d1 + cw[0:1] * d2).astype(BF16)
            du_ref[...] = du
            acc_ref[...] += _nn(du, wut_ref[...])

        @pl.when(j == nj - 1)
        def _():
            dx, dg = _norm_bwd(x_ref[...], g_ref[...], acc_ref[...])
            dx2_ref[...] = dx3_ref[...] + dx
            dg_ref[...] += dg

    return pl.pallas_call(
        body,
        name="ffn_bwd",
        grid=(nt, nj),
        in_specs=[
            pl.BlockSpec((tm, D), lambda i, j: (rev(i), 0)),
            pl.BlockSpec((tm, D), lambda i, j: (rev(i), 0)),
            pl.BlockSpec((1, D), lambda i, j: (0, 0)),
            pl.BlockSpec((tm, fc), lambda i, j: (rev(i), j)),
            pl.BlockSpec((tm, fc), lambda i, j: (rev(i), j)),
            pl.BlockSpec((8, fc), lambda i, j: (jnp.maximum(rev(i) * hb - 1, 0), j)),
            pl.BlockSpec((8, fc), lambda i, j: (jnp.maximum(rev(i) * hb - 1, 0), j)),
            pl.BlockSpec((3, fc), lambda i, j: (0, j)),
            pl.BlockSpec((3, fc), lambda i, j: (0, nj + j)),
            pl.BlockSpec((1, fc), lambda i, j: (0, j)),
            pl.BlockSpec((1, fc), lambda i, j: (0, nj + j)),
            pl.BlockSpec((D, fc), lambda i, j: (0, j)),
            pl.BlockSpec((fc, D), lambda i, j: (j, 0)),
            pl.BlockSpec((fc, D), lambda i, j: (nj + j, 0)),
        ],
        out_specs=[
            pl.BlockSpec((tm, D), lambda i, j: (rev(i), 0)),
            pl.BlockSpec((tm, fc), lambda i, j: (rev(i), j)),
            pl.BlockSpec((tm, fc), lambda i, j: (rev(i), j)),
            pl.BlockSpec((1, D), lambda i, j: (0, 0)),
            pl.BlockSpec((8, D_FF), lambda i, j: (0, 0)),
            pl.BlockSpec((8, D_FF), lambda i, j: (0, 0)),
        ],
        out_shape=[
            jax.ShapeDtypeStruct((T, D), F32),
            jax.ShapeDtypeStruct((T, D_FF), BF16),
            jax.ShapeDtypeStruct((T, D_FF), BF16),
            jax.ShapeDtypeStruct((1, D), F32),
            jax.ShapeDtypeStruct((8, D_FF), F32),
            jax.ShapeDtypeStruct((8, D_FF), F32),
        ],
        scratch_shapes=[
            pltpu.VMEM((tm, D), F32),
            pltpu.VMEM((nj, 2, 8, fc), F32),
            pltpu.VMEM((4, tm + 8, fc), F32),
        ],
        compiler_params=_cparams(("arbitrary", "arbitrary")),
    )(dx3, x2, g3, ug, uv, ug, uv, conv_w, conv_w, conv_b, conv_b, w_down_t, w_up_t, w_up_t)


def _xattn_bwd(dx2, x1, g2, qb, kv, w_mo_t, w_mq_t, tm):
    T, D = x1.shape
    M = kv.shape[0]

    def body(dx2_ref, x_ref, g_ref, q_ref, kv_ref, wot_ref, wqt_ref, dx1_ref, dq_ref, dkv_ref, dg_ref):
        i = pl.program_id(0)

        @pl.when(i == 0)
        def _():
            dkv_ref[...] = jnp.zeros_like(dkv_ref)
            dg_ref[...] = jnp.zeros_like(dg_ref)

        dxv = dx2_ref[...]
        dom = _nn(dxv.astype(BF16), wot_ref[...]).astype(BF16)
        qb_ = q_ref[...]
        kvv = kv_ref[...]
        for hd in range(N_MEM_HEADS):
            sl = slice(hd * MEM_HD, (hd + 1) * MEM_HD)
            vsl = slice(D + hd * MEM_HD, D + (hd + 1) * MEM_HD)
            p = _xattn_probs(qb_, kvv, hd)
            dp = _nt(dom[:, sl], kvv[:, vsl])
            ds = (p * (dp - jnp.sum(p * dp, axis=1, keepdims=True)) * (MEM_HD ** -0.5)).astype(BF16)
            dq_ref[:, sl] = _nn(ds, kvv[:, sl]).astype(BF16)
            dkv_ref[:, sl] += _tn(ds, qb_[:, sl])
            dkv_ref[:, vsl] += _tn(p.astype(BF16), dom[:, sl])
        dh = _nn(dq_ref[...], wqt_ref[...])
        dx, dg = _norm_bwd(x_ref[...], g_ref[...], dh)
        dx1_ref[...] = dxv + dx
        dg_ref[...] += dg

    return pl.pallas_call(
        body,
        name="xattn_bwd",
        grid=(T // tm,),
        in_specs=[
            pl.BlockSpec((tm, D), lambda i: (i, 0)),
            pl.BlockSpec((tm, D), lambda i: (i, 0)),
            pl.BlockSpec((1, D), lambda i: (0, 0)),
            pl.BlockSpec((tm, D), lambda i: (i, 0)),
            pl.BlockSpec((M, 2 * D), lambda i: (0, 0)),
            pl.BlockSpec((D, D), lambda i: (0, 0)),
            pl.BlockSpec((D, D), lambda i: (0, 0)),
        ],
        out_specs=[
            pl.BlockSpec((tm, D), lambda i: (i, 0)),
            pl.BlockSpec((tm, D), lambda i: (i, 0)),
            pl.BlockSpec((M, 2 * D), lambda i: (0, 0)),
            pl.BlockSpec((1, D), lambda i: (0, 0)),
        ],
        out_shape=[
            jax.ShapeDtypeStruct((T, D), F32),
            jax.ShapeDtypeStruct((T, D), BF16),
            jax.ShapeDtypeStruct((M, 2 * D), F32),
            jax.ShapeDtypeStruct((1, D), F32),
        ],
        compiler_params=_cparams(("arbitrary",)),
    )(dx2, x1, g2, qb, kv, w_mo_t, w_mq_t)


def _mem_kv_bwd(mem, gm, mb, dkv, w_mkv_t):
    M, D = mem.shape
    N = dkv.shape[1]

    def body(mem_ref, g_ref, m_ref, dkv_ref, wt_ref, dw_ref, dg_ref):
        dkvb = dkv_ref[...].astype(BF16)
        for n0 in range(0, N, 512):
            dw_ref[:, n0:n0 + 512] = _tn(m_ref[...], dkvb[:, n0:n0 + 512])
        dm = _nn(dkvb, wt_ref[...])
        mv = mem_ref[...]
        dg_ref[...] = jnp.sum(dm * (mv * _rstd(mv)), axis=0, keepdims=True)

    return pl.pallas_call(
        body,
        name="mem_kv_bwd",
        out_shape=[jax.ShapeDtypeStruct((D, N), F32), jax.ShapeDtypeStruct((1, D), F32)],
        compiler_params=_cparams(),
    )(mem, gm, mb, dkv, w_mkv_t)


def _post_attn_bwd(dx1, fox_o, sb_o, gf, gs, w_out_t, tm):
    T, D = dx1.shape

    def body(dx_ref, f_ref, s_ref, gf_ref, gs_ref, wt_ref, df_ref, ds_ref, dgf_ref, dgs_ref):
        i = pl.program_id(0)

        @pl.when(i == 0)
        def _():
            dgf_ref[...] = jnp.zeros_like(dgf_ref)
            dgs_ref[...] = jnp.zeros_like(dgs_ref)

        dmix = _nn(dx_ref[...].astype(BF16), wt_ref[...])
        d, dg = _norm_bwd(f_ref[...], gf_ref[...], dmix[:, :FOX_W])
        df_ref[...] = d
        dgf_ref[...] += dg
        d, dg = _norm_bwd(s_ref[...], gs_ref[...], dmix[:, FOX_W:])
        ds_ref[...] = d
        dgs_ref[...] += dg

    return pl.pallas_call(
        body,
        name="post_attn_bwd",
        grid=(T // tm,),
        in_specs=[
            pl.BlockSpec((tm, D), lambda i: (i, 0)),
            pl.BlockSpec((tm, FOX_W), lambda i: (i, 0)),
            pl.BlockSpec((tm, FOX_W), lambda i: (i, 0)),
            pl.BlockSpec((1, FOX_W), lambda i: (0, 0)),
            pl.BlockSpec((1, FOX_W), lambda i: (0, 0)),
            pl.BlockSpec((D, D), lambda i: (0, 0)),
        ],
        out_specs=[
            pl.BlockSpec((tm, FOX_W), lambda i: (i, 0)),
            pl.BlockSpec((tm, FOX_W), lambda i: (i, 0)),
            pl.BlockSpec((1, FOX_W), lambda i: (0, 0)),
            pl.BlockSpec((1, FOX_W), lambda i: (0, 0)),
        ],
        out_shape=[
            jax.ShapeDtypeStruct((T, FOX_W), F32),
            jax.ShapeDtypeStruct((T, FOX_W), F32),
            jax.ShapeDtypeStruct((1, FOX_W), F32),
            jax.ShapeDtypeStruct((1, FOX_W), F32),
        ],
        compiler_params=_cparams(("arbitrary",)),
    )(dx1, fox_o, sb_o, gf, gs, w_out_t)


def _sb_bwd(proj, ltot, d_o, tq):
    T = proj.shape[0]

    def body(q_ref, k_ref, v_ref, lt_ref, do_ref, dq_ref, dk_ref, dv_ref,
             qh_s, doh_s, lt_s, z_s, da_s, ab_s, dzb_s, run_s, runw_s, dq_s):
        i = pl.program_id(1)

        @pl.when(i == 0)
        def _():
            dk_ref[...] = jnp.zeros_like(dk_ref)
            dv_ref[...] = jnp.zeros_like(dv_ref)

        lane = lax.broadcasted_iota(jnp.int32, (1, 128), 1)
        row = lax.broadcasted_iota(jnp.int32, (tq, tq), 0)
        col = lax.broadcasted_iota(jnp.int32, (tq, tq), 1)
        strict = col < row
        upto = jnp.where(row <= col, 1.0, 0.0).astype(BF16)
        before = jnp.where(row < col, 1.0, 0.0).astype(BF16)
        q = q_ref[...]
        dov = do_ref[...]
        for hh in range(2):
            qh, hmask = _head_q(q, hh, lane)
            qh_s[hh] = -qh
            doh_s[hh] = jnp.where(hmask, dov, 0.0).astype(BF16)
            lt_s[hh] = jnp.broadcast_to(lt_ref[hh], (tq, 128))
        run_s[...] = jnp.zeros_like(run_s)
        runw_s[...] = jnp.zeros_like(runw_s)
        dq_s[...] = jnp.zeros_like(dq_s)

        def rows(t):
            return pl.ds(pl.multiple_of(t * tq, tq), tq)

        def stage_a(t, slot):
            k = k_ref[rows(t), :]
            v = v_ref[rows(t), :]
            for hh in range(2):
                z_s[slot, hh] = _nt(qh_s[hh], k)
                da_s[slot, hh] = _nt(doh_s[hh], v)

        def stage_b(t, slot, diag):
            for hh in range(2):
                L, g = _sb_logs(z_s[slot, hh], strict if diag else None)
                upto_s = _split_dot(L, upto, SB_SUM_TERMS)
                run = run_s[hh]
                arg = (g + _lanes2(lt_s[hh] - run)) - upto_s
                if diag:
                    arg = jnp.where(strict, arg, NEG)
                a = jnp.exp(arg)
                w = a * da_s[slot, hh]
                w_before = _split_dot(w, before, SB_SUM_TERMS)
                run_w = runw_s[hh]
                d_keep = w_before + _lanes2(run_w)
                beta = jnp.exp(g)
                ndz = beta * (w + d_keep) - w
                if diag:
                    ndz = jnp.where(strict, ndz, 0.0)
                dzb_s[hh] = ndz.astype(BF16)
                ab_s[hh] = a.astype(BF16)
                run_s[hh] = run + jnp.broadcast_to(upto_s[:, tq - 1:tq], (tq, 128))
                runw_s[hh] = run_w + jnp.broadcast_to(w_before[:, tq - 1:tq] + w[:, tq - 1:tq], (tq, 128))

        def stage_c(t):
            k = k_ref[rows(t), :]
            dk_blk = None
            dv_blk = None
            for hh in range(2):
                dzb = dzb_s[hh]
                dq_s[hh] += _nn(dzb, k)
                dk_h = _tn(dzb, qh_s[hh])
                dv_h = _tn(ab_s[hh], doh_s[hh])
                dk_blk = dk_h if dk_blk is None else dk_blk + dk_h
                dv_blk = dv_h if dv_blk is None else dv_blk + dv_h
            dk_ref[rows(t), :] += dk_blk
            dv_ref[rows(t), :] += dv_blk

        _pipeline3(i + 1, stage_a, stage_b, stage_c, True)
        dq_ref[...] = (jnp.where(lane < HEAD_DIM, dq_s[0], dq_s[1]) * -(HEAD_DIM ** -0.5)).astype(BF16)

    return pl.pallas_call(
        body,
        name="sb_bwd",
        grid=(4, T // tq),
        in_specs=[
            pl.BlockSpec((tq, 128), lambda p, i: (i, 12 + p)),
            pl.BlockSpec((T, 128), lambda p, i: (0, 16 + p)),
            pl.BlockSpec((T, 128), lambda p, i: (0, 20 + p)),
            pl.BlockSpec((2, tq, 1), lambda p, i: (p, i, 0)),
            pl.BlockSpec((tq, 128), lambda p, i: (i, p)),
        ],
        out_specs=[
            pl.BlockSpec((tq, 128), lambda p, i: (i, p)),
            pl.BlockSpec((T, 128), lambda p, i: (0, p)),
            pl.BlockSpec((T, 128), lambda p, i: (0, p)),
        ],
        out_shape=[
            jax.ShapeDtypeStruct((T, FOX_W), BF16),
            jax.ShapeDtypeStruct((T, FOX_W), F32),
            jax.ShapeDtypeStruct((T, FOX_W), F32),
        ],
        scratch_shapes=[
            pltpu.VMEM((2, tq, 128), BF16),
            pltpu.VMEM((2, tq, 128), BF16),
            pltpu.VMEM((2, tq, 128), F32),
            pltpu.VMEM((2, 2, tq, tq), F32),
            pltpu.VMEM((2, 2, tq, tq), F32),
            pltpu.VMEM((2, tq, tq), BF16),
            pltpu.VMEM((2, tq, tq), BF16),
            pltpu.VMEM((2, tq, 128), F32),
            pltpu.VMEM((2, tq, 128), F32),
            pltpu.VMEM((2, tq, 128), F32),
        ],
        compiler_params=_cparams(("arbitrary", "arbitrary")),
    )(proj, proj, proj, ltot, d_o)


def _fox_bwd(proj, c_col, c_row, lse, d_o, o, tq, scatter=()):
    T = proj.shape[0]
    ns = len(scatter)
    nq = T // tq

    def body(*refs):
        q_ref, k_ref, v_ref, cq_ref, ck_ref, lse_ref, do_ref, o_ref = refs[:8]
        dq_ref, dk_ref, dv_ref, dck_ref, dcq_ref = refs[8 + ns:13 + ns]
        qh_s, doh_s, delta_s, shift_s, z_s, dp_s, pb_s, dsb_s, rs_s, dq_s = refs[13 + 2 * ns:23 + 2 * ns]
        i = pl.program_id(1)
        if ns:
            pair = pl.program_id(0)
            exchange = _Scatter(refs[8:8 + ns], refs[13 + ns:13 + 2 * ns], *refs[23 + 2 * ns:])

            @pl.when((pair == 0) & (i == 0))
            def _():
                exchange.start()

        @pl.when(i == 0)
        def _():
            dk_ref[...] = jnp.zeros_like(dk_ref)
            dv_ref[...] = jnp.zeros_like(dv_ref)
            dck_ref[...] = jnp.zeros_like(dck_ref)

        lane = lax.broadcasted_iota(jnp.int32, (1, 128), 1)
        row = lax.broadcasted_iota(jnp.int32, (tq, tq), 0)
        col = lax.broadcasted_iota(jnp.int32, (tq, tq), 1)
        q = q_ref[...]
        dov = do_ref[...]
        ov = o_ref[...]
        for hh in range(2):
            qh, hmask = _head_q(q, hh, lane)
            dohb = jnp.where(hmask, dov, 0.0).astype(BF16)
            qh_s[hh] = qh
            doh_s[hh] = dohb
            delta_s[hh] = jnp.broadcast_to(jnp.sum(dohb.astype(F32) * ov, axis=1, keepdims=True), (tq, tq))
            shift_s[hh] = jnp.broadcast_to(cq_ref[hh] - lse_ref[hh], (tq, tq))
        rs_s[...] = jnp.zeros_like(rs_s)
        dq_s[...] = jnp.zeros_like(dq_s)

        def rows(t):
            return pl.ds(pl.multiple_of((i - t) * tq, tq), tq)

        def stage_a(t, slot):
            k = k_ref[rows(t), :]
            v = v_ref[rows(t), :]
            for hh in range(2):
                z_s[slot, hh] = _nt(qh_s[hh], k)
                dp_s[slot, hh] = _nt(doh_s[hh], v)

        def stage_b(t, slot, diag):
            for hh in range(2):
                s = z_s[slot, hh] + shift_s[hh] - ck_ref[hh, :, rows(t)]
                if diag:
                    s = jnp.where(col <= row, s, NEG)
                p = jnp.exp(s)
                ds = p * (dp_s[slot, hh] - delta_s[hh])
                pb_s[hh] = p.astype(BF16)
                dsb_s[hh] = ds.astype(BF16)
                dck_ref[hh, :, rows(t)] += jnp.sum(ds, axis=0, keepdims=True)
                rs_s[hh] += jnp.sum(ds, axis=1, keepdims=True)

        def stage_c(t):
            k = k_ref[rows(t), :]
            dk_blk = None
            dv_blk = None
            for hh in range(2):
                dsb = dsb_s[hh]
                dq_s[hh] += _nn(dsb, k)
                dk_h = _tn(dsb, qh_s[hh])
                dv_h = _tn(pb_s[hh], doh_s[hh])
                dk_blk = dk_h if dk_blk is None else dk_blk + dk_h
                dv_blk = dv_h if dv_blk is None else dv_blk + dv_h
            dk_ref[rows(t), :] += dk_blk
            dv_ref[rows(t), :] += dv_blk

        _pipeline3(i + 1, stage_a, stage_b, stage_c, False)
        dcq_ref[0] = rs_s[0][:, 0:1]
        dcq_ref[1] = rs_s[1][:, 0:1]
        dq_ref[...] = (jnp.where(lane < HEAD_DIM, dq_s[0], dq_s[1]) * (HEAD_DIM ** -0.5)).astype(BF16)
        if ns:
            @pl.when((pair == 3) & (i == nq - 1))
            def _():
                exchange.finish()

    res = pl.pallas_call(
        body,
        name="fox_bwd",
        grid=(4, nq),
        in_specs=[
            pl.BlockSpec((tq, 128), lambda p, i: (i, p)),
            pl.BlockSpec((T, 128), lambda p, i: (0, 4 + p)),
            pl.BlockSpec((T, 128), lambda p, i: (0, 8 + p)),
            pl.BlockSpec((2, tq, 1), lambda p, i: (p, i, 0)),
            pl.BlockSpec((2, 1, T), lambda p, i: (p, 0, 0)),
            pl.BlockSpec((2, tq, 1), lambda p, i: (p, i, 0)),
            pl.BlockSpec((tq, 128), lambda p, i: (i, p)),
            pl.BlockSpec((tq, 128), lambda p, i: (i, p)),
        ] + [_ANY] * ns,
        out_specs=[
            pl.BlockSpec((tq, 128), lambda p, i: (i, p)),
            pl.BlockSpec((T, 128), lambda p, i: (0, p)),
            pl.BlockSpec((T, 128), lambda p, i: (0, p)),
            pl.BlockSpec((2, 1, T), lambda p, i: (p, 0, 0)),
            pl.BlockSpec((2, tq, 1), lambda p, i: (p, i, 0)),
        ] + [_ANY] * ns,
        out_shape=[
            jax.ShapeDtypeStruct((T, FOX_W), BF16),
            jax.ShapeDtypeStruct((T, FOX_W), F32),
            jax.ShapeDtypeStruct((T, FOX_W), F32),
            jax.ShapeDtypeStruct((N_FOX, 1, T), F32),
            jax.ShapeDtypeStruct((N_FOX, T, 1), F32),
        ] + [jax.ShapeDtypeStruct(b.shape, b.dtype) for b in scatter],
        scratch_shapes=[
            pltpu.VMEM((2, tq, 128), BF16),
            pltpu.VMEM((2, tq, 128), BF16),
            pltpu.VMEM((2, tq, tq), F32),
            pltpu.VMEM((2, tq, tq), F32),
            pltpu.VMEM((2, 2, tq, tq), F32),
            pltpu.VMEM((2, 2, tq, tq), F32),
            pltpu.VMEM((2, tq, tq), BF16),
            pltpu.VMEM((2, tq, tq), BF16),
            pltpu.VMEM((2, tq, 128), F32),
            pltpu.VMEM((2, tq, 128), F32),
        ] + (_comm_sems(ns) if ns else []),
        compiler_params=_cparams(("arbitrary", "arbitrary")),
    )(proj, proj, proj, c_col, c_row, lse, d_o, o, *scatter)
    res = list(res)
    return (*res[:5], res[5:])


def _forget_bwd(dcq, dck, xf, tc):
    H, T = xf.shape
    nc = T // tc

    def body(dcq_ref, dck_ref, xf_ref, dxf_ref, db_ref):
        row = lax.broadcasted_iota(jnp.int32, (tc, tc), 0)
        col = lax.broadcasted_iota(jnp.int32, (tc, tc), 1)
        from_here = jnp.where(row >= col, 1.0, 0.0).astype(BF16)

        def chunk(n, carry):
            run, db = carry
            cs = pl.multiple_of((nc - 1 - n) * tc, tc)
            dc = dcq_ref[:, pl.ds(cs, tc)] - dck_ref[:, pl.ds(cs, tc)]
            dlogf = _split_dot(dc, from_here, 3) + run
            xfv = xf_ref[:, pl.ds(cs, tc)]
            dxf = dlogf * jax.nn.sigmoid(-xfv)
            dxf_ref[:, pl.ds(cs, tc)] = dxf
            return dlogf[:, 0:1], db + jnp.sum(dxf, axis=1, keepdims=True)

        _, db = lax.fori_loop(0, nc, chunk, (jnp.zeros((H, 1), F32), jnp.zeros((H, 1), F32)))
        db_ref[...] = db

    return pl.pallas_call(
        body,
        name="forget_bwd",
        out_shape=[jax.ShapeDtypeStruct((H, T), F32), jax.ShapeDtypeStruct((H, 1), F32)],
        compiler_params=_cparams(),
    )(dcq, dck, xf)


def _inproj_bwd(dproj, w_in_t, x, g1, dx1, tm, scatter=()):
    T, D = x.shape
    N = dproj.shape[1]
    ns = len(scatter)
    nt = T // tm

    def body(*refs):
        dp_ref, wt_ref, x_ref, g_ref, dx1_ref = refs[:5]
        dx_ref, dg_ref = refs[5 + ns:7 + ns]
        i = pl.program_id(0)
        if ns:
            exchange = _Scatter(refs[5:5 + ns], refs[7 + ns:7 + 2 * ns], *refs[7 + 2 * ns:])

            @pl.when(i == 0)
            def _():
                exchange.start()

        @pl.when(i == 0)
        def _():
            dg_ref[...] = jnp.zeros_like(dg_ref)

        dh = _nn(dp_ref[...], wt_ref[...])
        dx, dg = _norm_bwd(x_ref[...], g_ref[...], dh)
        dx_ref[...] = dx1_ref[...] + dx
        dg_ref[...] += dg
        if ns:
            @pl.when(i == nt - 1)
            def _():
                exchange.finish()

    res = pl.pallas_call(
        body,
        name="inproj_bwd",
        grid=(nt,),
        in_specs=[
            pl.BlockSpec((tm, N), lambda i: (i, 0)),
            pl.BlockSpec((N, D), lambda i: (0, 0)),
            pl.BlockSpec((tm, D), lambda i: (i, 0)),
            pl.BlockSpec((1, D), lambda i: (0, 0)),
            pl.BlockSpec((tm, D), lambda i: (i, 0)),
        ] + [_ANY] * ns,
        out_specs=[
            pl.BlockSpec((tm, D), lambda i: (i, 0)),
            pl.BlockSpec((1, D), lambda i: (0, 0)),
        ] + [_ANY] * ns,
        out_shape=[jax.ShapeDtypeStruct((T, D), F32), jax.ShapeDtypeStruct((1, D), F32)]
        + [jax.ShapeDtypeStruct(b.shape, b.dtype) for b in scatter],
        scratch_shapes=_comm_sems(ns) if ns else [],
        compiler_params=_cparams(("arbitrary",)),
    )(dproj, w_in_t, x, g1, dx1, *scatter)
    res = list(res)
    return res[0], res[1], res[2:]


def _matmul_tn(a, b, name, cast_b=False):
    T, K = a.shape
    N = b.shape[1]
    bt = min(T, 512)
    bk = _tile_div(K, 1536)
    bn = _tile_div(N, 1536)

    def body(a_ref, b_ref, o_ref):
        @pl.when(pl.program_id(2) == 0)
        def _():
            o_ref[...] = jnp.zeros_like(o_ref)

        bv = b_ref[...]
        if cast_b:
            bv = bv.astype(BF16)
        o_ref[...] += _tn(a_ref[...], bv)

    return pl.pallas_call(
        body,
        name=name,
        grid=(K // bk, N // bn, T // bt),
        in_specs=[
            pl.BlockSpec((bt, bk), lambda k, n, t: (t, k)),
            pl.BlockSpec((bt, bn), lambda k, n, t: (t, n)),
        ],
        out_specs=pl.BlockSpec((bk, bn), lambda k, n, t: (k, n)),
        out_shape=jax.ShapeDtypeStruct((K, N), F32),
        compiler_params=_cparams(("arbitrary", "arbitrary", "arbitrary")),
    )(a, b)


def _local_step(x, mem, target, p, tm, tq, late=None):
    T, D = x.shape
    w_in = p["w_in"]
    w_qkv = w_in[:, :QKV_W]
    w_f_t = w_in[:, QKV_W:].T
    w_in_t = jnp.pad(w_in, ((0, 0), (0, IN_PAD - w_in.shape[1]))).T
    b_f = p["b_forget"].reshape(N_FOX, 1)

    proj, h1, xf, c = _inproj_fwd(x, p["attn_norm_g"], w_qkv, w_f_t, b_f, tm)
    c_col = c.reshape(N_FOX, T, 1)
    c_row = c.reshape(N_FOX, 1, T)
    if late:
        fox_o, lse, gathered = _fox_fwd(proj, c_col, c_row, tq, gather=[late[n] for n in _LATE])
        p = dict(p, **{n: _gathered_full(n, gv) for n, gv in zip(_LATE, gathered)})
    else:
        fox_o, lse, _ = _fox_fwd(proj, c_col, c_row, tq)
    sb_o, sb_ltot = _sb_fwd(proj, tq)
    x1, mixed = _post_attn_fwd(fox_o, sb_o, p["fox_out_g"], p["sb_out_g"], p["w_out"], x, tm)
    mb, kv = _mem_kv_fwd(mem, p["mem_norm_g"], p["w_mkv"])
    x2, h2, qb, om = _xattn_fwd(x1, p["xattn_norm_g"], p["w_mq"], kv, p["w_mo"], tm)
    x3, h3, ug, uv, a = _ffn_fwd(x2, p["ffn_norm_g"], p["w_up"], p["conv_w"], p["conv_b"], p["w_down"], tm)
    dx3, loss_blk, d_final_g = _loss_head(x3, p["final_norm_g"], target, tm)

    g = {"final_norm_g": d_final_g}
    dx2, du_g, du_v, g["ffn_norm_g"], dc_g, dc_v = _ffn_bwd(
        dx3, x2, p["ffn_norm_g"], ug, uv, p["conv_w"], p["conv_b"], p["w_down"].T, p["w_up"].T, tm)
    g["w_down"] = _matmul_tn(a, dx3, "dw_down", cast_b=True)
    g["w_up"] = jnp.concatenate([_matmul_tn(h3, du_g, "dw_up_gate"), _matmul_tn(h3, du_v, "dw_up_val")], axis=1)
    dconv = jnp.concatenate([dc_g, dc_v], axis=1)
    g["conv_w"] = dconv[0:3]
    g["conv_b"] = dconv[3:4]
    dx1, dq_m, dkv, g["xattn_norm_g"] = _xattn_bwd(dx2, x1, p["xattn_norm_g"], qb, kv, p["w_mo"].T, p["w_mq"].T, tm)
    g["w_mo"] = _matmul_tn(om, dx2, "dw_mo", cast_b=True)
    g["w_mq"] = _matmul_tn(h2, dq_m, "dw_mq")
    g["w_mkv"], g["mem_norm_g"] = _mem_kv_bwd(mem, p["mem_norm_g"], mb, dkv, p["w_mkv"].T)
    d_fox, d_sb, g["fox_out_g"], g["sb_out_g"] = _post_attn_bwd(
        dx1, fox_o, sb_o, p["fox_out_g"], p["sb_out_g"], p["w_out"].T, tm)
    g["w_out"] = _matmul_tn(mixed, dx1, "dw_out", cast_b=True)
    dq_s, dk_s, dv_s = _sb_bwd(proj, sb_ltot, d_sb, tq)
    if late:
        dq_f, dk_f, dv_f, dck, dcq, parts = _fox_bwd(proj, c_col, c_row, lse, d_fox, fox_o, tq,
                                                      scatter=[_grad_blocks(n, g[n]) for n in _LATE])
        g["parts"] = dict(zip(_LATE, parts))
    else:
        dq_f, dk_f, dv_f, dck, dcq, _ = _fox_bwd(proj, c_col, c_row, lse, d_fox, fox_o, tq)
    dxf, db = _forget_bwd(dcq.reshape(N_FOX, T), dck.reshape(N_FOX, T), xf, min(T, 512))
    g["b_forget"] = db.reshape(1, N_FOX)
    dproj = jnp.concatenate([
        dq_f, dk_f.astype(BF16), dv_f.astype(BF16), dq_s, dk_s.astype(BF16), dv_s.astype(BF16),
        jnp.pad(dxf.T, ((0, 0), (0, IN_PAD - QKV_W - N_FOX))).astype(BF16)], axis=1)
    g["w_in"] = _matmul_tn(h1, dproj, "dw_in")[:, :w_in.shape[1]]
    if late:
        grad_x, g["attn_norm_g"], (g["parts"]["w_in"],) = _inproj_bwd(
            dproj, w_in_t, x, p["attn_norm_g"], dx1, tm, scatter=[_grad_blocks("w_in", g["w_in"])])
    else:
        grad_x, g["attn_norm_g"], _ = _inproj_bwd(dproj, w_in_t, x, p["attn_norm_g"], dx1, tm)
    return loss_blk, grad_x, g


def _mesh_pos():
    return lax.axis_index("x"), lax.axis_index("y"), lax.axis_index("c")


def _flip(pos, k):
    return tuple(1 - v if (k >> b) & 1 else v for v, b in zip(pos, (2, 1, 0)))


def _slot(pos):
    return 4 * pos[0] + 2 * pos[1] + pos[2]


_CHIPS = (4, 2, 6)


def _comm_sems(n):
    return [pltpu.SemaphoreType.DMA((7 * n,)), pltpu.SemaphoreType.DMA((7 * n,)), pltpu.SemaphoreType.DMA((n,))]


class _Gather:
    def __init__(self, ins, outs, send_sems, recv_sems, local_sems):
        self.ins, self.outs, self.n = ins, outs, len(ins)
        self.send_sems, self.recv_sems, self.local_sems = send_sems, recv_sems, local_sems
        self.me = _mesh_pos()
        self.sibling = _flip(self.me, 1)

    def _copy(self, a, kk, block, to, src=None):
        rows = self.outs[a].at[_slot(block)]
        return pltpu.make_async_remote_copy(
            src_ref=rows if src is None else src, dst_ref=rows,
            send_sem=self.send_sems.at[7 * a + kk], recv_sem=self.recv_sems.at[7 * a + kk],
            device_id=to, device_id_type=MESH)

    def _mine(self):
        return [pltpu.make_async_copy(self.ins[a], self.outs[a].at[_slot(self.me)], self.local_sems.at[a])
                for a in range(self.n)]

    def _first(self):
        out = []
        for a in range(self.n):
            out.append(self._copy(a, 0, self.me, self.sibling, src=self.ins[a]))
            out += [self._copy(a, 1 + j, self.me, _flip(self.me, k), src=self.ins[a]) for j, k in enumerate(_CHIPS)]
        return out

    def _passed(self):
        return [self._copy(a, 4 + j, _flip(self.me, k), self.sibling)
                for j, k in enumerate(_CHIPS) for a in range(self.n)]

    def start(self):
        for cp in self._mine() + self._first():
            cp.start()

    def forward(self):
        for j, k in enumerate(_CHIPS):
            for a in range(self.n):
                self._copy(a, 1 + j, _flip(self.me, k), self.me).wait_recv()
                self._copy(a, 4 + j, _flip(self.me, k), self.sibling).start()

    def finish(self):
        for a in range(self.n):
            self._copy(a, 0, self.sibling, self.me).wait_recv()
            for j, k in enumerate(_CHIPS):
                self._copy(a, 4 + j, _flip(self.sibling, k), self.me).wait_recv()
        for cp in self._first() + self._passed():
            cp.wait_send()
        for cp in self._mine():
            cp.wait()


class _Scatter:
    def __init__(self, ins, outs, send_sems, recv_sems, local_sems):
        self.ins, self.outs, self.n = ins, outs, len(ins)
        self.send_sems, self.recv_sems, self.local_sems = send_sems, recv_sems, local_sems
        self.me = _mesh_pos()

    def _copy(self, a, k, landed=False):
        peer = _flip(self.me, k)
        return pltpu.make_async_remote_copy(
            src_ref=self.ins[a].at[_slot(peer)], dst_ref=self.outs[a].at[_slot(peer if landed else self.me)],
            send_sem=self.send_sems.at[7 * a + k - 1], recv_sem=self.recv_sems.at[7 * a + k - 1],
            device_id=peer, device_id_type=MESH)

    def _mine(self):
        s = _slot(self.me)
        return [pltpu.make_async_copy(self.ins[a].at[s], self.outs[a].at[s], self.local_sems.at[a])
                for a in range(self.n)]

    def start(self):
        for cp in self._mine() + [self._copy(a, k) for k in range(1, 8) for a in range(self.n)]:
            cp.start()

    def finish(self):
        for k in range(1, 8):
            for a in range(self.n):
                self._copy(a, k, landed=True).wait_recv()
        for k in range(1, 8):
            for a in range(self.n):
                self._copy(a, k).wait_send()
        for cp in self._mine():
            cp.wait()


_ANY = pl.BlockSpec(memory_space=pl.ANY)


def _gathered_shapes(shards):
    return [jax.ShapeDtypeStruct((N_DEV,) + s.shape, s.dtype) for s in shards]


def _all_gather(shards, name):
    n = len(shards)

    def body(*refs):
        g = _Gather(refs[:n], refs[n:2 * n], *refs[2 * n:])
        g.start()
        g.forward()
        g.finish()

    return pl.pallas_call(
        body, name=name, in_specs=[_ANY] * n, out_specs=[_ANY] * n,
        out_shape=_gathered_shapes(shards), scratch_shapes=_comm_sems(n),
    )(*shards)


def _adamw_math(w, g, m, v):
    m2 = ADAM_B1 * m + (1.0 - ADAM_B1) * g
    v2 = ADAM_B2 * v + (1.0 - ADAM_B2) * (g * g)
    m_hat = m2 / (1.0 - ADAM_B1 ** ADAM_STEP)
    v_hat = v2 / (1.0 - ADAM_B2 ** ADAM_STEP)
    delta = -ADAM_LR * (m_hat / (jnp.sqrt(v_hat) + ADAM_EPS) + ADAM_WD * w)
    return delta, m2, v2


def _adamw(w, parts, m, v, name):
    R, C = w.shape
    br = 128 if R % 128 == 0 else R

    def body(w_ref, p_ref, m_ref, v_ref, g_ref, d_ref, nm_ref, nv_ref):
        g = p_ref[0].astype(F32)
        for s in range(1, N_DEV):
            g = g + p_ref[s].astype(F32)
        g_ref[...] = g
        d_ref[...], nm_ref[...], nv_ref[...] = _adamw_math(w_ref[...], g, m_ref[...], v_ref[...])

    spec = pl.BlockSpec((br, C), lambda i: (i, 0))
    return pl.pallas_call(
        body,
        name=name,
        grid=(R // br,),
        in_specs=[spec, pl.BlockSpec((N_DEV, br, C), lambda i: (0, i, 0)), spec, spec],
        out_specs=[spec] * 4,
        out_shape=[jax.ShapeDtypeStruct((R, C), F32)] * 4,
        compiler_params=_cparams(("arbitrary",)),
    )(w, parts, m, v)


_SHARDED = ("w_in", "w_out", "w_mq", "w_mkv", "w_mo", "w_up", "conv_w", "w_down")
_LATE = _SHARDED[1:]
_COL_SHARDED = ("w_in", "w_mkv", "w_up", "conv_w")
_REPLICATED = ("attn_norm_g", "b_forget", "fox_out_g", "sb_out_g", "xattn_norm_g", "mem_norm_g",
               "ffn_norm_g", "conv_b", "final_norm_g")
_WEIGHTS = ("attn_norm_g", "w_in", "b_forget", "fox_out_g", "sb_out_g", "w_out", "xattn_norm_g", "mem_norm_g",
            "w_mq", "w_mkv", "w_mo", "ffn_norm_g", "w_up", "conv_w", "conv_b", "w_down", "final_norm_g")


def _pack_rows(n):
    return -(-n // 128)


def _pack(vals, rows_total):
    parts = []
    for v in vals:
        flat = v.reshape(-1)
        parts.append(jnp.pad(flat, (0, _pack_rows(flat.shape[0]) * 128 - flat.shape[0])))
    flat = jnp.concatenate(parts)
    return jnp.pad(flat, (0, rows_total * 128 - flat.shape[0])).reshape(rows_total, 128)


def _unpack(packed, shapes):
    out = []
    r = 0
    for shp in shapes:
        n = 1
        for d in shp:
            n *= d
        out.append(packed[r:r + _pack_rows(n)].reshape(-1)[:n].reshape(shp))
        r += _pack_rows(n)
    return out


def _gathered_full(name, gathered):
    if name in _COL_SHARDED:
        return jnp.transpose(gathered, (1, 0, 2)).reshape(gathered.shape[1], -1)
    return gathered.reshape(-1, gathered.shape[2])


def _to_blocks(name, full):
    if name in _COL_SHARDED:
        r = full.shape[0]
        return jnp.transpose(full.reshape(r, N_DEV, -1), (1, 0, 2))
    return full.reshape(N_DEV, -1, full.shape[1])


def _grad_blocks(name, full):
    blocks = _to_blocks(name, full)
    return blocks if name == "conv_w" else blocks.astype(BF16)


def _step(args, tm, tq):
    w = {n: args[n] for n in _WEIGHTS}
    mom = {n: args["m_" + n] for n in _WEIGHTS}
    var = {n: args["v_" + n] for n in _WEIGHTS}
    x = args["x"][0]
    mem = args["mem"][0]
    target = args["loss_target"][0]

    def flat2(a):
        return a.reshape(a.shape[-2], a.shape[-1]) if a.ndim == 3 else a.reshape(1, -1)

    shards = {n: flat2(w[n]) if n == "conv_w" else flat2(w[n]).astype(BF16) for n in _SHARDED}
    (w_in_all,) = _all_gather([shards["w_in"]], "gather_w_in")
    p = {"w_in": _gathered_full("w_in", w_in_all)}
    for n in _REPLICATED:
        p[n] = flat2(w[n])

    loss_blk, grad_x, g = _local_step(x, mem, target, p, tm, tq, late={n: shards[n] for n in _LATE})

    parts = g["parts"]
    out = {}
    for n in _SHARDED:
        res = _adamw(flat2(w[n]), parts[n], flat2(mom[n]), flat2(var[n]), "adamw_" + n)
        out[n] = [r.reshape(w[n].shape) for r in res]

    shapes = [w[n].shape for n in _REPLICATED]
    rows = sum(_pack_rows(flat2(w[n]).shape[1]) for n in _REPLICATED) + 1
    rows = -(-rows // 8) * 8
    g_pack = _pack([g[n] for n in _REPLICATED] + [loss_blk[0:1, :]], rows)
    (g_all,) = _all_gather([g_pack], "gather_small")
    res = _adamw(_pack([w[n] for n in _REPLICATED], rows), g_all,
                 _pack([mom[n] for n in _REPLICATED], rows), _pack([var[n] for n in _REPLICATED], rows),
                 "adamw_small")
    n_rows_params = sum(_pack_rows(flat2(w[n]).shape[1]) for n in _REPLICATED)
    loss = res[0][n_rows_params, 0]
    unpacked = [_unpack(r, shapes) for r in res]
    for k, n in enumerate(_REPLICATED):
        out[n] = [unpacked[q][k] for q in range(4)]

    grads = [out[n][0] for n in _WEIGHTS]
    deltas = [out[n][1] for n in _WEIGHTS]
    new_m = [out[n][2] for n in _WEIGHTS]
    new_v = [out[n][3] for n in _WEIGHTS]
    return (loss, grad_x[None], *grads, *deltas, *new_m, *new_v)


def kernel(x, mem, attn_norm_g, w_in, b_forget, fox_out_g, sb_out_g, w_out, xattn_norm_g, mem_norm_g, w_mq, w_mkv, w_mo, ffn_norm_g, w_up, conv_w, conv_b, w_down, final_norm_g, loss_target, m_attn_norm_g, m_w_in, m_b_forget, m_fox_out_g, m_sb_out_g, m_w_out, m_xattn_norm_g, m_mem_norm_g, m_w_mq, m_w_mkv, m_w_mo, m_ffn_norm_g, m_w_up, m_conv_w, m_conv_b, m_w_down, m_final_norm_g, v_attn_norm_g, v_w_in, v_b_forget, v_fox_out_g, v_sb_out_g, v_w_out, v_xattn_norm_g, v_mem_norm_g, v_w_mq, v_w_mkv, v_w_mo, v_ffn_norm_g, v_w_up, v_conv_w, v_conv_b, v_w_down, v_final_norm_g):
    args = dict(locals())
    T = x.shape[1]
    return _step(args, tm=min(T, 512), tq=min(T, 256))
```

```python
import functools

import jax
import jax.numpy as jnp
from jax import lax
from jax.experimental import pallas as pl
from jax.experimental.pallas import tpu as pltpu

F32 = jnp.float32
BF16 = jnp.bfloat16
EPS = 1e-6
NEG = -1e30
LOG2E = 1.4426950408889634

HEAD_DIM = 64
N_FOX = 8
FOX_W = 512
QKV_W = 3072
IN_PAD = 3200
N_MEM_HEADS = 4
MEM_HD = 256
D_FF = 2816
FF_CHUNK = 256
N_DEV = 8

ADAM_LR = 0.001
ADAM_B1 = 0.9
ADAM_B2 = 0.999
ADAM_EPS = 1e-08
ADAM_WD = 0.01
ADAM_STEP = 10

SB_SUM_TERMS = 1

VMEM_LIMIT = 56 * 1024 * 1024
MESH = pl.DeviceIdType.MESH


def _cparams(sem=None):
    return pltpu.CompilerParams(dimension_semantics=sem, vmem_limit_bytes=VMEM_LIMIT)


def _nt(a, b):
    return lax.dot_general(a, b, (((1,), (1,)), ((), ())), preferred_element_type=F32)


def _tn(a, b):
    return lax.dot_general(a, b, (((0,), (0,)), ((), ())), preferred_element_type=F32)


def _nn(a, b):
    return jnp.dot(a, b, preferred_element_type=F32)


def _split_dot(a, m01, terms):
    out = None
    r = a
    for t in range(terms):
        p = r.astype(BF16)
        d = _nn(p, m01)
        out = d if out is None else out + d
        if t + 1 < terms:
            r = r - p.astype(F32)
    return out


def _rstd(xv):
    return lax.rsqrt(jnp.mean(xv * xv, axis=-1, keepdims=True) + EPS)


def _norm_bwd(xv, g, dh):
    r = _rstd(xv)
    xhat = xv * r
    dxhat = dh * g
    dx = r * (dxhat - xhat * jnp.mean(dxhat * xhat, axis=-1, keepdims=True))
    dg = jnp.sum(dh * xhat, axis=0, keepdims=True)
    return dx, dg


def _tile_div(n, cap):
    best = None
    for d in range(128, min(n, cap) + 1, 128):
        if n % d == 0:
            best = d
    assert best is not None, n
    return best


def _inproj_fwd(x, g1, w_qkv, w_f_t, b_f, tm):
    T, D = x.shape
    N = w_qkv.shape[1]
    H = w_f_t.shape[0]

    def body(x_ref, g_ref, w_ref, wf_ref, b_ref, proj_ref, h_ref, xf_ref, c_ref, carry_ref):
        i = pl.program_id(0)

        @pl.when(i == 0)
        def _():
            carry_ref[...] = jnp.zeros_like(carry_ref)

        xv = x_ref[...]
        h = (xv * _rstd(xv) * g_ref[...]).astype(BF16)
        h_ref[...] = h
        for n0 in range(0, N, 512):
            proj_ref[:, n0:n0 + 512] = _nn(h, w_ref[:, n0:n0 + 512]).astype(BF16)
        xf = _nt(wf_ref[...], h) + b_ref[...]
        xf_ref[...] = xf
        logf = jnp.minimum(xf, 0.0) - jnp.log1p(jnp.exp(-jnp.abs(xf)))
        row = lax.broadcasted_iota(jnp.int32, (tm, tm), 0)
        col = lax.broadcasted_iota(jnp.int32, (tm, tm), 1)
        upper = jnp.where(row <= col, 1.0, 0.0).astype(BF16)
        c = _split_dot(logf, upper, 3) + carry_ref[...]
        c_ref[...] = c
        carry_ref[...] = c[:, tm - 1:tm]

    return pl.pallas_call(
        body,
        name="inproj_fwd",
        grid=(T // tm,),
        in_specs=[
            pl.BlockSpec((tm, D), lambda i: (i, 0)),
            pl.BlockSpec((1, D), lambda i: (0, 0)),
            pl.BlockSpec((D, N), lambda i: (0, 0)),
            pl.BlockSpec((H, D), lambda i: (0, 0)),
            pl.BlockSpec((H, 1), lambda i: (0, 0)),
        ],
        out_specs=[
            pl.BlockSpec((tm, N), lambda i: (i, 0)),
            pl.BlockSpec((tm, D), lambda i: (i, 0)),
            pl.BlockSpec((H, tm), lambda i: (0, i)),
            pl.BlockSpec((H, tm), lambda i: (0, i)),
        ],
        out_shape=[
            jax.ShapeDtypeStruct((T, N), BF16),
            jax.ShapeDtypeStruct((T, D), BF16),
            jax.ShapeDtypeStruct((H, T), F32),
            jax.ShapeDtypeStruct((H, T), F32),
        ],
        scratch_shapes=[pltpu.VMEM((H, 1), F32)],
        compiler_params=_cparams(("arbitrary",)),
    )(x, g1, w_qkv, w_f_t, b_f)


def _head_q(q, hh, lane):
    hmask = (lane >= HEAD_DIM * hh) & (lane < HEAD_DIM * (hh + 1))
    qh = jnp.where(hmask, q.astype(F32), 0.0) * (HEAD_DIM ** -0.5)
    return qh.astype(BF16), hmask


def _pipeline3(n, stage_a, stage_b, stage_c, diag_last):
    stage_a(0, 0)
    if diag_last:
        @pl.when(n == 1)
        def _():
            stage_b(0, 0, True)

        @pl.when(n >= 2)
        def _():
            stage_b(0, 0, False)
    else:
        stage_b(0, 0, True)

    @pl.when(n >= 2)
    def _():
        stage_a(1, 1)

    def pair(m, carry):
        t = 2 + 2 * m
        stage_c(t - 2)
        stage_b(t - 1, 1, False)
        stage_a(t, 0)
        stage_c(t - 1)
        stage_b(t, 0, False)
        stage_a(t + 1, 1)
        return carry

    lax.fori_loop(0, (n - 2) // 2, pair, 0)
    odd = n % 2 == 1

    @pl.when((n >= 3) & odd)
    def _():
        stage_c(n - 3)
        stage_b(n - 2, 1, False)
        stage_a(n - 1, 0)

    @pl.when((n >= 2) & odd)
    def _():
        stage_c(n - 2)
        stage_b(n - 1, 0, diag_last)

    @pl.when((n >= 2) & jnp.logical_not(odd))
    def _():
        stage_c(n - 2)
        stage_b(n - 1, 1, diag_last)

    stage_c(n - 1)


def _lanes2(x):
    return jnp.concatenate([x, x], axis=1)


def _fox_fwd(proj, c_col, c_row, tq, gather=()):
    T = proj.shape[0]
    assert tq == 256
    ng = len(gather)
    nq = T // tq

    def body(*refs):
        q_ref, k_ref, v_ref, cq_ref, ck_ref = refs[:5]
        o_ref, lse_ref = refs[5 + ng:7 + ng]
        qh_s, cq_s, z_s, p_s, al_s, m_s, l_s, acc_s = refs[7 + 2 * ng:15 + 2 * ng]
        i = pl.program_id(1)
        if ng:
            pair = pl.program_id(0)
            exchange = _Gather(refs[5:5 + ng], refs[7 + ng:7 + 2 * ng], *refs[15 + 2 * ng:])

            @pl.when((pair == 0) & (i == 0))
            def _():
                exchange.start()

            @pl.when((pair == 1) & (i == 0))
            def _():
                exchange.forward()

        lane = lax.broadcasted_iota(jnp.int32, (1, 128), 1)
        row = lax.broadcasted_iota(jnp.int32, (tq, tq), 0)
        col = lax.broadcasted_iota(jnp.int32, (tq, tq), 1)
        ones = jnp.ones((tq, 128), BF16)
        q = q_ref[...]
        for hh in range(2):
            qh_s[hh] = _head_q(q, hh, lane)[0]
            cq_s[hh] = jnp.broadcast_to(cq_ref[hh], (tq, tq))
        m_s[...] = jnp.full(m_s.shape, NEG, F32)
        l_s[...] = jnp.zeros_like(l_s)
        acc_s[...] = jnp.zeros_like(acc_s)

        def rows(t):
            return pl.ds(pl.multiple_of((i - t) * tq, tq), tq)

        def stage_a(t, slot):
            k = k_ref[rows(t), :]
            for hh in range(2):
                z_s[slot, hh] = _nt(qh_s[hh], k)

        def stage_b(t, slot, diag):
            for hh in range(2):
                s = z_s[slot, hh] + cq_s[hh] - ck_ref[hh, :, rows(t)]
                if diag:
                    s = jnp.where(col <= row, s, NEG)
                m = m_s[hh]
                half = jnp.maximum(s[:, :128], s[:, 128:])
                m_new = jnp.maximum(m, jnp.max(half, axis=1, keepdims=True))
                alpha = jnp.exp(m - m_new)
                p = jnp.exp(s - _lanes2(m_new)).astype(BF16)
                l_s[hh] = alpha * l_s[hh] + _nn(p, ones)
                m_s[hh] = m_new
                al_s[hh] = alpha
                p_s[hh] = p

        def stage_c(t):
            v = v_ref[rows(t), :]
            for hh in range(2):
                acc_s[hh] = al_s[hh] * acc_s[hh] + _nn(p_s[hh], v)

        _pipeline3(i + 1, stage_a, stage_b, stage_c, False)
        l0, l1 = l_s[0], l_s[1]
        o_ref[...] = jnp.where(lane < HEAD_DIM, acc_s[0] / l0, acc_s[1] / l1)
        lse_ref[0] = (m_s[0] + jnp.log(l0))[:, 0:1]
        lse_ref[1] = (m_s[1] + jnp.log(l1))[:, 0:1]
        if ng:
            @pl.when((pair == 3) & (i == nq - 1))
            def _():
                exchange.finish()

    res = pl.pallas_call(
        body,
        name="fox_fwd",
        grid=(4, nq),
        in_specs=[
            pl.BlockSpec((tq, 128), lambda p, i: (i, p)),
            pl.BlockSpec((T, 128), lambda p, i: (0, 4 + p)),
            pl.BlockSpec((T, 128), lambda p, i: (0, 8 + p)),
            pl.BlockSpec((2, tq, 1), lambda p, i: (p, i, 0)),
            pl.BlockSpec((2, 1, T), lambda p, i: (p, 0, 0)),
        ] + [_ANY] * ng,
        out_specs=[
            pl.BlockSpec((tq, 128), lambda p, i: (i, p)),
            pl.BlockSpec((2, tq, 1), lambda p, i: (p, i, 0)),
        ] + [_ANY] * ng,
        out_shape=[
            jax.ShapeDtypeStruct((T, FOX_W), F32),
            jax.ShapeDtypeStruct((N_FOX, T, 1), F32),
        ] + _gathered_shapes(gather),
        scratch_shapes=[
            pltpu.VMEM((2, tq, 128), BF16),
            pltpu.VMEM((2, tq, tq), F32),
            pltpu.VMEM((2, 2, tq, tq), F32),
            pltpu.VMEM((2, tq, tq), BF16),
            pltpu.VMEM((2, tq, 128), F32),
            pltpu.VMEM((2, tq, 128), F32),
            pltpu.VMEM((2, tq, 128), F32),
            pltpu.VMEM((2, tq, 128), F32),
        ] + (_comm_sems(ng) if ng else []),
        compiler_params=_cparams(("arbitrary", "arbitrary")),
    )(proj, proj, proj, c_col, c_row, *gather)
    res = list(res)
    return res[0], res[1], res[2:]


def _sb_logs(zn, strict):
    e = jnp.exp2(jnp.abs(zn) * (-LOG2E))
    L = jnp.minimum(zn, 0.0) - jnp.log(1.0 + e)
    G = L - zn
    if strict is not None:
        L = jnp.where(strict, L, 0.0)
    return L, G


def _sb_fwd(proj, tq):
    T = proj.shape[0]

    def body(q_ref, k_ref, v_ref, o_ref, ltot_ref, qh_s, z_s, g_s, tot_s, run_s, acc_s):
        i = pl.program_id(1)
        lane = lax.broadcasted_iota(jnp.int32, (1, 128), 1)
        row = lax.broadcasted_iota(jnp.int32, (tq, tq), 0)
        col = lax.broadcasted_iota(jnp.int32, (tq, tq), 1)
        strict = col < row
        later = jnp.where(row > col, 1.0, 0.0).astype(BF16)
        q = q_ref[...]
        for hh in range(2):
            qh_s[hh] = -_head_q(q, hh, lane)[0]
        run_s[...] = jnp.zeros_like(run_s)
        acc_s[...] = jnp.zeros_like(acc_s)

        def rows(t):
            return pl.ds(pl.multiple_of((i - t) * tq, tq), tq)

        def stage_a(t, slot):
            k = k_ref[rows(t), :]
            for hh in range(2):
                z_s[slot, hh] = _nt(qh_s[hh], k)

        def stage_b(t, slot, diag):
            for hh in range(2):
                L, g = _sb_logs(z_s[slot, hh], strict if diag else None)
                if diag:
                    g = jnp.where(strict, g, NEG)
                after = _split_dot(L, later, SB_SUM_TERMS)
                g_s[hh] = g + after
                first = L[:, 0:1]
                if SB_SUM_TERMS == 1:
                    first = first.astype(BF16).astype(F32)
                tot_s[hh] = jnp.broadcast_to(after[:, 0:1] + first, (tq, 128))

        def stage_c(t):
            v = v_ref[rows(t), :]
            for hh in range(2):
                run = run_s[hh]
                a = jnp.exp(g_s[hh] + _lanes2(run))
                acc_s[hh] += _nn(a.astype(BF16), v)
                run_s[hh] = run + tot_s[hh]

        _pipeline3(i + 1, stage_a, stage_b, stage_c, False)
        ltot_ref[0] = run_s[0][:, 0:1]
        ltot_ref[1] = run_s[1][:, 0:1]
        o_ref[...] = jnp.where(lane < HEAD_DIM, acc_s[0], acc_s[1])

    return pl.pallas_call(
        body,
        name="sb_fwd",
        grid=(4, T // tq),
        in_specs=[
            pl.BlockSpec((tq, 128), lambda p, i: (i, 12 + p)),
            pl.BlockSpec((T, 128), lambda p, i: (0, 16 + p)),
            pl.BlockSpec((T, 128), lambda p, i: (0, 20 + p)),
        ],
        out_specs=[
            pl.BlockSpec((tq, 128), lambda p, i: (i, p)),
            pl.BlockSpec((2, tq, 1), lambda p, i: (p, i, 0)),
        ],
        out_shape=[
            jax.ShapeDtypeStruct((T, FOX_W), F32),
            jax.ShapeDtypeStruct((N_FOX, T, 1), F32),
        ],
        scratch_shapes=[
            pltpu.VMEM((2, tq, 128), BF16),
            pltpu.VMEM((2, 2, tq, tq), F32),
            pltpu.VMEM((2, tq, tq), F32),
            pltpu.VMEM((2, tq, 128), F32),
            pltpu.VMEM((2, tq, 128), F32),
            pltpu.VMEM((2, tq, 128), F32),
        ],
        compiler_params=_cparams(("arbitrary", "arbitrary")),
    )(proj, proj, proj)


def _post_attn_fwd(fox_o, sb_o, gf, gs, w_out, x, tm):
    T, D = x.shape

    def body(f_ref, s_ref, gf_ref, gs_ref, w_ref, x_ref, x1_ref, mix_ref):
        f = f_ref[...]
        s = s_ref[...]
        mix_ref[:, :FOX_W] = (f * _rstd(f) * gf_ref[...]).astype(BF16)
        mix_ref[:, FOX_W:] = (s * _rstd(s) * gs_ref[...]).astype(BF16)
        x1_ref[...] = x_ref[...] + _nn(mix_ref[...], w_ref[...])

    return pl.pallas_call(
        body,
        name="post_attn_fwd",
        grid=(T // tm,),
        in_specs=[
            pl.BlockSpec((tm, FOX_W), lambda i: (i, 0)),
            pl.BlockSpec((tm, FOX_W), lambda i: (i, 0)),
            pl.BlockSpec((1, FOX_W), lambda i: (0, 0)),
            pl.BlockSpec((1, FOX_W), lambda i: (0, 0)),
            pl.BlockSpec((D, D), lambda i: (0, 0)),
            pl.BlockSpec((tm, D), lambda i: (i, 0)),
        ],
        out_specs=[
            pl.BlockSpec((tm, D), lambda i: (i, 0)),
            pl.BlockSpec((tm, D), lambda i: (i, 0)),
        ],
        out_shape=[jax.ShapeDtypeStruct((T, D), F32), jax.ShapeDtypeStruct((T, D), BF16)],
        compiler_params=_cparams(("arbitrary",)),
    )(fox_o, sb_o, gf, gs, w_out, x)


def _mem_kv_fwd(mem, gm, w_mkv):
    M, D = mem.shape
    N = w_mkv.shape[1]

    def body(mem_ref, g_ref, w_ref, m_ref, kv_ref):
        mv = mem_ref[...]
        m = (mv * _rstd(mv) * g_ref[...]).astype(BF16)
        m_ref[...] = m
        for n0 in range(0, N, 512):
            kv_ref[:, n0:n0 + 512] = _nn(m, w_ref[:, n0:n0 + 512]).astype(BF16)

    return pl.pallas_call(
        body,
        name="mem_kv_fwd",
        out_shape=[jax.ShapeDtypeStruct((M, D), BF16), jax.ShapeDtypeStruct((M, N), BF16)],
        compiler_params=_cparams(),
    )(mem, gm, w_mkv)


def _xattn_probs(qb, kv, h):
    k = kv[:, h * MEM_HD:(h + 1) * MEM_HD]
    s = _nt(qb[:, h * MEM_HD:(h + 1) * MEM_HD], k) * (MEM_HD ** -0.5)
    s = s - jnp.max(s, axis=1, keepdims=True)
    p = jnp.exp(s)
    return p / jnp.sum(p, axis=1, keepdims=True)


def _xattn_fwd(x1, g2, w_mq, kv, w_mo, tm):
    T, D = x1.shape
    M = kv.shape[0]

    def body(x_ref, g_ref, wq_ref, kv_ref, wo_ref, x2_ref, h_ref, q_ref, om_ref):
        xv = x_ref[...]
        h = (xv * _rstd(xv) * g_ref[...]).astype(BF16)
        h_ref[...] = h
        q_ref[...] = _nn(h, wq_ref[...]).astype(BF16)
        qb = q_ref[...]
        kvv = kv_ref[...]
        for hd in range(N_MEM_HEADS):
            p = _xattn_probs(qb, kvv, hd)
            v = kvv[:, D + hd * MEM_HD:D + (hd + 1) * MEM_HD]
            om_ref[:, hd * MEM_HD:(hd + 1) * MEM_HD] = _nn(p.astype(BF16), v).astype(BF16)
        x2_ref[...] = xv + _nn(om_ref[...], wo_ref[...])

    return pl.pallas_call(
        body,
        name="xattn_fwd",
        grid=(T // tm,),
        in_specs=[
            pl.BlockSpec((tm, D), lambda i: (i, 0)),
            pl.BlockSpec((1, D), lambda i: (0, 0)),
            pl.BlockSpec((D, D), lambda i: (0, 0)),
            pl.BlockSpec((M, 2 * D), lambda i: (0, 0)),
            pl.BlockSpec((D, D), lambda i: (0, 0)),
        ],
        out_specs=[pl.BlockSpec((tm, D), lambda i: (i, 0))] * 4,
        out_shape=[jax.ShapeDtypeStruct((T, D), F32)] + [jax.ShapeDtypeStruct((T, D), BF16)] * 3,
        compiler_params=_cparams(("arbitrary",)),
    )(x1, g2, w_mq, kv, w_mo)


def _conv_taps(ext_ref, tm, back):
    if back:
        return ext_ref[pl.ds(6, tm), :], ext_ref[pl.ds(7, tm), :], ext_ref[pl.ds(8, tm), :]
    return ext_ref[pl.ds(0, tm), :], ext_ref[pl.ds(1, tm), :], ext_ref[pl.ds(2, tm), :]


def _ffn_fwd(x2, g3, w_up, conv_w, conv_b, w_down, tm):
    T, D = x2.shape
    fc = FF_CHUNK
    nj = D_FF // fc

    def body(x_ref, g_ref, wg_ref, wv_ref, cwg_ref, cwv_ref, cbg_ref, cbv_ref, wd_ref,
             x3_ref, h_ref, ug_ref, uv_ref, yg_ref, yv_ref, a_ref, acc_ref, carry_ref, ext_ref):
        i = pl.program_id(0)
        j = pl.program_id(1)

        @pl.when(j == 0)
        def _():
            xv = x_ref[...]
            h_ref[...] = (xv * _rstd(xv) * g_ref[...]).astype(BF16)
            acc_ref[...] = xv

        @pl.when(i == 0)
        def _():
            carry_ref[j] = jnp.zeros((2, 8, fc), F32)

        h = h_ref[...]
        halves = []
        for part, (w_ref, cw_ref, cb_ref, u_ref, y_ref) in enumerate(
                ((wg_ref, cwg_ref, cbg_ref, ug_ref, yg_ref), (wv_ref, cwv_ref, cbv_ref, uv_ref, yv_ref))):
            u = _nn(h, w_ref[...])
            u_ref[...] = u.astype(BF16)
            ext = ext_ref.at[part]
            ext[pl.ds(0, 8), :] = carry_ref[j, part]
            ext[pl.ds(8, tm), :] = u
            carry_ref[j, part] = u[tm - 8:, :]
            u2, u1, u0 = _conv_taps(ext, tm, True)
            cw = cw_ref[...]
            y = cb_ref[...] + cw[0:1] * u2 + cw[1:2] * u1 + cw[2:3] * u0
            y_ref[...] = y.astype(BF16)
            halves.append(y)
        gate, val = halves
        a = (gate * jax.nn.sigmoid(gate) * val).astype(BF16)
        a_ref[...] = a
        acc_ref[...] += _nn(a, wd_ref[...])

        @pl.when(j == nj - 1)
        def _():
            x3_ref[...] = acc_ref[...]

    return pl.pallas_call(
        body,
        name="ffn_fwd",
        grid=(T // tm, nj),
        in_specs=[
            pl.BlockSpec((tm, D), lambda i, j: (i, 0)),
            pl.BlockSpec((1, D), lambda i, j: (0, 0)),
            pl.BlockSpec((D, fc), lambda i, j: (0, j)),
            pl.BlockSpec((D, fc), lambda i, j: (0, nj + j)),
            pl.BlockSpec((3, fc), lambda i, j: (0, j)),
            pl.BlockSpec((3, fc), lambda i, j: (0, nj + j)),
            pl.BlockSpec((1, fc), lambda i, j: (0, j)),
            pl.BlockSpec((1, fc), lambda i, j: (0, nj + j)),
            pl.BlockSpec((fc, D), lambda i, j: (j, 0)),
        ],
        out_specs=[
            pl.BlockSpec((tm, D), lambda i, j: (i, 0)),
            pl.BlockSpec((tm, D), lambda i, j: (i, 0)),
        ] + [pl.BlockSpec((tm, fc), lambda i, j: (i, j))] * 5,
        out_shape=[
            jax.ShapeDtypeStruct((T, D), F32),
            jax.ShapeDtypeStruct((T, D), BF16),
        ] + [jax.ShapeDtypeStruct((T, D_FF), BF16)] * 5,
        scratch_shapes=[
            pltpu.VMEM((tm, D), F32),
            pltpu.VMEM((nj, 2, 8, fc), F32),
            pltpu.VMEM((2, tm + 8, fc), F32),
        ],
        compiler_params=_cparams(("arbitrary", "arbitrary")),
    )(x2, g3, w_up, w_up, conv_w, conv_w, conv_b, conv_b, w_down)


def _loss_head(x3, gfin, target, tm):
    T, D = x3.shape

    def body(x_ref, g_ref, t_ref, dx_ref, loss_ref, dg_ref):
        i = pl.program_id(0)

        @pl.when(i == 0)
        def _():
            loss_ref[...] = jnp.zeros_like(loss_ref)
            dg_ref[...] = jnp.zeros_like(dg_ref)

        xv = x_ref[...]
        g = g_ref[...]
        r = _rstd(xv)
        xhat = xv * r
        err = xhat * g - t_ref[...]
        part = jnp.sum(jnp.sum(err * err, axis=1, keepdims=True), axis=0, keepdims=True) * (0.5 / D)
        loss_ref[...] += jnp.broadcast_to(part, loss_ref.shape)
        dy = err * (1.0 / D)
        dg_ref[...] += jnp.sum(dy * xhat, axis=0, keepdims=True)
        dxhat = dy * g
        dx_ref[...] = r * (dxhat - xhat * jnp.mean(dxhat * xhat, axis=-1, keepdims=True))

    return pl.pallas_call(
        body,
        name="loss_head",
        grid=(T // tm,),
        in_specs=[
            pl.BlockSpec((tm, D), lambda i: (i, 0)),
            pl.BlockSpec((1, D), lambda i: (0, 0)),
            pl.BlockSpec((tm, D), lambda i: (i, 0)),
        ],
        out_specs=[
            pl.BlockSpec((tm, D), lambda i: (i, 0)),
            pl.BlockSpec((8, 128), lambda i: (0, 0)),
            pl.BlockSpec((1, D), lambda i: (0, 0)),
        ],
        out_shape=[
            jax.ShapeDtypeStruct((T, D), F32),
            jax.ShapeDtypeStruct((8, 128), F32),
            jax.ShapeDtypeStruct((1, D), F32),
        ],
        compiler_params=_cparams(("arbitrary",)),
    )(x3, gfin, target)


def _ffn_bwd(dx3, x2, g3, ug, uv, yg, yv, conv_w, w_down, w_up, tm):
    T, D = x2.shape
    fc = FF_CHUNK
    nj = D_FF // fc
    nt = T // tm

    def rev(i):
        return nt - 1 - i

    def body(dx3_ref, x_ref, g_ref, ug_ref, uv_ref, yg_ref, yv_ref, cwg_ref, cwv_ref,
             wd_ref, wug_ref, wuv_ref,
             dx2_ref, dug_ref, duv_ref, dg_ref, dcg_ref, dcv_ref,
             acc_ref, carry_ref, ext_ref):
        i = pl.program_id(0)
        j = pl.program_id(1)
        cols = pl.ds(pl.multiple_of(j * fc, fc), fc)

        @pl.when(j == 0)
        def _():
            acc_ref[...] = jnp.zeros_like(acc_ref)

        @pl.when((i == 0) & (j == 0))
        def _():
            dg_ref[...] = jnp.zeros_like(dg_ref)
            dcg_ref[...] = jnp.zeros_like(dcg_ref)
            dcv_ref[...] = jnp.zeros_like(dcv_ref)

        @pl.when(i == 0)
        def _():
            carry_ref[j] = jnp.zeros((2, 8, fc), F32)

        da = _nt(dx3_ref[...].astype(BF16), wd_ref[...])
        gate = yg_ref[...].astype(F32)
        val = yv_ref[...].astype(F32)
        sig = jax.nn.sigmoid(gate)
        silu = gate * sig
        dys = (da * val * (sig * (1.0 + gate * (1.0 - sig))), da * silu)
        for part, (dy, u_ref, cw_ref, du_ref, wu_ref, dc_ref) in enumerate(
                ((dys[0], ug_ref, cwg_ref, dug_ref, wug_ref, dcg_ref),
                 (dys[1], uv_ref, cwv_ref, duv_ref, wuv_ref, dcv_ref))):
            ext = ext_ref.at[part]
            ext[pl.ds(0, tm), :] = dy
            ext[pl.ds(tm, 8), :] = carry_ref[j, part]
            carry_ref[j, part] = dy[:8, :]
            d0, d1, d2 = _conv_taps(ext, tm, False)
            u = u_ref[...].astype(F32)
            upd = jnp.concatenate([
                jnp.sum(u * d2, axis=0, keepdims=True),
                jnp.sum(u * d1, axis=0, keepdims=True),
                jnp.sum(u * d0, axis=0, keepdims=True),
                jnp.sum(d0, axis=0, keepdims=True),
                jnp.zeros((4, fc), F32)], axis=0)
            dc_ref[:, cols] += upd
            cw = cw_ref[...]
            du = (cw[2:3] * d0 + cw[1:2] * d1 + cw[0:1] * d2).astype(BF16)
            du_ref[...] = du
            acc_ref[...] += _nt(du, wu_ref[...])

        @pl.when(j == nj - 1)
        def _():
            dx, dg = _norm_bwd(x_ref[...], g_ref[...], acc_ref[...])
            dx2_ref[...] = dx3_ref[...] + dx
            dg_ref[...] += dg

    return pl.pallas_call(
        body,
        name="ffn_bwd",
        grid=(nt, nj),
        in_specs=[
            pl.BlockSpec((tm, D), lambda i, j: (rev(i), 0)),
            pl.BlockSpec((tm, D), lambda i, j: (rev(i), 0)),
            pl.BlockSpec((1, D), lambda i, j: (0, 0)),
            pl.BlockSpec((tm, fc), lambda i, j: (rev(i), j)),
            pl.BlockSpec((tm, fc), lambda i, j: (rev(i), j)),
            pl.BlockSpec((tm, fc), lambda i, j: (rev(i), j)),
            pl.BlockSpec((tm, fc), lambda i, j: (rev(i), j)),
            pl.BlockSpec((3, fc), lambda i, j: (0, j)),
            pl.BlockSpec((3, fc), lambda i, j: (0, nj + j)),
            pl.BlockSpec((fc, D), lambda i, j: (j, 0)),
            pl.BlockSpec((D, fc), lambda i, j: (0, j)),
            pl.BlockSpec((D, fc), lambda i, j: (0, nj + j)),
        ],
        out_specs=[
            pl.BlockSpec((tm, D), lambda i, j: (rev(i), 0)),
            pl.BlockSpec((tm, fc), lambda i, j: (rev(i), j)),
            pl.BlockSpec((tm, fc), lambda i, j: (rev(i), j)),
            pl.BlockSpec((1, D), lambda i, j: (0, 0)),
            pl.BlockSpec((8, D_FF), lambda i, j: (0, 0)),
            pl.BlockSpec((8, D_FF), lambda i, j: (0, 0)),
        ],
        out_shape=[
            jax.ShapeDtypeStruct((T, D), F32),
            jax.ShapeDtypeStruct((T, D_FF), BF16),
            jax.ShapeDtypeStruct((T, D_FF), BF16),
            jax.ShapeDtypeStruct((1, D), F32),
            jax.ShapeDtypeStruct((8, D_FF), F32),
            jax.ShapeDtypeStruct((8, D_FF), F32),
        ],
        scratch_shapes=[
            pltpu.VMEM((tm, D), F32),
            pltpu.VMEM((nj, 2, 8, fc), F32),
            pltpu.VMEM((2, tm + 8, fc), F32),
        ],
        compiler_params=_cparams(("arbitrary", "arbitrary")),
    )(dx3, x2, g3, ug, uv, yg, yv, conv_w, conv_w, w_down, w_up, w_up)


def _xattn_bwd(dx2, x1, g2, qb, kv, w_mo, w_mq, tm):
    T, D = x1.shape
    M = kv.shape[0]

    def body(dx2_ref, x_ref, g_ref, q_ref, kv_ref, wo_ref, wq_ref, dx1_ref, dq_ref, dkv_ref, dg_ref):
        i = pl.program_id(0)

        @pl.when(i == 0)
        def _():
            dkv_ref[...] = jnp.zeros_like(dkv_ref)
            dg_ref[...] = jnp.zeros_like(dg_ref)

        dxv = dx2_ref[...]
        dom = _nt(dxv.astype(BF16), wo_ref[...]).astype(BF16)
        qb_ = q_ref[...]
        kvv = kv_ref[...]
        for hd in range(N_MEM_HEADS):
            sl = slice(hd * MEM_HD, (hd + 1) * MEM_HD)
            vsl = slice(D + hd * MEM_HD, D + (hd + 1) * MEM_HD)
            p = _xattn_probs(qb_, kvv, hd)
            dp = _nt(dom[:, sl], kvv[:, vsl])
            ds = (p * (dp - jnp.sum(p * dp, axis=1, keepdims=True)) * (MEM_HD ** -0.5)).astype(BF16)
            dq_ref[:, sl] = _nn(ds, kvv[:, sl]).astype(BF16)
            dkv_ref[:, sl] += _tn(ds, qb_[:, sl])
            dkv_ref[:, vsl] += _tn(p.astype(BF16), dom[:, sl])
        dh = _nt(dq_ref[...], wq_ref[...])
        dx, dg = _norm_bwd(x_ref[...], g_ref[...], dh)
        dx1_ref[...] = dxv + dx
        dg_ref[...] += dg

    return pl.pallas_call(
        body,
        name="xattn_bwd",
        grid=(T // tm,),
        in_specs=[
            pl.BlockSpec((tm, D), lambda i: (i, 0)),
            pl.BlockSpec((tm, D), lambda i: (i, 0)),
            pl.BlockSpec((1, D), lambda i: (0, 0)),
            pl.BlockSpec((tm, D), lambda i: (i, 0)),
            pl.BlockSpec((M, 2 * D), lambda i: (0, 0)),
            pl.BlockSpec((D, D), lambda i: (0, 0)),
            pl.BlockSpec((D, D), lambda i: (0, 0)),
        ],
        out_specs=[
            pl.BlockSpec((tm, D), lambda i: (i, 0)),
            pl.BlockSpec((tm, D), lambda i: (i, 0)),
            pl.BlockSpec((M, 2 * D), lambda i: (0, 0)),
            pl.BlockSpec((1, D), lambda i: (0, 0)),
        ],
        out_shape=[
            jax.ShapeDtypeStruct((T, D), F32),
            jax.ShapeDtypeStruct((T, D), BF16),
            jax.ShapeDtypeStruct((M, 2 * D), F32),
            jax.ShapeDtypeStruct((1, D), F32),
        ],
        compiler_params=_cparams(("arbitrary",)),
    )(dx2, x1, g2, qb, kv, w_mo, w_mq)


def _mem_kv_bwd(mem, gm, mb, dkv, w_mkv):
    M, D = mem.shape
    N = dkv.shape[1]

    def body(mem_ref, g_ref, m_ref, dkv_ref, w_ref, dw_ref, dg_ref):
        dkvb = dkv_ref[...].astype(BF16)
        for n0 in range(0, N, 512):
            dw_ref[:, n0:n0 + 512] = _tn(m_ref[...], dkvb[:, n0:n0 + 512])
        dm = _nt(dkvb, w_ref[...])
        mv = mem_ref[...]
        dg_ref[...] = jnp.sum(dm * (mv * _rstd(mv)), axis=0, keepdims=True)

    return pl.pallas_call(
        body,
        name="mem_kv_bwd",
        out_shape=[jax.ShapeDtypeStruct((D, N), F32), jax.ShapeDtypeStruct((1, D), F32)],
        compiler_params=_cparams(),
    )(mem, gm, mb, dkv, w_mkv)


def _post_attn_bwd(dx1, fox_o, sb_o, gf, gs, w_out, tm):
    T, D = dx1.shape

    def body(dx_ref, f_ref, s_ref, gf_ref, gs_ref, w_ref, df_ref, ds_ref, dgf_ref, dgs_ref):
        i = pl.program_id(0)

        @pl.when(i == 0)
        def _():
            dgf_ref[...] = jnp.zeros_like(dgf_ref)
            dgs_ref[...] = jnp.zeros_like(dgs_ref)

        dmix = _nt(dx_ref[...].astype(BF16), w_ref[...])
        d, dg = _norm_bwd(f_ref[...], gf_ref[...], dmix[:, :FOX_W])
        df_ref[...] = d
        dgf_ref[...] += dg
        d, dg = _norm_bwd(s_ref[...], gs_ref[...], dmix[:, FOX_W:])
        ds_ref[...] = d
        dgs_ref[...] += dg

    return pl.pallas_call(
        body,
        name="post_attn_bwd",
        grid=(T // tm,),
        in_specs=[
            pl.BlockSpec((tm, D), lambda i: (i, 0)),
            pl.BlockSpec((tm, FOX_W), lambda i: (i, 0)),
            pl.BlockSpec((tm, FOX_W), lambda i: (i, 0)),
            pl.BlockSpec((1, FOX_W), lambda i: (0, 0)),
            pl.BlockSpec((1, FOX_W), lambda i: (0, 0)),
            pl.BlockSpec((D, D), lambda i: (0, 0)),
        ],
        out_specs=[
            pl.BlockSpec((tm, FOX_W), lambda i: (i, 0)),
            pl.BlockSpec((tm, FOX_W), lambda i: (i, 0)),
            pl.BlockSpec((1, FOX_W), lambda i: (0, 0)),
            pl.BlockSpec((1, FOX_W), lambda i: (0, 0)),
        ],
        out_shape=[
            jax.ShapeDtypeStruct((T, FOX_W), F32),
            jax.ShapeDtypeStruct((T, FOX_W), F32),
            jax.ShapeDtypeStruct((1, FOX_W), F32),
            jax.ShapeDtypeStruct((1, FOX_W), F32),
        ],
        compiler_params=_cparams(("arbitrary",)),
    )(dx1, fox_o, sb_o, gf, gs, w_out)


def _sb_bwd(proj, ltot, d_o, tq):
    T = proj.shape[0]

    def body(q_ref, k_ref, v_ref, lt_ref, do_ref, dq_ref, dk_ref, dv_ref,
             qh_s, doh_s, lt_s, z_s, da_s, ab_s, dzb_s, run_s, runw_s, dq_s):
        i = pl.program_id(1)

        @pl.when(i == 0)
        def _():
            dk_ref[...] = jnp.zeros_like(dk_ref)
            dv_ref[...] = jnp.zeros_like(dv_ref)

        lane = lax.broadcasted_iota(jnp.int32, (1, 128), 1)
        row = lax.broadcasted_iota(jnp.int32, (tq, tq), 0)
        col = lax.broadcasted_iota(jnp.int32, (tq, tq), 1)
        strict = col < row
        upto = jnp.where(row <= col, 1.0, 0.0).astype(BF16)
        before = jnp.where(row < col, 1.0, 0.0).astype(BF16)
        q = q_ref[...]
        dov = do_ref[...]
        for hh in range(2):
            qh, hmask = _head_q(q, hh, lane)
            qh_s[hh] = -qh
            doh_s[hh] = jnp.where(hmask, dov, 0.0).astype(BF16)
            lt_s[hh] = jnp.broadcast_to(lt_ref[hh], (tq, 128))
        run_s[...] = jnp.zeros_like(run_s)
        runw_s[...] = jnp.zeros_like(runw_s)
        dq_s[...] = jnp.zeros_like(dq_s)

        def rows(t):
            return pl.ds(pl.multiple_of(t * tq, tq), tq)

        def stage_a(t, slot):
            k = k_ref[rows(t), :]
            v = v_ref[rows(t), :]
            for hh in range(2):
                z_s[slot, hh] = _nt(qh_s[hh], k)
                da_s[slot, hh] = _nt(doh_s[hh], v)

        def stage_b(t, slot, diag):
            for hh in range(2):
                L, g = _sb_logs(z_s[slot, hh], strict if diag else None)
                upto_s = _split_dot(L, upto, SB_SUM_TERMS)
                run = run_s[hh]
                arg = (g + _lanes2(lt_s[hh] - run)) - upto_s
                if diag:
                    arg = jnp.where(strict, arg, NEG)
                a = jnp.exp(arg)
                w = a * da_s[slot, hh]
                w_before = _split_dot(w, before, SB_SUM_TERMS)
                run_w = runw_s[hh]
                d_keep = w_before + _lanes2(run_w)
                beta = jnp.exp(g)
                ndz = beta * (w + d_keep) - w
                if diag:
                    ndz = jnp.where(strict, ndz, 0.0)
                dzb_s[hh] = ndz.astype(BF16)
                ab_s[hh] = a.astype(BF16)
                run_s[hh] = run + jnp.broadcast_to(upto_s[:, tq - 1:tq], (tq, 128))
                runw_s[hh] = run_w + jnp.broadcast_to(w_before[:, tq - 1:tq] + w[:, tq - 1:tq], (tq, 128))

        def stage_c(t):
            k = k_ref[rows(t), :]
            dk_blk = None
            dv_blk = None
            for hh in range(2):
                dzb = dzb_s[hh]
                dq_s[hh] += _nn(dzb, k)
                dk_h = _tn(dzb, qh_s[hh])
                dv_h = _tn(ab_s[hh], doh_s[hh])
                dk_blk = dk_h if dk_blk is None else dk_blk + dk_h
                dv_blk = dv_h if dv_blk is None else dv_blk + dv_h
            dk_ref[rows(t), :] += dk_blk
            dv_ref[rows(t), :] += dv_blk

        _pipeline3(i + 1, stage_a, stage_b, stage_c, True)
        dq_ref[...] = (jnp.where(lane < HEAD_DIM, dq_s[0], dq_s[1]) * -(HEAD_DIM ** -0.5)).astype(BF16)

    return pl.pallas_call(
        body,
        name="sb_bwd",
        grid=(4, T // tq),
        in_specs=[
            pl.BlockSpec((tq, 128), lambda p, i: (i, 12 + p)),
            pl.BlockSpec((T, 128), lambda p, i: (0, 16 + p)),
            pl.BlockSpec((T, 128), lambda p, i: (0, 20 + p)),
            pl.BlockSpec((2, tq, 1), lambda p, i: (p, i, 0)),
            pl.BlockSpec((tq, 128), lambda p, i: (i, p)),
        ],
        out_specs=[
            pl.BlockSpec((tq, 128), lambda p, i: (i, p)),
            pl.BlockSpec((T, 128), lambda p, i: (0, p)),
            pl.BlockSpec((T, 128), lambda p, i: (0, p)),
        ],
        out_shape=[
            jax.ShapeDtypeStruct((T, FOX_W), BF16),
            jax.ShapeDtypeStruct((T, FOX_W), F32),
            jax.ShapeDtypeStruct((T, FOX_W), F32),
        ],
        scratch_shapes=[
            pltpu.VMEM((2, tq, 128), BF16),
            pltpu.VMEM((2, tq, 128), BF16),
            pltpu.VMEM((2, tq, 128), F32),
            pltpu.VMEM((2, 2, tq, tq), F32),
            pltpu.VMEM((2, 2, tq, tq), F32),
            pltpu.VMEM((2, tq, tq), BF16),
            pltpu.VMEM((2, tq, tq), BF16),
            pltpu.VMEM((2, tq, 128), F32),
            pltpu.VMEM((2, tq, 128), F32),
            pltpu.VMEM((2, tq, 128), F32),
        ],
        compiler_params=_cparams(("arbitrary", "arbitrary")),
    )(proj, proj, proj, ltot, d_o)


def _fox_bwd(proj, c_col, c_row, lse, d_o, o, tq, scatter=()):
    T = proj.shape[0]
    ns = len(scatter)
    nq = T // tq

    def body(*refs):
        q_ref, k_ref, v_ref, cq_ref, ck_ref, lse_ref, do_ref, o_ref = refs[:8]
        dq_ref, dk_ref, dv_ref, dck_ref, dcq_ref = refs[8 + ns:13 + ns]
        qh_s, doh_s, delta_s, shift_s, z_s, dp_s, pb_s, dsb_s, rs_s, dq_s = refs[13 + 2 * ns:23 + 2 * ns]
        i = pl.program_id(1)
        if ns:
            pair = pl.program_id(0)
            exchange = _Scatter(refs[8:8 + ns], refs[13 + ns:13 + 2 * ns], *refs[23 + 2 * ns:])

            @pl.when((pair == 0) & (i == 0))
            def _():
                exchange.start()

        @pl.when(i == 0)
        def _():
            dk_ref[...] = jnp.zeros_like(dk_ref)
            dv_ref[...] = jnp.zeros_like(dv_ref)
            dck_ref[...] = jnp.zeros_like(dck_ref)

        lane = lax.broadcasted_iota(jnp.int32, (1, 128), 1)
        row = lax.broadcasted_iota(jnp.int32, (tq, tq), 0)
        col = lax.broadcasted_iota(jnp.int32, (tq, tq), 1)
        q = q_ref[...]
        dov = do_ref[...]
        ov = o_ref[...]
        for hh in range(2):
            qh, hmask = _head_q(q, hh, lane)
            dohb = jnp.where(hmask, dov, 0.0).astype(BF16)
            qh_s[hh] = qh
            doh_s[hh] = dohb
            delta_s[hh] = jnp.broadcast_to(jnp.sum(dohb.astype(F32) * ov, axis=1, keepdims=True), (tq, tq))
            shift_s[hh] = jnp.broadcast_to(cq_ref[hh] - lse_ref[hh], (tq, tq))
        rs_s[...] = jnp.zeros_like(rs_s)
        dq_s[...] = jnp.zeros_like(dq_s)

        def rows(t):
            return pl.ds(pl.multiple_of((i - t) * tq, tq), tq)

        def stage_a(t, slot):
            k = k_ref[rows(t), :]
            v = v_ref[rows(t), :]
            for hh in range(2):
                z_s[slot, hh] = _nt(qh_s[hh], k)
                dp_s[slot, hh] = _nt(doh_s[hh], v)

        def stage_b(t, slot, diag):
            for hh in range(2):
                s = z_s[slot, hh] + shift_s[hh] - ck_ref[hh, :, rows(t)]
                if diag:
                    s = jnp.where(col <= row, s, NEG)
                p = jnp.exp(s)
                ds = p * (dp_s[slot, hh] - delta_s[hh])
                pb_s[hh] = p.astype(BF16)
                dsb_s[hh] = ds.astype(BF16)
                dck_ref[hh, :, rows(t)] += jnp.sum(ds, axis=0, keepdims=True)
                rs_s[hh] += jnp.sum(ds, axis=1, keepdims=True)

        def stage_c(t):
            k = k_ref[rows(t), :]
            dk_blk = None
            dv_blk = None
            for hh in range(2):
                dsb = dsb_s[hh]
                dq_s[hh] += _nn(dsb, k)
                dk_h = _tn(dsb, qh_s[hh])
                dv_h = _tn(pb_s[hh], doh_s[hh])
                dk_blk = dk_h if dk_blk is None else dk_blk + dk_h
                dv_blk = dv_h if dv_blk is None else dv_blk + dv_h
            dk_ref[rows(t), :] += dk_blk
            dv_ref[rows(t), :] += dv_blk

        _pipeline3(i + 1, stage_a, stage_b, stage_c, False)
        dcq_ref[0] = rs_s[0][:, 0:1]
        dcq_ref[1] = rs_s[1][:, 0:1]
        dq_ref[...] = (jnp.where(lane < HEAD_DIM, dq_s[0], dq_s[1]) * (HEAD_DIM ** -0.5)).astype(BF16)
        if ns:
            @pl.when((pair == 3) & (i == nq - 1))
            def _():
                exchange.finish()

    res = pl.pallas_call(
        body,
        name="fox_bwd",
        grid=(4, nq),
        in_specs=[
            pl.BlockSpec((tq, 128), lambda p, i: (i, p)),
            pl.BlockSpec((T, 128), lambda p, i: (0, 4 + p)),
            pl.BlockSpec((T, 128), lambda p, i: (0, 8 + p)),
            pl.BlockSpec((2, tq, 1), lambda p, i: (p, i, 0)),
            pl.BlockSpec((2, 1, T), lambda p, i: (p, 0, 0)),
            pl.BlockSpec((2, tq, 1), lambda p, i: (p, i, 0)),
            pl.BlockSpec((tq, 128), lambda p, i: (i, p)),
            pl.BlockSpec((tq, 128), lambda p, i: (i, p)),
        ] + [_ANY] * ns,
        out_specs=[
            pl.BlockSpec((tq, 128), lambda p, i: (i, p)),
            pl.BlockSpec((T, 128), lambda p, i: (0, p)),
            pl.BlockSpec((T, 128), lambda p, i: (0, p)),
            pl.BlockSpec((2, 1, T), lambda p, i: (p, 0, 0)),
            pl.BlockSpec((2, tq, 1), lambda p, i: (p, i, 0)),
        ] + [_ANY] * ns,
        out_shape=[
            jax.ShapeDtypeStruct((T, FOX_W), BF16),
            jax.ShapeDtypeStruct((T, FOX_W), F32),
            jax.ShapeDtypeStruct((T, FOX_W), F32),
            jax.ShapeDtypeStruct((N_FOX, 1, T), F32),
            jax.ShapeDtypeStruct((N_FOX, T, 1), F32),
        ] + [jax.ShapeDtypeStruct(b.shape, b.dtype) for b in scatter],
        scratch_shapes=[
            pltpu.VMEM((2, tq, 128), BF16),
            pltpu.VMEM((2, tq, 128), BF16),
            pltpu.VMEM((2, tq, tq), F32),
            pltpu.VMEM((2, tq, tq), F32),
            pltpu.VMEM((2, 2, tq, tq), F32),
            pltpu.VMEM((2, 2, tq, tq), F32),
            pltpu.VMEM((2, tq, tq), BF16),
            pltpu.VMEM((2, tq, tq), BF16),
            pltpu.VMEM((2, tq, 128), F32),
            pltpu.VMEM((2, tq, 128), F32),
        ] + (_comm_sems(ns) if ns else []),
        compiler_params=_cparams(("arbitrary", "arbitrary")),
    )(proj, proj, proj, c_col, c_row, lse, d_o, o, *scatter)
    res = list(res)
    return (*res[:5], res[5:])


def _forget_bwd(dcq, dck, xf, tc):
    H, T = xf.shape
    nc = T // tc

    def body(dcq_ref, dck_ref, xf_ref, dxf_ref, db_ref):
        row = lax.broadcasted_iota(jnp.int32, (tc, tc), 0)
        col = lax.broadcasted_iota(jnp.int32, (tc, tc), 1)
        from_here = jnp.where(row >= col, 1.0, 0.0).astype(BF16)

        def chunk(n, carry):
            run, db = carry
            cs = pl.multiple_of((nc - 1 - n) * tc, tc)
            dc = dcq_ref[:, pl.ds(cs, tc)] - dck_ref[:, pl.ds(cs, tc)]
            dlogf = _split_dot(dc, from_here, 3) + run
            xfv = xf_ref[:, pl.ds(cs, tc)]
            dxf = dlogf * jax.nn.sigmoid(-xfv)
            dxf_ref[:, pl.ds(cs, tc)] = dxf
            return dlogf[:, 0:1], db + jnp.sum(dxf, axis=1, keepdims=True)

        _, db = lax.fori_loop(0, nc, chunk, (jnp.zeros((H, 1), F32), jnp.zeros((H, 1), F32)))
        db_ref[...] = db

    return pl.pallas_call(
        body,
        name="forget_bwd",
        out_shape=[jax.ShapeDtypeStruct((H, T), F32), jax.ShapeDtypeStruct((H, 1), F32)],
        compiler_params=_cparams(),
    )(dcq, dck, xf)


def _inproj_bwd(dproj, w_in_pad, x, g1, dx1, tm, scatter=()):
    T, D = x.shape
    N = dproj.shape[1]
    ns = len(scatter)
    nt = T // tm

    def body(*refs):
        dp_ref, w_ref, x_ref, g_ref, dx1_ref = refs[:5]
        dx_ref, dg_ref = refs[5 + ns:7 + ns]
        i = pl.program_id(0)
        if ns:
            exchange = _Scatter(refs[5:5 + ns], refs[7 + ns:7 + 2 * ns], *refs[7 + 2 * ns:])

            @pl.when(i == 0)
            def _():
                exchange.start()

        @pl.when(i == 0)
        def _():
            dg_ref[...] = jnp.zeros_like(dg_ref)

        dh = _nt(dp_ref[...], w_ref[...])
        dx, dg = _norm_bwd(x_ref[...], g_ref[...], dh)
        dx_ref[...] = dx1_ref[...] + dx
        dg_ref[...] += dg
        if ns:
            @pl.when(i == nt - 1)
            def _():
                exchange.finish()

    res = pl.pallas_call(
        body,
        name="inproj_bwd",
        grid=(nt,),
        in_specs=[
            pl.BlockSpec((tm, N), lambda i: (i, 0)),
            pl.BlockSpec((D, N), lambda i: (0, 0)),
            pl.BlockSpec((tm, D), lambda i: (i, 0)),
            pl.BlockSpec((1, D), lambda i: (0, 0)),
            pl.BlockSpec((tm, D), lambda i: (i, 0)),
        ] + [_ANY] * ns,
        out_specs=[
            pl.BlockSpec((tm, D), lambda i: (i, 0)),
            pl.BlockSpec((1, D), lambda i: (0, 0)),
        ] + [_ANY] * ns,
        out_shape=[jax.ShapeDtypeStruct((T, D), F32), jax.ShapeDtypeStruct((1, D), F32)]
        + [jax.ShapeDtypeStruct(b.shape, b.dtype) for b in scatter],
        scratch_shapes=_comm_sems(ns) if ns else [],
        compiler_params=_cparams(("arbitrary",)),
    )(dproj, w_in_pad, x, g1, dx1, *scatter)
    res = list(res)
    return res[0], res[1], res[2:]


def _matmul_tn(a, b, name, cast_b=False):
    T, K = a.shape
    N = b.shape[1]
    bt = min(T, 512)
    bk = _tile_div(K, 1536)
    bn = _tile_div(N, 1536)

    def body(a_ref, b_ref, o_ref):
        @pl.when(pl.program_id(2) == 0)
        def _():
            o_ref[...] = jnp.zeros_like(o_ref)

        bv = b_ref[...]
        if cast_b:
            bv = bv.astype(BF16)
        o_ref[...] += _tn(a_ref[...], bv)

    return pl.pallas_call(
        body,
        name=name,
        grid=(K // bk, N // bn, T // bt),
        in_specs=[
            pl.BlockSpec((bt, bk), lambda k, n, t: (t, k)),
            pl.BlockSpec((bt, bn), lambda k, n, t: (t, n)),
        ],
        out_specs=pl.BlockSpec((bk, bn), lambda k, n, t: (k, n)),
        out_shape=jax.ShapeDtypeStruct((K, N), F32),
        compiler_params=_cparams(("arbitrary", "arbitrary", "arbitrary")),
    )(a, b)


def _local_step(x, mem, target, p, tm, tq, late=None):
    T, D = x.shape
    w_in = p["w_in"]
    w_qkv = w_in[:, :QKV_W]
    w_f_t = w_in[:, QKV_W:].T
    w_in_pad = jnp.pad(w_in, ((0, 0), (0, IN_PAD - w_in.shape[1])))
    b_f = p["b_forget"].reshape(N_FOX, 1)

    proj, h1, xf, c = _inproj_fwd(x, p["attn_norm_g"], w_qkv, w_f_t, b_f, tm)
    c_col = c.reshape(N_FOX, T, 1)
    c_row = c.reshape(N_FOX, 1, T)
    if late:
        fox_o, lse, gathered = _fox_fwd(proj, c_col, c_row, tq, gather=[late[n] for n in _LATE])
        p = dict(p, **{n: _gathered_full(n, gv) for n, gv in zip(_LATE, gathered)})
    else:
        fox_o, lse, _ = _fox_fwd(proj, c_col, c_row, tq)
    sb_o, sb_ltot = _sb_fwd(proj, tq)
    x1, mixed = _post_attn_fwd(fox_o, sb_o, p["fox_out_g"], p["sb_out_g"], p["w_out"], x, tm)
    mb, kv = _mem_kv_fwd(mem, p["mem_norm_g"], p["w_mkv"])
    x2, h2, qb, om = _xattn_fwd(x1, p["xattn_norm_g"], p["w_mq"], kv, p["w_mo"], tm)
    x3, h3, ug, uv, yg, yv, a = _ffn_fwd(
        x2, p["ffn_norm_g"], p["w_up"], p["conv_w"], p["conv_b"], p["w_down"], tm)
    dx3, loss_blk, d_final_g = _loss_head(x3, p["final_norm_g"], target, tm)

    g = {"final_norm_g": d_final_g}
    dx2, du_g, du_v, g["ffn_norm_g"], dc_g, dc_v = _ffn_bwd(
        dx3, x2, p["ffn_norm_g"], ug, uv, yg, yv, p["conv_w"], p["w_down"], p["w_up"], tm)
    g["w_down"] = _matmul_tn(a, dx3, "dw_down", cast_b=True)
    g["w_up"] = jnp.concatenate([_matmul_tn(h3, du_g, "dw_up_gate"), _matmul_tn(h3, du_v, "dw_up_val")], axis=1)
    dconv = jnp.concatenate([dc_g, dc_v], axis=1)
    g["conv_w"] = dconv[0:3]
    g["conv_b"] = dconv[3:4]
    dx1, dq_m, dkv, g["xattn_norm_g"] = _xattn_bwd(dx2, x1, p["xattn_norm_g"], qb, kv, p["w_mo"], p["w_mq"], tm)
    g["w_mo"] = _matmul_tn(om, dx2, "dw_mo", cast_b=True)
    g["w_mq"] = _matmul_tn(h2, dq_m, "dw_mq")
    g["w_mkv"], g["mem_norm_g"] = _mem_kv_bwd(mem, p["mem_norm_g"], mb, dkv, p["w_mkv"])
    d_fox, d_sb, g["fox_out_g"], g["sb_out_g"] = _post_attn_bwd(
        dx1, fox_o, sb_o, p["fox_out_g"], p["sb_out_g"], p["w_out"], tm)
    g["w_out"] = _matmul_tn(mixed, dx1, "dw_out", cast_b=True)
    dq_s, dk_s, dv_s = _sb_bwd(proj, sb_ltot, d_sb, tq)
    if late:
        dq_f, dk_f, dv_f, dck, dcq, parts = _fox_bwd(proj, c_col, c_row, lse, d_fox, fox_o, tq,
                                                      scatter=[_grad_blocks(n, g[n]) for n in _LATE])
        g["parts"] = dict(zip(_LATE, parts))
    else:
        dq_f, dk_f, dv_f, dck, dcq, _ = _fox_bwd(proj, c_col, c_row, lse, d_fox, fox_o, tq)
    dxf, db = _forget_bwd(dcq.reshape(N_FOX, T), dck.reshape(N_FOX, T), xf, min(T, 512))
    g["b_forget"] = db.reshape(1, N_FOX)
    dproj = jnp.concatenate([
        dq_f, dk_f.astype(BF16), dv_f.astype(BF16), dq_s, dk_s.astype(BF16), dv_s.astype(BF16),
        jnp.pad(dxf.T, ((0, 0), (0, IN_PAD - QKV_W - N_FOX))).astype(BF16)], axis=1)
    g["w_in"] = _matmul_tn(h1, dproj, "dw_in")[:, :w_in.shape[1]]
    if late:
        grad_x, g["attn_norm_g"], (g["parts"]["w_in"],) = _inproj_bwd(
            dproj, w_in_pad, x, p["attn_norm_g"], dx1, tm, scatter=[_grad_blocks("w_in", g["w_in"])])
    else:
        grad_x, g["attn_norm_g"], _ = _inproj_bwd(dproj, w_in_pad, x, p["attn_norm_g"], dx1, tm)
    return loss_blk, grad_x, g


def _mesh_pos():
    return lax.axis_index("x"), lax.axis_index("y"), lax.axis_index("c")


def _flip(pos, k):
    return tuple(1 - v if (k >> b) & 1 else v for v, b in zip(pos, (2, 1, 0)))


def _slot(pos):
    return 4 * pos[0] + 2 * pos[1] + pos[2]


_CHIPS = (4, 2, 6)


def _comm_sems(n):
    return [pltpu.SemaphoreType.DMA((7 * n,)), pltpu.SemaphoreType.DMA((7 * n,)), pltpu.SemaphoreType.DMA((n,))]


class _Gather:
    def __init__(self, ins, outs, send_sems, recv_sems, local_sems):
        self.ins, self.outs, self.n = ins, outs, len(ins)
        self.send_sems, self.recv_sems, self.local_sems = send_sems, recv_sems, local_sems
        self.me = _mesh_pos()
        self.sibling = _flip(self.me, 1)

    def _copy(self, a, kk, block, to, src=None):
        rows = self.outs[a].at[_slot(block)]
        return pltpu.make_async_remote_copy(
            src_ref=rows if src is None else src, dst_ref=rows,
            send_sem=self.send_sems.at[7 * a + kk], recv_sem=self.recv_sems.at[7 * a + kk],
            device_id=to, device_id_type=MESH)

    def _mine(self):
        return [pltpu.make_async_copy(self.ins[a], self.outs[a].at[_slot(self.me)], self.local_sems.at[a])
                for a in range(self.n)]

    def _first(self):
        out = []
        for a in range(self.n):
            out.append(self._copy(a, 0, self.me, self.sibling, src=self.ins[a]))
            out += [self._copy(a, 1 + j, self.me, _flip(self.me, k), src=self.ins[a]) for j, k in enumerate(_CHIPS)]
        return out

    def _passed(self):
        return [self._copy(a, 4 + j, _flip(self.me, k), self.sibling)
                for j, k in enumerate(_CHIPS) for a in range(self.n)]

    def start(self):
        for cp in self._mine() + self._first():
            cp.start()

    def forward(self):
        for j, k in enumerate(_CHIPS):
            for a in range(self.n):
                self._copy(a, 1 + j, _flip(self.me, k), self.me).wait_recv()
                self._copy(a, 4 + j, _flip(self.me, k), self.sibling).start()

    def finish(self):
        for a in range(self.n):
            self._copy(a, 0, self.sibling, self.me).wait_recv()
            for j, k in enumerate(_CHIPS):
                self._copy(a, 4 + j, _flip(self.sibling, k), self.me).wait_recv()
        for cp in self._first() + self._passed():
            cp.wait_send()
        for cp in self._mine():
            cp.wait()


class _Scatter:
    def __init__(self, ins, outs, send_sems, recv_sems, local_sems):
        self.ins, self.outs, self.n = ins, outs, len(ins)
        self.send_sems, self.recv_sems, self.local_sems = send_sems, recv_sems, local_sems
        self.me = _mesh_pos()

    def _copy(self, a, k, landed=False):
        peer = _flip(self.me, k)
        return pltpu.make_async_remote_copy(
            src_ref=self.ins[a].at[_slot(peer)], dst_ref=self.outs[a].at[_slot(peer if landed else self.me)],
            send_sem=self.send_sems.at[7 * a + k - 1], recv_sem=self.recv_sems.at[7 * a + k - 1],
            device_id=peer, device_id_type=MESH)

    def _mine(self):
        s = _slot(self.me)
        return [pltpu.make_async_copy(self.ins[a].at[s], self.outs[a].at[s], self.local_sems.at[a])
                for a in range(self.n)]

    def start(self):
        for cp in self._mine() + [self._copy(a, k) for k in range(1, 8) for a in range(self.n)]:
            cp.start()

    def finish(self):
        for k in range(1, 8):
            for a in range(self.n):
                self._copy(a, k, landed=True).wait_recv()
        for k in range(1, 8):
            for a in range(self.n):
                self._copy(a, k).wait_send()
        for cp in self._mine():
            cp.wait()


_ANY = pl.BlockSpec(memory_space=pl.ANY)


def _gathered_shapes(shards):
    return [jax.ShapeDtypeStruct((N_DEV,) + s.shape, s.dtype) for s in shards]


def _all_gather(shards, name):
    n = len(shards)

    def body(*refs):
        g = _Gather(refs[:n], refs[n:2 * n], *refs[2 * n:])
        g.start()
        g.forward()
        g.finish()

    return pl.pallas_call(
        body, name=name, in_specs=[_ANY] * n, out_specs=[_ANY] * n,
        out_shape=_gathered_shapes(shards), scratch_shapes=_comm_sems(n),
    )(*shards)


def _adamw_math(w, g, m, v):
    m2 = ADAM_B1 * m + (1.0 - ADAM_B1) * g
    v2 = ADAM_B2 * v + (1.0 - ADAM_B2) * (g * g)
    m_hat = m2 / (1.0 - ADAM_B1 ** ADAM_STEP)
    v_hat = v2 / (1.0 - ADAM_B2 ** ADAM_STEP)
    delta = -ADAM_LR * (m_hat / (jnp.sqrt(v_hat) + ADAM_EPS) + ADAM_WD * w)
    return delta, m2, v2


def _adamw(w, parts, m, v, name):
    R, C = w.shape
    br = 128 if R % 128 == 0 else R

    def body(w_ref, p_ref, m_ref, v_ref, g_ref, d_ref, nm_ref, nv_ref):
        g = p_ref[0].astype(F32)
        for s in range(1, N_DEV):
            g = g + p_ref[s].astype(F32)
        g_ref[...] = g
        d_ref[...], nm_ref[...], nv_ref[...] = _adamw_math(w_ref[...], g, m_ref[...], v_ref[...])

    spec = pl.BlockSpec((br, C), lambda i: (i, 0))
    return pl.pallas_call(
        body,
        name=name,
        grid=(R // br,),
        in_specs=[spec, pl.BlockSpec((N_DEV, br, C), lambda i: (0, i, 0)), spec, spec],
        out_specs=[spec] * 4,
        out_shape=[jax.ShapeDtypeStruct((R, C), F32)] * 4,
        compiler_params=_cparams(("arbitrary",)),
    )(w, parts, m, v)


_SHARDED = ("w_in", "w_out", "w_mq", "w_mkv", "w_mo", "w_up", "conv_w", "w_down")
_LATE = _SHARDED[1:]
_COL_SHARDED = ("w_in", "w_mkv", "w_up", "conv_w")
_REPLICATED = ("attn_norm_g", "b_forget", "fox_out_g", "sb_out_g", "xattn_norm_g", "mem_norm_g",
               "ffn_norm_g", "conv_b", "final_norm_g")
_WEIGHTS = ("attn_norm_g", "w_in", "b_forget", "fox_out_g", "sb_out_g", "w_out", "xattn_norm_g", "mem_norm_g",
            "w_mq", "w_mkv", "w_mo", "ffn_norm_g", "w_up", "conv_w", "conv_b", "w_down", "final_norm_g")


def _pack_rows(n):
    return -(-n // 128)


def _pack(vals, rows_total):
    parts = []
    for v in vals:
        flat = v.reshape(-1)
        parts.append(jnp.pad(flat, (0, _pack_rows(flat.shape[0]) * 128 - flat.shape[0])))
    flat = jnp.concatenate(parts)
    return jnp.pad(flat, (0, rows_total * 128 - flat.shape[0])).reshape(rows_total, 128)


def _unpack(packed, shapes):
    out = []
    r = 0
    for shp in shapes:
        n = 1
        for d in shp:
            n *= d
        out.append(packed[r:r + _pack_rows(n)].reshape(-1)[:n].reshape(shp))
        r += _pack_rows(n)
    return out


def _gathered_full(name, gathered):
    if name in _COL_SHARDED:
        return jnp.transpose(gathered, (1, 0, 2)).reshape(gathered.shape[1], -1)
    return gathered.reshape(-1, gathered.shape[2])


def _to_blocks(name, full):
    if name in _COL_SHARDED:
        r = full.shape[0]
        return jnp.transpose(full.reshape(r, N_DEV, -1), (1, 0, 2))
    return full.reshape(N_DEV, -1, full.shape[1])


def _grad_blocks(name, full):
    blocks = _to_blocks(name, full)
    return blocks if name == "conv_w" else blocks.astype(BF16)


def _step(args, tm, tq):
    w = {n: args[n] for n in _WEIGHTS}
    mom = {n: args["m_" + n] for n in _WEIGHTS}
    var = {n: args["v_" + n] for n in _WEIGHTS}
    x = args["x"][0]
    mem = args["mem"][0]
    target = args["loss_target"][0]

    def flat2(a):
        return a.reshape(a.shape[-2], a.shape[-1]) if a.ndim == 3 else a.reshape(1, -1)

    shards = {n: flat2(w[n]) if n == "conv_w" else flat2(w[n]).astype(BF16) for n in _SHARDED}
    (w_in_all,) = _all_gather([shards["w_in"]], "gather_w_in")
    p = {"w_in": _gathered_full("w_in", w_in_all)}
    for n in _REPLICATED:
        p[n] = flat2(w[n])

    loss_blk, grad_x, g = _local_step(x, mem, target, p, tm, tq, late={n: shards[n] for n in _LATE})

    parts = g["parts"]
    out = {}
    for n in _SHARDED:
        res = _adamw(flat2(w[n]), parts[n], flat2(mom[n]), flat2(var[n]), "adamw_" + n)
        out[n] = [r.reshape(w[n].shape) for r in res]

    shapes = [w[n].shape for n in _REPLICATED]
    rows = sum(_pack_rows(flat2(w[n]).shape[1]) for n in _REPLICATED) + 1
    rows = -(-rows // 8) * 8
    g_pack = _pack([g[n] for n in _REPLICATED] + [loss_blk[0:1, :]], rows)
    (g_all,) = _all_gather([g_pack], "gather_small")
    res = _adamw(_pack([w[n] for n in _REPLICATED], rows), g_all,
                 _pack([mom[n] for n in _REPLICATED], rows), _pack([var[n] for n in _REPLICATED], rows),
                 "adamw_small")
    n_rows_params = sum(_pack_rows(flat2(w[n]).shape[1]) for n in _REPLICATED)
    loss = res[0][n_rows_params, 0]
    unpacked = [_unpack(r, shapes) for r in res]
    for k, n in enumerate(_REPLICATED):
        out[n] = [unpacked[q][k] for q in range(4)]

    grads = [out[n][0] for n in _WEIGHTS]
    deltas = [out[n][1] for n in _WEIGHTS]
    new_m = [out[n][2] for n in _WEIGHTS]
    new_v = [out[n][3] for n in _WEIGHTS]
    return (loss, grad_x[None], *grads, *deltas, *new_m, *new_v)


def kernel(x, mem, attn_norm_g, w_in, b_forget, fox_out_g, sb_out_g, w_out, xattn_norm_g, mem_norm_g, w_mq, w_mkv, w_mo, ffn_norm_g, w_up, conv_w, conv_b, w_down, final_norm_g, loss_target, m_attn_norm_g, m_w_in, m_b_forget, m_fox_out_g, m_sb_out_g, m_w_out, m_xattn_norm_g, m_mem_norm_g, m_w_mq, m_w_mkv, m_w_mo, m_ffn_norm_g, m_w_up, m_conv_w, m_conv_b, m_w_down, m_final_norm_g, v_attn_norm_g, v_w_in, v_b_forget, v_fox_out_g, v_sb_out_g, v_w_out, v_xattn_norm_g, v_mem_norm_g, v_w_mq, v_w_mkv, v_w_mo, v_ffn_norm_g, v_w_up, v_conv_w, v_conv_b, v_w_down, v_final_norm_g):
    args = dict(locals())
    T = x.shape[1]
    return _step(args, tm=min(T, 512), tq=min(T, 256))
```

```python
import functools

import jax
import jax.numpy as jnp
from jax import lax
from jax.experimental import pallas as pl
from jax.experimental.pallas import tpu as pltpu

F32 = jnp.float32
BF16 = jnp.bfloat16
EPS = 1e-6
NEG = -1e30
LOG2E = 1.4426950408889634

HEAD_DIM = 64
N_FOX = 8
FOX_W = 512
QKV_W = 3072
IN_PAD = 3200
N_MEM_HEADS = 4
MEM_HD = 256
D_FF = 2816
FF_CHUNK = 256
N_DEV = 8

ADAM_LR = 0.001
ADAM_B1 = 0.9
ADAM_B2 = 0.999
ADAM_EPS = 1e-08
ADAM_WD = 0.01
ADAM_STEP = 10

SB_SUM_TERMS = 1

VMEM_LIMIT = 56 * 1024 * 1024
MESH = pl.DeviceIdType.MESH


def _cparams(sem=None):
    return pltpu.CompilerParams(dimension_semantics=sem, vmem_limit_bytes=VMEM_LIMIT)


def _nt(a, b):
    return lax.dot_general(a, b, (((1,), (1,)), ((), ())), preferred_element_type=F32)


def _tn(a, b):
    return lax.dot_general(a, b, (((0,), (0,)), ((), ())), preferred_element_type=F32)


def _nn(a, b):
    return jnp.dot(a, b, preferred_element_type=F32)


def _split_dot(a, m01, terms):
    out = None
    r = a
    for t in range(terms):
        p = r.astype(BF16)
        d = _nn(p, m01)
        out = d if out is None else out + d
        if t + 1 < terms:
            r = r - p.astype(F32)
    return out


def _rstd(xv):
    return lax.rsqrt(jnp.mean(xv * xv, axis=-1, keepdims=True) + EPS)


def _norm_bwd(xv, g, dh):
    r = _rstd(xv)
    xhat = xv * r
    dxhat = dh * g
    dx = r * (dxhat - xhat * jnp.mean(dxhat * xhat, axis=-1, keepdims=True))
    dg = jnp.sum(dh * xhat, axis=0, keepdims=True)
    return dx, dg


def _tile_div(n, cap):
    best = None
    for d in range(128, min(n, cap) + 1, 128):
        if n % d == 0:
            best = d
    assert best is not None, n
    return best


def _inproj_fwd(x, g1, w_qkv, w_f_t, b_f, tm):
    T, D = x.shape
    N = w_qkv.shape[1]
    H = w_f_t.shape[0]

    def body(x_ref, g_ref, w_ref, wf_ref, b_ref, proj_ref, h_ref, xf_ref, c_ref, carry_ref):
        i = pl.program_id(0)

        @pl.when(i == 0)
        def _():
            carry_ref[...] = jnp.zeros_like(carry_ref)

        xv = x_ref[...]
        h = (xv * _rstd(xv) * g_ref[...]).astype(BF16)
        h_ref[...] = h
        for n0 in range(0, N, 512):
            proj_ref[:, n0:n0 + 512] = _nn(h, w_ref[:, n0:n0 + 512]).astype(BF16)
        xf = _nt(wf_ref[...], h) + b_ref[...]
        xf_ref[...] = xf
        logf = jnp.minimum(xf, 0.0) - jnp.log1p(jnp.exp(-jnp.abs(xf)))
        row = lax.broadcasted_iota(jnp.int32, (tm, tm), 0)
        col = lax.broadcasted_iota(jnp.int32, (tm, tm), 1)
        upper = jnp.where(row <= col, 1.0, 0.0).astype(BF16)
        c = _split_dot(logf, upper, 3) + carry_ref[...]
        c_ref[...] = c
        carry_ref[...] = c[:, tm - 1:tm]

    return pl.pallas_call(
        body,
        name="inproj_fwd",
        grid=(T // tm,),
        in_specs=[
            pl.BlockSpec((tm, D), lambda i: (i, 0)),
            pl.BlockSpec((1, D), lambda i: (0, 0)),
            pl.BlockSpec((D, N), lambda i: (0, 0)),
            pl.BlockSpec((H, D), lambda i: (0, 0)),
            pl.BlockSpec((H, 1), lambda i: (0, 0)),
        ],
        out_specs=[
            pl.BlockSpec((tm, N), lambda i: (i, 0)),
            pl.BlockSpec((tm, D), lambda i: (i, 0)),
            pl.BlockSpec((H, tm), lambda i: (0, i)),
            pl.BlockSpec((H, tm), lambda i: (0, i)),
        ],
        out_shape=[
            jax.ShapeDtypeStruct((T, N), BF16),
            jax.ShapeDtypeStruct((T, D), BF16),
            jax.ShapeDtypeStruct((H, T), F32),
            jax.ShapeDtypeStruct((H, T), F32),
        ],
        scratch_shapes=[pltpu.VMEM((H, 1), F32)],
        compiler_params=_cparams(("arbitrary",)),
    )(x, g1, w_qkv, w_f_t, b_f)


def _head_q(q, hh, lane):
    hmask = (lane >= HEAD_DIM * hh) & (lane < HEAD_DIM * (hh + 1))
    qh = jnp.where(hmask, q.astype(F32), 0.0) * (HEAD_DIM ** -0.5)
    return qh.astype(BF16), hmask


def _pipeline3(n, stage_a, stage_b, stage_c, diag_last):
    stage_a(0, 0)
    if diag_last:
        @pl.when(n == 1)
        def _():
            stage_b(0, 0, True)

        @pl.when(n >= 2)
        def _():
            stage_b(0, 0, False)
    else:
        stage_b(0, 0, True)

    @pl.when(n >= 2)
    def _():
        stage_a(1, 1)

    def pair(m, carry):
        t = 2 + 2 * m
        stage_c(t - 2)
        stage_b(t - 1, 1, False)
        stage_a(t, 0)
        stage_c(t - 1)
        stage_b(t, 0, False)
        stage_a(t + 1, 1)
        return carry

    lax.fori_loop(0, (n - 2) // 2, pair, 0)
    odd = n % 2 == 1

    @pl.when((n >= 3) & odd)
    def _():
        stage_c(n - 3)
        stage_b(n - 2, 1, False)
        stage_a(n - 1, 0)

    @pl.when((n >= 2) & odd)
    def _():
        stage_c(n - 2)
        stage_b(n - 1, 0, diag_last)

    @pl.when((n >= 2) & jnp.logical_not(odd))
    def _():
        stage_c(n - 2)
        stage_b(n - 1, 1, diag_last)

    stage_c(n - 1)


def _lanes2(x):
    return jnp.concatenate([x, x], axis=1)


def _lanes_to_rows(vec, eye):
    return jnp.sum(jnp.where(eye, jnp.broadcast_to(vec, eye.shape), 0.0), axis=1, keepdims=True)


def _rows_to_lanes(rep, eye):
    return jnp.sum(jnp.where(eye, _lanes2(rep), 0.0), axis=0, keepdims=True)


def _fox_fwd(proj, c_col, c_row, tq, gather=()):
    T = proj.shape[0]
    assert tq == 256
    ng = len(gather)
    nq = T // tq

    def body(*refs):
        q_ref, k_ref, v_ref, cq_ref, ck_ref = refs[:5]
        o_ref, lse_ref = refs[5 + ng:7 + ng]
        qh_s, cq_s, z_s, p_s, al_s, m_s, l_s, acc_s = refs[7 + 2 * ng:15 + 2 * ng]
        i = pl.program_id(1)
        if ng:
            pair = pl.program_id(0)
            exchange = _Gather(refs[5:5 + ng], refs[7 + ng:7 + 2 * ng], *refs[15 + 2 * ng:])

            @pl.when((pair == 0) & (i == 0))
            def _():
                exchange.start()

            @pl.when((pair == 1) & (i == 0))
            def _():
                exchange.forward()

        lane = lax.broadcasted_iota(jnp.int32, (1, 128), 1)
        row = lax.broadcasted_iota(jnp.int32, (tq, tq), 0)
        col = lax.broadcasted_iota(jnp.int32, (tq, tq), 1)
        ones = jnp.ones((tq, 128), BF16)
        q = q_ref[...]
        for hh in range(2):
            qh_s[hh] = _head_q(q, hh, lane)[0]
            cq_s[hh] = jnp.broadcast_to(_lanes_to_rows(cq_ref[hh], row == col), (tq, tq))
        m_s[...] = jnp.full(m_s.shape, NEG, F32)
        l_s[...] = jnp.zeros_like(l_s)
        acc_s[...] = jnp.zeros_like(acc_s)

        def rows(t):
            return pl.ds(pl.multiple_of((i - t) * tq, tq), tq)

        def stage_a(t, slot):
            k = k_ref[rows(t), :]
            for hh in range(2):
                z_s[slot, hh] = _nt(qh_s[hh], k)

        def stage_b(t, slot, diag):
            for hh in range(2):
                s = z_s[slot, hh] + cq_s[hh] - ck_ref[hh, :, rows(t)]
                if diag:
                    s = jnp.where(col <= row, s, NEG)
                m = m_s[hh]
                half = jnp.maximum(s[:, :128], s[:, 128:])
                m_new = jnp.maximum(m, jnp.max(half, axis=1, keepdims=True))
                alpha = jnp.exp(m - m_new)
                p = jnp.exp(s - _lanes2(m_new)).astype(BF16)
                l_s[hh] = alpha * l_s[hh] + _nn(p, ones)
                m_s[hh] = m_new
                al_s[hh] = alpha
                p_s[hh] = p

        def stage_c(t):
            v = v_ref[rows(t), :]
            for hh in range(2):
                acc_s[hh] = al_s[hh] * acc_s[hh] + _nn(p_s[hh], v)

        _pipeline3(i + 1, stage_a, stage_b, stage_c, False)
        l0, l1 = l_s[0], l_s[1]
        o_ref[...] = jnp.where(lane < HEAD_DIM, acc_s[0] / l0, acc_s[1] / l1)
        lse_ref[0] = _rows_to_lanes(m_s[0] + jnp.log(l0), row == col)
        lse_ref[1] = _rows_to_lanes(m_s[1] + jnp.log(l1), row == col)
        if ng:
            @pl.when((pair == 3) & (i == nq - 1))
            def _():
                exchange.finish()

    res = pl.pallas_call(
        body,
        name="fox_fwd",
        grid=(4, nq),
        in_specs=[
            pl.BlockSpec((tq, 128), lambda p, i: (i, p)),
            pl.BlockSpec((T, 128), lambda p, i: (0, 4 + p)),
            pl.BlockSpec((T, 128), lambda p, i: (0, 8 + p)),
            pl.BlockSpec((2, 1, tq), lambda p, i: (p, 0, i)),
            pl.BlockSpec((2, 1, T), lambda p, i: (p, 0, 0)),
        ] + [_ANY] * ng,
        out_specs=[
            pl.BlockSpec((tq, 128), lambda p, i: (i, p)),
            pl.BlockSpec((2, 1, tq), lambda p, i: (p, 0, i)),
        ] + [_ANY] * ng,
        out_shape=[
            jax.ShapeDtypeStruct((T, FOX_W), F32),
            jax.ShapeDtypeStruct((N_FOX, 1, T), F32),
        ] + _gathered_shapes(gather),
        scratch_shapes=[
            pltpu.VMEM((2, tq, 128), BF16),
            pltpu.VMEM((2, tq, tq), F32),
            pltpu.VMEM((2, 2, tq, tq), F32),
            pltpu.VMEM((2, tq, tq), BF16),
            pltpu.VMEM((2, tq, 128), F32),
            pltpu.VMEM((2, tq, 128), F32),
            pltpu.VMEM((2, tq, 128), F32),
            pltpu.VMEM((2, tq, 128), F32),
        ] + (_comm_sems(ng) if ng else []),
        compiler_params=_cparams(("arbitrary", "arbitrary")),
    )(proj, proj, proj, c_col, c_row, *gather)
    res = list(res)
    return res[0], res[1], res[2:]


def _sb_logs(zn, strict):
    e = jnp.exp2(jnp.abs(zn) * (-LOG2E))
    L = jnp.minimum(zn, 0.0) - jnp.log(1.0 + e)
    G = L - zn
    if strict is not None:
        L = jnp.where(strict, L, 0.0)
    return L, G


def _sb_fwd(proj, tq):
    T = proj.shape[0]

    def body(q_ref, k_ref, v_ref, o_ref, ltot_ref, qh_s, z_s, g_s, tot_s, run_s, acc_s):
        i = pl.program_id(1)
        lane = lax.broadcasted_iota(jnp.int32, (1, 128), 1)
        row = lax.broadcasted_iota(jnp.int32, (tq, tq), 0)
        col = lax.broadcasted_iota(jnp.int32, (tq, tq), 1)
        strict = col < row
        later = jnp.where(row > col, 1.0, 0.0).astype(BF16)
        q = q_ref[...]
        for hh in range(2):
            qh_s[hh] = -_head_q(q, hh, lane)[0]
        run_s[...] = jnp.zeros_like(run_s)
        acc_s[...] = jnp.zeros_like(acc_s)

        def rows(t):
            return pl.ds(pl.multiple_of((i - t) * tq, tq), tq)

        def stage_a(t, slot):
            k = k_ref[rows(t), :]
            for hh in range(2):
                z_s[slot, hh] = _nt(qh_s[hh], k)

        def stage_b(t, slot, diag):
            for hh in range(2):
                L, g = _sb_logs(z_s[slot, hh], strict if diag else None)
                if diag:
                    g = jnp.where(strict, g, NEG)
                after = _split_dot(L, later, SB_SUM_TERMS)
                g_s[hh] = g + after
                first = L[:, 0:1]
                if SB_SUM_TERMS == 1:
                    first = first.astype(BF16).astype(F32)
                tot_s[hh] = jnp.broadcast_to(after[:, 0:1] + first, (tq, 128))

        def stage_c(t):
            v = v_ref[rows(t), :]
            for hh in range(2):
                run = run_s[hh]
                a = jnp.exp(g_s[hh] + _lanes2(run))
                acc_s[hh] += _nn(a.astype(BF16), v)
                run_s[hh] = run + tot_s[hh]

        _pipeline3(i + 1, stage_a, stage_b, stage_c, False)
        ltot_ref[0] = _rows_to_lanes(run_s[0], row == col)
        ltot_ref[1] = _rows_to_lanes(run_s[1], row == col)
        o_ref[...] = jnp.where(lane < HEAD_DIM, acc_s[0], acc_s[1])

    return pl.pallas_call(
        body,
        name="sb_fwd",
        grid=(4, T // tq),
        in_specs=[
            pl.BlockSpec((tq, 128), lambda p, i: (i, 12 + p)),
            pl.BlockSpec((T, 128), lambda p, i: (0, 16 + p)),
            pl.BlockSpec((T, 128), lambda p, i: (0, 20 + p)),
        ],
        out_specs=[
            pl.BlockSpec((tq, 128), lambda p, i: (i, p)),
            pl.BlockSpec((2, 1, tq), lambda p, i: (p, 0, i)),
        ],
        out_shape=[
            jax.ShapeDtypeStruct((T, FOX_W), F32),
            jax.ShapeDtypeStruct((N_FOX, 1, T), F32),
        ],
        scratch_shapes=[
            pltpu.VMEM((2, tq, 128), BF16),
            pltpu.VMEM((2, 2, tq, tq), F32),
            pltpu.VMEM((2, tq, tq), F32),
            pltpu.VMEM((2, tq, 128), F32),
            pltpu.VMEM((2, tq, 128), F32),
            pltpu.VMEM((2, tq, 128), F32),
        ],
        compiler_params=_cparams(("arbitrary", "arbitrary")),
    )(proj, proj, proj)


def _post_attn_fwd(fox_o, sb_o, gf, gs, w_out, x, tm):
    T, D = x.shape

    def body(f_ref, s_ref, gf_ref, gs_ref, w_ref, x_ref, x1_ref, mix_ref):
        f = f_ref[...]
        s = s_ref[...]
        mix_ref[:, :FOX_W] = (f * _rstd(f) * gf_ref[...]).astype(BF16)
        mix_ref[:, FOX_W:] = (s * _rstd(s) * gs_ref[...]).astype(BF16)
        x1_ref[...] = x_ref[...] + _nn(mix_ref[...], w_ref[...])

    return pl.pallas_call(
        body,
        name="post_attn_fwd",
        grid=(T // tm,),
        in_specs=[
            pl.BlockSpec((tm, FOX_W), lambda i: (i, 0)),
            pl.BlockSpec((tm, FOX_W), lambda i: (i, 0)),
            pl.BlockSpec((1, FOX_W), lambda i: (0, 0)),
            pl.BlockSpec((1, FOX_W), lambda i: (0, 0)),
            pl.BlockSpec((D, D), lambda i: (0, 0)),
            pl.BlockSpec((tm, D), lambda i: (i, 0)),
        ],
        out_specs=[
            pl.BlockSpec((tm, D), lambda i: (i, 0)),
            pl.BlockSpec((tm, D), lambda i: (i, 0)),
        ],
        out_shape=[jax.ShapeDtypeStruct((T, D), F32), jax.ShapeDtypeStruct((T, D), BF16)],
        compiler_params=_cparams(("arbitrary",)),
    )(fox_o, sb_o, gf, gs, w_out, x)


def _mem_kv_fwd(mem, gm, w_mkv):
    M, D = mem.shape
    N = w_mkv.shape[1]

    def body(mem_ref, g_ref, w_ref, m_ref, kv_ref):
        mv = mem_ref[...]
        m = (mv * _rstd(mv) * g_ref[...]).astype(BF16)
        m_ref[...] = m
        for n0 in range(0, N, 512):
            kv_ref[:, n0:n0 + 512] = _nn(m, w_ref[:, n0:n0 + 512]).astype(BF16)

    return pl.pallas_call(
        body,
        name="mem_kv_fwd",
        out_shape=[jax.ShapeDtypeStruct((M, D), BF16), jax.ShapeDtypeStruct((M, N), BF16)],
        compiler_params=_cparams(),
    )(mem, gm, w_mkv)


def _xattn_probs(qb, kv, h):
    k = kv[:, h * MEM_HD:(h + 1) * MEM_HD]
    s = _nt(qb[:, h * MEM_HD:(h + 1) * MEM_HD], k) * (MEM_HD ** -0.5)
    s = s - jnp.max(s, axis=1, keepdims=True)
    p = jnp.exp(s)
    return p / jnp.sum(p, axis=1, keepdims=True)


def _xattn_fwd(x1, g2, w_mq, kv, w_mo, tm):
    T, D = x1.shape
    M = kv.shape[0]

    def body(x_ref, g_ref, wq_ref, kv_ref, wo_ref, x2_ref, h_ref, q_ref, om_ref):
        xv = x_ref[...]
        h = (xv * _rstd(xv) * g_ref[...]).astype(BF16)
        h_ref[...] = h
        q_ref[...] = _nn(h, wq_ref[...]).astype(BF16)
        qb = q_ref[...]
        kvv = kv_ref[...]
        for hd in range(N_MEM_HEADS):
            p = _xattn_probs(qb, kvv, hd)
            v = kvv[:, D + hd * MEM_HD:D + (hd + 1) * MEM_HD]
            om_ref[:, hd * MEM_HD:(hd + 1) * MEM_HD] = _nn(p.astype(BF16), v).astype(BF16)
        x2_ref[...] = xv + _nn(om_ref[...], wo_ref[...])

    return pl.pallas_call(
        body,
        name="xattn_fwd",
        grid=(T // tm,),
        in_specs=[
            pl.BlockSpec((tm, D), lambda i: (i, 0)),
            pl.BlockSpec((1, D), lambda i: (0, 0)),
            pl.BlockSpec((D, D), lambda i: (0, 0)),
            pl.BlockSpec((M, 2 * D), lambda i: (0, 0)),
            pl.BlockSpec((D, D), lambda i: (0, 0)),
        ],
        out_specs=[pl.BlockSpec((tm, D), lambda i: (i, 0))] * 4,
        out_shape=[jax.ShapeDtypeStruct((T, D), F32)] + [jax.ShapeDtypeStruct((T, D), BF16)] * 3,
        compiler_params=_cparams(("arbitrary",)),
    )(x1, g2, w_mq, kv, w_mo)


def _conv_taps(ext_ref, tm, back):
    if back:
        return ext_ref[pl.ds(6, tm), :], ext_ref[pl.ds(7, tm), :], ext_ref[pl.ds(8, tm), :]
    return ext_ref[pl.ds(0, tm), :], ext_ref[pl.ds(1, tm), :], ext_ref[pl.ds(2, tm), :]


def _ffn_fwd(x2, g3, w_up, conv_w, conv_b, w_down, tm):
    T, D = x2.shape
    fc = FF_CHUNK
    nj = D_FF // fc

    def body(x_ref, g_ref, wg_ref, wv_ref, cwg_ref, cwv_ref, cbg_ref, cbv_ref, wd_ref,
             x3_ref, h_ref, ug_ref, uv_ref, yg_ref, yv_ref, a_ref, acc_ref, carry_ref, ext_ref):
        i = pl.program_id(0)
        j = pl.program_id(1)

        @pl.when(j == 0)
        def _():
            xv = x_ref[...]
            h_ref[...] = (xv * _rstd(xv) * g_ref[...]).astype(BF16)
            acc_ref[...] = xv

        @pl.when(i == 0)
        def _():
            carry_ref[j] = jnp.zeros((2, 8, fc), F32)

        h = h_ref[...]
        halves = []
        for part, (w_ref, cw_ref, cb_ref, u_ref, y_ref) in enumerate(
                ((wg_ref, cwg_ref, cbg_ref, ug_ref, yg_ref), (wv_ref, cwv_ref, cbv_ref, uv_ref, yv_ref))):
            u = _nn(h, w_ref[...])
            u_ref[...] = u.astype(BF16)
            ext = ext_ref.at[part]
            ext[pl.ds(0, 8), :] = carry_ref[j, part]
            ext[pl.ds(8, tm), :] = u
            carry_ref[j, part] = u[tm - 8:, :]
            u2, u1, u0 = _conv_taps(ext, tm, True)
            cw = cw_ref[...]
            y = cb_ref[...] + cw[0:1] * u2 + cw[1:2] * u1 + cw[2:3] * u0
            y_ref[...] = y.astype(BF16)
            halves.append(y)
        gate, val = halves
        a = (gate * jax.nn.sigmoid(gate) * val).astype(BF16)
        a_ref[...] = a
        acc_ref[...] += _nn(a, wd_ref[...])

        @pl.when(j == nj - 1)
        def _():
            x3_ref[...] = acc_ref[...]

    return pl.pallas_call(
        body,
        name="ffn_fwd",
        grid=(T // tm, nj),
        in_specs=[
            pl.BlockSpec((tm, D), lambda i, j: (i, 0)),
            pl.BlockSpec((1, D), lambda i, j: (0, 0)),
            pl.BlockSpec((D, fc), lambda i, j: (0, j)),
            pl.BlockSpec((D, fc), lambda i, j: (0, nj + j)),
            pl.BlockSpec((3, fc), lambda i, j: (0, j)),
            pl.BlockSpec((3, fc), lambda i, j: (0, nj + j)),
            pl.BlockSpec((1, fc), lambda i, j: (0, j)),
            pl.BlockSpec((1, fc), lambda i, j: (0, nj + j)),
            pl.BlockSpec((fc, D), lambda i, j: (j, 0)),
        ],
        out_specs=[
            pl.BlockSpec((tm, D), lambda i, j: (i, 0)),
            pl.BlockSpec((tm, D), lambda i, j: (i, 0)),
        ] + [pl.BlockSpec((tm, fc), lambda i, j: (i, j))] * 5,
        out_shape=[
            jax.ShapeDtypeStruct((T, D), F32),
            jax.ShapeDtypeStruct((T, D), BF16),
        ] + [jax.ShapeDtypeStruct((T, D_FF), BF16)] * 5,
        scratch_shapes=[
            pltpu.VMEM((tm, D), F32),
            pltpu.VMEM((nj, 2, 8, fc), F32),
            pltpu.VMEM((2, tm + 8, fc), F32),
        ],
        compiler_params=_cparams(("arbitrary", "arbitrary")),
    )(x2, g3, w_up, w_up, conv_w, conv_w, conv_b, conv_b, w_down)


def _loss_head(x3, gfin, target, tm):
    T, D = x3.shape

    def body(x_ref, g_ref, t_ref, dx_ref, loss_ref, dg_ref):
        i = pl.program_id(0)

        @pl.when(i == 0)
        def _():
            loss_ref[...] = jnp.zeros_like(loss_ref)
            dg_ref[...] = jnp.zeros_like(dg_ref)

        xv = x_ref[...]
        g = g_ref[...]
        r = _rstd(xv)
        xhat = xv * r
        err = xhat * g - t_ref[...]
        part = jnp.sum(jnp.sum(err * err, axis=1, keepdims=True), axis=0, keepdims=True) * (0.5 / D)
        loss_ref[...] += jnp.broadcast_to(part, loss_ref.shape)
        dy = err * (1.0 / D)
        dg_ref[...] += jnp.sum(dy * xhat, axis=0, keepdims=True)
        dxhat = dy * g
        dx_ref[...] = r * (dxhat - xhat * jnp.mean(dxhat * xhat, axis=-1, keepdims=True))

    return pl.pallas_call(
        body,
        name="loss_head",
        grid=(T // tm,),
        in_specs=[
            pl.BlockSpec((tm, D), lambda i: (i, 0)),
            pl.BlockSpec((1, D), lambda i: (0, 0)),
            pl.BlockSpec((tm, D), lambda i: (i, 0)),
        ],
        out_specs=[
            pl.BlockSpec((tm, D), lambda i: (i, 0)),
            pl.BlockSpec((8, 128), lambda i: (0, 0)),
            pl.BlockSpec((1, D), lambda i: (0, 0)),
        ],
        out_shape=[
            jax.ShapeDtypeStruct((T, D), F32),
            jax.ShapeDtypeStruct((8, 128), F32),
            jax.ShapeDtypeStruct((1, D), F32),
        ],
        compiler_params=_cparams(("arbitrary",)),
    )(x3, gfin, target)


def _ffn_bwd(dx3, x2, g3, ug, uv, yg, yv, conv_w, w_down, w_up, tm):
    T, D = x2.shape
    fc = FF_CHUNK
    nj = D_FF // fc
    nt = T // tm

    def rev(i):
        return nt - 1 - i

    def body(dx3_ref, x_ref, g_ref, ug_ref, uv_ref, yg_ref, yv_ref, cwg_ref, cwv_ref,
             wd_ref, wug_ref, wuv_ref,
             dx2_ref, dug_ref, duv_ref, dg_ref, dcg_ref, dcv_ref,
             acc_ref, carry_ref, ext_ref):
        i = pl.program_id(0)
        j = pl.program_id(1)
        cols = pl.ds(pl.multiple_of(j * fc, fc), fc)

        @pl.when(j == 0)
        def _():
            acc_ref[...] = jnp.zeros_like(acc_ref)

        @pl.when((i == 0) & (j == 0))
        def _():
            dg_ref[...] = jnp.zeros_like(dg_ref)
            dcg_ref[...] = jnp.zeros_like(dcg_ref)
            dcv_ref[...] = jnp.zeros_like(dcv_ref)

        @pl.when(i == 0)
        def _():
            carry_ref[j] = jnp.zeros((2, 8, fc), F32)

        da = _nt(dx3_ref[...].astype(BF16), wd_ref[...])
        gate = yg_ref[...].astype(F32)
        val = yv_ref[...].astype(F32)
        sig = jax.nn.sigmoid(gate)
        silu = gate * sig
        dys = (da * val * (sig * (1.0 + gate * (1.0 - sig))), da * silu)
        for part, (dy, u_ref, cw_ref, du_ref, wu_ref, dc_ref) in enumerate(
                ((dys[0], ug_ref, cwg_ref, dug_ref, wug_ref, dcg_ref),
                 (dys[1], uv_ref, cwv_ref, duv_ref, wuv_ref, dcv_ref))):
            ext = ext_ref.at[part]
            ext[pl.ds(0, tm), :] = dy
            ext[pl.ds(tm, 8), :] = carry_ref[j, part]
            carry_ref[j, part] = dy[:8, :]
            d0, d1, d2 = _conv_taps(ext, tm, False)
            u = u_ref[...].astype(F32)
            upd = jnp.concatenate([
                jnp.sum(u * d2, axis=0, keepdims=True),
                jnp.sum(u * d1, axis=0, keepdims=True),
                jnp.sum(u * d0, axis=0, keepdims=True),
                jnp.sum(d0, axis=0, keepdims=True),
                jnp.zeros((4, fc), F32)], axis=0)
            dc_ref[:, cols] += upd
            cw = cw_ref[...]
            du = (cw[2:3] * d0 + cw[1:2] * d1 + cw[0:1] * d2).astype(BF16)
            du_ref[...] = du
            acc_ref[...] += _nt(du, wu_ref[...])

        @pl.when(j == nj - 1)
        def _():
            dx, dg = _norm_bwd(x_ref[...], g_ref[...], acc_ref[...])
            dx2_ref[...] = dx3_ref[...] + dx
            dg_ref[...] += dg

    return pl.pallas_call(
        body,
        name="ffn_bwd",
        grid=(nt, nj),
        in_specs=[
            pl.BlockSpec((tm, D), lambda i, j: (rev(i), 0)),
            pl.BlockSpec((tm, D), lambda i, j: (rev(i), 0)),
            pl.BlockSpec((1, D), lambda i, j: (0, 0)),
            pl.BlockSpec((tm, fc), lambda i, j: (rev(i), j)),
            pl.BlockSpec((tm, fc), lambda i, j: (rev(i), j)),
            pl.BlockSpec((tm, fc), lambda i, j: (rev(i), j)),
            pl.BlockSpec((tm, fc), lambda i, j: (rev(i), j)),
            pl.BlockSpec((3, fc), lambda i, j: (0, j)),
            pl.BlockSpec((3, fc), lambda i, j: (0, nj + j)),
            pl.BlockSpec((fc, D), lambda i, j: (j, 0)),
            pl.BlockSpec((D, fc), lambda i, j: (0, j)),
            pl.BlockSpec((D, fc), lambda i, j: (0, nj + j)),
        ],
        out_specs=[
            pl.BlockSpec((tm, D), lambda i, j: (rev(i), 0)),
            pl.BlockSpec((tm, fc), lambda i, j: (rev(i), j)),
            pl.BlockSpec((tm, fc), lambda i, j: (rev(i), j)),
            pl.BlockSpec((1, D), lambda i, j: (0, 0)),
            pl.BlockSpec((8, D_FF), lambda i, j: (0, 0)),
            pl.BlockSpec((8, D_FF), lambda i, j: (0, 0)),
        ],
        out_shape=[
            jax.ShapeDtypeStruct((T, D), F32),
            jax.ShapeDtypeStruct((T, D_FF), BF16),
            jax.ShapeDtypeStruct((T, D_FF), BF16),
            jax.ShapeDtypeStruct((1, D), F32),
            jax.ShapeDtypeStruct((8, D_FF), F32),
            jax.ShapeDtypeStruct((8, D_FF), F32),
        ],
        scratch_shapes=[
            pltpu.VMEM((tm, D), F32),
            pltpu.VMEM((nj, 2, 8, fc), F32),
            pltpu.VMEM((2, tm + 8, fc), F32),
        ],
        compiler_params=_cparams(("arbitrary", "arbitrary")),
    )(dx3, x2, g3, ug, uv, yg, yv, conv_w, conv_w, w_down, w_up, w_up)


def _xattn_bwd(dx2, x1, g2, qb, kv, w_mo, w_mq, tm):
    T, D = x1.shape
    M = kv.shape[0]

    def body(dx2_ref, x_ref, g_ref, q_ref, kv_ref, wo_ref, wq_ref, dx1_ref, dq_ref, dkv_ref, dg_ref):
        i = pl.program_id(0)

        @pl.when(i == 0)
        def _():
            dkv_ref[...] = jnp.zeros_like(dkv_ref)
            dg_ref[...] = jnp.zeros_like(dg_ref)

        dxv = dx2_ref[...]
        dom = _nt(dxv.astype(BF16), wo_ref[...]).astype(BF16)
        qb_ = q_ref[...]
        kvv = kv_ref[...]
        for hd in range(N_MEM_HEADS):
            sl = slice(hd * MEM_HD, (hd + 1) * MEM_HD)
            vsl = slice(D + hd * MEM_HD, D + (hd + 1) * MEM_HD)
            p = _xattn_probs(qb_, kvv, hd)
            dp = _nt(dom[:, sl], kvv[:, vsl])
            ds = (p * (dp - jnp.sum(p * dp, axis=1, keepdims=True)) * (MEM_HD ** -0.5)).astype(BF16)
            dq_ref[:, sl] = _nn(ds, kvv[:, sl]).astype(BF16)
            dkv_ref[:, sl] += _tn(ds, qb_[:, sl])
            dkv_ref[:, vsl] += _tn(p.astype(BF16), dom[:, sl])
        dh = _nt(dq_ref[...], wq_ref[...])
        dx, dg = _norm_bwd(x_ref[...], g_ref[...], dh)
        dx1_ref[...] = dxv + dx
        dg_ref[...] += dg

    return pl.pallas_call(
        body,
        name="xattn_bwd",
        grid=(T // tm,),
        in_specs=[
            pl.BlockSpec((tm, D), lambda i: (i, 0)),
            pl.BlockSpec((tm, D), lambda i: (i, 0)),
            pl.BlockSpec((1, D), lambda i: (0, 0)),
            pl.BlockSpec((tm, D), lambda i: (i, 0)),
            pl.BlockSpec((M, 2 * D), lambda i: (0, 0)),
            pl.BlockSpec((D, D), lambda i: (0, 0)),
            pl.BlockSpec((D, D), lambda i: (0, 0)),
        ],
        out_specs=[
            pl.BlockSpec((tm, D), lambda i: (i, 0)),
            pl.BlockSpec((tm, D), lambda i: (i, 0)),
            pl.BlockSpec((M, 2 * D), lambda i: (0, 0)),
            pl.BlockSpec((1, D), lambda i: (0, 0)),
        ],
        out_shape=[
            jax.ShapeDtypeStruct((T, D), F32),
            jax.ShapeDtypeStruct((T, D), BF16),
            jax.ShapeDtypeStruct((M, 2 * D), F32),
            jax.ShapeDtypeStruct((1, D), F32),
        ],
        compiler_params=_cparams(("arbitrary",)),
    )(dx2, x1, g2, qb, kv, w_mo, w_mq)


def _mem_kv_bwd(mem, gm, mb, dkv, w_mkv):
    M, D = mem.shape
    N = dkv.shape[1]

    def body(mem_ref, g_ref, m_ref, dkv_ref, w_ref, dw_ref, dg_ref):
        dkvb = dkv_ref[...].astype(BF16)
        for n0 in range(0, N, 512):
            dw_ref[:, n0:n0 + 512] = _tn(m_ref[...], dkvb[:, n0:n0 + 512]).astype(BF16)
        dm = _nt(dkvb, w_ref[...])
        mv = mem_ref[...]
        dg_ref[...] = jnp.sum(dm * (mv * _rstd(mv)), axis=0, keepdims=True)

    return pl.pallas_call(
        body,
        name="mem_kv_bwd",
        out_shape=[jax.ShapeDtypeStruct((D, N), BF16), jax.ShapeDtypeStruct((1, D), F32)],
        compiler_params=_cparams(),
    )(mem, gm, mb, dkv, w_mkv)


def _post_attn_bwd(dx1, fox_o, sb_o, gf, gs, w_out, tm):
    T, D = dx1.shape

    def body(dx_ref, f_ref, s_ref, gf_ref, gs_ref, w_ref, df_ref, ds_ref, dgf_ref, dgs_ref):
        i = pl.program_id(0)

        @pl.when(i == 0)
        def _():
            dgf_ref[...] = jnp.zeros_like(dgf_ref)
            dgs_ref[...] = jnp.zeros_like(dgs_ref)

        dmix = _nt(dx_ref[...].astype(BF16), w_ref[...])
        d, dg = _norm_bwd(f_ref[...], gf_ref[...], dmix[:, :FOX_W])
        df_ref[...] = d
        dgf_ref[...] += dg
        d, dg = _norm_bwd(s_ref[...], gs_ref[...], dmix[:, FOX_W:])
        ds_ref[...] = d
        dgs_ref[...] += dg

    return pl.pallas_call(
        body,
        name="post_attn_bwd",
        grid=(T // tm,),
        in_specs=[
            pl.BlockSpec((tm, D), lambda i: (i, 0)),
            pl.BlockSpec((tm, FOX_W), lambda i: (i, 0)),
            pl.BlockSpec((tm, FOX_W), lambda i: (i, 0)),
            pl.BlockSpec((1, FOX_W), lambda i: (0, 0)),
            pl.BlockSpec((1, FOX_W), lambda i: (0, 0)),
            pl.BlockSpec((D, D), lambda i: (0, 0)),
        ],
        out_specs=[
            pl.BlockSpec((tm, FOX_W), lambda i: (i, 0)),
            pl.BlockSpec((tm, FOX_W), lambda i: (i, 0)),
            pl.BlockSpec((1, FOX_W), lambda i: (0, 0)),
            pl.BlockSpec((1, FOX_W), lambda i: (0, 0)),
        ],
        out_shape=[
            jax.ShapeDtypeStruct((T, FOX_W), F32),
            jax.ShapeDtypeStruct((T, FOX_W), F32),
            jax.ShapeDtypeStruct((1, FOX_W), F32),
            jax.ShapeDtypeStruct((1, FOX_W), F32),
        ],
        compiler_params=_cparams(("arbitrary",)),
    )(dx1, fox_o, sb_o, gf, gs, w_out)


def _sb_bwd(proj, ltot, d_o, tq):
    T = proj.shape[0]

    def body(q_ref, k_ref, v_ref, lt_ref, do_ref, dq_ref, dk_ref, dv_ref,
             qh_s, doh_s, lt_s, z_s, da_s, ab_s, dzb_s, run_s, runw_s, dq_s):
        i = pl.program_id(1)

        @pl.when(i == 0)
        def _():
            dk_ref[...] = jnp.zeros_like(dk_ref)
            dv_ref[...] = jnp.zeros_like(dv_ref)

        lane = lax.broadcasted_iota(jnp.int32, (1, 128), 1)
        row = lax.broadcasted_iota(jnp.int32, (tq, tq), 0)
        col = lax.broadcasted_iota(jnp.int32, (tq, tq), 1)
        strict = col < row
        upto = jnp.where(row <= col, 1.0, 0.0).astype(BF16)
        before = jnp.where(row < col, 1.0, 0.0).astype(BF16)
        q = q_ref[...]
        dov = do_ref[...]
        for hh in range(2):
            qh, hmask = _head_q(q, hh, lane)
            qh_s[hh] = -qh
            doh_s[hh] = jnp.where(hmask, dov, 0.0).astype(BF16)
            lt_s[hh] = jnp.broadcast_to(_lanes_to_rows(lt_ref[hh], row == col), (tq, 128))
        run_s[...] = jnp.zeros_like(run_s)
        runw_s[...] = jnp.zeros_like(runw_s)
        dq_s[...] = jnp.zeros_like(dq_s)

        def rows(t):
            return pl.ds(pl.multiple_of(t * tq, tq), tq)

        def stage_a(t, slot):
            k = k_ref[rows(t), :]
            v = v_ref[rows(t), :]
            for hh in range(2):
                z_s[slot, hh] = _nt(qh_s[hh], k)
                da_s[slot, hh] = _nt(doh_s[hh], v)

        def stage_b(t, slot, diag):
            for hh in range(2):
                L, g = _sb_logs(z_s[slot, hh], strict if diag else None)
                upto_s = _split_dot(L, upto, SB_SUM_TERMS)
                run = run_s[hh]
                arg = (g + _lanes2(lt_s[hh] - run)) - upto_s
                if diag:
                    arg = jnp.where(strict, arg, NEG)
                a = jnp.exp(arg)
                w = a * da_s[slot, hh]
                w_before = _split_dot(w, before, SB_SUM_TERMS)
                run_w = runw_s[hh]
                d_keep = w_before + _lanes2(run_w)
                beta = jnp.exp(g)
                ndz = beta * (w + d_keep) - w
                if diag:
                    ndz = jnp.where(strict, ndz, 0.0)
                dzb_s[hh] = ndz.astype(BF16)
                ab_s[hh] = a.astype(BF16)
                run_s[hh] = run + jnp.broadcast_to(upto_s[:, tq - 1:tq], (tq, 128))
                runw_s[hh] = run_w + jnp.broadcast_to(w_before[:, tq - 1:tq] + w[:, tq - 1:tq], (tq, 128))

        def stage_c(t):
            k = k_ref[rows(t), :]
            dk_blk = None
            dv_blk = None
            for hh in range(2):
                dzb = dzb_s[hh]
                dq_s[hh] += _nn(dzb, k)
                dk_h = _tn(dzb, qh_s[hh])
                dv_h = _tn(ab_s[hh], doh_s[hh])
                dk_blk = dk_h if dk_blk is None else dk_blk + dk_h
                dv_blk = dv_h if dv_blk is None else dv_blk + dv_h
            dk_ref[rows(t), :] += dk_blk
            dv_ref[rows(t), :] += dv_blk

        _pipeline3(i + 1, stage_a, stage_b, stage_c, True)
        dq_ref[...] = (jnp.where(lane < HEAD_DIM, dq_s[0], dq_s[1]) * -(HEAD_DIM ** -0.5)).astype(BF16)

    return pl.pallas_call(
        body,
        name="sb_bwd",
        grid=(4, T // tq),
        in_specs=[
            pl.BlockSpec((tq, 128), lambda p, i: (i, 12 + p)),
            pl.BlockSpec((T, 128), lambda p, i: (0, 16 + p)),
            pl.BlockSpec((T, 128), lambda p, i: (0, 20 + p)),
            pl.BlockSpec((2, 1, tq), lambda p, i: (p, 0, i)),
            pl.BlockSpec((tq, 128), lambda p, i: (i, p)),
        ],
        out_specs=[
            pl.BlockSpec((tq, 128), lambda p, i: (i, p)),
            pl.BlockSpec((T, 128), lambda p, i: (0, p)),
            pl.BlockSpec((T, 128), lambda p, i: (0, p)),
        ],
        out_shape=[
            jax.ShapeDtypeStruct((T, FOX_W), BF16),
            jax.ShapeDtypeStruct((T, FOX_W), F32),
            jax.ShapeDtypeStruct((T, FOX_W), F32),
        ],
        scratch_shapes=[
            pltpu.VMEM((2, tq, 128), BF16),
            pltpu.VMEM((2, tq, 128), BF16),
            pltpu.VMEM((2, tq, 128), F32),
            pltpu.VMEM((2, 2, tq, tq), F32),
            pltpu.VMEM((2, 2, tq, tq), F32),
            pltpu.VMEM((2, tq, tq), BF16),
            pltpu.VMEM((2, tq, tq), BF16),
            pltpu.VMEM((2, tq, 128), F32),
            pltpu.VMEM((2, tq, 128), F32),
            pltpu.VMEM((2, tq, 128), F32),
        ],
        compiler_params=_cparams(("arbitrary", "arbitrary")),
    )(proj, proj, proj, ltot, d_o)


def _fox_bwd(proj, c_col, c_row, lse, d_o, o, tq, scatter=()):
    T = proj.shape[0]
    ns = len(scatter)
    nq = T // tq

    def body(*refs):
        q_ref, k_ref, v_ref, cq_ref, ck_ref, lse_ref, do_ref, o_ref = refs[:8]
        dq_ref, dk_ref, dv_ref, dck_ref, dcq_ref = refs[8 + ns:13 + ns]
        qh_s, doh_s, delta_s, shift_s, z_s, dp_s, pb_s, dsb_s, rs_s, dq_s = refs[13 + 2 * ns:23 + 2 * ns]
        i = pl.program_id(1)
        if ns:
            pair = pl.program_id(0)
            exchange = _Scatter(refs[8:8 + ns], refs[13 + ns:13 + 2 * ns], *refs[23 + 2 * ns:])

            @pl.when((pair == 0) & (i == 0))
            def _():
                exchange.start()

        @pl.when(i == 0)
        def _():
            dk_ref[...] = jnp.zeros_like(dk_ref)
            dv_ref[...] = jnp.zeros_like(dv_ref)
            dck_ref[...] = jnp.zeros_like(dck_ref)

        lane = lax.broadcasted_iota(jnp.int32, (1, 128), 1)
        row = lax.broadcasted_iota(jnp.int32, (tq, tq), 0)
        col = lax.broadcasted_iota(jnp.int32, (tq, tq), 1)
        q = q_ref[...]
        dov = do_ref[...]
        ov = o_ref[...]
        for hh in range(2):
            qh, hmask = _head_q(q, hh, lane)
            dohb = jnp.where(hmask, dov, 0.0).astype(BF16)
            qh_s[hh] = qh
            doh_s[hh] = dohb
            delta_s[hh] = jnp.broadcast_to(jnp.sum(dohb.astype(F32) * ov, axis=1, keepdims=True), (tq, tq))
            shift_s[hh] = jnp.broadcast_to(_lanes_to_rows(cq_ref[hh] - lse_ref[hh], row == col), (tq, tq))
        rs_s[...] = jnp.zeros_like(rs_s)
        dq_s[...] = jnp.zeros_like(dq_s)

        def rows(t):
            return pl.ds(pl.multiple_of((i - t) * tq, tq), tq)

        def stage_a(t, slot):
            k = k_ref[rows(t), :]
            v = v_ref[rows(t), :]
            for hh in range(2):
                z_s[slot, hh] = _nt(qh_s[hh], k)
                dp_s[slot, hh] = _nt(doh_s[hh], v)

        def stage_b(t, slot, diag):
            for hh in range(2):
                s = z_s[slot, hh] + shift_s[hh] - ck_ref[hh, :, rows(t)]
                if diag:
                    s = jnp.where(col <= row, s, NEG)
                p = jnp.exp(s)
                ds = p * (dp_s[slot, hh] - delta_s[hh])
                pb_s[hh] = p.astype(BF16)
                dsb_s[hh] = ds.astype(BF16)
                dck_ref[hh, :, rows(t)] += jnp.sum(ds, axis=0, keepdims=True)
                rs_s[hh] += jnp.sum(ds, axis=1, keepdims=True)

        def stage_c(t):
            k = k_ref[rows(t), :]
            dk_blk = None
            dv_blk = None
            for hh in range(2):
                dsb = dsb_s[hh]
                dq_s[hh] += _nn(dsb, k)
                dk_h = _tn(dsb, qh_s[hh])
                dv_h = _tn(pb_s[hh], doh_s[hh])
                dk_blk = dk_h if dk_blk is None else dk_blk + dk_h
                dv_blk = dv_h if dv_blk is None else dv_blk + dv_h
            dk_ref[rows(t), :] += dk_blk
            dv_ref[rows(t), :] += dv_blk

        _pipeline3(i + 1, stage_a, stage_b, stage_c, False)
        dcq_ref[0] = _rows_to_lanes(rs_s[0], row == col)
        dcq_ref[1] = _rows_to_lanes(rs_s[1], row == col)
        dq_ref[...] = (jnp.where(lane < HEAD_DIM, dq_s[0], dq_s[1]) * (HEAD_DIM ** -0.5)).astype(BF16)
        if ns:
            @pl.when((pair == 3) & (i == nq - 1))
            def _():
                exchange.finish()

    res = pl.pallas_call(
        body,
        name="fox_bwd",
        grid=(4, nq),
        in_specs=[
            pl.BlockSpec((tq, 128), lambda p, i: (i, p)),
            pl.BlockSpec((T, 128), lambda p, i: (0, 4 + p)),
            pl.BlockSpec((T, 128), lambda p, i: (0, 8 + p)),
            pl.BlockSpec((2, 1, tq), lambda p, i: (p, 0, i)),
            pl.BlockSpec((2, 1, T), lambda p, i: (p, 0, 0)),
            pl.BlockSpec((2, 1, tq), lambda p, i: (p, 0, i)),
            pl.BlockSpec((tq, 128), lambda p, i: (i, p)),
            pl.BlockSpec((tq, 128), lambda p, i: (i, p)),
        ] + [_ANY] * ns,
        out_specs=[
            pl.BlockSpec((tq, 128), lambda p, i: (i, p)),
            pl.BlockSpec((T, 128), lambda p, i: (0, p)),
            pl.BlockSpec((T, 128), lambda p, i: (0, p)),
            pl.BlockSpec((2, 1, T), lambda p, i: (p, 0, 0)),
            pl.BlockSpec((2, 1, tq), lambda p, i: (p, 0, i)),
        ] + [_ANY] * ns,
        out_shape=[
            jax.ShapeDtypeStruct((T, FOX_W), BF16),
            jax.ShapeDtypeStruct((T, FOX_W), F32),
            jax.ShapeDtypeStruct((T, FOX_W), F32),
            jax.ShapeDtypeStruct((N_FOX, 1, T), F32),
            jax.ShapeDtypeStruct((N_FOX, 1, T), F32),
        ] + [jax.ShapeDtypeStruct(b.shape, b.dtype) for b in scatter],
        scratch_shapes=[
            pltpu.VMEM((2, tq, 128), BF16),
            pltpu.VMEM((2, tq, 128), BF16),
            pltpu.VMEM((2, tq, tq), F32),
            pltpu.VMEM((2, tq, tq), F32),
            pltpu.VMEM((2, 2, tq, tq), F32),
            pltpu.VMEM((2, 2, tq, tq), F32),
            pltpu.VMEM((2, tq, tq), BF16),
            pltpu.VMEM((2, tq, tq), BF16),
            pltpu.VMEM((2, tq, 128), F32),
            pltpu.VMEM((2, tq, 128), F32),
        ] + (_comm_sems(ns) if ns else []),
        compiler_params=_cparams(("arbitrary", "arbitrary")),
    )(proj, proj, proj, c_col, c_row, lse, d_o, o, *scatter)
    res = list(res)
    return (*res[:5], res[5:])


def _forget_bwd(dcq, dck, xf, tc):
    H, T = xf.shape
    nc = T // tc

    def body(dcq_ref, dck_ref, xf_ref, dxf_ref, db_ref):
        row = lax.broadcasted_iota(jnp.int32, (tc, tc), 0)
        col = lax.broadcasted_iota(jnp.int32, (tc, tc), 1)
        from_here = jnp.where(row >= col, 1.0, 0.0).astype(BF16)

        def chunk(n, carry):
            run, db = carry
            cs = pl.multiple_of((nc - 1 - n) * tc, tc)
            dc = dcq_ref[:, pl.ds(cs, tc)] - dck_ref[:, pl.ds(cs, tc)]
            dlogf = _split_dot(dc, from_here, 3) + run
            xfv = xf_ref[:, pl.ds(cs, tc)]
            dxf = dlogf * jax.nn.sigmoid(-xfv)
            dxf_ref[:, pl.ds(cs, tc)] = dxf
            return dlogf[:, 0:1], db + jnp.sum(dxf, axis=1, keepdims=True)

        _, db = lax.fori_loop(0, nc, chunk, (jnp.zeros((H, 1), F32), jnp.zeros((H, 1), F32)))
        db_ref[...] = db

    return pl.pallas_call(
        body,
        name="forget_bwd",
        out_shape=[jax.ShapeDtypeStruct((H, T), F32), jax.ShapeDtypeStruct((H, 1), F32)],
        compiler_params=_cparams(),
    )(dcq, dck, xf)


def _inproj_bwd(dproj, w_in_pad, x, g1, dx1, tm, scatter=()):
    T, D = x.shape
    N = dproj.shape[1]
    ns = len(scatter)
    nt = T // tm

    def body(*refs):
        dp_ref, w_ref, x_ref, g_ref, dx1_ref = refs[:5]
        dx_ref, dg_ref = refs[5 + ns:7 + ns]
        i = pl.program_id(0)
        if ns:
            exchange = _Scatter(refs[5:5 + ns], refs[7 + ns:7 + 2 * ns], *refs[7 + 2 * ns:])

            @pl.when(i == 0)
            def _():
                exchange.start()

        @pl.when(i == 0)
        def _():
            dg_ref[...] = jnp.zeros_like(dg_ref)

        dh = _nt(dp_ref[...], w_ref[...])
        dx, dg = _norm_bwd(x_ref[...], g_ref[...], dh)
        dx_ref[...] = dx1_ref[...] + dx
        dg_ref[...] += dg
        if ns:
            @pl.when(i == nt - 1)
            def _():
                exchange.finish()

    res = pl.pallas_call(
        body,
        name="inproj_bwd",
        grid=(nt,),
        in_specs=[
            pl.BlockSpec((tm, N), lambda i: (i, 0)),
            pl.BlockSpec((D, N), lambda i: (0, 0)),
            pl.BlockSpec((tm, D), lambda i: (i, 0)),
            pl.BlockSpec((1, D), lambda i: (0, 0)),
            pl.BlockSpec((tm, D), lambda i: (i, 0)),
        ] + [_ANY] * ns,
        out_specs=[
            pl.BlockSpec((tm, D), lambda i: (i, 0)),
            pl.BlockSpec((1, D), lambda i: (0, 0)),
        ] + [_ANY] * ns,
        out_shape=[jax.ShapeDtypeStruct((T, D), F32), jax.ShapeDtypeStruct((1, D), F32)]
        + [jax.ShapeDtypeStruct(b.shape, b.dtype) for b in scatter],
        scratch_shapes=_comm_sems(ns) if ns else [],
        compiler_params=_cparams(("arbitrary",)),
    )(dproj, w_in_pad, x, g1, dx1, *scatter)
    res = list(res)
    return res[0], res[1], res[2:]


def _matmul_tn(a, b, name, cast_b=False):
    T, K = a.shape
    N = b.shape[1]
    bt = min(T, 512)
    bk = _tile_div(K, 1536)
    bn = _tile_div(N, 1536)
    nt = T // bt

    def body(a_ref, b_ref, o_ref, acc_ref):
        t = pl.program_id(2)

        @pl.when(t == 0)
        def _():
            acc_ref[...] = jnp.zeros_like(acc_ref)

        bv = b_ref[...]
        if cast_b:
            bv = bv.astype(BF16)
        acc_ref[...] += _tn(a_ref[...], bv)

        @pl.when(t == nt - 1)
        def _():
            o_ref[...] = acc_ref[...].astype(BF16)

    return pl.pallas_call(
        body,
        name=name,
        grid=(K // bk, N // bn, nt),
        in_specs=[
            pl.BlockSpec((bt, bk), lambda k, n, t: (t, k)),
            pl.BlockSpec((bt, bn), lambda k, n, t: (t, n)),
        ],
        out_specs=pl.BlockSpec((bk, bn), lambda k, n, t: (k, n)),
        out_shape=jax.ShapeDtypeStruct((K, N), BF16),
        scratch_shapes=[pltpu.VMEM((bk, bn), F32)],
        compiler_params=_cparams(("arbitrary", "arbitrary", "arbitrary")),
    )(a, b)


def _local_step(x, mem, target, p, tm, tq, late=None):
    T, D = x.shape
    w_in = p["w_in"]
    w_qkv = w_in[:, :QKV_W]
    w_f_t = w_in[:, QKV_W:].T
    w_in_pad = jnp.pad(w_in, ((0, 0), (0, IN_PAD - w_in.shape[1])))
    b_f = p["b_forget"].reshape(N_FOX, 1)

    proj, h1, xf, c = _inproj_fwd(x, p["attn_norm_g"], w_qkv, w_f_t, b_f, tm)
    c_col = c.reshape(N_FOX, 1, T)
    c_row = c.reshape(N_FOX, 1, T)
    if late:
        fox_o, lse, gathered = _fox_fwd(proj, c_col, c_row, tq, gather=[late[n] for n in _LATE])
        p = dict(p, **{n: _gathered_full(n, gv) for n, gv in zip(_LATE, gathered)})
    else:
        fox_o, lse, _ = _fox_fwd(proj, c_col, c_row, tq)
    sb_o, sb_ltot = _sb_fwd(proj, tq)
    x1, mixed = _post_attn_fwd(fox_o, sb_o, p["fox_out_g"], p["sb_out_g"], p["w_out"], x, tm)
    mb, kv = _mem_kv_fwd(mem, p["mem_norm_g"], p["w_mkv"])
    x2, h2, qb, om = _xattn_fwd(x1, p["xattn_norm_g"], p["w_mq"], kv, p["w_mo"], tm)
    x3, h3, ug, uv, yg, yv, a = _ffn_fwd(
        x2, p["ffn_norm_g"], p["w_up"], p["conv_w"], p["conv_b"], p["w_down"], tm)
    dx3, loss_blk, d_final_g = _loss_head(x3, p["final_norm_g"], target, tm)

    g = {"final_norm_g": d_final_g}
    dx2, du_g, du_v, g["ffn_norm_g"], dc_g, dc_v = _ffn_bwd(
        dx3, x2, p["ffn_norm_g"], ug, uv, yg, yv, p["conv_w"], p["w_down"], p["w_up"], tm)
    g["w_down"] = _matmul_tn(a, dx3, "dw_down", cast_b=True)
    g["w_up"] = jnp.concatenate([_matmul_tn(h3, du_g, "dw_up_gate"), _matmul_tn(h3, du_v, "dw_up_val")], axis=1)
    dconv = jnp.concatenate([dc_g, dc_v], axis=1)
    g["conv_w"] = dconv[0:3]
    g["conv_b"] = dconv[3:4]
    dx1, dq_m, dkv, g["xattn_norm_g"] = _xattn_bwd(dx2, x1, p["xattn_norm_g"], qb, kv, p["w_mo"], p["w_mq"], tm)
    g["w_mo"] = _matmul_tn(om, dx2, "dw_mo", cast_b=True)
    g["w_mq"] = _matmul_tn(h2, dq_m, "dw_mq")
    g["w_mkv"], g["mem_norm_g"] = _mem_kv_bwd(mem, p["mem_norm_g"], mb, dkv, p["w_mkv"])
    d_fox, d_sb, g["fox_out_g"], g["sb_out_g"] = _post_attn_bwd(
        dx1, fox_o, sb_o, p["fox_out_g"], p["sb_out_g"], p["w_out"], tm)
    g["w_out"] = _matmul_tn(mixed, dx1, "dw_out", cast_b=True)
    dq_s, dk_s, dv_s = _sb_bwd(proj, sb_ltot, d_sb, tq)
    if late:
        dq_f, dk_f, dv_f, dck, dcq, parts = _fox_bwd(proj, c_col, c_row, lse, d_fox, fox_o, tq,
                                                      scatter=[_grad_blocks(n, g[n]) for n in _LATE])
        g["parts"] = dict(zip(_LATE, parts))
    else:
        dq_f, dk_f, dv_f, dck, dcq, _ = _fox_bwd(proj, c_col, c_row, lse, d_fox, fox_o, tq)
    dxf, db = _forget_bwd(dcq.reshape(N_FOX, T), dck.reshape(N_FOX, T), xf, min(T, 512))
    g["b_forget"] = db.reshape(1, N_FOX)
    dproj = jnp.concatenate([
        dq_f, dk_f.astype(BF16), dv_f.astype(BF16), dq_s, dk_s.astype(BF16), dv_s.astype(BF16),
        jnp.pad(dxf.T, ((0, 0), (0, IN_PAD - QKV_W - N_FOX))).astype(BF16)], axis=1)
    g["w_in"] = _matmul_tn(h1, dproj, "dw_in")[:, :w_in.shape[1]]
    if late:
        grad_x, g["attn_norm_g"], (g["parts"]["w_in"],) = _inproj_bwd(
            dproj, w_in_pad, x, p["attn_norm_g"], dx1, tm, scatter=[_grad_blocks("w_in", g["w_in"])])
    else:
        grad_x, g["attn_norm_g"], _ = _inproj_bwd(dproj, w_in_pad, x, p["attn_norm_g"], dx1, tm)
    return loss_blk, grad_x, g


def _mesh_pos():
    return lax.axis_index("x"), lax.axis_index("y"), lax.axis_index("c")


def _flip(pos, k):
    return tuple(1 - v if (k >> b) & 1 else v for v, b in zip(pos, (2, 1, 0)))


def _slot(pos):
    return 4 * pos[0] + 2 * pos[1] + pos[2]


_CHIPS = (4, 2, 6)


def _comm_sems(n):
    return [pltpu.SemaphoreType.DMA((7 * n,)), pltpu.SemaphoreType.DMA((7 * n,)), pltpu.SemaphoreType.DMA((n,))]


class _Gather:
    def __init__(self, ins, outs, send_sems, recv_sems, local_sems):
        self.ins, self.outs, self.n = ins, outs, len(ins)
        self.send_sems, self.recv_sems, self.local_sems = send_sems, recv_sems, local_sems
        self.me = _mesh_pos()
        self.sibling = _flip(self.me, 1)

    def _copy(self, a, kk, block, to, src=None):
        rows = self.outs[a].at[_slot(block)]
        return pltpu.make_async_remote_copy(
            src_ref=rows if src is None else src, dst_ref=rows,
            send_sem=self.send_sems.at[7 * a + kk], recv_sem=self.recv_sems.at[7 * a + kk],
            device_id=to, device_id_type=MESH)

    def _mine(self):
        return [pltpu.make_async_copy(self.ins[a], self.outs[a].at[_slot(self.me)], self.local_sems.at[a])
                for a in range(self.n)]

    def _first(self):
        out = []
        for a in range(self.n):
            out.append(self._copy(a, 0, self.me, self.sibling, src=self.ins[a]))
            out += [self._copy(a, 1 + j, self.me, _flip(self.me, k), src=self.ins[a]) for j, k in enumerate(_CHIPS)]
        return out

    def _passed(self):
        return [self._copy(a, 4 + j, _flip(self.me, k), self.sibling)
                for j, k in enumerate(_CHIPS) for a in range(self.n)]

    def start(self):
        for cp in self._mine() + self._first():
            cp.start()

    def forward(self):
        for j, k in enumerate(_CHIPS):
            for a in range(self.n):
                self._copy(a, 1 + j, _flip(self.me, k), self.me).wait_recv()
                self._copy(a, 4 + j, _flip(self.me, k), self.sibling).start()

    def finish(self):
        for a in range(self.n):
            self._copy(a, 0, self.sibling, self.me).wait_recv()
            for j, k in enumerate(_CHIPS):
                self._copy(a, 4 + j, _flip(self.sibling, k), self.me).wait_recv()
        for cp in self._first() + self._passed():
            cp.wait_send()
        for cp in self._mine():
            cp.wait()


class _Scatter:
    def __init__(self, ins, outs, send_sems, recv_sems, local_sems):
        self.ins, self.outs, self.n = ins, outs, len(ins)
        self.send_sems, self.recv_sems, self.local_sems = send_sems, recv_sems, local_sems
        self.me = _mesh_pos()

    def _copy(self, a, k, landed=False):
        peer = _flip(self.me, k)
        return pltpu.make_async_remote_copy(
            src_ref=self.ins[a].at[_slot(peer)], dst_ref=self.outs[a].at[_slot(peer if landed else self.me)],
            send_sem=self.send_sems.at[7 * a + k - 1], recv_sem=self.recv_sems.at[7 * a + k - 1],
            device_id=peer, device_id_type=MESH)

    def _mine(self):
        s = _slot(self.me)
        return [pltpu.make_async_copy(self.ins[a].at[s], self.outs[a].at[s], self.local_sems.at[a])
                for a in range(self.n)]

    def start(self):
        for cp in self._mine() + [self._copy(a, k) for k in range(1, 8) for a in range(self.n)]:
            cp.start()

    def finish(self):
        for k in range(1, 8):
            for a in range(self.n):
                self._copy(a, k, landed=True).wait_recv()
        for k in range(1, 8):
            for a in range(self.n):
                self._copy(a, k).wait_send()
        for cp in self._mine():
            cp.wait()


_ANY = pl.BlockSpec(memory_space=pl.ANY)


def _gathered_shapes(shards):
    return [jax.ShapeDtypeStruct((N_DEV,) + s.shape, s.dtype) for s in shards]


def _all_gather(shards, name):
    n = len(shards)

    def body(*refs):
        g = _Gather(refs[:n], refs[n:2 * n], *refs[2 * n:])
        g.start()
        g.forward()
        g.finish()

    return pl.pallas_call(
        body, name=name, in_specs=[_ANY] * n, out_specs=[_ANY] * n,
        out_shape=_gathered_shapes(shards), scratch_shapes=_comm_sems(n),
    )(*shards)


def _adamw_math(w, g, m, v):
    m2 = ADAM_B1 * m + (1.0 - ADAM_B1) * g
    v2 = ADAM_B2 * v + (1.0 - ADAM_B2) * (g * g)
    m_hat = m2 / (1.0 - ADAM_B1 ** ADAM_STEP)
    v_hat = v2 / (1.0 - ADAM_B2 ** ADAM_STEP)
    delta = -ADAM_LR * (m_hat / (jnp.sqrt(v_hat) + ADAM_EPS) + ADAM_WD * w)
    return delta, m2, v2


def _adamw(w, parts, m, v, name):
    R, C = w.shape
    br = 128 if R % 128 == 0 else R

    def body(w_ref, p_ref, m_ref, v_ref, g_ref, d_ref, nm_ref, nv_ref):
        g = p_ref[0].astype(F32)
        for s in range(1, N_DEV):
            g = g + p_ref[s].astype(F32)
        g_ref[...] = g
        d_ref[...], nm_ref[...], nv_ref[...] = _adamw_math(w_ref[...], g, m_ref[...], v_ref[...])

    spec = pl.BlockSpec((br, C), lambda i: (i, 0))
    return pl.pallas_call(
        body,
        name=name,
        grid=(R // br,),
        in_specs=[spec, pl.BlockSpec((N_DEV, br, C), lambda i: (0, i, 0)), spec, spec],
        out_specs=[spec] * 4,
        out_shape=[jax.ShapeDtypeStruct((R, C), F32)] * 4,
        compiler_params=_cparams(("arbitrary",)),
    )(w, parts, m, v)


_SHARDED = ("w_in", "w_out", "w_mq", "w_mkv", "w_mo", "w_up", "conv_w", "w_down")
_LATE = _SHARDED[1:]
_COL_SHARDED = ("w_in", "w_mkv", "w_up", "conv_w")
_REPLICATED = ("attn_norm_g", "b_forget", "fox_out_g", "sb_out_g", "xattn_norm_g", "mem_norm_g",
               "ffn_norm_g", "conv_b", "final_norm_g")
_WEIGHTS = ("attn_norm_g", "w_in", "b_forget", "fox_out_g", "sb_out_g", "w_out", "xattn_norm_g", "mem_norm_g",
            "w_mq", "w_mkv", "w_mo", "ffn_norm_g", "w_up", "conv_w", "conv_b", "w_down", "final_norm_g")


def _pack_rows(n):
    return -(-n // 128)


def _pack(vals, rows_total):
    parts = []
    for v in vals:
        flat = v.reshape(-1)
        parts.append(jnp.pad(flat, (0, _pack_rows(flat.shape[0]) * 128 - flat.shape[0])))
    flat = jnp.concatenate(parts)
    return jnp.pad(flat, (0, rows_total * 128 - flat.shape[0])).reshape(rows_total, 128)


def _unpack(packed, shapes):
    out = []
    r = 0
    for shp in shapes:
        n = 1
        for d in shp:
            n *= d
        out.append(packed[r:r + _pack_rows(n)].reshape(-1)[:n].reshape(shp))
        r += _pack_rows(n)
    return out


def _gathered_full(name, gathered):
    if name in _COL_SHARDED:
        return jnp.transpose(gathered, (1, 0, 2)).reshape(gathered.shape[1], -1)
    return gathered.reshape(-1, gathered.shape[2])


def _to_blocks(name, full):
    if name in _COL_SHARDED:
        r = full.shape[0]
        return jnp.transpose(full.reshape(r, N_DEV, -1), (1, 0, 2))
    return full.reshape(N_DEV, -1, full.shape[1])


def _grad_blocks(name, full):
    blocks = _to_blocks(name, full)
    return blocks if name == "conv_w" else blocks.astype(BF16)


def _step(args, tm, tq):
    w = {n: args[n] for n in _WEIGHTS}
    mom = {n: args["m_" + n] for n in _WEIGHTS}
    var = {n: args["v_" + n] for n in _WEIGHTS}
    x = args["x"][0]
    mem = args["mem"][0]
    target = args["loss_target"][0]

    def flat2(a):
        return a.reshape(a.shape[-2], a.shape[-1]) if a.ndim == 3 else a.reshape(1, -1)

    shards = {n: flat2(w[n]) if n == "conv_w" else flat2(w[n]).astype(BF16) for n in _SHARDED}
    (w_in_all,) = _all_gather([shards["w_in"]], "gather_w_in")
    p = {"w_in": _gathered_full("w_in", w_in_all)}
    for n in _REPLICATED:
        p[n] = flat2(w[n])

    loss_blk, grad_x, g = _local_step(x, mem, target, p, tm, tq, late={n: shards[n] for n in _LATE})

    parts = g["parts"]
    out = {}
    for n in _SHARDED:
        res = _adamw(flat2(w[n]), parts[n], flat2(mom[n]), flat2(var[n]), "adamw_" + n)
        out[n] = [r.reshape(w[n].shape) for r in res]

    shapes = [w[n].shape for n in _REPLICATED]
    rows = sum(_pack_rows(flat2(w[n]).shape[1]) for n in _REPLICATED) + 1
    rows = -(-rows // 8) * 8
    g_pack = _pack([g[n] for n in _REPLICATED] + [loss_blk[0:1, :]], rows)
    (g_all,) = _all_gather([g_pack], "gather_small")
    res = _adamw(_pack([w[n] for n in _REPLICATED], rows), g_all,
                 _pack([mom[n] for n in _REPLICATED], rows), _pack([var[n] for n in _REPLICATED], rows),
                 "adamw_small")
    n_rows_params = sum(_pack_rows(flat2(w[n]).shape[1]) for n in _REPLICATED)
    loss = res[0][n_rows_params, 0]
    unpacked = [_unpack(r, shapes) for r in res]
    for k, n in enumerate(_REPLICATED):
        out[n] = [unpacked[q][k] for q in range(4)]

    grads = [out[n][0] for n in _WEIGHTS]
    deltas = [out[n][1] for n in _WEIGHTS]
    new_m = [out[n][2] for n in _WEIGHTS]
    new_v = [out[n][3] for n in _WEIGHTS]
    return (loss, grad_x[None], *grads, *deltas, *new_m, *new_v)


def kernel(x, mem, attn_norm_g, w_in, b_forget, fox_out_g, sb_out_g, w_out, xattn_norm_g, mem_norm_g, w_mq, w_mkv, w_mo, ffn_norm_g, w_up, conv_w, conv_b, w_down, final_norm_g, loss_target, m_attn_norm_g, m_w_in, m_b_forget, m_fox_out_g, m_sb_out_g, m_w_out, m_xattn_norm_g, m_mem_norm_g, m_w_mq, m_w_mkv, m_w_mo, m_ffn_norm_g, m_w_up, m_conv_w, m_conv_b, m_w_down, m_final_norm_g, v_attn_norm_g, v_w_in, v_b_forget, v_fox_out_g, v_sb_out_g, v_w_out, v_xattn_norm_g, v_mem_norm_g, v_w_mq, v_w_mkv, v_w_mo, v_ffn_norm_g, v_w_up, v_conv_w, v_conv_b, v_w_down, v_final_norm_g):
    args = dict(locals())
    T = x.shape[1]
    return _step(args, tm=min(T, 512), tq=min(T, 256))
```

```python
import functools

import jax
import jax.numpy as jnp
from jax import lax
from jax.experimental import pallas as pl
from jax.experimental.pallas import tpu as pltpu

F32 = jnp.float32
BF16 = jnp.bfloat16
EPS = 1e-6
NEG = -1e30
LOG2E = 1.4426950408889634

HEAD_DIM = 64
N_FOX = 8
FOX_W = 512
QKV_W = 3072
IN_PAD = 3200
N_MEM_HEADS = 4
MEM_HD = 256
D_FF = 2816
FF_CHUNK = 256
N_DEV = 8

ADAM_LR = 0.001
ADAM_B1 = 0.9
ADAM_B2 = 0.999
ADAM_EPS = 1e-08
ADAM_WD = 0.01
ADAM_STEP = 10

SB_SUM_TERMS = 1

VMEM_LIMIT = 56 * 1024 * 1024
MESH = pl.DeviceIdType.MESH


def _cparams(sem=None):
    return pltpu.CompilerParams(dimension_semantics=sem, vmem_limit_bytes=VMEM_LIMIT)


def _nt(a, b):
    return lax.dot_general(a, b, (((1,), (1,)), ((), ())), preferred_element_type=F32)


def _tn(a, b):
    return lax.dot_general(a, b, (((0,), (0,)), ((), ())), preferred_element_type=F32)


def _nn(a, b):
    return jnp.dot(a, b, preferred_element_type=F32)


def _split_dot(a, m01, terms):
    out = None
    r = a
    for t in range(terms):
        p = r.astype(BF16)
        d = _nn(p, m01)
        out = d if out is None else out + d
        if t + 1 < terms:
            r = r - p.astype(F32)
    return out


def _rstd(xv):
    return lax.rsqrt(jnp.mean(xv * xv, axis=-1, keepdims=True) + EPS)


def _norm_bwd(xv, g, dh):
    r = _rstd(xv)
    xhat = xv * r
    dxhat = dh * g
    dx = r * (dxhat - xhat * jnp.mean(dxhat * xhat, axis=-1, keepdims=True))
    dg = jnp.sum(dh * xhat, axis=0, keepdims=True)
    return dx, dg


def _tile_div(n, cap):
    best = None
    for d in range(128, min(n, cap) + 1, 128):
        if n % d == 0:
            best = d
    assert best is not None, n
    return best


def _inproj_fwd(x, g1, w_qkv, w_f_t, b_f, tm):
    T, D = x.shape
    N = w_qkv.shape[1]
    H = w_f_t.shape[0]

    def body(x_ref, g_ref, w_ref, wf_ref, b_ref, proj_ref, h_ref, xf_ref, c_ref, carry_ref):
        i = pl.program_id(0)

        @pl.when(i == 0)
        def _():
            carry_ref[...] = jnp.zeros_like(carry_ref)

        xv = x_ref[...]
        h = (xv * _rstd(xv) * g_ref[...]).astype(BF16)
        h_ref[...] = h
        for n0 in range(0, N, 512):
            proj_ref[:, n0:n0 + 512] = _nn(h, w_ref[:, n0:n0 + 512]).astype(BF16)
        xf = _nt(wf_ref[...], h) + b_ref[...]
        xf_ref[...] = xf
        logf = jnp.minimum(xf, 0.0) - jnp.log1p(jnp.exp(-jnp.abs(xf)))
        row = lax.broadcasted_iota(jnp.int32, (tm, tm), 0)
        col = lax.broadcasted_iota(jnp.int32, (tm, tm), 1)
        upper = jnp.where(row <= col, 1.0, 0.0).astype(BF16)
        c = _split_dot(logf, upper, 3) + carry_ref[...]
        c_ref[...] = c
        carry_ref[...] = c[:, tm - 1:tm]

    return pl.pallas_call(
        body,
        name="inproj_fwd",
        grid=(T // tm,),
        in_specs=[
            pl.BlockSpec((tm, D), lambda i: (i, 0)),
            pl.BlockSpec((1, D), lambda i: (0, 0)),
            pl.BlockSpec((D, N), lambda i: (0, 0)),
            pl.BlockSpec((H, D), lambda i: (0, 0)),
            pl.BlockSpec((H, 1), lambda i: (0, 0)),
        ],
        out_specs=[
            pl.BlockSpec((tm, N), lambda i: (i, 0)),
            pl.BlockSpec((tm, D), lambda i: (i, 0)),
            pl.BlockSpec((H, tm), lambda i: (0, i)),
            pl.BlockSpec((H, tm), lambda i: (0, i)),
        ],
        out_shape=[
            jax.ShapeDtypeStruct((T, N), BF16),
            jax.ShapeDtypeStruct((T, D), BF16),
            jax.ShapeDtypeStruct((H, T), F32),
            jax.ShapeDtypeStruct((H, T), F32),
        ],
        scratch_shapes=[pltpu.VMEM((H, 1), F32)],
        compiler_params=_cparams(("arbitrary",)),
    )(x, g1, w_qkv, w_f_t, b_f)


def _head_q(q, hh, lane):
    hmask = (lane >= HEAD_DIM * hh) & (lane < HEAD_DIM * (hh + 1))
    qh = jnp.where(hmask, q.astype(F32), 0.0) * (HEAD_DIM ** -0.5)
    return qh.astype(BF16), hmask


def _pipeline3(n, stage_a, stage_b, stage_c, diag_last):
    stage_a(0, 0)
    if diag_last:
        @pl.when(n == 1)
        def _():
            stage_b(0, 0, True)

        @pl.when(n >= 2)
        def _():
            stage_b(0, 0, False)
    else:
        stage_b(0, 0, True)

    @pl.when(n >= 2)
    def _():
        stage_a(1, 1)

    def pair(m, carry):
        t = 2 + 2 * m
        stage_c(t - 2)
        stage_b(t - 1, 1, False)
        stage_a(t, 0)
        stage_c(t - 1)
        stage_b(t, 0, False)
        stage_a(t + 1, 1)
        return carry

    lax.fori_loop(0, (n - 2) // 2, pair, 0)
    odd = n % 2 == 1

    @pl.when((n >= 3) & odd)
    def _():
        stage_c(n - 3)
        stage_b(n - 2, 1, False)
        stage_a(n - 1, 0)

    @pl.when((n >= 2) & odd)
    def _():
        stage_c(n - 2)
        stage_b(n - 1, 0, diag_last)

    @pl.when((n >= 2) & jnp.logical_not(odd))
    def _():
        stage_c(n - 2)
        stage_b(n - 1, 1, diag_last)

    stage_c(n - 1)


def _lanes2(x):
    return jnp.concatenate([x, x], axis=1)


def _lanes_to_rows(vec, eye):
    return jnp.sum(jnp.where(eye, jnp.broadcast_to(vec, eye.shape), 0.0), axis=1, keepdims=True)


def _rows_to_lanes(rep, eye):
    return jnp.sum(jnp.where(eye, _lanes2(rep), 0.0), axis=0, keepdims=True)


FOX_DEAD = -110.0


def _fox_key_norms(k_ref, kn_s, lane):
    T = k_ref.shape[0]
    rows = min(T, 512)
    for hh in range(2):
        hmask = (lane >= HEAD_DIM * hh) & (lane < HEAD_DIM * (hh + 1))

        def chunk(n, best, hmask=hmask):
            kf = jnp.where(hmask, k_ref[pl.ds(pl.multiple_of(n * rows, rows), rows), :].astype(F32), 0.0)
            sq = jnp.sum(kf * kf, axis=1, keepdims=True)
            return jnp.maximum(best, jnp.max(sq, axis=0, keepdims=True))

        best = lax.fori_loop(0, T // rows, chunk, jnp.zeros((1, 1), F32))
        kn_s[hh] = jnp.broadcast_to(best, kn_s.shape[1:])


def _fox_live_blocks(i, qh_s, kn_s, cq_ref, cke_ref):
    nq = cke_ref.shape[-1]
    jj = lax.broadcasted_iota(jnp.int32, (1, nq), 1)
    first = None
    for hh in range(2):
        qf = qh_s[hh].astype(F32)
        qn = jnp.max(jnp.sum(qf * qf, axis=1, keepdims=True), axis=0, keepdims=True)
        zb = jnp.sqrt(qn * kn_s[hh][0:1, 0:1]) * 1.001
        bound = (2.0 * zb + cq_ref[hh][:, 0:1]) - cke_ref[hh]
        live = (bound >= FOX_DEAD) & (jj <= i)
        f = jnp.min(jnp.where(live, jj, i).astype(F32), axis=1, keepdims=True)
        first = f if first is None else jnp.minimum(first, f)
    return i + 1 - first[0, 0].astype(jnp.int32)


def _fox_fwd(proj, c_col, c_row, c_ends, tq):
    T = proj.shape[0]
    assert tq == 256
    nq = T // tq

    def body(q_ref, k_ref, v_ref, cq_ref, ck_ref, cke_ref, o_ref, lse_ref,
             qh_s, cq_s, z_s, p_s, al_s, m_s, l_s, acc_s, kn_s):
        i = pl.program_id(1)
        lane = lax.broadcasted_iota(jnp.int32, (1, 128), 1)
        row = lax.broadcasted_iota(jnp.int32, (tq, tq), 0)
        col = lax.broadcasted_iota(jnp.int32, (tq, tq), 1)
        ones = jnp.ones((tq, 128), BF16)

        @pl.when(i == 0)
        def _():
            _fox_key_norms(k_ref, kn_s, lane)

        q = q_ref[...]
        for hh in range(2):
            qh_s[hh] = _head_q(q, hh, lane)[0]
            cq_s[hh] = jnp.broadcast_to(_lanes_to_rows(cq_ref[hh], row == col), (tq, tq))
        m_s[...] = jnp.full(m_s.shape, NEG, F32)
        l_s[...] = jnp.zeros_like(l_s)
        acc_s[...] = jnp.zeros_like(acc_s)

        def rows(t):
            return pl.ds(pl.multiple_of((i - t) * tq, tq), tq)

        def stage_a(t, slot):
            k = k_ref[rows(t), :]
            for hh in range(2):
                z_s[slot, hh] = _nt(qh_s[hh], k)

        def stage_b(t, slot, diag):
            for hh in range(2):
                s = z_s[slot, hh] + cq_s[hh] - ck_ref[hh, :, rows(t)]
                if diag:
                    s = jnp.where(col <= row, s, NEG)
                m = m_s[hh]
                half = jnp.maximum(s[:, :128], s[:, 128:])
                m_new = jnp.maximum(m, jnp.max(half, axis=1, keepdims=True))
                alpha = jnp.exp(m - m_new)
                p = jnp.exp(s - _lanes2(m_new)).astype(BF16)
                l_s[hh] = alpha * l_s[hh] + _nn(p, ones)
                m_s[hh] = m_new
                al_s[hh] = alpha
                p_s[hh] = p

        def stage_c(t):
            v = v_ref[rows(t), :]
            for hh in range(2):
                acc_s[hh] = al_s[hh] * acc_s[hh] + _nn(p_s[hh], v)

        _pipeline3(_fox_live_blocks(i, qh_s, kn_s, cq_ref, cke_ref), stage_a, stage_b, stage_c, False)
        l0, l1 = l_s[0], l_s[1]
        o_ref[...] = jnp.where(lane < HEAD_DIM, acc_s[0] / l0, acc_s[1] / l1)
        lse_ref[0] = _rows_to_lanes(m_s[0] + jnp.log(l0), row == col)
        lse_ref[1] = _rows_to_lanes(m_s[1] + jnp.log(l1), row == col)

    return pl.pallas_call(
        body,
        name="fox_fwd",
        grid=(4, nq),
        in_specs=[
            pl.BlockSpec((tq, 128), lambda p, i: (i, p)),
            pl.BlockSpec((T, 128), lambda p, i: (0, 4 + p)),
            pl.BlockSpec((T, 128), lambda p, i: (0, 8 + p)),
            pl.BlockSpec((2, 1, tq), lambda p, i: (p, 0, i)),
            pl.BlockSpec((2, 1, T), lambda p, i: (p, 0, 0)),
            pl.BlockSpec((2, 1, nq), lambda p, i: (p, 0, 0)),
        ],
        out_specs=[
            pl.BlockSpec((tq, 128), lambda p, i: (i, p)),
            pl.BlockSpec((2, 1, tq), lambda p, i: (p, 0, i)),
        ],
        out_shape=[
            jax.ShapeDtypeStruct((T, FOX_W), F32),
            jax.ShapeDtypeStruct((N_FOX, 1, T), F32),
        ],
        scratch_shapes=[
            pltpu.VMEM((2, tq, 128), BF16),
            pltpu.VMEM((2, tq, tq), F32),
            pltpu.VMEM((2, 2, tq, tq), F32),
            pltpu.VMEM((2, tq, tq), BF16),
            pltpu.VMEM((2, tq, 128), F32),
            pltpu.VMEM((2, tq, 128), F32),
            pltpu.VMEM((2, tq, 128), F32),
            pltpu.VMEM((2, tq, 128), F32),
            pltpu.VMEM((2, 8, 128), F32),
        ],
        compiler_params=_cparams(("arbitrary", "arbitrary")),
    )(proj, proj, proj, c_col, c_row, c_ends)


def _sb_logs(zn, strict):
    e = jnp.exp2(jnp.abs(zn) * (-LOG2E))
    L = jnp.minimum(zn, 0.0) - jnp.log(1.0 + e)
    G = L - zn
    if strict is not None:
        L = jnp.where(strict, L, 0.0)
    return L, G


def _sb_fwd(proj, tq, gather=()):
    T = proj.shape[0]
    ng = len(gather)
    nq = T // tq

    def body(*refs):
        q_ref, k_ref, v_ref = refs[:3]
        o_ref, ltot_ref = refs[3 + ng:5 + ng]
        qh_s, z_s, g_s, tot_s, run_s, acc_s = refs[5 + 2 * ng:11 + 2 * ng]
        i = pl.program_id(1)
        if ng:
            pair = pl.program_id(0)
            exchange = _Gather(refs[3:3 + ng], refs[5 + ng:5 + 2 * ng], *refs[11 + 2 * ng:])

            @pl.when((pair == 0) & (i == 0))
            def _():
                exchange.start()

            @pl.when((pair == 1) & (i == 0))
            def _():
                exchange.forward()

        lane = lax.broadcasted_iota(jnp.int32, (1, 128), 1)
        row = lax.broadcasted_iota(jnp.int32, (tq, tq), 0)
        col = lax.broadcasted_iota(jnp.int32, (tq, tq), 1)
        strict = col < row
        later = jnp.where(row > col, 1.0, 0.0).astype(BF16)
        q = q_ref[...]
        for hh in range(2):
            qh_s[hh] = -_head_q(q, hh, lane)[0]
        run_s[...] = jnp.zeros_like(run_s)
        acc_s[...] = jnp.zeros_like(acc_s)

        def rows(t):
            return pl.ds(pl.multiple_of((i - t) * tq, tq), tq)

        def stage_a(t, slot):
            k = k_ref[rows(t), :]
            for hh in range(2):
                z_s[slot, hh] = _nt(qh_s[hh], k)

        def stage_b(t, slot, diag):
            for hh in range(2):
                L, g = _sb_logs(z_s[slot, hh], strict if diag else None)
                if diag:
                    g = jnp.where(strict, g, NEG)
                after = _split_dot(L, later, SB_SUM_TERMS)
                g_s[hh] = g + after
                first = L[:, 0:1]
                if SB_SUM_TERMS == 1:
                    first = first.astype(BF16).astype(F32)
                tot_s[hh] = jnp.broadcast_to(after[:, 0:1] + first, (tq, 128))

        def stage_c(t):
            v = v_ref[rows(t), :]
            for hh in range(2):
                run = run_s[hh]
                a = jnp.exp(g_s[hh] + _lanes2(run))
                acc_s[hh] += _nn(a.astype(BF16), v)
                run_s[hh] = run + tot_s[hh]

        _pipeline3(i + 1, stage_a, stage_b, stage_c, False)
        ltot_ref[0] = _rows_to_lanes(run_s[0], row == col)
        ltot_ref[1] = _rows_to_lanes(run_s[1], row == col)
        o_ref[...] = jnp.where(lane < HEAD_DIM, acc_s[0], acc_s[1])
        if ng:
            @pl.when((pair == 3) & (i == nq - 1))
            def _():
                exchange.finish()

    res = pl.pallas_call(
        body,
        name="sb_fwd",
        grid=(4, nq),
        in_specs=[
            pl.BlockSpec((tq, 128), lambda p, i: (i, 12 + p)),
            pl.BlockSpec((T, 128), lambda p, i: (0, 16 + p)),
            pl.BlockSpec((T, 128), lambda p, i: (0, 20 + p)),
        ] + [_ANY] * ng,
        out_specs=[
            pl.BlockSpec((tq, 128), lambda p, i: (i, p)),
            pl.BlockSpec((2, 1, tq), lambda p, i: (p, 0, i)),
        ] + [_ANY] * ng,
        out_shape=[
            jax.ShapeDtypeStruct((T, FOX_W), F32),
            jax.ShapeDtypeStruct((N_FOX, 1, T), F32),
        ] + _gathered_shapes(gather),
        scratch_shapes=[
            pltpu.VMEM((2, tq, 128), BF16),
            pltpu.VMEM((2, 2, tq, tq), F32),
            pltpu.VMEM((2, tq, tq), F32),
            pltpu.VMEM((2, tq, 128), F32),
            pltpu.VMEM((2, tq, 128), F32),
            pltpu.VMEM((2, tq, 128), F32),
        ] + (_comm_sems(ng) if ng else []),
        compiler_params=_cparams(("arbitrary", "arbitrary")),
    )(proj, proj, proj, *gather)
    res = list(res)
    return res[0], res[1], res[2:]


def _post_attn_fwd(fox_o, sb_o, gf, gs, w_out, x, tm):
    T, D = x.shape

    def body(f_ref, s_ref, gf_ref, gs_ref, w_ref, x_ref, x1_ref, mix_ref):
        f = f_ref[...]
        s = s_ref[...]
        mix_ref[:, :FOX_W] = (f * _rstd(f) * gf_ref[...]).astype(BF16)
        mix_ref[:, FOX_W:] = (s * _rstd(s) * gs_ref[...]).astype(BF16)
        x1_ref[...] = x_ref[...] + _nn(mix_ref[...], w_ref[...])

    return pl.pallas_call(
        body,
        name="post_attn_fwd",
        grid=(T // tm,),
        in_specs=[
            pl.BlockSpec((tm, FOX_W), lambda i: (i, 0)),
            pl.BlockSpec((tm, FOX_W), lambda i: (i, 0)),
            pl.BlockSpec((1, FOX_W), lambda i: (0, 0)),
            pl.BlockSpec((1, FOX_W), lambda i: (0, 0)),
            pl.BlockSpec((D, D), lambda i: (0, 0)),
            pl.BlockSpec((tm, D), lambda i: (i, 0)),
        ],
        out_specs=[
            pl.BlockSpec((tm, D), lambda i: (i, 0)),
            pl.BlockSpec((tm, D), lambda i: (i, 0)),
        ],
        out_shape=[jax.ShapeDtypeStruct((T, D), F32), jax.ShapeDtypeStruct((T, D), BF16)],
        compiler_params=_cparams(("arbitrary",)),
    )(fox_o, sb_o, gf, gs, w_out, x)


def _mem_kv_fwd(mem, gm, w_mkv):
    M, D = mem.shape
    N = w_mkv.shape[1]

    def body(mem_ref, g_ref, w_ref, m_ref, kv_ref):
        mv = mem_ref[...]
        m = (mv * _rstd(mv) * g_ref[...]).astype(BF16)
        m_ref[...] = m
        for n0 in range(0, N, 512):
            kv_ref[:, n0:n0 + 512] = _nn(m, w_ref[:, n0:n0 + 512]).astype(BF16)

    return pl.pallas_call(
        body,
        name="mem_kv_fwd",
        out_shape=[jax.ShapeDtypeStruct((M, D), BF16), jax.ShapeDtypeStruct((M, N), BF16)],
        compiler_params=_cparams(),
    )(mem, gm, w_mkv)


def _xattn_probs(qb, kv, h):
    k = kv[:, h * MEM_HD:(h + 1) * MEM_HD]
    s = _nt(qb[:, h * MEM_HD:(h + 1) * MEM_HD], k) * (MEM_HD ** -0.5)
    s = s - jnp.max(s, axis=1, keepdims=True)
    p = jnp.exp(s)
    return p / jnp.sum(p, axis=1, keepdims=True)


def _xattn_fwd(x1, g2, w_mq, kv, w_mo, tm):
    T, D = x1.shape
    M = kv.shape[0]

    def body(x_ref, g_ref, wq_ref, kv_ref, wo_ref, x2_ref, h_ref, q_ref, om_ref):
        xv = x_ref[...]
        h = (xv * _rstd(xv) * g_ref[...]).astype(BF16)
        h_ref[...] = h
        q_ref[...] = _nn(h, wq_ref[...]).astype(BF16)
        qb = q_ref[...]
        kvv = kv_ref[...]
        for hd in range(N_MEM_HEADS):
            p = _xattn_probs(qb, kvv, hd)
            v = kvv[:, D + hd * MEM_HD:D + (hd + 1) * MEM_HD]
            om_ref[:, hd * MEM_HD:(hd + 1) * MEM_HD] = _nn(p.astype(BF16), v).astype(BF16)
        x2_ref[...] = xv + _nn(om_ref[...], wo_ref[...])

    return pl.pallas_call(
        body,
        name="xattn_fwd",
        grid=(T // tm,),
        in_specs=[
            pl.BlockSpec((tm, D), lambda i: (i, 0)),
            pl.BlockSpec((1, D), lambda i: (0, 0)),
            pl.BlockSpec((D, D), lambda i: (0, 0)),
            pl.BlockSpec((M, 2 * D), lambda i: (0, 0)),
            pl.BlockSpec((D, D), lambda i: (0, 0)),
        ],
        out_specs=[pl.BlockSpec((tm, D), lambda i: (i, 0))] * 4,
        out_shape=[jax.ShapeDtypeStruct((T, D), F32)] + [jax.ShapeDtypeStruct((T, D), BF16)] * 3,
        compiler_params=_cparams(("arbitrary",)),
    )(x1, g2, w_mq, kv, w_mo)


def _conv_taps(ext_ref, tm, back):
    if back:
        return ext_ref[pl.ds(6, tm), :], ext_ref[pl.ds(7, tm), :], ext_ref[pl.ds(8, tm), :]
    return ext_ref[pl.ds(0, tm), :], ext_ref[pl.ds(1, tm), :], ext_ref[pl.ds(2, tm), :]


def _ffn_fwd(x2, g3, w_up, conv_w, conv_b, w_down, tm):
    T, D = x2.shape
    fc = FF_CHUNK
    nj = D_FF // fc

    def body(x_ref, g_ref, wg_ref, wv_ref, cwg_ref, cwv_ref, cbg_ref, cbv_ref, wd_ref,
             x3_ref, h_ref, ug_ref, uv_ref, yg_ref, yv_ref, a_ref, acc_ref, carry_ref, ext_ref):
        i = pl.program_id(0)
        j = pl.program_id(1)

        @pl.when(j == 0)
        def _():
            xv = x_ref[...]
            h_ref[...] = (xv * _rstd(xv) * g_ref[...]).astype(BF16)
            acc_ref[...] = xv

        @pl.when(i == 0)
        def _():
            carry_ref[j] = jnp.zeros((2, 8, fc), F32)

        h = h_ref[...]
        halves = []
        for part, (w_ref, cw_ref, cb_ref, u_ref, y_ref) in enumerate(
                ((wg_ref, cwg_ref, cbg_ref, ug_ref, yg_ref), (wv_ref, cwv_ref, cbv_ref, uv_ref, yv_ref))):
            u = _nn(h, w_ref[...])
            u_ref[...] = u.astype(BF16)
            ext = ext_ref.at[part]
            ext[pl.ds(0, 8), :] = carry_ref[j, part]
            ext[pl.ds(8, tm), :] = u
            carry_ref[j, part] = u[tm - 8:, :]
            u2, u1, u0 = _conv_taps(ext, tm, True)
            cw = cw_ref[...]
            y = cb_ref[...] + cw[0:1] * u2 + cw[1:2] * u1 + cw[2:3] * u0
            y_ref[...] = y.astype(BF16)
            halves.append(y)
        gate, val = halves
        a = (gate * jax.nn.sigmoid(gate) * val).astype(BF16)
        a_ref[...] = a
        acc_ref[...] += _nn(a, wd_ref[...])

        @pl.when(j == nj - 1)
        def _():
            x3_ref[...] = acc_ref[...]

    return pl.pallas_call(
        body,
        name="ffn_fwd",
        grid=(T // tm, nj),
        in_specs=[
            pl.BlockSpec((tm, D), lambda i, j: (i, 0)),
            pl.BlockSpec((1, D), lambda i, j: (0, 0)),
            pl.BlockSpec((D, fc), lambda i, j: (0, j)),
            pl.BlockSpec((D, fc), lambda i, j: (0, nj + j)),
            pl.BlockSpec((3, fc), lambda i, j: (0, j)),
            pl.BlockSpec((3, fc), lambda i, j: (0, nj + j)),
            pl.BlockSpec((1, fc), lambda i, j: (0, j)),
            pl.BlockSpec((1, fc), lambda i, j: (0, nj + j)),
            pl.BlockSpec((fc, D), lambda i, j: (j, 0)),
        ],
        out_specs=[
            pl.BlockSpec((tm, D), lambda i, j: (i, 0)),
            pl.BlockSpec((tm, D), lambda i, j: (i, 0)),
        ] + [pl.BlockSpec((tm, fc), lambda i, j: (i, j))] * 5,
        out_shape=[
            jax.ShapeDtypeStruct((T, D), F32),
            jax.ShapeDtypeStruct((T, D), BF16),
        ] + [jax.ShapeDtypeStruct((T, D_FF), BF16)] * 5,
        scratch_shapes=[
            pltpu.VMEM((tm, D), F32),
            pltpu.VMEM((nj, 2, 8, fc), F32),
            pltpu.VMEM((2, tm + 8, fc), F32),
        ],
        compiler_params=_cparams(("arbitrary", "arbitrary")),
    )(x2, g3, w_up, w_up, conv_w, conv_w, conv_b, conv_b, w_down)


def _loss_head(x3, gfin, target, tm):
    T, D = x3.shape

    def body(x_ref, g_ref, t_ref, dx_ref, loss_ref, dg_ref):
        i = pl.program_id(0)

        @pl.when(i == 0)
        def _():
            loss_ref[...] = jnp.zeros_like(loss_ref)
            dg_ref[...] = jnp.zeros_like(dg_ref)

        xv = x_ref[...]
        g = g_ref[...]
        r = _rstd(xv)
        xhat = xv * r
        err = xhat * g - t_ref[...]
        part = jnp.sum(jnp.sum(err * err, axis=1, keepdims=True), axis=0, keepdims=True) * (0.5 / D)
        loss_ref[...] += jnp.broadcast_to(part, loss_ref.shape)
        dy = err * (1.0 / D)
        dg_ref[...] += jnp.sum(dy * xhat, axis=0, keepdims=True)
        dxhat = dy * g
        dx_ref[...] = r * (dxhat - xhat * jnp.mean(dxhat * xhat, axis=-1, keepdims=True))

    return pl.pallas_call(
        body,
        name="loss_head",
        grid=(T // tm,),
        in_specs=[
            pl.BlockSpec((tm, D), lambda i: (i, 0)),
            pl.BlockSpec((1, D), lambda i: (0, 0)),
            pl.BlockSpec((tm, D), lambda i: (i, 0)),
        ],
        out_specs=[
            pl.BlockSpec((tm, D), lambda i: (i, 0)),
            pl.BlockSpec((8, 128), lambda i: (0, 0)),
            pl.BlockSpec((1, D), lambda i: (0, 0)),
        ],
        out_shape=[
            jax.ShapeDtypeStruct((T, D), F32),
            jax.ShapeDtypeStruct((8, 128), F32),
            jax.ShapeDtypeStruct((1, D), F32),
        ],
        compiler_params=_cparams(("arbitrary",)),
    )(x3, gfin, target)


def _ffn_bwd(dx3, x2, g3, ug, uv, yg, yv, conv_w, w_down, w_up, tm):
    T, D = x2.shape
    fc = FF_CHUNK
    nj = D_FF // fc
    nt = T // tm

    def rev(i):
        return nt - 1 - i

    def body(dx3_ref, x_ref, g_ref, ug_ref, uv_ref, yg_ref, yv_ref, cwg_ref, cwv_ref,
             wd_ref, wug_ref, wuv_ref,
             dx2_ref, dug_ref, duv_ref, dg_ref, dcg_ref, dcv_ref,
             acc_ref, carry_ref, ext_ref):
        i = pl.program_id(0)
        j = pl.program_id(1)
        cols = pl.ds(pl.multiple_of(j * fc, fc), fc)

        @pl.when(j == 0)
        def _():
            acc_ref[...] = jnp.zeros_like(acc_ref)

        @pl.when((i == 0) & (j == 0))
        def _():
            dg_ref[...] = jnp.zeros_like(dg_ref)
            dcg_ref[...] = jnp.zeros_like(dcg_ref)
            dcv_ref[...] = jnp.zeros_like(dcv_ref)

        @pl.when(i == 0)
        def _():
            carry_ref[j] = jnp.zeros((2, 8, fc), F32)

        da = _nt(dx3_ref[...].astype(BF16), wd_ref[...])
        gate = yg_ref[...].astype(F32)
        val = yv_ref[...].astype(F32)
        sig = jax.nn.sigmoid(gate)
        silu = gate * sig
        dys = (da * val * (sig * (1.0 + gate * (1.0 - sig))), da * silu)
        for part, (dy, u_ref, cw_ref, du_ref, wu_ref, dc_ref) in enumerate(
                ((dys[0], ug_ref, cwg_ref, dug_ref, wug_ref, dcg_ref),
                 (dys[1], uv_ref, cwv_ref, duv_ref, wuv_ref, dcv_ref))):
            ext = ext_ref.at[part]
            ext[pl.ds(0, tm), :] = dy
            ext[pl.ds(tm, 8), :] = carry_ref[j, part]
            carry_ref[j, part] = dy[:8, :]
            d0, d1, d2 = _conv_taps(ext, tm, False)
            u = u_ref[...].astype(F32)
            upd = jnp.concatenate([
                jnp.sum(u * d2, axis=0, keepdims=True),
                jnp.sum(u * d1, axis=0, keepdims=True),
                jnp.sum(u * d0, axis=0, keepdims=True),
                jnp.sum(d0, axis=0, keepdims=True),
                jnp.zeros((4, fc), F32)], axis=0)
            dc_ref[:, cols] += upd
            cw = cw_ref[...]
            du = (cw[2:3] * d0 + cw[1:2] * d1 + cw[0:1] * d2).astype(BF16)
            du_ref[...] = du
            acc_ref[...] += _nt(du, wu_ref[...])

        @pl.when(j == nj - 1)
        def _():
            dx, dg = _norm_bwd(x_ref[...], g_ref[...], acc_ref[...])
            dx2_ref[...] = dx3_ref[...] + dx
            dg_ref[...] += dg

    return pl.pallas_call(
        body,
        name="ffn_bwd",
        grid=(nt, nj),
        in_specs=[
            pl.BlockSpec((tm, D), lambda i, j: (rev(i), 0)),
            pl.BlockSpec((tm, D), lambda i, j: (rev(i), 0)),
            pl.BlockSpec((1, D), lambda i, j: (0, 0)),
            pl.BlockSpec((tm, fc), lambda i, j: (rev(i), j)),
            pl.BlockSpec((tm, fc), lambda i, j: (rev(i), j)),
            pl.BlockSpec((tm, fc), lambda i, j: (rev(i), j)),
            pl.BlockSpec((tm, fc), lambda i, j: (rev(i), j)),
            pl.BlockSpec((3, fc), lambda i, j: (0, j)),
            pl.BlockSpec((3, fc), lambda i, j: (0, nj + j)),
            pl.BlockSpec((fc, D), lambda i, j: (j, 0)),
            pl.BlockSpec((D, fc), lambda i, j: (0, j)),
            pl.BlockSpec((D, fc), lambda i, j: (0, nj + j)),
        ],
        out_specs=[
            pl.BlockSpec((tm, D), lambda i, j: (rev(i), 0)),
            pl.BlockSpec((tm, fc), lambda i, j: (rev(i), j)),
            pl.BlockSpec((tm, fc), lambda i, j: (rev(i), j)),
            pl.BlockSpec((1, D), lambda i, j: (0, 0)),
            pl.BlockSpec((8, D_FF), lambda i, j: (0, 0)),
            pl.BlockSpec((8, D_FF), lambda i, j: (0, 0)),
        ],
        out_shape=[
            jax.ShapeDtypeStruct((T, D), F32),
            jax.ShapeDtypeStruct((T, D_FF), BF16),
            jax.ShapeDtypeStruct((T, D_FF), BF16),
            jax.ShapeDtypeStruct((1, D), F32),
            jax.ShapeDtypeStruct((8, D_FF), F32),
            jax.ShapeDtypeStruct((8, D_FF), F32),
        ],
        scratch_shapes=[
            pltpu.VMEM((tm, D), F32),
            pltpu.VMEM((nj, 2, 8, fc), F32),
            pltpu.VMEM((2, tm + 8, fc), F32),
        ],
        compiler_params=_cparams(("arbitrary", "arbitrary")),
    )(dx3, x2, g3, ug, uv, yg, yv, conv_w, conv_w, w_down, w_up, w_up)


def _xattn_bwd(dx2, x1, g2, qb, kv, w_mo, w_mq, tm):
    T, D = x1.shape
    M = kv.shape[0]

    def body(dx2_ref, x_ref, g_ref, q_ref, kv_ref, wo_ref, wq_ref, dx1_ref, dq_ref, dkv_ref, dg_ref):
        i = pl.program_id(0)

        @pl.when(i == 0)
        def _():
            dkv_ref[...] = jnp.zeros_like(dkv_ref)
            dg_ref[...] = jnp.zeros_like(dg_ref)

        dxv = dx2_ref[...]
        dom = _nt(dxv.astype(BF16), wo_ref[...]).astype(BF16)
        qb_ = q_ref[...]
        kvv = kv_ref[...]
        for hd in range(N_MEM_HEADS):
            sl = slice(hd * MEM_HD, (hd + 1) * MEM_HD)
            vsl = slice(D + hd * MEM_HD, D + (hd + 1) * MEM_HD)
            p = _xattn_probs(qb_, kvv, hd)
            dp = _nt(dom[:, sl], kvv[:, vsl])
            ds = (p * (dp - jnp.sum(p * dp, axis=1, keepdims=True)) * (MEM_HD ** -0.5)).astype(BF16)
            dq_ref[:, sl] = _nn(ds, kvv[:, sl]).astype(BF16)
            dkv_ref[:, sl] += _tn(ds, qb_[:, sl])
            dkv_ref[:, vsl] += _tn(p.astype(BF16), dom[:, sl])
        dh = _nt(dq_ref[...], wq_ref[...])
        dx, dg = _norm_bwd(x_ref[...], g_ref[...], dh)
        dx1_ref[...] = dxv + dx
        dg_ref[...] += dg

    return pl.pallas_call(
        body,
        name="xattn_bwd",
        grid=(T // tm,),
        in_specs=[
            pl.BlockSpec((tm, D), lambda i: (i, 0)),
            pl.BlockSpec((tm, D), lambda i: (i, 0)),
            pl.BlockSpec((1, D), lambda i: (0, 0)),
            pl.BlockSpec((tm, D), lambda i: (i, 0)),
            pl.BlockSpec((M, 2 * D), lambda i: (0, 0)),
            pl.BlockSpec((D, D), lambda i: (0, 0)),
            pl.BlockSpec((D, D), lambda i: (0, 0)),
        ],
        out_specs=[
            pl.BlockSpec((tm, D), lambda i: (i, 0)),
            pl.BlockSpec((tm, D), lambda i: (i, 0)),
            pl.BlockSpec((M, 2 * D), lambda i: (0, 0)),
            pl.BlockSpec((1, D), lambda i: (0, 0)),
        ],
        out_shape=[
            jax.ShapeDtypeStruct((T, D), F32),
            jax.ShapeDtypeStruct((T, D), BF16),
            jax.ShapeDtypeStruct((M, 2 * D), F32),
            jax.ShapeDtypeStruct((1, D), F32),
        ],
        compiler_params=_cparams(("arbitrary",)),
    )(dx2, x1, g2, qb, kv, w_mo, w_mq)


def _mem_kv_bwd(mem, gm, mb, dkv, w_mkv):
    M, D = mem.shape
    N = dkv.shape[1]

    def body(mem_ref, g_ref, m_ref, dkv_ref, w_ref, dw_ref, dg_ref):
        dkvb = dkv_ref[...].astype(BF16)
        for n0 in range(0, N, 512):
            dw_ref[:, n0:n0 + 512] = _tn(m_ref[...], dkvb[:, n0:n0 + 512]).astype(BF16)
        dm = _nt(dkvb, w_ref[...])
        mv = mem_ref[...]
        dg_ref[...] = jnp.sum(dm * (mv * _rstd(mv)), axis=0, keepdims=True)

    return pl.pallas_call(
        body,
        name="mem_kv_bwd",
        out_shape=[jax.ShapeDtypeStruct((D, N), BF16), jax.ShapeDtypeStruct((1, D), F32)],
        compiler_params=_cparams(),
    )(mem, gm, mb, dkv, w_mkv)


def _post_attn_bwd(dx1, fox_o, sb_o, gf, gs, w_out, tm):
    T, D = dx1.shape

    def body(dx_ref, f_ref, s_ref, gf_ref, gs_ref, w_ref, df_ref, ds_ref, dgf_ref, dgs_ref):
        i = pl.program_id(0)

        @pl.when(i == 0)
        def _():
            dgf_ref[...] = jnp.zeros_like(dgf_ref)
            dgs_ref[...] = jnp.zeros_like(dgs_ref)

        dmix = _nt(dx_ref[...].astype(BF16), w_ref[...])
        d, dg = _norm_bwd(f_ref[...], gf_ref[...], dmix[:, :FOX_W])
        df_ref[...] = d
        dgf_ref[...] += dg
        d, dg = _norm_bwd(s_ref[...], gs_ref[...], dmix[:, FOX_W:])
        ds_ref[...] = d
        dgs_ref[...] += dg

    return pl.pallas_call(
        body,
        name="post_attn_bwd",
        grid=(T // tm,),
        in_specs=[
            pl.BlockSpec((tm, D), lambda i: (i, 0)),
            pl.BlockSpec((tm, FOX_W), lambda i: (i, 0)),
            pl.BlockSpec((tm, FOX_W), lambda i: (i, 0)),
            pl.BlockSpec((1, FOX_W), lambda i: (0, 0)),
            pl.BlockSpec((1, FOX_W), lambda i: (0, 0)),
            pl.BlockSpec((D, D), lambda i: (0, 0)),
        ],
        out_specs=[
            pl.BlockSpec((tm, FOX_W), lambda i: (i, 0)),
            pl.BlockSpec((tm, FOX_W), lambda i: (i, 0)),
            pl.BlockSpec((1, FOX_W), lambda i: (0, 0)),
            pl.BlockSpec((1, FOX_W), lambda i: (0, 0)),
        ],
        out_shape=[
            jax.ShapeDtypeStruct((T, FOX_W), F32),
            jax.ShapeDtypeStruct((T, FOX_W), F32),
            jax.ShapeDtypeStruct((1, FOX_W), F32),
            jax.ShapeDtypeStruct((1, FOX_W), F32),
        ],
        compiler_params=_cparams(("arbitrary",)),
    )(dx1, fox_o, sb_o, gf, gs, w_out)


def _sb_bwd(proj, ltot, d_o, tq, scatter=()):
    T = proj.shape[0]
    ns = len(scatter)
    nq = T // tq

    def body(*refs):
        q_ref, k_ref, v_ref, lt_ref, do_ref = refs[:5]
        dq_ref, dk_ref, dv_ref = refs[5 + ns:8 + ns]
        qh_s, doh_s, lt_s, z_s, da_s, ab_s, dzb_s, run_s, runw_s, dq_s = refs[8 + 2 * ns:18 + 2 * ns]
        i = pl.program_id(1)
        if ns:
            pair = pl.program_id(0)
            exchange = _Scatter(refs[5:5 + ns], refs[8 + ns:8 + 2 * ns], *refs[18 + 2 * ns:])

            @pl.when((pair == 0) & (i == 0))
            def _():
                exchange.start()

        @pl.when(i == 0)
        def _():
            dk_ref[...] = jnp.zeros_like(dk_ref)
            dv_ref[...] = jnp.zeros_like(dv_ref)

        lane = lax.broadcasted_iota(jnp.int32, (1, 128), 1)
        row = lax.broadcasted_iota(jnp.int32, (tq, tq), 0)
        col = lax.broadcasted_iota(jnp.int32, (tq, tq), 1)
        strict = col < row
        upto = jnp.where(row <= col, 1.0, 0.0).astype(BF16)
        before = jnp.where(row < col, 1.0, 0.0).astype(BF16)
        q = q_ref[...]
        dov = do_ref[...]
        for hh in range(2):
            qh, hmask = _head_q(q, hh, lane)
            qh_s[hh] = -qh
            doh_s[hh] = jnp.where(hmask, dov, 0.0).astype(BF16)
            lt_s[hh] = jnp.broadcast_to(_lanes_to_rows(lt_ref[hh], row == col), (tq, 128))
        run_s[...] = jnp.zeros_like(run_s)
        runw_s[...] = jnp.zeros_like(runw_s)
        dq_s[...] = jnp.zeros_like(dq_s)

        def rows(t):
            return pl.ds(pl.multiple_of(t * tq, tq), tq)

        def stage_a(t, slot):
            k = k_ref[rows(t), :]
            v = v_ref[rows(t), :]
            for hh in range(2):
                z_s[slot, hh] = _nt(qh_s[hh], k)
                da_s[slot, hh] = _nt(doh_s[hh], v)

        def stage_b(t, slot, diag):
            for hh in range(2):
                L, g = _sb_logs(z_s[slot, hh], strict if diag else None)
                upto_s = _split_dot(L, upto, SB_SUM_TERMS)
                run = run_s[hh]
                arg = (g + _lanes2(lt_s[hh] - run)) - upto_s
                if diag:
                    arg = jnp.where(strict, arg, NEG)
                a = jnp.exp(arg)
                w = a * da_s[slot, hh]
                w_before = _split_dot(w, before, SB_SUM_TERMS)
                run_w = runw_s[hh]
                d_keep = w_before + _lanes2(run_w)
                beta = jnp.exp(g)
                ndz = beta * (w + d_keep) - w
                if diag:
                    ndz = jnp.where(strict, ndz, 0.0)
                dzb_s[hh] = ndz.astype(BF16)
                ab_s[hh] = a.astype(BF16)
                run_s[hh] = run + jnp.broadcast_to(upto_s[:, tq - 1:tq], (tq, 128))
                runw_s[hh] = run_w + jnp.broadcast_to(w_before[:, tq - 1:tq] + w[:, tq - 1:tq], (tq, 128))

        def stage_c(t):
            k = k_ref[rows(t), :]
            dk_blk = None
            dv_blk = None
            for hh in range(2):
                dzb = dzb_s[hh]
                dq_s[hh] += _nn(dzb, k)
                dk_h = _tn(dzb, qh_s[hh])
                dv_h = _tn(ab_s[hh], doh_s[hh])
                dk_blk = dk_h if dk_blk is None else dk_blk + dk_h
                dv_blk = dv_h if dv_blk is None else dv_blk + dv_h
            dk_ref[rows(t), :] += dk_blk
            dv_ref[rows(t), :] += dv_blk

        _pipeline3(i + 1, stage_a, stage_b, stage_c, True)
        dq_ref[...] = (jnp.where(lane < HEAD_DIM, dq_s[0], dq_s[1]) * -(HEAD_DIM ** -0.5)).astype(BF16)
        if ns:
            @pl.when((pair == 3) & (i == nq - 1))
            def _():
                exchange.finish()

    res = pl.pallas_call(
        body,
        name="sb_bwd",
        grid=(4, nq),
        in_specs=[
            pl.BlockSpec((tq, 128), lambda p, i: (i, 12 + p)),
            pl.BlockSpec((T, 128), lambda p, i: (0, 16 + p)),
            pl.BlockSpec((T, 128), lambda p, i: (0, 20 + p)),
            pl.BlockSpec((2, 1, tq), lambda p, i: (p, 0, i)),
            pl.BlockSpec((tq, 128), lambda p, i: (i, p)),
        ] + [_ANY] * ns,
        out_specs=[
            pl.BlockSpec((tq, 128), lambda p, i: (i, p)),
            pl.BlockSpec((T, 128), lambda p, i: (0, p)),
            pl.BlockSpec((T, 128), lambda p, i: (0, p)),
        ] + [_ANY] * ns,
        out_shape=[
            jax.ShapeDtypeStruct((T, FOX_W), BF16),
            jax.ShapeDtypeStruct((T, FOX_W), F32),
            jax.ShapeDtypeStruct((T, FOX_W), F32),
        ] + [jax.ShapeDtypeStruct(b.shape, b.dtype) for b in scatter],
        scratch_shapes=[
            pltpu.VMEM((2, tq, 128), BF16),
            pltpu.VMEM((2, tq, 128), BF16),
            pltpu.VMEM((2, tq, 128), F32),
            pltpu.VMEM((2, 2, tq, tq), F32),
            pltpu.VMEM((2, 2, tq, tq), F32),
            pltpu.VMEM((2, tq, tq), BF16),
            pltpu.VMEM((2, tq, tq), BF16),
            pltpu.VMEM((2, tq, 128), F32),
            pltpu.VMEM((2, tq, 128), F32),
            pltpu.VMEM((2, tq, 128), F32),
        ] + (_comm_sems(ns) if ns else []),
        compiler_params=_cparams(("arbitrary", "arbitrary")),
    )(proj, proj, proj, ltot, d_o, *scatter)
    res = list(res)
    return (*res[:3], res[3:])


def _fox_bwd(proj, c_col, c_row, c_ends, lse, d_o, o, tq):
    T = proj.shape[0]
    nq = T // tq

    def body(q_ref, k_ref, v_ref, cq_ref, ck_ref, cke_ref, lse_ref, do_ref, o_ref,
             dq_ref, dk_ref, dv_ref, dck_ref, dcq_ref,
             qh_s, doh_s, delta_s, shift_s, z_s, dp_s, pb_s, dsb_s, rs_s, dq_s, kn_s):
        i = pl.program_id(1)
        lane = lax.broadcasted_iota(jnp.int32, (1, 128), 1)

        @pl.when(i == 0)
        def _():
            dk_ref[...] = jnp.zeros_like(dk_ref)
            dv_ref[...] = jnp.zeros_like(dv_ref)
            dck_ref[...] = jnp.zeros_like(dck_ref)
            _fox_key_norms(k_ref, kn_s, lane)

        row = lax.broadcasted_iota(jnp.int32, (tq, tq), 0)
        col = lax.broadcasted_iota(jnp.int32, (tq, tq), 1)
        q = q_ref[...]
        dov = do_ref[...]
        ov = o_ref[...]
        for hh in range(2):
            qh, hmask = _head_q(q, hh, lane)
            dohb = jnp.where(hmask, dov, 0.0).astype(BF16)
            qh_s[hh] = qh
            doh_s[hh] = dohb
            delta_s[hh] = jnp.broadcast_to(jnp.sum(dohb.astype(F32) * ov, axis=1, keepdims=True), (tq, tq))
            shift_s[hh] = jnp.broadcast_to(_lanes_to_rows(cq_ref[hh] - lse_ref[hh], row == col), (tq, tq))
        rs_s[...] = jnp.zeros_like(rs_s)
        dq_s[...] = jnp.zeros_like(dq_s)

        def rows(t):
            return pl.ds(pl.multiple_of((i - t) * tq, tq), tq)

        def stage_a(t, slot):
            k = k_ref[rows(t), :]
            v = v_ref[rows(t), :]
            for hh in range(2):
                z_s[slot, hh] = _nt(qh_s[hh], k)
                dp_s[slot, hh] = _nt(doh_s[hh], v)

        def stage_b(t, slot, diag):
            for hh in range(2):
                s = z_s[slot, hh] + shift_s[hh] - ck_ref[hh, :, rows(t)]
                if diag:
                    s = jnp.where(col <= row, s, NEG)
                p = jnp.exp(s)
                ds = p * (dp_s[slot, hh] - delta_s[hh])
                pb_s[hh] = p.astype(BF16)
                dsb_s[hh] = ds.astype(BF16)
                dck_ref[hh, :, rows(t)] += jnp.sum(ds, axis=0, keepdims=True)
                rs_s[hh] += jnp.sum(ds, axis=1, keepdims=True)

        def stage_c(t):
            k = k_ref[rows(t), :]
            dk_blk = None
            dv_blk = None
            for hh in range(2):
                dsb = dsb_s[hh]
                dq_s[hh] += _nn(dsb, k)
                dk_h = _tn(dsb, qh_s[hh])
                dv_h = _tn(pb_s[hh], doh_s[hh])
                dk_blk = dk_h if dk_blk is None else dk_blk + dk_h
                dv_blk = dv_h if dv_blk is None else dv_blk + dv_h
            dk_ref[rows(t), :] += dk_blk
            dv_ref[rows(t), :] += dv_blk

        _pipeline3(_fox_live_blocks(i, qh_s, kn_s, cq_ref, cke_ref), stage_a, stage_b, stage_c, False)
        dcq_ref[0] = _rows_to_lanes(rs_s[0], row == col)
        dcq_ref[1] = _rows_to_lanes(rs_s[1], row == col)
        dq_ref[...] = (jnp.where(lane < HEAD_DIM, dq_s[0], dq_s[1]) * (HEAD_DIM ** -0.5)).astype(BF16)

    return pl.pallas_call(
        body,
        name="fox_bwd",
        grid=(4, nq),
        in_specs=[
            pl.BlockSpec((tq, 128), lambda p, i: (i, p)),
            pl.BlockSpec((T, 128), lambda p, i: (0, 4 + p)),
            pl.BlockSpec((T, 128), lambda p, i: (0, 8 + p)),
            pl.BlockSpec((2, 1, tq), lambda p, i: (p, 0, i)),
            pl.BlockSpec((2, 1, T), lambda p, i: (p, 0, 0)),
            pl.BlockSpec((2, 1, nq), lambda p, i: (p, 0, 0)),
            pl.BlockSpec((2, 1, tq), lambda p, i: (p, 0, i)),
            pl.BlockSpec((tq, 128), lambda p, i: (i, p)),
            pl.BlockSpec((tq, 128), lambda p, i: (i, p)),
        ],
        out_specs=[
            pl.BlockSpec((tq, 128), lambda p, i: (i, p)),
            pl.BlockSpec((T, 128), lambda p, i: (0, p)),
            pl.BlockSpec((T, 128), lambda p, i: (0, p)),
            pl.BlockSpec((2, 1, T), lambda p, i: (p, 0, 0)),
            pl.BlockSpec((2, 1, tq), lambda p, i: (p, 0, i)),
        ],
        out_shape=[
            jax.ShapeDtypeStruct((T, FOX_W), BF16),
            jax.ShapeDtypeStruct((T, FOX_W), F32),
            jax.ShapeDtypeStruct((T, FOX_W), F32),
            jax.ShapeDtypeStruct((N_FOX, 1, T), F32),
            jax.ShapeDtypeStruct((N_FOX, 1, T), F32),
        ],
        scratch_shapes=[
            pltpu.VMEM((2, tq, 128), BF16),
            pltpu.VMEM((2, tq, 128), BF16),
            pltpu.VMEM((2, tq, tq), F32),
            pltpu.VMEM((2, tq, tq), F32),
            pltpu.VMEM((2, 2, tq, tq), F32),
            pltpu.VMEM((2, 2, tq, tq), F32),
            pltpu.VMEM((2, tq, tq), BF16),
            pltpu.VMEM((2, tq, tq), BF16),
            pltpu.VMEM((2, tq, 128), F32),
            pltpu.VMEM((2, tq, 128), F32),
            pltpu.VMEM((2, 8, 128), F32),
        ],
        compiler_params=_cparams(("arbitrary", "arbitrary")),
    )(proj, proj, proj, c_col, c_row, c_ends, lse, d_o, o)


def _forget_bwd(dcq, dck, xf, tc):
    H, T = xf.shape
    nc = T // tc

    def body(dcq_ref, dck_ref, xf_ref, dxf_ref, db_ref):
        row = lax.broadcasted_iota(jnp.int32, (tc, tc), 0)
        col = lax.broadcasted_iota(jnp.int32, (tc, tc), 1)
        from_here = jnp.where(row >= col, 1.0, 0.0).astype(BF16)

        def chunk(n, carry):
            run, db = carry
            cs = pl.multiple_of((nc - 1 - n) * tc, tc)
            dc = dcq_ref[:, pl.ds(cs, tc)] - dck_ref[:, pl.ds(cs, tc)]
            dlogf = _split_dot(dc, from_here, 3) + run
            xfv = xf_ref[:, pl.ds(cs, tc)]
            dxf = dlogf * jax.nn.sigmoid(-xfv)
            dxf_ref[:, pl.ds(cs, tc)] = dxf
            return dlogf[:, 0:1], db + jnp.sum(dxf, axis=1, keepdims=True)

        _, db = lax.fori_loop(0, nc, chunk, (jnp.zeros((H, 1), F32), jnp.zeros((H, 1), F32)))
        db_ref[...] = db

    return pl.pallas_call(
        body,
        name="forget_bwd",
        out_shape=[jax.ShapeDtypeStruct((H, T), F32), jax.ShapeDtypeStruct((H, 1), F32)],
        compiler_params=_cparams(),
    )(dcq, dck, xf)


def _inproj_bwd(dproj, w_in_pad, x, g1, dx1, tm, scatter=()):
    T, D = x.shape
    N = dproj.shape[1]
    ns = len(scatter)
    nt = T // tm

    def body(*refs):
        dp_ref, w_ref, x_ref, g_ref, dx1_ref = refs[:5]
        dx_ref, dg_ref = refs[5 + ns:7 + ns]
        i = pl.program_id(0)
        if ns:
            exchange = _Scatter(refs[5:5 + ns], refs[7 + ns:7 + 2 * ns], *refs[7 + 2 * ns:])

            @pl.when(i == 0)
            def _():
                exchange.start()

        @pl.when(i == 0)
        def _():
            dg_ref[...] = jnp.zeros_like(dg_ref)

        dh = _nt(dp_ref[...], w_ref[...])
        dx, dg = _norm_bwd(x_ref[...], g_ref[...], dh)
        dx_ref[...] = dx1_ref[...] + dx
        dg_ref[...] += dg
        if ns:
            @pl.when(i == nt - 1)
            def _():
                exchange.finish()

    res = pl.pallas_call(
        body,
        name="inproj_bwd",
        grid=(nt,),
        in_specs=[
            pl.BlockSpec((tm, N), lambda i: (i, 0)),
            pl.BlockSpec((D, N), lambda i: (0, 0)),
            pl.BlockSpec((tm, D), lambda i: (i, 0)),
            pl.BlockSpec((1, D), lambda i: (0, 0)),
            pl.BlockSpec((tm, D), lambda i: (i, 0)),
        ] + [_ANY] * ns,
        out_specs=[
            pl.BlockSpec((tm, D), lambda i: (i, 0)),
            pl.BlockSpec((1, D), lambda i: (0, 0)),
        ] + [_ANY] * ns,
        out_shape=[jax.ShapeDtypeStruct((T, D), F32), jax.ShapeDtypeStruct((1, D), F32)]
        + [jax.ShapeDtypeStruct(b.shape, b.dtype) for b in scatter],
        scratch_shapes=_comm_sems(ns) if ns else [],
        compiler_params=_cparams(("arbitrary",)),
    )(dproj, w_in_pad, x, g1, dx1, *scatter)
    res = list(res)
    return res[0], res[1], res[2:]


def _matmul_tn(a, b, name, cast_b=False):
    T, K = a.shape
    N = b.shape[1]
    bt = min(T, 512)
    bk = _tile_div(K, 1536)
    bn = _tile_div(N, 1536)
    nt = T // bt

    def body(a_ref, b_ref, o_ref, acc_ref):
        t = pl.program_id(2)

        @pl.when(t == 0)
        def _():
            acc_ref[...] = jnp.zeros_like(acc_ref)

        bv = b_ref[...]
        if cast_b:
            bv = bv.astype(BF16)
        acc_ref[...] += _tn(a_ref[...], bv)

        @pl.when(t == nt - 1)
        def _():
            o_ref[...] = acc_ref[...].astype(BF16)

    return pl.pallas_call(
        body,
        name=name,
        grid=(K // bk, N // bn, nt),
        in_specs=[
            pl.BlockSpec((bt, bk), lambda k, n, t: (t, k)),
            pl.BlockSpec((bt, bn), lambda k, n, t: (t, n)),
        ],
        out_specs=pl.BlockSpec((bk, bn), lambda k, n, t: (k, n)),
        out_shape=jax.ShapeDtypeStruct((K, N), BF16),
        scratch_shapes=[pltpu.VMEM((bk, bn), F32)],
        compiler_params=_cparams(("arbitrary", "arbitrary", "arbitrary")),
    )(a, b)


def _local_step(x, mem, target, p, tm, tq, late=None):
    T, D = x.shape
    w_in = p["w_in"]
    w_qkv = w_in[:, :QKV_W]
    w_f_t = w_in[:, QKV_W:].T
    w_in_pad = jnp.pad(w_in, ((0, 0), (0, IN_PAD - w_in.shape[1])))
    b_f = p["b_forget"].reshape(N_FOX, 1)

    proj, h1, xf, c = _inproj_fwd(x, p["attn_norm_g"], w_qkv, w_f_t, b_f, tm)
    c_col = c.reshape(N_FOX, 1, T)
    c_row = c.reshape(N_FOX, 1, T)
    c_ends = c[:, tq - 1::tq].reshape(N_FOX, 1, T // tq)
    fox_o, lse = _fox_fwd(proj, c_col, c_row, c_ends, tq)
    if late:
        sb_o, sb_ltot, gathered = _sb_fwd(proj, tq, gather=[late[n] for n in _LATE])
        p = dict(p, **{n: _gathered_full(n, gv) for n, gv in zip(_LATE, gathered)})
    else:
        sb_o, sb_ltot, _ = _sb_fwd(proj, tq)
    x1, mixed = _post_attn_fwd(fox_o, sb_o, p["fox_out_g"], p["sb_out_g"], p["w_out"], x, tm)
    mb, kv = _mem_kv_fwd(mem, p["mem_norm_g"], p["w_mkv"])
    x2, h2, qb, om = _xattn_fwd(x1, p["xattn_norm_g"], p["w_mq"], kv, p["w_mo"], tm)
    x3, h3, ug, uv, yg, yv, a = _ffn_fwd(
        x2, p["ffn_norm_g"], p["w_up"], p["conv_w"], p["conv_b"], p["w_down"], tm)
    dx3, loss_blk, d_final_g = _loss_head(x3, p["final_norm_g"], target, tm)

    g = {"final_norm_g": d_final_g}
    dx2, du_g, du_v, g["ffn_norm_g"], dc_g, dc_v = _ffn_bwd(
        dx3, x2, p["ffn_norm_g"], ug, uv, yg, yv, p["conv_w"], p["w_down"], p["w_up"], tm)
    g["w_down"] = _matmul_tn(a, dx3, "dw_down", cast_b=True)
    g["w_up"] = jnp.concatenate([_matmul_tn(h3, du_g, "dw_up_gate"), _matmul_tn(h3, du_v, "dw_up_val")], axis=1)
    dconv = jnp.concatenate([dc_g, dc_v], axis=1)
    g["conv_w"] = dconv[0:3]
    g["conv_b"] = dconv[3:4]
    dx1, dq_m, dkv, g["xattn_norm_g"] = _xattn_bwd(dx2, x1, p["xattn_norm_g"], qb, kv, p["w_mo"], p["w_mq"], tm)
    g["w_mo"] = _matmul_tn(om, dx2, "dw_mo", cast_b=True)
    g["w_mq"] = _matmul_tn(h2, dq_m, "dw_mq")
    g["w_mkv"], g["mem_norm_g"] = _mem_kv_bwd(mem, p["mem_norm_g"], mb, dkv, p["w_mkv"])
    d_fox, d_sb, g["fox_out_g"], g["sb_out_g"] = _post_attn_bwd(
        dx1, fox_o, sb_o, p["fox_out_g"], p["sb_out_g"], p["w_out"], tm)
    g["w_out"] = _matmul_tn(mixed, dx1, "dw_out", cast_b=True)
    if late:
        dq_s, dk_s, dv_s, parts = _sb_bwd(proj, sb_ltot, d_sb, tq, scatter=[_grad_blocks(n, g[n]) for n in _LATE])
        g["parts"] = dict(zip(_LATE, parts))
    else:
        dq_s, dk_s, dv_s, _ = _sb_bwd(proj, sb_ltot, d_sb, tq)
    dq_f, dk_f, dv_f, dck, dcq = _fox_bwd(proj, c_col, c_row, c_ends, lse, d_fox, fox_o, tq)
    dxf, db = _forget_bwd(dcq.reshape(N_FOX, T), dck.reshape(N_FOX, T), xf, min(T, 512))
    g["b_forget"] = db.reshape(1, N_FOX)
    dproj = jnp.concatenate([
        dq_f, dk_f.astype(BF16), dv_f.astype(BF16), dq_s, dk_s.astype(BF16), dv_s.astype(BF16),
        jnp.pad(dxf.T, ((0, 0), (0, IN_PAD - QKV_W - N_FOX))).astype(BF16)], axis=1)
    g["w_in"] = _matmul_tn(h1, dproj, "dw_in")[:, :w_in.shape[1]]
    if late:
        grad_x, g["attn_norm_g"], (g["parts"]["w_in"],) = _inproj_bwd(
            dproj, w_in_pad, x, p["attn_norm_g"], dx1, tm, scatter=[_grad_blocks("w_in", g["w_in"])])
    else:
        grad_x, g["attn_norm_g"], _ = _inproj_bwd(dproj, w_in_pad, x, p["attn_norm_g"], dx1, tm)
    return loss_blk, grad_x, g


def _mesh_pos():
    return lax.axis_index("x"), lax.axis_index("y"), lax.axis_index("c")


def _flip(pos, k):
    return tuple(1 - v if (k >> b) & 1 else v for v, b in zip(pos, (2, 1, 0)))


def _slot(pos):
    return 4 * pos[0] + 2 * pos[1] + pos[2]


_CHIPS = (4, 2, 6)


def _comm_sems(n):
    return [pltpu.SemaphoreType.DMA((7 * n,)), pltpu.SemaphoreType.DMA((7 * n,)), pltpu.SemaphoreType.DMA((n,))]


class _Gather:
    def __init__(self, ins, outs, send_sems, recv_sems, local_sems):
        self.ins, self.outs, self.n = ins, outs, len(ins)
        self.send_sems, self.recv_sems, self.local_sems = send_sems, recv_sems, local_sems
        self.me = _mesh_pos()
        self.sibling = _flip(self.me, 1)

    def _copy(self, a, kk, block, to, src=None):
        rows = self.outs[a].at[_slot(block)]
        return pltpu.make_async_remote_copy(
            src_ref=rows if src is None else src, dst_ref=rows,
            send_sem=self.send_sems.at[7 * a + kk], recv_sem=self.recv_sems.at[7 * a + kk],
            device_id=to, device_id_type=MESH)

    def _mine(self):
        return [pltpu.make_async_copy(self.ins[a], self.outs[a].at[_slot(self.me)], self.local_sems.at[a])
                for a in range(self.n)]

    def _first(self):
        out = []
        for a in range(self.n):
            out.append(self._copy(a, 0, self.me, self.sibling, src=self.ins[a]))
            out += [self._copy(a, 1 + j, self.me, _flip(self.me, k), src=self.ins[a]) for j, k in enumerate(_CHIPS)]
        return out

    def _passed(self):
        return [self._copy(a, 4 + j, _flip(self.me, k), self.sibling)
                for j, k in enumerate(_CHIPS) for a in range(self.n)]

    def start(self):
        for cp in self._mine() + self._first():
            cp.start()

    def forward(self):
        for j, k in enumerate(_CHIPS):
            for a in range(self.n):
                self._copy(a, 1 + j, _flip(self.me, k), self.me).wait_recv()
                self._copy(a, 4 + j, _flip(self.me, k), self.sibling).start()

    def finish(self):
        for a in range(self.n):
            self._copy(a, 0, self.sibling, self.me).wait_recv()
            for j, k in enumerate(_CHIPS):
                self._copy(a, 4 + j, _flip(self.sibling, k), self.me).wait_recv()
        for cp in self._first() + self._passed():
            cp.wait_send()
        for cp in self._mine():
            cp.wait()


class _Scatter:
    def __init__(self, ins, outs, send_sems, recv_sems, local_sems):
        self.ins, self.outs, self.n = ins, outs, len(ins)
        self.send_sems, self.recv_sems, self.local_sems = send_sems, recv_sems, local_sems
        self.me = _mesh_pos()

    def _copy(self, a, k, landed=False):
        peer = _flip(self.me, k)
        return pltpu.make_async_remote_copy(
            src_ref=self.ins[a].at[_slot(peer)], dst_ref=self.outs[a].at[_slot(peer if landed else self.me)],
            send_sem=self.send_sems.at[7 * a + k - 1], recv_sem=self.recv_sems.at[7 * a + k - 1],
            device_id=peer, device_id_type=MESH)

    def _mine(self):
        s = _slot(self.me)
        return [pltpu.make_async_copy(self.ins[a].at[s], self.outs[a].at[s], self.local_sems.at[a])
                for a in range(self.n)]

    def start(self):
        for cp in self._mine() + [self._copy(a, k) for k in range(1, 8) for a in range(self.n)]:
            cp.start()

    def finish(self):
        for k in range(1, 8):
            for a in range(self.n):
                self._copy(a, k, landed=True).wait_recv()
        for k in range(1, 8):
            for a in range(self.n):
                self._copy(a, k).wait_send()
        for cp in self._mine():
            cp.wait()


_ANY = pl.BlockSpec(memory_space=pl.ANY)


def _gathered_shapes(shards):
    return [jax.ShapeDtypeStruct((N_DEV,) + s.shape, s.dtype) for s in shards]


def _all_gather(shards, name):
    n = len(shards)

    def body(*refs):
        g = _Gather(refs[:n], refs[n:2 * n], *refs[2 * n:])
        g.start()
        g.forward()
        g.finish()

    return pl.pallas_call(
        body, name=name, in_specs=[_ANY] * n, out_specs=[_ANY] * n,
        out_shape=_gathered_shapes(shards), scratch_shapes=_comm_sems(n),
    )(*shards)


def _adamw_math(w, g, m, v):
    m2 = ADAM_B1 * m + (1.0 - ADAM_B1) * g
    v2 = ADAM_B2 * v + (1.0 - ADAM_B2) * (g * g)
    m_hat = m2 / (1.0 - ADAM_B1 ** ADAM_STEP)
    v_hat = v2 / (1.0 - ADAM_B2 ** ADAM_STEP)
    delta = -ADAM_LR * (m_hat / (jnp.sqrt(v_hat) + ADAM_EPS) + ADAM_WD * w)
    return delta, m2, v2


def _adamw(w, parts, m, v, name):
    R, C = w.shape
    br = 128 if R % 128 == 0 else R

    def body(w_ref, p_ref, m_ref, v_ref, g_ref, d_ref, nm_ref, nv_ref):
        g = p_ref[0].astype(F32)
        for s in range(1, N_DEV):
            g = g + p_ref[s].astype(F32)
        g_ref[...] = g
        d_ref[...], nm_ref[...], nv_ref[...] = _adamw_math(w_ref[...], g, m_ref[...], v_ref[...])

    spec = pl.BlockSpec((br, C), lambda i: (i, 0))
    return pl.pallas_call(
        body,
        name=name,
        grid=(R // br,),
        in_specs=[spec, pl.BlockSpec((N_DEV, br, C), lambda i: (0, i, 0)), spec, spec],
        out_specs=[spec] * 4,
        out_shape=[jax.ShapeDtypeStruct((R, C), F32)] * 4,
        compiler_params=_cparams(("arbitrary",)),
    )(w, parts, m, v)


_SHARDED = ("w_in", "w_out", "w_mq", "w_mkv", "w_mo", "w_up", "conv_w", "w_down")
_LATE = _SHARDED[1:]
_COL_SHARDED = ("w_in", "w_mkv", "w_up", "conv_w")
_REPLICATED = ("attn_norm_g", "b_forget", "fox_out_g", "sb_out_g", "xattn_norm_g", "mem_norm_g",
               "ffn_norm_g", "conv_b", "final_norm_g")
_WEIGHTS = ("attn_norm_g", "w_in", "b_forget", "fox_out_g", "sb_out_g", "w_out", "xattn_norm_g", "mem_norm_g",
            "w_mq", "w_mkv", "w_mo", "ffn_norm_g", "w_up", "conv_w", "conv_b", "w_down", "final_norm_g")


def _pack_rows(n):
    return -(-n // 128)


def _pack(vals, rows_total):
    parts = []
    for v in vals:
        flat = v.reshape(-1)
        parts.append(jnp.pad(flat, (0, _pack_rows(flat.shape[0]) * 128 - flat.shape[0])))
    flat = jnp.concatenate(parts)
    return jnp.pad(flat, (0, rows_total * 128 - flat.shape[0])).reshape(rows_total, 128)


def _unpack(packed, shapes):
    out = []
    r = 0
    for shp in shapes:
        n = 1
        for d in shp:
            n *= d
        out.append(packed[r:r + _pack_rows(n)].reshape(-1)[:n].reshape(shp))
        r += _pack_rows(n)
    return out


def _gathered_full(name, gathered):
    if name in _COL_SHARDED:
        return jnp.transpose(gathered, (1, 0, 2)).reshape(gathered.shape[1], -1)
    return gathered.reshape(-1, gathered.shape[2])


def _to_blocks(name, full):
    if name in _COL_SHARDED:
        r = full.shape[0]
        return jnp.transpose(full.reshape(r, N_DEV, -1), (1, 0, 2))
    return full.reshape(N_DEV, -1, full.shape[1])


def _grad_blocks(name, full):
    blocks = _to_blocks(name, full)
    return blocks if name == "conv_w" else blocks.astype(BF16)


def _step(args, tm, tq):
    w = {n: args[n] for n in _WEIGHTS}
    mom = {n: args["m_" + n] for n in _WEIGHTS}
    var = {n: args["v_" + n] for n in _WEIGHTS}
    x = args["x"][0]
    mem = args["mem"][0]
    target = args["loss_target"][0]

    def flat2(a):
        return a.reshape(a.shape[-2], a.shape[-1]) if a.ndim == 3 else a.reshape(1, -1)

    shards = {n: flat2(w[n]) if n == "conv_w" else flat2(w[n]).astype(BF16) for n in _SHARDED}
    (w_in_all,) = _all_gather([shards["w_in"]], "gather_w_in")
    p = {"w_in": _gathered_full("w_in", w_in_all)}
    for n in _REPLICATED:
        p[n] = flat2(w[n])

    loss_blk, grad_x, g = _local_step(x, mem, target, p, tm, tq, late={n: shards[n] for n in _LATE})

    parts = g["parts"]
    out = {}
    for n in _SHARDED:
        res = _adamw(flat2(w[n]), parts[n], flat2(mom[n]), flat2(var[n]), "adamw_" + n)
        out[n] = [r.reshape(w[n].shape) for r in res]

    shapes = [w[n].shape for n in _REPLICATED]
    rows = sum(_pack_rows(flat2(w[n]).shape[1]) for n in _REPLICATED) + 1
    rows = -(-rows // 8) * 8
    g_pack = _pack([g[n] for n in _REPLICATED] + [loss_blk[0:1, :]], rows)
    (g_all,) = _all_gather([g_pack], "gather_small")
    res = _adamw(_pack([w[n] for n in _REPLICATED], rows), g_all,
                 _pack([mom[n] for n in _REPLICATED], rows), _pack([var[n] for n in _REPLICATED], rows),
                 "adamw_small")
    n_rows_params = sum(_pack_rows(flat2(w[n]).shape[1]) for n in _REPLICATED)
    loss = res[0][n_rows_params, 0]
    unpacked = [_unpack(r, shapes) for r in res]
    for k, n in enumerate(_REPLICATED):
        out[n] = [unpacked[q][k] for q in range(4)]

    grads = [out[n][0] for n in _WEIGHTS]
    deltas = [out[n][1] for n in _WEIGHTS]
    new_m = [out[n][2] for n in _WEIGHTS]
    new_v = [out[n][3] for n in _WEIGHTS]
    return (loss, grad_x[None], *grads, *deltas, *new_m, *new_v)


def kernel(x, mem, attn_norm_g, w_in, b_forget, fox_out_g, sb_out_g, w_out, xattn_norm_g, mem_norm_g, w_mq, w_mkv, w_mo, ffn_norm_g, w_up, conv_w, conv_b, w_down, final_norm_g, loss_target, m_attn_norm_g, m_w_in, m_b_forget, m_fox_out_g, m_sb_out_g, m_w_out, m_xattn_norm_g, m_mem_norm_g, m_w_mq, m_w_mkv, m_w_mo, m_ffn_norm_g, m_w_up, m_conv_w, m_conv_b, m_w_down, m_final_norm_g, v_attn_norm_g, v_w_in, v_b_forget, v_fox_out_g, v_sb_out_g, v_w_out, v_xattn_norm_g, v_mem_norm_g, v_w_mq, v_w_mkv, v_w_mo, v_ffn_norm_g, v_w_up, v_conv_w, v_conv_b, v_w_down, v_final_norm_g):
    args = dict(locals())
    T = x.shape[1]
    return _step(args, tm=min(T, 512), tq=min(T, 256))
```

```python
import functools

import jax
import jax.numpy as jnp
from jax import lax
from jax.experimental import pallas as pl
from jax.experimental.pallas import tpu as pltpu

F32 = jnp.float32
BF16 = jnp.bfloat16
EPS = 1e-6
NEG = -1e30
LOG2E = 1.4426950408889634

HEAD_DIM = 64
N_FOX = 8
FOX_W = 512
QKV_W = 3072
IN_PAD = 3200
N_MEM_HEADS = 4
MEM_HD = 256
D_FF = 2816
FF_CHUNK = 256
N_DEV = 8

ADAM_LR = 0.001
ADAM_B1 = 0.9
ADAM_B2 = 0.999
ADAM_EPS = 1e-08
ADAM_WD = 0.01
ADAM_STEP = 10

SB_SUM_TERMS = 1

VMEM_LIMIT = 56 * 1024 * 1024
MESH = pl.DeviceIdType.MESH


def _cparams(sem=None):
    return pltpu.CompilerParams(dimension_semantics=sem, vmem_limit_bytes=VMEM_LIMIT)


def _nt(a, b):
    return lax.dot_general(a, b, (((1,), (1,)), ((), ())), preferred_element_type=F32)


def _tn(a, b):
    return lax.dot_general(a, b, (((0,), (0,)), ((), ())), preferred_element_type=F32)


def _nn(a, b):
    return jnp.dot(a, b, preferred_element_type=F32)


def _split_dot(a, m01, terms):
    out = None
    r = a
    for t in range(terms):
        p = r.astype(BF16)
        d = _nn(p, m01)
        out = d if out is None else out + d
        if t + 1 < terms:
            r = r - p.astype(F32)
    return out


def _rstd(xv):
    return lax.rsqrt(jnp.mean(xv * xv, axis=-1, keepdims=True) + EPS)


def _norm_bwd(xv, g, dh):
    r = _rstd(xv)
    xhat = xv * r
    dxhat = dh * g
    dx = r * (dxhat - xhat * jnp.mean(dxhat * xhat, axis=-1, keepdims=True))
    dg = jnp.sum(dh * xhat, axis=0, keepdims=True)
    return dx, dg


def _tile_div(n, cap):
    best = None
    for d in range(128, min(n, cap) + 1, 128):
        if n % d == 0:
            best = d
    assert best is not None, n
    return best


def _inproj_fwd(x, g1, w_qkv, w_f_t, b_f, tm):
    T, D = x.shape
    N = w_qkv.shape[1]
    H = w_f_t.shape[0]

    def body(x_ref, g_ref, w_ref, wf_ref, b_ref, proj_ref, h_ref, xf_ref, c_ref, carry_ref):
        i = pl.program_id(0)

        @pl.when(i == 0)
        def _():
            carry_ref[...] = jnp.zeros_like(carry_ref)

        xv = x_ref[...]
        h = (xv * _rstd(xv) * g_ref[...]).astype(BF16)
        h_ref[...] = h
        for n0 in range(0, N, 512):
            proj_ref[:, n0:n0 + 512] = _nn(h, w_ref[:, n0:n0 + 512]).astype(BF16)
        xf = _nt(wf_ref[...], h) + b_ref[...]
        xf_ref[...] = xf
        logf = jnp.minimum(xf, 0.0) - jnp.log1p(jnp.exp(-jnp.abs(xf)))
        row = lax.broadcasted_iota(jnp.int32, (tm, tm), 0)
        col = lax.broadcasted_iota(jnp.int32, (tm, tm), 1)
        upper = jnp.where(row <= col, 1.0, 0.0).astype(BF16)
        c = _split_dot(logf, upper, 3) + carry_ref[...]
        c_ref[...] = c
        carry_ref[...] = c[:, tm - 1:tm]

    return pl.pallas_call(
        body,
        name="inproj_fwd",
        grid=(T // tm,),
        in_specs=[
            pl.BlockSpec((tm, D), lambda i: (i, 0)),
            pl.BlockSpec((1, D), lambda i: (0, 0)),
            pl.BlockSpec((D, N), lambda i: (0, 0)),
            pl.BlockSpec((H, D), lambda i: (0, 0)),
            pl.BlockSpec((H, 1), lambda i: (0, 0)),
        ],
        out_specs=[
            pl.BlockSpec((tm, N), lambda i: (i, 0)),
            pl.BlockSpec((tm, D), lambda i: (i, 0)),
            pl.BlockSpec((H, tm), lambda i: (0, i)),
            pl.BlockSpec((H, tm), lambda i: (0, i)),
        ],
        out_shape=[
            jax.ShapeDtypeStruct((T, N), BF16),
            jax.ShapeDtypeStruct((T, D), BF16),
            jax.ShapeDtypeStruct((H, T), F32),
            jax.ShapeDtypeStruct((H, T), F32),
        ],
        scratch_shapes=[pltpu.VMEM((H, 1), F32)],
        compiler_params=_cparams(("arbitrary",)),
    )(x, g1, w_qkv, w_f_t, b_f)


def _head_q(q, hh, lane):
    hmask = (lane >= HEAD_DIM * hh) & (lane < HEAD_DIM * (hh + 1))
    qh = jnp.where(hmask, q.astype(F32), 0.0) * (HEAD_DIM ** -0.5)
    return qh.astype(BF16), hmask


def _pipeline3(n, stage_a, stage_b, stage_c, diag_last, alive=None):
    stage_a(0, 0)
    if diag_last:
        @pl.when(n == 1)
        def _():
            stage_b(0, 0, True)

        @pl.when(n >= 2)
        def _():
            stage_b(0, 0, False)
    else:
        stage_b(0, 0, True)

    @pl.when(n >= 2)
    def _():
        stage_a(1, 1)

    def pair(m, carry):
        t = 2 + 2 * m
        stage_c(t - 2)
        stage_b(t - 1, 1, False)
        stage_a(t, 0)
        stage_c(t - 1)
        stage_b(t, 0, False)
        stage_a(t + 1, 1)
        return carry

    pairs = (n - 2) // 2
    if alive is None:
        lax.fori_loop(0, pairs, pair, 0)
        go_on = True
        done = n
    else:
        def more(state):
            return (state[0] < pairs) & state[1]

        def step(state):
            pair(state[0], 0)
            return state[0] + 1, alive()

        m_end, go_on = lax.while_loop(more, step, (jnp.int32(0), jnp.bool_(True)))
        done = jnp.where(go_on, n, 2 * m_end)
    odd = n % 2 == 1

    @pl.when((n >= 3) & odd & go_on)
    def _():
        stage_c(n - 3)
        stage_b(n - 2, 1, False)
        stage_a(n - 1, 0)

    @pl.when((n >= 2) & odd & go_on)
    def _():
        stage_c(n - 2)
        stage_b(n - 1, 0, diag_last)

    @pl.when((n >= 2) & jnp.logical_not(odd) & go_on)
    def _():
        stage_c(n - 2)
        stage_b(n - 1, 1, diag_last)

    if alive is None:
        stage_c(n - 1)
    else:
        @pl.when(go_on)
        def _():
            stage_c(n - 1)

    return done


def _lanes2(x):
    return jnp.concatenate([x, x], axis=1)


def _lanes_to_rows(vec, eye):
    return jnp.sum(jnp.where(eye, jnp.broadcast_to(vec, eye.shape), 0.0), axis=1, keepdims=True)


def _rows_to_lanes(rep, eye):
    return jnp.sum(jnp.where(eye, _lanes2(rep), 0.0), axis=0, keepdims=True)


FOX_DEAD = -110.0


def _fox_key_norms(k_ref, kn_s, lane):
    T = k_ref.shape[0]
    rows = min(T, 512)
    for hh in range(2):
        hmask = (lane >= HEAD_DIM * hh) & (lane < HEAD_DIM * (hh + 1))

        def chunk(n, best, hmask=hmask):
            kf = jnp.where(hmask, k_ref[pl.ds(pl.multiple_of(n * rows, rows), rows), :].astype(F32), 0.0)
            sq = jnp.sum(kf * kf, axis=1, keepdims=True)
            return jnp.maximum(best, jnp.max(sq, axis=0, keepdims=True))

        best = lax.fori_loop(0, T // rows, chunk, jnp.zeros((1, 1), F32))
        kn_s[hh] = jnp.broadcast_to(best, kn_s.shape[1:])


def _fox_live_blocks(i, qh_s, kn_s, cq_ref, cke_ref):
    nq = cke_ref.shape[-1]
    jj = lax.broadcasted_iota(jnp.int32, (1, nq), 1)
    first = None
    for hh in range(2):
        qf = qh_s[hh].astype(F32)
        qn = jnp.max(jnp.sum(qf * qf, axis=1, keepdims=True), axis=0, keepdims=True)
        zb = jnp.sqrt(qn * kn_s[hh][0:1, 0:1]) * 1.001
        bound = (2.0 * zb + cq_ref[hh][:, 0:1]) - cke_ref[hh]
        live = (bound >= FOX_DEAD) & (jj <= i)
        f = jnp.min(jnp.where(live, jj, i).astype(F32), axis=1, keepdims=True)
        first = f if first is None else jnp.minimum(first, f)
    return i + 1 - first[0, 0].astype(jnp.int32)


def _fox_fwd(proj, c_col, c_row, c_ends, tq):
    T = proj.shape[0]
    assert tq == 256
    nq = T // tq

    def body(q_ref, k_ref, v_ref, cq_ref, ck_ref, cke_ref, o_ref, lse_ref,
             qh_s, cq_s, z_s, p_s, al_s, m_s, l_s, acc_s, kn_s):
        i = pl.program_id(1)
        lane = lax.broadcasted_iota(jnp.int32, (1, 128), 1)
        row = lax.broadcasted_iota(jnp.int32, (tq, tq), 0)
        col = lax.broadcasted_iota(jnp.int32, (tq, tq), 1)
        ones = jnp.ones((tq, 128), BF16)

        @pl.when(i == 0)
        def _():
            _fox_key_norms(k_ref, kn_s, lane)

        q = q_ref[...]
        for hh in range(2):
            qh_s[hh] = _head_q(q, hh, lane)[0]
            cq_s[hh] = jnp.broadcast_to(_lanes_to_rows(cq_ref[hh], row == col), (tq, tq))
        m_s[...] = jnp.full(m_s.shape, NEG, F32)
        l_s[...] = jnp.zeros_like(l_s)
        acc_s[...] = jnp.zeros_like(acc_s)

        def rows(t):
            return pl.ds(pl.multiple_of((i - t) * tq, tq), tq)

        def stage_a(t, slot):
            k = k_ref[rows(t), :]
            for hh in range(2):
                z_s[slot, hh] = _nt(qh_s[hh], k)

        def stage_b(t, slot, diag):
            for hh in range(2):
                s = z_s[slot, hh] + cq_s[hh] - ck_ref[hh, :, rows(t)]
                if diag:
                    s = jnp.where(col <= row, s, NEG)
                m = m_s[hh]
                half = jnp.maximum(s[:, :128], s[:, 128:])
                m_new = jnp.maximum(m, jnp.max(half, axis=1, keepdims=True))
                alpha = jnp.exp(m - m_new)
                p = jnp.exp(s - _lanes2(m_new)).astype(BF16)
                l_s[hh] = alpha * l_s[hh] + _nn(p, ones)
                m_s[hh] = m_new
                al_s[hh] = alpha
                p_s[hh] = p

        def stage_c(t):
            v = v_ref[rows(t), :]
            for hh in range(2):
                acc_s[hh] = al_s[hh] * acc_s[hh] + _nn(p_s[hh], v)

        _pipeline3(_fox_live_blocks(i, qh_s, kn_s, cq_ref, cke_ref), stage_a, stage_b, stage_c, False)
        l0, l1 = l_s[0], l_s[1]
        o_ref[...] = jnp.where(lane < HEAD_DIM, acc_s[0] / l0, acc_s[1] / l1)
        lse_ref[0] = _rows_to_lanes(m_s[0] + jnp.log(l0), row == col)
        lse_ref[1] = _rows_to_lanes(m_s[1] + jnp.log(l1), row == col)

    return pl.pallas_call(
        body,
        name="fox_fwd",
        grid=(4, nq),
        in_specs=[
            pl.BlockSpec((tq, 128), lambda p, i: (i, p)),
            pl.BlockSpec((T, 128), lambda p, i: (0, 4 + p)),
            pl.BlockSpec((T, 128), lambda p, i: (0, 8 + p)),
            pl.BlockSpec((2, 1, tq), lambda p, i: (p, 0, i)),
            pl.BlockSpec((2, 1, T), lambda p, i: (p, 0, 0)),
            pl.BlockSpec((2, 1, nq), lambda p, i: (p, 0, 0)),
        ],
        out_specs=[
            pl.BlockSpec((tq, 128), lambda p, i: (i, p)),
            pl.BlockSpec((2, 1, tq), lambda p, i: (p, 0, i)),
        ],
        out_shape=[
            jax.ShapeDtypeStruct((T, FOX_W), F32),
            jax.ShapeDtypeStruct((N_FOX, 1, T), F32),
        ],
        scratch_shapes=[
            pltpu.VMEM((2, tq, 128), BF16),
            pltpu.VMEM((2, tq, tq), F32),
            pltpu.VMEM((2, 2, tq, tq), F32),
            pltpu.VMEM((2, tq, tq), BF16),
            pltpu.VMEM((2, tq, 128), F32),
            pltpu.VMEM((2, tq, 128), F32),
            pltpu.VMEM((2, tq, 128), F32),
            pltpu.VMEM((2, tq, 128), F32),
            pltpu.VMEM((2, 8, 128), F32),
        ],
        compiler_params=_cparams(("arbitrary", "arbitrary")),
    )(proj, proj, proj, c_col, c_row, c_ends)


def _sb_logs(zn, strict):
    e = jnp.exp2(jnp.abs(zn) * (-LOG2E))
    L = jnp.minimum(zn, 0.0) - jnp.log(1.0 + e)
    G = L - zn
    if strict is not None:
        L = jnp.where(strict, L, 0.0)
    return L, G


SB_DEAD = -110.0


def _sb_fwd(proj, tq, gather=()):
    T = proj.shape[0]
    ng = len(gather)
    nq = T // tq

    def body(*refs):
        q_ref, k_ref, v_ref = refs[:3]
        o_ref, ltot_ref, live_ref = refs[3 + ng:6 + ng]
        qh_s, z_s, g_s, tot_s, run_s, acc_s = refs[6 + 2 * ng:12 + 2 * ng]
        i = pl.program_id(1)
        if ng:
            pair = pl.program_id(0)
            exchange = _Gather(refs[3:3 + ng], refs[6 + ng:6 + 2 * ng], *refs[12 + 2 * ng:])

            @pl.when((pair == 0) & (i == 0))
            def _():
                exchange.start()

            @pl.when((pair == 1) & (i == 0))
            def _():
                exchange.forward()

        lane = lax.broadcasted_iota(jnp.int32, (1, 128), 1)
        row = lax.broadcasted_iota(jnp.int32, (tq, tq), 0)
        col = lax.broadcasted_iota(jnp.int32, (tq, tq), 1)
        strict = col < row
        later = jnp.where(row > col, 1.0, 0.0).astype(BF16)
        q = q_ref[...]
        for hh in range(2):
            qh_s[hh] = -_head_q(q, hh, lane)[0]
        run_s[...] = jnp.zeros_like(run_s)
        acc_s[...] = jnp.zeros_like(acc_s)

        def rows(t):
            return pl.ds(pl.multiple_of((i - t) * tq, tq), tq)

        def stage_a(t, slot):
            k = k_ref[rows(t), :]
            for hh in range(2):
                z_s[slot, hh] = _nt(qh_s[hh], k)

        def stage_b(t, slot, diag):
            for hh in range(2):
                L, g = _sb_logs(z_s[slot, hh], strict if diag else None)
                if diag:
                    g = jnp.where(strict, g, NEG)
                after = _split_dot(L, later, SB_SUM_TERMS)
                g_s[hh] = g + after
                first = L[:, 0:1]
                if SB_SUM_TERMS == 1:
                    first = first.astype(BF16).astype(F32)
                tot_s[hh] = jnp.broadcast_to(after[:, 0:1] + first, (tq, 128))

        def stage_c(t):
            v = v_ref[rows(t), :]
            for hh in range(2):
                run = run_s[hh]
                a = jnp.exp(g_s[hh] + _lanes2(run))
                acc_s[hh] += _nn(a.astype(BF16), v)
                run_s[hh] = run + tot_s[hh]

        def alive():
            return jnp.max(jnp.maximum(run_s[0], run_s[1])) > SB_DEAD

        done = _pipeline3(i + 1, stage_a, stage_b, stage_c, False, alive)
        ltot_ref[0] = _rows_to_lanes(run_s[0], row == col)
        ltot_ref[1] = _rows_to_lanes(run_s[1], row == col)
        o_ref[...] = jnp.where(lane < HEAD_DIM, acc_s[0], acc_s[1])
        at = lax.broadcasted_iota(jnp.int32, (1, nq), 1)

        @pl.when(i == 0)
        def _():
            live_ref[0] = jnp.zeros((1, nq), F32)

        live_ref[0] = jnp.where(at == i, done.astype(F32), live_ref[0])
        if ng:
            @pl.when((pair == 3) & (i == nq - 1))
            def _():
                exchange.finish()

    res = pl.pallas_call(
        body,
        name="sb_fwd",
        grid=(4, nq),
        in_specs=[
            pl.BlockSpec((tq, 128), lambda p, i: (i, 12 + p)),
            pl.BlockSpec((T, 128), lambda p, i: (0, 16 + p)),
            pl.BlockSpec((T, 128), lambda p, i: (0, 20 + p)),
        ] + [_ANY] * ng,
        out_specs=[
            pl.BlockSpec((tq, 128), lambda p, i: (i, p)),
            pl.BlockSpec((2, 1, tq), lambda p, i: (p, 0, i)),
            pl.BlockSpec((1, 1, nq), lambda p, i: (p, 0, 0)),
        ] + [_ANY] * ng,
        out_shape=[
            jax.ShapeDtypeStruct((T, FOX_W), F32),
            jax.ShapeDtypeStruct((N_FOX, 1, T), F32),
            jax.ShapeDtypeStruct((N_FOX // 2, 1, nq), F32),
        ] + _gathered_shapes(gather),
        scratch_shapes=[
            pltpu.VMEM((2, tq, 128), BF16),
            pltpu.VMEM((2, 2, tq, tq), F32),
            pltpu.VMEM((2, tq, tq), F32),
            pltpu.VMEM((2, tq, 128), F32),
            pltpu.VMEM((2, tq, 128), F32),
            pltpu.VMEM((2, tq, 128), F32),
        ] + (_comm_sems(ng) if ng else []),
        compiler_params=_cparams(("arbitrary", "arbitrary")),
    )(proj, proj, proj, *gather)
    res = list(res)
    return res[0], res[1], res[2], res[3:]


def _post_attn_fwd(fox_o, sb_o, gf, gs, w_out, x, tm):
    T, D = x.shape

    def body(f_ref, s_ref, gf_ref, gs_ref, w_ref, x_ref, x1_ref, mix_ref):
        f = f_ref[...]
        s = s_ref[...]
        mix_ref[:, :FOX_W] = (f * _rstd(f) * gf_ref[...]).astype(BF16)
        mix_ref[:, FOX_W:] = (s * _rstd(s) * gs_ref[...]).astype(BF16)
        x1_ref[...] = x_ref[...] + _nn(mix_ref[...], w_ref[...])

    return pl.pallas_call(
        body,
        name="post_attn_fwd",
        grid=(T // tm,),
        in_specs=[
            pl.BlockSpec((tm, FOX_W), lambda i: (i, 0)),
            pl.BlockSpec((tm, FOX_W), lambda i: (i, 0)),
            pl.BlockSpec((1, FOX_W), lambda i: (0, 0)),
            pl.BlockSpec((1, FOX_W), lambda i: (0, 0)),
            pl.BlockSpec((D, D), lambda i: (0, 0)),
            pl.BlockSpec((tm, D), lambda i: (i, 0)),
        ],
        out_specs=[
            pl.BlockSpec((tm, D), lambda i: (i, 0)),
            pl.BlockSpec((tm, D), lambda i: (i, 0)),
        ],
        out_shape=[jax.ShapeDtypeStruct((T, D), F32), jax.ShapeDtypeStruct((T, D), BF16)],
        compiler_params=_cparams(("arbitrary",)),
    )(fox_o, sb_o, gf, gs, w_out, x)


def _mem_kv_fwd(mem, gm, w_mkv):
    M, D = mem.shape
    N = w_mkv.shape[1]

    def body(mem_ref, g_ref, w_ref, m_ref, kv_ref):
        mv = mem_ref[...]
        m = (mv * _rstd(mv) * g_ref[...]).astype(BF16)
        m_ref[...] = m
        for n0 in range(0, N, 512):
            kv_ref[:, n0:n0 + 512] = _nn(m, w_ref[:, n0:n0 + 512]).astype(BF16)

    return pl.pallas_call(
        body,
        name="mem_kv_fwd",
        out_shape=[jax.ShapeDtypeStruct((M, D), BF16), jax.ShapeDtypeStruct((M, N), BF16)],
        compiler_params=_cparams(),
    )(mem, gm, w_mkv)


def _xattn_probs(qb, kv, h):
    k = kv[:, h * MEM_HD:(h + 1) * MEM_HD]
    s = _nt(qb[:, h * MEM_HD:(h + 1) * MEM_HD], k) * (MEM_HD ** -0.5)
    s = s - jnp.max(s, axis=1, keepdims=True)
    p = jnp.exp(s)
    return p / jnp.sum(p, axis=1, keepdims=True)


def _xattn_fwd(x1, g2, w_mq, kv, w_mo, tm):
    T, D = x1.shape
    M = kv.shape[0]

    def body(x_ref, g_ref, wq_ref, kv_ref, wo_ref, x2_ref, h_ref, q_ref, om_ref):
        xv = x_ref[...]
        h = (xv * _rstd(xv) * g_ref[...]).astype(BF16)
        h_ref[...] = h
        q_ref[...] = _nn(h, wq_ref[...]).astype(BF16)
        qb = q_ref[...]
        kvv = kv_ref[...]
        for hd in range(N_MEM_HEADS):
            p = _xattn_probs(qb, kvv, hd)
            v = kvv[:, D + hd * MEM_HD:D + (hd + 1) * MEM_HD]
            om_ref[:, hd * MEM_HD:(hd + 1) * MEM_HD] = _nn(p.astype(BF16), v).astype(BF16)
        x2_ref[...] = xv + _nn(om_ref[...], wo_ref[...])

    return pl.pallas_call(
        body,
        name="xattn_fwd",
        grid=(T // tm,),
        in_specs=[
            pl.BlockSpec((tm, D), lambda i: (i, 0)),
            pl.BlockSpec((1, D), lambda i: (0, 0)),
            pl.BlockSpec((D, D), lambda i: (0, 0)),
            pl.BlockSpec((M, 2 * D), lambda i: (0, 0)),
            pl.BlockSpec((D, D), lambda i: (0, 0)),
        ],
        out_specs=[pl.BlockSpec((tm, D), lambda i: (i, 0))] * 4,
        out_shape=[jax.ShapeDtypeStruct((T, D), F32)] + [jax.ShapeDtypeStruct((T, D), BF16)] * 3,
        compiler_params=_cparams(("arbitrary",)),
    )(x1, g2, w_mq, kv, w_mo)


def _conv_taps(ext_ref, tm, back):
    if back:
        return ext_ref[pl.ds(6, tm), :], ext_ref[pl.ds(7, tm), :], ext_ref[pl.ds(8, tm), :]
    return ext_ref[pl.ds(0, tm), :], ext_ref[pl.ds(1, tm), :], ext_ref[pl.ds(2, tm), :]


def _ffn_fwd(x2, g3, w_up, conv_w, conv_b, w_down, tm):
    T, D = x2.shape
    fc = FF_CHUNK
    nj = D_FF // fc

    def body(x_ref, g_ref, wg_ref, wv_ref, cwg_ref, cwv_ref, cbg_ref, cbv_ref, wd_ref,
             x3_ref, h_ref, ug_ref, uv_ref, yg_ref, yv_ref, a_ref, acc_ref, carry_ref, ext_ref):
        i = pl.program_id(0)
        j = pl.program_id(1)

        @pl.when(j == 0)
        def _():
            xv = x_ref[...]
            h_ref[...] = (xv * _rstd(xv) * g_ref[...]).astype(BF16)
            acc_ref[...] = xv

        @pl.when(i == 0)
        def _():
            carry_ref[j] = jnp.zeros((2, 8, fc), F32)

        h = h_ref[...]
        halves = []
        for part, (w_ref, cw_ref, cb_ref, u_ref, y_ref) in enumerate(
                ((wg_ref, cwg_ref, cbg_ref, ug_ref, yg_ref), (wv_ref, cwv_ref, cbv_ref, uv_ref, yv_ref))):
            u = _nn(h, w_ref[...])
            u_ref[...] = u.astype(BF16)
            ext = ext_ref.at[part]
            ext[pl.ds(0, 8), :] = carry_ref[j, part]
            ext[pl.ds(8, tm), :] = u
            carry_ref[j, part] = u[tm - 8:, :]
            u2, u1, u0 = _conv_taps(ext, tm, True)
            cw = cw_ref[...]
            y = cb_ref[...] + cw[0:1] * u2 + cw[1:2] * u1 + cw[2:3] * u0
            y_ref[...] = y.astype(BF16)
            halves.append(y)
        gate, val = halves
        a = (gate * jax.nn.sigmoid(gate) * val).astype(BF16)
        a_ref[...] = a
        acc_ref[...] += _nn(a, wd_ref[...])

        @pl.when(j == nj - 1)
        def _():
            x3_ref[...] = acc_ref[...]

    return pl.pallas_call(
        body,
        name="ffn_fwd",
        grid=(T // tm, nj),
        in_specs=[
            pl.BlockSpec((tm, D), lambda i, j: (i, 0)),
            pl.BlockSpec((1, D), lambda i, j: (0, 0)),
            pl.BlockSpec((D, fc), lambda i, j: (0, j)),
            pl.BlockSpec((D, fc), lambda i, j: (0, nj + j)),
            pl.BlockSpec((3, fc), lambda i, j: (0, j)),
            pl.BlockSpec((3, fc), lambda i, j: (0, nj + j)),
            pl.BlockSpec((1, fc), lambda i, j: (0, j)),
            pl.BlockSpec((1, fc), lambda i, j: (0, nj + j)),
            pl.BlockSpec((fc, D), lambda i, j: (j, 0)),
        ],
        out_specs=[
            pl.BlockSpec((tm, D), lambda i, j: (i, 0)),
            pl.BlockSpec((tm, D), lambda i, j: (i, 0)),
        ] + [pl.BlockSpec((tm, fc), lambda i, j: (i, j))] * 5,
        out_shape=[
            jax.ShapeDtypeStruct((T, D), F32),
            jax.ShapeDtypeStruct((T, D), BF16),
        ] + [jax.ShapeDtypeStruct((T, D_FF), BF16)] * 5,
        scratch_shapes=[
            pltpu.VMEM((tm, D), F32),
            pltpu.VMEM((nj, 2, 8, fc), F32),
            pltpu.VMEM((2, tm + 8, fc), F32),
        ],
        compiler_params=_cparams(("arbitrary", "arbitrary")),
    )(x2, g3, w_up, w_up, conv_w, conv_w, conv_b, conv_b, w_down)


def _loss_head(x3, gfin, target, tm):
    T, D = x3.shape

    def body(x_ref, g_ref, t_ref, dx_ref, loss_ref, dg_ref):
        i = pl.program_id(0)

        @pl.when(i == 0)
        def _():
            loss_ref[...] = jnp.zeros_like(loss_ref)
            dg_ref[...] = jnp.zeros_like(dg_ref)

        xv = x_ref[...]
        g = g_ref[...]
        r = _rstd(xv)
        xhat = xv * r
        err = xhat * g - t_ref[...]
        part = jnp.sum(jnp.sum(err * err, axis=1, keepdims=True), axis=0, keepdims=True) * (0.5 / D)
        loss_ref[...] += jnp.broadcast_to(part, loss_ref.shape)
        dy = err * (1.0 / D)
        dg_ref[...] += jnp.sum(dy * xhat, axis=0, keepdims=True)
        dxhat = dy * g
        dx_ref[...] = r * (dxhat - xhat * jnp.mean(dxhat * xhat, axis=-1, keepdims=True))

    return pl.pallas_call(
        body,
        name="loss_head",
        grid=(T // tm,),
        in_specs=[
            pl.BlockSpec((tm, D), lambda i: (i, 0)),
            pl.BlockSpec((1, D), lambda i: (0, 0)),
            pl.BlockSpec((tm, D), lambda i: (i, 0)),
        ],
        out_specs=[
            pl.BlockSpec((tm, D), lambda i: (i, 0)),
            pl.BlockSpec((8, 128), lambda i: (0, 0)),
            pl.BlockSpec((1, D), lambda i: (0, 0)),
        ],
        out_shape=[
            jax.ShapeDtypeStruct((T, D), F32),
            jax.ShapeDtypeStruct((8, 128), F32),
            jax.ShapeDtypeStruct((1, D), F32),
        ],
        compiler_params=_cparams(("arbitrary",)),
    )(x3, gfin, target)


def _ffn_bwd(dx3, x2, g3, ug, uv, yg, yv, conv_w, w_down, w_up, tm):
    T, D = x2.shape
    fc = FF_CHUNK
    nj = D_FF // fc
    nt = T // tm

    def rev(i):
        return nt - 1 - i

    def body(dx3_ref, x_ref, g_ref, ug_ref, uv_ref, yg_ref, yv_ref, cwg_ref, cwv_ref,
             wd_ref, wug_ref, wuv_ref,
             dx2_ref, dug_ref, duv_ref, dg_ref, dcg_ref, dcv_ref,
             acc_ref, carry_ref, ext_ref):
        i = pl.program_id(0)
        j = pl.program_id(1)
        cols = pl.ds(pl.multiple_of(j * fc, fc), fc)

        @pl.when(j == 0)
        def _():
            acc_ref[...] = jnp.zeros_like(acc_ref)

        @pl.when((i == 0) & (j == 0))
        def _():
            dg_ref[...] = jnp.zeros_like(dg_ref)
            dcg_ref[...] = jnp.zeros_like(dcg_ref)
            dcv_ref[...] = jnp.zeros_like(dcv_ref)

        @pl.when(i == 0)
        def _():
            carry_ref[j] = jnp.zeros((2, 8, fc), F32)

        da = _nt(dx3_ref[...].astype(BF16), wd_ref[...])
        gate = yg_ref[...].astype(F32)
        val = yv_ref[...].astype(F32)
        sig = jax.nn.sigmoid(gate)
        silu = gate * sig
        dys = (da * val * (sig * (1.0 + gate * (1.0 - sig))), da * silu)
        for part, (dy, u_ref, cw_ref, du_ref, wu_ref, dc_ref) in enumerate(
                ((dys[0], ug_ref, cwg_ref, dug_ref, wug_ref, dcg_ref),
                 (dys[1], uv_ref, cwv_ref, duv_ref, wuv_ref, dcv_ref))):
            ext = ext_ref.at[part]
            ext[pl.ds(0, tm), :] = dy
            ext[pl.ds(tm, 8), :] = carry_ref[j, part]
            carry_ref[j, part] = dy[:8, :]
            d0, d1, d2 = _conv_taps(ext, tm, False)
            u = u_ref[...].astype(F32)
            upd = jnp.concatenate([
                jnp.sum(u * d2, axis=0, keepdims=True),
                jnp.sum(u * d1, axis=0, keepdims=True),
                jnp.sum(u * d0, axis=0, keepdims=True),
                jnp.sum(d0, axis=0, keepdims=True),
                jnp.zeros((4, fc), F32)], axis=0)
            dc_ref[:, cols] += upd
            cw = cw_ref[...]
            du = (cw[2:3] * d0 + cw[1:2] * d1 + cw[0:1] * d2).astype(BF16)
            du_ref[...] = du
            acc_ref[...] += _nt(du, wu_ref[...])

        @pl.when(j == nj - 1)
        def _():
            dx, dg = _norm_bwd(x_ref[...], g_ref[...], acc_ref[...])
            dx2_ref[...] = dx3_ref[...] + dx
            dg_ref[...] += dg

    return pl.pallas_call(
        body,
        name="ffn_bwd",
        grid=(nt, nj),
        in_specs=[
            pl.BlockSpec((tm, D), lambda i, j: (rev(i), 0)),
            pl.BlockSpec((tm, D), lambda i, j: (rev(i), 0)),
            pl.BlockSpec((1, D), lambda i, j: (0, 0)),
            pl.BlockSpec((tm, fc), lambda i, j: (rev(i), j)),
            pl.BlockSpec((tm, fc), lambda i, j: (rev(i), j)),
            pl.BlockSpec((tm, fc), lambda i, j: (rev(i), j)),
            pl.BlockSpec((tm, fc), lambda i, j: (rev(i), j)),
            pl.BlockSpec((3, fc), lambda i, j: (0, j)),
            pl.BlockSpec((3, fc), lambda i, j: (0, nj + j)),
            pl.BlockSpec((fc, D), lambda i, j: (j, 0)),
            pl.BlockSpec((D, fc), lambda i, j: (0, j)),
            pl.BlockSpec((D, fc), lambda i, j: (0, nj + j)),
        ],
        out_specs=[
            pl.BlockSpec((tm, D), lambda i, j: (rev(i), 0)),
            pl.BlockSpec((tm, fc), lambda i, j: (rev(i), j)),
            pl.BlockSpec((tm, fc), lambda i, j: (rev(i), j)),
            pl.BlockSpec((1, D), lambda i, j: (0, 0)),
            pl.BlockSpec((8, D_FF), lambda i, j: (0, 0)),
            pl.BlockSpec((8, D_FF), lambda i, j: (0, 0)),
        ],
        out_shape=[
            jax.ShapeDtypeStruct((T, D), F32),
            jax.ShapeDtypeStruct((T, D_FF), BF16),
            jax.ShapeDtypeStruct((T, D_FF), BF16),
            jax.ShapeDtypeStruct((1, D), F32),
            jax.ShapeDtypeStruct((8, D_FF), F32),
            jax.ShapeDtypeStruct((8, D_FF), F32),
        ],
        scratch_shapes=[
            pltpu.VMEM((tm, D), F32),
            pltpu.VMEM((nj, 2, 8, fc), F32),
            pltpu.VMEM((2, tm + 8, fc), F32),
        ],
        compiler_params=_cparams(("arbitrary", "arbitrary")),
    )(dx3, x2, g3, ug, uv, yg, yv, conv_w, conv_w, w_down, w_up, w_up)


def _xattn_bwd(dx2, x1, g2, qb, kv, w_mo, w_mq, tm):
    T, D = x1.shape
    M = kv.shape[0]

    def body(dx2_ref, x_ref, g_ref, q_ref, kv_ref, wo_ref, wq_ref, dx1_ref, dq_ref, dkv_ref, dg_ref):
        i = pl.program_id(0)

        @pl.when(i == 0)
        def _():
            dkv_ref[...] = jnp.zeros_like(dkv_ref)
            dg_ref[...] = jnp.zeros_like(dg_ref)

        dxv = dx2_ref[...]
        dom = _nt(dxv.astype(BF16), wo_ref[...]).astype(BF16)
        qb_ = q_ref[...]
        kvv = kv_ref[...]
        for hd in range(N_MEM_HEADS):
            sl = slice(hd * MEM_HD, (hd + 1) * MEM_HD)
            vsl = slice(D + hd * MEM_HD, D + (hd + 1) * MEM_HD)
            p = _xattn_probs(qb_, kvv, hd)
            dp = _nt(dom[:, sl], kvv[:, vsl])
            ds = (p * (dp - jnp.sum(p * dp, axis=1, keepdims=True)) * (MEM_HD ** -0.5)).astype(BF16)
            dq_ref[:, sl] = _nn(ds, kvv[:, sl]).astype(BF16)
            dkv_ref[:, sl] += _tn(ds, qb_[:, sl])
            dkv_ref[:, vsl] += _tn(p.astype(BF16), dom[:, sl])
        dh = _nt(dq_ref[...], wq_ref[...])
        dx, dg = _norm_bwd(x_ref[...], g_ref[...], dh)
        dx1_ref[...] = dxv + dx
        dg_ref[...] += dg

    return pl.pallas_call(
        body,
        name="xattn_bwd",
        grid=(T // tm,),
        in_specs=[
            pl.BlockSpec((tm, D), lambda i: (i, 0)),
            pl.BlockSpec((tm, D), lambda i: (i, 0)),
            pl.BlockSpec((1, D), lambda i: (0, 0)),
            pl.BlockSpec((tm, D), lambda i: (i, 0)),
            pl.BlockSpec((M, 2 * D), lambda i: (0, 0)),
            pl.BlockSpec((D, D), lambda i: (0, 0)),
            pl.BlockSpec((D, D), lambda i: (0, 0)),
        ],
        out_specs=[
            pl.BlockSpec((tm, D), lambda i: (i, 0)),
            pl.BlockSpec((tm, D), lambda i: (i, 0)),
            pl.BlockSpec((M, 2 * D), lambda i: (0, 0)),
            pl.BlockSpec((1, D), lambda i: (0, 0)),
        ],
        out_shape=[
            jax.ShapeDtypeStruct((T, D), F32),
            jax.ShapeDtypeStruct((T, D), BF16),
            jax.ShapeDtypeStruct((M, 2 * D), F32),
            jax.ShapeDtypeStruct((1, D), F32),
        ],
        compiler_params=_cparams(("arbitrary",)),
    )(dx2, x1, g2, qb, kv, w_mo, w_mq)


def _mem_kv_bwd(mem, gm, mb, dkv, w_mkv):
    M, D = mem.shape
    N = dkv.shape[1]

    def body(mem_ref, g_ref, m_ref, dkv_ref, w_ref, dw_ref, dg_ref):
        dkvb = dkv_ref[...].astype(BF16)
        for n0 in range(0, N, 512):
            dw_ref[:, n0:n0 + 512] = _tn(m_ref[...], dkvb[:, n0:n0 + 512]).astype(BF16)
        dm = _nt(dkvb, w_ref[...])
        mv = mem_ref[...]
        dg_ref[...] = jnp.sum(dm * (mv * _rstd(mv)), axis=0, keepdims=True)

    return pl.pallas_call(
        body,
        name="mem_kv_bwd",
        out_shape=[jax.ShapeDtypeStruct((D, N), BF16), jax.ShapeDtypeStruct((1, D), F32)],
        compiler_params=_cparams(),
    )(mem, gm, mb, dkv, w_mkv)


def _post_attn_bwd(dx1, fox_o, sb_o, gf, gs, w_out, tm):
    T, D = dx1.shape

    def body(dx_ref, f_ref, s_ref, gf_ref, gs_ref, w_ref, df_ref, ds_ref, dgf_ref, dgs_ref):
        i = pl.program_id(0)

        @pl.when(i == 0)
        def _():
            dgf_ref[...] = jnp.zeros_like(dgf_ref)
            dgs_ref[...] = jnp.zeros_like(dgs_ref)

        dmix = _nt(dx_ref[...].astype(BF16), w_ref[...])
        d, dg = _norm_bwd(f_ref[...], gf_ref[...], dmix[:, :FOX_W])
        df_ref[...] = d
        dgf_ref[...] += dg
        d, dg = _norm_bwd(s_ref[...], gs_ref[...], dmix[:, FOX_W:])
        ds_ref[...] = d
        dgs_ref[...] += dg

    return pl.pallas_call(
        body,
        name="post_attn_bwd",
        grid=(T // tm,),
        in_specs=[
            pl.BlockSpec((tm, D), lambda i: (i, 0)),
            pl.BlockSpec((tm, FOX_W), lambda i: (i, 0)),
            pl.BlockSpec((tm, FOX_W), lambda i: (i, 0)),
            pl.BlockSpec((1, FOX_W), lambda i: (0, 0)),
            pl.BlockSpec((1, FOX_W), lambda i: (0, 0)),
            pl.BlockSpec((D, D), lambda i: (0, 0)),
        ],
        out_specs=[
            pl.BlockSpec((tm, FOX_W), lambda i: (i, 0)),
            pl.BlockSpec((tm, FOX_W), lambda i: (i, 0)),
            pl.BlockSpec((1, FOX_W), lambda i: (0, 0)),
            pl.BlockSpec((1, FOX_W), lambda i: (0, 0)),
        ],
        out_shape=[
            jax.ShapeDtypeStruct((T, FOX_W), F32),
            jax.ShapeDtypeStruct((T, FOX_W), F32),
            jax.ShapeDtypeStruct((1, FOX_W), F32),
            jax.ShapeDtypeStruct((1, FOX_W), F32),
        ],
        compiler_params=_cparams(("arbitrary",)),
    )(dx1, fox_o, sb_o, gf, gs, w_out)


def _sb_bwd(proj, ltot, live, d_o, tq, scatter=()):
    T = proj.shape[0]
    ns = len(scatter)
    nq = T // tq

    def body(*refs):
        q_ref, k_ref, v_ref, lt_ref, live_ref, do_ref = refs[:6]
        dq_ref, dk_ref, dv_ref = refs[6 + ns:9 + ns]
        qh_s, doh_s, lt_s, z_s, da_s, ab_s, dzb_s, run_s, runw_s, dq_s = refs[9 + 2 * ns:19 + 2 * ns]
        i = pl.program_id(1)
        if ns:
            pair = pl.program_id(0)
            exchange = _Scatter(refs[6:6 + ns], refs[9 + ns:9 + 2 * ns], *refs[19 + 2 * ns:])

            @pl.when((pair == 0) & (i == 0))
            def _():
                exchange.start()

        @pl.when(i == 0)
        def _():
            dk_ref[...] = jnp.zeros_like(dk_ref)
            dv_ref[...] = jnp.zeros_like(dv_ref)

        lane = lax.broadcasted_iota(jnp.int32, (1, 128), 1)
        row = lax.broadcasted_iota(jnp.int32, (tq, tq), 0)
        col = lax.broadcasted_iota(jnp.int32, (tq, tq), 1)
        strict = col < row
        upto = jnp.where(row <= col, 1.0, 0.0).astype(BF16)
        before = jnp.where(row < col, 1.0, 0.0).astype(BF16)
        q = q_ref[...]
        dov = do_ref[...]
        for hh in range(2):
            qh, hmask = _head_q(q, hh, lane)
            qh_s[hh] = -qh
            doh_s[hh] = jnp.where(hmask, dov, 0.0).astype(BF16)
            lt_s[hh] = jnp.broadcast_to(_lanes_to_rows(lt_ref[hh], row == col), (tq, 128))
        run_s[...] = jnp.zeros_like(run_s)
        runw_s[...] = jnp.zeros_like(runw_s)
        dq_s[...] = jnp.zeros_like(dq_s)

        at = lax.broadcasted_iota(jnp.int32, (1, nq), 1)
        count = jnp.sum(jnp.where(at == i, live_ref[0], 0.0), axis=1, keepdims=True)[0, 0].astype(jnp.int32)
        n_live = jnp.clip(count, 1, i + 1)
        oldest = i + 1 - n_live

        def rows(t):
            return pl.ds(pl.multiple_of((oldest + t) * tq, tq), tq)

        def stage_a(t, slot):
            k = k_ref[rows(t), :]
            v = v_ref[rows(t), :]
            for hh in range(2):
                z_s[slot, hh] = _nt(qh_s[hh], k)
                da_s[slot, hh] = _nt(doh_s[hh], v)

        def stage_b(t, slot, diag):
            for hh in range(2):
                L, g = _sb_logs(z_s[slot, hh], strict if diag else None)
                upto_s = _split_dot(L, upto, SB_SUM_TERMS)
                run = run_s[hh]
                arg = (g + _lanes2(lt_s[hh] - run)) - upto_s
                if diag:
                    arg = jnp.where(strict, arg, NEG)
                a = jnp.exp(arg)
                w = a * da_s[slot, hh]
                w_before = _split_dot(w, before, SB_SUM_TERMS)
                run_w = runw_s[hh]
                d_keep = w_before + _lanes2(run_w)
                beta = jnp.exp(g)
                ndz = beta * (w + d_keep) - w
                if diag:
                    ndz = jnp.where(strict, ndz, 0.0)
                dzb_s[hh] = ndz.astype(BF16)
                ab_s[hh] = a.astype(BF16)
                run_s[hh] = run + jnp.broadcast_to(upto_s[:, tq - 1:tq], (tq, 128))
                runw_s[hh] = run_w + jnp.broadcast_to(w_before[:, tq - 1:tq] + w[:, tq - 1:tq], (tq, 128))

        def stage_c(t):
            k = k_ref[rows(t), :]
            dk_blk = None
            dv_blk = None
            for hh in range(2):
                dzb = dzb_s[hh]
                dq_s[hh] += _nn(dzb, k)
                dk_h = _tn(dzb, qh_s[hh])
                dv_h = _tn(ab_s[hh], doh_s[hh])
                dk_blk = dk_h if dk_blk is None else dk_blk + dk_h
                dv_blk = dv_h if dv_blk is None else dv_blk + dv_h
            dk_ref[rows(t), :] += dk_blk
            dv_ref[rows(t), :] += dv_blk

        _pipeline3(n_live, stage_a, stage_b, stage_c, True)
        dq_ref[...] = (jnp.where(lane < HEAD_DIM, dq_s[0], dq_s[1]) * -(HEAD_DIM ** -0.5)).astype(BF16)
        if ns:
            @pl.when((pair == 3) & (i == nq - 1))
            def _():
                exchange.finish()

    res = pl.pallas_call(
        body,
        name="sb_bwd",
        grid=(4, nq),
        in_specs=[
            pl.BlockSpec((tq, 128), lambda p, i: (i, 12 + p)),
            pl.BlockSpec((T, 128), lambda p, i: (0, 16 + p)),
            pl.BlockSpec((T, 128), lambda p, i: (0, 20 + p)),
            pl.BlockSpec((2, 1, tq), lambda p, i: (p, 0, i)),
            pl.BlockSpec((1, 1, nq), lambda p, i: (p, 0, 0)),
            pl.BlockSpec((tq, 128), lambda p, i: (i, p)),
        ] + [_ANY] * ns,
        out_specs=[
            pl.BlockSpec((tq, 128), lambda p, i: (i, p)),
            pl.BlockSpec((T, 128), lambda p, i: (0, p)),
            pl.BlockSpec((T, 128), lambda p, i: (0, p)),
        ] + [_ANY] * ns,
        out_shape=[
            jax.ShapeDtypeStruct((T, FOX_W), BF16),
            jax.ShapeDtypeStruct((T, FOX_W), F32),
            jax.ShapeDtypeStruct((T, FOX_W), F32),
        ] + [jax.ShapeDtypeStruct(b.shape, b.dtype) for b in scatter],
        scratch_shapes=[
            pltpu.VMEM((2, tq, 128), BF16),
            pltpu.VMEM((2, tq, 128), BF16),
            pltpu.VMEM((2, tq, 128), F32),
            pltpu.VMEM((2, 2, tq, tq), F32),
            pltpu.VMEM((2, 2, tq, tq), F32),
            pltpu.VMEM((2, tq, tq), BF16),
            pltpu.VMEM((2, tq, tq), BF16),
            pltpu.VMEM((2, tq, 128), F32),
            pltpu.VMEM((2, tq, 128), F32),
            pltpu.VMEM((2, tq, 128), F32),
        ] + (_comm_sems(ns) if ns else []),
        compiler_params=_cparams(("arbitrary", "arbitrary")),
    )(proj, proj, proj, ltot, live, d_o, *scatter)
    res = list(res)
    return (*res[:3], res[3:])


def _fox_bwd(proj, c_col, c_row, c_ends, lse, d_o, o, tq):
    T = proj.shape[0]
    nq = T // tq

    def body(q_ref, k_ref, v_ref, cq_ref, ck_ref, cke_ref, lse_ref, do_ref, o_ref,
             dq_ref, dk_ref, dv_ref, dck_ref, dcq_ref,
             qh_s, doh_s, delta_s, shift_s, z_s, dp_s, pb_s, dsb_s, rs_s, dq_s, kn_s):
        i = pl.program_id(1)
        lane = lax.broadcasted_iota(jnp.int32, (1, 128), 1)

        @pl.when(i == 0)
        def _():
            dk_ref[...] = jnp.zeros_like(dk_ref)
            dv_ref[...] = jnp.zeros_like(dv_ref)
            dck_ref[...] = jnp.zeros_like(dck_ref)
            _fox_key_norms(k_ref, kn_s, lane)

        row = lax.broadcasted_iota(jnp.int32, (tq, tq), 0)
        col = lax.broadcasted_iota(jnp.int32, (tq, tq), 1)
        q = q_ref[...]
        dov = do_ref[...]
        ov = o_ref[...]
        for hh in range(2):
            qh, hmask = _head_q(q, hh, lane)
            dohb = jnp.where(hmask, dov, 0.0).astype(BF16)
            qh_s[hh] = qh
            doh_s[hh] = dohb
            delta_s[hh] = jnp.broadcast_to(jnp.sum(dohb.astype(F32) * ov, axis=1, keepdims=True), (tq, tq))
            shift_s[hh] = jnp.broadcast_to(_lanes_to_rows(cq_ref[hh] - lse_ref[hh], row == col), (tq, tq))
        rs_s[...] = jnp.zeros_like(rs_s)
        dq_s[...] = jnp.zeros_like(dq_s)

        def rows(t):
            return pl.ds(pl.multiple_of((i - t) * tq, tq), tq)

        def stage_a(t, slot):
            k = k_ref[rows(t), :]
            v = v_ref[rows(t), :]
            for hh in range(2):
                z_s[slot, hh] = _nt(qh_s[hh], k)
                dp_s[slot, hh] = _nt(doh_s[hh], v)

        def stage_b(t, slot, diag):
            for hh in range(2):
                s = z_s[slot, hh] + shift_s[hh] - ck_ref[hh, :, rows(t)]
                if diag:
                    s = jnp.where(col <= row, s, NEG)
                p = jnp.exp(s)
                ds = p * (dp_s[slot, hh] - delta_s[hh])
                pb_s[hh] = p.astype(BF16)
                dsb_s[hh] = ds.astype(BF16)
                dck_ref[hh, :, rows(t)] += jnp.sum(ds, axis=0, keepdims=True)
                rs_s[hh] += jnp.sum(ds, axis=1, keepdims=True)

        def stage_c(t):
            k = k_ref[rows(t), :]
            dk_blk = None
            dv_blk = None
            for hh in range(2):
                dsb = dsb_s[hh]
                dq_s[hh] += _nn(dsb, k)
                dk_h = _tn(dsb, qh_s[hh])
                dv_h = _tn(pb_s[hh], doh_s[hh])
                dk_blk = dk_h if dk_blk is None else dk_blk + dk_h
                dv_blk = dv_h if dv_blk is None else dv_blk + dv_h
            dk_ref[rows(t), :] += dk_blk
            dv_ref[rows(t), :] += dv_blk

        _pipeline3(_fox_live_blocks(i, qh_s, kn_s, cq_ref, cke_ref), stage_a, stage_b, stage_c, False)
        dcq_ref[0] = _rows_to_lanes(rs_s[0], row == col)
        dcq_ref[1] = _rows_to_lanes(rs_s[1], row == col)
        dq_ref[...] = (jnp.where(lane < HEAD_DIM, dq_s[0], dq_s[1]) * (HEAD_DIM ** -0.5)).astype(BF16)

    return pl.pallas_call(
        body,
        name="fox_bwd",
        grid=(4, nq),
        in_specs=[
            pl.BlockSpec((tq, 128), lambda p, i: (i, p)),
            pl.BlockSpec((T, 128), lambda p, i: (0, 4 + p)),
            pl.BlockSpec((T, 128), lambda p, i: (0, 8 + p)),
            pl.BlockSpec((2, 1, tq), lambda p, i: (p, 0, i)),
            pl.BlockSpec((2, 1, T), lambda p, i: (p, 0, 0)),
            pl.BlockSpec((2, 1, nq), lambda p, i: (p, 0, 0)),
            pl.BlockSpec((2, 1, tq), lambda p, i: (p, 0, i)),
            pl.BlockSpec((tq, 128), lambda p, i: (i, p)),
            pl.BlockSpec((tq, 128), lambda p, i: (i, p)),
        ],
        out_specs=[
            pl.BlockSpec((tq, 128), lambda p, i: (i, p)),
            pl.BlockSpec((T, 128), lambda p, i: (0, p)),
            pl.BlockSpec((T, 128), lambda p, i: (0, p)),
            pl.BlockSpec((2, 1, T), lambda p, i: (p, 0, 0)),
            pl.BlockSpec((2, 1, tq), lambda p, i: (p, 0, i)),
        ],
        out_shape=[
            jax.ShapeDtypeStruct((T, FOX_W), BF16),
            jax.ShapeDtypeStruct((T, FOX_W), F32),
            jax.ShapeDtypeStruct((T, FOX_W), F32),
            jax.ShapeDtypeStruct((N_FOX, 1, T), F32),
            jax.ShapeDtypeStruct((N_FOX, 1, T), F32),
        ],
        scratch_shapes=[
            pltpu.VMEM((2, tq, 128), BF16),
            pltpu.VMEM((2, tq, 128), BF16),
            pltpu.VMEM((2, tq, tq), F32),
            pltpu.VMEM((2, tq, tq), F32),
            pltpu.VMEM((2, 2, tq, tq), F32),
            pltpu.VMEM((2, 2, tq, tq), F32),
            pltpu.VMEM((2, tq, tq), BF16),
            pltpu.VMEM((2, tq, tq), BF16),
            pltpu.VMEM((2, tq, 128), F32),
            pltpu.VMEM((2, tq, 128), F32),
            pltpu.VMEM((2, 8, 128), F32),
        ],
        compiler_params=_cparams(("arbitrary", "arbitrary")),
    )(proj, proj, proj, c_col, c_row, c_ends, lse, d_o, o)


def _forget_bwd(dcq, dck, xf, tc):
    H, T = xf.shape
    nc = T // tc

    def body(dcq_ref, dck_ref, xf_ref, dxf_ref, db_ref):
        row = lax.broadcasted_iota(jnp.int32, (tc, tc), 0)
        col = lax.broadcasted_iota(jnp.int32, (tc, tc), 1)
        from_here = jnp.where(row >= col, 1.0, 0.0).astype(BF16)

        def chunk(n, carry):
            run, db = carry
            cs = pl.multiple_of((nc - 1 - n) * tc, tc)
            dc = dcq_ref[:, pl.ds(cs, tc)] - dck_ref[:, pl.ds(cs, tc)]
            dlogf = _split_dot(dc, from_here, 3) + run
            xfv = xf_ref[:, pl.ds(cs, tc)]
            dxf = dlogf * jax.nn.sigmoid(-xfv)
            dxf_ref[:, pl.ds(cs, tc)] = dxf
            return dlogf[:, 0:1], db + jnp.sum(dxf, axis=1, keepdims=True)

        _, db = lax.fori_loop(0, nc, chunk, (jnp.zeros((H, 1), F32), jnp.zeros((H, 1), F32)))
        db_ref[...] = db

    return pl.pallas_call(
        body,
        name="forget_bwd",
        out_shape=[jax.ShapeDtypeStruct((H, T), F32), jax.ShapeDtypeStruct((H, 1), F32)],
        compiler_params=_cparams(),
    )(dcq, dck, xf)


def _inproj_bwd(dproj, w_in_pad, x, g1, dx1, tm, scatter=()):
    T, D = x.shape
    N = dproj.shape[1]
    ns = len(scatter)
    nt = T // tm

    def body(*refs):
        dp_ref, w_ref, x_ref, g_ref, dx1_ref = refs[:5]
        dx_ref, dg_ref = refs[5 + ns:7 + ns]
        i = pl.program_id(0)
        if ns:
            exchange = _Scatter(refs[5:5 + ns], refs[7 + ns:7 + 2 * ns], *refs[7 + 2 * ns:])

            @pl.when(i == 0)
            def _():
                exchange.start()

        @pl.when(i == 0)
        def _():
            dg_ref[...] = jnp.zeros_like(dg_ref)

        dh = _nt(dp_ref[...], w_ref[...])
        dx, dg = _norm_bwd(x_ref[...], g_ref[...], dh)
        dx_ref[...] = dx1_ref[...] + dx
        dg_ref[...] += dg
        if ns:
            @pl.when(i == nt - 1)
            def _():
                exchange.finish()

    res = pl.pallas_call(
        body,
        name="inproj_bwd",
        grid=(nt,),
        in_specs=[
            pl.BlockSpec((tm, N), lambda i: (i, 0)),
            pl.BlockSpec((D, N), lambda i: (0, 0)),
            pl.BlockSpec((tm, D), lambda i: (i, 0)),
            pl.BlockSpec((1, D), lambda i: (0, 0)),
            pl.BlockSpec((tm, D), lambda i: (i, 0)),
        ] + [_ANY] * ns,
        out_specs=[
            pl.BlockSpec((tm, D), lambda i: (i, 0)),
            pl.BlockSpec((1, D), lambda i: (0, 0)),
        ] + [_ANY] * ns,
        out_shape=[jax.ShapeDtypeStruct((T, D), F32), jax.ShapeDtypeStruct((1, D), F32)]
        + [jax.ShapeDtypeStruct(b.shape, b.dtype) for b in scatter],
        scratch_shapes=_comm_sems(ns) if ns else [],
        compiler_params=_cparams(("arbitrary",)),
    )(dproj, w_in_pad, x, g1, dx1, *scatter)
    res = list(res)
    return res[0], res[1], res[2:]


def _matmul_tn(a, b, name, cast_b=False):
    T, K = a.shape
    N = b.shape[1]
    bt = min(T, 512)
    bk = _tile_div(K, 1536)
    bn = _tile_div(N, 1536)
    nt = T // bt

    def body(a_ref, b_ref, o_ref, acc_ref):
        t = pl.program_id(2)

        @pl.when(t == 0)
        def _():
            acc_ref[...] = jnp.zeros_like(acc_ref)

        bv = b_ref[...]
        if cast_b:
            bv = bv.astype(BF16)
        acc_ref[...] += _tn(a_ref[...], bv)

        @pl.when(t == nt - 1)
        def _():
            o_ref[...] = acc_ref[...].astype(BF16)

    return pl.pallas_call(
        body,
        name=name,
        grid=(K // bk, N // bn, nt),
        in_specs=[
            pl.BlockSpec((bt, bk), lambda k, n, t: (t, k)),
            pl.BlockSpec((bt, bn), lambda k, n, t: (t, n)),
        ],
        out_specs=pl.BlockSpec((bk, bn), lambda k, n, t: (k, n)),
        out_shape=jax.ShapeDtypeStruct((K, N), BF16),
        scratch_shapes=[pltpu.VMEM((bk, bn), F32)],
        compiler_params=_cparams(("arbitrary", "arbitrary", "arbitrary")),
    )(a, b)


def _local_step(x, mem, target, p, tm, tq, late=None):
    T, D = x.shape
    w_in = p["w_in"]
    w_qkv = w_in[:, :QKV_W]
    w_f_t = w_in[:, QKV_W:].T
    w_in_pad = jnp.pad(w_in, ((0, 0), (0, IN_PAD - w_in.shape[1])))
    b_f = p["b_forget"].reshape(N_FOX, 1)

    proj, h1, xf, c = _inproj_fwd(x, p["attn_norm_g"], w_qkv, w_f_t, b_f, tm)
    c_col = c.reshape(N_FOX, 1, T)
    c_row = c.reshape(N_FOX, 1, T)
    c_ends = c[:, tq - 1::tq].reshape(N_FOX, 1, T // tq)
    fox_o, lse = _fox_fwd(proj, c_col, c_row, c_ends, tq)
    if late:
        sb_o, sb_ltot, sb_live, gathered = _sb_fwd(proj, tq, gather=[late[n] for n in _LATE])
        p = dict(p, **{n: _gathered_full(n, gv) for n, gv in zip(_LATE, gathered)})
    else:
        sb_o, sb_ltot, sb_live, _ = _sb_fwd(proj, tq)
    x1, mixed = _post_attn_fwd(fox_o, sb_o, p["fox_out_g"], p["sb_out_g"], p["w_out"], x, tm)
    mb, kv = _mem_kv_fwd(mem, p["mem_norm_g"], p["w_mkv"])
    x2, h2, qb, om = _xattn_fwd(x1, p["xattn_norm_g"], p["w_mq"], kv, p["w_mo"], tm)
    x3, h3, ug, uv, yg, yv, a = _ffn_fwd(
        x2, p["ffn_norm_g"], p["w_up"], p["conv_w"], p["conv_b"], p["w_down"], tm)
    dx3, loss_blk, d_final_g = _loss_head(x3, p["final_norm_g"], target, tm)

    g = {"final_norm_g": d_final_g}
    dx2, du_g, du_v, g["ffn_norm_g"], dc_g, dc_v = _ffn_bwd(
        dx3, x2, p["ffn_norm_g"], ug, uv, yg, yv, p["conv_w"], p["w_down"], p["w_up"], tm)
    g["w_down"] = _matmul_tn(a, dx3, "dw_down", cast_b=True)
    g["w_up"] = jnp.concatenate([_matmul_tn(h3, du_g, "dw_up_gate"), _matmul_tn(h3, du_v, "dw_up_val")], axis=1)
    dconv = jnp.concatenate([dc_g, dc_v], axis=1)
    g["conv_w"] = dconv[0:3]
    g["conv_b"] = dconv[3:4]
    dx1, dq_m, dkv, g["xattn_norm_g"] = _xattn_bwd(dx2, x1, p["xattn_norm_g"], qb, kv, p["w_mo"], p["w_mq"], tm)
    g["w_mo"] = _matmul_tn(om, dx2, "dw_mo", cast_b=True)
    g["w_mq"] = _matmul_tn(h2, dq_m, "dw_mq")
    g["w_mkv"], g["mem_norm_g"] = _mem_kv_bwd(mem, p["mem_norm_g"], mb, dkv, p["w_mkv"])
    d_fox, d_sb, g["fox_out_g"], g["sb_out_g"] = _post_attn_bwd(
        dx1, fox_o, sb_o, p["fox_out_g"], p["sb_out_g"], p["w_out"], tm)
    g["w_out"] = _matmul_tn(mixed, dx1, "dw_out", cast_b=True)
    if late:
        dq_s, dk_s, dv_s, parts = _sb_bwd(proj, sb_ltot, sb_live, d_sb, tq,
                                          scatter=[_grad_blocks(n, g[n]) for n in _LATE])
        g["parts"] = dict(zip(_LATE, parts))
    else:
        dq_s, dk_s, dv_s, _ = _sb_bwd(proj, sb_ltot, sb_live, d_sb, tq)
    dq_f, dk_f, dv_f, dck, dcq = _fox_bwd(proj, c_col, c_row, c_ends, lse, d_fox, fox_o, tq)
    dxf, db = _forget_bwd(dcq.reshape(N_FOX, T), dck.reshape(N_FOX, T), xf, min(T, 512))
    g["b_forget"] = db.reshape(1, N_FOX)
    dproj = jnp.concatenate([
        dq_f, dk_f.astype(BF16), dv_f.astype(BF16), dq_s, dk_s.astype(BF16), dv_s.astype(BF16),
        jnp.pad(dxf.T, ((0, 0), (0, IN_PAD - QKV_W - N_FOX))).astype(BF16)], axis=1)
    g["w_in"] = _matmul_tn(h1, dproj, "dw_in")[:, :w_in.shape[1]]
    if late:
        grad_x, g["attn_norm_g"], (g["parts"]["w_in"],) = _inproj_bwd(
            dproj, w_in_pad, x, p["attn_norm_g"], dx1, tm, scatter=[_grad_blocks("w_in", g["w_in"])])
    else:
        grad_x, g["attn_norm_g"], _ = _inproj_bwd(dproj, w_in_pad, x, p["attn_norm_g"], dx1, tm)
    return loss_blk, grad_x, g


def _mesh_pos():
    return lax.axis_index("x"), lax.axis_index("y"), lax.axis_index("c")


def _flip(pos, k):
    return tuple(1 - v if (k >> b) & 1 else v for v, b in zip(pos, (2, 1, 0)))


def _slot(pos):
    return 4 * pos[0] + 2 * pos[1] + pos[2]


_CHIPS = (4, 2, 6)


def _comm_sems(n):
    return [pltpu.SemaphoreType.DMA((7 * n,)), pltpu.SemaphoreType.DMA((7 * n,)), pltpu.SemaphoreType.DMA((n,))]


class _Gather:
    def __init__(self, ins, outs, send_sems, recv_sems, local_sems):
        self.ins, self.outs, self.n = ins, outs, len(ins)
        self.send_sems, self.recv_sems, self.local_sems = send_sems, recv_sems, local_sems
        self.me = _mesh_pos()
        self.sibling = _flip(self.me, 1)

    def _copy(self, a, kk, block, to, src=None):
        rows = self.outs[a].at[_slot(block)]
        return pltpu.make_async_remote_copy(
            src_ref=rows if src is None else src, dst_ref=rows,
            send_sem=self.send_sems.at[7 * a + kk], recv_sem=self.recv_sems.at[7 * a + kk],
            device_id=to, device_id_type=MESH)

    def _mine(self):
        return [pltpu.make_async_copy(self.ins[a], self.outs[a].at[_slot(self.me)], self.local_sems.at[a])
                for a in range(self.n)]

    def _first(self):
        out = []
        for a in range(self.n):
            out.append(self._copy(a, 0, self.me, self.sibling, src=self.ins[a]))
            out += [self._copy(a, 1 + j, self.me, _flip(self.me, k), src=self.ins[a]) for j, k in enumerate(_CHIPS)]
        return out

    def _passed(self):
        return [self._copy(a, 4 + j, _flip(self.me, k), self.sibling)
                for j, k in enumerate(_CHIPS) for a in range(self.n)]

    def start(self):
        for cp in self._mine() + self._first():
            cp.start()

    def forward(self):
        for j, k in enumerate(_CHIPS):
            for a in range(self.n):
                self._copy(a, 1 + j, _flip(self.me, k), self.me).wait_recv()
                self._copy(a, 4 + j, _flip(self.me, k), self.sibling).start()

    def finish(self):
        for a in range(self.n):
            self._copy(a, 0, self.sibling, self.me).wait_recv()
            for j, k in enumerate(_CHIPS):
                self._copy(a, 4 + j, _flip(self.sibling, k), self.me).wait_recv()
        for cp in self._first() + self._passed():
            cp.wait_send()
        for cp in self._mine():
            cp.wait()


class _Scatter:
    def __init__(self, ins, outs, send_sems, recv_sems, local_sems):
        self.ins, self.outs, self.n = ins, outs, len(ins)
        self.send_sems, self.recv_sems, self.local_sems = send_sems, recv_sems, local_sems
        self.me = _mesh_pos()

    def _copy(self, a, k, landed=False):
        peer = _flip(self.me, k)
        return pltpu.make_async_remote_copy(
            src_ref=self.ins[a].at[_slot(peer)], dst_ref=self.outs[a].at[_slot(peer if landed else self.me)],
            send_sem=self.send_sems.at[7 * a + k - 1], recv_sem=self.recv_sems.at[7 * a + k - 1],
            device_id=peer, device_id_type=MESH)

    def _mine(self):
        s = _slot(self.me)
        return [pltpu.make_async_copy(self.ins[a].at[s], self.outs[a].at[s], self.local_sems.at[a])
                for a in range(self.n)]

    def start(self):
        for cp in self._mine() + [self._copy(a, k) for k in range(1, 8) for a in range(self.n)]:
            cp.start()

    def finish(self):
        for k in range(1, 8):
            for a in range(self.n):
                self._copy(a, k, landed=True).wait_recv()
        for k in range(1, 8):
            for a in range(self.n):
                self._copy(a, k).wait_send()
        for cp in self._mine():
            cp.wait()


_ANY = pl.BlockSpec(memory_space=pl.ANY)


def _gathered_shapes(shards):
    return [jax.ShapeDtypeStruct((N_DEV,) + s.shape, s.dtype) for s in shards]


def _all_gather(shards, name):
    n = len(shards)

    def body(*refs):
        g = _Gather(refs[:n], refs[n:2 * n], *refs[2 * n:])
        g.start()
        g.forward()
        g.finish()

    return pl.pallas_call(
        body, name=name, in_specs=[_ANY] * n, out_specs=[_ANY] * n,
        out_shape=_gathered_shapes(shards), scratch_shapes=_comm_sems(n),
    )(*shards)


def _adamw_math(w, g, m, v):
    m2 = ADAM_B1 * m + (1.0 - ADAM_B1) * g
    v2 = ADAM_B2 * v + (1.0 - ADAM_B2) * (g * g)
    m_hat = m2 / (1.0 - ADAM_B1 ** ADAM_STEP)
    v_hat = v2 / (1.0 - ADAM_B2 ** ADAM_STEP)
    delta = -ADAM_LR * (m_hat / (jnp.sqrt(v_hat) + ADAM_EPS) + ADAM_WD * w)
    return delta, m2, v2


def _adamw(w, parts, m, v, name):
    R, C = w.shape
    br = 128 if R % 128 == 0 else R

    def body(w_ref, p_ref, m_ref, v_ref, g_ref, d_ref, nm_ref, nv_ref):
        g = p_ref[0].astype(F32)
        for s in range(1, N_DEV):
            g = g + p_ref[s].astype(F32)
        g_ref[...] = g
        d_ref[...], nm_ref[...], nv_ref[...] = _adamw_math(w_ref[...], g, m_ref[...], v_ref[...])

    spec = pl.BlockSpec((br, C), lambda i: (i, 0))
    return pl.pallas_call(
        body,
        name=name,
        grid=(R // br,),
        in_specs=[spec, pl.BlockSpec((N_DEV, br, C), lambda i: (0, i, 0)), spec, spec],
        out_specs=[spec] * 4,
        out_shape=[jax.ShapeDtypeStruct((R, C), F32)] * 4,
        compiler_params=_cparams(("arbitrary",)),
    )(w, parts, m, v)


_SHARDED = ("w_in", "w_out", "w_mq", "w_mkv", "w_mo", "w_up", "conv_w", "w_down")
_LATE = _SHARDED[1:]
_COL_SHARDED = ("w_in", "w_mkv", "w_up", "conv_w")
_REPLICATED = ("attn_norm_g", "b_forget", "fox_out_g", "sb_out_g", "xattn_norm_g", "mem_norm_g",
               "ffn_norm_g", "conv_b", "final_norm_g")
_WEIGHTS = ("attn_norm_g", "w_in", "b_forget", "fox_out_g", "sb_out_g", "w_out", "xattn_norm_g", "mem_norm_g",
            "w_mq", "w_mkv", "w_mo", "ffn_norm_g", "w_up", "conv_w", "conv_b", "w_down", "final_norm_g")


def _pack_rows(n):
    return -(-n // 128)


def _pack(vals, rows_total):
    parts = []
    for v in vals:
        flat = v.reshape(-1)
        parts.append(jnp.pad(flat, (0, _pack_rows(flat.shape[0]) * 128 - flat.shape[0])))
    flat = jnp.concatenate(parts)
    return jnp.pad(flat, (0, rows_total * 128 - flat.shape[0])).reshape(rows_total, 128)


def _unpack(packed, shapes):
    out = []
    r = 0
    for shp in shapes:
        n = 1
        for d in shp:
            n *= d
        out.append(packed[r:r + _pack_rows(n)].reshape(-1)[:n].reshape(shp))
        r += _pack_rows(n)
    return out


def _gathered_full(name, gathered):
    if name in _COL_SHARDED:
        return jnp.transpose(gathered, (1, 0, 2)).reshape(gathered.shape[1], -1)
    return gathered.reshape(-1, gathered.shape[2])


def _to_blocks(name, full):
    if name in _COL_SHARDED:
        r = full.shape[0]
        return jnp.transpose(full.reshape(r, N_DEV, -1), (1, 0, 2))
    return full.reshape(N_DEV, -1, full.shape[1])


def _grad_blocks(name, full):
    blocks = _to_blocks(name, full)
    return blocks if name == "conv_w" else blocks.astype(BF16)


def _step(args, tm, tq):
    w = {n: args[n] for n in _WEIGHTS}
    mom = {n: args["m_" + n] for n in _WEIGHTS}
    var = {n: args["v_" + n] for n in _WEIGHTS}
    x = args["x"][0]
    mem = args["mem"][0]
    target = args["loss_target"][0]

    def flat2(a):
        return a.reshape(a.shape[-2], a.shape[-1]) if a.ndim == 3 else a.reshape(1, -1)

    shards = {n: flat2(w[n]) if n == "conv_w" else flat2(w[n]).astype(BF16) for n in _SHARDED}
    (w_in_all,) = _all_gather([shards["w_in"]], "gather_w_in")
    p = {"w_in": _gathered_full("w_in", w_in_all)}
    for n in _REPLICATED:
        p[n] = flat2(w[n])

    loss_blk, grad_x, g = _local_step(x, mem, target, p, tm, tq, late={n: shards[n] for n in _LATE})

    parts = g["parts"]
    out = {}
    for n in _SHARDED:
        res = _adamw(flat2(w[n]), parts[n], flat2(mom[n]), flat2(var[n]), "adamw_" + n)
        out[n] = [r.reshape(w[n].shape) for r in res]

    shapes = [w[n].shape for n in _REPLICATED]
    rows = sum(_pack_rows(flat2(w[n]).shape[1]) for n in _REPLICATED) + 1
    rows = -(-rows // 8) * 8
    g_pack = _pack([g[n] for n in _REPLICATED] + [loss_blk[0:1, :]], rows)
    (g_all,) = _all_gather([g_pack], "gather_small")
    res = _adamw(_pack([w[n] for n in _REPLICATED], rows), g_all,
                 _pack([mom[n] for n in _REPLICATED], rows), _pack([var[n] for n in _REPLICATED], rows),
                 "adamw_small")
    n_rows_params = sum(_pack_rows(flat2(w[n]).shape[1]) for n in _REPLICATED)
    loss = res[0][n_rows_params, 0]
    unpacked = [_unpack(r, shapes) for r in res]
    for k, n in enumerate(_REPLICATED):
        out[n] = [unpacked[q][k] for q in range(4)]

    grads = [out[n][0] for n in _WEIGHTS]
    deltas = [out[n][1] for n in _WEIGHTS]
    new_m = [out[n][2] for n in _WEIGHTS]
    new_v = [out[n][3] for n in _WEIGHTS]
    return (loss, grad_x[None], *grads, *deltas, *new_m, *new_v)


def kernel(x, mem, attn_norm_g, w_in, b_forget, fox_out_g, sb_out_g, w_out, xattn_norm_g, mem_norm_g, w_mq, w_mkv, w_mo, ffn_norm_g, w_up, conv_w, conv_b, w_down, final_norm_g, loss_target, m_attn_norm_g, m_w_in, m_b_forget, m_fox_out_g, m_sb_out_g, m_w_out, m_xattn_norm_g, m_mem_norm_g, m_w_mq, m_w_mkv, m_w_mo, m_ffn_norm_g, m_w_up, m_conv_w, m_conv_b, m_w_down, m_final_norm_g, v_attn_norm_g, v_w_in, v_b_forget, v_fox_out_g, v_sb_out_g, v_w_out, v_xattn_norm_g, v_mem_norm_g, v_w_mq, v_w_mkv, v_w_mo, v_ffn_norm_g, v_w_up, v_conv_w, v_conv_b, v_w_down, v_final_norm_g):
    args = dict(locals())
    T = x.shape[1]
    return _step(args, tm=min(T, 512), tq=min(T, 256))
```

```python
import functools

import jax
import jax.numpy as jnp
from jax import lax
from jax.experimental import pallas as pl
from jax.experimental.pallas import tpu as pltpu

F32 = jnp.float32
BF16 = jnp.bfloat16
EPS = 1e-6
NEG = -1e30
LOG2E = 1.4426950408889634

HEAD_DIM = 64
N_FOX = 8
FOX_W = 512
QKV_W = 3072
IN_PAD = 3200
N_MEM_HEADS = 4
MEM_HD = 256
D_FF = 2816
FF_CHUNK = 256
N_DEV = 8

ADAM_LR = 0.001
ADAM_B1 = 0.9
ADAM_B2 = 0.999
ADAM_EPS = 1e-08
ADAM_WD = 0.01
ADAM_STEP = 10

SB_SUM_TERMS = 1

VMEM_LIMIT = 56 * 1024 * 1024
MESH = pl.DeviceIdType.MESH


def _cparams(sem=None):
    return pltpu.CompilerParams(dimension_semantics=sem, vmem_limit_bytes=VMEM_LIMIT)


def _nt(a, b):
    return lax.dot_general(a, b, (((1,), (1,)), ((), ())), preferred_element_type=F32)


def _tn(a, b):
    return lax.dot_general(a, b, (((0,), (0,)), ((), ())), preferred_element_type=F32)


def _nn(a, b):
    return jnp.dot(a, b, preferred_element_type=F32)


def _split_dot(a, m01, terms):
    out = None
    r = a
    for t in range(terms):
        p = r.astype(BF16)
        d = _nn(p, m01)
        out = d if out is None else out + d
        if t + 1 < terms:
            r = r - p.astype(F32)
    return out


def _rstd(xv):
    return lax.rsqrt(jnp.mean(xv * xv, axis=-1, keepdims=True) + EPS)


def _norm_bwd(xv, g, dh):
    r = _rstd(xv)
    xhat = xv * r
    dxhat = dh * g
    dx = r * (dxhat - xhat * jnp.mean(dxhat * xhat, axis=-1, keepdims=True))
    dg = jnp.sum(dh * xhat, axis=0, keepdims=True)
    return dx, dg


def _tile_div(n, cap):
    best = None
    for d in range(128, min(n, cap) + 1, 128):
        if n % d == 0:
            best = d
    assert best is not None, n
    return best


def _inproj_fwd(x, g1, w_qkv, w_f_t, b_f, tm):
    T, D = x.shape
    N = w_qkv.shape[1]
    H = w_f_t.shape[0]

    def body(x_ref, g_ref, w_ref, wf_ref, b_ref, proj_ref, h_ref, xf_ref, c_ref, carry_ref):
        i = pl.program_id(0)

        @pl.when(i == 0)
        def _():
            carry_ref[...] = jnp.zeros_like(carry_ref)

        xv = x_ref[...]
        h = (xv * _rstd(xv) * g_ref[...]).astype(BF16)
        h_ref[...] = h
        for n0 in range(0, N, 512):
            proj_ref[:, n0:n0 + 512] = _nn(h, w_ref[:, n0:n0 + 512]).astype(BF16)
        xf = _nt(wf_ref[...], h) + b_ref[...]
        xf_ref[...] = xf
        logf = jnp.minimum(xf, 0.0) - jnp.log1p(jnp.exp(-jnp.abs(xf)))
        row = lax.broadcasted_iota(jnp.int32, (tm, tm), 0)
        col = lax.broadcasted_iota(jnp.int32, (tm, tm), 1)
        upper = jnp.where(row <= col, 1.0, 0.0).astype(BF16)
        c = _split_dot(logf, upper, 3) + carry_ref[...]
        c_ref[...] = c
        carry_ref[...] = c[:, tm - 1:tm]

    return pl.pallas_call(
        body,
        name="inproj_fwd",
        grid=(T // tm,),
        in_specs=[
            pl.BlockSpec((tm, D), lambda i: (i, 0)),
            pl.BlockSpec((1, D), lambda i: (0, 0)),
            pl.BlockSpec((D, N), lambda i: (0, 0)),
            pl.BlockSpec((H, D), lambda i: (0, 0)),
            pl.BlockSpec((H, 1), lambda i: (0, 0)),
        ],
        out_specs=[
            pl.BlockSpec((tm, N), lambda i: (i, 0)),
            pl.BlockSpec((tm, D), lambda i: (i, 0)),
            pl.BlockSpec((H, tm), lambda i: (0, i)),
            pl.BlockSpec((H, tm), lambda i: (0, i)),
        ],
        out_shape=[
            jax.ShapeDtypeStruct((T, N), BF16),
            jax.ShapeDtypeStruct((T, D), BF16),
            jax.ShapeDtypeStruct((H, T), F32),
            jax.ShapeDtypeStruct((H, T), F32),
        ],
        scratch_shapes=[pltpu.VMEM((H, 1), F32)],
        compiler_params=_cparams(("arbitrary",)),
    )(x, g1, w_qkv, w_f_t, b_f)


def _head_q(q, hh, lane):
    hmask = (lane >= HEAD_DIM * hh) & (lane < HEAD_DIM * (hh + 1))
    qh = jnp.where(hmask, q.astype(F32), 0.0) * (HEAD_DIM ** -0.5)
    return qh.astype(BF16), hmask


def _pipeline3(n, stage_a, stage_b, stage_c, diag_last, alive=None):
    stage_a(0, 0)
    if diag_last:
        @pl.when(n == 1)
        def _():
            stage_b(0, 0, True)

        @pl.when(n >= 2)
        def _():
            stage_b(0, 0, False)
    else:
        stage_b(0, 0, True)

    @pl.when(n >= 2)
    def _():
        stage_a(1, 1)

    def pair(m, carry):
        t = 2 + 2 * m
        stage_c(t - 2)
        stage_b(t - 1, 1, False)
        stage_a(t, 0)
        stage_c(t - 1)
        stage_b(t, 0, False)
        stage_a(t + 1, 1)
        return carry

    pairs = (n - 2) // 2
    if alive is None:
        lax.fori_loop(0, pairs, pair, 0)
        go_on = True
        done = n
    else:
        def more(state):
            return (state[0] < pairs) & state[1]

        def step(state):
            pair(state[0], 0)
            return state[0] + 1, alive()

        m_end, go_on = lax.while_loop(more, step, (jnp.int32(0), jnp.bool_(True)))
        done = jnp.where(go_on, n, 2 * m_end)
    odd = n % 2 == 1

    @pl.when((n >= 3) & odd & go_on)
    def _():
        stage_c(n - 3)
        stage_b(n - 2, 1, False)
        stage_a(n - 1, 0)

    @pl.when((n >= 2) & odd & go_on)
    def _():
        stage_c(n - 2)
        stage_b(n - 1, 0, diag_last)

    @pl.when((n >= 2) & jnp.logical_not(odd) & go_on)
    def _():
        stage_c(n - 2)
        stage_b(n - 1, 1, diag_last)

    if alive is None:
        stage_c(n - 1)
    else:
        @pl.when(go_on)
        def _():
            stage_c(n - 1)

    return done


def _lanes2(x):
    return jnp.concatenate([x, x], axis=1)


def _lanes_to_rows(vec, eye):
    return jnp.sum(jnp.where(eye, jnp.broadcast_to(vec, eye.shape), 0.0), axis=1, keepdims=True)


def _rows_to_lanes(rep, eye):
    return jnp.sum(jnp.where(eye, _lanes2(rep), 0.0), axis=0, keepdims=True)


FOX_DEAD = -110.0


def _fox_key_norms(k_ref, kn_s, lane):
    T = k_ref.shape[0]
    rows = min(T, 512)
    for hh in range(2):
        hmask = (lane >= HEAD_DIM * hh) & (lane < HEAD_DIM * (hh + 1))

        def chunk(n, best, hmask=hmask):
            kf = jnp.where(hmask, k_ref[pl.ds(pl.multiple_of(n * rows, rows), rows), :].astype(F32), 0.0)
            sq = jnp.sum(kf * kf, axis=1, keepdims=True)
            return jnp.maximum(best, jnp.max(sq, axis=0, keepdims=True))

        best = lax.fori_loop(0, T // rows, chunk, jnp.zeros((1, 1), F32))
        kn_s[hh] = jnp.broadcast_to(best, kn_s.shape[1:])


def _fox_live_blocks(i, qh_s, kn_s, cq_ref, cke_ref):
    nq = cke_ref.shape[-1]
    jj = lax.broadcasted_iota(jnp.int32, (1, nq), 1)
    first = None
    for hh in range(2):
        qf = qh_s[hh].astype(F32)
        qn = jnp.max(jnp.sum(qf * qf, axis=1, keepdims=True), axis=0, keepdims=True)
        zb = jnp.sqrt(qn * kn_s[hh][0:1, 0:1]) * 1.001
        bound = (2.0 * zb + cq_ref[hh][:, 0:1]) - cke_ref[hh]
        live = (bound >= FOX_DEAD) & (jj <= i)
        f = jnp.min(jnp.where(live, jj, i).astype(F32), axis=1, keepdims=True)
        first = f if first is None else jnp.minimum(first, f)
    return i + 1 - first[0, 0].astype(jnp.int32)


def _ride_along(exchange, at_start, at_middle, at_end):
    @pl.when(at_start)
    def _():
        exchange.start()

    if at_middle is not None:
        @pl.when(at_middle)
        def _():
            exchange.forward()

    def finish():
        @pl.when(at_end)
        def _():
            exchange.finish()

    return finish


def _fox_fwd(proj, c_col, c_row, c_ends, tq, gather=()):
    T = proj.shape[0]
    assert tq == 256
    nq = T // tq
    ng = len(gather)

    def body(*refs):
        q_ref, k_ref, v_ref, cq_ref, ck_ref, cke_ref = refs[:6]
        o_ref, lse_ref = refs[6 + ng:8 + ng]
        qh_s, cq_s, z_s, p_s, al_s, m_s, l_s, acc_s, kn_s = refs[8 + 2 * ng:17 + 2 * ng]
        i = pl.program_id(1)
        if ng:
            pair = pl.program_id(0)
            finish = _ride_along(_Gather(refs[6:6 + ng], refs[8 + ng:8 + 2 * ng], *refs[17 + 2 * ng:]),
                                 (pair == 0) & (i == 0), (pair == 1) & (i == 0), (pair == 3) & (i == nq - 1))
        lane = lax.broadcasted_iota(jnp.int32, (1, 128), 1)
        row = lax.broadcasted_iota(jnp.int32, (tq, tq), 0)
        col = lax.broadcasted_iota(jnp.int32, (tq, tq), 1)
        ones = jnp.ones((tq, 128), BF16)

        @pl.when(i == 0)
        def _():
            _fox_key_norms(k_ref, kn_s, lane)

        q = q_ref[...]
        for hh in range(2):
            qh_s[hh] = _head_q(q, hh, lane)[0]
            cq_s[hh] = jnp.broadcast_to(_lanes_to_rows(cq_ref[hh], row == col), (tq, tq))
        m_s[...] = jnp.full(m_s.shape, NEG, F32)
        l_s[...] = jnp.zeros_like(l_s)
        acc_s[...] = jnp.zeros_like(acc_s)

        def rows(t):
            return pl.ds(pl.multiple_of((i - t) * tq, tq), tq)

        def stage_a(t, slot):
            k = k_ref[rows(t), :]
            for hh in range(2):
                z_s[slot, hh] = _nt(qh_s[hh], k)

        def stage_b(t, slot, diag):
            for hh in range(2):
                s = z_s[slot, hh] + cq_s[hh] - ck_ref[hh, :, rows(t)]
                if diag:
                    s = jnp.where(col <= row, s, NEG)
                m = m_s[hh]
                half = jnp.maximum(s[:, :128], s[:, 128:])
                m_new = jnp.maximum(m, jnp.max(half, axis=1, keepdims=True))
                alpha = jnp.exp(m - m_new)
                p = jnp.exp(s - _lanes2(m_new)).astype(BF16)
                l_s[hh] = alpha * l_s[hh] + _nn(p, ones)
                m_s[hh] = m_new
                al_s[hh] = alpha
                p_s[hh] = p

        def stage_c(t):
            v = v_ref[rows(t), :]
            for hh in range(2):
                acc_s[hh] = al_s[hh] * acc_s[hh] + _nn(p_s[hh], v)

        _pipeline3(_fox_live_blocks(i, qh_s, kn_s, cq_ref, cke_ref), stage_a, stage_b, stage_c, False)
        l0, l1 = l_s[0], l_s[1]
        o_ref[...] = jnp.where(lane < HEAD_DIM, acc_s[0] / l0, acc_s[1] / l1)
        lse_ref[0] = _rows_to_lanes(m_s[0] + jnp.log(l0), row == col)
        lse_ref[1] = _rows_to_lanes(m_s[1] + jnp.log(l1), row == col)
        if ng:
            finish()

    res = pl.pallas_call(
        body,
        name="fox_fwd",
        grid=(4, nq),
        in_specs=[
            pl.BlockSpec((tq, 128), lambda p, i: (i, p)),
            pl.BlockSpec((T, 128), lambda p, i: (0, 4 + p)),
            pl.BlockSpec((T, 128), lambda p, i: (0, 8 + p)),
            pl.BlockSpec((2, 1, tq), lambda p, i: (p, 0, i)),
            pl.BlockSpec((2, 1, T), lambda p, i: (p, 0, 0)),
            pl.BlockSpec((2, 1, nq), lambda p, i: (p, 0, 0)),
        ] + [_ANY] * ng,
        out_specs=[
            pl.BlockSpec((tq, 128), lambda p, i: (i, p)),
            pl.BlockSpec((2, 1, tq), lambda p, i: (p, 0, i)),
        ] + [_ANY] * ng,
        out_shape=[
            jax.ShapeDtypeStruct((T, FOX_W), F32),
            jax.ShapeDtypeStruct((N_FOX, 1, T), F32),
        ] + _gathered_shapes(gather),
        scratch_shapes=[
            pltpu.VMEM((2, tq, 128), BF16),
            pltpu.VMEM((2, tq, tq), F32),
            pltpu.VMEM((2, 2, tq, tq), F32),
            pltpu.VMEM((2, tq, tq), BF16),
            pltpu.VMEM((2, tq, 128), F32),
            pltpu.VMEM((2, tq, 128), F32),
            pltpu.VMEM((2, tq, 128), F32),
            pltpu.VMEM((2, tq, 128), F32),
            pltpu.VMEM((2, 8, 128), F32),
        ] + (_comm_sems(ng) if ng else []),
        compiler_params=_cparams(("arbitrary", "arbitrary")),
    )(proj, proj, proj, c_col, c_row, c_ends, *gather)
    res = list(res)
    return res[0], res[1], res[2:]


def _sb_logs(zn, strict):
    e = jnp.exp2(jnp.abs(zn) * (-LOG2E))
    L = jnp.minimum(zn, 0.0) - jnp.log(1.0 + e)
    G = L - zn
    if strict is not None:
        L = jnp.where(strict, L, 0.0)
    return L, G


SB_DEAD = -110.0


def _sb_fwd(proj, tq):
    T = proj.shape[0]
    nq = T // tq

    def body(q_ref, k_ref, v_ref, o_ref, ltot_ref, live_ref, qh_s, z_s, g_s, tot_s, run_s, acc_s):
        i = pl.program_id(1)
        lane = lax.broadcasted_iota(jnp.int32, (1, 128), 1)
        row = lax.broadcasted_iota(jnp.int32, (tq, tq), 0)
        col = lax.broadcasted_iota(jnp.int32, (tq, tq), 1)
        strict = col < row
        later = jnp.where(row > col, 1.0, 0.0).astype(BF16)
        q = q_ref[...]
        for hh in range(2):
            qh_s[hh] = -_head_q(q, hh, lane)[0]
        run_s[...] = jnp.zeros_like(run_s)
        acc_s[...] = jnp.zeros_like(acc_s)

        def rows(t):
            return pl.ds(pl.multiple_of((i - t) * tq, tq), tq)

        def stage_a(t, slot):
            k = k_ref[rows(t), :]
            for hh in range(2):
                z_s[slot, hh] = _nt(qh_s[hh], k)

        def stage_b(t, slot, diag):
            for hh in range(2):
                L, g = _sb_logs(z_s[slot, hh], strict if diag else None)
                if diag:
                    g = jnp.where(strict, g, NEG)
                after = _split_dot(L, later, SB_SUM_TERMS)
                g_s[hh] = g + after
                first = L[:, 0:1]
                if SB_SUM_TERMS == 1:
                    first = first.astype(BF16).astype(F32)
                tot_s[hh] = jnp.broadcast_to(after[:, 0:1] + first, (tq, 128))

        def stage_c(t):
            v = v_ref[rows(t), :]
            for hh in range(2):
                run = run_s[hh]
                a = jnp.exp(g_s[hh] + _lanes2(run))
                acc_s[hh] += _nn(a.astype(BF16), v)
                run_s[hh] = run + tot_s[hh]

        def alive():
            return jnp.max(jnp.maximum(run_s[0], run_s[1])) > SB_DEAD

        done = _pipeline3(i + 1, stage_a, stage_b, stage_c, False, alive)
        ltot_ref[0] = _rows_to_lanes(run_s[0], row == col)
        ltot_ref[1] = _rows_to_lanes(run_s[1], row == col)
        o_ref[...] = jnp.where(lane < HEAD_DIM, acc_s[0], acc_s[1])
        at = lax.broadcasted_iota(jnp.int32, (1, nq), 1)

        @pl.when(i == 0)
        def _():
            live_ref[0] = jnp.zeros((1, nq), F32)

        live_ref[0] = jnp.where(at == i, done.astype(F32), live_ref[0])

    return pl.pallas_call(
        body,
        name="sb_fwd",
        grid=(4, nq),
        in_specs=[
            pl.BlockSpec((tq, 128), lambda p, i: (i, 12 + p)),
            pl.BlockSpec((T, 128), lambda p, i: (0, 16 + p)),
            pl.BlockSpec((T, 128), lambda p, i: (0, 20 + p)),
        ],
        out_specs=[
            pl.BlockSpec((tq, 128), lambda p, i: (i, p)),
            pl.BlockSpec((2, 1, tq), lambda p, i: (p, 0, i)),
            pl.BlockSpec((1, 1, nq), lambda p, i: (p, 0, 0)),
        ],
        out_shape=[
            jax.ShapeDtypeStruct((T, FOX_W), F32),
            jax.ShapeDtypeStruct((N_FOX, 1, T), F32),
            jax.ShapeDtypeStruct((N_FOX // 2, 1, nq), F32),
        ],
        scratch_shapes=[
            pltpu.VMEM((2, tq, 128), BF16),
            pltpu.VMEM((2, 2, tq, tq), F32),
            pltpu.VMEM((2, tq, tq), F32),
            pltpu.VMEM((2, tq, 128), F32),
            pltpu.VMEM((2, tq, 128), F32),
            pltpu.VMEM((2, tq, 128), F32),
        ],
        compiler_params=_cparams(("arbitrary", "arbitrary")),
    )(proj, proj, proj)


def _post_attn_fwd(fox_o, sb_o, gf, gs, w_out, x, tm):
    T, D = x.shape

    def body(f_ref, s_ref, gf_ref, gs_ref, w_ref, x_ref, x1_ref, mix_ref):
        f = f_ref[...]
        s = s_ref[...]
        mix_ref[:, :FOX_W] = (f * _rstd(f) * gf_ref[...]).astype(BF16)
        mix_ref[:, FOX_W:] = (s * _rstd(s) * gs_ref[...]).astype(BF16)
        x1_ref[...] = x_ref[...] + _nn(mix_ref[...], w_ref[...])

    return pl.pallas_call(
        body,
        name="post_attn_fwd",
        grid=(T // tm,),
        in_specs=[
            pl.BlockSpec((tm, FOX_W), lambda i: (i, 0)),
            pl.BlockSpec((tm, FOX_W), lambda i: (i, 0)),
            pl.BlockSpec((1, FOX_W), lambda i: (0, 0)),
            pl.BlockSpec((1, FOX_W), lambda i: (0, 0)),
            pl.BlockSpec((D, D), lambda i: (0, 0)),
            pl.BlockSpec((tm, D), lambda i: (i, 0)),
        ],
        out_specs=[
            pl.BlockSpec((tm, D), lambda i: (i, 0)),
            pl.BlockSpec((tm, D), lambda i: (i, 0)),
        ],
        out_shape=[jax.ShapeDtypeStruct((T, D), F32), jax.ShapeDtypeStruct((T, D), BF16)],
        compiler_params=_cparams(("arbitrary",)),
    )(fox_o, sb_o, gf, gs, w_out, x)


def _mem_kv_fwd(mem, gm, w_mkv):
    M, D = mem.shape
    N = w_mkv.shape[1]

    def body(mem_ref, g_ref, w_ref, m_ref, kv_ref):
        mv = mem_ref[...]
        m = (mv * _rstd(mv) * g_ref[...]).astype(BF16)
        m_ref[...] = m
        for n0 in range(0, N, 512):
            kv_ref[:, n0:n0 + 512] = _nn(m, w_ref[:, n0:n0 + 512]).astype(BF16)

    return pl.pallas_call(
        body,
        name="mem_kv_fwd",
        out_shape=[jax.ShapeDtypeStruct((M, D), BF16), jax.ShapeDtypeStruct((M, N), BF16)],
        compiler_params=_cparams(),
    )(mem, gm, w_mkv)


def _xattn_probs(qb, kv, h):
    k = kv[:, h * MEM_HD:(h + 1) * MEM_HD]
    s = _nt(qb[:, h * MEM_HD:(h + 1) * MEM_HD], k) * (MEM_HD ** -0.5)
    s = s - jnp.max(s, axis=1, keepdims=True)
    p = jnp.exp(s)
    return p / jnp.sum(p, axis=1, keepdims=True)


def _xattn_fwd(x1, g2, w_mq, kv, w_mo, tm):
    T, D = x1.shape
    M = kv.shape[0]

    def body(x_ref, g_ref, wq_ref, kv_ref, wo_ref, x2_ref, h_ref, q_ref, om_ref):
        xv = x_ref[...]
        h = (xv * _rstd(xv) * g_ref[...]).astype(BF16)
        h_ref[...] = h
        q_ref[...] = _nn(h, wq_ref[...]).astype(BF16)
        qb = q_ref[...]
        kvv = kv_ref[...]
        for hd in range(N_MEM_HEADS):
            p = _xattn_probs(qb, kvv, hd)
            v = kvv[:, D + hd * MEM_HD:D + (hd + 1) * MEM_HD]
            om_ref[:, hd * MEM_HD:(hd + 1) * MEM_HD] = _nn(p.astype(BF16), v).astype(BF16)
        x2_ref[...] = xv + _nn(om_ref[...], wo_ref[...])

    return pl.pallas_call(
        body,
        name="xattn_fwd",
        grid=(T // tm,),
        in_specs=[
            pl.BlockSpec((tm, D), lambda i: (i, 0)),
            pl.BlockSpec((1, D), lambda i: (0, 0)),
            pl.BlockSpec((D, D), lambda i: (0, 0)),
            pl.BlockSpec((M, 2 * D), lambda i: (0, 0)),
            pl.BlockSpec((D, D), lambda i: (0, 0)),
        ],
        out_specs=[pl.BlockSpec((tm, D), lambda i: (i, 0))] * 4,
        out_shape=[jax.ShapeDtypeStruct((T, D), F32)] + [jax.ShapeDtypeStruct((T, D), BF16)] * 3,
        compiler_params=_cparams(("arbitrary",)),
    )(x1, g2, w_mq, kv, w_mo)


def _conv_taps(ext_ref, tm, back):
    if back:
        return ext_ref[pl.ds(6, tm), :], ext_ref[pl.ds(7, tm), :], ext_ref[pl.ds(8, tm), :]
    return ext_ref[pl.ds(0, tm), :], ext_ref[pl.ds(1, tm), :], ext_ref[pl.ds(2, tm), :]


def _ffn_fwd(x2, g3, w_up, conv_w, conv_b, w_down, tm):
    T, D = x2.shape
    fc = FF_CHUNK
    nj = D_FF // fc

    def body(x_ref, g_ref, wg_ref, wv_ref, cwg_ref, cwv_ref, cbg_ref, cbv_ref, wd_ref,
             x3_ref, h_ref, ug_ref, uv_ref, yg_ref, yv_ref, a_ref, acc_ref, carry_ref, ext_ref):
        i = pl.program_id(0)
        j = pl.program_id(1)

        @pl.when(j == 0)
        def _():
            xv = x_ref[...]
            h_ref[...] = (xv * _rstd(xv) * g_ref[...]).astype(BF16)
            acc_ref[...] = xv

        @pl.when(i == 0)
        def _():
            carry_ref[j] = jnp.zeros((2, 8, fc), F32)

        h = h_ref[...]
        halves = []
        for part, (w_ref, cw_ref, cb_ref, u_ref, y_ref) in enumerate(
                ((wg_ref, cwg_ref, cbg_ref, ug_ref, yg_ref), (wv_ref, cwv_ref, cbv_ref, uv_ref, yv_ref))):
            u = _nn(h, w_ref[...])
            u_ref[...] = u.astype(BF16)
            ext = ext_ref.at[part]
            ext[pl.ds(0, 8), :] = carry_ref[j, part]
            ext[pl.ds(8, tm), :] = u
            carry_ref[j, part] = u[tm - 8:, :]
            u2, u1, u0 = _conv_taps(ext, tm, True)
            cw = cw_ref[...]
            y = cb_ref[...] + cw[0:1] * u2 + cw[1:2] * u1 + cw[2:3] * u0
            y_ref[...] = y.astype(BF16)
            halves.append(y)
        gate, val = halves
        a = (gate * jax.nn.sigmoid(gate) * val).astype(BF16)
        a_ref[...] = a
        acc_ref[...] += _nn(a, wd_ref[...])

        @pl.when(j == nj - 1)
        def _():
            x3_ref[...] = acc_ref[...]

    return pl.pallas_call(
        body,
        name="ffn_fwd",
        grid=(T // tm, nj),
        in_specs=[
            pl.BlockSpec((tm, D), lambda i, j: (i, 0)),
            pl.BlockSpec((1, D), lambda i, j: (0, 0)),
            pl.BlockSpec((D, fc), lambda i, j: (0, j)),
            pl.BlockSpec((D, fc), lambda i, j: (0, nj + j)),
            pl.BlockSpec((3, fc), lambda i, j: (0, j)),
            pl.BlockSpec((3, fc), lambda i, j: (0, nj + j)),
            pl.BlockSpec((1, fc), lambda i, j: (0, j)),
            pl.BlockSpec((1, fc), lambda i, j: (0, nj + j)),
            pl.BlockSpec((fc, D), lambda i, j: (j, 0)),
        ],
        out_specs=[
            pl.BlockSpec((tm, D), lambda i, j: (i, 0)),
            pl.BlockSpec((tm, D), lambda i, j: (i, 0)),
        ] + [pl.BlockSpec((tm, fc), lambda i, j: (i, j))] * 5,
        out_shape=[
            jax.ShapeDtypeStruct((T, D), F32),
            jax.ShapeDtypeStruct((T, D), BF16),
        ] + [jax.ShapeDtypeStruct((T, D_FF), BF16)] * 5,
        scratch_shapes=[
            pltpu.VMEM((tm, D), F32),
            pltpu.VMEM((nj, 2, 8, fc), F32),
            pltpu.VMEM((2, tm + 8, fc), F32),
        ],
        compiler_params=_cparams(("arbitrary", "arbitrary")),
    )(x2, g3, w_up, w_up, conv_w, conv_w, conv_b, conv_b, w_down)


def _loss_head(x3, gfin, target, tm):
    T, D = x3.shape

    def body(x_ref, g_ref, t_ref, dx_ref, loss_ref, dg_ref):
        i = pl.program_id(0)

        @pl.when(i == 0)
        def _():
            loss_ref[...] = jnp.zeros_like(loss_ref)
            dg_ref[...] = jnp.zeros_like(dg_ref)

        xv = x_ref[...]
        g = g_ref[...]
        r = _rstd(xv)
        xhat = xv * r
        err = xhat * g - t_ref[...]
        part = jnp.sum(jnp.sum(err * err, axis=1, keepdims=True), axis=0, keepdims=True) * (0.5 / D)
        loss_ref[...] += jnp.broadcast_to(part, loss_ref.shape)
        dy = err * (1.0 / D)
        dg_ref[...] += jnp.sum(dy * xhat, axis=0, keepdims=True)
        dxhat = dy * g
        dx_ref[...] = r * (dxhat - xhat * jnp.mean(dxhat * xhat, axis=-1, keepdims=True))

    return pl.pallas_call(
        body,
        name="loss_head",
        grid=(T // tm,),
        in_specs=[
            pl.BlockSpec((tm, D), lambda i: (i, 0)),
            pl.BlockSpec((1, D), lambda i: (0, 0)),
            pl.BlockSpec((tm, D), lambda i: (i, 0)),
        ],
        out_specs=[
            pl.BlockSpec((tm, D), lambda i: (i, 0)),
            pl.BlockSpec((8, 128), lambda i: (0, 0)),
            pl.BlockSpec((1, D), lambda i: (0, 0)),
        ],
        out_shape=[
            jax.ShapeDtypeStruct((T, D), F32),
            jax.ShapeDtypeStruct((8, 128), F32),
            jax.ShapeDtypeStruct((1, D), F32),
        ],
        compiler_params=_cparams(("arbitrary",)),
    )(x3, gfin, target)


def _ffn_bwd(dx3, x2, g3, ug, uv, yg, yv, conv_w, w_down, w_up, tm):
    T, D = x2.shape
    fc = FF_CHUNK
    nj = D_FF // fc
    nt = T // tm

    def rev(i):
        return nt - 1 - i

    def body(dx3_ref, x_ref, g_ref, ug_ref, uv_ref, yg_ref, yv_ref, cwg_ref, cwv_ref,
             wd_ref, wug_ref, wuv_ref,
             dx2_ref, dug_ref, duv_ref, dg_ref, dcg_ref, dcv_ref,
             acc_ref, carry_ref, ext_ref):
        i = pl.program_id(0)
        j = pl.program_id(1)
        cols = pl.ds(pl.multiple_of(j * fc, fc), fc)

        @pl.when(j == 0)
        def _():
            acc_ref[...] = jnp.zeros_like(acc_ref)

        @pl.when((i == 0) & (j == 0))
        def _():
            dg_ref[...] = jnp.zeros_like(dg_ref)
            dcg_ref[...] = jnp.zeros_like(dcg_ref)
            dcv_ref[...] = jnp.zeros_like(dcv_ref)

        @pl.when(i == 0)
        def _():
            carry_ref[j] = jnp.zeros((2, 8, fc), F32)

        da = _nt(dx3_ref[...].astype(BF16), wd_ref[...])
        gate = yg_ref[...].astype(F32)
        val = yv_ref[...].astype(F32)
        sig = jax.nn.sigmoid(gate)
        silu = gate * sig
        dys = (da * val * (sig * (1.0 + gate * (1.0 - sig))), da * silu)
        for part, (dy, u_ref, cw_ref, du_ref, wu_ref, dc_ref) in enumerate(
                ((dys[0], ug_ref, cwg_ref, dug_ref, wug_ref, dcg_ref),
                 (dys[1], uv_ref, cwv_ref, duv_ref, wuv_ref, dcv_ref))):
            ext = ext_ref.at[part]
            ext[pl.ds(0, tm), :] = dy
            ext[pl.ds(tm, 8), :] = carry_ref[j, part]
            carry_ref[j, part] = dy[:8, :]
            d0, d1, d2 = _conv_taps(ext, tm, False)
            u = u_ref[...].astype(F32)
            upd = jnp.concatenate([
                jnp.sum(u * d2, axis=0, keepdims=True),
                jnp.sum(u * d1, axis=0, keepdims=True),
                jnp.sum(u * d0, axis=0, keepdims=True),
                jnp.sum(d0, axis=0, keepdims=True),
                jnp.zeros((4, fc), F32)], axis=0)
            dc_ref[:, cols] += upd
            cw = cw_ref[...]
            du = (cw[2:3] * d0 + cw[1:2] * d1 + cw[0:1] * d2).astype(BF16)
            du_ref[...] = du
            acc_ref[...] += _nt(du, wu_ref[...])

        @pl.when(j == nj - 1)
        def _():
            dx, dg = _norm_bwd(x_ref[...], g_ref[...], acc_ref[...])
            dx2_ref[...] = dx3_ref[...] + dx
            dg_ref[...] += dg

    return pl.pallas_call(
        body,
        name="ffn_bwd",
        grid=(nt, nj),
        in_specs=[
            pl.BlockSpec((tm, D), lambda i, j: (rev(i), 0)),
            pl.BlockSpec((tm, D), lambda i, j: (rev(i), 0)),
            pl.BlockSpec((1, D), lambda i, j: (0, 0)),
            pl.BlockSpec((tm, fc), lambda i, j: (rev(i), j)),
            pl.BlockSpec((tm, fc), lambda i, j: (rev(i), j)),
            pl.BlockSpec((tm, fc), lambda i, j: (rev(i), j)),
            pl.BlockSpec((tm, fc), lambda i, j: (rev(i), j)),
            pl.BlockSpec((3, fc), lambda i, j: (0, j)),
            pl.BlockSpec((3, fc), lambda i, j: (0, nj + j)),
            pl.BlockSpec((fc, D), lambda i, j: (j, 0)),
            pl.BlockSpec((D, fc), lambda i, j: (0, j)),
            pl.BlockSpec((D, fc), lambda i, j: (0, nj + j)),
        ],
        out_specs=[
            pl.BlockSpec((tm, D), lambda i, j: (rev(i), 0)),
            pl.BlockSpec((tm, fc), lambda i, j: (rev(i), j)),
            pl.BlockSpec((tm, fc), lambda i, j: (rev(i), j)),
            pl.BlockSpec((1, D), lambda i, j: (0, 0)),
            pl.BlockSpec((8, D_FF), lambda i, j: (0, 0)),
            pl.BlockSpec((8, D_FF), lambda i, j: (0, 0)),
        ],
        out_shape=[
            jax.ShapeDtypeStruct((T, D), F32),
            jax.ShapeDtypeStruct((T, D_FF), BF16),
            jax.ShapeDtypeStruct((T, D_FF), BF16),
            jax.ShapeDtypeStruct((1, D), F32),
            jax.ShapeDtypeStruct((8, D_FF), F32),
            jax.ShapeDtypeStruct((8, D_FF), F32),
        ],
        scratch_shapes=[
            pltpu.VMEM((tm, D), F32),
            pltpu.VMEM((nj, 2, 8, fc), F32),
            pltpu.VMEM((2, tm + 8, fc), F32),
        ],
        compiler_params=_cparams(("arbitrary", "arbitrary")),
    )(dx3, x2, g3, ug, uv, yg, yv, conv_w, conv_w, w_down, w_up, w_up)


def _xattn_bwd(dx2, x1, g2, qb, kv, w_mo, w_mq, tm):
    T, D = x1.shape
    M = kv.shape[0]

    def body(dx2_ref, x_ref, g_ref, q_ref, kv_ref, wo_ref, wq_ref, dx1_ref, dq_ref, dkv_ref, dg_ref):
        i = pl.program_id(0)

        @pl.when(i == 0)
        def _():
            dkv_ref[...] = jnp.zeros_like(dkv_ref)
            dg_ref[...] = jnp.zeros_like(dg_ref)

        dxv = dx2_ref[...]
        dom = _nt(dxv.astype(BF16), wo_ref[...]).astype(BF16)
        qb_ = q_ref[...]
        kvv = kv_ref[...]
        for hd in range(N_MEM_HEADS):
            sl = slice(hd * MEM_HD, (hd + 1) * MEM_HD)
            vsl = slice(D + hd * MEM_HD, D + (hd + 1) * MEM_HD)
            p = _xattn_probs(qb_, kvv, hd)
            dp = _nt(dom[:, sl], kvv[:, vsl])
            ds = (p * (dp - jnp.sum(p * dp, axis=1, keepdims=True)) * (MEM_HD ** -0.5)).astype(BF16)
            dq_ref[:, sl] = _nn(ds, kvv[:, sl]).astype(BF16)
            dkv_ref[:, sl] += _tn(ds, qb_[:, sl])
            dkv_ref[:, vsl] += _tn(p.astype(BF16), dom[:, sl])
        dh = _nt(dq_ref[...], wq_ref[...])
        dx, dg = _norm_bwd(x_ref[...], g_ref[...], dh)
        dx1_ref[...] = dxv + dx
        dg_ref[...] += dg

    return pl.pallas_call(
        body,
        name="xattn_bwd",
        grid=(T // tm,),
        in_specs=[
            pl.BlockSpec((tm, D), lambda i: (i, 0)),
            pl.BlockSpec((tm, D), lambda i: (i, 0)),
            pl.BlockSpec((1, D), lambda i: (0, 0)),
            pl.BlockSpec((tm, D), lambda i: (i, 0)),
            pl.BlockSpec((M, 2 * D), lambda i: (0, 0)),
            pl.BlockSpec((D, D), lambda i: (0, 0)),
            pl.BlockSpec((D, D), lambda i: (0, 0)),
        ],
        out_specs=[
            pl.BlockSpec((tm, D), lambda i: (i, 0)),
            pl.BlockSpec((tm, D), lambda i: (i, 0)),
            pl.BlockSpec((M, 2 * D), lambda i: (0, 0)),
            pl.BlockSpec((1, D), lambda i: (0, 0)),
        ],
        out_shape=[
            jax.ShapeDtypeStruct((T, D), F32),
            jax.ShapeDtypeStruct((T, D), BF16),
            jax.ShapeDtypeStruct((M, 2 * D), F32),
            jax.ShapeDtypeStruct((1, D), F32),
        ],
        compiler_params=_cparams(("arbitrary",)),
    )(dx2, x1, g2, qb, kv, w_mo, w_mq)


def _mem_kv_bwd(mem, gm, mb, dkv, w_mkv):
    M, D = mem.shape
    N = dkv.shape[1]

    def body(mem_ref, g_ref, m_ref, dkv_ref, w_ref, dw_ref, dg_ref):
        dkvb = dkv_ref[...].astype(BF16)
        for n0 in range(0, N, 512):
            dw_ref[:, n0:n0 + 512] = _tn(m_ref[...], dkvb[:, n0:n0 + 512]).astype(BF16)
        dm = _nt(dkvb, w_ref[...])
        mv = mem_ref[...]
        dg_ref[...] = jnp.sum(dm * (mv * _rstd(mv)), axis=0, keepdims=True)

    return pl.pallas_call(
        body,
        name="mem_kv_bwd",
        out_shape=[jax.ShapeDtypeStruct((D, N), BF16), jax.ShapeDtypeStruct((1, D), F32)],
        compiler_params=_cparams(),
    )(mem, gm, mb, dkv, w_mkv)


def _post_attn_bwd(dx1, fox_o, sb_o, gf, gs, w_out, tm):
    T, D = dx1.shape

    def body(dx_ref, f_ref, s_ref, gf_ref, gs_ref, w_ref, df_ref, ds_ref, dgf_ref, dgs_ref):
        i = pl.program_id(0)

        @pl.when(i == 0)
        def _():
            dgf_ref[...] = jnp.zeros_like(dgf_ref)
            dgs_ref[...] = jnp.zeros_like(dgs_ref)

        dmix = _nt(dx_ref[...].astype(BF16), w_ref[...])
        d, dg = _norm_bwd(f_ref[...], gf_ref[...], dmix[:, :FOX_W])
        df_ref[...] = d
        dgf_ref[...] += dg
        d, dg = _norm_bwd(s_ref[...], gs_ref[...], dmix[:, FOX_W:])
        ds_ref[...] = d
        dgs_ref[...] += dg

    return pl.pallas_call(
        body,
        name="post_attn_bwd",
        grid=(T // tm,),
        in_specs=[
            pl.BlockSpec((tm, D), lambda i: (i, 0)),
            pl.BlockSpec((tm, FOX_W), lambda i: (i, 0)),
            pl.BlockSpec((tm, FOX_W), lambda i: (i, 0)),
            pl.BlockSpec((1, FOX_W), lambda i: (0, 0)),
            pl.BlockSpec((1, FOX_W), lambda i: (0, 0)),
            pl.BlockSpec((D, D), lambda i: (0, 0)),
        ],
        out_specs=[
            pl.BlockSpec((tm, FOX_W), lambda i: (i, 0)),
            pl.BlockSpec((tm, FOX_W), lambda i: (i, 0)),
            pl.BlockSpec((1, FOX_W), lambda i: (0, 0)),
            pl.BlockSpec((1, FOX_W), lambda i: (0, 0)),
        ],
        out_shape=[
            jax.ShapeDtypeStruct((T, FOX_W), F32),
            jax.ShapeDtypeStruct((T, FOX_W), F32),
            jax.ShapeDtypeStruct((1, FOX_W), F32),
            jax.ShapeDtypeStruct((1, FOX_W), F32),
        ],
        compiler_params=_cparams(("arbitrary",)),
    )(dx1, fox_o, sb_o, gf, gs, w_out)


def _sb_bwd(proj, ltot, live, d_o, tq):
    T = proj.shape[0]
    nq = T // tq

    def body(q_ref, k_ref, v_ref, lt_ref, live_ref, do_ref, dq_ref, dk_ref, dv_ref,
             qh_s, doh_s, lt_s, z_s, da_s, ab_s, dzb_s, run_s, runw_s, dq_s):
        i = pl.program_id(1)

        @pl.when(i == 0)
        def _():
            dk_ref[...] = jnp.zeros_like(dk_ref)
            dv_ref[...] = jnp.zeros_like(dv_ref)

        lane = lax.broadcasted_iota(jnp.int32, (1, 128), 1)
        row = lax.broadcasted_iota(jnp.int32, (tq, tq), 0)
        col = lax.broadcasted_iota(jnp.int32, (tq, tq), 1)
        strict = col < row
        upto = jnp.where(row <= col, 1.0, 0.0).astype(BF16)
        before = jnp.where(row < col, 1.0, 0.0).astype(BF16)
        q = q_ref[...]
        dov = do_ref[...]
        for hh in range(2):
            qh, hmask = _head_q(q, hh, lane)
            qh_s[hh] = -qh
            doh_s[hh] = jnp.where(hmask, dov, 0.0).astype(BF16)
            lt_s[hh] = jnp.broadcast_to(_lanes_to_rows(lt_ref[hh], row == col), (tq, 128))
        run_s[...] = jnp.zeros_like(run_s)
        runw_s[...] = jnp.zeros_like(runw_s)
        dq_s[...] = jnp.zeros_like(dq_s)

        at = lax.broadcasted_iota(jnp.int32, (1, nq), 1)
        count = jnp.sum(jnp.where(at == i, live_ref[0], 0.0), axis=1, keepdims=True)[0, 0].astype(jnp.int32)
        n_live = jnp.clip(count, 1, i + 1)
        oldest = i + 1 - n_live

        def rows(t):
            return pl.ds(pl.multiple_of((oldest + t) * tq, tq), tq)

        def stage_a(t, slot):
            k = k_ref[rows(t), :]
            v = v_ref[rows(t), :]
            for hh in range(2):
                z_s[slot, hh] = _nt(qh_s[hh], k)
                da_s[slot, hh] = _nt(doh_s[hh], v)

        def stage_b(t, slot, diag):
            for hh in range(2):
                L, g = _sb_logs(z_s[slot, hh], strict if diag else None)
                upto_s = _split_dot(L, upto, SB_SUM_TERMS)
                run = run_s[hh]
                arg = (g + _lanes2(lt_s[hh] - run)) - upto_s
                if diag:
                    arg = jnp.where(strict, arg, NEG)
                a = jnp.exp(arg)
                w = a * da_s[slot, hh]
                w_before = _split_dot(w, before, SB_SUM_TERMS)
                run_w = runw_s[hh]
                d_keep = w_before + _lanes2(run_w)
                beta = jnp.exp(g)
                ndz = beta * (w + d_keep) - w
                if diag:
                    ndz = jnp.where(strict, ndz, 0.0)
                dzb_s[hh] = ndz.astype(BF16)
                ab_s[hh] = a.astype(BF16)
                run_s[hh] = run + jnp.broadcast_to(upto_s[:, tq - 1:tq], (tq, 128))
                runw_s[hh] = run_w + jnp.broadcast_to(w_before[:, tq - 1:tq] + w[:, tq - 1:tq], (tq, 128))

        def stage_c(t):
            k = k_ref[rows(t), :]
            dk_blk = None
            dv_blk = None
            for hh in range(2):
                dzb = dzb_s[hh]
                dq_s[hh] += _nn(dzb, k)
                dk_h = _tn(dzb, qh_s[hh])
                dv_h = _tn(ab_s[hh], doh_s[hh])
                dk_blk = dk_h if dk_blk is None else dk_blk + dk_h
                dv_blk = dv_h if dv_blk is None else dv_blk + dv_h
            dk_ref[rows(t), :] += dk_blk
            dv_ref[rows(t), :] += dv_blk

        _pipeline3(n_live, stage_a, stage_b, stage_c, True)
        dq_ref[...] = (jnp.where(lane < HEAD_DIM, dq_s[0], dq_s[1]) * -(HEAD_DIM ** -0.5)).astype(BF16)

    return pl.pallas_call(
        body,
        name="sb_bwd",
        grid=(4, nq),
        in_specs=[
            pl.BlockSpec((tq, 128), lambda p, i: (i, 12 + p)),
            pl.BlockSpec((T, 128), lambda p, i: (0, 16 + p)),
            pl.BlockSpec((T, 128), lambda p, i: (0, 20 + p)),
            pl.BlockSpec((2, 1, tq), lambda p, i: (p, 0, i)),
            pl.BlockSpec((1, 1, nq), lambda p, i: (p, 0, 0)),
            pl.BlockSpec((tq, 128), lambda p, i: (i, p)),
        ],
        out_specs=[
            pl.BlockSpec((tq, 128), lambda p, i: (i, p)),
            pl.BlockSpec((T, 128), lambda p, i: (0, p)),
            pl.BlockSpec((T, 128), lambda p, i: (0, p)),
        ],
        out_shape=[
            jax.ShapeDtypeStruct((T, FOX_W), BF16),
            jax.ShapeDtypeStruct((T, FOX_W), F32),
            jax.ShapeDtypeStruct((T, FOX_W), F32),
        ],
        scratch_shapes=[
            pltpu.VMEM((2, tq, 128), BF16),
            pltpu.VMEM((2, tq, 128), BF16),
            pltpu.VMEM((2, tq, 128), F32),
            pltpu.VMEM((2, 2, tq, tq), F32),
            pltpu.VMEM((2, 2, tq, tq), F32),
            pltpu.VMEM((2, tq, tq), BF16),
            pltpu.VMEM((2, tq, tq), BF16),
            pltpu.VMEM((2, tq, 128), F32),
            pltpu.VMEM((2, tq, 128), F32),
            pltpu.VMEM((2, tq, 128), F32),
        ],
        compiler_params=_cparams(("arbitrary", "arbitrary")),
    )(proj, proj, proj, ltot, live, d_o)


def _fox_bwd(proj, c_col, c_row, c_ends, lse, d_o, o, tq, scatter=()):
    T = proj.shape[0]
    nq = T // tq
    ns = len(scatter)

    def body(*refs):
        q_ref, k_ref, v_ref, cq_ref, ck_ref, cke_ref, lse_ref, do_ref, o_ref = refs[:9]
        dq_ref, dk_ref, dv_ref, dck_ref, dcq_ref = refs[9 + ns:14 + ns]
        (qh_s, doh_s, delta_s, shift_s, z_s, dp_s, pb_s, dsb_s, rs_s, dq_s,
         kn_s) = refs[14 + 2 * ns:25 + 2 * ns]
        i = pl.program_id(1)
        if ns:
            pair = pl.program_id(0)
            finish = _ride_along(_Scatter(refs[9:9 + ns], refs[14 + ns:14 + 2 * ns], *refs[25 + 2 * ns:]),
                                 (pair == 0) & (i == 0), None, (pair == 3) & (i == nq - 1))
        lane = lax.broadcasted_iota(jnp.int32, (1, 128), 1)

        @pl.when(i == 0)
        def _():
            dk_ref[...] = jnp.zeros_like(dk_ref)
            dv_ref[...] = jnp.zeros_like(dv_ref)
            dck_ref[...] = jnp.zeros_like(dck_ref)
            _fox_key_norms(k_ref, kn_s, lane)

        row = lax.broadcasted_iota(jnp.int32, (tq, tq), 0)
        col = lax.broadcasted_iota(jnp.int32, (tq, tq), 1)
        q = q_ref[...]
        dov = do_ref[...]
        ov = o_ref[...]
        for hh in range(2):
            qh, hmask = _head_q(q, hh, lane)
            dohb = jnp.where(hmask, dov, 0.0).astype(BF16)
            qh_s[hh] = qh
            doh_s[hh] = dohb
            delta_s[hh] = jnp.broadcast_to(jnp.sum(dohb.astype(F32) * ov, axis=1, keepdims=True), (tq, tq))
            shift_s[hh] = jnp.broadcast_to(_lanes_to_rows(cq_ref[hh] - lse_ref[hh], row == col), (tq, tq))
        rs_s[...] = jnp.zeros_like(rs_s)
        dq_s[...] = jnp.zeros_like(dq_s)

        def rows(t):
            return pl.ds(pl.multiple_of((i - t) * tq, tq), tq)

        def stage_a(t, slot):
            k = k_ref[rows(t), :]
            v = v_ref[rows(t), :]
            for hh in range(2):
                z_s[slot, hh] = _nt(qh_s[hh], k)
                dp_s[slot, hh] = _nt(doh_s[hh], v)

        def stage_b(t, slot, diag):
            for hh in range(2):
                s = z_s[slot, hh] + shift_s[hh] - ck_ref[hh, :, rows(t)]
                if diag:
                    s = jnp.where(col <= row, s, NEG)
                p = jnp.exp(s)
                ds = p * (dp_s[slot, hh] - delta_s[hh])
                pb_s[hh] = p.astype(BF16)
                dsb_s[hh] = ds.astype(BF16)
                dck_ref[hh, :, rows(t)] += jnp.sum(ds, axis=0, keepdims=True)
                rs_s[hh] += jnp.sum(ds, axis=1, keepdims=True)

        def stage_c(t):
            k = k_ref[rows(t), :]
            dk_blk = None
            dv_blk = None
            for hh in range(2):
                dsb = dsb_s[hh]
                dq_s[hh] += _nn(dsb, k)
                dk_h = _tn(dsb, qh_s[hh])
                dv_h = _tn(pb_s[hh], doh_s[hh])
                dk_blk = dk_h if dk_blk is None else dk_blk + dk_h
                dv_blk = dv_h if dv_blk is None else dv_blk + dv_h
            dk_ref[rows(t), :] += dk_blk
            dv_ref[rows(t), :] += dv_blk

        _pipeline3(_fox_live_blocks(i, qh_s, kn_s, cq_ref, cke_ref), stage_a, stage_b, stage_c, False)
        dcq_ref[0] = _rows_to_lanes(rs_s[0], row == col)
        dcq_ref[1] = _rows_to_lanes(rs_s[1], row == col)
        dq_ref[...] = (jnp.where(lane < HEAD_DIM, dq_s[0], dq_s[1]) * (HEAD_DIM ** -0.5)).astype(BF16)
        if ns:
            finish()

    res = pl.pallas_call(
        body,
        name="fox_bwd",
        grid=(4, nq),
        in_specs=[
            pl.BlockSpec((tq, 128), lambda p, i: (i, p)),
            pl.BlockSpec((T, 128), lambda p, i: (0, 4 + p)),
            pl.BlockSpec((T, 128), lambda p, i: (0, 8 + p)),
            pl.BlockSpec((2, 1, tq), lambda p, i: (p, 0, i)),
            pl.BlockSpec((2, 1, T), lambda p, i: (p, 0, 0)),
            pl.BlockSpec((2, 1, nq), lambda p, i: (p, 0, 0)),
            pl.BlockSpec((2, 1, tq), lambda p, i: (p, 0, i)),
            pl.BlockSpec((tq, 128), lambda p, i: (i, p)),
            pl.BlockSpec((tq, 128), lambda p, i: (i, p)),
        ] + [_ANY] * ns,
        out_specs=[
            pl.BlockSpec((tq, 128), lambda p, i: (i, p)),
            pl.BlockSpec((T, 128), lambda p, i: (0, p)),
            pl.BlockSpec((T, 128), lambda p, i: (0, p)),
            pl.BlockSpec((2, 1, T), lambda p, i: (p, 0, 0)),
            pl.BlockSpec((2, 1, tq), lambda p, i: (p, 0, i)),
        ] + [_ANY] * ns,
        out_shape=[
            jax.ShapeDtypeStruct((T, FOX_W), BF16),
            jax.ShapeDtypeStruct((T, FOX_W), F32),
            jax.ShapeDtypeStruct((T, FOX_W), F32),
            jax.ShapeDtypeStruct((N_FOX, 1, T), F32),
            jax.ShapeDtypeStruct((N_FOX, 1, T), F32),
        ] + [jax.ShapeDtypeStruct(b.shape, b.dtype) for b in scatter],
        scratch_shapes=[
            pltpu.VMEM((2, tq, 128), BF16),
            pltpu.VMEM((2, tq, 128), BF16),
            pltpu.VMEM((2, tq, tq), F32),
            pltpu.VMEM((2, tq, tq), F32),
            pltpu.VMEM((2, 2, tq, tq), F32),
            pltpu.VMEM((2, 2, tq, tq), F32),
            pltpu.VMEM((2, tq, tq), BF16),
            pltpu.VMEM((2, tq, tq), BF16),
            pltpu.VMEM((2, tq, 128), F32),
            pltpu.VMEM((2, tq, 128), F32),
            pltpu.VMEM((2, 8, 128), F32),
        ] + (_comm_sems(ns) if ns else []),
        compiler_params=_cparams(("arbitrary", "arbitrary")),
    )(proj, proj, proj, c_col, c_row, c_ends, lse, d_o, o, *scatter)
    res = list(res)
    return (*res[:5], res[5:])


def _forget_bwd(dcq, dck, xf, tc):
    H, T = xf.shape
    nc = T // tc

    def body(dcq_ref, dck_ref, xf_ref, dxf_ref, db_ref):
        row = lax.broadcasted_iota(jnp.int32, (tc, tc), 0)
        col = lax.broadcasted_iota(jnp.int32, (tc, tc), 1)
        from_here = jnp.where(row >= col, 1.0, 0.0).astype(BF16)

        def chunk(n, carry):
            run, db = carry
            cs = pl.multiple_of((nc - 1 - n) * tc, tc)
            dc = dcq_ref[:, pl.ds(cs, tc)] - dck_ref[:, pl.ds(cs, tc)]
            dlogf = _split_dot(dc, from_here, 3) + run
            xfv = xf_ref[:, pl.ds(cs, tc)]
            dxf = dlogf * jax.nn.sigmoid(-xfv)
            dxf_ref[:, pl.ds(cs, tc)] = dxf
            return dlogf[:, 0:1], db + jnp.sum(dxf, axis=1, keepdims=True)

        _, db = lax.fori_loop(0, nc, chunk, (jnp.zeros((H, 1), F32), jnp.zeros((H, 1), F32)))
        db_ref[...] = db

    return pl.pallas_call(
        body,
        name="forget_bwd",
        out_shape=[jax.ShapeDtypeStruct((H, T), F32), jax.ShapeDtypeStruct((H, 1), F32)],
        compiler_params=_cparams(),
    )(dcq, dck, xf)


def _inproj_bwd(dproj, w_in_pad, x, g1, dx1, tm, scatter=()):
    T, D = x.shape
    N = dproj.shape[1]
    ns = len(scatter)
    nt = T // tm

    def body(*refs):
        dp_ref, w_ref, x_ref, g_ref, dx1_ref = refs[:5]
        dx_ref, dg_ref = refs[5 + ns:7 + ns]
        i = pl.program_id(0)
        if ns:
            exchange = _Scatter(refs[5:5 + ns], refs[7 + ns:7 + 2 * ns], *refs[7 + 2 * ns:])

            @pl.when(i == 0)
            def _():
                exchange.start()

        @pl.when(i == 0)
        def _():
            dg_ref[...] = jnp.zeros_like(dg_ref)

        dh = _nt(dp_ref[...], w_ref[...])
        dx, dg = _norm_bwd(x_ref[...], g_ref[...], dh)
        dx_ref[...] = dx1_ref[...] + dx
        dg_ref[...] += dg
        if ns:
            @pl.when(i == nt - 1)
            def _():
                exchange.finish()

    res = pl.pallas_call(
        body,
        name="inproj_bwd",
        grid=(nt,),
        in_specs=[
            pl.BlockSpec((tm, N), lambda i: (i, 0)),
            pl.BlockSpec((D, N), lambda i: (0, 0)),
            pl.BlockSpec((tm, D), lambda i: (i, 0)),
            pl.BlockSpec((1, D), lambda i: (0, 0)),
            pl.BlockSpec((tm, D), lambda i: (i, 0)),
        ] + [_ANY] * ns,
        out_specs=[
            pl.BlockSpec((tm, D), lambda i: (i, 0)),
            pl.BlockSpec((1, D), lambda i: (0, 0)),
        ] + [_ANY] * ns,
        out_shape=[jax.ShapeDtypeStruct((T, D), F32), jax.ShapeDtypeStruct((1, D), F32)]
        + [jax.ShapeDtypeStruct(b.shape, b.dtype) for b in scatter],
        scratch_shapes=_comm_sems(ns) if ns else [],
        compiler_params=_cparams(("arbitrary",)),
    )(dproj, w_in_pad, x, g1, dx1, *scatter)
    res = list(res)
    return res[0], res[1], res[2:]


def _matmul_tn(a, b, name, cast_b=False):
    T, K = a.shape
    N = b.shape[1]
    bt = min(T, 512)
    bk = _tile_div(K, 1536)
    bn = _tile_div(N, 1536)
    nt = T // bt

    def body(a_ref, b_ref, o_ref, acc_ref):
        t = pl.program_id(2)

        @pl.when(t == 0)
        def _():
            acc_ref[...] = jnp.zeros_like(acc_ref)

        bv = b_ref[...]
        if cast_b:
            bv = bv.astype(BF16)
        acc_ref[...] += _tn(a_ref[...], bv)

        @pl.when(t == nt - 1)
        def _():
            o_ref[...] = acc_ref[...].astype(BF16)

    return pl.pallas_call(
        body,
        name=name,
        grid=(K // bk, N // bn, nt),
        in_specs=[
            pl.BlockSpec((bt, bk), lambda k, n, t: (t, k)),
            pl.BlockSpec((bt, bn), lambda k, n, t: (t, n)),
        ],
        out_specs=pl.BlockSpec((bk, bn), lambda k, n, t: (k, n)),
        out_shape=jax.ShapeDtypeStruct((K, N), BF16),
        scratch_shapes=[pltpu.VMEM((bk, bn), F32)],
        compiler_params=_cparams(("arbitrary", "arbitrary", "arbitrary")),
    )(a, b)


def _local_step(x, mem, target, p, tm, tq, late=None):
    T, D = x.shape
    w_in = p["w_in"]
    w_qkv = w_in[:, :QKV_W]
    w_f_t = w_in[:, QKV_W:].T
    w_in_pad = jnp.pad(w_in, ((0, 0), (0, IN_PAD - w_in.shape[1])))
    b_f = p["b_forget"].reshape(N_FOX, 1)

    proj, h1, xf, c = _inproj_fwd(x, p["attn_norm_g"], w_qkv, w_f_t, b_f, tm)
    c_col = c.reshape(N_FOX, 1, T)
    c_row = c.reshape(N_FOX, 1, T)
    c_ends = c[:, tq - 1::tq].reshape(N_FOX, 1, T // tq)
    fox_o, lse, gathered = _fox_fwd(proj, c_col, c_row, c_ends, tq, gather=[late[n] for n in _LATE] if late else ())
    if late:
        p = dict(p, **{n: _gathered_full(n, gv) for n, gv in zip(_LATE, gathered)})
    sb_o, sb_ltot, sb_live = _sb_fwd(proj, tq)
    x1, mixed = _post_attn_fwd(fox_o, sb_o, p["fox_out_g"], p["sb_out_g"], p["w_out"], x, tm)
    mb, kv = _mem_kv_fwd(mem, p["mem_norm_g"], p["w_mkv"])
    x2, h2, qb, om = _xattn_fwd(x1, p["xattn_norm_g"], p["w_mq"], kv, p["w_mo"], tm)
    x3, h3, ug, uv, yg, yv, a = _ffn_fwd(
        x2, p["ffn_norm_g"], p["w_up"], p["conv_w"], p["conv_b"], p["w_down"], tm)
    dx3, loss_blk, d_final_g = _loss_head(x3, p["final_norm_g"], target, tm)

    g = {"final_norm_g": d_final_g}
    dx2, du_g, du_v, g["ffn_norm_g"], dc_g, dc_v = _ffn_bwd(
        dx3, x2, p["ffn_norm_g"], ug, uv, yg, yv, p["conv_w"], p["w_down"], p["w_up"], tm)
    g["w_down"] = _matmul_tn(a, dx3, "dw_down", cast_b=True)
    g["w_up"] = jnp.concatenate([_matmul_tn(h3, du_g, "dw_up_gate"), _matmul_tn(h3, du_v, "dw_up_val")], axis=1)
    dconv = jnp.concatenate([dc_g, dc_v], axis=1)
    g["conv_w"] = dconv[0:3]
    g["conv_b"] = dconv[3:4]
    dx1, dq_m, dkv, g["xattn_norm_g"] = _xattn_bwd(dx2, x1, p["xattn_norm_g"], qb, kv, p["w_mo"], p["w_mq"], tm)
    g["w_mo"] = _matmul_tn(om, dx2, "dw_mo", cast_b=True)
    g["w_mq"] = _matmul_tn(h2, dq_m, "dw_mq")
    g["w_mkv"], g["mem_norm_g"] = _mem_kv_bwd(mem, p["mem_norm_g"], mb, dkv, p["w_mkv"])
    d_fox, d_sb, g["fox_out_g"], g["sb_out_g"] = _post_attn_bwd(
        dx1, fox_o, sb_o, p["fox_out_g"], p["sb_out_g"], p["w_out"], tm)
    g["w_out"] = _matmul_tn(mixed, dx1, "dw_out", cast_b=True)
    dq_s, dk_s, dv_s = _sb_bwd(proj, sb_ltot, sb_live, d_sb, tq)
    dq_f, dk_f, dv_f, dck, dcq, parts = _fox_bwd(
        proj, c_col, c_row, c_ends, lse, d_fox, fox_o, tq,
        scatter=[_grad_blocks(n, g[n]) for n in _LATE] if late else ())
    if late:
        g["parts"] = dict(zip(_LATE, parts))
    dxf, db = _forget_bwd(dcq.reshape(N_FOX, T), dck.reshape(N_FOX, T), xf, min(T, 512))
    g["b_forget"] = db.reshape(1, N_FOX)
    dproj = jnp.concatenate([
        dq_f, dk_f.astype(BF16), dv_f.astype(BF16), dq_s, dk_s.astype(BF16), dv_s.astype(BF16),
        jnp.pad(dxf.T, ((0, 0), (0, IN_PAD - QKV_W - N_FOX))).astype(BF16)], axis=1)
    g["w_in"] = _matmul_tn(h1, dproj, "dw_in")[:, :w_in.shape[1]]
    if late:
        grad_x, g["attn_norm_g"], (g["parts"]["w_in"],) = _inproj_bwd(
            dproj, w_in_pad, x, p["attn_norm_g"], dx1, tm, scatter=[_grad_blocks("w_in", g["w_in"])])
    else:
        grad_x, g["attn_norm_g"], _ = _inproj_bwd(dproj, w_in_pad, x, p["attn_norm_g"], dx1, tm)
    return loss_blk, grad_x, g


def _mesh_pos():
    return lax.axis_index("x"), lax.axis_index("y"), lax.axis_index("c")


def _flip(pos, k):
    return tuple(1 - v if (k >> b) & 1 else v for v, b in zip(pos, (2, 1, 0)))


def _slot(pos):
    return 4 * pos[0] + 2 * pos[1] + pos[2]


_CHIPS = (4, 2, 6)


def _comm_sems(n):
    return [pltpu.SemaphoreType.DMA((7 * n,)), pltpu.SemaphoreType.DMA((7 * n,)), pltpu.SemaphoreType.DMA((n,))]


class _Gather:
    def __init__(self, ins, outs, send_sems, recv_sems, local_sems):
        self.ins, self.outs, self.n = ins, outs, len(ins)
        self.send_sems, self.recv_sems, self.local_sems = send_sems, recv_sems, local_sems
        self.me = _mesh_pos()
        self.sibling = _flip(self.me, 1)

    def _copy(self, a, kk, block, to, src=None):
        rows = self.outs[a].at[_slot(block)]
        return pltpu.make_async_remote_copy(
            src_ref=rows if src is None else src, dst_ref=rows,
            send_sem=self.send_sems.at[7 * a + kk], recv_sem=self.recv_sems.at[7 * a + kk],
            device_id=to, device_id_type=MESH)

    def _mine(self):
        return [pltpu.make_async_copy(self.ins[a], self.outs[a].at[_slot(self.me)], self.local_sems.at[a])
                for a in range(self.n)]

    def _first(self):
        out = []
        for a in range(self.n):
            out.append(self._copy(a, 0, self.me, self.sibling, src=self.ins[a]))
            out += [self._copy(a, 1 + j, self.me, _flip(self.me, k), src=self.ins[a]) for j, k in enumerate(_CHIPS)]
        return out

    def _passed(self):
        return [self._copy(a, 4 + j, _flip(self.me, k), self.sibling)
                for j, k in enumerate(_CHIPS) for a in range(self.n)]

    def start(self):
        for cp in self._mine() + self._first():
            cp.start()

    def forward(self):
        for j, k in enumerate(_CHIPS):
            for a in range(self.n):
                self._copy(a, 1 + j, _flip(self.me, k), self.me).wait_recv()
                self._copy(a, 4 + j, _flip(self.me, k), self.sibling).start()

    def finish(self):
        for a in range(self.n):
            self._copy(a, 0, self.sibling, self.me).wait_recv()
            for j, k in enumerate(_CHIPS):
                self._copy(a, 4 + j, _flip(self.sibling, k), self.me).wait_recv()
        for cp in self._first() + self._passed():
            cp.wait_send()
        for cp in self._mine():
            cp.wait()


class _Scatter:
    def __init__(self, ins, outs, send_sems, recv_sems, local_sems):
        self.ins, self.outs, self.n = ins, outs, len(ins)
        self.send_sems, self.recv_sems, self.local_sems = send_sems, recv_sems, local_sems
        self.me = _mesh_pos()

    def _copy(self, a, k, landed=False):
        peer = _flip(self.me, k)
        return pltpu.make_async_remote_copy(
            src_ref=self.ins[a].at[_slot(peer)], dst_ref=self.outs[a].at[_slot(peer if landed else self.me)],
            send_sem=self.send_sems.at[7 * a + k - 1], recv_sem=self.recv_sems.at[7 * a + k - 1],
            device_id=peer, device_id_type=MESH)

    def _mine(self):
        s = _slot(self.me)
        return [pltpu.make_async_copy(self.ins[a].at[s], self.outs[a].at[s], self.local_sems.at[a])
                for a in range(self.n)]

    def start(self):
        for cp in self._mine() + [self._copy(a, k) for k in range(1, 8) for a in range(self.n)]:
            cp.start()

    def finish(self):
        for k in range(1, 8):
            for a in range(self.n):
                self._copy(a, k, landed=True).wait_recv()
        for k in range(1, 8):
            for a in range(self.n):
                self._copy(a, k).wait_send()
        for cp in self._mine():
            cp.wait()


_ANY = pl.BlockSpec(memory_space=pl.ANY)


def _gathered_shapes(shards):
    return [jax.ShapeDtypeStruct((N_DEV,) + s.shape, s.dtype) for s in shards]


def _all_gather(shards, name):
    n = len(shards)

    def body(*refs):
        g = _Gather(refs[:n], refs[n:2 * n], *refs[2 * n:])
        g.start()
        g.forward()
        g.finish()

    return pl.pallas_call(
        body, name=name, in_specs=[_ANY] * n, out_specs=[_ANY] * n,
        out_shape=_gathered_shapes(shards), scratch_shapes=_comm_sems(n),
    )(*shards)


def _adamw_math(w, g, m, v):
    m2 = ADAM_B1 * m + (1.0 - ADAM_B1) * g
    v2 = ADAM_B2 * v + (1.0 - ADAM_B2) * (g * g)
    m_hat = m2 / (1.0 - ADAM_B1 ** ADAM_STEP)
    v_hat = v2 / (1.0 - ADAM_B2 ** ADAM_STEP)
    delta = -ADAM_LR * (m_hat / (jnp.sqrt(v_hat) + ADAM_EPS) + ADAM_WD * w)
    return delta, m2, v2


def _adamw(w, parts, m, v, name):
    R, C = w.shape
    br = 128 if R % 128 == 0 else R

    def body(w_ref, p_ref, m_ref, v_ref, g_ref, d_ref, nm_ref, nv_ref):
        g = p_ref[0].astype(F32)
        for s in range(1, N_DEV):
            g = g + p_ref[s].astype(F32)
        g_ref[...] = g
        d_ref[...], nm_ref[...], nv_ref[...] = _adamw_math(w_ref[...], g, m_ref[...], v_ref[...])

    spec = pl.BlockSpec((br, C), lambda i: (i, 0))
    return pl.pallas_call(
        body,
        name=name,
        grid=(R // br,),
        in_specs=[spec, pl.BlockSpec((N_DEV, br, C), lambda i: (0, i, 0)), spec, spec],
        out_specs=[spec] * 4,
        out_shape=[jax.ShapeDtypeStruct((R, C), F32)] * 4,
        compiler_params=_cparams(("arbitrary",)),
    )(w, parts, m, v)


_SHARDED = ("w_in", "w_out", "w_mq", "w_mkv", "w_mo", "w_up", "conv_w", "w_down")
_LATE = _SHARDED[1:]
_COL_SHARDED = ("w_in", "w_mkv", "w_up", "conv_w")
_REPLICATED = ("attn_norm_g", "b_forget", "fox_out_g", "sb_out_g", "xattn_norm_g", "mem_norm_g",
               "ffn_norm_g", "conv_b", "final_norm_g")
_WEIGHTS = ("attn_norm_g", "w_in", "b_forget", "fox_out_g", "sb_out_g", "w_out", "xattn_norm_g", "mem_norm_g",
            "w_mq", "w_mkv", "w_mo", "ffn_norm_g", "w_up", "conv_w", "conv_b", "w_down", "final_norm_g")


def _pack_rows(n):
    return -(-n // 128)


def _pack(vals, rows_total):
    parts = []
    for v in vals:
        flat = v.reshape(-1)
        parts.append(jnp.pad(flat, (0, _pack_rows(flat.shape[0]) * 128 - flat.shape[0])))
    flat = jnp.concatenate(parts)
    return jnp.pad(flat, (0, rows_total * 128 - flat.shape[0])).reshape(rows_total, 128)


def _unpack(packed, shapes):
    out = []
    r = 0
    for shp in shapes:
        n = 1
        for d in shp:
            n *= d
        out.append(packed[r:r + _pack_rows(n)].reshape(-1)[:n].reshape(shp))
        r += _pack_rows(n)
    return out


def _gathered_full(name, gathered):
    if name in _COL_SHARDED:
        return jnp.transpose(gathered, (1, 0, 2)).reshape(gathered.shape[1], -1)
    return gathered.reshape(-1, gathered.shape[2])


def _to_blocks(name, full):
    if name in _COL_SHARDED:
        r = full.shape[0]
        return jnp.transpose(full.reshape(r, N_DEV, -1), (1, 0, 2))
    return full.reshape(N_DEV, -1, full.shape[1])


def _grad_blocks(name, full):
    blocks = _to_blocks(name, full)
    return blocks if name == "conv_w" else blocks.astype(BF16)


def _step(args, tm, tq):
    w = {n: args[n] for n in _WEIGHTS}
    mom = {n: args["m_" + n] for n in _WEIGHTS}
    var = {n: args["v_" + n] for n in _WEIGHTS}
    x = args["x"][0]
    mem = args["mem"][0]
    target = args["loss_target"][0]

    def flat2(a):
        return a.reshape(a.shape[-2], a.shape[-1]) if a.ndim == 3 else a.reshape(1, -1)

    shards = {n: flat2(w[n]) if n == "conv_w" else flat2(w[n]).astype(BF16) for n in _SHARDED}
    (w_in_all,) = _all_gather([shards["w_in"]], "gather_w_in")
    p = {"w_in": _gathered_full("w_in", w_in_all)}
    for n in _REPLICATED:
        p[n] = flat2(w[n])

    loss_blk, grad_x, g = _local_step(x, mem, target, p, tm, tq, late={n: shards[n] for n in _LATE})

    parts = g["parts"]
    out = {}
    for n in _SHARDED:
        res = _adamw(flat2(w[n]), parts[n], flat2(mom[n]), flat2(var[n]), "adamw_" + n)
        out[n] = [r.reshape(w[n].shape) for r in res]

    shapes = [w[n].shape for n in _REPLICATED]
    rows = sum(_pack_rows(flat2(w[n]).shape[1]) for n in _REPLICATED) + 1
    rows = -(-rows // 8) * 8
    g_pack = _pack([g[n] for n in _REPLICATED] + [loss_blk[0:1, :]], rows)
    (g_all,) = _all_gather([g_pack], "gather_small")
    res = _adamw(_pack([w[n] for n in _REPLICATED], rows), g_all,
                 _pack([mom[n] for n in _REPLICATED], rows), _pack([var[n] for n in _REPLICATED], rows),
                 "adamw_small")
    n_rows_params = sum(_pack_rows(flat2(w[n]).shape[1]) for n in _REPLICATED)
    loss = res[0][n_rows_params, 0]
    unpacked = [_unpack(r, shapes) for r in res]
    for k, n in enumerate(_REPLICATED):
        out[n] = [unpacked[q][k] for q in range(4)]

    grads = [out[n][0] for n in _WEIGHTS]
    deltas = [out[n][1] for n in _WEIGHTS]
    new_m = [out[n][2] for n in _WEIGHTS]
    new_v = [out[n][3] for n in _WEIGHTS]
    return (loss, grad_x[None], *grads, *deltas, *new_m, *new_v)


def kernel(x, mem, attn_norm_g, w_in, b_forget, fox_out_g, sb_out_g, w_out, xattn_norm_g, mem_norm_g, w_mq, w_mkv, w_mo, ffn_norm_g, w_up, conv_w, conv_b, w_down, final_norm_g, loss_target, m_attn_norm_g, m_w_in, m_b_forget, m_fox_out_g, m_sb_out_g, m_w_out, m_xattn_norm_g, m_mem_norm_g, m_w_mq, m_w_mkv, m_w_mo, m_ffn_norm_g, m_w_up, m_conv_w, m_conv_b, m_w_down, m_final_norm_g, v_attn_norm_g, v_w_in, v_b_forget, v_fox_out_g, v_sb_out_g, v_w_out, v_xattn_norm_g, v_mem_norm_g, v_w_mq, v_w_mkv, v_w_mo, v_ffn_norm_g, v_w_up, v_conv_w, v_conv_b, v_w_down, v_final_norm_g):
    args = dict(locals())
    T = x.shape[1]
    return _step(args, tm=min(T, 512), tq=min(T, 256))
```

```python
import functools

import jax
import jax.numpy as jnp
from jax import lax
from jax.experimental import pallas as pl
from jax.experimental.pallas import tpu as pltpu

F32 = jnp.float32
BF16 = jnp.bfloat16
EPS = 1e-6
NEG = -1e30
LOG2E = 1.4426950408889634

HEAD_DIM = 64
N_FOX = 8
FOX_W = 512
QKV_W = 3072
IN_PAD = 3200
N_MEM_HEADS = 4
MEM_HD = 256
D_FF = 2816
FF_CHUNK = 256
N_DEV = 8

ADAM_LR = 0.001
ADAM_B1 = 0.9
ADAM_B2 = 0.999
ADAM_EPS = 1e-08
ADAM_WD = 0.01
ADAM_STEP = 10

SB_SUM_TERMS = 1

VMEM_LIMIT = 56 * 1024 * 1024
MESH = pl.DeviceIdType.MESH


def _cparams(sem=None):
    return pltpu.CompilerParams(dimension_semantics=sem, vmem_limit_bytes=VMEM_LIMIT)


def _nt(a, b):
    return lax.dot_general(a, b, (((1,), (1,)), ((), ())), preferred_element_type=F32)


def _tn(a, b):
    return lax.dot_general(a, b, (((0,), (0,)), ((), ())), preferred_element_type=F32)


def _nn(a, b):
    return jnp.dot(a, b, preferred_element_type=F32)


def _split_dot(a, m01, terms):
    out = None
    r = a
    for t in range(terms):
        p = r.astype(BF16)
        d = _nn(p, m01)
        out = d if out is None else out + d
        if t + 1 < terms:
            r = r - p.astype(F32)
    return out


def _rstd(xv):
    return lax.rsqrt(jnp.mean(xv * xv, axis=-1, keepdims=True) + EPS)


def _norm_bwd(xv, g, dh):
    r = _rstd(xv)
    xhat = xv * r
    dxhat = dh * g
    dx = r * (dxhat - xhat * jnp.mean(dxhat * xhat, axis=-1, keepdims=True))
    dg = jnp.sum(dh * xhat, axis=0, keepdims=True)
    return dx, dg


def _tile_div(n, cap):
    best = None
    for d in range(128, min(n, cap) + 1, 128):
        if n % d == 0:
            best = d
    assert best is not None, n
    return best


def _inproj_fwd(x, g1, w_qkv, w_f_t, b_f, tm):
    T, D = x.shape
    N = w_qkv.shape[1]
    H = w_f_t.shape[0]

    def body(x_ref, g_ref, w_ref, wf_ref, b_ref, proj_ref, h_ref, xf_ref, c_ref, carry_ref):
        i = pl.program_id(0)

        @pl.when(i == 0)
        def _():
            carry_ref[...] = jnp.zeros_like(carry_ref)

        xv = x_ref[...]
        h = (xv * _rstd(xv) * g_ref[...]).astype(BF16)
        h_ref[...] = h
        for n0 in range(0, N, 512):
            proj_ref[:, n0:n0 + 512] = _nn(h, w_ref[:, n0:n0 + 512]).astype(BF16)
        xf = _nt(wf_ref[...], h) + b_ref[...]
        xf_ref[...] = xf
        logf = jnp.minimum(xf, 0.0) - jnp.log1p(jnp.exp(-jnp.abs(xf)))
        row = lax.broadcasted_iota(jnp.int32, (tm, tm), 0)
        col = lax.broadcasted_iota(jnp.int32, (tm, tm), 1)
        upper = jnp.where(row <= col, 1.0, 0.0).astype(BF16)
        c = _split_dot(logf, upper, 3) + carry_ref[...]
        c_ref[...] = c
        carry_ref[...] = c[:, tm - 1:tm]

    return pl.pallas_call(
        body,
        name="inproj_fwd",
        grid=(T // tm,),
        in_specs=[
            pl.BlockSpec((tm, D), lambda i: (i, 0)),
            pl.BlockSpec((1, D), lambda i: (0, 0)),
            pl.BlockSpec((D, N), lambda i: (0, 0)),
            pl.BlockSpec((H, D), lambda i: (0, 0)),
            pl.BlockSpec((H, 1), lambda i: (0, 0)),
        ],
        out_specs=[
            pl.BlockSpec((tm, N), lambda i: (i, 0)),
            pl.BlockSpec((tm, D), lambda i: (i, 0)),
            pl.BlockSpec((H, tm), lambda i: (0, i)),
            pl.BlockSpec((H, tm), lambda i: (0, i)),
        ],
        out_shape=[
            jax.ShapeDtypeStruct((T, N), BF16),
            jax.ShapeDtypeStruct((T, D), BF16),
            jax.ShapeDtypeStruct((H, T), F32),
            jax.ShapeDtypeStruct((H, T), F32),
        ],
        scratch_shapes=[pltpu.VMEM((H, 1), F32)],
        compiler_params=_cparams(("arbitrary",)),
    )(x, g1, w_qkv, w_f_t, b_f)


def _head_q(q, hh, lane):
    hmask = (lane >= HEAD_DIM * hh) & (lane < HEAD_DIM * (hh + 1))
    qh = jnp.where(hmask, q.astype(F32), 0.0) * (HEAD_DIM ** -0.5)
    return qh.astype(BF16), hmask


def _pipeline3(n, stage_a, stage_b, stage_c, diag_last, alive=None):
    stage_a(0, 0)
    if diag_last:
        @pl.when(n == 1)
        def _():
            stage_b(0, 0, True)

        @pl.when(n >= 2)
        def _():
            stage_b(0, 0, False)
    else:
        stage_b(0, 0, True)

    @pl.when(n >= 2)
    def _():
        stage_a(1, 1)

    def pair(m, carry):
        t = 2 + 2 * m
        stage_c(t - 2)
        stage_b(t - 1, 1, False)
        stage_a(t, 0)
        stage_c(t - 1)
        stage_b(t, 0, False)
        stage_a(t + 1, 1)
        return carry

    pairs = (n - 2) // 2
    if alive is None:
        lax.fori_loop(0, pairs, pair, 0)
        go_on = True
        done = n
    else:
        def more(state):
            return (state[0] < pairs) & state[1]

        def step(state):
            pair(state[0], 0)
            return state[0] + 1, alive()

        m_end, go_on = lax.while_loop(more, step, (jnp.int32(0), jnp.bool_(True)))
        done = jnp.where(go_on, n, 2 * m_end)
    odd = n % 2 == 1

    @pl.when((n >= 3) & odd & go_on)
    def _():
        stage_c(n - 3)
        stage_b(n - 2, 1, False)
        stage_a(n - 1, 0)

    @pl.when((n >= 2) & odd & go_on)
    def _():
        stage_c(n - 2)
        stage_b(n - 1, 0, diag_last)

    @pl.when((n >= 2) & jnp.logical_not(odd) & go_on)
    def _():
        stage_c(n - 2)
        stage_b(n - 1, 1, diag_last)

    if alive is None:
        stage_c(n - 1)
    else:
        @pl.when(go_on)
        def _():
            stage_c(n - 1)

    return done


def _lanes2(x):
    return jnp.concatenate([x, x], axis=1)


def _lanes_to_rows(vec, eye):
    return jnp.sum(jnp.where(eye, jnp.broadcast_to(vec, eye.shape), 0.0), axis=1, keepdims=True)


def _rows_to_lanes(rep, eye):
    return jnp.sum(jnp.where(eye, _lanes2(rep), 0.0), axis=0, keepdims=True)


FOX_DEAD = -110.0


def _fox_key_norms(k_ref, kn_s, lane):
    T = k_ref.shape[0]
    rows = min(T, 512)
    for hh in range(2):
        hmask = (lane >= HEAD_DIM * hh) & (lane < HEAD_DIM * (hh + 1))

        def chunk(n, best, hmask=hmask):
            kf = jnp.where(hmask, k_ref[pl.ds(pl.multiple_of(n * rows, rows), rows), :].astype(F32), 0.0)
            sq = jnp.sum(kf * kf, axis=1, keepdims=True)
            return jnp.maximum(best, jnp.max(sq, axis=0, keepdims=True))

        best = lax.fori_loop(0, T // rows, chunk, jnp.zeros((1, 1), F32))
        kn_s[hh] = jnp.broadcast_to(best, kn_s.shape[1:])


def _fox_live_blocks(i, qh_s, kn_s, cq_ref, cke_ref):
    nq = cke_ref.shape[-1]
    jj = lax.broadcasted_iota(jnp.int32, (1, nq), 1)
    first = None
    for hh in range(2):
        qf = qh_s[hh].astype(F32)
        qn = jnp.max(jnp.sum(qf * qf, axis=1, keepdims=True), axis=0, keepdims=True)
        zb = jnp.sqrt(qn * kn_s[hh][0:1, 0:1]) * 1.001
        bound = (2.0 * zb + cq_ref[hh][:, 0:1]) - cke_ref[hh]
        live = (bound >= FOX_DEAD) & (jj <= i)
        f = jnp.min(jnp.where(live, jj, i).astype(F32), axis=1, keepdims=True)
        first = f if first is None else jnp.minimum(first, f)
    return i + 1 - first[0, 0].astype(jnp.int32)


def _ride_along(exchange, at_start, at_middle, at_end):
    @pl.when(at_start)
    def _():
        exchange.start()

    if at_middle is not None:
        @pl.when(at_middle)
        def _():
            exchange.forward()

    def finish():
        @pl.when(at_end)
        def _():
            exchange.finish()

    return finish


def _fox_fwd(proj, c_col, c_row, c_ends, tq, gather=()):
    T = proj.shape[0]
    assert tq == 256
    nq = T // tq
    ng = len(gather)

    def body(*refs):
        q_ref, k_ref, v_ref, cq_ref, ck_ref, cke_ref = refs[:6]
        o_ref, lse_ref = refs[6 + ng:8 + ng]
        qh_s, cq_s, z_s, p_s, al_s, m_s, acc_s, kn_s = refs[8 + 2 * ng:16 + 2 * ng]
        i = pl.program_id(1)
        if ng:
            pair = pl.program_id(0)
            finish = _ride_along(_Gather(refs[6:6 + ng], refs[8 + ng:8 + 2 * ng], *refs[16 + 2 * ng:]),
                                 (pair == 0) & (i == 0), (pair == 1) & (i == 0), (pair == 3) & (i == nq - 1))
        lane = lax.broadcasted_iota(jnp.int32, (1, 128), 1)
        row = lax.broadcasted_iota(jnp.int32, (tq, tq), 0)
        col = lax.broadcasted_iota(jnp.int32, (tq, tq), 1)

        @pl.when(i == 0)
        def _():
            _fox_key_norms(k_ref, kn_s, lane)

        q = q_ref[...]
        for hh in range(2):
            qh_s[hh] = _head_q(q, hh, lane)[0]
            cq_s[hh] = jnp.broadcast_to(_lanes_to_rows(cq_ref[hh], row == col), (tq, tq))
        m_s[...] = jnp.full(m_s.shape, NEG, F32)
        acc_s[...] = jnp.zeros_like(acc_s)

        def rows(t):
            return pl.ds(pl.multiple_of((i - t) * tq, tq), tq)

        def stage_a(t, slot):
            k = k_ref[rows(t), :]
            for hh in range(2):
                z_s[slot, hh] = _nt(qh_s[hh], k)

        def stage_b(t, slot, diag):
            for hh in range(2):
                s = z_s[slot, hh] + cq_s[hh] - ck_ref[hh, :, rows(t)]
                if diag:
                    s = jnp.where(col <= row, s, NEG)
                m = m_s[hh]
                half = jnp.maximum(s[:, :128], s[:, 128:])
                m_new = jnp.maximum(m, jnp.max(half, axis=1, keepdims=True))
                m_s[hh] = m_new
                al_s[hh] = jnp.exp(m - m_new)
                p_s[hh] = jnp.exp(s - _lanes2(m_new)).astype(BF16)

        def stage_c(t):
            v = v_ref[rows(t), :]
            for hh in range(2):
                own = (lane >= HEAD_DIM * hh) & (lane < HEAD_DIM * (hh + 1))
                acc_s[hh] = al_s[hh] * acc_s[hh] + _nn(p_s[hh], jnp.where(own, v, 1.0).astype(BF16))

        _pipeline3(_fox_live_blocks(i, qh_s, kn_s, cq_ref, cke_ref), stage_a, stage_b, stage_c, False)
        halves = []
        for hh in range(2):
            acc = acc_s[hh]
            own = (lane >= HEAD_DIM * hh) & (lane < HEAD_DIM * (hh + 1))
            halves.append(jnp.where(own, pltpu.roll(acc, HEAD_DIM, axis=1), acc))
        l0, l1 = halves
        o_ref[...] = jnp.where(lane < HEAD_DIM, acc_s[0] / l0, acc_s[1] / l1)
        lse_ref[0] = _rows_to_lanes(m_s[0] + jnp.log(l0), row == col)
        lse_ref[1] = _rows_to_lanes(m_s[1] + jnp.log(l1), row == col)
        if ng:
            finish()

    res = pl.pallas_call(
        body,
        name="fox_fwd",
        grid=(4, nq),
        in_specs=[
            pl.BlockSpec((tq, 128), lambda p, i: (i, p)),
            pl.BlockSpec((T, 128), lambda p, i: (0, 4 + p)),
            pl.BlockSpec((T, 128), lambda p, i: (0, 8 + p)),
            pl.BlockSpec((2, 1, tq), lambda p, i: (p, 0, i)),
            pl.BlockSpec((2, 1, T), lambda p, i: (p, 0, 0)),
            pl.BlockSpec((2, 1, nq), lambda p, i: (p, 0, 0)),
        ] + [_ANY] * ng,
        out_specs=[
            pl.BlockSpec((tq, 128), lambda p, i: (i, p)),
            pl.BlockSpec((2, 1, tq), lambda p, i: (p, 0, i)),
        ] + [_ANY] * ng,
        out_shape=[
            jax.ShapeDtypeStruct((T, FOX_W), F32),
            jax.ShapeDtypeStruct((N_FOX, 1, T), F32),
        ] + _gathered_shapes(gather),
        scratch_shapes=[
            pltpu.VMEM((2, tq, 128), BF16),
            pltpu.VMEM((2, tq, tq), F32),
            pltpu.VMEM((2, 2, tq, tq), F32),
            pltpu.VMEM((2, tq, tq), BF16),
            pltpu.VMEM((2, tq, 128), F32),
            pltpu.VMEM((2, tq, 128), F32),
            pltpu.VMEM((2, tq, 128), F32),
            pltpu.VMEM((2, 8, 128), F32),
        ] + (_comm_sems(ng) if ng else []),
        compiler_params=_cparams(("arbitrary", "arbitrary")),
    )(proj, proj, proj, c_col, c_row, c_ends, *gather)
    res = list(res)
    return res[0], res[1], res[2:]


def _sb_logs(zn, strict):
    e = jnp.exp2(jnp.abs(zn) * (-LOG2E))
    L = jnp.minimum(zn, 0.0) - jnp.log(1.0 + e)
    G = L - zn
    if strict is not None:
        L = jnp.where(strict, L, 0.0)
    return L, G


SB_DEAD = -110.0


def _sb_fwd(proj, tq):
    T = proj.shape[0]
    nq = T // tq

    def body(q_ref, k_ref, v_ref, o_ref, ltot_ref, live_ref, qh_s, z_s, g_s, tot_s, run_s, acc_s):
        i = pl.program_id(1)
        lane = lax.broadcasted_iota(jnp.int32, (1, 128), 1)
        row = lax.broadcasted_iota(jnp.int32, (tq, tq), 0)
        col = lax.broadcasted_iota(jnp.int32, (tq, tq), 1)
        strict = col < row
        later = jnp.where(row > col, 1.0, 0.0).astype(BF16)
        q = q_ref[...]
        for hh in range(2):
            qh_s[hh] = -_head_q(q, hh, lane)[0]
        run_s[...] = jnp.zeros_like(run_s)
        acc_s[...] = jnp.zeros_like(acc_s)

        def rows(t):
            return pl.ds(pl.multiple_of((i - t) * tq, tq), tq)

        def stage_a(t, slot):
            k = k_ref[rows(t), :]
            for hh in range(2):
                z_s[slot, hh] = _nt(qh_s[hh], k)

        def stage_b(t, slot, diag):
            for hh in range(2):
                L, g = _sb_logs(z_s[slot, hh], strict if diag else None)
                if diag:
                    g = jnp.where(strict, g, NEG)
                after = _split_dot(L, later, SB_SUM_TERMS)
                g_s[hh] = g + after
                first = L[:, 0:1]
                if SB_SUM_TERMS == 1:
                    first = first.astype(BF16).astype(F32)
                tot_s[hh] = jnp.broadcast_to(after[:, 0:1] + first, (tq, 128))

        def stage_c(t):
            v = v_ref[rows(t), :]
            for hh in range(2):
                run = run_s[hh]
                a = jnp.exp(g_s[hh] + _lanes2(run))
                acc_s[hh] += _nn(a.astype(BF16), v)
                run_s[hh] = run + tot_s[hh]

        def alive():
            return jnp.max(jnp.maximum(run_s[0], run_s[1])) > SB_DEAD

        done = _pipeline3(i + 1, stage_a, stage_b, stage_c, False, alive)
        ltot_ref[0] = _rows_to_lanes(run_s[0], row == col)
        ltot_ref[1] = _rows_to_lanes(run_s[1], row == col)
        o_ref[...] = jnp.where(lane < HEAD_DIM, acc_s[0], acc_s[1])
        at = lax.broadcasted_iota(jnp.int32, (1, nq), 1)

        @pl.when(i == 0)
        def _():
            live_ref[0] = jnp.zeros((1, nq), F32)

        live_ref[0] = jnp.where(at == i, done.astype(F32), live_ref[0])

    return pl.pallas_call(
        body,
        name="sb_fwd",
        grid=(4, nq),
        in_specs=[
            pl.BlockSpec((tq, 128), lambda p, i: (i, 12 + p)),
            pl.BlockSpec((T, 128), lambda p, i: (0, 16 + p)),
            pl.BlockSpec((T, 128), lambda p, i: (0, 20 + p)),
        ],
        out_specs=[
            pl.BlockSpec((tq, 128), lambda p, i: (i, p)),
            pl.BlockSpec((2, 1, tq), lambda p, i: (p, 0, i)),
            pl.BlockSpec((1, 1, nq), lambda p, i: (p, 0, 0)),
        ],
        out_shape=[
            jax.ShapeDtypeStruct((T, FOX_W), F32),
            jax.ShapeDtypeStruct((N_FOX, 1, T), F32),
            jax.ShapeDtypeStruct((N_FOX // 2, 1, nq), F32),
        ],
        scratch_shapes=[
            pltpu.VMEM((2, tq, 128), BF16),
            pltpu.VMEM((2, 2, tq, tq), F32),
            pltpu.VMEM((2, tq, tq), F32),
            pltpu.VMEM((2, tq, 128), F32),
            pltpu.VMEM((2, tq, 128), F32),
            pltpu.VMEM((2, tq, 128), F32),
        ],
        compiler_params=_cparams(("arbitrary", "arbitrary")),
    )(proj, proj, proj)


def _post_attn_fwd(fox_o, sb_o, gf, gs, w_out, x, tm):
    T, D = x.shape

    def body(f_ref, s_ref, gf_ref, gs_ref, w_ref, x_ref, x1_ref, mix_ref):
        f = f_ref[...]
        s = s_ref[...]
        mix_ref[:, :FOX_W] = (f * _rstd(f) * gf_ref[...]).astype(BF16)
        mix_ref[:, FOX_W:] = (s * _rstd(s) * gs_ref[...]).astype(BF16)
        x1_ref[...] = x_ref[...] + _nn(mix_ref[...], w_ref[...])

    return pl.pallas_call(
        body,
        name="post_attn_fwd",
        grid=(T // tm,),
        in_specs=[
            pl.BlockSpec((tm, FOX_W), lambda i: (i, 0)),
            pl.BlockSpec((tm, FOX_W), lambda i: (i, 0)),
            pl.BlockSpec((1, FOX_W), lambda i: (0, 0)),
            pl.BlockSpec((1, FOX_W), lambda i: (0, 0)),
            pl.BlockSpec((D, D), lambda i: (0, 0)),
            pl.BlockSpec((tm, D), lambda i: (i, 0)),
        ],
        out_specs=[
            pl.BlockSpec((tm, D), lambda i: (i, 0)),
            pl.BlockSpec((tm, D), lambda i: (i, 0)),
        ],
        out_shape=[jax.ShapeDtypeStruct((T, D), F32), jax.ShapeDtypeStruct((T, D), BF16)],
        compiler_params=_cparams(("arbitrary",)),
    )(fox_o, sb_o, gf, gs, w_out, x)


def _mem_kv_fwd(mem, gm, w_mkv):
    M, D = mem.shape
    N = w_mkv.shape[1]

    def body(mem_ref, g_ref, w_ref, m_ref, kv_ref):
        mv = mem_ref[...]
        m = (mv * _rstd(mv) * g_ref[...]).astype(BF16)
        m_ref[...] = m
        for n0 in range(0, N, 512):
            kv_ref[:, n0:n0 + 512] = _nn(m, w_ref[:, n0:n0 + 512]).astype(BF16)

    return pl.pallas_call(
        body,
        name="mem_kv_fwd",
        out_shape=[jax.ShapeDtypeStruct((M, D), BF16), jax.ShapeDtypeStruct((M, N), BF16)],
        compiler_params=_cparams(),
    )(mem, gm, w_mkv)


def _xattn_probs(qb, kv, h):
    k = kv[:, h * MEM_HD:(h + 1) * MEM_HD]
    s = _nt(qb[:, h * MEM_HD:(h + 1) * MEM_HD], k) * (MEM_HD ** -0.5)
    s = s - jnp.max(s, axis=1, keepdims=True)
    p = jnp.exp(s)
    return p / jnp.sum(p, axis=1, keepdims=True)


def _xattn_fwd(x1, g2, w_mq, kv, w_mo, tm):
    T, D = x1.shape
    M = kv.shape[0]

    def body(x_ref, g_ref, wq_ref, kv_ref, wo_ref, x2_ref, h_ref, q_ref, om_ref):
        xv = x_ref[...]
        h = (xv * _rstd(xv) * g_ref[...]).astype(BF16)
        h_ref[...] = h
        q_ref[...] = _nn(h, wq_ref[...]).astype(BF16)
        qb = q_ref[...]
        kvv = kv_ref[...]
        for hd in range(N_MEM_HEADS):
            p = _xattn_probs(qb, kvv, hd)
            v = kvv[:, D + hd * MEM_HD:D + (hd + 1) * MEM_HD]
            om_ref[:, hd * MEM_HD:(hd + 1) * MEM_HD] = _nn(p.astype(BF16), v).astype(BF16)
        x2_ref[...] = xv + _nn(om_ref[...], wo_ref[...])

    return pl.pallas_call(
        body,
        name="xattn_fwd",
        grid=(T // tm,),
        in_specs=[
            pl.BlockSpec((tm, D), lambda i: (i, 0)),
            pl.BlockSpec((1, D), lambda i: (0, 0)),
            pl.BlockSpec((D, D), lambda i: (0, 0)),
            pl.BlockSpec((M, 2 * D), lambda i: (0, 0)),
            pl.BlockSpec((D, D), lambda i: (0, 0)),
        ],
        out_specs=[pl.BlockSpec((tm, D), lambda i: (i, 0))] * 4,
        out_shape=[jax.ShapeDtypeStruct((T, D), F32)] + [jax.ShapeDtypeStruct((T, D), BF16)] * 3,
        compiler_params=_cparams(("arbitrary",)),
    )(x1, g2, w_mq, kv, w_mo)


def _conv_taps(ext_ref, tm, back):
    if back:
        return ext_ref[pl.ds(6, tm), :], ext_ref[pl.ds(7, tm), :], ext_ref[pl.ds(8, tm), :]
    return ext_ref[pl.ds(0, tm), :], ext_ref[pl.ds(1, tm), :], ext_ref[pl.ds(2, tm), :]


def _ffn_fwd(x2, g3, w_up, conv_w, conv_b, w_down, tm):
    T, D = x2.shape
    fc = FF_CHUNK
    nj = D_FF // fc

    def body(x_ref, g_ref, wg_ref, wv_ref, cwg_ref, cwv_ref, cbg_ref, cbv_ref, wd_ref,
             x3_ref, h_ref, ug_ref, uv_ref, yg_ref, yv_ref, a_ref, acc_ref, carry_ref, ext_ref):
        i = pl.program_id(0)
        j = pl.program_id(1)

        @pl.when(j == 0)
        def _():
            xv = x_ref[...]
            h_ref[...] = (xv * _rstd(xv) * g_ref[...]).astype(BF16)
            acc_ref[...] = xv

        @pl.when(i == 0)
        def _():
            carry_ref[j] = jnp.zeros((2, 8, fc), F32)

        h = h_ref[...]
        halves = []
        for part, (w_ref, cw_ref, cb_ref, u_ref, y_ref) in enumerate(
                ((wg_ref, cwg_ref, cbg_ref, ug_ref, yg_ref), (wv_ref, cwv_ref, cbv_ref, uv_ref, yv_ref))):
            u = _nn(h, w_ref[...])
            u_ref[...] = u.astype(BF16)
            ext = ext_ref.at[part]
            ext[pl.ds(0, 8), :] = carry_ref[j, part]
            ext[pl.ds(8, tm), :] = u
            carry_ref[j, part] = u[tm - 8:, :]
            u2, u1, u0 = _conv_taps(ext, tm, True)
            cw = cw_ref[...]
            y = cb_ref[...] + cw[0:1] * u2 + cw[1:2] * u1 + cw[2:3] * u0
            y_ref[...] = y.astype(BF16)
            halves.append(y)
        gate, val = halves
        a = (gate * jax.nn.sigmoid(gate) * val).astype(BF16)
        a_ref[...] = a
        acc_ref[...] += _nn(a, wd_ref[...])

        @pl.when(j == nj - 1)
        def _():
            x3_ref[...] = acc_ref[...]

    return pl.pallas_call(
        body,
        name="ffn_fwd",
        grid=(T // tm, nj),
        in_specs=[
            pl.BlockSpec((tm, D), lambda i, j: (i, 0)),
            pl.BlockSpec((1, D), lambda i, j: (0, 0)),
            pl.BlockSpec((D, fc), lambda i, j: (0, j)),
            pl.BlockSpec((D, fc), lambda i, j: (0, nj + j)),
            pl.BlockSpec((3, fc), lambda i, j: (0, j)),
            pl.BlockSpec((3, fc), lambda i, j: (0, nj + j)),
            pl.BlockSpec((1, fc), lambda i, j: (0, j)),
            pl.BlockSpec((1, fc), lambda i, j: (0, nj + j)),
            pl.BlockSpec((fc, D), lambda i, j: (j, 0)),
        ],
        out_specs=[
            pl.BlockSpec((tm, D), lambda i, j: (i, 0)),
            pl.BlockSpec((tm, D), lambda i, j: (i, 0)),
        ] + [pl.BlockSpec((tm, fc), lambda i, j: (i, j))] * 5,
        out_shape=[
            jax.ShapeDtypeStruct((T, D), F32),
            jax.ShapeDtypeStruct((T, D), BF16),
        ] + [jax.ShapeDtypeStruct((T, D_FF), BF16)] * 5,
        scratch_shapes=[
            pltpu.VMEM((tm, D), F32),
            pltpu.VMEM((nj, 2, 8, fc), F32),
            pltpu.VMEM((2, tm + 8, fc), F32),
        ],
        compiler_params=_cparams(("arbitrary", "arbitrary")),
    )(x2, g3, w_up, w_up, conv_w, conv_w, conv_b, conv_b, w_down)


def _loss_head(x3, gfin, target, tm):
    T, D = x3.shape

    def body(x_ref, g_ref, t_ref, dx_ref, loss_ref, dg_ref):
        i = pl.program_id(0)

        @pl.when(i == 0)
        def _():
            loss_ref[...] = jnp.zeros_like(loss_ref)
            dg_ref[...] = jnp.zeros_like(dg_ref)

        xv = x_ref[...]
        g = g_ref[...]
        r = _rstd(xv)
        xhat = xv * r
        err = xhat * g - t_ref[...]
        part = jnp.sum(jnp.sum(err * err, axis=1, keepdims=True), axis=0, keepdims=True) * (0.5 / D)
        loss_ref[...] += jnp.broadcast_to(part, loss_ref.shape)
        dy = err * (1.0 / D)
        dg_ref[...] += jnp.sum(dy * xhat, axis=0, keepdims=True)
        dxhat = dy * g
        dx_ref[...] = r * (dxhat - xhat * jnp.mean(dxhat * xhat, axis=-1, keepdims=True))

    return pl.pallas_call(
        body,
        name="loss_head",
        grid=(T // tm,),
        in_specs=[
            pl.BlockSpec((tm, D), lambda i: (i, 0)),
            pl.BlockSpec((1, D), lambda i: (0, 0)),
            pl.BlockSpec((tm, D), lambda i: (i, 0)),
        ],
        out_specs=[
            pl.BlockSpec((tm, D), lambda i: (i, 0)),
            pl.BlockSpec((8, 128), lambda i: (0, 0)),
            pl.BlockSpec((1, D), lambda i: (0, 0)),
        ],
        out_shape=[
            jax.ShapeDtypeStruct((T, D), F32),
            jax.ShapeDtypeStruct((8, 128), F32),
            jax.ShapeDtypeStruct((1, D), F32),
        ],
        compiler_params=_cparams(("arbitrary",)),
    )(x3, gfin, target)


def _ffn_bwd(dx3, x2, g3, ug, uv, yg, yv, conv_w, w_down, w_up, tm):
    T, D = x2.shape
    fc = FF_CHUNK
    nj = D_FF // fc
    nt = T // tm

    def rev(i):
        return nt - 1 - i

    def body(dx3_ref, x_ref, g_ref, ug_ref, uv_ref, yg_ref, yv_ref, cwg_ref, cwv_ref,
             wd_ref, wug_ref, wuv_ref,
             dx2_ref, dug_ref, duv_ref, dg_ref, dcg_ref, dcv_ref,
             acc_ref, carry_ref, ext_ref):
        i = pl.program_id(0)
        j = pl.program_id(1)
        cols = pl.ds(pl.multiple_of(j * fc, fc), fc)

        @pl.when(j == 0)
        def _():
            acc_ref[...] = jnp.zeros_like(acc_ref)

        @pl.when((i == 0) & (j == 0))
        def _():
            dg_ref[...] = jnp.zeros_like(dg_ref)
            dcg_ref[...] = jnp.zeros_like(dcg_ref)
            dcv_ref[...] = jnp.zeros_like(dcv_ref)

        @pl.when(i == 0)
        def _():
            carry_ref[j] = jnp.zeros((2, 8, fc), F32)

        da = _nt(dx3_ref[...].astype(BF16), wd_ref[...])
        gate = yg_ref[...].astype(F32)
        val = yv_ref[...].astype(F32)
        sig = jax.nn.sigmoid(gate)
        silu = gate * sig
        dys = (da * val * (sig * (1.0 + gate * (1.0 - sig))), da * silu)
        for part, (dy, u_ref, cw_ref, du_ref, wu_ref, dc_ref) in enumerate(
                ((dys[0], ug_ref, cwg_ref, dug_ref, wug_ref, dcg_ref),
                 (dys[1], uv_ref, cwv_ref, duv_ref, wuv_ref, dcv_ref))):
            ext = ext_ref.at[part]
            ext[pl.ds(0, tm), :] = dy
            ext[pl.ds(tm, 8), :] = carry_ref[j, part]
            carry_ref[j, part] = dy[:8, :]
            d0, d1, d2 = _conv_taps(ext, tm, False)
            u = u_ref[...].astype(F32)
            upd = jnp.concatenate([
                jnp.sum(u * d2, axis=0, keepdims=True),
                jnp.sum(u * d1, axis=0, keepdims=True),
                jnp.sum(u * d0, axis=0, keepdims=True),
                jnp.sum(d0, axis=0, keepdims=True),
                jnp.zeros((4, fc), F32)], axis=0)
            dc_ref[:, cols] += upd
            cw = cw_ref[...]
            du = (cw[2:3] * d0 + cw[1:2] * d1 + cw[0:1] * d2).astype(BF16)
            du_ref[...] = du
            acc_ref[...] += _nt(du, wu_ref[...])

        @pl.when(j == nj - 1)
        def _():
            dx, dg = _norm_bwd(x_ref[...], g_ref[...], acc_ref[...])
            dx2_ref[...] = dx3_ref[...] + dx
            dg_ref[...] += dg

    return pl.pallas_call(
        body,
        name="ffn_bwd",
        grid=(nt, nj),
        in_specs=[
            pl.BlockSpec((tm, D), lambda i, j: (rev(i), 0)),
            pl.BlockSpec((tm, D), lambda i, j: (rev(i), 0)),
            pl.BlockSpec((1, D), lambda i, j: (0, 0)),
            pl.BlockSpec((tm, fc), lambda i, j: (rev(i), j)),
            pl.BlockSpec((tm, fc), lambda i, j: (rev(i), j)),
            pl.BlockSpec((tm, fc), lambda i, j: (rev(i), j)),
            pl.BlockSpec((tm, fc), lambda i, j: (rev(i), j)),
            pl.BlockSpec((3, fc), lambda i, j: (0, j)),
            pl.BlockSpec((3, fc), lambda i, j: (0, nj + j)),
            pl.BlockSpec((fc, D), lambda i, j: (j, 0)),
            pl.BlockSpec((D, fc), lambda i, j: (0, j)),
            pl.BlockSpec((D, fc), lambda i, j: (0, nj + j)),
        ],
        out_specs=[
            pl.BlockSpec((tm, D), lambda i, j: (rev(i), 0)),
            pl.BlockSpec((tm, fc), lambda i, j: (rev(i), j)),
            pl.BlockSpec((tm, fc), lambda i, j: (rev(i), j)),
            pl.BlockSpec((1, D), lambda i, j: (0, 0)),
            pl.BlockSpec((8, D_FF), lambda i, j: (0, 0)),
            pl.BlockSpec((8, D_FF), lambda i, j: (0, 0)),
        ],
        out_shape=[
            jax.ShapeDtypeStruct((T, D), F32),
            jax.ShapeDtypeStruct((T, D_FF), BF16),
            jax.ShapeDtypeStruct((T, D_FF), BF16),
            jax.ShapeDtypeStruct((1, D), F32),
            jax.ShapeDtypeStruct((8, D_FF), F32),
            jax.ShapeDtypeStruct((8, D_FF), F32),
        ],
        scratch_shapes=[
            pltpu.VMEM((tm, D), F32),
            pltpu.VMEM((nj, 2, 8, fc), F32),
            pltpu.VMEM((2, tm + 8, fc), F32),
        ],
        compiler_params=_cparams(("arbitrary", "arbitrary")),
    )(dx3, x2, g3, ug, uv, yg, yv, conv_w, conv_w, w_down, w_up, w_up)


def _xattn_bwd(dx2, x1, g2, qb, kv, w_mo, w_mq, tm):
    T, D = x1.shape
    M = kv.shape[0]

    def body(dx2_ref, x_ref, g_ref, q_ref, kv_ref, wo_ref, wq_ref, dx1_ref, dq_ref, dkv_ref, dg_ref):
        i = pl.program_id(0)

        @pl.when(i == 0)
        def _():
            dkv_ref[...] = jnp.zeros_like(dkv_ref)
            dg_ref[...] = jnp.zeros_like(dg_ref)

        dxv = dx2_ref[...]
        dom = _nt(dxv.astype(BF16), wo_ref[...]).astype(BF16)
        qb_ = q_ref[...]
        kvv = kv_ref[...]
        for hd in range(N_MEM_HEADS):
            sl = slice(hd * MEM_HD, (hd + 1) * MEM_HD)
            vsl = slice(D + hd * MEM_HD, D + (hd + 1) * MEM_HD)
            p = _xattn_probs(qb_, kvv, hd)
            dp = _nt(dom[:, sl], kvv[:, vsl])
            ds = (p * (dp - jnp.sum(p * dp, axis=1, keepdims=True)) * (MEM_HD ** -0.5)).astype(BF16)
            dq_ref[:, sl] = _nn(ds, kvv[:, sl]).astype(BF16)
            dkv_ref[:, sl] += _tn(ds, qb_[:, sl])
            dkv_ref[:, vsl] += _tn(p.astype(BF16), dom[:, sl])
        dh = _nt(dq_ref[...], wq_ref[...])
        dx, dg = _norm_bwd(x_ref[...], g_ref[...], dh)
        dx1_ref[...] = dxv + dx
        dg_ref[...] += dg

    return pl.pallas_call(
        body,
        name="xattn_bwd",
        grid=(T // tm,),
        in_specs=[
            pl.BlockSpec((tm, D), lambda i: (i, 0)),
            pl.BlockSpec((tm, D), lambda i: (i, 0)),
            pl.BlockSpec((1, D), lambda i: (0, 0)),
            pl.BlockSpec((tm, D), lambda i: (i, 0)),
            pl.BlockSpec((M, 2 * D), lambda i: (0, 0)),
            pl.BlockSpec((D, D), lambda i: (0, 0)),
            pl.BlockSpec((D, D), lambda i: (0, 0)),
        ],
        out_specs=[
            pl.BlockSpec((tm, D), lambda i: (i, 0)),
            pl.BlockSpec((tm, D), lambda i: (i, 0)),
            pl.BlockSpec((M, 2 * D), lambda i: (0, 0)),
            pl.BlockSpec((1, D), lambda i: (0, 0)),
        ],
        out_shape=[
            jax.ShapeDtypeStruct((T, D), F32),
            jax.ShapeDtypeStruct((T, D), BF16),
            jax.ShapeDtypeStruct((M, 2 * D), F32),
            jax.ShapeDtypeStruct((1, D), F32),
        ],
        compiler_params=_cparams(("arbitrary",)),
    )(dx2, x1, g2, qb, kv, w_mo, w_mq)


def _mem_kv_bwd(mem, gm, mb, dkv, w_mkv):
    M, D = mem.shape
    N = dkv.shape[1]

    def body(mem_ref, g_ref, m_ref, dkv_ref, w_ref, dw_ref, dg_ref):
        dkvb = dkv_ref[...].astype(BF16)
        for n0 in range(0, N, 512):
            dw_ref[:, n0:n0 + 512] = _tn(m_ref[...], dkvb[:, n0:n0 + 512]).astype(BF16)
        dm = _nt(dkvb, w_ref[...])
        mv = mem_ref[...]
        dg_ref[...] = jnp.sum(dm * (mv * _rstd(mv)), axis=0, keepdims=True)

    return pl.pallas_call(
        body,
        name="mem_kv_bwd",
        out_shape=[jax.ShapeDtypeStruct((D, N), BF16), jax.ShapeDtypeStruct((1, D), F32)],
        compiler_params=_cparams(),
    )(mem, gm, mb, dkv, w_mkv)


def _post_attn_bwd(dx1, fox_o, sb_o, gf, gs, w_out, tm):
    T, D = dx1.shape

    def body(dx_ref, f_ref, s_ref, gf_ref, gs_ref, w_ref, df_ref, ds_ref, dgf_ref, dgs_ref):
        i = pl.program_id(0)

        @pl.when(i == 0)
        def _():
            dgf_ref[...] = jnp.zeros_like(dgf_ref)
            dgs_ref[...] = jnp.zeros_like(dgs_ref)

        dmix = _nt(dx_ref[...].astype(BF16), w_ref[...])
        d, dg = _norm_bwd(f_ref[...], gf_ref[...], dmix[:, :FOX_W])
        df_ref[...] = d
        dgf_ref[...] += dg
        d, dg = _norm_bwd(s_ref[...], gs_ref[...], dmix[:, FOX_W:])
        ds_ref[...] = d
        dgs_ref[...] += dg

    return pl.pallas_call(
        body,
        name="post_attn_bwd",
        grid=(T // tm,),
        in_specs=[
            pl.BlockSpec((tm, D), lambda i: (i, 0)),
            pl.BlockSpec((tm, FOX_W), lambda i: (i, 0)),
            pl.BlockSpec((tm, FOX_W), lambda i: (i, 0)),
            pl.BlockSpec((1, FOX_W), lambda i: (0, 0)),
            pl.BlockSpec((1, FOX_W), lambda i: (0, 0)),
            pl.BlockSpec((D, D), lambda i: (0, 0)),
        ],
        out_specs=[
            pl.BlockSpec((tm, FOX_W), lambda i: (i, 0)),
            pl.BlockSpec((tm, FOX_W), lambda i: (i, 0)),
            pl.BlockSpec((1, FOX_W), lambda i: (0, 0)),
            pl.BlockSpec((1, FOX_W), lambda i: (0, 0)),
        ],
        out_shape=[
            jax.ShapeDtypeStruct((T, FOX_W), F32),
            jax.ShapeDtypeStruct((T, FOX_W), F32),
            jax.ShapeDtypeStruct((1, FOX_W), F32),
            jax.ShapeDtypeStruct((1, FOX_W), F32),
        ],
        compiler_params=_cparams(("arbitrary",)),
    )(dx1, fox_o, sb_o, gf, gs, w_out)


def _sb_bwd(proj, ltot, live, d_o, tq):
    T = proj.shape[0]
    nq = T // tq

    def body(q_ref, k_ref, v_ref, lt_ref, live_ref, do_ref, dq_ref, dk_ref, dv_ref,
             qh_s, doh_s, lt_s, z_s, da_s, ab_s, dzb_s, run_s, runw_s, dq_s):
        i = pl.program_id(1)

        @pl.when(i == 0)
        def _():
            dk_ref[...] = jnp.zeros_like(dk_ref)
            dv_ref[...] = jnp.zeros_like(dv_ref)

        lane = lax.broadcasted_iota(jnp.int32, (1, 128), 1)
        row = lax.broadcasted_iota(jnp.int32, (tq, tq), 0)
        col = lax.broadcasted_iota(jnp.int32, (tq, tq), 1)
        strict = col < row
        upto = jnp.where(row <= col, 1.0, 0.0).astype(BF16)
        before = jnp.where(row < col, 1.0, 0.0).astype(BF16)
        q = q_ref[...]
        dov = do_ref[...]
        for hh in range(2):
            qh, hmask = _head_q(q, hh, lane)
            qh_s[hh] = -qh
            doh_s[hh] = jnp.where(hmask, dov, 0.0).astype(BF16)
            lt_s[hh] = jnp.broadcast_to(_lanes_to_rows(lt_ref[hh], row == col), (tq, 128))
        run_s[...] = jnp.zeros_like(run_s)
        runw_s[...] = jnp.zeros_like(runw_s)
        dq_s[...] = jnp.zeros_like(dq_s)

        at = lax.broadcasted_iota(jnp.int32, (1, nq), 1)
        count = jnp.sum(jnp.where(at == i, live_ref[0], 0.0), axis=1, keepdims=True)[0, 0].astype(jnp.int32)
        n_live = jnp.clip(count, 1, i + 1)
        oldest = i + 1 - n_live

        def rows(t):
            return pl.ds(pl.multiple_of((oldest + t) * tq, tq), tq)

        def stage_a(t, slot):
            k = k_ref[rows(t), :]
            v = v_ref[rows(t), :]
            for hh in range(2):
                z_s[slot, hh] = _nt(qh_s[hh], k)
                da_s[slot, hh] = _nt(doh_s[hh], v)

        def stage_b(t, slot, diag):
            for hh in range(2):
                L, g = _sb_logs(z_s[slot, hh], strict if diag else None)
                upto_s = _split_dot(L, upto, SB_SUM_TERMS)
                run = run_s[hh]
                arg = (g + _lanes2(lt_s[hh] - run)) - upto_s
                if diag:
                    arg = jnp.where(strict, arg, NEG)
                a = jnp.exp(arg)
                w = a * da_s[slot, hh]
                w_before = _split_dot(w, before, SB_SUM_TERMS)
                run_w = runw_s[hh]
                d_keep = w_before + _lanes2(run_w)
                beta = jnp.exp(g)
                ndz = beta * (w + d_keep) - w
                if diag:
                    ndz = jnp.where(strict, ndz, 0.0)
                dzb_s[hh] = ndz.astype(BF16)
                ab_s[hh] = a.astype(BF16)
                run_s[hh] = run + jnp.broadcast_to(upto_s[:, tq - 1:tq], (tq, 128))
                runw_s[hh] = run_w + jnp.broadcast_to(w_before[:, tq - 1:tq] + w[:, tq - 1:tq], (tq, 128))

        def stage_c(t):
            k = k_ref[rows(t), :]
            dk_blk = None
            dv_blk = None
            for hh in range(2):
                dzb = dzb_s[hh]
                dq_s[hh] += _nn(dzb, k)
                dk_h = _tn(dzb, qh_s[hh])
                dv_h = _tn(ab_s[hh], doh_s[hh])
                dk_blk = dk_h if dk_blk is None else dk_blk + dk_h
                dv_blk = dv_h if dv_blk is None else dv_blk + dv_h
            dk_ref[rows(t), :] += dk_blk
            dv_ref[rows(t), :] += dv_blk

        _pipeline3(n_live, stage_a, stage_b, stage_c, True)
        dq_ref[...] = (jnp.where(lane < HEAD_DIM, dq_s[0], dq_s[1]) * -(HEAD_DIM ** -0.5)).astype(BF16)

    return pl.pallas_call(
        body,
        name="sb_bwd",
        grid=(4, nq),
        in_specs=[
            pl.BlockSpec((tq, 128), lambda p, i: (i, 12 + p)),
            pl.BlockSpec((T, 128), lambda p, i: (0, 16 + p)),
            pl.BlockSpec((T, 128), lambda p, i: (0, 20 + p)),
            pl.BlockSpec((2, 1, tq), lambda p, i: (p, 0, i)),
            pl.BlockSpec((1, 1, nq), lambda p, i: (p, 0, 0)),
            pl.BlockSpec((tq, 128), lambda p, i: (i, p)),
        ],
        out_specs=[
            pl.BlockSpec((tq, 128), lambda p, i: (i, p)),
            pl.BlockSpec((T, 128), lambda p, i: (0, p)),
            pl.BlockSpec((T, 128), lambda p, i: (0, p)),
        ],
        out_shape=[
            jax.ShapeDtypeStruct((T, FOX_W), BF16),
            jax.ShapeDtypeStruct((T, FOX_W), F32),
            jax.ShapeDtypeStruct((T, FOX_W), F32),
        ],
        scratch_shapes=[
            pltpu.VMEM((2, tq, 128), BF16),
            pltpu.VMEM((2, tq, 128), BF16),
            pltpu.VMEM((2, tq, 128), F32),
            pltpu.VMEM((2, 2, tq, tq), F32),
            pltpu.VMEM((2, 2, tq, tq), F32),
            pltpu.VMEM((2, tq, tq), BF16),
            pltpu.VMEM((2, tq, tq), BF16),
            pltpu.VMEM((2, tq, 128), F32),
            pltpu.VMEM((2, tq, 128), F32),
            pltpu.VMEM((2, tq, 128), F32),
        ],
        compiler_params=_cparams(("arbitrary", "arbitrary")),
    )(proj, proj, proj, ltot, live, d_o)


def _fox_bwd(proj, c_col, c_row, c_ends, lse, d_o, o, tq, scatter=()):
    T = proj.shape[0]
    nq = T // tq
    ns = len(scatter)

    def body(*refs):
        q_ref, k_ref, v_ref, cq_ref, ck_ref, cke_ref, lse_ref, do_ref, o_ref = refs[:9]
        dq_ref, dk_ref, dv_ref, dck_ref, dcq_ref = refs[9 + ns:14 + ns]
        (qh_s, doh_s, delta_s, shift_s, z_s, dp_s, pb_s, dsb_s, rs_s, dq_s,
         kn_s) = refs[14 + 2 * ns:25 + 2 * ns]
        i = pl.program_id(1)
        if ns:
            pair = pl.program_id(0)
            finish = _ride_along(_Scatter(refs[9:9 + ns], refs[14 + ns:14 + 2 * ns], *refs[25 + 2 * ns:]),
                                 (pair == 0) & (i == 0), None, (pair == 3) & (i == nq - 1))
        lane = lax.broadcasted_iota(jnp.int32, (1, 128), 1)

        @pl.when(i == 0)
        def _():
            dk_ref[...] = jnp.zeros_like(dk_ref)
            dv_ref[...] = jnp.zeros_like(dv_ref)
            dck_ref[...] = jnp.zeros_like(dck_ref)
            _fox_key_norms(k_ref, kn_s, lane)

        row = lax.broadcasted_iota(jnp.int32, (tq, tq), 0)
        col = lax.broadcasted_iota(jnp.int32, (tq, tq), 1)
        q = q_ref[...]
        dov = do_ref[...]
        ov = o_ref[...]
        for hh in range(2):
            qh, hmask = _head_q(q, hh, lane)
            dohb = jnp.where(hmask, dov, 0.0).astype(BF16)
            qh_s[hh] = qh
            doh_s[hh] = dohb
            delta_s[hh] = jnp.broadcast_to(jnp.sum(dohb.astype(F32) * ov, axis=1, keepdims=True), (tq, tq))
            shift_s[hh] = jnp.broadcast_to(_lanes_to_rows(cq_ref[hh] - lse_ref[hh], row == col), (tq, tq))
        rs_s[...] = jnp.zeros_like(rs_s)
        dq_s[...] = jnp.zeros_like(dq_s)

        def rows(t):
            return pl.ds(pl.multiple_of((i - t) * tq, tq), tq)

        def stage_a(t, slot):
            k = k_ref[rows(t), :]
            v = v_ref[rows(t), :]
            for hh in range(2):
                z_s[slot, hh] = _nt(qh_s[hh], k)
                dp_s[slot, hh] = _nt(doh_s[hh], v)

        def stage_b(t, slot, diag):
            for hh in range(2):
                s = z_s[slot, hh] + shift_s[hh] - ck_ref[hh, :, rows(t)]
                if diag:
                    s = jnp.where(col <= row, s, NEG)
                p = jnp.exp(s)
                ds = p * (dp_s[slot, hh] - delta_s[hh])
                pb_s[hh] = p.astype(BF16)
                dsb_s[hh] = ds.astype(BF16)
                dck_ref[hh, :, rows(t)] += jnp.sum(ds, axis=0, keepdims=True)
                rs_s[hh] += jnp.sum(ds, axis=1, keepdims=True)

        def stage_c(t):
            k = k_ref[rows(t), :]
            dk_blk = None
            dv_blk = None
            for hh in range(2):
                dsb = dsb_s[hh]
                dq_s[hh] += _nn(dsb, k)
                dk_h = _tn(dsb, qh_s[hh])
                dv_h = _tn(pb_s[hh], doh_s[hh])
                dk_blk = dk_h if dk_blk is None else dk_blk + dk_h
                dv_blk = dv_h if dv_blk is None else dv_blk + dv_h
            dk_ref[rows(t), :] += dk_blk
            dv_ref[rows(t), :] += dv_blk

        _pipeline3(_fox_live_blocks(i, qh_s, kn_s, cq_ref, cke_ref), stage_a, stage_b, stage_c, False)
        dcq_ref[0] = _rows_to_lanes(rs_s[0], row == col)
        dcq_ref[1] = _rows_to_lanes(rs_s[1], row == col)
        dq_ref[...] = (jnp.where(lane < HEAD_DIM, dq_s[0], dq_s[1]) * (HEAD_DIM ** -0.5)).astype(BF16)
        if ns:
            finish()

    res = pl.pallas_call(
        body,
        name="fox_bwd",
        grid=(4, nq),
        in_specs=[
            pl.BlockSpec((tq, 128), lambda p, i: (i, p)),
            pl.BlockSpec((T, 128), lambda p, i: (0, 4 + p)),
            pl.BlockSpec((T, 128), lambda p, i: (0, 8 + p)),
            pl.BlockSpec((2, 1, tq), lambda p, i: (p, 0, i)),
            pl.BlockSpec((2, 1, T), lambda p, i: (p, 0, 0)),
            pl.BlockSpec((2, 1, nq), lambda p, i: (p, 0, 0)),
            pl.BlockSpec((2, 1, tq), lambda p, i: (p, 0, i)),
            pl.BlockSpec((tq, 128), lambda p, i: (i, p)),
            pl.BlockSpec((tq, 128), lambda p, i: (i, p)),
        ] + [_ANY] * ns,
        out_specs=[
            pl.BlockSpec((tq, 128), lambda p, i: (i, p)),
            pl.BlockSpec((T, 128), lambda p, i: (0, p)),
            pl.BlockSpec((T, 128), lambda p, i: (0, p)),
            pl.BlockSpec((2, 1, T), lambda p, i: (p, 0, 0)),
            pl.BlockSpec((2, 1, tq), lambda p, i: (p, 0, i)),
        ] + [_ANY] * ns,
        out_shape=[
            jax.ShapeDtypeStruct((T, FOX_W), BF16),
            jax.ShapeDtypeStruct((T, FOX_W), F32),
            jax.ShapeDtypeStruct((T, FOX_W), F32),
            jax.ShapeDtypeStruct((N_FOX, 1, T), F32),
            jax.ShapeDtypeStruct((N_FOX, 1, T), F32),
        ] + [jax.ShapeDtypeStruct(b.shape, b.dtype) for b in scatter],
        scratch_shapes=[
            pltpu.VMEM((2, tq, 128), BF16),
            pltpu.VMEM((2, tq, 128), BF16),
            pltpu.VMEM((2, tq, tq), F32),
            pltpu.VMEM((2, tq, tq), F32),
            pltpu.VMEM((2, 2, tq, tq), F32),
            pltpu.VMEM((2, 2, tq, tq), F32),
            pltpu.VMEM((2, tq, tq), BF16),
            pltpu.VMEM((2, tq, tq), BF16),
            pltpu.VMEM((2, tq, 128), F32),
            pltpu.VMEM((2, tq, 128), F32),
            pltpu.VMEM((2, 8, 128), F32),
        ] + (_comm_sems(ns) if ns else []),
        compiler_params=_cparams(("arbitrary", "arbitrary")),
    )(proj, proj, proj, c_col, c_row, c_ends, lse, d_o, o, *scatter)
    res = list(res)
    return (*res[:5], res[5:])


def _forget_bwd(dcq, dck, xf, tc):
    H, T = xf.shape
    nc = T // tc

    def body(dcq_ref, dck_ref, xf_ref, dxf_ref, db_ref):
        row = lax.broadcasted_iota(jnp.int32, (tc, tc), 0)
        col = lax.broadcasted_iota(jnp.int32, (tc, tc), 1)
        from_here = jnp.where(row >= col, 1.0, 0.0).astype(BF16)

        def chunk(n, carry):
            run, db = carry
            cs = pl.multiple_of((nc - 1 - n) * tc, tc)
            dc = dcq_ref[:, pl.ds(cs, tc)] - dck_ref[:, pl.ds(cs, tc)]
            dlogf = _split_dot(dc, from_here, 3) + run
            xfv = xf_ref[:, pl.ds(cs, tc)]
            dxf = dlogf * jax.nn.sigmoid(-xfv)
            dxf_ref[:, pl.ds(cs, tc)] = dxf
            return dlogf[:, 0:1], db + jnp.sum(dxf, axis=1, keepdims=True)

        _, db = lax.fori_loop(0, nc, chunk, (jnp.zeros((H, 1), F32), jnp.zeros((H, 1), F32)))
        db_ref[...] = db

    return pl.pallas_call(
        body,
        name="forget_bwd",
        out_shape=[jax.ShapeDtypeStruct((H, T), F32), jax.ShapeDtypeStruct((H, 1), F32)],
        compiler_params=_cparams(),
    )(dcq, dck, xf)


def _inproj_bwd(dproj, w_in_pad, x, g1, dx1, tm, scatter=()):
    T, D = x.shape
    N = dproj.shape[1]
    ns = len(scatter)
    nt = T // tm

    def body(*refs):
        dp_ref, w_ref, x_ref, g_ref, dx1_ref = refs[:5]
        dx_ref, dg_ref = refs[5 + ns:7 + ns]
        i = pl.program_id(0)
        if ns:
            exchange = _Scatter(refs[5:5 + ns], refs[7 + ns:7 + 2 * ns], *refs[7 + 2 * ns:])

            @pl.when(i == 0)
            def _():
                exchange.start()

        @pl.when(i == 0)
        def _():
            dg_ref[...] = jnp.zeros_like(dg_ref)

        dh = _nt(dp_ref[...], w_ref[...])
        dx, dg = _norm_bwd(x_ref[...], g_ref[...], dh)
        dx_ref[...] = dx1_ref[...] + dx
        dg_ref[...] += dg
        if ns:
            @pl.when(i == nt - 1)
            def _():
                exchange.finish()

    res = pl.pallas_call(
        body,
        name="inproj_bwd",
        grid=(nt,),
        in_specs=[
            pl.BlockSpec((tm, N), lambda i: (i, 0)),
            pl.BlockSpec((D, N), lambda i: (0, 0)),
            pl.BlockSpec((tm, D), lambda i: (i, 0)),
            pl.BlockSpec((1, D), lambda i: (0, 0)),
            pl.BlockSpec((tm, D), lambda i: (i, 0)),
        ] + [_ANY] * ns,
        out_specs=[
            pl.BlockSpec((tm, D), lambda i: (i, 0)),
            pl.BlockSpec((1, D), lambda i: (0, 0)),
        ] + [_ANY] * ns,
        out_shape=[jax.ShapeDtypeStruct((T, D), F32), jax.ShapeDtypeStruct((1, D), F32)]
        + [jax.ShapeDtypeStruct(b.shape, b.dtype) for b in scatter],
        scratch_shapes=_comm_sems(ns) if ns else [],
        compiler_params=_cparams(("arbitrary",)),
    )(dproj, w_in_pad, x, g1, dx1, *scatter)
    res = list(res)
    return res[0], res[1], res[2:]


def _matmul_tn(a, b, name, cast_b=False):
    T, K = a.shape
    N = b.shape[1]
    bt = min(T, 512)
    bk = _tile_div(K, 1536)
    bn = _tile_div(N, 1536)
    nt = T // bt

    def body(a_ref, b_ref, o_ref, acc_ref):
        t = pl.program_id(2)

        @pl.when(t == 0)
        def _():
            acc_ref[...] = jnp.zeros_like(acc_ref)

        bv = b_ref[...]
        if cast_b:
            bv = bv.astype(BF16)
        acc_ref[...] += _tn(a_ref[...], bv)

        @pl.when(t == nt - 1)
        def _():
            o_ref[...] = acc_ref[...].astype(BF16)

    return pl.pallas_call(
        body,
        name=name,
        grid=(K // bk, N // bn, nt),
        in_specs=[
            pl.BlockSpec((bt, bk), lambda k, n, t: (t, k)),
            pl.BlockSpec((bt, bn), lambda k, n, t: (t, n)),
        ],
        out_specs=pl.BlockSpec((bk, bn), lambda k, n, t: (k, n)),
        out_shape=jax.ShapeDtypeStruct((K, N), BF16),
        scratch_shapes=[pltpu.VMEM((bk, bn), F32)],
        compiler_params=_cparams(("arbitrary", "arbitrary", "arbitrary")),
    )(a, b)


def _local_step(x, mem, target, p, tm, tq, late=None):
    T, D = x.shape
    w_in = p["w_in"]
    w_qkv = w_in[:, :QKV_W]
    w_f_t = w_in[:, QKV_W:].T
    w_in_pad = jnp.pad(w_in, ((0, 0), (0, IN_PAD - w_in.shape[1])))
    b_f = p["b_forget"].reshape(N_FOX, 1)

    proj, h1, xf, c = _inproj_fwd(x, p["attn_norm_g"], w_qkv, w_f_t, b_f, tm)
    c_col = c.reshape(N_FOX, 1, T)
    c_row = c.reshape(N_FOX, 1, T)
    c_ends = c[:, tq - 1::tq].reshape(N_FOX, 1, T // tq)
    fox_o, lse, gathered = _fox_fwd(proj, c_col, c_row, c_ends, tq, gather=[late[n] for n in _LATE] if late else ())
    if late:
        p = dict(p, **{n: _gathered_full(n, gv) for n, gv in zip(_LATE, gathered)})
    sb_o, sb_ltot, sb_live = _sb_fwd(proj, tq)
    x1, mixed = _post_attn_fwd(fox_o, sb_o, p["fox_out_g"], p["sb_out_g"], p["w_out"], x, tm)
    mb, kv = _mem_kv_fwd(mem, p["mem_norm_g"], p["w_mkv"])
    x2, h2, qb, om = _xattn_fwd(x1, p["xattn_norm_g"], p["w_mq"], kv, p["w_mo"], tm)
    x3, h3, ug, uv, yg, yv, a = _ffn_fwd(
        x2, p["ffn_norm_g"], p["w_up"], p["conv_w"], p["conv_b"], p["w_down"], tm)
    dx3, loss_blk, d_final_g = _loss_head(x3, p["final_norm_g"], target, tm)

    g = {"final_norm_g": d_final_g}
    dx2, du_g, du_v, g["ffn_norm_g"], dc_g, dc_v = _ffn_bwd(
        dx3, x2, p["ffn_norm_g"], ug, uv, yg, yv, p["conv_w"], p["w_down"], p["w_up"], tm)
    g["w_down"] = _matmul_tn(a, dx3, "dw_down", cast_b=True)
    g["w_up"] = jnp.concatenate([_matmul_tn(h3, du_g, "dw_up_gate"), _matmul_tn(h3, du_v, "dw_up_val")], axis=1)
    dconv = jnp.concatenate([dc_g, dc_v], axis=1)
    g["conv_w"] = dconv[0:3]
    g["conv_b"] = dconv[3:4]
    dx1, dq_m, dkv, g["xattn_norm_g"] = _xattn_bwd(dx2, x1, p["xattn_norm_g"], qb, kv, p["w_mo"], p["w_mq"], tm)
    g["w_mo"] = _matmul_tn(om, dx2, "dw_mo", cast_b=True)
    g["w_mq"] = _matmul_tn(h2, dq_m, "dw_mq")
    g["w_mkv"], g["mem_norm_g"] = _mem_kv_bwd(mem, p["mem_norm_g"], mb, dkv, p["w_mkv"])
    d_fox, d_sb, g["fox_out_g"], g["sb_out_g"] = _post_attn_bwd(
        dx1, fox_o, sb_o, p["fox_out_g"], p["sb_out_g"], p["w_out"], tm)
    g["w_out"] = _matmul_tn(mixed, dx1, "dw_out", cast_b=True)
    dq_s, dk_s, dv_s = _sb_bwd(proj, sb_ltot, sb_live, d_sb, tq)
    dq_f, dk_f, dv_f, dck, dcq, parts = _fox_bwd(
        proj, c_col, c_row, c_ends, lse, d_fox, fox_o, tq,
        scatter=[_grad_blocks(n, g[n]) for n in _LATE] if late else ())
    if late:
        g["parts"] = dict(zip(_LATE, parts))
    dxf, db = _forget_bwd(dcq.reshape(N_FOX, T), dck.reshape(N_FOX, T), xf, min(T, 512))
    g["b_forget"] = db.reshape(1, N_FOX)
    dproj = jnp.concatenate([
        dq_f, dk_f.astype(BF16), dv_f.astype(BF16), dq_s, dk_s.astype(BF16), dv_s.astype(BF16),
        jnp.pad(dxf.T, ((0, 0), (0, IN_PAD - QKV_W - N_FOX))).astype(BF16)], axis=1)
    g["w_in"] = _matmul_tn(h1, dproj, "dw_in")[:, :w_in.shape[1]]
    if late:
        grad_x, g["attn_norm_g"], (g["parts"]["w_in"],) = _inproj_bwd(
            dproj, w_in_pad, x, p["attn_norm_g"], dx1, tm, scatter=[_grad_blocks("w_in", g["w_in"])])
    else:
        grad_x, g["attn_norm_g"], _ = _inproj_bwd(dproj, w_in_pad, x, p["attn_norm_g"], dx1, tm)
    return loss_blk, grad_x, g


def _mesh_pos():
    return lax.axis_index("x"), lax.axis_index("y"), lax.axis_index("c")


def _flip(pos, k):
    return tuple(1 - v if (k >> b) & 1 else v for v, b in zip(pos, (2, 1, 0)))


def _slot(pos):
    return 4 * pos[0] + 2 * pos[1] + pos[2]


_CHIPS = (4, 2, 6)


def _comm_sems(n):
    return [pltpu.SemaphoreType.DMA((7 * n,)), pltpu.SemaphoreType.DMA((7 * n,)), pltpu.SemaphoreType.DMA((n,))]


class _Gather:
    def __init__(self, ins, outs, send_sems, recv_sems, local_sems):
        self.ins, self.outs, self.n = ins, outs, len(ins)
        self.send_sems, self.recv_sems, self.local_sems = send_sems, recv_sems, local_sems
        self.me = _mesh_pos()
        self.sibling = _flip(self.me, 1)

    def _copy(self, a, kk, block, to, src=None):
        rows = self.outs[a].at[_slot(block)]
        return pltpu.make_async_remote_copy(
            src_ref=rows if src is None else src, dst_ref=rows,
            send_sem=self.send_sems.at[7 * a + kk], recv_sem=self.recv_sems.at[7 * a + kk],
            device_id=to, device_id_type=MESH)

    def _mine(self):
        return [pltpu.make_async_copy(self.ins[a], self.outs[a].at[_slot(self.me)], self.local_sems.at[a])
                for a in range(self.n)]

    def _first(self):
        out = []
        for a in range(self.n):
            out.append(self._copy(a, 0, self.me, self.sibling, src=self.ins[a]))
            out += [self._copy(a, 1 + j, self.me, _flip(self.me, k), src=self.ins[a]) for j, k in enumerate(_CHIPS)]
        return out

    def _passed(self):
        return [self._copy(a, 4 + j, _flip(self.me, k), self.sibling)
                for j, k in enumerate(_CHIPS) for a in range(self.n)]

    def start(self):
        for cp in self._mine() + self._first():
            cp.start()

    def forward(self):
        for j, k in enumerate(_CHIPS):
            for a in range(self.n):
                self._copy(a, 1 + j, _flip(self.me, k), self.me).wait_recv()
                self._copy(a, 4 + j, _flip(self.me, k), self.sibling).start()

    def finish(self):
        for a in range(self.n):
            self._copy(a, 0, self.sibling, self.me).wait_recv()
            for j, k in enumerate(_CHIPS):
                self._copy(a, 4 + j, _flip(self.sibling, k), self.me).wait_recv()
        for cp in self._first() + self._passed():
            cp.wait_send()
        for cp in self._mine():
            cp.wait()


class _Scatter:
    def __init__(self, ins, outs, send_sems, recv_sems, local_sems):
        self.ins, self.outs, self.n = ins, outs, len(ins)
        self.send_sems, self.recv_sems, self.local_sems = send_sems, recv_sems, local_sems
        self.me = _mesh_pos()

    def _copy(self, a, k, landed=False):
        peer = _flip(self.me, k)
        return pltpu.make_async_remote_copy(
            src_ref=self.ins[a].at[_slot(peer)], dst_ref=self.outs[a].at[_slot(peer if landed else self.me)],
            send_sem=self.send_sems.at[7 * a + k - 1], recv_sem=self.recv_sems.at[7 * a + k - 1],
            device_id=peer, device_id_type=MESH)

    def _mine(self):
        s = _slot(self.me)
        return [pltpu.make_async_copy(self.ins[a].at[s], self.outs[a].at[s], self.local_sems.at[a])
                for a in range(self.n)]

    def start(self):
        for cp in self._mine() + [self._copy(a, k) for k in range(1, 8) for a in range(self.n)]:
            cp.start()

    def finish(self):
        for k in range(1, 8):
            for a in range(self.n):
                self._copy(a, k, landed=True).wait_recv()
        for k in range(1, 8):
            for a in range(self.n):
                self._copy(a, k).wait_send()
        for cp in self._mine():
            cp.wait()


_ANY = pl.BlockSpec(memory_space=pl.ANY)


def _gathered_shapes(shards):
    return [jax.ShapeDtypeStruct((N_DEV,) + s.shape, s.dtype) for s in shards]


def _all_gather(shards, name):
    n = len(shards)

    def body(*refs):
        g = _Gather(refs[:n], refs[n:2 * n], *refs[2 * n:])
        g.start()
        g.forward()
        g.finish()

    return pl.pallas_call(
        body, name=name, in_specs=[_ANY] * n, out_specs=[_ANY] * n,
        out_shape=_gathered_shapes(shards), scratch_shapes=_comm_sems(n),
    )(*shards)


def _adamw_math(w, g, m, v):
    m2 = ADAM_B1 * m + (1.0 - ADAM_B1) * g
    v2 = ADAM_B2 * v + (1.0 - ADAM_B2) * (g * g)
    m_hat = m2 / (1.0 - ADAM_B1 ** ADAM_STEP)
    v_hat = v2 / (1.0 - ADAM_B2 ** ADAM_STEP)
    delta = -ADAM_LR * (m_hat / (jnp.sqrt(v_hat) + ADAM_EPS) + ADAM_WD * w)
    return delta, m2, v2


def _adamw(w, parts, m, v, name):
    R, C = w.shape
    br = 128 if R % 128 == 0 else R

    def body(w_ref, p_ref, m_ref, v_ref, g_ref, d_ref, nm_ref, nv_ref):
        g = p_ref[0].astype(F32)
        for s in range(1, N_DEV):
            g = g + p_ref[s].astype(F32)
        g_ref[...] = g
        d_ref[...], nm_ref[...], nv_ref[...] = _adamw_math(w_ref[...], g, m_ref[...], v_ref[...])

    spec = pl.BlockSpec((br, C), lambda i: (i, 0))
    return pl.pallas_call(
        body,
        name=name,
        grid=(R // br,),
        in_specs=[spec, pl.BlockSpec((N_DEV, br, C), lambda i: (0, i, 0)), spec, spec],
        out_specs=[spec] * 4,
        out_shape=[jax.ShapeDtypeStruct((R, C), F32)] * 4,
        compiler_params=_cparams(("arbitrary",)),
    )(w, parts, m, v)


_SHARDED = ("w_in", "w_out", "w_mq", "w_mkv", "w_mo", "w_up", "conv_w", "w_down")
_LATE = _SHARDED[1:]
_COL_SHARDED = ("w_in", "w_mkv", "w_up", "conv_w")
_REPLICATED = ("attn_norm_g", "b_forget", "fox_out_g", "sb_out_g", "xattn_norm_g", "mem_norm_g",
               "ffn_norm_g", "conv_b", "final_norm_g")
_WEIGHTS = ("attn_norm_g", "w_in", "b_forget", "fox_out_g", "sb_out_g", "w_out", "xattn_norm_g", "mem_norm_g",
            "w_mq", "w_mkv", "w_mo", "ffn_norm_g", "w_up", "conv_w", "conv_b", "w_down", "final_norm_g")


def _pack_rows(n):
    return -(-n // 128)


def _pack(vals, rows_total):
    parts = []
    for v in vals:
        flat = v.reshape(-1)
        parts.append(jnp.pad(flat, (0, _pack_rows(flat.shape[0]) * 128 - flat.shape[0])))
    flat = jnp.concatenate(parts)
    return jnp.pad(flat, (0, rows_total * 128 - flat.shape[0])).reshape(rows_total, 128)


def _unpack(packed, shapes):
    out = []
    r = 0
    for shp in shapes:
        n = 1
        for d in shp:
            n *= d
        out.append(packed[r:r + _pack_rows(n)].reshape(-1)[:n].reshape(shp))
        r += _pack_rows(n)
    return out


def _gathered_full(name, gathered):
    if name in _COL_SHARDED:
        return jnp.transpose(gathered, (1, 0, 2)).reshape(gathered.shape[1], -1)
    return gathered.reshape(-1, gathered.shape[2])


def _to_blocks(name, full):
    if name in _COL_SHARDED:
        r = full.shape[0]
        return jnp.transpose(full.reshape(r, N_DEV, -1), (1, 0, 2))
    return full.reshape(N_DEV, -1, full.shape[1])


def _grad_blocks(name, full):
    blocks = _to_blocks(name, full)
    return blocks if name == "conv_w" else blocks.astype(BF16)


def _step(args, tm, tq):
    w = {n: args[n] for n in _WEIGHTS}
    mom = {n: args["m_" + n] for n in _WEIGHTS}
    var = {n: args["v_" + n] for n in _WEIGHTS}
    x = args["x"][0]
    mem = args["mem"][0]
    target = args["loss_target"][0]

    def flat2(a):
        return a.reshape(a.shape[-2], a.shape[-1]) if a.ndim == 3 else a.reshape(1, -1)

    shards = {n: flat2(w[n]) if n == "conv_w" else flat2(w[n]).astype(BF16) for n in _SHARDED}
    (w_in_all,) = _all_gather([shards["w_in"]], "gather_w_in")
    p = {"w_in": _gathered_full("w_in", w_in_all)}
    for n in _REPLICATED:
        p[n] = flat2(w[n])

    loss_blk, grad_x, g = _local_step(x, mem, target, p, tm, tq, late={n: shards[n] for n in _LATE})

    parts = g["parts"]
    out = {}
    for n in _SHARDED:
        res = _adamw(flat2(w[n]), parts[n], flat2(mom[n]), flat2(var[n]), "adamw_" + n)
        out[n] = [r.reshape(w[n].shape) for r in res]

    shapes = [w[n].shape for n in _REPLICATED]
    rows = sum(_pack_rows(flat2(w[n]).shape[1]) for n in _REPLICATED) + 1
    rows = -(-rows // 8) * 8
    g_pack = _pack([g[n] for n in _REPLICATED] + [loss_blk[0:1, :]], rows)
    (g_all,) = _all_gather([g_pack], "gather_small")
    res = _adamw(_pack([w[n] for n in _REPLICATED], rows), g_all,
                 _pack([mom[n] for n in _REPLICATED], rows), _pack([var[n] for n in _REPLICATED], rows),
                 "adamw_small")
    n_rows_params = sum(_pack_rows(flat2(w[n]).shape[1]) for n in _REPLICATED)
    loss = res[0][n_rows_params, 0]
    unpacked = [_unpack(r, shapes) for r in res]
    for k, n in enumerate(_REPLICATED):
        out[n] = [unpacked[q][k] for q in range(4)]

    grads = [out[n][0] for n in _WEIGHTS]
    deltas = [out[n][1] for n in _WEIGHTS]
    new_m = [out[n][2] for n in _WEIGHTS]
    new_v = [out[n][3] for n in _WEIGHTS]
    return (loss, grad_x[None], *grads, *deltas, *new_m, *new_v)


def kernel(x, mem, attn_norm_g, w_in, b_forget, fox_out_g, sb_out_g, w_out, xattn_norm_g, mem_norm_g, w_mq, w_mkv, w_mo, ffn_norm_g, w_up, conv_w, conv_b, w_down, final_norm_g, loss_target, m_attn_norm_g, m_w_in, m_b_forget, m_fox_out_g, m_sb_out_g, m_w_out, m_xattn_norm_g, m_mem_norm_g, m_w_mq, m_w_mkv, m_w_mo, m_ffn_norm_g, m_w_up, m_conv_w, m_conv_b, m_w_down, m_final_norm_g, v_attn_norm_g, v_w_in, v_b_forget, v_fox_out_g, v_sb_out_g, v_w_out, v_xattn_norm_g, v_mem_norm_g, v_w_mq, v_w_mkv, v_w_mo, v_ffn_norm_g, v_w_up, v_conv_w, v_conv_b, v_w_down, v_final_norm_g):
    args = dict(locals())
    T = x.shape[1]
    return _step(args, tm=min(T, 512), tq=min(T, 256))
```

```python
import functools

import jax
import jax.numpy as jnp
from jax import lax
from jax.experimental import pallas as pl
from jax.experimental.pallas import tpu as pltpu

F32 = jnp.float32
BF16 = jnp.bfloat16
EPS = 1e-6
NEG = -1e30
LOG2E = 1.4426950408889634

HEAD_DIM = 64
N_FOX = 8
FOX_W = 512
QKV_W = 3072
N_MEM_HEADS = 4
MEM_HD = 256
D_FF = 2816
FF_CHUNK = 256
N_DEV = 8

ADAM_LR = 0.001
ADAM_B1 = 0.9
ADAM_B2 = 0.999
ADAM_EPS = 1e-08
ADAM_WD = 0.01
ADAM_STEP = 10

SB_SUM_TERMS = 1

VMEM_LIMIT = 56 * 1024 * 1024
MESH = pl.DeviceIdType.MESH


def _cparams(sem=None):
    return pltpu.CompilerParams(dimension_semantics=sem, vmem_limit_bytes=VMEM_LIMIT)


def _nt(a, b):
    return lax.dot_general(a, b, (((1,), (1,)), ((), ())), preferred_element_type=F32)


def _tn(a, b):
    return lax.dot_general(a, b, (((0,), (0,)), ((), ())), preferred_element_type=F32)


def _nn(a, b):
    return jnp.dot(a, b, preferred_element_type=F32)


def _split_dot(a, m01, terms):
    out = None
    r = a
    for t in range(terms):
        p = r.astype(BF16)
        d = _nn(p, m01)
        out = d if out is None else out + d
        if t + 1 < terms:
            r = r - p.astype(F32)
    return out


def _rstd(xv):
    return lax.rsqrt(jnp.mean(xv * xv, axis=-1, keepdims=True) + EPS)


def _norm_bwd(xv, g, dh):
    r = _rstd(xv)
    xhat = xv * r
    dxhat = dh * g
    dx = r * (dxhat - xhat * jnp.mean(dxhat * xhat, axis=-1, keepdims=True))
    dg = jnp.sum(dh * xhat, axis=0, keepdims=True)
    return dx, dg


def _tile_div(n, cap):
    best = None
    for d in range(128, min(n, cap) + 1, 128):
        if n % d == 0:
            best = d
    assert best is not None, n
    return best


def _inproj_fwd(x, g1, w_qkv, w_f_t, b_f, tm):
    T, D = x.shape
    N = w_qkv.shape[1]
    H = w_f_t.shape[0]

    def body(x_ref, g_ref, w_ref, wf_ref, b_ref, proj_ref, h_ref, xf_ref, c_ref, carry_ref):
        i = pl.program_id(0)

        @pl.when(i == 0)
        def _():
            carry_ref[...] = jnp.zeros_like(carry_ref)

        xv = x_ref[...]
        h = (xv * _rstd(xv) * g_ref[...]).astype(BF16)
        h_ref[...] = h
        for n0 in range(0, N, 512):
            proj_ref[:, n0:n0 + 512] = _nn(h, w_ref[:, n0:n0 + 512]).astype(BF16)
        xf = _nt(wf_ref[...], h) + b_ref[...]
        xf_ref[...] = xf
        logf = jnp.minimum(xf, 0.0) - jnp.log1p(jnp.exp(-jnp.abs(xf)))
        row = lax.broadcasted_iota(jnp.int32, (tm, tm), 0)
        col = lax.broadcasted_iota(jnp.int32, (tm, tm), 1)
        upper = jnp.where(row <= col, 1.0, 0.0).astype(BF16)
        c = _split_dot(logf, upper, 3) + carry_ref[...]
        c_ref[...] = c
        carry_ref[...] = c[:, tm - 1:tm]

    return pl.pallas_call(
        body,
        name="inproj_fwd",
        grid=(T // tm,),
        in_specs=[
            pl.BlockSpec((tm, D), lambda i: (i, 0)),
            pl.BlockSpec((1, D), lambda i: (0, 0)),
            pl.BlockSpec((D, N), lambda i: (0, 0)),
            pl.BlockSpec((H, D), lambda i: (0, 0)),
            pl.BlockSpec((H, 1), lambda i: (0, 0)),
        ],
        out_specs=[
            pl.BlockSpec((tm, N), lambda i: (i, 0)),
            pl.BlockSpec((tm, D), lambda i: (i, 0)),
            pl.BlockSpec((H, tm), lambda i: (0, i)),
            pl.BlockSpec((H, tm), lambda i: (0, i)),
        ],
        out_shape=[
            jax.ShapeDtypeStruct((T, N), BF16),
            jax.ShapeDtypeStruct((T, D), BF16),
            jax.ShapeDtypeStruct((H, T), F32),
            jax.ShapeDtypeStruct((H, T), F32),
        ],
        scratch_shapes=[pltpu.VMEM((H, 1), F32)],
        compiler_params=_cparams(("arbitrary",)),
    )(x, g1, w_qkv, w_f_t, b_f)


def _head_q(q, hh, lane):
    hmask = (lane >= HEAD_DIM * hh) & (lane < HEAD_DIM * (hh + 1))
    qh = jnp.where(hmask, q.astype(F32), 0.0) * (HEAD_DIM ** -0.5)
    return qh.astype(BF16), hmask


def _pipeline3(n, stage_a, stage_b, stage_c, diag_last, alive=None):
    stage_a(0, 0)
    if diag_last:
        @pl.when(n == 1)
        def _():
            stage_b(0, 0, True)

        @pl.when(n >= 2)
        def _():
            stage_b(0, 0, False)
    else:
        stage_b(0, 0, True)

    @pl.when(n >= 2)
    def _():
        stage_a(1, 1)

    def pair(m, carry):
        t = 2 + 2 * m
        stage_c(t - 2)
        stage_b(t - 1, 1, False)
        stage_a(t, 0)
        stage_c(t - 1)
        stage_b(t, 0, False)
        stage_a(t + 1, 1)
        return carry

    pairs = (n - 2) // 2
    if alive is None:
        lax.fori_loop(0, pairs, pair, 0)
        go_on = True
        done = n
    else:
        def more(state):
            return (state[0] < pairs) & state[1]

        def step(state):
            pair(state[0], 0)
            return state[0] + 1, alive()

        m_end, go_on = lax.while_loop(more, step, (jnp.int32(0), jnp.bool_(True)))
        done = jnp.where(go_on, n, 2 * m_end)
    odd = n % 2 == 1

    @pl.when((n >= 3) & odd & go_on)
    def _():
        stage_c(n - 3)
        stage_b(n - 2, 1, False)
        stage_a(n - 1, 0)

    @pl.when((n >= 2) & odd & go_on)
    def _():
        stage_c(n - 2)
        stage_b(n - 1, 0, diag_last)

    @pl.when((n >= 2) & jnp.logical_not(odd) & go_on)
    def _():
        stage_c(n - 2)
        stage_b(n - 1, 1, diag_last)

    if alive is None:
        stage_c(n - 1)
    else:
        @pl.when(go_on)
        def _():
            stage_c(n - 1)

    return done


def _lanes2(x):
    return jnp.concatenate([x, x], axis=1)


def _lanes_to_rows(vec, eye):
    return jnp.sum(jnp.where(eye, jnp.broadcast_to(vec, eye.shape), 0.0), axis=1, keepdims=True)


def _rows_to_lanes(rep, eye):
    return jnp.sum(jnp.where(eye, _lanes2(rep), 0.0), axis=0, keepdims=True)


FOX_DEAD = -110.0


def _fox_key_norms(k_ref, kn_s, lane):
    T = k_ref.shape[0]
    rows = min(T, 512)
    for hh in range(2):
        hmask = (lane >= HEAD_DIM * hh) & (lane < HEAD_DIM * (hh + 1))

        def chunk(n, best, hmask=hmask):
            kf = jnp.where(hmask, k_ref[pl.ds(pl.multiple_of(n * rows, rows), rows), :].astype(F32), 0.0)
            sq = jnp.sum(kf * kf, axis=1, keepdims=True)
            return jnp.maximum(best, jnp.max(sq, axis=0, keepdims=True))

        best = lax.fori_loop(0, T // rows, chunk, jnp.zeros((1, 1), F32))
        kn_s[hh] = jnp.broadcast_to(best, kn_s.shape[1:])


def _fox_live_blocks(i, qh_s, kn_s, cq_ref, cke_ref):
    nq = cke_ref.shape[-1]
    jj = lax.broadcasted_iota(jnp.int32, (1, nq), 1)
    first = None
    for hh in range(2):
        qf = qh_s[hh].astype(F32)
        qn = jnp.max(jnp.sum(qf * qf, axis=1, keepdims=True), axis=0, keepdims=True)
        zb = jnp.sqrt(qn * kn_s[hh][0:1, 0:1]) * 1.001
        bound = (2.0 * zb + cq_ref[hh][:, 0:1]) - cke_ref[hh]
        live = (bound >= FOX_DEAD) & (jj <= i)
        f = jnp.min(jnp.where(live, jj, i).astype(F32), axis=1, keepdims=True)
        first = f if first is None else jnp.minimum(first, f)
    return i + 1 - first[0, 0].astype(jnp.int32)


def _ride_along(exchange, at_start, at_middle, at_end):
    @pl.when(at_start)
    def _():
        exchange.start()

    if at_middle is not None:
        @pl.when(at_middle)
        def _():
            exchange.forward()

    def finish():
        @pl.when(at_end)
        def _():
            exchange.finish()

    return finish


def _fox_fwd(proj, c_col, c_row, c_ends, tq, gather=()):
    T = proj.shape[0]
    assert tq == 256
    nq = T // tq
    ng = len(gather)

    def body(*refs):
        q_ref, k_ref, v_ref, cq_ref, ck_ref, cke_ref = refs[:6]
        o_ref, lse_ref = refs[6 + ng:8 + ng]
        qh_s, cq_s, z_s, p_s, al_s, m_s, acc_s, kn_s = refs[8 + 2 * ng:16 + 2 * ng]
        i = pl.program_id(1)
        if ng:
            pair = pl.program_id(0)
            finish = _ride_along(_Gather(refs[6:6 + ng], refs[8 + ng:8 + 2 * ng], *refs[16 + 2 * ng:]),
                                 (pair == 0) & (i == 0), (pair == 1) & (i == 0), (pair == 3) & (i == nq - 1))
        lane = lax.broadcasted_iota(jnp.int32, (1, 128), 1)
        row = lax.broadcasted_iota(jnp.int32, (tq, tq), 0)
        col = lax.broadcasted_iota(jnp.int32, (tq, tq), 1)

        @pl.when(i == 0)
        def _():
            _fox_key_norms(k_ref, kn_s, lane)

        q = q_ref[...]
        for hh in range(2):
            qh_s[hh] = _head_q(q, hh, lane)[0]
            cq_s[hh] = jnp.broadcast_to(_lanes_to_rows(cq_ref[hh], row == col), (tq, tq))
        m_s[...] = jnp.full(m_s.shape, NEG, F32)
        acc_s[...] = jnp.zeros_like(acc_s)

        def rows(t):
            return pl.ds(pl.multiple_of((i - t) * tq, tq), tq)

        def stage_a(t, slot):
            k = k_ref[rows(t), :]
            for hh in range(2):
                z_s[slot, hh] = _nt(qh_s[hh], k)

        def stage_b(t, slot, diag):
            for hh in range(2):
                s = z_s[slot, hh] + cq_s[hh] - ck_ref[hh, :, rows(t)]
                if diag:
                    s = jnp.where(col <= row, s, NEG)
                m = m_s[hh]
                half = jnp.maximum(s[:, :128], s[:, 128:])
                m_new = jnp.maximum(m, jnp.max(half, axis=1, keepdims=True))
                m_s[hh] = m_new
                al_s[hh] = jnp.exp(m - m_new)
                p_s[hh] = jnp.exp(s - _lanes2(m_new)).astype(BF16)

        def stage_c(t):
            v = v_ref[rows(t), :]
            for hh in range(2):
                own = (lane >= HEAD_DIM * hh) & (lane < HEAD_DIM * (hh + 1))
                acc_s[hh] = al_s[hh] * acc_s[hh] + _nn(p_s[hh], jnp.where(own, v, 1.0).astype(BF16))

        _pipeline3(_fox_live_blocks(i, qh_s, kn_s, cq_ref, cke_ref), stage_a, stage_b, stage_c, False)
        halves = []
        for hh in range(2):
            acc = acc_s[hh]
            own = (lane >= HEAD_DIM * hh) & (lane < HEAD_DIM * (hh + 1))
            halves.append(jnp.where(own, pltpu.roll(acc, HEAD_DIM, axis=1), acc))
        l0, l1 = halves
        o_ref[...] = jnp.where(lane < HEAD_DIM, acc_s[0] / l0, acc_s[1] / l1)
        lse_ref[0] = _rows_to_lanes(m_s[0] + jnp.log(l0), row == col)
        lse_ref[1] = _rows_to_lanes(m_s[1] + jnp.log(l1), row == col)
        if ng:
            finish()

    res = pl.pallas_call(
        body,
        name="fox_fwd",
        grid=(4, nq),
        in_specs=[
            pl.BlockSpec((tq, 128), lambda p, i: (i, p)),
            pl.BlockSpec((T, 128), lambda p, i: (0, 4 + p)),
            pl.BlockSpec((T, 128), lambda p, i: (0, 8 + p)),
            pl.BlockSpec((2, 1, tq), lambda p, i: (p, 0, i)),
            pl.BlockSpec((2, 1, T), lambda p, i: (p, 0, 0)),
            pl.BlockSpec((2, 1, nq), lambda p, i: (p, 0, 0)),
        ] + [_ANY] * ng,
        out_specs=[
            pl.BlockSpec((tq, 128), lambda p, i: (i, p)),
            pl.BlockSpec((2, 1, tq), lambda p, i: (p, 0, i)),
        ] + [_ANY] * ng,
        out_shape=[
            jax.ShapeDtypeStruct((T, FOX_W), F32),
            jax.ShapeDtypeStruct((N_FOX, 1, T), F32),
        ] + _gathered_shapes(gather),
        scratch_shapes=[
            pltpu.VMEM((2, tq, 128), BF16),
            pltpu.VMEM((2, tq, tq), F32),
            pltpu.VMEM((2, 2, tq, tq), F32),
            pltpu.VMEM((2, tq, tq), BF16),
            pltpu.VMEM((2, tq, 128), F32),
            pltpu.VMEM((2, tq, 128), F32),
            pltpu.VMEM((2, tq, 128), F32),
            pltpu.VMEM((2, 8, 128), F32),
        ] + (_comm_sems(ng) if ng else []),
        compiler_params=_cparams(("arbitrary", "arbitrary")),
    )(proj, proj, proj, c_col, c_row, c_ends, *gather)
    res = list(res)
    return res[0], res[1], res[2:]


def _sb_logs(zn, strict):
    e = jnp.exp2(jnp.abs(zn) * (-LOG2E))
    L = jnp.minimum(zn, 0.0) - jnp.log(1.0 + e)
    G = L - zn
    if strict is not None:
        L = jnp.where(strict, L, 0.0)
    return L, G


SB_DEAD = -110.0


def _sb_fwd(proj, tq):
    T = proj.shape[0]
    nq = T // tq

    def body(q_ref, k_ref, v_ref, o_ref, ltot_ref, live_ref, qh_s, z_s, g_s, tot_s, run_s, acc_s):
        i = pl.program_id(1)
        lane = lax.broadcasted_iota(jnp.int32, (1, 128), 1)
        row = lax.broadcasted_iota(jnp.int32, (tq, tq), 0)
        col = lax.broadcasted_iota(jnp.int32, (tq, tq), 1)
        strict = col < row
        later = jnp.where(row > col, 1.0, 0.0).astype(BF16)
        q = q_ref[...]
        for hh in range(2):
            qh_s[hh] = -_head_q(q, hh, lane)[0]
        run_s[...] = jnp.zeros_like(run_s)
        acc_s[...] = jnp.zeros_like(acc_s)

        def rows(t):
            return pl.ds(pl.multiple_of((i - t) * tq, tq), tq)

        def stage_a(t, slot):
            k = k_ref[rows(t), :]
            for hh in range(2):
                z_s[slot, hh] = _nt(qh_s[hh], k)

        def stage_b(t, slot, diag):
            for hh in range(2):
                L, g = _sb_logs(z_s[slot, hh], strict if diag else None)
                if diag:
                    g = jnp.where(strict, g, NEG)
                after = _split_dot(L, later, SB_SUM_TERMS)
                g_s[hh] = g + after
                first = L[:, 0:1]
                if SB_SUM_TERMS == 1:
                    first = first.astype(BF16).astype(F32)
                tot_s[hh] = jnp.broadcast_to(after[:, 0:1] + first, (tq, 128))

        def stage_c(t):
            v = v_ref[rows(t), :]
            for hh in range(2):
                run = run_s[hh]
                a = jnp.exp(g_s[hh] + _lanes2(run))
                acc_s[hh] += _nn(a.astype(BF16), v)
                run_s[hh] = run + tot_s[hh]

        def alive():
            return jnp.max(jnp.maximum(run_s[0], run_s[1])) > SB_DEAD

        done = _pipeline3(i + 1, stage_a, stage_b, stage_c, False, alive)
        ltot_ref[0] = _rows_to_lanes(run_s[0], row == col)
        ltot_ref[1] = _rows_to_lanes(run_s[1], row == col)
        o_ref[...] = jnp.where(lane < HEAD_DIM, acc_s[0], acc_s[1])
        at = lax.broadcasted_iota(jnp.int32, (1, nq), 1)

        @pl.when(i == 0)
        def _():
            live_ref[0] = jnp.zeros((1, nq), F32)

        live_ref[0] = jnp.where(at == i, done.astype(F32), live_ref[0])

    return pl.pallas_call(
        body,
        name="sb_fwd",
        grid=(4, nq),
        in_specs=[
            pl.BlockSpec((tq, 128), lambda p, i: (i, 12 + p)),
            pl.BlockSpec((T, 128), lambda p, i: (0, 16 + p)),
            pl.BlockSpec((T, 128), lambda p, i: (0, 20 + p)),
        ],
        out_specs=[
            pl.BlockSpec((tq, 128), lambda p, i: (i, p)),
            pl.BlockSpec((2, 1, tq), lambda p, i: (p, 0, i)),
            pl.BlockSpec((1, 1, nq), lambda p, i: (p, 0, 0)),
        ],
        out_shape=[
            jax.ShapeDtypeStruct((T, FOX_W), F32),
            jax.ShapeDtypeStruct((N_FOX, 1, T), F32),
            jax.ShapeDtypeStruct((N_FOX // 2, 1, nq), F32),
        ],
        scratch_shapes=[
            pltpu.VMEM((2, tq, 128), BF16),
            pltpu.VMEM((2, 2, tq, tq), F32),
            pltpu.VMEM((2, tq, tq), F32),
            pltpu.VMEM((2, tq, 128), F32),
            pltpu.VMEM((2, tq, 128), F32),
            pltpu.VMEM((2, tq, 128), F32),
        ],
        compiler_params=_cparams(("arbitrary", "arbitrary")),
    )(proj, proj, proj)


def _post_attn_fwd(fox_o, sb_o, gf, gs, w_out, x, tm):
    T, D = x.shape

    def body(f_ref, s_ref, gf_ref, gs_ref, w_ref, x_ref, x1_ref, mix_ref):
        f = f_ref[...]
        s = s_ref[...]
        mix_ref[:, :FOX_W] = (f * _rstd(f) * gf_ref[...]).astype(BF16)
        mix_ref[:, FOX_W:] = (s * _rstd(s) * gs_ref[...]).astype(BF16)
        x1_ref[...] = x_ref[...] + _nn(mix_ref[...], w_ref[...])

    return pl.pallas_call(
        body,
        name="post_attn_fwd",
        grid=(T // tm,),
        in_specs=[
            pl.BlockSpec((tm, FOX_W), lambda i: (i, 0)),
            pl.BlockSpec((tm, FOX_W), lambda i: (i, 0)),
            pl.BlockSpec((1, FOX_W), lambda i: (0, 0)),
            pl.BlockSpec((1, FOX_W), lambda i: (0, 0)),
            pl.BlockSpec((D, D), lambda i: (0, 0)),
            pl.BlockSpec((tm, D), lambda i: (i, 0)),
        ],
        out_specs=[
            pl.BlockSpec((tm, D), lambda i: (i, 0)),
            pl.BlockSpec((tm, D), lambda i: (i, 0)),
        ],
        out_shape=[jax.ShapeDtypeStruct((T, D), F32), jax.ShapeDtypeStruct((T, D), BF16)],
        compiler_params=_cparams(("arbitrary",)),
    )(fox_o, sb_o, gf, gs, w_out, x)


def _mem_kv_fwd(mem, gm, w_mkv):
    M, D = mem.shape
    N = w_mkv.shape[1]

    def body(mem_ref, g_ref, w_ref, m_ref, kv_ref):
        mv = mem_ref[...]
        m = (mv * _rstd(mv) * g_ref[...]).astype(BF16)
        m_ref[...] = m
        for n0 in range(0, N, 512):
            kv_ref[:, n0:n0 + 512] = _nn(m, w_ref[:, n0:n0 + 512]).astype(BF16)

    return pl.pallas_call(
        body,
        name="mem_kv_fwd",
        out_shape=[jax.ShapeDtypeStruct((M, D), BF16), jax.ShapeDtypeStruct((M, N), BF16)],
        compiler_params=_cparams(),
    )(mem, gm, w_mkv)


def _xattn_probs(qb, kv, h):
    k = kv[:, h * MEM_HD:(h + 1) * MEM_HD]
    s = _nt(qb[:, h * MEM_HD:(h + 1) * MEM_HD], k) * (MEM_HD ** -0.5)
    s = s - jnp.max(s, axis=1, keepdims=True)
    p = jnp.exp(s)
    return p / jnp.sum(p, axis=1, keepdims=True)


def _xattn_fwd(x1, g2, w_mq, kv, w_mo, tm):
    T, D = x1.shape
    M = kv.shape[0]

    def body(x_ref, g_ref, wq_ref, kv_ref, wo_ref, x2_ref, h_ref, q_ref, om_ref):
        xv = x_ref[...]
        h = (xv * _rstd(xv) * g_ref[...]).astype(BF16)
        h_ref[...] = h
        q_ref[...] = _nn(h, wq_ref[...]).astype(BF16)
        qb = q_ref[...]
        kvv = kv_ref[...]
        for hd in range(N_MEM_HEADS):
            p = _xattn_probs(qb, kvv, hd)
            v = kvv[:, D + hd * MEM_HD:D + (hd + 1) * MEM_HD]
            om_ref[:, hd * MEM_HD:(hd + 1) * MEM_HD] = _nn(p.astype(BF16), v).astype(BF16)
        x2_ref[...] = xv + _nn(om_ref[...], wo_ref[...])

    return pl.pallas_call(
        body,
        name="xattn_fwd",
        grid=(T // tm,),
        in_specs=[
            pl.BlockSpec((tm, D), lambda i: (i, 0)),
            pl.BlockSpec((1, D), lambda i: (0, 0)),
            pl.BlockSpec((D, D), lambda i: (0, 0)),
            pl.BlockSpec((M, 2 * D), lambda i: (0, 0)),
            pl.BlockSpec((D, D), lambda i: (0, 0)),
        ],
        out_specs=[pl.BlockSpec((tm, D), lambda i: (i, 0))] * 4,
        out_shape=[jax.ShapeDtypeStruct((T, D), F32)] + [jax.ShapeDtypeStruct((T, D), BF16)] * 3,
        compiler_params=_cparams(("arbitrary",)),
    )(x1, g2, w_mq, kv, w_mo)


def _conv_taps(ext_ref, tm, back):
    if back:
        return ext_ref[pl.ds(6, tm), :], ext_ref[pl.ds(7, tm), :], ext_ref[pl.ds(8, tm), :]
    return ext_ref[pl.ds(0, tm), :], ext_ref[pl.ds(1, tm), :], ext_ref[pl.ds(2, tm), :]


def _ffn_fwd(x2, g3, w_up, conv_w, conv_b, w_down, tm):
    T, D = x2.shape
    fc = FF_CHUNK
    nj = D_FF // fc

    def body(x_ref, g_ref, wg_ref, wv_ref, cwg_ref, cwv_ref, cbg_ref, cbv_ref, wd_ref,
             x3_ref, h_ref, ug_ref, uv_ref, yg_ref, yv_ref, a_ref, acc_ref, carry_ref, ext_ref):
        i = pl.program_id(0)
        j = pl.program_id(1)

        @pl.when(j == 0)
        def _():
            xv = x_ref[...]
            h_ref[...] = (xv * _rstd(xv) * g_ref[...]).astype(BF16)
            acc_ref[...] = xv

        @pl.when(i == 0)
        def _():
            carry_ref[j] = jnp.zeros((2, 8, fc), F32)

        h = h_ref[...]
        halves = []
        for part, (w_ref, cw_ref, cb_ref, u_ref, y_ref) in enumerate(
                ((wg_ref, cwg_ref, cbg_ref, ug_ref, yg_ref), (wv_ref, cwv_ref, cbv_ref, uv_ref, yv_ref))):
            u = _nn(h, w_ref[...])
            u_ref[...] = u.astype(BF16)
            ext = ext_ref.at[part]
            ext[pl.ds(0, 8), :] = carry_ref[j, part]
            ext[pl.ds(8, tm), :] = u
            carry_ref[j, part] = u[tm - 8:, :]
            u2, u1, u0 = _conv_taps(ext, tm, True)
            cw = cw_ref[...]
            y = cb_ref[...] + cw[0:1] * u2 + cw[1:2] * u1 + cw[2:3] * u0
            y_ref[...] = y.astype(BF16)
            halves.append(y)
        gate, val = halves
        a = (gate * jax.nn.sigmoid(gate) * val).astype(BF16)
        a_ref[...] = a
        acc_ref[...] += _nn(a, wd_ref[...])

        @pl.when(j == nj - 1)
        def _():
            x3_ref[...] = acc_ref[...]

    return pl.pallas_call(
        body,
        name="ffn_fwd",
        grid=(T // tm, nj),
        in_specs=[
            pl.BlockSpec((tm, D), lambda i, j: (i, 0)),
            pl.BlockSpec((1, D), lambda i, j: (0, 0)),
            pl.BlockSpec((D, fc), lambda i, j: (0, j)),
            pl.BlockSpec((D, fc), lambda i, j: (0, nj + j)),
            pl.BlockSpec((3, fc), lambda i, j: (0, j)),
            pl.BlockSpec((3, fc), lambda i, j: (0, nj + j)),
            pl.BlockSpec((1, fc), lambda i, j: (0, j)),
            pl.BlockSpec((1, fc), lambda i, j: (0, nj + j)),
            pl.BlockSpec((fc, D), lambda i, j: (j, 0)),
        ],
        out_specs=[
            pl.BlockSpec((tm, D), lambda i, j: (i, 0)),
            pl.BlockSpec((tm, D), lambda i, j: (i, 0)),
        ] + [pl.BlockSpec((tm, fc), lambda i, j: (i, j))] * 5,
        out_shape=[
            jax.ShapeDtypeStruct((T, D), F32),
            jax.ShapeDtypeStruct((T, D), BF16),
        ] + [jax.ShapeDtypeStruct((T, D_FF), BF16)] * 5,
        scratch_shapes=[
            pltpu.VMEM((tm, D), F32),
            pltpu.VMEM((nj, 2, 8, fc), F32),
            pltpu.VMEM((2, tm + 8, fc), F32),
        ],
        compiler_params=_cparams(("arbitrary", "arbitrary")),
    )(x2, g3, w_up, w_up, conv_w, conv_w, conv_b, conv_b, w_down)


def _loss_head(x3, gfin, target, tm):
    T, D = x3.shape

    def body(x_ref, g_ref, t_ref, dx_ref, loss_ref, dg_ref):
        i = pl.program_id(0)

        @pl.when(i == 0)
        def _():
            loss_ref[...] = jnp.zeros_like(loss_ref)
            dg_ref[...] = jnp.zeros_like(dg_ref)

        xv = x_ref[...]
        g = g_ref[...]
        r = _rstd(xv)
        xhat = xv * r
        err = xhat * g - t_ref[...]
        part = jnp.sum(jnp.sum(err * err, axis=1, keepdims=True), axis=0, keepdims=True) * (0.5 / D)
        loss_ref[...] += jnp.broadcast_to(part, loss_ref.shape)
        dy = err * (1.0 / D)
        dg_ref[...] += jnp.sum(dy * xhat, axis=0, keepdims=True)
        dxhat = dy * g
        dx_ref[...] = r * (dxhat - xhat * jnp.mean(dxhat * xhat, axis=-1, keepdims=True))

    return pl.pallas_call(
        body,
        name="loss_head",
        grid=(T // tm,),
        in_specs=[
            pl.BlockSpec((tm, D), lambda i: (i, 0)),
            pl.BlockSpec((1, D), lambda i: (0, 0)),
            pl.BlockSpec((tm, D), lambda i: (i, 0)),
        ],
        out_specs=[
            pl.BlockSpec((tm, D), lambda i: (i, 0)),
            pl.BlockSpec((8, 128), lambda i: (0, 0)),
            pl.BlockSpec((1, D), lambda i: (0, 0)),
        ],
        out_shape=[
            jax.ShapeDtypeStruct((T, D), F32),
            jax.ShapeDtypeStruct((8, 128), F32),
            jax.ShapeDtypeStruct((1, D), F32),
        ],
        compiler_params=_cparams(("arbitrary",)),
    )(x3, gfin, target)


def _ffn_bwd(dx3, x2, g3, ug, uv, yg, yv, conv_w, w_down, w_up, tm):
    T, D = x2.shape
    fc = FF_CHUNK
    nj = D_FF // fc
    nt = T // tm

    def rev(i):
        return nt - 1 - i

    def body(dx3_ref, x_ref, g_ref, ug_ref, uv_ref, yg_ref, yv_ref, cwg_ref, cwv_ref,
             wd_ref, wug_ref, wuv_ref,
             dx2_ref, dug_ref, duv_ref, dg_ref, dcg_ref, dcv_ref,
             acc_ref, carry_ref, ext_ref):
        i = pl.program_id(0)
        j = pl.program_id(1)
        cols = pl.ds(pl.multiple_of(j * fc, fc), fc)

        @pl.when(j == 0)
        def _():
            acc_ref[...] = jnp.zeros_like(acc_ref)

        @pl.when((i == 0) & (j == 0))
        def _():
            dg_ref[...] = jnp.zeros_like(dg_ref)
            dcg_ref[...] = jnp.zeros_like(dcg_ref)
            dcv_ref[...] = jnp.zeros_like(dcv_ref)

        @pl.when(i == 0)
        def _():
            carry_ref[j] = jnp.zeros((2, 8, fc), F32)

        da = _nt(dx3_ref[...].astype(BF16), wd_ref[...])
        gate = yg_ref[...].astype(F32)
        val = yv_ref[...].astype(F32)
        sig = jax.nn.sigmoid(gate)
        silu = gate * sig
        dys = (da * val * (sig * (1.0 + gate * (1.0 - sig))), da * silu)
        for part, (dy, u_ref, cw_ref, du_ref, wu_ref, dc_ref) in enumerate(
                ((dys[0], ug_ref, cwg_ref, dug_ref, wug_ref, dcg_ref),
                 (dys[1], uv_ref, cwv_ref, duv_ref, wuv_ref, dcv_ref))):
            ext = ext_ref.at[part]
            ext[pl.ds(0, tm), :] = dy
            ext[pl.ds(tm, 8), :] = carry_ref[j, part]
            carry_ref[j, part] = dy[:8, :]
            d0, d1, d2 = _conv_taps(ext, tm, False)
            u = u_ref[...].astype(F32)
            upd = jnp.concatenate([
                jnp.sum(u * d2, axis=0, keepdims=True),
                jnp.sum(u * d1, axis=0, keepdims=True),
                jnp.sum(u * d0, axis=0, keepdims=True),
                jnp.sum(d0, axis=0, keepdims=True),
                jnp.zeros((4, fc), F32)], axis=0)
            dc_ref[:, cols] += upd
            cw = cw_ref[...]
            du = (cw[2:3] * d0 + cw[1:2] * d1 + cw[0:1] * d2).astype(BF16)
            du_ref[...] = du
            acc_ref[...] += _nt(du, wu_ref[...])

        @pl.when(j == nj - 1)
        def _():
            dx, dg = _norm_bwd(x_ref[...], g_ref[...], acc_ref[...])
            dx2_ref[...] = dx3_ref[...] + dx
            dg_ref[...] += dg

    return pl.pallas_call(
        body,
        name="ffn_bwd",
        grid=(nt, nj),
        in_specs=[
            pl.BlockSpec((tm, D), lambda i, j: (rev(i), 0)),
            pl.BlockSpec((tm, D), lambda i, j: (rev(i), 0)),
            pl.BlockSpec((1, D), lambda i, j: (0, 0)),
            pl.BlockSpec((tm, fc), lambda i, j: (rev(i), j)),
            pl.BlockSpec((tm, fc), lambda i, j: (rev(i), j)),
            pl.BlockSpec((tm, fc), lambda i, j: (rev(i), j)),
            pl.BlockSpec((tm, fc), lambda i, j: (rev(i), j)),
            pl.BlockSpec((3, fc), lambda i, j: (0, j)),
            pl.BlockSpec((3, fc), lambda i, j: (0, nj + j)),
            pl.BlockSpec((fc, D), lambda i, j: (j, 0)),
            pl.BlockSpec((D, fc), lambda i, j: (0, j)),
            pl.BlockSpec((D, fc), lambda i, j: (0, nj + j)),
        ],
        out_specs=[
            pl.BlockSpec((tm, D), lambda i, j: (rev(i), 0)),
            pl.BlockSpec((tm, fc), lambda i, j: (rev(i), j)),
            pl.BlockSpec((tm, fc), lambda i, j: (rev(i), j)),
            pl.BlockSpec((1, D), lambda i, j: (0, 0)),
            pl.BlockSpec((8, D_FF), lambda i, j: (0, 0)),
            pl.BlockSpec((8, D_FF), lambda i, j: (0, 0)),
        ],
        out_shape=[
            jax.ShapeDtypeStruct((T, D), F32),
            jax.ShapeDtypeStruct((T, D_FF), BF16),
            jax.ShapeDtypeStruct((T, D_FF), BF16),
            jax.ShapeDtypeStruct((1, D), F32),
            jax.ShapeDtypeStruct((8, D_FF), F32),
            jax.ShapeDtypeStruct((8, D_FF), F32),
        ],
        scratch_shapes=[
            pltpu.VMEM((tm, D), F32),
            pltpu.VMEM((nj, 2, 8, fc), F32),
            pltpu.VMEM((2, tm + 8, fc), F32),
        ],
        compiler_params=_cparams(("arbitrary", "arbitrary")),
    )(dx3, x2, g3, ug, uv, yg, yv, conv_w, conv_w, w_down, w_up, w_up)


def _xattn_bwd(dx2, x1, g2, qb, kv, w_mo, w_mq, tm):
    T, D = x1.shape
    M = kv.shape[0]

    def body(dx2_ref, x_ref, g_ref, q_ref, kv_ref, wo_ref, wq_ref, dx1_ref, dq_ref, dkv_ref, dg_ref):
        i = pl.program_id(0)

        @pl.when(i == 0)
        def _():
            dkv_ref[...] = jnp.zeros_like(dkv_ref)
            dg_ref[...] = jnp.zeros_like(dg_ref)

        dxv = dx2_ref[...]
        dom = _nt(dxv.astype(BF16), wo_ref[...]).astype(BF16)
        qb_ = q_ref[...]
        kvv = kv_ref[...]
        for hd in range(N_MEM_HEADS):
            sl = slice(hd * MEM_HD, (hd + 1) * MEM_HD)
            vsl = slice(D + hd * MEM_HD, D + (hd + 1) * MEM_HD)
            p = _xattn_probs(qb_, kvv, hd)
            dp = _nt(dom[:, sl], kvv[:, vsl])
            ds = (p * (dp - jnp.sum(p * dp, axis=1, keepdims=True)) * (MEM_HD ** -0.5)).astype(BF16)
            dq_ref[:, sl] = _nn(ds, kvv[:, sl]).astype(BF16)
            dkv_ref[:, sl] += _tn(ds, qb_[:, sl])
            dkv_ref[:, vsl] += _tn(p.astype(BF16), dom[:, sl])
        dh = _nt(dq_ref[...], wq_ref[...])
        dx, dg = _norm_bwd(x_ref[...], g_ref[...], dh)
        dx1_ref[...] = dxv + dx
        dg_ref[...] += dg

    return pl.pallas_call(
        body,
        name="xattn_bwd",
        grid=(T // tm,),
        in_specs=[
            pl.BlockSpec((tm, D), lambda i: (i, 0)),
            pl.BlockSpec((tm, D), lambda i: (i, 0)),
            pl.BlockSpec((1, D), lambda i: (0, 0)),
            pl.BlockSpec((tm, D), lambda i: (i, 0)),
            pl.BlockSpec((M, 2 * D), lambda i: (0, 0)),
            pl.BlockSpec((D, D), lambda i: (0, 0)),
            pl.BlockSpec((D, D), lambda i: (0, 0)),
        ],
        out_specs=[
            pl.BlockSpec((tm, D), lambda i: (i, 0)),
            pl.BlockSpec((tm, D), lambda i: (i, 0)),
            pl.BlockSpec((M, 2 * D), lambda i: (0, 0)),
            pl.BlockSpec((1, D), lambda i: (0, 0)),
        ],
        out_shape=[
            jax.ShapeDtypeStruct((T, D), F32),
            jax.ShapeDtypeStruct((T, D), BF16),
            jax.ShapeDtypeStruct((M, 2 * D), F32),
            jax.ShapeDtypeStruct((1, D), F32),
        ],
        compiler_params=_cparams(("arbitrary",)),
    )(dx2, x1, g2, qb, kv, w_mo, w_mq)


def _mem_kv_bwd(mem, gm, mb, dkv, w_mkv):
    M, D = mem.shape
    N = dkv.shape[1]

    def body(mem_ref, g_ref, m_ref, dkv_ref, w_ref, dw_ref, dg_ref):
        dkvb = dkv_ref[...].astype(BF16)
        for n0 in range(0, N, 512):
            dw_ref[:, n0:n0 + 512] = _tn(m_ref[...], dkvb[:, n0:n0 + 512]).astype(BF16)
        dm = _nt(dkvb, w_ref[...])
        mv = mem_ref[...]
        dg_ref[...] = jnp.sum(dm * (mv * _rstd(mv)), axis=0, keepdims=True)

    return pl.pallas_call(
        body,
        name="mem_kv_bwd",
        out_shape=[jax.ShapeDtypeStruct((D, N), BF16), jax.ShapeDtypeStruct((1, D), F32)],
        compiler_params=_cparams(),
    )(mem, gm, mb, dkv, w_mkv)


def _post_attn_bwd(dx1, fox_o, sb_o, gf, gs, w_out, tm):
    T, D = dx1.shape

    def body(dx_ref, f_ref, s_ref, gf_ref, gs_ref, w_ref, df_ref, ds_ref, dgf_ref, dgs_ref):
        i = pl.program_id(0)

        @pl.when(i == 0)
        def _():
            dgf_ref[...] = jnp.zeros_like(dgf_ref)
            dgs_ref[...] = jnp.zeros_like(dgs_ref)

        dmix = _nt(dx_ref[...].astype(BF16), w_ref[...])
        d, dg = _norm_bwd(f_ref[...], gf_ref[...], dmix[:, :FOX_W])
        df_ref[...] = d
        dgf_ref[...] += dg
        d, dg = _norm_bwd(s_ref[...], gs_ref[...], dmix[:, FOX_W:])
        ds_ref[...] = d
        dgs_ref[...] += dg

    return pl.pallas_call(
        body,
        name="post_attn_bwd",
        grid=(T // tm,),
        in_specs=[
            pl.BlockSpec((tm, D), lambda i: (i, 0)),
            pl.BlockSpec((tm, FOX_W), lambda i: (i, 0)),
            pl.BlockSpec((tm, FOX_W), lambda i: (i, 0)),
            pl.BlockSpec((1, FOX_W), lambda i: (0, 0)),
            pl.BlockSpec((1, FOX_W), lambda i: (0, 0)),
            pl.BlockSpec((D, D), lambda i: (0, 0)),
        ],
        out_specs=[
            pl.BlockSpec((tm, FOX_W), lambda i: (i, 0)),
            pl.BlockSpec((tm, FOX_W), lambda i: (i, 0)),
            pl.BlockSpec((1, FOX_W), lambda i: (0, 0)),
            pl.BlockSpec((1, FOX_W), lambda i: (0, 0)),
        ],
        out_shape=[
            jax.ShapeDtypeStruct((T, FOX_W), F32),
            jax.ShapeDtypeStruct((T, FOX_W), F32),
            jax.ShapeDtypeStruct((1, FOX_W), F32),
            jax.ShapeDtypeStruct((1, FOX_W), F32),
        ],
        compiler_params=_cparams(("arbitrary",)),
    )(dx1, fox_o, sb_o, gf, gs, w_out)


def _sb_bwd(proj, ltot, live, d_o, tq):
    T = proj.shape[0]
    nq = T // tq

    def body(q_ref, k_ref, v_ref, lt_ref, live_ref, do_ref, dq_ref, dk_ref, dv_ref,
             qh_s, doh_s, lt_s, z_s, da_s, ab_s, dzb_s, run_s, runw_s, dq_s):
        i = pl.program_id(1)

        @pl.when(i == 0)
        def _():
            dk_ref[...] = jnp.zeros_like(dk_ref)
            dv_ref[...] = jnp.zeros_like(dv_ref)

        lane = lax.broadcasted_iota(jnp.int32, (1, 128), 1)
        row = lax.broadcasted_iota(jnp.int32, (tq, tq), 0)
        col = lax.broadcasted_iota(jnp.int32, (tq, tq), 1)
        strict = col < row
        upto = jnp.where(row <= col, 1.0, 0.0).astype(BF16)
        before = jnp.where(row < col, 1.0, 0.0).astype(BF16)
        q = q_ref[...]
        dov = do_ref[...]
        for hh in range(2):
            qh, hmask = _head_q(q, hh, lane)
            qh_s[hh] = -qh
            doh_s[hh] = jnp.where(hmask, dov, 0.0).astype(BF16)
            lt_s[hh] = jnp.broadcast_to(_lanes_to_rows(lt_ref[hh], row == col), (tq, 128))
        run_s[...] = jnp.zeros_like(run_s)
        runw_s[...] = jnp.zeros_like(runw_s)
        dq_s[...] = jnp.zeros_like(dq_s)

        at = lax.broadcasted_iota(jnp.int32, (1, nq), 1)
        count = jnp.sum(jnp.where(at == i, live_ref[0], 0.0), axis=1, keepdims=True)[0, 0].astype(jnp.int32)
        n_live = jnp.clip(count, 1, i + 1)
        oldest = i + 1 - n_live

        def rows(t):
            return pl.ds(pl.multiple_of((oldest + t) * tq, tq), tq)

        def stage_a(t, slot):
            k = k_ref[rows(t), :]
            v = v_ref[rows(t), :]
            for hh in range(2):
                z_s[slot, hh] = _nt(qh_s[hh], k)
                da_s[slot, hh] = _nt(doh_s[hh], v)

        def stage_b(t, slot, diag):
            for hh in range(2):
                L, g = _sb_logs(z_s[slot, hh], strict if diag else None)
                upto_s = _split_dot(L, upto, SB_SUM_TERMS)
                run = run_s[hh]
                arg = (g + _lanes2(lt_s[hh] - run)) - upto_s
                if diag:
                    arg = jnp.where(strict, arg, NEG)
                a = jnp.exp(arg)
                w = a * da_s[slot, hh]
                w_before = _split_dot(w, before, SB_SUM_TERMS)
                run_w = runw_s[hh]
                d_keep = w_before + _lanes2(run_w)
                beta = jnp.exp(g)
                ndz = beta * (w + d_keep) - w
                if diag:
                    ndz = jnp.where(strict, ndz, 0.0)
                dzb_s[hh] = ndz.astype(BF16)
                ab_s[hh] = a.astype(BF16)
                run_s[hh] = run + jnp.broadcast_to(upto_s[:, tq - 1:tq], (tq, 128))
                runw_s[hh] = run_w + jnp.broadcast_to(w_before[:, tq - 1:tq] + w[:, tq - 1:tq], (tq, 128))

        def stage_c(t):
            k = k_ref[rows(t), :]
            dk_blk = None
            dv_blk = None
            for hh in range(2):
                dzb = dzb_s[hh]
                dq_s[hh] += _nn(dzb, k)
                dk_h = _tn(dzb, qh_s[hh])
                dv_h = _tn(ab_s[hh], doh_s[hh])
                dk_blk = dk_h if dk_blk is None else dk_blk + dk_h
                dv_blk = dv_h if dv_blk is None else dv_blk + dv_h
            dk_ref[rows(t), :] += dk_blk
            dv_ref[rows(t), :] += dv_blk

        _pipeline3(n_live, stage_a, stage_b, stage_c, True)
        dq_ref[...] = (jnp.where(lane < HEAD_DIM, dq_s[0], dq_s[1]) * -(HEAD_DIM ** -0.5)).astype(BF16)

    return pl.pallas_call(
        body,
        name="sb_bwd",
        grid=(4, nq),
        in_specs=[
            pl.BlockSpec((tq, 128), lambda p, i: (i, 12 + p)),
            pl.BlockSpec((T, 128), lambda p, i: (0, 16 + p)),
            pl.BlockSpec((T, 128), lambda p, i: (0, 20 + p)),
            pl.BlockSpec((2, 1, tq), lambda p, i: (p, 0, i)),
            pl.BlockSpec((1, 1, nq), lambda p, i: (p, 0, 0)),
            pl.BlockSpec((tq, 128), lambda p, i: (i, p)),
        ],
        out_specs=[
            pl.BlockSpec((tq, 128), lambda p, i: (i, p)),
            pl.BlockSpec((T, 128), lambda p, i: (0, p)),
            pl.BlockSpec((T, 128), lambda p, i: (0, p)),
        ],
        out_shape=[
            jax.ShapeDtypeStruct((T, FOX_W), BF16),
            jax.ShapeDtypeStruct((T, FOX_W), F32),
            jax.ShapeDtypeStruct((T, FOX_W), F32),
        ],
        scratch_shapes=[
            pltpu.VMEM((2, tq, 128), BF16),
            pltpu.VMEM((2, tq, 128), BF16),
            pltpu.VMEM((2, tq, 128), F32),
            pltpu.VMEM((2, 2, tq, tq), F32),
            pltpu.VMEM((2, 2, tq, tq), F32),
            pltpu.VMEM((2, tq, tq), BF16),
            pltpu.VMEM((2, tq, tq), BF16),
            pltpu.VMEM((2, tq, 128), F32),
            pltpu.VMEM((2, tq, 128), F32),
            pltpu.VMEM((2, tq, 128), F32),
        ],
        compiler_params=_cparams(("arbitrary", "arbitrary")),
    )(proj, proj, proj, ltot, live, d_o)


def _fox_bwd(proj, c_col, c_row, c_ends, lse, d_o, o, tq, scatter=()):
    T = proj.shape[0]
    nq = T // tq
    ns = len(scatter)

    def body(*refs):
        q_ref, k_ref, v_ref, cq_ref, ck_ref, cke_ref, lse_ref, do_ref, o_ref = refs[:9]
        dq_ref, dk_ref, dv_ref, dck_ref, dcq_ref = refs[9 + ns:14 + ns]
        (qh_s, doh_s, delta_s, shift_s, z_s, dp_s, pb_s, dsb_s, rs_s, dq_s,
         kn_s) = refs[14 + 2 * ns:25 + 2 * ns]
        i = pl.program_id(1)
        if ns:
            pair = pl.program_id(0)
            finish = _ride_along(_Scatter(refs[9:9 + ns], refs[14 + ns:14 + 2 * ns], *refs[25 + 2 * ns:]),
                                 (pair == 0) & (i == 0), None, (pair == 3) & (i == nq - 1))
        lane = lax.broadcasted_iota(jnp.int32, (1, 128), 1)

        @pl.when(i == 0)
        def _():
            dk_ref[...] = jnp.zeros_like(dk_ref)
            dv_ref[...] = jnp.zeros_like(dv_ref)
            dck_ref[...] = jnp.zeros_like(dck_ref)
            _fox_key_norms(k_ref, kn_s, lane)

        row = lax.broadcasted_iota(jnp.int32, (tq, tq), 0)
        col = lax.broadcasted_iota(jnp.int32, (tq, tq), 1)
        q = q_ref[...]
        dov = do_ref[...]
        ov = o_ref[...]
        for hh in range(2):
            qh, hmask = _head_q(q, hh, lane)
            dohb = jnp.where(hmask, dov, 0.0).astype(BF16)
            qh_s[hh] = qh
            doh_s[hh] = dohb
            delta_s[hh] = jnp.broadcast_to(jnp.sum(dohb.astype(F32) * ov, axis=1, keepdims=True), (tq, tq))
            shift_s[hh] = jnp.broadcast_to(_lanes_to_rows(cq_ref[hh] - lse_ref[hh], row == col), (tq, tq))
        rs_s[...] = jnp.zeros_like(rs_s)
        dq_s[...] = jnp.zeros_like(dq_s)

        def rows(t):
            return pl.ds(pl.multiple_of((i - t) * tq, tq), tq)

        def stage_a(t, slot):
            k = k_ref[rows(t), :]
            v = v_ref[rows(t), :]
            for hh in range(2):
                z_s[slot, hh] = _nt(qh_s[hh], k)
                dp_s[slot, hh] = _nt(doh_s[hh], v)

        def stage_b(t, slot, diag):
            for hh in range(2):
                s = z_s[slot, hh] + shift_s[hh] - ck_ref[hh, :, rows(t)]
                if diag:
                    s = jnp.where(col <= row, s, NEG)
                p = jnp.exp(s)
                ds = p * (dp_s[slot, hh] - delta_s[hh])
                pb_s[hh] = p.astype(BF16)
                dsb_s[hh] = ds.astype(BF16)
                dck_ref[hh, :, rows(t)] += jnp.sum(ds, axis=0, keepdims=True)
                rs_s[hh] += jnp.sum(ds, axis=1, keepdims=True)

        def stage_c(t):
            k = k_ref[rows(t), :]
            dk_blk = None
            dv_blk = None
            for hh in range(2):
                dsb = dsb_s[hh]
                dq_s[hh] += _nn(dsb, k)
                dk_h = _tn(dsb, qh_s[hh])
                dv_h = _tn(pb_s[hh], doh_s[hh])
                dk_blk = dk_h if dk_blk is None else dk_blk + dk_h
                dv_blk = dv_h if dv_blk is None else dv_blk + dv_h
            dk_ref[rows(t), :] += dk_blk
            dv_ref[rows(t), :] += dv_blk

        _pipeline3(_fox_live_blocks(i, qh_s, kn_s, cq_ref, cke_ref), stage_a, stage_b, stage_c, False)
        dcq_ref[0] = _rows_to_lanes(rs_s[0], row == col)
        dcq_ref[1] = _rows_to_lanes(rs_s[1], row == col)
        dq_ref[...] = (jnp.where(lane < HEAD_DIM, dq_s[0], dq_s[1]) * (HEAD_DIM ** -0.5)).astype(BF16)
        if ns:
            finish()

    res = pl.pallas_call(
        body,
        name="fox_bwd",
        grid=(4, nq),
        in_specs=[
            pl.BlockSpec((tq, 128), lambda p, i: (i, p)),
            pl.BlockSpec((T, 128), lambda p, i: (0, 4 + p)),
            pl.BlockSpec((T, 128), lambda p, i: (0, 8 + p)),
            pl.BlockSpec((2, 1, tq), lambda p, i: (p, 0, i)),
            pl.BlockSpec((2, 1, T), lambda p, i: (p, 0, 0)),
            pl.BlockSpec((2, 1, nq), lambda p, i: (p, 0, 0)),
            pl.BlockSpec((2, 1, tq), lambda p, i: (p, 0, i)),
            pl.BlockSpec((tq, 128), lambda p, i: (i, p)),
            pl.BlockSpec((tq, 128), lambda p, i: (i, p)),
        ] + [_ANY] * ns,
        out_specs=[
            pl.BlockSpec((tq, 128), lambda p, i: (i, p)),
            pl.BlockSpec((T, 128), lambda p, i: (0, p)),
            pl.BlockSpec((T, 128), lambda p, i: (0, p)),
            pl.BlockSpec((2, 1, T), lambda p, i: (p, 0, 0)),
            pl.BlockSpec((2, 1, tq), lambda p, i: (p, 0, i)),
        ] + [_ANY] * ns,
        out_shape=[
            jax.ShapeDtypeStruct((T, FOX_W), BF16),
            jax.ShapeDtypeStruct((T, FOX_W), F32),
            jax.ShapeDtypeStruct((T, FOX_W), F32),
            jax.ShapeDtypeStruct((N_FOX, 1, T), F32),
            jax.ShapeDtypeStruct((N_FOX, 1, T), F32),
        ] + [jax.ShapeDtypeStruct(b.shape, b.dtype) for b in scatter],
        scratch_shapes=[
            pltpu.VMEM((2, tq, 128), BF16),
            pltpu.VMEM((2, tq, 128), BF16),
            pltpu.VMEM((2, tq, tq), F32),
            pltpu.VMEM((2, tq, tq), F32),
            pltpu.VMEM((2, 2, tq, tq), F32),
            pltpu.VMEM((2, 2, tq, tq), F32),
            pltpu.VMEM((2, tq, tq), BF16),
            pltpu.VMEM((2, tq, tq), BF16),
            pltpu.VMEM((2, tq, 128), F32),
            pltpu.VMEM((2, tq, 128), F32),
            pltpu.VMEM((2, 8, 128), F32),
        ] + (_comm_sems(ns) if ns else []),
        compiler_params=_cparams(("arbitrary", "arbitrary")),
    )(proj, proj, proj, c_col, c_row, c_ends, lse, d_o, o, *scatter)
    res = list(res)
    return (*res[:5], res[5:])


def _forget_bwd(dcq, dck, xf, h1, tc):
    H, T = xf.shape
    D = h1.shape[1]
    nc = T // tc

    def body(dcq_ref, dck_ref, xf_ref, h_ref, dxf_ref, db_ref, dwf_ref):
        row = lax.broadcasted_iota(jnp.int32, (tc, tc), 0)
        col = lax.broadcasted_iota(jnp.int32, (tc, tc), 1)
        from_here = jnp.where(row >= col, 1.0, 0.0).astype(BF16)

        def chunk(n, carry):
            run, db, dwf = carry
            cs = pl.multiple_of((nc - 1 - n) * tc, tc)
            dc = dcq_ref[:, pl.ds(cs, tc)] - dck_ref[:, pl.ds(cs, tc)]
            dlogf = _split_dot(dc, from_here, 3) + run
            xfv = xf_ref[:, pl.ds(cs, tc)]
            dxf = dlogf * jax.nn.sigmoid(-xfv)
            dxf_ref[:, pl.ds(cs, tc)] = dxf
            dwf = dwf + _nn(dxf.astype(BF16), h_ref[pl.ds(cs, tc), :])
            return dlogf[:, 0:1], db + jnp.sum(dxf, axis=1, keepdims=True), dwf

        zero = jnp.zeros((H, 1), F32)
        _, db, dwf = lax.fori_loop(0, nc, chunk, (zero, zero, jnp.zeros((H, D), F32)))
        db_ref[...] = db
        dwf_ref[...] = dwf

    return pl.pallas_call(
        body,
        name="forget_bwd",
        out_shape=[jax.ShapeDtypeStruct((H, T), F32), jax.ShapeDtypeStruct((H, 1), F32),
                   jax.ShapeDtypeStruct((H, D), F32)],
        compiler_params=_cparams(),
    )(dcq, dck, xf, h1)


def _inproj_bwd(pieces, dxf_t, w_in, w_f_t, x, g1, dx1, tm, scatter=()):
    T, D = x.shape
    N = w_in.shape[1]
    ns = len(scatter)
    nt = T // tm
    npc = len(pieces)

    def body(*refs):
        pc_refs = refs[:npc]
        dxf_ref, w_ref, wf_ref, x_ref, g_ref, dx1_ref = refs[npc:npc + 6]
        base = npc + 6
        dx_ref, dg_ref = refs[base + ns:base + 2 + ns]
        i = pl.program_id(0)
        if ns:
            exchange = _Scatter(refs[base:base + ns], refs[base + 2 + ns:base + 2 + 2 * ns],
                                *refs[base + 2 + 2 * ns:])

            @pl.when(i == 0)
            def _():
                exchange.start()

        @pl.when(i == 0)
        def _():
            dg_ref[...] = jnp.zeros_like(dg_ref)

        dh = _nn(dxf_ref[...], wf_ref[...].astype(F32))
        for k, pc_ref in enumerate(pc_refs):
            dh = dh + _nt(pc_ref[...].astype(BF16), w_ref[:, k * FOX_W:(k + 1) * FOX_W])
        dx, dg = _norm_bwd(x_ref[...], g_ref[...], dh)
        dx_ref[...] = dx1_ref[...] + dx
        dg_ref[...] += dg
        if ns:
            @pl.when(i == nt - 1)
            def _():
                exchange.finish()

    res = pl.pallas_call(
        body,
        name="inproj_bwd",
        grid=(nt,),
        in_specs=[pl.BlockSpec((tm, FOX_W), lambda i: (i, 0))] * npc + [
            pl.BlockSpec((tm, N_FOX), lambda i: (i, 0)),
            pl.BlockSpec((D, N), lambda i: (0, 0)),
            pl.BlockSpec((N_FOX, D), lambda i: (0, 0)),
            pl.BlockSpec((tm, D), lambda i: (i, 0)),
            pl.BlockSpec((1, D), lambda i: (0, 0)),
            pl.BlockSpec((tm, D), lambda i: (i, 0)),
        ] + [_ANY] * ns,
        out_specs=[
            pl.BlockSpec((tm, D), lambda i: (i, 0)),
            pl.BlockSpec((1, D), lambda i: (0, 0)),
        ] + [_ANY] * ns,
        out_shape=[jax.ShapeDtypeStruct((T, D), F32), jax.ShapeDtypeStruct((1, D), F32)]
        + [jax.ShapeDtypeStruct(b.shape, b.dtype) for b in scatter],
        scratch_shapes=_comm_sems(ns) if ns else [],
        compiler_params=_cparams(("arbitrary",)),
    )(*pieces, dxf_t, w_in, w_f_t, x, g1, dx1, *scatter)
    res = list(res)
    return res[0], res[1], res[2:]


def _dw_in(h1, pieces, name):
    T, K = h1.shape
    bt = min(T, 512)
    nt = T // bt
    npc = len(pieces)

    def body(*refs):
        a_ref = refs[0]
        pc_refs = refs[1:1 + npc]
        o_ref, acc_ref = refs[1 + npc:]
        n = pl.program_id(0)
        t = pl.program_id(1)

        @pl.when(t == 0)
        def _():
            acc_ref[...] = jnp.zeros_like(acc_ref)

        for k, pc_ref in enumerate(pc_refs):
            @pl.when(n == k)
            def _(pc_ref=pc_ref):
                acc_ref[...] += _tn(a_ref[...], pc_ref[...].astype(BF16))

        @pl.when(t == nt - 1)
        def _():
            o_ref[...] = acc_ref[...].astype(BF16)

    def piece_spec(k):
        return pl.BlockSpec((bt, FOX_W), lambda n, t: (jnp.where(n == k, t, 0), 0))

    return pl.pallas_call(
        body,
        name=name,
        grid=(npc, nt),
        in_specs=[pl.BlockSpec((bt, K), lambda n, t: (t, 0))] + [piece_spec(k) for k in range(npc)],
        out_specs=pl.BlockSpec((K, FOX_W), lambda n, t: (0, n)),
        out_shape=jax.ShapeDtypeStruct((K, npc * FOX_W), BF16),
        scratch_shapes=[pltpu.VMEM((K, FOX_W), F32)],
        compiler_params=_cparams(("arbitrary", "arbitrary")),
    )(h1, *pieces)


def _matmul_tn(a, b, name, cast_b=False):
    T, K = a.shape
    N = b.shape[1]
    bt = min(T, 512)
    bk = _tile_div(K, 1536)
    bn = _tile_div(N, 1536)
    nt = T // bt

    def body(a_ref, b_ref, o_ref, acc_ref):
        t = pl.program_id(2)

        @pl.when(t == 0)
        def _():
            acc_ref[...] = jnp.zeros_like(acc_ref)

        bv = b_ref[...]
        if cast_b:
            bv = bv.astype(BF16)
        acc_ref[...] += _tn(a_ref[...], bv)

        @pl.when(t == nt - 1)
        def _():
            o_ref[...] = acc_ref[...].astype(BF16)

    return pl.pallas_call(
        body,
        name=name,
        grid=(K // bk, N // bn, nt),
        in_specs=[
            pl.BlockSpec((bt, bk), lambda k, n, t: (t, k)),
            pl.BlockSpec((bt, bn), lambda k, n, t: (t, n)),
        ],
        out_specs=pl.BlockSpec((bk, bn), lambda k, n, t: (k, n)),
        out_shape=jax.ShapeDtypeStruct((K, N), BF16),
        scratch_shapes=[pltpu.VMEM((bk, bn), F32)],
        compiler_params=_cparams(("arbitrary", "arbitrary", "arbitrary")),
    )(a, b)


def _local_step(x, mem, target, p, tm, tq, late=None):
    T, D = x.shape
    w_in = p["w_in"]
    w_qkv = w_in[:, :QKV_W]
    w_f_t = w_in[:, QKV_W:].T
    b_f = p["b_forget"].reshape(N_FOX, 1)

    proj, h1, xf, c = _inproj_fwd(x, p["attn_norm_g"], w_qkv, w_f_t, b_f, tm)
    c_col = c.reshape(N_FOX, 1, T)
    c_row = c.reshape(N_FOX, 1, T)
    c_ends = c[:, tq - 1::tq].reshape(N_FOX, 1, T // tq)
    fox_o, lse, gathered = _fox_fwd(proj, c_col, c_row, c_ends, tq, gather=[late[n] for n in _LATE] if late else ())
    if late:
        p = dict(p, **{n: _gathered_full(n, gv) for n, gv in zip(_LATE, gathered)})
    sb_o, sb_ltot, sb_live = _sb_fwd(proj, tq)
    x1, mixed = _post_attn_fwd(fox_o, sb_o, p["fox_out_g"], p["sb_out_g"], p["w_out"], x, tm)
    mb, kv = _mem_kv_fwd(mem, p["mem_norm_g"], p["w_mkv"])
    x2, h2, qb, om = _xattn_fwd(x1, p["xattn_norm_g"], p["w_mq"], kv, p["w_mo"], tm)
    x3, h3, ug, uv, yg, yv, a = _ffn_fwd(
        x2, p["ffn_norm_g"], p["w_up"], p["conv_w"], p["conv_b"], p["w_down"], tm)
    dx3, loss_blk, d_final_g = _loss_head(x3, p["final_norm_g"], target, tm)

    g = {"final_norm_g": d_final_g}
    dx2, du_g, du_v, g["ffn_norm_g"], dc_g, dc_v = _ffn_bwd(
        dx3, x2, p["ffn_norm_g"], ug, uv, yg, yv, p["conv_w"], p["w_down"], p["w_up"], tm)
    g["w_down"] = _matmul_tn(a, dx3, "dw_down", cast_b=True)
    g["w_up"] = jnp.concatenate([_matmul_tn(h3, du_g, "dw_up_gate"), _matmul_tn(h3, du_v, "dw_up_val")], axis=1)
    dconv = jnp.concatenate([dc_g, dc_v], axis=1)
    g["conv_w"] = dconv[0:3]
    g["conv_b"] = dconv[3:4]
    dx1, dq_m, dkv, g["xattn_norm_g"] = _xattn_bwd(dx2, x1, p["xattn_norm_g"], qb, kv, p["w_mo"], p["w_mq"], tm)
    g["w_mo"] = _matmul_tn(om, dx2, "dw_mo", cast_b=True)
    g["w_mq"] = _matmul_tn(h2, dq_m, "dw_mq")
    g["w_mkv"], g["mem_norm_g"] = _mem_kv_bwd(mem, p["mem_norm_g"], mb, dkv, p["w_mkv"])
    d_fox, d_sb, g["fox_out_g"], g["sb_out_g"] = _post_attn_bwd(
        dx1, fox_o, sb_o, p["fox_out_g"], p["sb_out_g"], p["w_out"], tm)
    g["w_out"] = _matmul_tn(mixed, dx1, "dw_out", cast_b=True)
    dq_s, dk_s, dv_s = _sb_bwd(proj, sb_ltot, sb_live, d_sb, tq)
    dq_f, dk_f, dv_f, dck, dcq, parts = _fox_bwd(
        proj, c_col, c_row, c_ends, lse, d_fox, fox_o, tq,
        scatter=[_grad_blocks(n, g[n]) for n in _LATE] if late else ())
    if late:
        g["parts"] = dict(zip(_LATE, parts))
    dxf, db, dwf_t = _forget_bwd(dcq.reshape(N_FOX, T), dck.reshape(N_FOX, T), xf, h1, min(T, 512))
    g["b_forget"] = db.reshape(1, N_FOX)
    pieces = [dq_f, dk_f, dv_f, dq_s, dk_s, dv_s]
    g["w_in"] = jnp.concatenate([_dw_in(h1, pieces, "dw_in"), dwf_t.T.astype(BF16)], axis=1)
    grad_x, g["attn_norm_g"], parts = _inproj_bwd(
        pieces, dxf.T, w_in, w_f_t, x, p["attn_norm_g"], dx1, tm,
        scatter=[_grad_blocks("w_in", g["w_in"])] if late else ())
    if late:
        (g["parts"]["w_in"],) = parts
    return loss_blk, grad_x, g


def _mesh_pos():
    return lax.axis_index("x"), lax.axis_index("y"), lax.axis_index("c")


def _flip(pos, k):
    return tuple(1 - v if (k >> b) & 1 else v for v, b in zip(pos, (2, 1, 0)))


def _slot(pos):
    return 4 * pos[0] + 2 * pos[1] + pos[2]


_CHIPS = (4, 2, 6)


def _comm_sems(n):
    return [pltpu.SemaphoreType.DMA((7 * n,)), pltpu.SemaphoreType.DMA((7 * n,)), pltpu.SemaphoreType.DMA((n,))]


class _Gather:
    def __init__(self, ins, outs, send_sems, recv_sems, local_sems):
        self.ins, self.outs, self.n = ins, outs, len(ins)
        self.send_sems, self.recv_sems, self.local_sems = send_sems, recv_sems, local_sems
        self.me = _mesh_pos()
        self.sibling = _flip(self.me, 1)

    def _copy(self, a, kk, block, to, src=None):
        rows = self.outs[a].at[_slot(block)]
        return pltpu.make_async_remote_copy(
            src_ref=rows if src is None else src, dst_ref=rows,
            send_sem=self.send_sems.at[7 * a + kk], recv_sem=self.recv_sems.at[7 * a + kk],
            device_id=to, device_id_type=MESH)

    def _mine(self):
        return [pltpu.make_async_copy(self.ins[a], self.outs[a].at[_slot(self.me)], self.local_sems.at[a])
                for a in range(self.n)]

    def _first(self):
        out = []
        for a in range(self.n):
            out.append(self._copy(a, 0, self.me, self.sibling, src=self.ins[a]))
            out += [self._copy(a, 1 + j, self.me, _flip(self.me, k), src=self.ins[a]) for j, k in enumerate(_CHIPS)]
        return out

    def _passed(self):
        return [self._copy(a, 4 + j, _flip(self.me, k), self.sibling)
                for j, k in enumerate(_CHIPS) for a in range(self.n)]

    def start(self):
        for cp in self._mine() + self._first():
            cp.start()

    def forward(self):
        for j, k in enumerate(_CHIPS):
            for a in range(self.n):
                self._copy(a, 1 + j, _flip(self.me, k), self.me).wait_recv()
                self._copy(a, 4 + j, _flip(self.me, k), self.sibling).start()

    def finish(self):
        for a in range(self.n):
            self._copy(a, 0, self.sibling, self.me).wait_recv()
            for j, k in enumerate(_CHIPS):
                self._copy(a, 4 + j, _flip(self.sibling, k), self.me).wait_recv()
        for cp in self._first() + self._passed():
            cp.wait_send()
        for cp in self._mine():
            cp.wait()


class _Scatter:
    def __init__(self, ins, outs, send_sems, recv_sems, local_sems):
        self.ins, self.outs, self.n = ins, outs, len(ins)
        self.send_sems, self.recv_sems, self.local_sems = send_sems, recv_sems, local_sems
        self.me = _mesh_pos()

    def _copy(self, a, k, landed=False):
        peer = _flip(self.me, k)
        return pltpu.make_async_remote_copy(
            src_ref=self.ins[a].at[_slot(peer)], dst_ref=self.outs[a].at[_slot(peer if landed else self.me)],
            send_sem=self.send_sems.at[7 * a + k - 1], recv_sem=self.recv_sems.at[7 * a + k - 1],
            device_id=peer, device_id_type=MESH)

    def _mine(self):
        s = _slot(self.me)
        return [pltpu.make_async_copy(self.ins[a].at[s], self.outs[a].at[s], self.local_sems.at[a])
                for a in range(self.n)]

    def start(self):
        for cp in self._mine() + [self._copy(a, k) for k in range(1, 8) for a in range(self.n)]:
            cp.start()

    def finish(self):
        for k in range(1, 8):
            for a in range(self.n):
                self._copy(a, k, landed=True).wait_recv()
        for k in range(1, 8):
            for a in range(self.n):
                self._copy(a, k).wait_send()
        for cp in self._mine():
            cp.wait()


_ANY = pl.BlockSpec(memory_space=pl.ANY)


def _gathered_shapes(shards):
    return [jax.ShapeDtypeStruct((N_DEV,) + s.shape, s.dtype) for s in shards]


def _all_gather(shards, name):
    n = len(shards)

    def body(*refs):
        g = _Gather(refs[:n], refs[n:2 * n], *refs[2 * n:])
        g.start()
        g.forward()
        g.finish()

    return pl.pallas_call(
        body, name=name, in_specs=[_ANY] * n, out_specs=[_ANY] * n,
        out_shape=_gathered_shapes(shards), scratch_shapes=_comm_sems(n),
    )(*shards)


def _adamw_math(w, g, m, v):
    m2 = ADAM_B1 * m + (1.0 - ADAM_B1) * g
    v2 = ADAM_B2 * v + (1.0 - ADAM_B2) * (g * g)
    m_hat = m2 / (1.0 - ADAM_B1 ** ADAM_STEP)
    v_hat = v2 / (1.0 - ADAM_B2 ** ADAM_STEP)
    delta = -ADAM_LR * (m_hat / (jnp.sqrt(v_hat) + ADAM_EPS) + ADAM_WD * w)
    return delta, m2, v2


def _adamw(w, parts, m, v, name):
    R, C = w.shape
    br = 128 if R % 128 == 0 else R

    def body(w_ref, p_ref, m_ref, v_ref, g_ref, d_ref, nm_ref, nv_ref):
        g = p_ref[0].astype(F32)
        for s in range(1, N_DEV):
            g = g + p_ref[s].astype(F32)
        g_ref[...] = g
        d_ref[...], nm_ref[...], nv_ref[...] = _adamw_math(w_ref[...], g, m_ref[...], v_ref[...])

    spec = pl.BlockSpec((br, C), lambda i: (i, 0))
    return pl.pallas_call(
        body,
        name=name,
        grid=(R // br,),
        in_specs=[spec, pl.BlockSpec((N_DEV, br, C), lambda i: (0, i, 0)), spec, spec],
        out_specs=[spec] * 4,
        out_shape=[jax.ShapeDtypeStruct((R, C), F32)] * 4,
        compiler_params=_cparams(("arbitrary",)),
    )(w, parts, m, v)


_SHARDED = ("w_in", "w_out", "w_mq", "w_mkv", "w_mo", "w_up", "conv_w", "w_down")
_LATE = _SHARDED[1:]
_COL_SHARDED = ("w_in", "w_mkv", "w_up", "conv_w")
_REPLICATED = ("attn_norm_g", "b_forget", "fox_out_g", "sb_out_g", "xattn_norm_g", "mem_norm_g",
               "ffn_norm_g", "conv_b", "final_norm_g")
_WEIGHTS = ("attn_norm_g", "w_in", "b_forget", "fox_out_g", "sb_out_g", "w_out", "xattn_norm_g", "mem_norm_g",
            "w_mq", "w_mkv", "w_mo", "ffn_norm_g", "w_up", "conv_w", "conv_b", "w_down", "final_norm_g")


def _pack_rows(n):
    return -(-n // 128)


def _pack(vals, rows_total):
    parts = []
    for v in vals:
        flat = v.reshape(-1)
        parts.append(jnp.pad(flat, (0, _pack_rows(flat.shape[0]) * 128 - flat.shape[0])))
    flat = jnp.concatenate(parts)
    return jnp.pad(flat, (0, rows_total * 128 - flat.shape[0])).reshape(rows_total, 128)


def _unpack(packed, shapes):
    out = []
    r = 0
    for shp in shapes:
        n = 1
        for d in shp:
            n *= d
        out.append(packed[r:r + _pack_rows(n)].reshape(-1)[:n].reshape(shp))
        r += _pack_rows(n)
    return out


def _gathered_full(name, gathered):
    if name in _COL_SHARDED:
        return jnp.transpose(gathered, (1, 0, 2)).reshape(gathered.shape[1], -1)
    return gathered.reshape(-1, gathered.shape[2])


def _to_blocks(name, full):
    if name in _COL_SHARDED:
        r = full.shape[0]
        return jnp.transpose(full.reshape(r, N_DEV, -1), (1, 0, 2))
    return full.reshape(N_DEV, -1, full.shape[1])


def _grad_blocks(name, full):
    blocks = _to_blocks(name, full)
    return blocks if name == "conv_w" else blocks.astype(BF16)


def _step(args, tm, tq):
    w = {n: args[n] for n in _WEIGHTS}
    mom = {n: args["m_" + n] for n in _WEIGHTS}
    var = {n: args["v_" + n] for n in _WEIGHTS}
    x = args["x"][0]
    mem = args["mem"][0]
    target = args["loss_target"][0]

    def flat2(a):
        return a.reshape(a.shape[-2], a.shape[-1]) if a.ndim == 3 else a.reshape(1, -1)

    shards = {n: flat2(w[n]) if n == "conv_w" else flat2(w[n]).astype(BF16) for n in _SHARDED}
    (w_in_all,) = _all_gather([shards["w_in"]], "gather_w_in")
    p = {"w_in": _gathered_full("w_in", w_in_all)}
    for n in _REPLICATED:
        p[n] = flat2(w[n])

    loss_blk, grad_x, g = _local_step(x, mem, target, p, tm, tq, late={n: shards[n] for n in _LATE})

    parts = g["parts"]
    out = {}
    for n in _SHARDED:
        res = _adamw(flat2(w[n]), parts[n], flat2(mom[n]), flat2(var[n]), "adamw_" + n)
        out[n] = [r.reshape(w[n].shape) for r in res]

    shapes = [w[n].shape for n in _REPLICATED]
    rows = sum(_pack_rows(flat2(w[n]).shape[1]) for n in _REPLICATED) + 1
    rows = -(-rows // 8) * 8
    g_pack = _pack([g[n] for n in _REPLICATED] + [loss_blk[0:1, :]], rows)
    (g_all,) = _all_gather([g_pack], "gather_small")
    res = _adamw(_pack([w[n] for n in _REPLICATED], rows), g_all,
                 _pack([mom[n] for n in _REPLICATED], rows), _pack([var[n] for n in _REPLICATED], rows),
                 "adamw_small")
    n_rows_params = sum(_pack_rows(flat2(w[n]).shape[1]) for n in _REPLICATED)
    loss = res[0][n_rows_params, 0]
    unpacked = [_unpack(r, shapes) for r in res]
    for k, n in enumerate(_REPLICATED):
        out[n] = [unpacked[q][k] for q in range(4)]

    grads = [out[n][0] for n in _WEIGHTS]
    deltas = [out[n][1] for n in _WEIGHTS]
    new_m = [out[n][2] for n in _WEIGHTS]
    new_v = [out[n][3] for n in _WEIGHTS]
    return (loss, grad_x[None], *grads, *deltas, *new_m, *new_v)


def kernel(x, mem, attn_norm_g, w_in, b_forget, fox_out_g, sb_out_g, w_out, xattn_norm_g, mem_norm_g, w_mq, w_mkv, w_mo, ffn_norm_g, w_up, conv_w, conv_b, w_down, final_norm_g, loss_target, m_attn_norm_g, m_w_in, m_b_forget, m_fox_out_g, m_sb_out_g, m_w_out, m_xattn_norm_g, m_mem_norm_g, m_w_mq, m_w_mkv, m_w_mo, m_ffn_norm_g, m_w_up, m_conv_w, m_conv_b, m_w_down, m_final_norm_g, v_attn_norm_g, v_w_in, v_b_forget, v_fox_out_g, v_sb_out_g, v_w_out, v_xattn_norm_g, v_mem_norm_g, v_w_mq, v_w_mkv, v_w_mo, v_ffn_norm_g, v_w_up, v_conv_w, v_conv_b, v_w_down, v_final_norm_g):
    args = dict(locals())
    T = x.shape[1]
    return _step(args, tm=min(T, 512), tq=min(T, 256))
```

```python
import functools

import jax
import jax.numpy as jnp
from jax import lax
from jax.experimental import pallas as pl
from jax.experimental.pallas import tpu as pltpu

F32 = jnp.float32
BF16 = jnp.bfloat16
EPS = 1e-6
NEG = -1e30
LOG2E = 1.4426950408889634

HEAD_DIM = 64
N_FOX = 8
FOX_W = 512
QKV_W = 3072
N_MEM_HEADS = 4
MEM_HD = 256
D_FF = 2816
FF_CHUNK = 256
N_DEV = 8

ADAM_LR = 0.001
ADAM_B1 = 0.9
ADAM_B2 = 0.999
ADAM_EPS = 1e-08
ADAM_WD = 0.01
ADAM_STEP = 10

SB_SUM_TERMS = 1

VMEM_LIMIT = 56 * 1024 * 1024
MESH = pl.DeviceIdType.MESH


def _cparams(sem=None):
    return pltpu.CompilerParams(dimension_semantics=sem, vmem_limit_bytes=VMEM_LIMIT)


def _nt(a, b):
    return lax.dot_general(a, b, (((1,), (1,)), ((), ())), preferred_element_type=F32)


def _tn(a, b):
    return lax.dot_general(a, b, (((0,), (0,)), ((), ())), preferred_element_type=F32)


def _nn(a, b):
    return jnp.dot(a, b, preferred_element_type=F32)


def _split_dot(a, m01, terms):
    out = None
    r = a
    for t in range(terms):
        p = r.astype(BF16)
        d = _nn(p, m01)
        out = d if out is None else out + d
        if t + 1 < terms:
            r = r - p.astype(F32)
    return out


def _rstd(xv):
    return lax.rsqrt(jnp.mean(xv * xv, axis=-1, keepdims=True) + EPS)


def _norm_bwd(xv, g, dh):
    r = _rstd(xv)
    xhat = xv * r
    dxhat = dh * g
    dx = r * (dxhat - xhat * jnp.mean(dxhat * xhat, axis=-1, keepdims=True))
    dg = jnp.sum(dh * xhat, axis=0, keepdims=True)
    return dx, dg


def _tile_div(n, cap):
    best = None
    for d in range(128, min(n, cap) + 1, 128):
        if n % d == 0:
            best = d
    assert best is not None, n
    return best


def _inproj_fwd(x, g1, w_qkv, w_f_t, b_f, tm):
    T, D = x.shape
    N = w_qkv.shape[1]
    H = w_f_t.shape[0]

    def body(x_ref, g_ref, w_ref, wf_ref, b_ref, proj_ref, h_ref, xf_ref, c_ref, carry_ref):
        i = pl.program_id(0)

        @pl.when(i == 0)
        def _():
            carry_ref[...] = jnp.zeros_like(carry_ref)

        xv = x_ref[...]
        h = (xv * _rstd(xv) * g_ref[...]).astype(BF16)
        h_ref[...] = h
        for n0 in range(0, N, 512):
            proj_ref[:, n0:n0 + 512] = _nn(h, w_ref[:, n0:n0 + 512]).astype(BF16)
        xf = _nt(wf_ref[...], h) + b_ref[...]
        xf_ref[...] = xf
        logf = jnp.minimum(xf, 0.0) - jnp.log1p(jnp.exp(-jnp.abs(xf)))
        row = lax.broadcasted_iota(jnp.int32, (tm, tm), 0)
        col = lax.broadcasted_iota(jnp.int32, (tm, tm), 1)
        upper = jnp.where(row <= col, 1.0, 0.0).astype(BF16)
        c = _split_dot(logf, upper, 3) + carry_ref[...]
        c_ref[...] = c
        carry_ref[...] = c[:, tm - 1:tm]

    return pl.pallas_call(
        body,
        name="inproj_fwd",
        grid=(T // tm,),
        in_specs=[
            pl.BlockSpec((tm, D), lambda i: (i, 0)),
            pl.BlockSpec((1, D), lambda i: (0, 0)),
            pl.BlockSpec((D, N), lambda i: (0, 0)),
            pl.BlockSpec((H, D), lambda i: (0, 0)),
            pl.BlockSpec((H, 1), lambda i: (0, 0)),
        ],
        out_specs=[
            pl.BlockSpec((tm, N), lambda i: (i, 0)),
            pl.BlockSpec((tm, D), lambda i: (i, 0)),
            pl.BlockSpec((H, tm), lambda i: (0, i)),
            pl.BlockSpec((H, tm), lambda i: (0, i)),
        ],
        out_shape=[
            jax.ShapeDtypeStruct((T, N), BF16),
            jax.ShapeDtypeStruct((T, D), BF16),
            jax.ShapeDtypeStruct((H, T), F32),
            jax.ShapeDtypeStruct((H, T), F32),
        ],
        scratch_shapes=[pltpu.VMEM((H, 1), F32)],
        compiler_params=_cparams(("arbitrary",)),
    )(x, g1, w_qkv, w_f_t, b_f)


def _head_q(q, hh, lane):
    hmask = (lane >= HEAD_DIM * hh) & (lane < HEAD_DIM * (hh + 1))
    qh = jnp.where(hmask, q.astype(F32), 0.0) * (HEAD_DIM ** -0.5)
    return qh.astype(BF16), hmask


def _pipeline3(n, stage_a, stage_b, stage_c, diag_last, alive=None):
    stage_a(0, 0)
    if diag_last:
        @pl.when(n == 1)
        def _():
            stage_b(0, 0, True)

        @pl.when(n >= 2)
        def _():
            stage_b(0, 0, False)
    else:
        stage_b(0, 0, True)

    @pl.when(n >= 2)
    def _():
        stage_a(1, 1)

    def pair(m, carry):
        t = 2 + 2 * m
        stage_c(t - 2)
        stage_b(t - 1, 1, False)
        stage_a(t, 0)
        stage_c(t - 1)
        stage_b(t, 0, False)
        stage_a(t + 1, 1)
        return carry

    pairs = (n - 2) // 2
    if alive is None:
        lax.fori_loop(0, pairs, pair, 0)
        go_on = True
        done = n
    else:
        def more(state):
            return (state[0] < pairs) & state[1]

        def step(state):
            pair(state[0], 0)
            return state[0] + 1, alive()

        m_end, go_on = lax.while_loop(more, step, (jnp.int32(0), jnp.bool_(True)))
        done = jnp.where(go_on, n, 2 * m_end)
    odd = n % 2 == 1

    @pl.when((n >= 3) & odd & go_on)
    def _():
        stage_c(n - 3)
        stage_b(n - 2, 1, False)
        stage_a(n - 1, 0)

    @pl.when((n >= 2) & odd & go_on)
    def _():
        stage_c(n - 2)
        stage_b(n - 1, 0, diag_last)

    @pl.when((n >= 2) & jnp.logical_not(odd) & go_on)
    def _():
        stage_c(n - 2)
        stage_b(n - 1, 1, diag_last)

    if alive is None:
        stage_c(n - 1)
    else:
        @pl.when(go_on)
        def _():
            stage_c(n - 1)

    return done


def _lanes2(x):
    return jnp.concatenate([x, x], axis=1)


def _lanes_to_rows(vec, eye):
    return jnp.sum(jnp.where(eye, jnp.broadcast_to(vec, eye.shape), 0.0), axis=1, keepdims=True)


def _rows_to_lanes(rep, eye):
    return jnp.sum(jnp.where(eye, _lanes2(rep), 0.0), axis=0, keepdims=True)


FOX_DEAD = -110.0


def _fox_key_norms(k_ref, kn_s, lane):
    T = k_ref.shape[0]
    rows = min(T, 512)
    for hh in range(2):
        hmask = (lane >= HEAD_DIM * hh) & (lane < HEAD_DIM * (hh + 1))

        def chunk(n, best, hmask=hmask):
            kf = jnp.where(hmask, k_ref[pl.ds(pl.multiple_of(n * rows, rows), rows), :].astype(F32), 0.0)
            sq = jnp.sum(kf * kf, axis=1, keepdims=True)
            return jnp.maximum(best, jnp.max(sq, axis=0, keepdims=True))

        best = lax.fori_loop(0, T // rows, chunk, jnp.zeros((1, 1), F32))
        kn_s[hh] = jnp.broadcast_to(best, kn_s.shape[1:])


def _fox_live_blocks(i, qh_s, kn_s, cq_ref, cke_ref):
    nq = cke_ref.shape[-1]
    jj = lax.broadcasted_iota(jnp.int32, (1, nq), 1)
    first = None
    for hh in range(2):
        qf = qh_s[hh].astype(F32)
        qn = jnp.max(jnp.sum(qf * qf, axis=1, keepdims=True), axis=0, keepdims=True)
        zb = jnp.sqrt(qn * kn_s[hh][0:1, 0:1]) * 1.001
        bound = (2.0 * zb + cq_ref[hh][:, 0:1]) - cke_ref[hh]
        live = (bound >= FOX_DEAD) & (jj <= i)
        f = jnp.min(jnp.where(live, jj, i).astype(F32), axis=1, keepdims=True)
        first = f if first is None else jnp.minimum(first, f)
    return i + 1 - first[0, 0].astype(jnp.int32)


def _ride_along(exchange, at_start, at_middle, at_end):
    @pl.when(at_start)
    def _():
        exchange.start()

    if at_middle is not None:
        @pl.when(at_middle)
        def _():
            exchange.forward()

    def finish():
        @pl.when(at_end)
        def _():
            exchange.finish()

    return finish


def _fox_fwd(proj, c_col, c_row, c_ends, tq, gather=()):
    T = proj.shape[0]
    assert tq == 256
    nq = T // tq
    ng = len(gather)

    def body(*refs):
        q_ref, k_ref, v_ref, cq_ref, ck_ref, cke_ref = refs[:6]
        o_ref, lse_ref = refs[6 + ng:8 + ng]
        qh_s, cq_s, z_s, p_s, al_s, m_s, acc_s, kn_s = refs[8 + 2 * ng:16 + 2 * ng]
        i = pl.program_id(1)
        if ng:
            pair = pl.program_id(0)
            finish = _ride_along(_Gather(refs[6:6 + ng], refs[8 + ng:8 + 2 * ng], *refs[16 + 2 * ng:]),
                                 (pair == 0) & (i == 0), (pair == 1) & (i == 0), (pair == 3) & (i == nq - 1))
        lane = lax.broadcasted_iota(jnp.int32, (1, 128), 1)
        row = lax.broadcasted_iota(jnp.int32, (tq, tq), 0)
        col = lax.broadcasted_iota(jnp.int32, (tq, tq), 1)

        @pl.when(i == 0)
        def _():
            _fox_key_norms(k_ref, kn_s, lane)

        q = q_ref[...]
        for hh in range(2):
            qh_s[hh] = _head_q(q, hh, lane)[0]
            cq_s[hh] = jnp.broadcast_to(_lanes_to_rows(cq_ref[hh], row == col), (tq, tq))
        m_s[...] = jnp.full(m_s.shape, NEG, F32)
        acc_s[...] = jnp.zeros_like(acc_s)

        def rows(t):
            return pl.ds(pl.multiple_of((i - t) * tq, tq), tq)

        def stage_a(t, slot):
            k = k_ref[rows(t), :]
            for hh in range(2):
                z_s[slot, hh] = _nt(qh_s[hh], k)

        def stage_b(t, slot, diag):
            for hh in range(2):
                s = z_s[slot, hh] + cq_s[hh] - ck_ref[hh, :, rows(t)]
                if diag:
                    s = jnp.where(col <= row, s, NEG)
                m = m_s[hh]
                half = jnp.maximum(s[:, :128], s[:, 128:])
                m_new = jnp.maximum(m, jnp.max(half, axis=1, keepdims=True))
                m_s[hh] = m_new
                al_s[hh] = jnp.exp(m - m_new)
                p_s[hh] = jnp.exp(s - _lanes2(m_new)).astype(BF16)

        def stage_c(t):
            v = v_ref[rows(t), :]
            for hh in range(2):
                own = (lane >= HEAD_DIM * hh) & (lane < HEAD_DIM * (hh + 1))
                acc_s[hh] = al_s[hh] * acc_s[hh] + _nn(p_s[hh], jnp.where(own, v, 1.0).astype(BF16))

        _pipeline3(_fox_live_blocks(i, qh_s, kn_s, cq_ref, cke_ref), stage_a, stage_b, stage_c, False)
        halves = []
        for hh in range(2):
            acc = acc_s[hh]
            own = (lane >= HEAD_DIM * hh) & (lane < HEAD_DIM * (hh + 1))
            halves.append(jnp.where(own, pltpu.roll(acc, HEAD_DIM, axis=1), acc))
        l0, l1 = halves
        o_ref[...] = jnp.where(lane < HEAD_DIM, acc_s[0] / l0, acc_s[1] / l1)
        lse_ref[0] = _rows_to_lanes(m_s[0] + jnp.log(l0), row == col)
        lse_ref[1] = _rows_to_lanes(m_s[1] + jnp.log(l1), row == col)
        if ng:
            finish()

    res = pl.pallas_call(
        body,
        name="fox_fwd",
        grid=(4, nq),
        in_specs=[
            pl.BlockSpec((tq, 128), lambda p, i: (i, p)),
            pl.BlockSpec((T, 128), lambda p, i: (0, 4 + p)),
            pl.BlockSpec((T, 128), lambda p, i: (0, 8 + p)),
            pl.BlockSpec((2, 1, tq), lambda p, i: (p, 0, i)),
            pl.BlockSpec((2, 1, T), lambda p, i: (p, 0, 0)),
            pl.BlockSpec((2, 1, nq), lambda p, i: (p, 0, 0)),
        ] + [_ANY] * ng,
        out_specs=[
            pl.BlockSpec((tq, 128), lambda p, i: (i, p)),
            pl.BlockSpec((2, 1, tq), lambda p, i: (p, 0, i)),
        ] + [_ANY] * ng,
        out_shape=[
            jax.ShapeDtypeStruct((T, FOX_W), F32),
            jax.ShapeDtypeStruct((N_FOX, 1, T), F32),
        ] + _gathered_shapes(gather),
        scratch_shapes=[
            pltpu.VMEM((2, tq, 128), BF16),
            pltpu.VMEM((2, tq, tq), F32),
            pltpu.VMEM((2, 2, tq, tq), F32),
            pltpu.VMEM((2, tq, tq), BF16),
            pltpu.VMEM((2, tq, 128), F32),
            pltpu.VMEM((2, tq, 128), F32),
            pltpu.VMEM((2, tq, 128), F32),
            pltpu.VMEM((2, 8, 128), F32),
        ] + (_comm_sems(ng) if ng else []),
        compiler_params=_cparams(("arbitrary", "arbitrary")),
    )(proj, proj, proj, c_col, c_row, c_ends, *gather)
    res = list(res)
    return res[0], res[1], res[2:]


def _sb_logs(zn, strict):
    e = jnp.exp2(jnp.abs(zn) * (-LOG2E))
    L = jnp.minimum(zn, 0.0) - jnp.log(1.0 + e)
    G = L - zn
    if strict is not None:
        L = jnp.where(strict, L, 0.0)
    return L, G


SB_DEAD = -110.0


def _sb_fwd(proj, tq):
    T = proj.shape[0]
    nq = T // tq

    def body(q_ref, k_ref, v_ref, o_ref, ltot_ref, live_ref, qh_s, z_s, g_s, tot_s, run_s, acc_s):
        i = pl.program_id(1)
        lane = lax.broadcasted_iota(jnp.int32, (1, 128), 1)
        row = lax.broadcasted_iota(jnp.int32, (tq, tq), 0)
        col = lax.broadcasted_iota(jnp.int32, (tq, tq), 1)
        strict = col < row
        later = jnp.where(row > col, 1.0, 0.0).astype(BF16)
        q = q_ref[...]
        for hh in range(2):
            qh_s[hh] = -_head_q(q, hh, lane)[0]
        run_s[...] = jnp.zeros_like(run_s)
        acc_s[...] = jnp.zeros_like(acc_s)

        def rows(t):
            return pl.ds(pl.multiple_of((i - t) * tq, tq), tq)

        def stage_a(t, slot):
            k = k_ref[rows(t), :]
            for hh in range(2):
                z_s[slot, hh] = _nt(qh_s[hh], k)

        def stage_b(t, slot, diag):
            for hh in range(2):
                L, g = _sb_logs(z_s[slot, hh], strict if diag else None)
                if diag:
                    g = jnp.where(strict, g, NEG)
                after = _split_dot(L, later, SB_SUM_TERMS)
                g_s[hh] = g + after
                first = L[:, 0:1]
                if SB_SUM_TERMS == 1:
                    first = first.astype(BF16).astype(F32)
                tot_s[hh] = jnp.broadcast_to(after[:, 0:1] + first, (tq, 128))

        def stage_c(t):
            v = v_ref[rows(t), :]
            for hh in range(2):
                run = run_s[hh]
                a = jnp.exp(g_s[hh] + _lanes2(run))
                acc_s[hh] += _nn(a.astype(BF16), v)
                run_s[hh] = run + tot_s[hh]

        def alive():
            return jnp.max(jnp.maximum(run_s[0], run_s[1])) > SB_DEAD

        done = _pipeline3(i + 1, stage_a, stage_b, stage_c, False, alive)
        ltot_ref[0] = _rows_to_lanes(run_s[0], row == col)
        ltot_ref[1] = _rows_to_lanes(run_s[1], row == col)
        o_ref[...] = jnp.where(lane < HEAD_DIM, acc_s[0], acc_s[1])
        at = lax.broadcasted_iota(jnp.int32, (1, nq), 1)

        @pl.when(i == 0)
        def _():
            live_ref[0] = jnp.zeros((1, nq), F32)

        live_ref[0] = jnp.where(at == i, done.astype(F32), live_ref[0])

    return pl.pallas_call(
        body,
        name="sb_fwd",
        grid=(4, nq),
        in_specs=[
            pl.BlockSpec((tq, 128), lambda p, i: (i, 12 + p)),
            pl.BlockSpec((T, 128), lambda p, i: (0, 16 + p)),
            pl.BlockSpec((T, 128), lambda p, i: (0, 20 + p)),
        ],
        out_specs=[
            pl.BlockSpec((tq, 128), lambda p, i: (i, p)),
            pl.BlockSpec((2, 1, tq), lambda p, i: (p, 0, i)),
            pl.BlockSpec((1, 1, nq), lambda p, i: (p, 0, 0)),
        ],
        out_shape=[
            jax.ShapeDtypeStruct((T, FOX_W), F32),
            jax.ShapeDtypeStruct((N_FOX, 1, T), F32),
            jax.ShapeDtypeStruct((N_FOX // 2, 1, nq), F32),
        ],
        scratch_shapes=[
            pltpu.VMEM((2, tq, 128), BF16),
            pltpu.VMEM((2, 2, tq, tq), F32),
            pltpu.VMEM((2, tq, tq), F32),
            pltpu.VMEM((2, tq, 128), F32),
            pltpu.VMEM((2, tq, 128), F32),
            pltpu.VMEM((2, tq, 128), F32),
        ],
        compiler_params=_cparams(("arbitrary", "arbitrary")),
    )(proj, proj, proj)


def _post_attn_fwd(fox_o, sb_o, gf, gs, w_out, x, tm):
    T, D = x.shape

    def body(f_ref, s_ref, gf_ref, gs_ref, w_ref, x_ref, x1_ref, mix_ref):
        f = f_ref[...]
        s = s_ref[...]
        mix_ref[:, :FOX_W] = (f * _rstd(f) * gf_ref[...]).astype(BF16)
        mix_ref[:, FOX_W:] = (s * _rstd(s) * gs_ref[...]).astype(BF16)
        x1_ref[...] = x_ref[...] + _nn(mix_ref[...], w_ref[...])

    return pl.pallas_call(
        body,
        name="post_attn_fwd",
        grid=(T // tm,),
        in_specs=[
            pl.BlockSpec((tm, FOX_W), lambda i: (i, 0)),
            pl.BlockSpec((tm, FOX_W), lambda i: (i, 0)),
            pl.BlockSpec((1, FOX_W), lambda i: (0, 0)),
            pl.BlockSpec((1, FOX_W), lambda i: (0, 0)),
            pl.BlockSpec((D, D), lambda i: (0, 0)),
            pl.BlockSpec((tm, D), lambda i: (i, 0)),
        ],
        out_specs=[
            pl.BlockSpec((tm, D), lambda i: (i, 0)),
            pl.BlockSpec((tm, D), lambda i: (i, 0)),
        ],
        out_shape=[jax.ShapeDtypeStruct((T, D), F32), jax.ShapeDtypeStruct((T, D), BF16)],
        compiler_params=_cparams(("arbitrary",)),
    )(fox_o, sb_o, gf, gs, w_out, x)


def _mem_kv_fwd(mem, gm, w_mkv):
    M, D = mem.shape
    N = w_mkv.shape[1]

    def body(mem_ref, g_ref, w_ref, m_ref, kv_ref):
        mv = mem_ref[...]
        m = (mv * _rstd(mv) * g_ref[...]).astype(BF16)
        m_ref[...] = m
        for n0 in range(0, N, 512):
            kv_ref[:, n0:n0 + 512] = _nn(m, w_ref[:, n0:n0 + 512]).astype(BF16)

    return pl.pallas_call(
        body,
        name="mem_kv_fwd",
        out_shape=[jax.ShapeDtypeStruct((M, D), BF16), jax.ShapeDtypeStruct((M, N), BF16)],
        compiler_params=_cparams(),
    )(mem, gm, w_mkv)


def _xattn_probs(qb, kv, h):
    k = kv[:, h * MEM_HD:(h + 1) * MEM_HD]
    s = _nt(qb[:, h * MEM_HD:(h + 1) * MEM_HD], k) * (MEM_HD ** -0.5)
    s = s - jnp.max(s, axis=1, keepdims=True)
    p = jnp.exp(s)
    return p / jnp.sum(p, axis=1, keepdims=True)


def _xattn_fwd(x1, g2, w_mq, kv, w_mo, tm):
    T, D = x1.shape
    M = kv.shape[0]

    def body(x_ref, g_ref, wq_ref, kv_ref, wo_ref, x2_ref, h_ref, q_ref, om_ref):
        xv = x_ref[...]
        h = (xv * _rstd(xv) * g_ref[...]).astype(BF16)
        h_ref[...] = h
        q_ref[...] = _nn(h, wq_ref[...]).astype(BF16)
        qb = q_ref[...]
        kvv = kv_ref[...]
        for hd in range(N_MEM_HEADS):
            p = _xattn_probs(qb, kvv, hd)
            v = kvv[:, D + hd * MEM_HD:D + (hd + 1) * MEM_HD]
            om_ref[:, hd * MEM_HD:(hd + 1) * MEM_HD] = _nn(p.astype(BF16), v).astype(BF16)
        x2_ref[...] = xv + _nn(om_ref[...], wo_ref[...])

    return pl.pallas_call(
        body,
        name="xattn_fwd",
        grid=(T // tm,),
        in_specs=[
            pl.BlockSpec((tm, D), lambda i: (i, 0)),
            pl.BlockSpec((1, D), lambda i: (0, 0)),
            pl.BlockSpec((D, D), lambda i: (0, 0)),
            pl.BlockSpec((M, 2 * D), lambda i: (0, 0)),
            pl.BlockSpec((D, D), lambda i: (0, 0)),
        ],
        out_specs=[pl.BlockSpec((tm, D), lambda i: (i, 0))] * 4,
        out_shape=[jax.ShapeDtypeStruct((T, D), F32)] + [jax.ShapeDtypeStruct((T, D), BF16)] * 3,
        compiler_params=_cparams(("arbitrary",)),
    )(x1, g2, w_mq, kv, w_mo)


def _conv_taps(ext_ref, tm, back):
    if back:
        return ext_ref[pl.ds(6, tm), :], ext_ref[pl.ds(7, tm), :], ext_ref[pl.ds(8, tm), :]
    return ext_ref[pl.ds(0, tm), :], ext_ref[pl.ds(1, tm), :], ext_ref[pl.ds(2, tm), :]


def _ffn_fwd(x2, g3, w_up, conv_w, conv_b, w_down, tm):
    T, D = x2.shape
    fc = FF_CHUNK
    nj = D_FF // fc

    def body(x_ref, g_ref, wg_ref, wv_ref, cwg_ref, cwv_ref, cbg_ref, cbv_ref, wd_ref,
             x3_ref, h_ref, ug_ref, uv_ref, yg_ref, yv_ref, a_ref, acc_ref, carry_ref, ext_ref):
        i = pl.program_id(0)
        j = pl.program_id(1)

        @pl.when(j == 0)
        def _():
            xv = x_ref[...]
            h_ref[...] = (xv * _rstd(xv) * g_ref[...]).astype(BF16)
            acc_ref[...] = xv

        @pl.when(i == 0)
        def _():
            carry_ref[j] = jnp.zeros((2, 8, fc), F32)

        h = h_ref[...]
        halves = []
        for part, (w_ref, cw_ref, cb_ref, u_ref, y_ref) in enumerate(
                ((wg_ref, cwg_ref, cbg_ref, ug_ref, yg_ref), (wv_ref, cwv_ref, cbv_ref, uv_ref, yv_ref))):
            u = _nn(h, w_ref[...])
            u_ref[...] = u.astype(BF16)
            ext = ext_ref.at[part]
            ext[pl.ds(0, 8), :] = carry_ref[j, part]
            ext[pl.ds(8, tm), :] = u
            carry_ref[j, part] = u[tm - 8:, :]
            u2, u1, u0 = _conv_taps(ext, tm, True)
            cw = cw_ref[...]
            y = cb_ref[...] + cw[0:1] * u2 + cw[1:2] * u1 + cw[2:3] * u0
            y_ref[...] = y.astype(BF16)
            halves.append(y)
        gate, val = halves
        a = (gate * jax.nn.sigmoid(gate) * val).astype(BF16)
        a_ref[...] = a
        acc_ref[...] += _nn(a, wd_ref[...])

        @pl.when(j == nj - 1)
        def _():
            x3_ref[...] = acc_ref[...]

    return pl.pallas_call(
        body,
        name="ffn_fwd",
        grid=(T // tm, nj),
        in_specs=[
            pl.BlockSpec((tm, D), lambda i, j: (i, 0)),
            pl.BlockSpec((1, D), lambda i, j: (0, 0)),
            pl.BlockSpec((D, fc), lambda i, j: (0, j)),
            pl.BlockSpec((D, fc), lambda i, j: (0, nj + j)),
            pl.BlockSpec((3, fc), lambda i, j: (0, j)),
            pl.BlockSpec((3, fc), lambda i, j: (0, nj + j)),
            pl.BlockSpec((1, fc), lambda i, j: (0, j)),
            pl.BlockSpec((1, fc), lambda i, j: (0, nj + j)),
            pl.BlockSpec((fc, D), lambda i, j: (j, 0)),
        ],
        out_specs=[
            pl.BlockSpec((tm, D), lambda i, j: (i, 0)),
            pl.BlockSpec((tm, D), lambda i, j: (i, 0)),
        ] + [pl.BlockSpec((tm, fc), lambda i, j: (i, j))] * 5,
        out_shape=[
            jax.ShapeDtypeStruct((T, D), F32),
            jax.ShapeDtypeStruct((T, D), BF16),
        ] + [jax.ShapeDtypeStruct((T, D_FF), BF16)] * 5,
        scratch_shapes=[
            pltpu.VMEM((tm, D), F32),
            pltpu.VMEM((nj, 2, 8, fc), F32),
            pltpu.VMEM((2, tm + 8, fc), F32),
        ],
        compiler_params=_cparams(("arbitrary", "arbitrary")),
    )(x2, g3, w_up, w_up, conv_w, conv_w, conv_b, conv_b, w_down)


def _loss_head(x3, gfin, target, tm):
    T, D = x3.shape

    def body(x_ref, g_ref, t_ref, dx_ref, loss_ref, dg_ref):
        i = pl.program_id(0)

        @pl.when(i == 0)
        def _():
            loss_ref[...] = jnp.zeros_like(loss_ref)
            dg_ref[...] = jnp.zeros_like(dg_ref)

        xv = x_ref[...]
        g = g_ref[...]
        r = _rstd(xv)
        xhat = xv * r
        err = xhat * g - t_ref[...]
        part = jnp.sum(jnp.sum(err * err, axis=1, keepdims=True), axis=0, keepdims=True) * (0.5 / D)
        loss_ref[...] += jnp.broadcast_to(part, loss_ref.shape)
        dy = err * (1.0 / D)
        dg_ref[...] += jnp.sum(dy * xhat, axis=0, keepdims=True)
        dxhat = dy * g
        dx_ref[...] = r * (dxhat - xhat * jnp.mean(dxhat * xhat, axis=-1, keepdims=True))

    return pl.pallas_call(
        body,
        name="loss_head",
        grid=(T // tm,),
        in_specs=[
            pl.BlockSpec((tm, D), lambda i: (i, 0)),
            pl.BlockSpec((1, D), lambda i: (0, 0)),
            pl.BlockSpec((tm, D), lambda i: (i, 0)),
        ],
        out_specs=[
            pl.BlockSpec((tm, D), lambda i: (i, 0)),
            pl.BlockSpec((8, 128), lambda i: (0, 0)),
            pl.BlockSpec((1, D), lambda i: (0, 0)),
        ],
        out_shape=[
            jax.ShapeDtypeStruct((T, D), F32),
            jax.ShapeDtypeStruct((8, 128), F32),
            jax.ShapeDtypeStruct((1, D), F32),
        ],
        compiler_params=_cparams(("arbitrary",)),
    )(x3, gfin, target)


def _ffn_bwd(dx3, x2, g3, ug, uv, yg, yv, conv_w, w_down, w_up, tm):
    T, D = x2.shape
    fc = FF_CHUNK
    nj = D_FF // fc
    nt = T // tm

    def rev(i):
        return nt - 1 - i

    def body(dx3_ref, x_ref, g_ref, ug_ref, uv_ref, yg_ref, yv_ref, cwg_ref, cwv_ref,
             wd_ref, wug_ref, wuv_ref,
             dx2_ref, dug_ref, duv_ref, dg_ref, dcg_ref, dcv_ref,
             acc_ref, carry_ref, ext_ref):
        i = pl.program_id(0)
        j = pl.program_id(1)
        cols = pl.ds(pl.multiple_of(j * fc, fc), fc)

        @pl.when(j == 0)
        def _():
            acc_ref[...] = jnp.zeros_like(acc_ref)

        @pl.when((i == 0) & (j == 0))
        def _():
            dg_ref[...] = jnp.zeros_like(dg_ref)
            dcg_ref[...] = jnp.zeros_like(dcg_ref)
            dcv_ref[...] = jnp.zeros_like(dcv_ref)

        @pl.when(i == 0)
        def _():
            carry_ref[j] = jnp.zeros((2, 8, fc), F32)

        da = _nt(dx3_ref[...].astype(BF16), wd_ref[...])
        gate = yg_ref[...].astype(F32)
        val = yv_ref[...].astype(F32)
        sig = jax.nn.sigmoid(gate)
        silu = gate * sig
        dys = (da * val * (sig * (1.0 + gate * (1.0 - sig))), da * silu)
        for part, (dy, u_ref, cw_ref, du_ref, wu_ref, dc_ref) in enumerate(
                ((dys[0], ug_ref, cwg_ref, dug_ref, wug_ref, dcg_ref),
                 (dys[1], uv_ref, cwv_ref, duv_ref, wuv_ref, dcv_ref))):
            ext = ext_ref.at[part]
            ext[pl.ds(0, tm), :] = dy
            ext[pl.ds(tm, 8), :] = carry_ref[j, part]
            carry_ref[j, part] = dy[:8, :]
            d0, d1, d2 = _conv_taps(ext, tm, False)
            u = u_ref[...].astype(F32)
            upd = jnp.concatenate([
                jnp.sum(u * d2, axis=0, keepdims=True),
                jnp.sum(u * d1, axis=0, keepdims=True),
                jnp.sum(u * d0, axis=0, keepdims=True),
                jnp.sum(d0, axis=0, keepdims=True),
                jnp.zeros((4, fc), F32)], axis=0)
            dc_ref[:, cols] += upd
            cw = cw_ref[...]
            du = (cw[2:3] * d0 + cw[1:2] * d1 + cw[0:1] * d2).astype(BF16)
            du_ref[...] = du
            acc_ref[...] += _nt(du, wu_ref[...])

        @pl.when(j == nj - 1)
        def _():
            dx, dg = _norm_bwd(x_ref[...], g_ref[...], acc_ref[...])
            dx2_ref[...] = dx3_ref[...] + dx
            dg_ref[...] += dg

    return pl.pallas_call(
        body,
        name="ffn_bwd",
        grid=(nt, nj),
        in_specs=[
            pl.BlockSpec((tm, D), lambda i, j: (rev(i), 0)),
            pl.BlockSpec((tm, D), lambda i, j: (rev(i), 0)),
            pl.BlockSpec((1, D), lambda i, j: (0, 0)),
            pl.BlockSpec((tm, fc), lambda i, j: (rev(i), j)),
            pl.BlockSpec((tm, fc), lambda i, j: (rev(i), j)),
            pl.BlockSpec((tm, fc), lambda i, j: (rev(i), j)),
            pl.BlockSpec((tm, fc), lambda i, j: (rev(i), j)),
            pl.BlockSpec((3, fc), lambda i, j: (0, j)),
            pl.BlockSpec((3, fc), lambda i, j: (0, nj + j)),
            pl.BlockSpec((fc, D), lambda i, j: (j, 0)),
            pl.BlockSpec((D, fc), lambda i, j: (0, j)),
            pl.BlockSpec((D, fc), lambda i, j: (0, nj + j)),
        ],
        out_specs=[
            pl.BlockSpec((tm, D), lambda i, j: (rev(i), 0)),
            pl.BlockSpec((tm, fc), lambda i, j: (rev(i), j)),
            pl.BlockSpec((tm, fc), lambda i, j: (rev(i), j)),
            pl.BlockSpec((1, D), lambda i, j: (0, 0)),
            pl.BlockSpec((8, D_FF), lambda i, j: (0, 0)),
            pl.BlockSpec((8, D_FF), lambda i, j: (0, 0)),
        ],
        out_shape=[
            jax.ShapeDtypeStruct((T, D), F32),
            jax.ShapeDtypeStruct((T, D_FF), BF16),
            jax.ShapeDtypeStruct((T, D_FF), BF16),
            jax.ShapeDtypeStruct((1, D), F32),
            jax.ShapeDtypeStruct((8, D_FF), F32),
            jax.ShapeDtypeStruct((8, D_FF), F32),
        ],
        scratch_shapes=[
            pltpu.VMEM((tm, D), F32),
            pltpu.VMEM((nj, 2, 8, fc), F32),
            pltpu.VMEM((2, tm + 8, fc), F32),
        ],
        compiler_params=_cparams(("arbitrary", "arbitrary")),
    )(dx3, x2, g3, ug, uv, yg, yv, conv_w, conv_w, w_down, w_up, w_up)


def _xattn_bwd(dx2, x1, g2, qb, kv, w_mo, w_mq, tm):
    T, D = x1.shape
    M = kv.shape[0]

    def body(dx2_ref, x_ref, g_ref, q_ref, kv_ref, wo_ref, wq_ref, dx1_ref, dq_ref, dkv_ref, dg_ref):
        i = pl.program_id(0)

        @pl.when(i == 0)
        def _():
            dkv_ref[...] = jnp.zeros_like(dkv_ref)
            dg_ref[...] = jnp.zeros_like(dg_ref)

        dxv = dx2_ref[...]
        dom = _nt(dxv.astype(BF16), wo_ref[...]).astype(BF16)
        qb_ = q_ref[...]
        kvv = kv_ref[...]
        for hd in range(N_MEM_HEADS):
            sl = slice(hd * MEM_HD, (hd + 1) * MEM_HD)
            vsl = slice(D + hd * MEM_HD, D + (hd + 1) * MEM_HD)
            p = _xattn_probs(qb_, kvv, hd)
            dp = _nt(dom[:, sl], kvv[:, vsl])
            ds = (p * (dp - jnp.sum(p * dp, axis=1, keepdims=True)) * (MEM_HD ** -0.5)).astype(BF16)
            dq_ref[:, sl] = _nn(ds, kvv[:, sl]).astype(BF16)
            dkv_ref[:, sl] += _tn(ds, qb_[:, sl])
            dkv_ref[:, vsl] += _tn(p.astype(BF16), dom[:, sl])
        dh = _nt(dq_ref[...], wq_ref[...])
        dx, dg = _norm_bwd(x_ref[...], g_ref[...], dh)
        dx1_ref[...] = dxv + dx
        dg_ref[...] += dg

    return pl.pallas_call(
        body,
        name="xattn_bwd",
        grid=(T // tm,),
        in_specs=[
            pl.BlockSpec((tm, D), lambda i: (i, 0)),
            pl.BlockSpec((tm, D), lambda i: (i, 0)),
            pl.BlockSpec((1, D), lambda i: (0, 0)),
            pl.BlockSpec((tm, D), lambda i: (i, 0)),
            pl.BlockSpec((M, 2 * D), lambda i: (0, 0)),
            pl.BlockSpec((D, D), lambda i: (0, 0)),
            pl.BlockSpec((D, D), lambda i: (0, 0)),
        ],
        out_specs=[
            pl.BlockSpec((tm, D), lambda i: (i, 0)),
            pl.BlockSpec((tm, D), lambda i: (i, 0)),
            pl.BlockSpec((M, 2 * D), lambda i: (0, 0)),
            pl.BlockSpec((1, D), lambda i: (0, 0)),
        ],
        out_shape=[
            jax.ShapeDtypeStruct((T, D), F32),
            jax.ShapeDtypeStruct((T, D), BF16),
            jax.ShapeDtypeStruct((M, 2 * D), F32),
            jax.ShapeDtypeStruct((1, D), F32),
        ],
        compiler_params=_cparams(("arbitrary",)),
    )(dx2, x1, g2, qb, kv, w_mo, w_mq)


def _mem_kv_bwd(mem, gm, mb, dkv, w_mkv):
    M, D = mem.shape
    N = dkv.shape[1]

    def body(mem_ref, g_ref, m_ref, dkv_ref, w_ref, dw_ref, dg_ref):
        dkvb = dkv_ref[...].astype(BF16)
        for n0 in range(0, N, 512):
            dw_ref[:, n0:n0 + 512] = _tn(m_ref[...], dkvb[:, n0:n0 + 512]).astype(BF16)
        dm = _nt(dkvb, w_ref[...])
        mv = mem_ref[...]
        dg_ref[...] = jnp.sum(dm * (mv * _rstd(mv)), axis=0, keepdims=True)

    return pl.pallas_call(
        body,
        name="mem_kv_bwd",
        out_shape=[jax.ShapeDtypeStruct((D, N), BF16), jax.ShapeDtypeStruct((1, D), F32)],
        compiler_params=_cparams(),
    )(mem, gm, mb, dkv, w_mkv)


def _post_attn_bwd(dx1, fox_o, sb_o, gf, gs, w_out, tm):
    T, D = dx1.shape

    def body(dx_ref, f_ref, s_ref, gf_ref, gs_ref, w_ref, df_ref, ds_ref, dgf_ref, dgs_ref):
        i = pl.program_id(0)

        @pl.when(i == 0)
        def _():
            dgf_ref[...] = jnp.zeros_like(dgf_ref)
            dgs_ref[...] = jnp.zeros_like(dgs_ref)

        dmix = _nt(dx_ref[...].astype(BF16), w_ref[...])
        d, dg = _norm_bwd(f_ref[...], gf_ref[...], dmix[:, :FOX_W])
        df_ref[...] = d
        dgf_ref[...] += dg
        d, dg = _norm_bwd(s_ref[...], gs_ref[...], dmix[:, FOX_W:])
        ds_ref[...] = d
        dgs_ref[...] += dg

    return pl.pallas_call(
        body,
        name="post_attn_bwd",
        grid=(T // tm,),
        in_specs=[
            pl.BlockSpec((tm, D), lambda i: (i, 0)),
            pl.BlockSpec((tm, FOX_W), lambda i: (i, 0)),
            pl.BlockSpec((tm, FOX_W), lambda i: (i, 0)),
            pl.BlockSpec((1, FOX_W), lambda i: (0, 0)),
            pl.BlockSpec((1, FOX_W), lambda i: (0, 0)),
            pl.BlockSpec((D, D), lambda i: (0, 0)),
        ],
        out_specs=[
            pl.BlockSpec((tm, FOX_W), lambda i: (i, 0)),
            pl.BlockSpec((tm, FOX_W), lambda i: (i, 0)),
            pl.BlockSpec((1, FOX_W), lambda i: (0, 0)),
            pl.BlockSpec((1, FOX_W), lambda i: (0, 0)),
        ],
        out_shape=[
            jax.ShapeDtypeStruct((T, FOX_W), F32),
            jax.ShapeDtypeStruct((T, FOX_W), F32),
            jax.ShapeDtypeStruct((1, FOX_W), F32),
            jax.ShapeDtypeStruct((1, FOX_W), F32),
        ],
        compiler_params=_cparams(("arbitrary",)),
    )(dx1, fox_o, sb_o, gf, gs, w_out)


def _sb_bwd(proj, ltot, live, d_o, tq):
    T = proj.shape[0]
    nq = T // tq

    def body(q_ref, k_ref, v_ref, lt_ref, live_ref, do_ref, dq_ref, dk_ref, dv_ref,
             qh_s, doh_s, lt_s, z_s, da_s, ab_s, dzb_s, run_s, runw_s, dq_s):
        i = pl.program_id(1)

        @pl.when(i == 0)
        def _():
            dk_ref[...] = jnp.zeros_like(dk_ref)
            dv_ref[...] = jnp.zeros_like(dv_ref)

        lane = lax.broadcasted_iota(jnp.int32, (1, 128), 1)
        row = lax.broadcasted_iota(jnp.int32, (tq, tq), 0)
        col = lax.broadcasted_iota(jnp.int32, (tq, tq), 1)
        strict = col < row
        upto = jnp.where(row <= col, 1.0, 0.0).astype(BF16)
        before = jnp.where(row < col, 1.0, 0.0).astype(BF16)
        q = q_ref[...]
        dov = do_ref[...]
        for hh in range(2):
            qh, hmask = _head_q(q, hh, lane)
            qh_s[hh] = -qh
            doh_s[hh] = jnp.where(hmask, dov, 0.0).astype(BF16)
            lt_s[hh] = jnp.broadcast_to(_lanes_to_rows(lt_ref[hh], row == col), (tq, 128))
        run_s[...] = jnp.zeros_like(run_s)
        runw_s[...] = jnp.zeros_like(runw_s)
        dq_s[...] = jnp.zeros_like(dq_s)

        at = lax.broadcasted_iota(jnp.int32, (1, nq), 1)
        count = jnp.sum(jnp.where(at == i, live_ref[0], 0.0), axis=1, keepdims=True)[0, 0].astype(jnp.int32)
        n_live = jnp.clip(count, 1, i + 1)
        oldest = i + 1 - n_live

        def rows(t):
            return pl.ds(pl.multiple_of((oldest + t) * tq, tq), tq)

        def stage_a(t, slot):
            k = k_ref[rows(t), :]
            v = v_ref[rows(t), :]
            for hh in range(2):
                z_s[slot, hh] = _nt(qh_s[hh], k)
                da_s[slot, hh] = _nt(doh_s[hh], v)

        def stage_b(t, slot, diag):
            for hh in range(2):
                L, g = _sb_logs(z_s[slot, hh], strict if diag else None)
                upto_s = _split_dot(L, upto, SB_SUM_TERMS)
                run = run_s[hh]
                arg = (g + _lanes2(lt_s[hh] - run)) - upto_s
                if diag:
                    arg = jnp.where(strict, arg, NEG)
                a = jnp.exp(arg)
                w = a * da_s[slot, hh]
                w_before = _split_dot(w, before, SB_SUM_TERMS)
                run_w = runw_s[hh]
                d_keep = w_before + _lanes2(run_w)
                beta = jnp.exp(g)
                ndz = beta * (w + d_keep) - w
                if diag:
                    ndz = jnp.where(strict, ndz, 0.0)
                dzb_s[hh] = ndz.astype(BF16)
                ab_s[hh] = a.astype(BF16)
                run_s[hh] = run + jnp.broadcast_to(upto_s[:, tq - 1:tq], (tq, 128))
                runw_s[hh] = run_w + jnp.broadcast_to(w_before[:, tq - 1:tq] + w[:, tq - 1:tq], (tq, 128))

        def stage_c(t):
            k = k_ref[rows(t), :]
            dk_blk = None
            dv_blk = None
            for hh in range(2):
                dzb = dzb_s[hh]
                dq_s[hh] += _nn(dzb, k)
                dk_h = _tn(dzb, qh_s[hh])
                dv_h = _tn(ab_s[hh], doh_s[hh])
                dk_blk = dk_h if dk_blk is None else dk_blk + dk_h
                dv_blk = dv_h if dv_blk is None else dv_blk + dv_h
            dk_ref[rows(t), :] += dk_blk
            dv_ref[rows(t), :] += dv_blk

        _pipeline3(n_live, stage_a, stage_b, stage_c, True)
        dq_ref[...] = (jnp.where(lane < HEAD_DIM, dq_s[0], dq_s[1]) * -(HEAD_DIM ** -0.5)).astype(BF16)

    return pl.pallas_call(
        body,
        name="sb_bwd",
        grid=(4, nq),
        in_specs=[
            pl.BlockSpec((tq, 128), lambda p, i: (i, 12 + p)),
            pl.BlockSpec((T, 128), lambda p, i: (0, 16 + p)),
            pl.BlockSpec((T, 128), lambda p, i: (0, 20 + p)),
            pl.BlockSpec((2, 1, tq), lambda p, i: (p, 0, i)),
            pl.BlockSpec((1, 1, nq), lambda p, i: (p, 0, 0)),
            pl.BlockSpec((tq, 128), lambda p, i: (i, p)),
        ],
        out_specs=[
            pl.BlockSpec((tq, 128), lambda p, i: (i, p)),
            pl.BlockSpec((T, 128), lambda p, i: (0, p)),
            pl.BlockSpec((T, 128), lambda p, i: (0, p)),
        ],
        out_shape=[
            jax.ShapeDtypeStruct((T, FOX_W), BF16),
            jax.ShapeDtypeStruct((T, FOX_W), F32),
            jax.ShapeDtypeStruct((T, FOX_W), F32),
        ],
        scratch_shapes=[
            pltpu.VMEM((2, tq, 128), BF16),
            pltpu.VMEM((2, tq, 128), BF16),
            pltpu.VMEM((2, tq, 128), F32),
            pltpu.VMEM((2, 2, tq, tq), F32),
            pltpu.VMEM((2, 2, tq, tq), F32),
            pltpu.VMEM((2, tq, tq), BF16),
            pltpu.VMEM((2, tq, tq), BF16),
            pltpu.VMEM((2, tq, 128), F32),
            pltpu.VMEM((2, tq, 128), F32),
            pltpu.VMEM((2, tq, 128), F32),
        ],
        compiler_params=_cparams(("arbitrary", "arbitrary")),
    )(proj, proj, proj, ltot, live, d_o)


def _fox_bwd(proj, c_col, c_row, c_ends, lse, d_o, o, tq, scatter=()):
    T = proj.shape[0]
    nq = T // tq
    ns = len(scatter)

    def body(*refs):
        q_ref, k_ref, v_ref, cq_ref, ck_ref, cke_ref, lse_ref, do_ref, o_ref = refs[:9]
        dq_ref, dk_ref, dv_ref, dck_ref, dcq_ref = refs[9 + ns:14 + ns]
        (qh_s, doh_s, delta_s, shift_s, z_s, dp_s, pb_s, dsb_s, rs_s, dq_s,
         kn_s) = refs[14 + 2 * ns:25 + 2 * ns]
        i = pl.program_id(1)
        if ns:
            pair = pl.program_id(0)
            finish = _ride_along(_Scatter(refs[9:9 + ns], refs[14 + ns:14 + 2 * ns], *refs[25 + 2 * ns:]),
                                 (pair == 0) & (i == 0), None, (pair == 3) & (i == nq - 1))
        lane = lax.broadcasted_iota(jnp.int32, (1, 128), 1)

        @pl.when(i == 0)
        def _():
            dk_ref[...] = jnp.zeros_like(dk_ref)
            dv_ref[...] = jnp.zeros_like(dv_ref)
            dck_ref[...] = jnp.zeros_like(dck_ref)
            _fox_key_norms(k_ref, kn_s, lane)

        row = lax.broadcasted_iota(jnp.int32, (tq, tq), 0)
        col = lax.broadcasted_iota(jnp.int32, (tq, tq), 1)
        q = q_ref[...]
        dov = do_ref[...]
        ov = o_ref[...]
        for hh in range(2):
            qh, hmask = _head_q(q, hh, lane)
            dohb = jnp.where(hmask, dov, 0.0).astype(BF16)
            qh_s[hh] = qh
            doh_s[hh] = dohb
            delta_s[hh] = jnp.broadcast_to(jnp.sum(dohb.astype(F32) * ov, axis=1, keepdims=True), (tq, tq))
            shift_s[hh] = jnp.broadcast_to(_lanes_to_rows(cq_ref[hh] - lse_ref[hh], row == col), (tq, tq))
        rs_s[...] = jnp.zeros_like(rs_s)
        dq_s[...] = jnp.zeros_like(dq_s)

        def rows(t):
            return pl.ds(pl.multiple_of((i - t) * tq, tq), tq)

        def stage_a(t, slot):
            k = k_ref[rows(t), :]
            v = v_ref[rows(t), :]
            for hh in range(2):
                z_s[slot, hh] = _nt(qh_s[hh], k)
                dp_s[slot, hh] = _nt(doh_s[hh], v)

        def stage_b(t, slot, diag):
            for hh in range(2):
                s = z_s[slot, hh] + shift_s[hh] - ck_ref[hh, :, rows(t)]
                if diag:
                    s = jnp.where(col <= row, s, NEG)
                p = jnp.exp(s)
                ds = p * (dp_s[slot, hh] - delta_s[hh])
                pb_s[hh] = p.astype(BF16)
                dsb_s[hh] = ds.astype(BF16)
                dck_ref[hh, :, rows(t)] += jnp.sum(ds, axis=0, keepdims=True)
                rs_s[hh] += jnp.sum(ds, axis=1, keepdims=True)

        def stage_c(t):
            k = k_ref[rows(t), :]
            dk_blk = None
            dv_blk = None
            for hh in range(2):
                dsb = dsb_s[hh]
                dq_s[hh] += _nn(dsb, k)
                dk_h = _tn(dsb, qh_s[hh])
                dv_h = _tn(pb_s[hh], doh_s[hh])
                dk_blk = dk_h if dk_blk is None else dk_blk + dk_h
                dv_blk = dv_h if dv_blk is None else dv_blk + dv_h
            dk_ref[rows(t), :] += dk_blk
            dv_ref[rows(t), :] += dv_blk

        _pipeline3(_fox_live_blocks(i, qh_s, kn_s, cq_ref, cke_ref), stage_a, stage_b, stage_c, False)
        dcq_ref[0] = _rows_to_lanes(rs_s[0], row == col)
        dcq_ref[1] = _rows_to_lanes(rs_s[1], row == col)
        dq_ref[...] = (jnp.where(lane < HEAD_DIM, dq_s[0], dq_s[1]) * (HEAD_DIM ** -0.5)).astype(BF16)
        if ns:
            finish()

    res = pl.pallas_call(
        body,
        name="fox_bwd",
        grid=(4, nq),
        in_specs=[
            pl.BlockSpec((tq, 128), lambda p, i: (i, p)),
            pl.BlockSpec((T, 128), lambda p, i: (0, 4 + p)),
            pl.BlockSpec((T, 128), lambda p, i: (0, 8 + p)),
            pl.BlockSpec((2, 1, tq), lambda p, i: (p, 0, i)),
            pl.BlockSpec((2, 1, T), lambda p, i: (p, 0, 0)),
            pl.BlockSpec((2, 1, nq), lambda p, i: (p, 0, 0)),
            pl.BlockSpec((2, 1, tq), lambda p, i: (p, 0, i)),
            pl.BlockSpec((tq, 128), lambda p, i: (i, p)),
            pl.BlockSpec((tq, 128), lambda p, i: (i, p)),
        ] + [_ANY] * ns,
        out_specs=[
            pl.BlockSpec((tq, 128), lambda p, i: (i, p)),
            pl.BlockSpec((T, 128), lambda p, i: (0, p)),
            pl.BlockSpec((T, 128), lambda p, i: (0, p)),
            pl.BlockSpec((2, 1, T), lambda p, i: (p, 0, 0)),
            pl.BlockSpec((2, 1, tq), lambda p, i: (p, 0, i)),
        ] + [_ANY] * ns,
        out_shape=[
            jax.ShapeDtypeStruct((T, FOX_W), BF16),
            jax.ShapeDtypeStruct((T, FOX_W), F32),
            jax.ShapeDtypeStruct((T, FOX_W), F32),
            jax.ShapeDtypeStruct((N_FOX, 1, T), F32),
            jax.ShapeDtypeStruct((N_FOX, 1, T), F32),
        ] + [jax.ShapeDtypeStruct(b.shape, b.dtype) for b in scatter],
        scratch_shapes=[
            pltpu.VMEM((2, tq, 128), BF16),
            pltpu.VMEM((2, tq, 128), BF16),
            pltpu.VMEM((2, tq, tq), F32),
            pltpu.VMEM((2, tq, tq), F32),
            pltpu.VMEM((2, 2, tq, tq), F32),
            pltpu.VMEM((2, 2, tq, tq), F32),
            pltpu.VMEM((2, tq, tq), BF16),
            pltpu.VMEM((2, tq, tq), BF16),
            pltpu.VMEM((2, tq, 128), F32),
            pltpu.VMEM((2, tq, 128), F32),
            pltpu.VMEM((2, 8, 128), F32),
        ] + (_comm_sems(ns) if ns else []),
        compiler_params=_cparams(("arbitrary", "arbitrary")),
    )(proj, proj, proj, c_col, c_row, c_ends, lse, d_o, o, *scatter)
    res = list(res)
    return (*res[:5], res[5:])


def _forget_bwd(dcq, dck, xf, h1, tc):
    H, T = xf.shape
    D = h1.shape[1]
    nc = T // tc

    def body(dcq_ref, dck_ref, xf_ref, h_ref, dxf_ref, db_ref, dwf_ref):
        row = lax.broadcasted_iota(jnp.int32, (tc, tc), 0)
        col = lax.broadcasted_iota(jnp.int32, (tc, tc), 1)
        from_here = jnp.where(row >= col, 1.0, 0.0).astype(BF16)

        def chunk(n, carry):
            run, db, dwf = carry
            cs = pl.multiple_of((nc - 1 - n) * tc, tc)
            dc = dcq_ref[:, pl.ds(cs, tc)] - dck_ref[:, pl.ds(cs, tc)]
            dlogf = _split_dot(dc, from_here, 3) + run
            xfv = xf_ref[:, pl.ds(cs, tc)]
            dxf = dlogf * jax.nn.sigmoid(-xfv)
            dxf_ref[:, pl.ds(cs, tc)] = dxf
            dwf = dwf + _nn(dxf.astype(BF16), h_ref[pl.ds(cs, tc), :])
            return dlogf[:, 0:1], db + jnp.sum(dxf, axis=1, keepdims=True), dwf

        zero = jnp.zeros((H, 1), F32)
        _, db, dwf = lax.fori_loop(0, nc, chunk, (zero, zero, jnp.zeros((H, D), F32)))
        db_ref[...] = db
        dwf_ref[...] = dwf

    return pl.pallas_call(
        body,
        name="forget_bwd",
        out_shape=[jax.ShapeDtypeStruct((H, T), F32), jax.ShapeDtypeStruct((H, 1), F32),
                   jax.ShapeDtypeStruct((H, D), F32)],
        compiler_params=_cparams(),
    )(dcq, dck, xf, h1)


def _inproj_bwd(pieces, dxf_t, w_in, w_f_t, x, g1, dx1, tm, scatter=()):
    T, D = x.shape
    N = w_in.shape[1]
    ns = len(scatter)
    nt = T // tm
    npc = len(pieces)

    def body(*refs):
        pc_refs = refs[:npc]
        dxf_ref, w_ref, wf_ref, x_ref, g_ref, dx1_ref = refs[npc:npc + 6]
        base = npc + 6
        dx_ref, dg_ref = refs[base + ns:base + 2 + ns]
        i = pl.program_id(0)
        if ns:
            exchange = _Scatter(refs[base:base + ns], refs[base + 2 + ns:base + 2 + 2 * ns],
                                *refs[base + 2 + 2 * ns:])

            @pl.when(i == 0)
            def _():
                exchange.start()

        @pl.when(i == 0)
        def _():
            dg_ref[...] = jnp.zeros_like(dg_ref)

        dh = _nn(dxf_ref[...], wf_ref[...].astype(F32))
        for k, pc_ref in enumerate(pc_refs):
            dh = dh + _nt(pc_ref[...].astype(BF16), w_ref[:, k * FOX_W:(k + 1) * FOX_W])
        dx, dg = _norm_bwd(x_ref[...], g_ref[...], dh)
        dx_ref[...] = dx1_ref[...] + dx
        dg_ref[...] += dg
        if ns:
            @pl.when(i == nt - 1)
            def _():
                exchange.finish()

    res = pl.pallas_call(
        body,
        name="inproj_bwd",
        grid=(nt,),
        in_specs=[pl.BlockSpec((tm, FOX_W), lambda i: (i, 0))] * npc + [
            pl.BlockSpec((tm, N_FOX), lambda i: (i, 0)),
            pl.BlockSpec((D, N), lambda i: (0, 0)),
            pl.BlockSpec((N_FOX, D), lambda i: (0, 0)),
            pl.BlockSpec((tm, D), lambda i: (i, 0)),
            pl.BlockSpec((1, D), lambda i: (0, 0)),
            pl.BlockSpec((tm, D), lambda i: (i, 0)),
        ] + [_ANY] * ns,
        out_specs=[
            pl.BlockSpec((tm, D), lambda i: (i, 0)),
            pl.BlockSpec((1, D), lambda i: (0, 0)),
        ] + [_ANY] * ns,
        out_shape=[jax.ShapeDtypeStruct((T, D), F32), jax.ShapeDtypeStruct((1, D), F32)]
        + [jax.ShapeDtypeStruct(b.shape, b.dtype) for b in scatter],
        scratch_shapes=_comm_sems(ns) if ns else [],
        compiler_params=_cparams(("arbitrary",)),
    )(*pieces, dxf_t, w_in, w_f_t, x, g1, dx1, *scatter)
    res = list(res)
    return res[0], res[1], res[2:]


def _dw_in(h1, pieces, name):
    T, K = h1.shape
    bt = min(T, 512)
    nt = T // bt
    npc = len(pieces)

    def body(*refs):
        a_ref = refs[0]
        pc_refs = refs[1:1 + npc]
        o_ref, acc_ref = refs[1 + npc:]
        t = pl.program_id(0)

        @pl.when(t == 0)
        def _():
            acc_ref[...] = jnp.zeros_like(acc_ref)

        a = a_ref[...]
        for k, pc_ref in enumerate(pc_refs):
            acc_ref[:, k * FOX_W:(k + 1) * FOX_W] += _tn(a, pc_ref[...].astype(BF16))

        @pl.when(t == nt - 1)
        def _():
            o_ref[...] = acc_ref[...].astype(BF16)

    return pl.pallas_call(
        body,
        name=name,
        grid=(nt,),
        in_specs=[pl.BlockSpec((bt, K), lambda t: (t, 0))] + [pl.BlockSpec((bt, FOX_W), lambda t: (t, 0))] * npc,
        out_specs=pl.BlockSpec((K, npc * FOX_W), lambda t: (0, 0)),
        out_shape=jax.ShapeDtypeStruct((K, npc * FOX_W), BF16),
        scratch_shapes=[pltpu.VMEM((K, npc * FOX_W), F32)],
        compiler_params=_cparams(("arbitrary",)),
    )(h1, *pieces)


def _matmul_tn(a, b, name, cast_b=False):
    T, K = a.shape
    N = b.shape[1]
    bt = min(T, 512)
    bk = _tile_div(K, 1536)
    bn = _tile_div(N, 1536)
    nt = T // bt

    def body(a_ref, b_ref, o_ref, acc_ref):
        t = pl.program_id(2)

        @pl.when(t == 0)
        def _():
            acc_ref[...] = jnp.zeros_like(acc_ref)

        bv = b_ref[...]
        if cast_b:
            bv = bv.astype(BF16)
        acc_ref[...] += _tn(a_ref[...], bv)

        @pl.when(t == nt - 1)
        def _():
            o_ref[...] = acc_ref[...].astype(BF16)

    return pl.pallas_call(
        body,
        name=name,
        grid=(K // bk, N // bn, nt),
        in_specs=[
            pl.BlockSpec((bt, bk), lambda k, n, t: (t, k)),
            pl.BlockSpec((bt, bn), lambda k, n, t: (t, n)),
        ],
        out_specs=pl.BlockSpec((bk, bn), lambda k, n, t: (k, n)),
        out_shape=jax.ShapeDtypeStruct((K, N), BF16),
        scratch_shapes=[pltpu.VMEM((bk, bn), F32)],
        compiler_params=_cparams(("arbitrary", "arbitrary", "arbitrary")),
    )(a, b)


def _local_step(x, mem, target, p, tm, tq, late=None):
    T, D = x.shape
    w_in = p["w_in"]
    w_qkv = w_in[:, :QKV_W]
    w_f_t = w_in[:, QKV_W:].T
    b_f = p["b_forget"].reshape(N_FOX, 1)

    proj, h1, xf, c = _inproj_fwd(x, p["attn_norm_g"], w_qkv, w_f_t, b_f, tm)
    c_col = c.reshape(N_FOX, 1, T)
    c_row = c.reshape(N_FOX, 1, T)
    c_ends = c[:, tq - 1::tq].reshape(N_FOX, 1, T // tq)
    fox_o, lse, gathered = _fox_fwd(proj, c_col, c_row, c_ends, tq, gather=[late[n] for n in _LATE] if late else ())
    if late:
        p = dict(p, **{n: _gathered_full(n, gv) for n, gv in zip(_LATE, gathered)})
    sb_o, sb_ltot, sb_live = _sb_fwd(proj, tq)
    x1, mixed = _post_attn_fwd(fox_o, sb_o, p["fox_out_g"], p["sb_out_g"], p["w_out"], x, tm)
    mb, kv = _mem_kv_fwd(mem, p["mem_norm_g"], p["w_mkv"])
    x2, h2, qb, om = _xattn_fwd(x1, p["xattn_norm_g"], p["w_mq"], kv, p["w_mo"], tm)
    tf = 2 * tm if T % (2 * tm) == 0 else tm
    x3, h3, ug, uv, yg, yv, a = _ffn_fwd(
        x2, p["ffn_norm_g"], p["w_up"], p["conv_w"], p["conv_b"], p["w_down"], tf)
    dx3, loss_blk, d_final_g = _loss_head(x3, p["final_norm_g"], target, tm)

    g = {"final_norm_g": d_final_g}
    dx2, du_g, du_v, g["ffn_norm_g"], dc_g, dc_v = _ffn_bwd(
        dx3, x2, p["ffn_norm_g"], ug, uv, yg, yv, p["conv_w"], p["w_down"], p["w_up"], tf)
    g["w_down"] = _matmul_tn(a, dx3, "dw_down", cast_b=True)
    g["w_up"] = jnp.concatenate([_matmul_tn(h3, du_g, "dw_up_gate"), _matmul_tn(h3, du_v, "dw_up_val")], axis=1)
    dconv = jnp.concatenate([dc_g, dc_v], axis=1)
    g["conv_w"] = dconv[0:3]
    g["conv_b"] = dconv[3:4]
    dx1, dq_m, dkv, g["xattn_norm_g"] = _xattn_bwd(dx2, x1, p["xattn_norm_g"], qb, kv, p["w_mo"], p["w_mq"], tm)
    g["w_mo"] = _matmul_tn(om, dx2, "dw_mo", cast_b=True)
    g["w_mq"] = _matmul_tn(h2, dq_m, "dw_mq")
    g["w_mkv"], g["mem_norm_g"] = _mem_kv_bwd(mem, p["mem_norm_g"], mb, dkv, p["w_mkv"])
    d_fox, d_sb, g["fox_out_g"], g["sb_out_g"] = _post_attn_bwd(
        dx1, fox_o, sb_o, p["fox_out_g"], p["sb_out_g"], p["w_out"], tm)
    g["w_out"] = _matmul_tn(mixed, dx1, "dw_out", cast_b=True)
    dq_s, dk_s, dv_s = _sb_bwd(proj, sb_ltot, sb_live, d_sb, tq)
    dq_f, dk_f, dv_f, dck, dcq, parts = _fox_bwd(
        proj, c_col, c_row, c_ends, lse, d_fox, fox_o, tq,
        scatter=[_grad_blocks(n, g[n]) for n in _LATE] if late else ())
    if late:
        g["parts"] = dict(zip(_LATE, parts))
    dxf, db, dwf_t = _forget_bwd(dcq.reshape(N_FOX, T), dck.reshape(N_FOX, T), xf, h1, min(T, 512))
    g["b_forget"] = db.reshape(1, N_FOX)
    pieces = [dq_f, dk_f, dv_f, dq_s, dk_s, dv_s]
    g["w_in"] = jnp.concatenate([_dw_in(h1, pieces, "dw_in"), dwf_t.T.astype(BF16)], axis=1)
    grad_x, g["attn_norm_g"], parts = _inproj_bwd(
        pieces, dxf.T, w_in, w_f_t, x, p["attn_norm_g"], dx1, tm,
        scatter=[_grad_blocks("w_in", g["w_in"])] if late else ())
    if late:
        (g["parts"]["w_in"],) = parts
    return loss_blk, grad_x, g


def _mesh_pos():
    return lax.axis_index("x"), lax.axis_index("y"), lax.axis_index("c")


def _flip(pos, k):
    return tuple(1 - v if (k >> b) & 1 else v for v, b in zip(pos, (2, 1, 0)))


def _slot(pos):
    return 4 * pos[0] + 2 * pos[1] + pos[2]


_CHIPS = (4, 2, 6)


def _comm_sems(n):
    return [pltpu.SemaphoreType.DMA((7 * n,)), pltpu.SemaphoreType.DMA((7 * n,)), pltpu.SemaphoreType.DMA((n,))]


class _Gather:
    def __init__(self, ins, outs, send_sems, recv_sems, local_sems):
        self.ins, self.outs, self.n = ins, outs, len(ins)
        self.send_sems, self.recv_sems, self.local_sems = send_sems, recv_sems, local_sems
        self.me = _mesh_pos()
        self.sibling = _flip(self.me, 1)

    def _copy(self, a, kk, block, to, src=None):
        rows = self.outs[a].at[_slot(block)]
        return pltpu.make_async_remote_copy(
            src_ref=rows if src is None else src, dst_ref=rows,
            send_sem=self.send_sems.at[7 * a + kk], recv_sem=self.recv_sems.at[7 * a + kk],
            device_id=to, device_id_type=MESH)

    def _mine(self):
        return [pltpu.make_async_copy(self.ins[a], self.outs[a].at[_slot(self.me)], self.local_sems.at[a])
                for a in range(self.n)]

    def _first(self):
        out = []
        for a in range(self.n):
            out.append(self._copy(a, 0, self.me, self.sibling, src=self.ins[a]))
            out += [self._copy(a, 1 + j, self.me, _flip(self.me, k), src=self.ins[a]) for j, k in enumerate(_CHIPS)]
        return out

    def _passed(self):
        return [self._copy(a, 4 + j, _flip(self.me, k), self.sibling)
                for j, k in enumerate(_CHIPS) for a in range(self.n)]

    def start(self):
        for cp in self._mine() + self._first():
            cp.start()

    def forward(self):
        for j, k in enumerate(_CHIPS):
            for a in range(self.n):
                self._copy(a, 1 + j, _flip(self.me, k), self.me).wait_recv()
                self._copy(a, 4 + j, _flip(self.me, k), self.sibling).start()

    def finish(self):
        for a in range(self.n):
            self._copy(a, 0, self.sibling, self.me).wait_recv()
            for j, k in enumerate(_CHIPS):
                self._copy(a, 4 + j, _flip(self.sibling, k), self.me).wait_recv()
        for cp in self._first() + self._passed():
            cp.wait_send()
        for cp in self._mine():
            cp.wait()


class _Scatter:
    def __init__(self, ins, outs, send_sems, recv_sems, local_sems):
        self.ins, self.outs, self.n = ins, outs, len(ins)
        self.send_sems, self.recv_sems, self.local_sems = send_sems, recv_sems, local_sems
        self.me = _mesh_pos()

    def _copy(self, a, k, landed=False):
        peer = _flip(self.me, k)
        return pltpu.make_async_remote_copy(
            src_ref=self.ins[a].at[_slot(peer)], dst_ref=self.outs[a].at[_slot(peer if landed else self.me)],
            send_sem=self.send_sems.at[7 * a + k - 1], recv_sem=self.recv_sems.at[7 * a + k - 1],
            device_id=peer, device_id_type=MESH)

    def _mine(self):
        s = _slot(self.me)
        return [pltpu.make_async_copy(self.ins[a].at[s], self.outs[a].at[s], self.local_sems.at[a])
                for a in range(self.n)]

    def start(self):
        for cp in self._mine() + [self._copy(a, k) for k in range(1, 8) for a in range(self.n)]:
            cp.start()

    def finish(self):
        for k in range(1, 8):
            for a in range(self.n):
                self._copy(a, k, landed=True).wait_recv()
        for k in range(1, 8):
            for a in range(self.n):
                self._copy(a, k).wait_send()
        for cp in self._mine():
            cp.wait()


_ANY = pl.BlockSpec(memory_space=pl.ANY)


def _gathered_shapes(shards):
    return [jax.ShapeDtypeStruct((N_DEV,) + s.shape, s.dtype) for s in shards]


def _all_gather(shards, name):
    n = len(shards)

    def body(*refs):
        g = _Gather(refs[:n], refs[n:2 * n], *refs[2 * n:])
        g.start()
        g.forward()
        g.finish()

    return pl.pallas_call(
        body, name=name, in_specs=[_ANY] * n, out_specs=[_ANY] * n,
        out_shape=_gathered_shapes(shards), scratch_shapes=_comm_sems(n),
    )(*shards)


def _adamw_math(w, g, m, v):
    m2 = ADAM_B1 * m + (1.0 - ADAM_B1) * g
    v2 = ADAM_B2 * v + (1.0 - ADAM_B2) * (g * g)
    m_hat = m2 / (1.0 - ADAM_B1 ** ADAM_STEP)
    v_hat = v2 / (1.0 - ADAM_B2 ** ADAM_STEP)
    delta = -ADAM_LR * (m_hat / (jnp.sqrt(v_hat) + ADAM_EPS) + ADAM_WD * w)
    return delta, m2, v2


def _adamw(w, parts, m, v, name):
    R, C = w.shape
    br = 128 if R % 128 == 0 else R

    def body(w_ref, p_ref, m_ref, v_ref, g_ref, d_ref, nm_ref, nv_ref):
        g = p_ref[0].astype(F32)
        for s in range(1, N_DEV):
            g = g + p_ref[s].astype(F32)
        g_ref[...] = g
        d_ref[...], nm_ref[...], nv_ref[...] = _adamw_math(w_ref[...], g, m_ref[...], v_ref[...])

    spec = pl.BlockSpec((br, C), lambda i: (i, 0))
    return pl.pallas_call(
        body,
        name=name,
        grid=(R // br,),
        in_specs=[spec, pl.BlockSpec((N_DEV, br, C), lambda i: (0, i, 0)), spec, spec],
        out_specs=[spec] * 4,
        out_shape=[jax.ShapeDtypeStruct((R, C), F32)] * 4,
        compiler_params=_cparams(("arbitrary",)),
    )(w, parts, m, v)


_SHARDED = ("w_in", "w_out", "w_mq", "w_mkv", "w_mo", "w_up", "conv_w", "w_down")
_LATE = _SHARDED[1:]
_COL_SHARDED = ("w_in", "w_mkv", "w_up", "conv_w")
_REPLICATED = ("attn_norm_g", "b_forget", "fox_out_g", "sb_out_g", "xattn_norm_g", "mem_norm_g",
               "ffn_norm_g", "conv_b", "final_norm_g")
_WEIGHTS = ("attn_norm_g", "w_in", "b_forget", "fox_out_g", "sb_out_g", "w_out", "xattn_norm_g", "mem_norm_g",
            "w_mq", "w_mkv", "w_mo", "ffn_norm_g", "w_up", "conv_w", "conv_b", "w_down", "final_norm_g")


def _pack_rows(n):
    return -(-n // 128)


def _pack(vals, rows_total):
    parts = []
    for v in vals:
        flat = v.reshape(-1)
        parts.append(jnp.pad(flat, (0, _pack_rows(flat.shape[0]) * 128 - flat.shape[0])))
    flat = jnp.concatenate(parts)
    return jnp.pad(flat, (0, rows_total * 128 - flat.shape[0])).reshape(rows_total, 128)


def _unpack(packed, shapes):
    out = []
    r = 0
    for shp in shapes:
        n = 1
        for d in shp:
            n *= d
        out.append(packed[r:r + _pack_rows(n)].reshape(-1)[:n].reshape(shp))
        r += _pack_rows(n)
    return out


def _gathered_full(name, gathered):
    if name in _COL_SHARDED:
        return jnp.transpose(gathered, (1, 0, 2)).reshape(gathered.shape[1], -1)
    return gathered.reshape(-1, gathered.shape[2])


def _to_blocks(name, full):
    if name in _COL_SHARDED:
        r = full.shape[0]
        return jnp.transpose(full.reshape(r, N_DEV, -1), (1, 0, 2))
    return full.reshape(N_DEV, -1, full.shape[1])


def _grad_blocks(name, full):
    blocks = _to_blocks(name, full)
    return blocks if name == "conv_w" else blocks.astype(BF16)


def _step(args, tm, tq):
    w = {n: args[n] for n in _WEIGHTS}
    mom = {n: args["m_" + n] for n in _WEIGHTS}
    var = {n: args["v_" + n] for n in _WEIGHTS}
    x = args["x"][0]
    mem = args["mem"][0]
    target = args["loss_target"][0]

    def flat2(a):
        return a.reshape(a.shape[-2], a.shape[-1]) if a.ndim == 3 else a.reshape(1, -1)

    shards = {n: flat2(w[n]) if n == "conv_w" else flat2(w[n]).astype(BF16) for n in _SHARDED}
    (w_in_all,) = _all_gather([shards["w_in"]], "gather_w_in")
    p = {"w_in": _gathered_full("w_in", w_in_all)}
    for n in _REPLICATED:
        p[n] = flat2(w[n])

    loss_blk, grad_x, g = _local_step(x, mem, target, p, tm, tq, late={n: shards[n] for n in _LATE})

    parts = g["parts"]
    out = {}
    for n in _SHARDED:
        res = _adamw(flat2(w[n]), parts[n], flat2(mom[n]), flat2(var[n]), "adamw_" + n)
        out[n] = [r.reshape(w[n].shape) for r in res]

    shapes = [w[n].shape for n in _REPLICATED]
    rows = sum(_pack_rows(flat2(w[n]).shape[1]) for n in _REPLICATED) + 1
    rows = -(-rows // 8) * 8
    g_pack = _pack([g[n] for n in _REPLICATED] + [loss_blk[0:1, :]], rows)
    (g_all,) = _all_gather([g_pack], "gather_small")
    res = _adamw(_pack([w[n] for n in _REPLICATED], rows), g_all,
                 _pack([mom[n] for n in _REPLICATED], rows), _pack([var[n] for n in _REPLICATED], rows),
                 "adamw_small")
    n_rows_params = sum(_pack_rows(flat2(w[n]).shape[1]) for n in _REPLICATED)
    loss = res[0][n_rows_params, 0]
    unpacked = [_unpack(r, shapes) for r in res]
    for k, n in enumerate(_REPLICATED):
        out[n] = [unpacked[q][k] for q in range(4)]

    grads = [out[n][0] for n in _WEIGHTS]
    deltas = [out[n][1] for n in _WEIGHTS]
    new_m = [out[n][2] for n in _WEIGHTS]
    new_v = [out[n][3] for n in _WEIGHTS]
    return (loss, grad_x[None], *grads, *deltas, *new_m, *new_v)


def kernel(x, mem, attn_norm_g, w_in, b_forget, fox_out_g, sb_out_g, w_out, xattn_norm_g, mem_norm_g, w_mq, w_mkv, w_mo, ffn_norm_g, w_up, conv_w, conv_b, w_down, final_norm_g, loss_target, m_attn_norm_g, m_w_in, m_b_forget, m_fox_out_g, m_sb_out_g, m_w_out, m_xattn_norm_g, m_mem_norm_g, m_w_mq, m_w_mkv, m_w_mo, m_ffn_norm_g, m_w_up, m_conv_w, m_conv_b, m_w_down, m_final_norm_g, v_attn_norm_g, v_w_in, v_b_forget, v_fox_out_g, v_sb_out_g, v_w_out, v_xattn_norm_g, v_mem_norm_g, v_w_mq, v_w_mkv, v_w_mo, v_ffn_norm_g, v_w_up, v_conv_w, v_conv_b, v_w_down, v_final_norm_g):
    args = dict(locals())
    T = x.shape[1]
    return _step(args, tm=min(T, 512), tq=min(T, 256))
```

```python
import functools

import jax
import jax.numpy as jnp
from jax import lax
from jax.experimental import pallas as pl
from jax.experimental.pallas import tpu as pltpu

F32 = jnp.float32
BF16 = jnp.bfloat16
EPS = 1e-6
NEG = -1e30
LOG2E = 1.4426950408889634

HEAD_DIM = 64
N_FOX = 8
FOX_W = 512
QKV_W = 3072
N_MEM_HEADS = 4
MEM_HD = 256
D_FF = 2816
FF_CHUNK = 256
N_DEV = 8

ADAM_LR = 0.001
ADAM_B1 = 0.9
ADAM_B2 = 0.999
ADAM_EPS = 1e-08
ADAM_WD = 0.01
ADAM_STEP = 10

SB_SUM_TERMS = 1

VMEM_LIMIT = 56 * 1024 * 1024
MESH = pl.DeviceIdType.MESH


def _cparams(sem=None):
    return pltpu.CompilerParams(dimension_semantics=sem, vmem_limit_bytes=VMEM_LIMIT)


def _nt(a, b):
    return lax.dot_general(a, b, (((1,), (1,)), ((), ())), preferred_element_type=F32)


def _tn(a, b):
    return lax.dot_general(a, b, (((0,), (0,)), ((), ())), preferred_element_type=F32)


def _nn(a, b):
    return jnp.dot(a, b, preferred_element_type=F32)


def _split_dot(a, m01, terms):
    out = None
    r = a
    for t in range(terms):
        p = r.astype(BF16)
        d = _nn(p, m01)
        out = d if out is None else out + d
        if t + 1 < terms:
            r = r - p.astype(F32)
    return out


def _rstd(xv):
    return lax.rsqrt(jnp.mean(xv * xv, axis=-1, keepdims=True) + EPS)


def _norm_bwd(xv, g, dh):
    r = _rstd(xv)
    xhat = xv * r
    dxhat = dh * g
    dx = r * (dxhat - xhat * jnp.mean(dxhat * xhat, axis=-1, keepdims=True))
    dg = jnp.sum(dh * xhat, axis=0, keepdims=True)
    return dx, dg


def _tile_div(n, cap):
    best = None
    for d in range(128, min(n, cap) + 1, 128):
        if n % d == 0:
            best = d
    assert best is not None, n
    return best


def _inproj_fwd(x, g1, w_qkv, w_f_t, b_f, tm):
    T, D = x.shape
    N = w_qkv.shape[1]
    H = w_f_t.shape[0]

    def body(x_ref, g_ref, w_ref, wf_ref, b_ref, proj_ref, h_ref, xf_ref, c_ref, carry_ref):
        i = pl.program_id(0)

        @pl.when(i == 0)
        def _():
            carry_ref[...] = jnp.zeros_like(carry_ref)

        xv = x_ref[...]
        h = (xv * _rstd(xv) * g_ref[...]).astype(BF16)
        h_ref[...] = h
        for n0 in range(0, N, 512):
            proj_ref[:, n0:n0 + 512] = _nn(h, w_ref[:, n0:n0 + 512]).astype(BF16)
        xf = _nt(wf_ref[...], h) + b_ref[...]
        xf_ref[...] = xf
        logf = jnp.minimum(xf, 0.0) - jnp.log1p(jnp.exp(-jnp.abs(xf)))
        row = lax.broadcasted_iota(jnp.int32, (tm, tm), 0)
        col = lax.broadcasted_iota(jnp.int32, (tm, tm), 1)
        upper = jnp.where(row <= col, 1.0, 0.0).astype(BF16)
        c = _split_dot(logf, upper, 3) + carry_ref[...]
        c_ref[...] = c
        carry_ref[...] = c[:, tm - 1:tm]

    return pl.pallas_call(
        body,
        name="inproj_fwd",
        grid=(T // tm,),
        in_specs=[
            pl.BlockSpec((tm, D), lambda i: (i, 0)),
            pl.BlockSpec((1, D), lambda i: (0, 0)),
            pl.BlockSpec((D, N), lambda i: (0, 0)),
            pl.BlockSpec((H, D), lambda i: (0, 0)),
            pl.BlockSpec((H, 1), lambda i: (0, 0)),
        ],
        out_specs=[
            pl.BlockSpec((tm, N), lambda i: (i, 0)),
            pl.BlockSpec((tm, D), lambda i: (i, 0)),
            pl.BlockSpec((H, tm), lambda i: (0, i)),
            pl.BlockSpec((H, tm), lambda i: (0, i)),
        ],
        out_shape=[
            jax.ShapeDtypeStruct((T, N), BF16),
            jax.ShapeDtypeStruct((T, D), BF16),
            jax.ShapeDtypeStruct((H, T), F32),
            jax.ShapeDtypeStruct((H, T), F32),
        ],
        scratch_shapes=[pltpu.VMEM((H, 1), F32)],
        compiler_params=_cparams(("arbitrary",)),
    )(x, g1, w_qkv, w_f_t, b_f)


def _head_q(q, hh, lane):
    hmask = (lane >= HEAD_DIM * hh) & (lane < HEAD_DIM * (hh + 1))
    qh = jnp.where(hmask, q.astype(F32), 0.0) * (HEAD_DIM ** -0.5)
    return qh.astype(BF16), hmask


def _pipeline3(n, stage_a, stage_b, stage_c, diag_last, alive=None):
    stage_a(0, 0)
    if diag_last:
        @pl.when(n == 1)
        def _():
            stage_b(0, 0, True)

        @pl.when(n >= 2)
        def _():
            stage_b(0, 0, False)
    else:
        stage_b(0, 0, True)

    @pl.when(n >= 2)
    def _():
        stage_a(1, 1)

    def pair(m, carry):
        t = 2 + 2 * m
        stage_c(t - 2)
        stage_b(t - 1, 1, False)
        stage_a(t, 0)
        stage_c(t - 1)
        stage_b(t, 0, False)
        stage_a(t + 1, 1)
        return carry

    pairs = (n - 2) // 2
    if alive is None:
        lax.fori_loop(0, pairs, pair, 0)
        go_on = True
        done = n
    else:
        def more(state):
            return (state[0] < pairs) & state[1]

        def step(state):
            pair(state[0], 0)
            return state[0] + 1, alive()

        m_end, go_on = lax.while_loop(more, step, (jnp.int32(0), jnp.bool_(True)))
        done = jnp.where(go_on, n, 2 * m_end)
    odd = n % 2 == 1

    @pl.when((n >= 3) & odd & go_on)
    def _():
        stage_c(n - 3)
        stage_b(n - 2, 1, False)
        stage_a(n - 1, 0)

    @pl.when((n >= 2) & odd & go_on)
    def _():
        stage_c(n - 2)
        stage_b(n - 1, 0, diag_last)

    @pl.when((n >= 2) & jnp.logical_not(odd) & go_on)
    def _():
        stage_c(n - 2)
        stage_b(n - 1, 1, diag_last)

    if alive is None:
        stage_c(n - 1)
    else:
        @pl.when(go_on)
        def _():
            stage_c(n - 1)

    return done


def _lanes2(x):
    return jnp.concatenate([x, x], axis=1)


def _lanes_to_rows(vec, eye):
    return jnp.sum(jnp.where(eye, jnp.broadcast_to(vec, eye.shape), 0.0), axis=1, keepdims=True)


def _rows_to_lanes(rep, eye):
    return jnp.sum(jnp.where(eye, _lanes2(rep), 0.0), axis=0, keepdims=True)


FOX_DEAD = -110.0


def _fox_key_norms(k_ref, kn_s, lane):
    T = k_ref.shape[0]
    rows = min(T, 512)
    for hh in range(2):
        hmask = (lane >= HEAD_DIM * hh) & (lane < HEAD_DIM * (hh + 1))

        def chunk(n, best, hmask=hmask):
            kf = jnp.where(hmask, k_ref[pl.ds(pl.multiple_of(n * rows, rows), rows), :].astype(F32), 0.0)
            sq = jnp.sum(kf * kf, axis=1, keepdims=True)
            return jnp.maximum(best, jnp.max(sq, axis=0, keepdims=True))

        best = lax.fori_loop(0, T // rows, chunk, jnp.zeros((1, 1), F32))
        kn_s[hh] = jnp.broadcast_to(best, kn_s.shape[1:])


def _fox_live_blocks(i, qh_s, kn_s, cq_ref, cke_ref):
    nq = cke_ref.shape[-1]
    jj = lax.broadcasted_iota(jnp.int32, (1, nq), 1)
    first = None
    for hh in range(2):
        qf = qh_s[hh].astype(F32)
        qn = jnp.max(jnp.sum(qf * qf, axis=1, keepdims=True), axis=0, keepdims=True)
        zb = jnp.sqrt(qn * kn_s[hh][0:1, 0:1]) * 1.001
        bound = (2.0 * zb + cq_ref[hh][:, 0:1]) - cke_ref[hh]
        live = (bound >= FOX_DEAD) & (jj <= i)
        f = jnp.min(jnp.where(live, jj, i).astype(F32), axis=1, keepdims=True)
        first = f if first is None else jnp.minimum(first, f)
    return i + 1 - first[0, 0].astype(jnp.int32)


def _ride_along(exchange, at_start, at_middle, at_end):
    @pl.when(at_start)
    def _():
        exchange.start()

    if at_middle is not None:
        @pl.when(at_middle)
        def _():
            exchange.forward()

    def finish():
        @pl.when(at_end)
        def _():
            exchange.finish()

    return finish


def _fox_fwd(proj, c_col, c_row, c_ends, tq, gather=()):
    T = proj.shape[0]
    assert tq == 256
    nq = T // tq
    ng = len(gather)

    def body(*refs):
        q_ref, k_ref, v_ref, cq_ref, ck_ref, cke_ref = refs[:6]
        o_ref, lse_ref = refs[6 + ng:8 + ng]
        qh_s, cq_s, z_s, p_s, al_s, m_s, acc_s, kn_s = refs[8 + 2 * ng:16 + 2 * ng]
        i = pl.program_id(1)
        if ng:
            pair = pl.program_id(0)
            finish = _ride_along(_Gather(refs[6:6 + ng], refs[8 + ng:8 + 2 * ng], *refs[16 + 2 * ng:]),
                                 (pair == 0) & (i == 0), (pair == 1) & (i == 0), (pair == 3) & (i == nq - 1))
        lane = lax.broadcasted_iota(jnp.int32, (1, 128), 1)
        row = lax.broadcasted_iota(jnp.int32, (tq, tq), 0)
        col = lax.broadcasted_iota(jnp.int32, (tq, tq), 1)

        @pl.when(i == 0)
        def _():
            _fox_key_norms(k_ref, kn_s, lane)

        q = q_ref[...]
        for hh in range(2):
            qh_s[hh] = _head_q(q, hh, lane)[0]
            cq_s[hh] = jnp.broadcast_to(_lanes_to_rows(cq_ref[hh], row == col), (tq, tq))
        m_s[...] = jnp.full(m_s.shape, NEG, F32)
        acc_s[...] = jnp.zeros_like(acc_s)

        def rows(t):
            return pl.ds(pl.multiple_of((i - t) * tq, tq), tq)

        def stage_a(t, slot):
            k = k_ref[rows(t), :]
            for hh in range(2):
                z_s[slot, hh] = _nt(qh_s[hh], k)

        def stage_b(t, slot, diag):
            for hh in range(2):
                s = z_s[slot, hh] + cq_s[hh] - ck_ref[hh, :, rows(t)]
                if diag:
                    s = jnp.where(col <= row, s, NEG)
                m = m_s[hh]
                half = jnp.maximum(s[:, :128], s[:, 128:])
                m_new = jnp.maximum(m, jnp.max(half, axis=1, keepdims=True))
                m_s[hh] = m_new
                al_s[hh] = jnp.exp(m - m_new)
                p_s[hh] = jnp.exp(s - _lanes2(m_new)).astype(BF16)

        def stage_c(t):
            v = v_ref[rows(t), :]
            for hh in range(2):
                own = (lane >= HEAD_DIM * hh) & (lane < HEAD_DIM * (hh + 1))
                acc_s[hh] = al_s[hh] * acc_s[hh] + _nn(p_s[hh], jnp.where(own, v, 1.0).astype(BF16))

        _pipeline3(_fox_live_blocks(i, qh_s, kn_s, cq_ref, cke_ref), stage_a, stage_b, stage_c, False)
        halves = []
        for hh in range(2):
            acc = acc_s[hh]
            own = (lane >= HEAD_DIM * hh) & (lane < HEAD_DIM * (hh + 1))
            halves.append(jnp.where(own, pltpu.roll(acc, HEAD_DIM, axis=1), acc))
        l0, l1 = halves
        o_ref[...] = jnp.where(lane < HEAD_DIM, acc_s[0] / l0, acc_s[1] / l1)
        lse_ref[0] = _rows_to_lanes(m_s[0] + jnp.log(l0), row == col)
        lse_ref[1] = _rows_to_lanes(m_s[1] + jnp.log(l1), row == col)
        if ng:
            finish()

    res = pl.pallas_call(
        body,
        name="fox_fwd",
        grid=(4, nq),
        in_specs=[
            pl.BlockSpec((tq, 128), lambda p, i: (i, p)),
            pl.BlockSpec((T, 128), lambda p, i: (0, 4 + p)),
            pl.BlockSpec((T, 128), lambda p, i: (0, 8 + p)),
            pl.BlockSpec((2, 1, tq), lambda p, i: (p, 0, i)),
            pl.BlockSpec((2, 1, T), lambda p, i: (p, 0, 0)),
            pl.BlockSpec((2, 1, nq), lambda p, i: (p, 0, 0)),
        ] + [_ANY] * ng,
        out_specs=[
            pl.BlockSpec((tq, 128), lambda p, i: (i, p)),
            pl.BlockSpec((2, 1, tq), lambda p, i: (p, 0, i)),
        ] + [_ANY] * ng,
        out_shape=[
            jax.ShapeDtypeStruct((T, FOX_W), F32),
            jax.ShapeDtypeStruct((N_FOX, 1, T), F32),
        ] + _gathered_shapes(gather),
        scratch_shapes=[
            pltpu.VMEM((2, tq, 128), BF16),
            pltpu.VMEM((2, tq, tq), F32),
            pltpu.VMEM((2, 2, tq, tq), F32),
            pltpu.VMEM((2, tq, tq), BF16),
            pltpu.VMEM((2, tq, 128), F32),
            pltpu.VMEM((2, tq, 128), F32),
            pltpu.VMEM((2, tq, 128), F32),
            pltpu.VMEM((2, 8, 128), F32),
        ] + (_comm_sems(ng) if ng else []),
        compiler_params=_cparams(("arbitrary", "arbitrary")),
    )(proj, proj, proj, c_col, c_row, c_ends, *gather)
    res = list(res)
    return res[0], res[1], res[2:]


def _sb_logs(zn, strict):
    e = jnp.exp2(jnp.abs(zn) * (-LOG2E))
    L = jnp.minimum(zn, 0.0) - jnp.log(1.0 + e)
    G = L - zn
    if strict is not None:
        L = jnp.where(strict, L, 0.0)
    return L, G


SB_DEAD = -110.0


def _sb_fwd(proj, tq):
    T = proj.shape[0]
    nq = T // tq

    def body(q_ref, k_ref, v_ref, o_ref, ltot_ref, live_ref, qh_s, z_s, g_s, tot_s, run_s, acc_s):
        i = pl.program_id(1)
        lane = lax.broadcasted_iota(jnp.int32, (1, 128), 1)
        row = lax.broadcasted_iota(jnp.int32, (tq, tq), 0)
        col = lax.broadcasted_iota(jnp.int32, (tq, tq), 1)
        strict = col < row
        later = jnp.where(row > col, 1.0, 0.0).astype(BF16)
        q = q_ref[...]
        for hh in range(2):
            qh_s[hh] = -_head_q(q, hh, lane)[0]
        run_s[...] = jnp.zeros_like(run_s)
        acc_s[...] = jnp.zeros_like(acc_s)

        def rows(t):
            return pl.ds(pl.multiple_of((i - t) * tq, tq), tq)

        def stage_a(t, slot):
            k = k_ref[rows(t), :]
            for hh in range(2):
                z_s[slot, hh] = _nt(qh_s[hh], k)

        def stage_b(t, slot, diag):
            for hh in range(2):
                L, g = _sb_logs(z_s[slot, hh], strict if diag else None)
                if diag:
                    g = jnp.where(strict, g, NEG)
                after = _split_dot(L, later, SB_SUM_TERMS)
                g_s[hh] = g + after
                first = L[:, 0:1]
                if SB_SUM_TERMS == 1:
                    first = first.astype(BF16).astype(F32)
                tot_s[hh] = jnp.broadcast_to(after[:, 0:1] + first, (tq, 128))

        def stage_c(t):
            v = v_ref[rows(t), :]
            for hh in range(2):
                run = run_s[hh]
                a = jnp.exp(g_s[hh] + _lanes2(run))
                acc_s[hh] += _nn(a.astype(BF16), v)
                run_s[hh] = run + tot_s[hh]

        def alive():
            return jnp.max(jnp.maximum(run_s[0], run_s[1])) > SB_DEAD

        done = _pipeline3(i + 1, stage_a, stage_b, stage_c, False, alive)
        ltot_ref[0] = _rows_to_lanes(run_s[0], row == col)
        ltot_ref[1] = _rows_to_lanes(run_s[1], row == col)
        o_ref[...] = jnp.where(lane < HEAD_DIM, acc_s[0], acc_s[1])
        at = lax.broadcasted_iota(jnp.int32, (1, nq), 1)

        @pl.when(i == 0)
        def _():
            live_ref[0] = jnp.zeros((1, nq), F32)

        live_ref[0] = jnp.where(at == i, done.astype(F32), live_ref[0])

    return pl.pallas_call(
        body,
        name="sb_fwd",
        grid=(4, nq),
        in_specs=[
            pl.BlockSpec((tq, 128), lambda p, i: (i, 12 + p)),
            pl.BlockSpec((T, 128), lambda p, i: (0, 16 + p)),
            pl.BlockSpec((T, 128), lambda p, i: (0, 20 + p)),
        ],
        out_specs=[
            pl.BlockSpec((tq, 128), lambda p, i: (i, p)),
            pl.BlockSpec((2, 1, tq), lambda p, i: (p, 0, i)),
            pl.BlockSpec((1, 1, nq), lambda p, i: (p, 0, 0)),
        ],
        out_shape=[
            jax.ShapeDtypeStruct((T, FOX_W), F32),
            jax.ShapeDtypeStruct((N_FOX, 1, T), F32),
            jax.ShapeDtypeStruct((N_FOX // 2, 1, nq), F32),
        ],
        scratch_shapes=[
            pltpu.VMEM((2, tq, 128), BF16),
            pltpu.VMEM((2, 2, tq, tq), F32),
            pltpu.VMEM((2, tq, tq), F32),
            pltpu.VMEM((2, tq, 128), F32),
            pltpu.VMEM((2, tq, 128), F32),
            pltpu.VMEM((2, tq, 128), F32),
        ],
        compiler_params=_cparams(("arbitrary", "arbitrary")),
    )(proj, proj, proj)


def _post_attn_fwd(fox_o, sb_o, gf, gs, w_out, x, tm):
    T, D = x.shape

    def body(f_ref, s_ref, gf_ref, gs_ref, w_ref, x_ref, x1_ref, mix_ref):
        f = f_ref[...]
        s = s_ref[...]
        mix_ref[:, :FOX_W] = (f * _rstd(f) * gf_ref[...]).astype(BF16)
        mix_ref[:, FOX_W:] = (s * _rstd(s) * gs_ref[...]).astype(BF16)
        x1_ref[...] = x_ref[...] + _nn(mix_ref[...], w_ref[...])

    return pl.pallas_call(
        body,
        name="post_attn_fwd",
        grid=(T // tm,),
        in_specs=[
            pl.BlockSpec((tm, FOX_W), lambda i: (i, 0)),
            pl.BlockSpec((tm, FOX_W), lambda i: (i, 0)),
            pl.BlockSpec((1, FOX_W), lambda i: (0, 0)),
            pl.BlockSpec((1, FOX_W), lambda i: (0, 0)),
            pl.BlockSpec((D, D), lambda i: (0, 0)),
            pl.BlockSpec((tm, D), lambda i: (i, 0)),
        ],
        out_specs=[
            pl.BlockSpec((tm, D), lambda i: (i, 0)),
            pl.BlockSpec((tm, D), lambda i: (i, 0)),
        ],
        out_shape=[jax.ShapeDtypeStruct((T, D), F32), jax.ShapeDtypeStruct((T, D), BF16)],
        compiler_params=_cparams(("arbitrary",)),
    )(fox_o, sb_o, gf, gs, w_out, x)


def _mem_kv_fwd(mem, gm, w_mkv):
    M, D = mem.shape
    N = w_mkv.shape[1]

    def body(mem_ref, g_ref, w_ref, m_ref, kv_ref):
        mv = mem_ref[...]
        m = (mv * _rstd(mv) * g_ref[...]).astype(BF16)
        m_ref[...] = m
        for n0 in range(0, N, 512):
            kv_ref[:, n0:n0 + 512] = _nn(m, w_ref[:, n0:n0 + 512]).astype(BF16)

    return pl.pallas_call(
        body,
        name="mem_kv_fwd",
        out_shape=[jax.ShapeDtypeStruct((M, D), BF16), jax.ShapeDtypeStruct((M, N), BF16)],
        compiler_params=_cparams(),
    )(mem, gm, w_mkv)


def _xattn_probs(qb, kv, h):
    k = kv[:, h * MEM_HD:(h + 1) * MEM_HD]
    s = _nt(qb[:, h * MEM_HD:(h + 1) * MEM_HD], k) * (MEM_HD ** -0.5)
    s = s - jnp.max(s, axis=1, keepdims=True)
    p = jnp.exp(s)
    return p / jnp.sum(p, axis=1, keepdims=True)


def _xattn_fwd(x1, g2, w_mq, kv, w_mo, tm):
    T, D = x1.shape
    M = kv.shape[0]

    def body(x_ref, g_ref, wq_ref, kv_ref, wo_ref, x2_ref, h_ref, q_ref, om_ref):
        xv = x_ref[...]
        h = (xv * _rstd(xv) * g_ref[...]).astype(BF16)
        h_ref[...] = h
        q_ref[...] = _nn(h, wq_ref[...]).astype(BF16)
        qb = q_ref[...]
        kvv = kv_ref[...]
        for hd in range(N_MEM_HEADS):
            p = _xattn_probs(qb, kvv, hd)
            v = kvv[:, D + hd * MEM_HD:D + (hd + 1) * MEM_HD]
            om_ref[:, hd * MEM_HD:(hd + 1) * MEM_HD] = _nn(p.astype(BF16), v).astype(BF16)
        x2_ref[...] = xv + _nn(om_ref[...], wo_ref[...])

    return pl.pallas_call(
        body,
        name="xattn_fwd",
        grid=(T // tm,),
        in_specs=[
            pl.BlockSpec((tm, D), lambda i: (i, 0)),
            pl.BlockSpec((1, D), lambda i: (0, 0)),
            pl.BlockSpec((D, D), lambda i: (0, 0)),
            pl.BlockSpec((M, 2 * D), lambda i: (0, 0)),
            pl.BlockSpec((D, D), lambda i: (0, 0)),
        ],
        out_specs=[pl.BlockSpec((tm, D), lambda i: (i, 0))] * 4,
        out_shape=[jax.ShapeDtypeStruct((T, D), F32)] + [jax.ShapeDtypeStruct((T, D), BF16)] * 3,
        compiler_params=_cparams(("arbitrary",)),
    )(x1, g2, w_mq, kv, w_mo)


def _conv_taps(ext_ref, tm, back):
    if back:
        return ext_ref[pl.ds(6, tm), :], ext_ref[pl.ds(7, tm), :], ext_ref[pl.ds(8, tm), :]
    return ext_ref[pl.ds(0, tm), :], ext_ref[pl.ds(1, tm), :], ext_ref[pl.ds(2, tm), :]


def _ffn_fwd(x2, g3, w_up, conv_w, conv_b, w_down, tm):
    T, D = x2.shape
    fc = FF_CHUNK
    nj = D_FF // fc

    def body(x_ref, g_ref, wg_ref, wv_ref, cwg_ref, cwv_ref, cbg_ref, cbv_ref, wd_ref,
             x3_ref, h_ref, ug_ref, uv_ref, yg_ref, yv_ref, a_ref, acc_ref, carry_ref, ext_ref):
        i = pl.program_id(0)
        j = pl.program_id(1)

        @pl.when(j == 0)
        def _():
            xv = x_ref[...]
            h_ref[...] = (xv * _rstd(xv) * g_ref[...]).astype(BF16)
            acc_ref[...] = xv

        @pl.when(i == 0)
        def _():
            carry_ref[j] = jnp.zeros((2, 8, fc), F32)

        h = h_ref[...]
        halves = []
        for part, (w_ref, cw_ref, cb_ref, u_ref, y_ref) in enumerate(
                ((wg_ref, cwg_ref, cbg_ref, ug_ref, yg_ref), (wv_ref, cwv_ref, cbv_ref, uv_ref, yv_ref))):
            u = _nn(h, w_ref[...])
            u_ref[...] = u.astype(BF16)
            ext = ext_ref.at[part]
            ext[pl.ds(0, 8), :] = carry_ref[j, part]
            ext[pl.ds(8, tm), :] = u
            carry_ref[j, part] = u[tm - 8:, :]
            u2, u1, u0 = _conv_taps(ext, tm, True)
            cw = cw_ref[...]
            y = cb_ref[...] + cw[0:1] * u2 + cw[1:2] * u1 + cw[2:3] * u0
            y_ref[...] = y.astype(BF16)
            halves.append(y)
        gate, val = halves
        a = (gate * jax.nn.sigmoid(gate) * val).astype(BF16)
        a_ref[...] = a
        acc_ref[...] += _nn(a, wd_ref[...])

        @pl.when(j == nj - 1)
        def _():
            x3_ref[...] = acc_ref[...]

    return pl.pallas_call(
        body,
        name="ffn_fwd",
        grid=(T // tm, nj),
        in_specs=[
            pl.BlockSpec((tm, D), lambda i, j: (i, 0)),
            pl.BlockSpec((1, D), lambda i, j: (0, 0)),
            pl.BlockSpec((D, fc), lambda i, j: (0, j)),
            pl.BlockSpec((D, fc), lambda i, j: (0, nj + j)),
            pl.BlockSpec((3, fc), lambda i, j: (0, j)),
            pl.BlockSpec((3, fc), lambda i, j: (0, nj + j)),
            pl.BlockSpec((1, fc), lambda i, j: (0, j)),
            pl.BlockSpec((1, fc), lambda i, j: (0, nj + j)),
            pl.BlockSpec((fc, D), lambda i, j: (j, 0)),
        ],
        out_specs=[
            pl.BlockSpec((tm, D), lambda i, j: (i, 0)),
            pl.BlockSpec((tm, D), lambda i, j: (i, 0)),
        ] + [pl.BlockSpec((tm, fc), lambda i, j: (i, j))] * 5,
        out_shape=[
            jax.ShapeDtypeStruct((T, D), F32),
            jax.ShapeDtypeStruct((T, D), BF16),
        ] + [jax.ShapeDtypeStruct((T, D_FF), BF16)] * 5,
        scratch_shapes=[
            pltpu.VMEM((tm, D), F32),
            pltpu.VMEM((nj, 2, 8, fc), F32),
            pltpu.VMEM((2, tm + 8, fc), F32),
        ],
        compiler_params=_cparams(("arbitrary", "arbitrary")),
    )(x2, g3, w_up, w_up, conv_w, conv_w, conv_b, conv_b, w_down)


def _loss_head(x3, gfin, target, tm):
    T, D = x3.shape

    def body(x_ref, g_ref, t_ref, dx_ref, loss_ref, dg_ref):
        i = pl.program_id(0)

        @pl.when(i == 0)
        def _():
            loss_ref[...] = jnp.zeros_like(loss_ref)
            dg_ref[...] = jnp.zeros_like(dg_ref)

        xv = x_ref[...]
        g = g_ref[...]
        r = _rstd(xv)
        xhat = xv * r
        err = xhat * g - t_ref[...]
        part = jnp.sum(jnp.sum(err * err, axis=1, keepdims=True), axis=0, keepdims=True) * (0.5 / D)
        loss_ref[...] += jnp.broadcast_to(part, loss_ref.shape)
        dy = err * (1.0 / D)
        dg_ref[...] += jnp.sum(dy * xhat, axis=0, keepdims=True)
        dxhat = dy * g
        dx_ref[...] = r * (dxhat - xhat * jnp.mean(dxhat * xhat, axis=-1, keepdims=True))

    return pl.pallas_call(
        body,
        name="loss_head",
        grid=(T // tm,),
        in_specs=[
            pl.BlockSpec((tm, D), lambda i: (i, 0)),
            pl.BlockSpec((1, D), lambda i: (0, 0)),
            pl.BlockSpec((tm, D), lambda i: (i, 0)),
        ],
        out_specs=[
            pl.BlockSpec((tm, D), lambda i: (i, 0)),
            pl.BlockSpec((8, 128), lambda i: (0, 0)),
            pl.BlockSpec((1, D), lambda i: (0, 0)),
        ],
        out_shape=[
            jax.ShapeDtypeStruct((T, D), F32),
            jax.ShapeDtypeStruct((8, 128), F32),
            jax.ShapeDtypeStruct((1, D), F32),
        ],
        compiler_params=_cparams(("arbitrary",)),
    )(x3, gfin, target)


def _ffn_bwd(dx3, x2, g3, ug, uv, yg, yv, conv_w, w_down, w_up, tm):
    T, D = x2.shape
    fc = FF_CHUNK
    nj = D_FF // fc
    nt = T // tm

    def rev(i):
        return nt - 1 - i

    def body(dx3_ref, x_ref, g_ref, ug_ref, uv_ref, yg_ref, yv_ref, cwg_ref, cwv_ref,
             wd_ref, wug_ref, wuv_ref,
             dx2_ref, dug_ref, duv_ref, dg_ref, dcg_ref, dcv_ref,
             acc_ref, carry_ref, ext_ref):
        i = pl.program_id(0)
        j = pl.program_id(1)
        cols = pl.ds(pl.multiple_of(j * fc, fc), fc)

        @pl.when(j == 0)
        def _():
            acc_ref[...] = jnp.zeros_like(acc_ref)

        @pl.when((i == 0) & (j == 0))
        def _():
            dg_ref[...] = jnp.zeros_like(dg_ref)
            dcg_ref[...] = jnp.zeros_like(dcg_ref)
            dcv_ref[...] = jnp.zeros_like(dcv_ref)

        @pl.when(i == 0)
        def _():
            carry_ref[j] = jnp.zeros((2, 8, fc), F32)

        da = _nt(dx3_ref[...].astype(BF16), wd_ref[...])
        gate = yg_ref[...].astype(F32)
        val = yv_ref[...].astype(F32)
        sig = jax.nn.sigmoid(gate)
        silu = gate * sig
        dys = (da * val * (sig * (1.0 + gate * (1.0 - sig))), da * silu)
        for part, (dy, u_ref, cw_ref, du_ref, wu_ref, dc_ref) in enumerate(
                ((dys[0], ug_ref, cwg_ref, dug_ref, wug_ref, dcg_ref),
                 (dys[1], uv_ref, cwv_ref, duv_ref, wuv_ref, dcv_ref))):
            ext = ext_ref.at[part]
            ext[pl.ds(0, tm), :] = dy
            ext[pl.ds(tm, 8), :] = carry_ref[j, part]
            carry_ref[j, part] = dy[:8, :]
            d0, d1, d2 = _conv_taps(ext, tm, False)
            u = u_ref[...].astype(F32)
            upd = jnp.concatenate([
                jnp.sum(u * d2, axis=0, keepdims=True),
                jnp.sum(u * d1, axis=0, keepdims=True),
                jnp.sum(u * d0, axis=0, keepdims=True),
                jnp.sum(d0, axis=0, keepdims=True),
                jnp.zeros((4, fc), F32)], axis=0)
            dc_ref[:, cols] += upd
            cw = cw_ref[...]
            du = (cw[2:3] * d0 + cw[1:2] * d1 + cw[0:1] * d2).astype(BF16)
            du_ref[...] = du
            acc_ref[...] += _nt(du, wu_ref[...])

        @pl.when(j == nj - 1)
        def _():
            dx, dg = _norm_bwd(x_ref[...], g_ref[...], acc_ref[...])
            dx2_ref[...] = dx3_ref[...] + dx
            dg_ref[...] += dg

    return pl.pallas_call(
        body,
        name="ffn_bwd",
        grid=(nt, nj),
        in_specs=[
            pl.BlockSpec((tm, D), lambda i, j: (rev(i), 0)),
            pl.BlockSpec((tm, D), lambda i, j: (rev(i), 0)),
            pl.BlockSpec((1, D), lambda i, j: (0, 0)),
            pl.BlockSpec((tm, fc), lambda i, j: (rev(i), j)),
            pl.BlockSpec((tm, fc), lambda i, j: (rev(i), j)),
            pl.BlockSpec((tm, fc), lambda i, j: (rev(i), j)),
            pl.BlockSpec((tm, fc), lambda i, j: (rev(i), j)),
            pl.BlockSpec((3, fc), lambda i, j: (0, j)),
            pl.BlockSpec((3, fc), lambda i, j: (0, nj + j)),
            pl.BlockSpec((fc, D), lambda i, j: (j, 0)),
            pl.BlockSpec((D, fc), lambda i, j: (0, j)),
            pl.BlockSpec((D, fc), lambda i, j: (0, nj + j)),
        ],
        out_specs=[
            pl.BlockSpec((tm, D), lambda i, j: (rev(i), 0)),
            pl.BlockSpec((tm, fc), lambda i, j: (rev(i), j)),
            pl.BlockSpec((tm, fc), lambda i, j: (rev(i), j)),
            pl.BlockSpec((1, D), lambda i, j: (0, 0)),
            pl.BlockSpec((8, D_FF), lambda i, j: (0, 0)),
            pl.BlockSpec((8, D_FF), lambda i, j: (0, 0)),
        ],
        out_shape=[
            jax.ShapeDtypeStruct((T, D), F32),
            jax.ShapeDtypeStruct((T, D_FF), BF16),
            jax.ShapeDtypeStruct((T, D_FF), BF16),
            jax.ShapeDtypeStruct((1, D), F32),
            jax.ShapeDtypeStruct((8, D_FF), F32),
            jax.ShapeDtypeStruct((8, D_FF), F32),
        ],
        scratch_shapes=[
            pltpu.VMEM((tm, D), F32),
            pltpu.VMEM((nj, 2, 8, fc), F32),
            pltpu.VMEM((2, tm + 8, fc), F32),
        ],
        compiler_params=_cparams(("arbitrary", "arbitrary")),
    )(dx3, x2, g3, ug, uv, yg, yv, conv_w, conv_w, w_down, w_up, w_up)


def _xattn_bwd(dx2, x1, g2, qb, kv, w_mo, w_mq, tm):
    T, D = x1.shape
    M = kv.shape[0]

    def body(dx2_ref, x_ref, g_ref, q_ref, kv_ref, wo_ref, wq_ref, dx1_ref, dq_ref, dkv_ref, dg_ref):
        i = pl.program_id(0)

        @pl.when(i == 0)
        def _():
            dkv_ref[...] = jnp.zeros_like(dkv_ref)
            dg_ref[...] = jnp.zeros_like(dg_ref)

        dxv = dx2_ref[...]
        dom = _nt(dxv.astype(BF16), wo_ref[...]).astype(BF16)
        qb_ = q_ref[...]
        kvv = kv_ref[...]
        for hd in range(N_MEM_HEADS):
            sl = slice(hd * MEM_HD, (hd + 1) * MEM_HD)
            vsl = slice(D + hd * MEM_HD, D + (hd + 1) * MEM_HD)
            p = _xattn_probs(qb_, kvv, hd)
            dp = _nt(dom[:, sl], kvv[:, vsl])
            ds = (p * (dp - jnp.sum(p * dp, axis=1, keepdims=True)) * (MEM_HD ** -0.5)).astype(BF16)
            dq_ref[:, sl] = _nn(ds, kvv[:, sl]).astype(BF16)
            dkv_ref[:, sl] += _tn(ds, qb_[:, sl])
            dkv_ref[:, vsl] += _tn(p.astype(BF16), dom[:, sl])
        dh = _nt(dq_ref[...], wq_ref[...])
        dx, dg = _norm_bwd(x_ref[...], g_ref[...], dh)
        dx1_ref[...] = dxv + dx
        dg_ref[...] += dg

    return pl.pallas_call(
        body,
        name="xattn_bwd",
        grid=(T // tm,),
        in_specs=[
            pl.BlockSpec((tm, D), lambda i: (i, 0)),
            pl.BlockSpec((tm, D), lambda i: (i, 0)),
            pl.BlockSpec((1, D), lambda i: (0, 0)),
            pl.BlockSpec((tm, D), lambda i: (i, 0)),
            pl.BlockSpec((M, 2 * D), lambda i: (0, 0)),
            pl.BlockSpec((D, D), lambda i: (0, 0)),
            pl.BlockSpec((D, D), lambda i: (0, 0)),
        ],
        out_specs=[
            pl.BlockSpec((tm, D), lambda i: (i, 0)),
            pl.BlockSpec((tm, D), lambda i: (i, 0)),
            pl.BlockSpec((M, 2 * D), lambda i: (0, 0)),
            pl.BlockSpec((1, D), lambda i: (0, 0)),
        ],
        out_shape=[
            jax.ShapeDtypeStruct((T, D), F32),
            jax.ShapeDtypeStruct((T, D), BF16),
            jax.ShapeDtypeStruct((M, 2 * D), F32),
            jax.ShapeDtypeStruct((1, D), F32),
        ],
        compiler_params=_cparams(("arbitrary",)),
    )(dx2, x1, g2, qb, kv, w_mo, w_mq)


def _mem_kv_bwd(mem, gm, mb, dkv, w_mkv):
    M, D = mem.shape
    N = dkv.shape[1]

    def body(mem_ref, g_ref, m_ref, dkv_ref, w_ref, dw_ref, dg_ref):
        dkvb = dkv_ref[...].astype(BF16)
        for n0 in range(0, N, 512):
            dw_ref[:, n0:n0 + 512] = _tn(m_ref[...], dkvb[:, n0:n0 + 512]).astype(BF16)
        dm = _nt(dkvb, w_ref[...])
        mv = mem_ref[...]
        dg_ref[...] = jnp.sum(dm * (mv * _rstd(mv)), axis=0, keepdims=True)

    return pl.pallas_call(
        body,
        name="mem_kv_bwd",
        out_shape=[jax.ShapeDtypeStruct((D, N), BF16), jax.ShapeDtypeStruct((1, D), F32)],
        compiler_params=_cparams(),
    )(mem, gm, mb, dkv, w_mkv)


def _post_attn_bwd(dx1, fox_o, sb_o, gf, gs, w_out, tm):
    T, D = dx1.shape

    def body(dx_ref, f_ref, s_ref, gf_ref, gs_ref, w_ref, df_ref, ds_ref, dgf_ref, dgs_ref):
        i = pl.program_id(0)

        @pl.when(i == 0)
        def _():
            dgf_ref[...] = jnp.zeros_like(dgf_ref)
            dgs_ref[...] = jnp.zeros_like(dgs_ref)

        dmix = _nt(dx_ref[...].astype(BF16), w_ref[...])
        d, dg = _norm_bwd(f_ref[...], gf_ref[...], dmix[:, :FOX_W])
        df_ref[...] = d
        dgf_ref[...] += dg
        d, dg = _norm_bwd(s_ref[...], gs_ref[...], dmix[:, FOX_W:])
        ds_ref[...] = d
        dgs_ref[...] += dg

    return pl.pallas_call(
        body,
        name="post_attn_bwd",
        grid=(T // tm,),
        in_specs=[
            pl.BlockSpec((tm, D), lambda i: (i, 0)),
            pl.BlockSpec((tm, FOX_W), lambda i: (i, 0)),
            pl.BlockSpec((tm, FOX_W), lambda i: (i, 0)),
            pl.BlockSpec((1, FOX_W), lambda i: (0, 0)),
            pl.BlockSpec((1, FOX_W), lambda i: (0, 0)),
            pl.BlockSpec((D, D), lambda i: (0, 0)),
        ],
        out_specs=[
            pl.BlockSpec((tm, FOX_W), lambda i: (i, 0)),
            pl.BlockSpec((tm, FOX_W), lambda i: (i, 0)),
            pl.BlockSpec((1, FOX_W), lambda i: (0, 0)),
            pl.BlockSpec((1, FOX_W), lambda i: (0, 0)),
        ],
        out_shape=[
            jax.ShapeDtypeStruct((T, FOX_W), F32),
            jax.ShapeDtypeStruct((T, FOX_W), F32),
            jax.ShapeDtypeStruct((1, FOX_W), F32),
            jax.ShapeDtypeStruct((1, FOX_W), F32),
        ],
        compiler_params=_cparams(("arbitrary",)),
    )(dx1, fox_o, sb_o, gf, gs, w_out)


def _sb_bwd(proj, ltot, live, d_o, tq):
    T = proj.shape[0]
    nq = T // tq

    def body(q_ref, k_ref, v_ref, lt_ref, live_ref, do_ref, dq_ref, dk_ref, dv_ref,
             qh_s, doh_s, lt_s, z_s, da_s, ab_s, dzb_s, run_s, runw_s, dq_s, qt_s, dot_s, dkt_s, dvt_s):
        i = pl.program_id(1)

        @pl.when(i == 0)
        def _():
            dkt_s[...] = jnp.zeros_like(dkt_s)
            dvt_s[...] = jnp.zeros_like(dvt_s)

        lane = lax.broadcasted_iota(jnp.int32, (1, 128), 1)
        row = lax.broadcasted_iota(jnp.int32, (tq, tq), 0)
        col = lax.broadcasted_iota(jnp.int32, (tq, tq), 1)
        strict = col < row
        upto = jnp.where(row <= col, 1.0, 0.0).astype(BF16)
        before = jnp.where(row < col, 1.0, 0.0).astype(BF16)
        q = q_ref[...]
        dov = do_ref[...]
        for hh in range(2):
            qh, hmask = _head_q(q, hh, lane)
            qh_s[hh] = -qh
            doh_s[hh] = jnp.where(hmask, dov, 0.0).astype(BF16)
            lt_s[hh] = jnp.broadcast_to(_lanes_to_rows(lt_ref[hh], row == col), (tq, 128))
        qt_s[...] = (q.astype(F32) * -(HEAD_DIM ** -0.5)).T.astype(BF16)
        dot_s[...] = dov.astype(F32).T.astype(BF16)
        run_s[...] = jnp.zeros_like(run_s)
        runw_s[...] = jnp.zeros_like(runw_s)
        dq_s[...] = jnp.zeros_like(dq_s)

        at = lax.broadcasted_iota(jnp.int32, (1, nq), 1)
        count = jnp.sum(jnp.where(at == i, live_ref[0], 0.0), axis=1, keepdims=True)[0, 0].astype(jnp.int32)
        n_live = jnp.clip(count, 1, i + 1)
        oldest = i + 1 - n_live

        def rows(t):
            return pl.ds(pl.multiple_of((oldest + t) * tq, tq), tq)

        def stage_a(t, slot):
            k = k_ref[rows(t), :]
            v = v_ref[rows(t), :]
            for hh in range(2):
                z_s[slot, hh] = _nt(qh_s[hh], k)
                da_s[slot, hh] = _nt(doh_s[hh], v)

        def stage_b(t, slot, diag):
            for hh in range(2):
                L, g = _sb_logs(z_s[slot, hh], strict if diag else None)
                upto_s = _split_dot(L, upto, SB_SUM_TERMS)
                run = run_s[hh]
                arg = (g + _lanes2(lt_s[hh] - run)) - upto_s
                if diag:
                    arg = jnp.where(strict, arg, NEG)
                a = jnp.exp(arg)
                w = a * da_s[slot, hh]
                w_before = _split_dot(w, before, SB_SUM_TERMS)
                run_w = runw_s[hh]
                d_keep = w_before + _lanes2(run_w)
                beta = jnp.exp(g)
                ndz = beta * (w + d_keep) - w
                if diag:
                    ndz = jnp.where(strict, ndz, 0.0)
                dzb_s[hh] = ndz.astype(BF16)
                ab_s[hh] = a.astype(BF16)
                run_s[hh] = run + jnp.broadcast_to(upto_s[:, tq - 1:tq], (tq, 128))
                runw_s[hh] = run_w + jnp.broadcast_to(w_before[:, tq - 1:tq] + w[:, tq - 1:tq], (tq, 128))

        def stage_c(t):
            k = k_ref[rows(t), :]
            for hh in range(2):
                dzb = dzb_s[hh]
                dq_s[hh] += _nn(dzb, k)
                dims = pl.ds(HEAD_DIM * hh, HEAD_DIM)
                dkt_s[oldest + t, dims, :] += _nn(qt_s[dims, :], dzb)
                dvt_s[oldest + t, dims, :] += _nn(dot_s[dims, :], ab_s[hh])

        _pipeline3(n_live, stage_a, stage_b, stage_c, True)
        dq_ref[...] = (jnp.where(lane < HEAD_DIM, dq_s[0], dq_s[1]) * -(HEAD_DIM ** -0.5)).astype(BF16)

        @pl.when(i == nq - 1)
        def _():
            def flush(n, carry):
                keys = pl.ds(pl.multiple_of(n * tq, tq), tq)
                dk_ref[keys, :] = dkt_s[n].T
                dv_ref[keys, :] = dvt_s[n].T
                return carry

            lax.fori_loop(0, nq, flush, 0)

    return pl.pallas_call(
        body,
        name="sb_bwd",
        grid=(4, nq),
        in_specs=[
            pl.BlockSpec((tq, 128), lambda p, i: (i, 12 + p)),
            pl.BlockSpec((T, 128), lambda p, i: (0, 16 + p)),
            pl.BlockSpec((T, 128), lambda p, i: (0, 20 + p)),
            pl.BlockSpec((2, 1, tq), lambda p, i: (p, 0, i)),
            pl.BlockSpec((1, 1, nq), lambda p, i: (p, 0, 0)),
            pl.BlockSpec((tq, 128), lambda p, i: (i, p)),
        ],
        out_specs=[
            pl.BlockSpec((tq, 128), lambda p, i: (i, p)),
            pl.BlockSpec((T, 128), lambda p, i: (0, p)),
            pl.BlockSpec((T, 128), lambda p, i: (0, p)),
        ],
        out_shape=[
            jax.ShapeDtypeStruct((T, FOX_W), BF16),
            jax.ShapeDtypeStruct((T, FOX_W), F32),
            jax.ShapeDtypeStruct((T, FOX_W), F32),
        ],
        scratch_shapes=[
            pltpu.VMEM((2, tq, 128), BF16),
            pltpu.VMEM((2, tq, 128), BF16),
            pltpu.VMEM((2, tq, 128), F32),
            pltpu.VMEM((2, 2, tq, tq), F32),
            pltpu.VMEM((2, 2, tq, tq), F32),
            pltpu.VMEM((2, tq, tq), BF16),
            pltpu.VMEM((2, tq, tq), BF16),
            pltpu.VMEM((2, tq, 128), F32),
            pltpu.VMEM((2, tq, 128), F32),
            pltpu.VMEM((2, tq, 128), F32),
            pltpu.VMEM((128, tq), BF16),
            pltpu.VMEM((128, tq), BF16),
            pltpu.VMEM((nq, 128, tq), F32),
            pltpu.VMEM((nq, 128, tq), F32),
        ],
        compiler_params=_cparams(("arbitrary", "arbitrary")),
    )(proj, proj, proj, ltot, live, d_o)


def _fox_bwd(proj, c_col, c_row, c_ends, lse, d_o, o, tq, scatter=()):
    T = proj.shape[0]
    nq = T // tq
    ns = len(scatter)

    def body(*refs):
        q_ref, k_ref, v_ref, cq_ref, ck_ref, cke_ref, lse_ref, do_ref, o_ref = refs[:9]
        dq_ref, dk_ref, dv_ref, dck_ref, dcq_ref = refs[9 + ns:14 + ns]
        (qh_s, doh_s, delta_s, shift_s, z_s, dp_s, pb_s, dsb_s, rs_s, dq_s,
         kn_s, qt_s, dot_s, dkt_s, dvt_s) = refs[14 + 2 * ns:29 + 2 * ns]
        i = pl.program_id(1)
        if ns:
            pair = pl.program_id(0)
            finish = _ride_along(_Scatter(refs[9:9 + ns], refs[14 + ns:14 + 2 * ns], *refs[29 + 2 * ns:]),
                                 (pair == 0) & (i == 0), None, (pair == 3) & (i == nq - 1))
        lane = lax.broadcasted_iota(jnp.int32, (1, 128), 1)

        @pl.when(i == 0)
        def _():
            dkt_s[...] = jnp.zeros_like(dkt_s)
            dvt_s[...] = jnp.zeros_like(dvt_s)
            dck_ref[...] = jnp.zeros_like(dck_ref)
            _fox_key_norms(k_ref, kn_s, lane)

        row = lax.broadcasted_iota(jnp.int32, (tq, tq), 0)
        col = lax.broadcasted_iota(jnp.int32, (tq, tq), 1)
        q = q_ref[...]
        dov = do_ref[...]
        ov = o_ref[...]
        qt_s[...] = (q.astype(F32) * (HEAD_DIM ** -0.5)).T.astype(BF16)
        dot_s[...] = dov.astype(F32).T.astype(BF16)
        for hh in range(2):
            qh, hmask = _head_q(q, hh, lane)
            dohb = jnp.where(hmask, dov, 0.0).astype(BF16)
            qh_s[hh] = qh
            doh_s[hh] = dohb
            delta_s[hh] = jnp.broadcast_to(jnp.sum(dohb.astype(F32) * ov, axis=1, keepdims=True), (tq, tq))
            shift_s[hh] = jnp.broadcast_to(_lanes_to_rows(cq_ref[hh] - lse_ref[hh], row == col), (tq, tq))
        rs_s[...] = jnp.zeros_like(rs_s)
        dq_s[...] = jnp.zeros_like(dq_s)

        def rows(t):
            return pl.ds(pl.multiple_of((i - t) * tq, tq), tq)

        def stage_a(t, slot):
            k = k_ref[rows(t), :]
            v = v_ref[rows(t), :]
            for hh in range(2):
                z_s[slot, hh] = _nt(qh_s[hh], k)
                dp_s[slot, hh] = _nt(doh_s[hh], v)

        def stage_b(t, slot, diag):
            for hh in range(2):
                s = z_s[slot, hh] + shift_s[hh] - ck_ref[hh, :, rows(t)]
                if diag:
                    s = jnp.where(col <= row, s, NEG)
                p = jnp.exp(s)
                ds = p * (dp_s[slot, hh] - delta_s[hh])
                pb_s[hh] = p.astype(BF16)
                dsb_s[hh] = ds.astype(BF16)
                dck_ref[hh, :, rows(t)] += jnp.sum(ds, axis=0, keepdims=True)
                rs_s[hh] += jnp.sum(ds, axis=1, keepdims=True)

        def stage_c(t):
            k = k_ref[rows(t), :]
            for hh in range(2):
                dsb = dsb_s[hh]
                dq_s[hh] += _nn(dsb, k)
                dims = pl.ds(HEAD_DIM * hh, HEAD_DIM)
                dkt_s[i - t, dims, :] += _nn(qt_s[dims, :], dsb)
                dvt_s[i - t, dims, :] += _nn(dot_s[dims, :], pb_s[hh])

        _pipeline3(_fox_live_blocks(i, qh_s, kn_s, cq_ref, cke_ref), stage_a, stage_b, stage_c, False)

        @pl.when(i == nq - 1)
        def _():
            def flush(n, carry):
                keys = pl.ds(pl.multiple_of(n * tq, tq), tq)
                dk_ref[keys, :] = dkt_s[n].T
                dv_ref[keys, :] = dvt_s[n].T
                return carry

            lax.fori_loop(0, nq, flush, 0)
        dcq_ref[0] = _rows_to_lanes(rs_s[0], row == col)
        dcq_ref[1] = _rows_to_lanes(rs_s[1], row == col)
        dq_ref[...] = (jnp.where(lane < HEAD_DIM, dq_s[0], dq_s[1]) * (HEAD_DIM ** -0.5)).astype(BF16)
        if ns:
            finish()

    res = pl.pallas_call(
        body,
        name="fox_bwd",
        grid=(4, nq),
        in_specs=[
            pl.BlockSpec((tq, 128), lambda p, i: (i, p)),
            pl.BlockSpec((T, 128), lambda p, i: (0, 4 + p)),
            pl.BlockSpec((T, 128), lambda p, i: (0, 8 + p)),
            pl.BlockSpec((2, 1, tq), lambda p, i: (p, 0, i)),
            pl.BlockSpec((2, 1, T), lambda p, i: (p, 0, 0)),
            pl.BlockSpec((2, 1, nq), lambda p, i: (p, 0, 0)),
            pl.BlockSpec((2, 1, tq), lambda p, i: (p, 0, i)),
            pl.BlockSpec((tq, 128), lambda p, i: (i, p)),
            pl.BlockSpec((tq, 128), lambda p, i: (i, p)),
        ] + [_ANY] * ns,
        out_specs=[
            pl.BlockSpec((tq, 128), lambda p, i: (i, p)),
            pl.BlockSpec((T, 128), lambda p, i: (0, p)),
            pl.BlockSpec((T, 128), lambda p, i: (0, p)),
            pl.BlockSpec((2, 1, T), lambda p, i: (p, 0, 0)),
            pl.BlockSpec((2, 1, tq), lambda p, i: (p, 0, i)),
        ] + [_ANY] * ns,
        out_shape=[
            jax.ShapeDtypeStruct((T, FOX_W), BF16),
            jax.ShapeDtypeStruct((T, FOX_W), F32),
            jax.ShapeDtypeStruct((T, FOX_W), F32),
            jax.ShapeDtypeStruct((N_FOX, 1, T), F32),
            jax.ShapeDtypeStruct((N_FOX, 1, T), F32),
        ] + [jax.ShapeDtypeStruct(b.shape, b.dtype) for b in scatter],
        scratch_shapes=[
            pltpu.VMEM((2, tq, 128), BF16),
            pltpu.VMEM((2, tq, 128), BF16),
            pltpu.VMEM((2, tq, tq), F32),
            pltpu.VMEM((2, tq, tq), F32),
            pltpu.VMEM((2, 2, tq, tq), F32),
            pltpu.VMEM((2, 2, tq, tq), F32),
            pltpu.VMEM((2, tq, tq), BF16),
            pltpu.VMEM((2, tq, tq), BF16),
            pltpu.VMEM((2, tq, 128), F32),
            pltpu.VMEM((2, tq, 128), F32),
            pltpu.VMEM((2, 8, 128), F32),
            pltpu.VMEM((128, tq), BF16),
            pltpu.VMEM((128, tq), BF16),
            pltpu.VMEM((nq, 128, tq), F32),
            pltpu.VMEM((nq, 128, tq), F32),
        ] + (_comm_sems(ns) if ns else []),
        compiler_params=_cparams(("arbitrary", "arbitrary")),
    )(proj, proj, proj, c_col, c_row, c_ends, lse, d_o, o, *scatter)
    res = list(res)
    return (*res[:5], res[5:])


def _forget_bwd(dcq, dck, xf, h1, tc):
    H, T = xf.shape
    D = h1.shape[1]
    nc = T // tc

    def body(dcq_ref, dck_ref, xf_ref, h_ref, dxf_ref, db_ref, dwf_ref):
        row = lax.broadcasted_iota(jnp.int32, (tc, tc), 0)
        col = lax.broadcasted_iota(jnp.int32, (tc, tc), 1)
        from_here = jnp.where(row >= col, 1.0, 0.0).astype(BF16)

        def chunk(n, carry):
            run, db, dwf = carry
            cs = pl.multiple_of((nc - 1 - n) * tc, tc)
            dc = dcq_ref[:, pl.ds(cs, tc)] - dck_ref[:, pl.ds(cs, tc)]
            dlogf = _split_dot(dc, from_here, 3) + run
            xfv = xf_ref[:, pl.ds(cs, tc)]
            dxf = dlogf * jax.nn.sigmoid(-xfv)
            dxf_ref[:, pl.ds(cs, tc)] = dxf
            dwf = dwf + _nn(dxf.astype(BF16), h_ref[pl.ds(cs, tc), :])
            return dlogf[:, 0:1], db + jnp.sum(dxf, axis=1, keepdims=True), dwf

        zero = jnp.zeros((H, 1), F32)
        _, db, dwf = lax.fori_loop(0, nc, chunk, (zero, zero, jnp.zeros((H, D), F32)))
        db_ref[...] = db
        dwf_ref[...] = dwf

    return pl.pallas_call(
        body,
        name="forget_bwd",
        out_shape=[jax.ShapeDtypeStruct((H, T), F32), jax.ShapeDtypeStruct((H, 1), F32),
                   jax.ShapeDtypeStruct((H, D), F32)],
        compiler_params=_cparams(),
    )(dcq, dck, xf, h1)


def _inproj_bwd(pieces, dxf_t, w_in, w_f_t, x, g1, dx1, tm, scatter=()):
    T, D = x.shape
    N = w_in.shape[1]
    ns = len(scatter)
    nt = T // tm
    npc = len(pieces)

    def body(*refs):
        pc_refs = refs[:npc]
        dxf_ref, w_ref, wf_ref, x_ref, g_ref, dx1_ref = refs[npc:npc + 6]
        base = npc + 6
        dx_ref, dg_ref = refs[base + ns:base + 2 + ns]
        i = pl.program_id(0)
        if ns:
            exchange = _Scatter(refs[base:base + ns], refs[base + 2 + ns:base + 2 + 2 * ns],
                                *refs[base + 2 + 2 * ns:])

            @pl.when(i == 0)
            def _():
                exchange.start()

        @pl.when(i == 0)
        def _():
            dg_ref[...] = jnp.zeros_like(dg_ref)

        dh = _nn(dxf_ref[...], wf_ref[...].astype(F32))
        for k, pc_ref in enumerate(pc_refs):
            dh = dh + _nt(pc_ref[...].astype(BF16), w_ref[:, k * FOX_W:(k + 1) * FOX_W])
        dx, dg = _norm_bwd(x_ref[...], g_ref[...], dh)
        dx_ref[...] = dx1_ref[...] + dx
        dg_ref[...] += dg
        if ns:
            @pl.when(i == nt - 1)
            def _():
                exchange.finish()

    res = pl.pallas_call(
        body,
        name="inproj_bwd",
        grid=(nt,),
        in_specs=[pl.BlockSpec((tm, FOX_W), lambda i: (i, 0))] * npc + [
            pl.BlockSpec((tm, N_FOX), lambda i: (i, 0)),
            pl.BlockSpec((D, N), lambda i: (0, 0)),
            pl.BlockSpec((N_FOX, D), lambda i: (0, 0)),
            pl.BlockSpec((tm, D), lambda i: (i, 0)),
            pl.BlockSpec((1, D), lambda i: (0, 0)),
            pl.BlockSpec((tm, D), lambda i: (i, 0)),
        ] + [_ANY] * ns,
        out_specs=[
            pl.BlockSpec((tm, D), lambda i: (i, 0)),
            pl.BlockSpec((1, D), lambda i: (0, 0)),
        ] + [_ANY] * ns,
        out_shape=[jax.ShapeDtypeStruct((T, D), F32), jax.ShapeDtypeStruct((1, D), F32)]
        + [jax.ShapeDtypeStruct(b.shape, b.dtype) for b in scatter],
        scratch_shapes=_comm_sems(ns) if ns else [],
        compiler_params=_cparams(("arbitrary",)),
    )(*pieces, dxf_t, w_in, w_f_t, x, g1, dx1, *scatter)
    res = list(res)
    return res[0], res[1], res[2:]


def _dw_in(h1, pieces, name):
    T, K = h1.shape
    bt = min(T, 512)
    nt = T // bt
    npc = len(pieces)

    def body(*refs):
        a_ref = refs[0]
        pc_refs = refs[1:1 + npc]
        o_ref, acc_ref = refs[1 + npc:]
        t = pl.program_id(0)

        @pl.when(t == 0)
        def _():
            acc_ref[...] = jnp.zeros_like(acc_ref)

        a = a_ref[...]
        for k, pc_ref in enumerate(pc_refs):
            acc_ref[:, k * FOX_W:(k + 1) * FOX_W] += _tn(a, pc_ref[...].astype(BF16))

        @pl.when(t == nt - 1)
        def _():
            o_ref[...] = acc_ref[...].astype(BF16)

    return pl.pallas_call(
        body,
        name=name,
        grid=(nt,),
        in_specs=[pl.BlockSpec((bt, K), lambda t: (t, 0))] + [pl.BlockSpec((bt, FOX_W), lambda t: (t, 0))] * npc,
        out_specs=pl.BlockSpec((K, npc * FOX_W), lambda t: (0, 0)),
        out_shape=jax.ShapeDtypeStruct((K, npc * FOX_W), BF16),
        scratch_shapes=[pltpu.VMEM((K, npc * FOX_W), F32)],
        compiler_params=_cparams(("arbitrary",)),
    )(h1, *pieces)


def _matmul_tn(a, b, name, cast_b=False):
    T, K = a.shape
    N = b.shape[1]
    bt = min(T, 512)
    bk = _tile_div(K, 1536)
    bn = _tile_div(N, 1536)
    nt = T // bt

    def body(a_ref, b_ref, o_ref, acc_ref):
        t = pl.program_id(2)

        @pl.when(t == 0)
        def _():
            acc_ref[...] = jnp.zeros_like(acc_ref)

        bv = b_ref[...]
        if cast_b:
            bv = bv.astype(BF16)
        acc_ref[...] += _tn(a_ref[...], bv)

        @pl.when(t == nt - 1)
        def _():
            o_ref[...] = acc_ref[...].astype(BF16)

    return pl.pallas_call(
        body,
        name=name,
        grid=(K // bk, N // bn, nt),
        in_specs=[
            pl.BlockSpec((bt, bk), lambda k, n, t: (t, k)),
            pl.BlockSpec((bt, bn), lambda k, n, t: (t, n)),
        ],
        out_specs=pl.BlockSpec((bk, bn), lambda k, n, t: (k, n)),
        out_shape=jax.ShapeDtypeStruct((K, N), BF16),
        scratch_shapes=[pltpu.VMEM((bk, bn), F32)],
        compiler_params=_cparams(("arbitrary", "arbitrary", "arbitrary")),
    )(a, b)


def _local_step(x, mem, target, p, tm, tq, late=None):
    T, D = x.shape
    w_in = p["w_in"]
    w_qkv = w_in[:, :QKV_W]
    w_f_t = w_in[:, QKV_W:].T
    b_f = p["b_forget"].reshape(N_FOX, 1)

    proj, h1, xf, c = _inproj_fwd(x, p["attn_norm_g"], w_qkv, w_f_t, b_f, tm)
    c_col = c.reshape(N_FOX, 1, T)
    c_row = c.reshape(N_FOX, 1, T)
    c_ends = c[:, tq - 1::tq].reshape(N_FOX, 1, T // tq)
    fox_o, lse, gathered = _fox_fwd(proj, c_col, c_row, c_ends, tq, gather=[late[n] for n in _LATE] if late else ())
    if late:
        p = dict(p, **{n: _gathered_full(n, gv) for n, gv in zip(_LATE, gathered)})
    sb_o, sb_ltot, sb_live = _sb_fwd(proj, tq)
    x1, mixed = _post_attn_fwd(fox_o, sb_o, p["fox_out_g"], p["sb_out_g"], p["w_out"], x, tm)
    mb, kv = _mem_kv_fwd(mem, p["mem_norm_g"], p["w_mkv"])
    x2, h2, qb, om = _xattn_fwd(x1, p["xattn_norm_g"], p["w_mq"], kv, p["w_mo"], tm)
    tf = 2 * tm if T % (2 * tm) == 0 else tm
    x3, h3, ug, uv, yg, yv, a = _ffn_fwd(
        x2, p["ffn_norm_g"], p["w_up"], p["conv_w"], p["conv_b"], p["w_down"], tf)
    dx3, loss_blk, d_final_g = _loss_head(x3, p["final_norm_g"], target, tm)

    g = {"final_norm_g": d_final_g}
    dx2, du_g, du_v, g["ffn_norm_g"], dc_g, dc_v = _ffn_bwd(
        dx3, x2, p["ffn_norm_g"], ug, uv, yg, yv, p["conv_w"], p["w_down"], p["w_up"], tf)
    g["w_down"] = _matmul_tn(a, dx3, "dw_down", cast_b=True)
    g["w_up"] = jnp.concatenate([_matmul_tn(h3, du_g, "dw_up_gate"), _matmul_tn(h3, du_v, "dw_up_val")], axis=1)
    dconv = jnp.concatenate([dc_g, dc_v], axis=1)
    g["conv_w"] = dconv[0:3]
    g["conv_b"] = dconv[3:4]
    dx1, dq_m, dkv, g["xattn_norm_g"] = _xattn_bwd(dx2, x1, p["xattn_norm_g"], qb, kv, p["w_mo"], p["w_mq"], tm)
    g["w_mo"] = _matmul_tn(om, dx2, "dw_mo", cast_b=True)
    g["w_mq"] = _matmul_tn(h2, dq_m, "dw_mq")
    g["w_mkv"], g["mem_norm_g"] = _mem_kv_bwd(mem, p["mem_norm_g"], mb, dkv, p["w_mkv"])
    d_fox, d_sb, g["fox_out_g"], g["sb_out_g"] = _post_attn_bwd(
        dx1, fox_o, sb_o, p["fox_out_g"], p["sb_out_g"], p["w_out"], tm)
    g["w_out"] = _matmul_tn(mixed, dx1, "dw_out", cast_b=True)
    dq_s, dk_s, dv_s = _sb_bwd(proj, sb_ltot, sb_live, d_sb, tq)
    dq_f, dk_f, dv_f, dck, dcq, parts = _fox_bwd(
        proj, c_col, c_row, c_ends, lse, d_fox, fox_o, tq,
        scatter=[_grad_blocks(n, g[n]) for n in _LATE] if late else ())
    if late:
        g["parts"] = dict(zip(_LATE, parts))
    dxf, db, dwf_t = _forget_bwd(dcq.reshape(N_FOX, T), dck.reshape(N_FOX, T), xf, h1, min(T, 512))
    g["b_forget"] = db.reshape(1, N_FOX)
    pieces = [dq_f, dk_f, dv_f, dq_s, dk_s, dv_s]
    g["w_in"] = jnp.concatenate([_dw_in(h1, pieces, "dw_in"), dwf_t.T.astype(BF16)], axis=1)
    grad_x, g["attn_norm_g"], parts = _inproj_bwd(
        pieces, dxf.T, w_in, w_f_t, x, p["attn_norm_g"], dx1, tm,
        scatter=[_grad_blocks("w_in", g["w_in"])] if late else ())
    if late:
        (g["parts"]["w_in"],) = parts
    return loss_blk, grad_x, g


def _mesh_pos():
    return lax.axis_index("x"), lax.axis_index("y"), lax.axis_index("c")


def _flip(pos, k):
    return tuple(1 - v if (k >> b) & 1 else v for v, b in zip(pos, (2, 1, 0)))


def _slot(pos):
    return 4 * pos[0] + 2 * pos[1] + pos[2]


_CHIPS = (4, 2, 6)


def _comm_sems(n):
    return [pltpu.SemaphoreType.DMA((7 * n,)), pltpu.SemaphoreType.DMA((7 * n,)), pltpu.SemaphoreType.DMA((n,))]


class _Gather:
    def __init__(self, ins, outs, send_sems, recv_sems, local_sems):
        self.ins, self.outs, self.n = ins, outs, len(ins)
        self.send_sems, self.recv_sems, self.local_sems = send_sems, recv_sems, local_sems
        self.me = _mesh_pos()
        self.sibling = _flip(self.me, 1)

    def _copy(self, a, kk, block, to, src=None):
        rows = self.outs[a].at[_slot(block)]
        return pltpu.make_async_remote_copy(
            src_ref=rows if src is None else src, dst_ref=rows,
            send_sem=self.send_sems.at[7 * a + kk], recv_sem=self.recv_sems.at[7 * a + kk],
            device_id=to, device_id_type=MESH)

    def _mine(self):
        return [pltpu.make_async_copy(self.ins[a], self.outs[a].at[_slot(self.me)], self.local_sems.at[a])
                for a in range(self.n)]

    def _first(self):
        out = []
        for a in range(self.n):
            out.append(self._copy(a, 0, self.me, self.sibling, src=self.ins[a]))
            out += [self._copy(a, 1 + j, self.me, _flip(self.me, k), src=self.ins[a]) for j, k in enumerate(_CHIPS)]
        return out

    def _passed(self):
        return [self._copy(a, 4 + j, _flip(self.me, k), self.sibling)
                for j, k in enumerate(_CHIPS) for a in range(self.n)]

    def start(self):
        for cp in self._mine() + self._first():
            cp.start()

    def forward(self):
        for j, k in enumerate(_CHIPS):
            for a in range(self.n):
                self._copy(a, 1 + j, _flip(self.me, k), self.me).wait_recv()
                self._copy(a, 4 + j, _flip(self.me, k), self.sibling).start()

    def finish(self):
        for a in range(self.n):
            self._copy(a, 0, self.sibling, self.me).wait_recv()
            for j, k in enumerate(_CHIPS):
                self._copy(a, 4 + j, _flip(self.sibling, k), self.me).wait_recv()
        for cp in self._first() + self._passed():
            cp.wait_send()
        for cp in self._mine():
            cp.wait()


class _Scatter:
    def __init__(self, ins, outs, send_sems, recv_sems, local_sems):
        self.ins, self.outs, self.n = ins, outs, len(ins)
        self.send_sems, self.recv_sems, self.local_sems = send_sems, recv_sems, local_sems
        self.me = _mesh_pos()

    def _copy(self, a, k, landed=False):
        peer = _flip(self.me, k)
        return pltpu.make_async_remote_copy(
            src_ref=self.ins[a].at[_slot(peer)], dst_ref=self.outs[a].at[_slot(peer if landed else self.me)],
            send_sem=self.send_sems.at[7 * a + k - 1], recv_sem=self.recv_sems.at[7 * a + k - 1],
            device_id=peer, device_id_type=MESH)

    def _mine(self):
        s = _slot(self.me)
        return [pltpu.make_async_copy(self.ins[a].at[s], self.outs[a].at[s], self.local_sems.at[a])
                for a in range(self.n)]

    def start(self):
        for cp in self._mine() + [self._copy(a, k) for k in range(1, 8) for a in range(self.n)]:
            cp.start()

    def finish(self):
        for k in range(1, 8):
            for a in range(self.n):
                self._copy(a, k, landed=True).wait_recv()
        for k in range(1, 8):
            for a in range(self.n):
                self._copy(a, k).wait_send()
        for cp in self._mine():
            cp.wait()


_ANY = pl.BlockSpec(memory_space=pl.ANY)


def _gathered_shapes(shards):
    return [jax.ShapeDtypeStruct((N_DEV,) + s.shape, s.dtype) for s in shards]


def _all_gather(shards, name):
    n = len(shards)

    def body(*refs):
        g = _Gather(refs[:n], refs[n:2 * n], *refs[2 * n:])
        g.start()
        g.forward()
        g.finish()

    return pl.pallas_call(
        body, name=name, in_specs=[_ANY] * n, out_specs=[_ANY] * n,
        out_shape=_gathered_shapes(shards), scratch_shapes=_comm_sems(n),
    )(*shards)


def _adamw_math(w, g, m, v):
    m2 = ADAM_B1 * m + (1.0 - ADAM_B1) * g
    v2 = ADAM_B2 * v + (1.0 - ADAM_B2) * (g * g)
    m_hat = m2 / (1.0 - ADAM_B1 ** ADAM_STEP)
    v_hat = v2 / (1.0 - ADAM_B2 ** ADAM_STEP)
    delta = -ADAM_LR * (m_hat / (jnp.sqrt(v_hat) + ADAM_EPS) + ADAM_WD * w)
    return delta, m2, v2


def _adamw(w, parts, m, v, name):
    R, C = w.shape
    br = 128 if R % 128 == 0 else R

    def body(w_ref, p_ref, m_ref, v_ref, g_ref, d_ref, nm_ref, nv_ref):
        g = p_ref[0].astype(F32)
        for s in range(1, N_DEV):
            g = g + p_ref[s].astype(F32)
        g_ref[...] = g
        d_ref[...], nm_ref[...], nv_ref[...] = _adamw_math(w_ref[...], g, m_ref[...], v_ref[...])

    spec = pl.BlockSpec((br, C), lambda i: (i, 0))
    return pl.pallas_call(
        body,
        name=name,
        grid=(R // br,),
        in_specs=[spec, pl.BlockSpec((N_DEV, br, C), lambda i: (0, i, 0)), spec, spec],
        out_specs=[spec] * 4,
        out_shape=[jax.ShapeDtypeStruct((R, C), F32)] * 4,
        compiler_params=_cparams(("arbitrary",)),
    )(w, parts, m, v)


_SHARDED = ("w_in", "w_out", "w_mq", "w_mkv", "w_mo", "w_up", "conv_w", "w_down")
_LATE = _SHARDED[1:]
_COL_SHARDED = ("w_in", "w_mkv", "w_up", "conv_w")
_REPLICATED = ("attn_norm_g", "b_forget", "fox_out_g", "sb_out_g", "xattn_norm_g", "mem_norm_g",
               "ffn_norm_g", "conv_b", "final_norm_g")
_WEIGHTS = ("attn_norm_g", "w_in", "b_forget", "fox_out_g", "sb_out_g", "w_out", "xattn_norm_g", "mem_norm_g",
            "w_mq", "w_mkv", "w_mo", "ffn_norm_g", "w_up", "conv_w", "conv_b", "w_down", "final_norm_g")


def _pack_rows(n):
    return -(-n // 128)


def _pack(vals, rows_total):
    parts = []
    for v in vals:
        flat = v.reshape(-1)
        parts.append(jnp.pad(flat, (0, _pack_rows(flat.shape[0]) * 128 - flat.shape[0])))
    flat = jnp.concatenate(parts)
    return jnp.pad(flat, (0, rows_total * 128 - flat.shape[0])).reshape(rows_total, 128)


def _unpack(packed, shapes):
    out = []
    r = 0
    for shp in shapes:
        n = 1
        for d in shp:
            n *= d
        out.append(packed[r:r + _pack_rows(n)].reshape(-1)[:n].reshape(shp))
        r += _pack_rows(n)
    return out


def _gathered_full(name, gathered):
    if name in _COL_SHARDED:
        return jnp.transpose(gathered, (1, 0, 2)).reshape(gathered.shape[1], -1)
    return gathered.reshape(-1, gathered.shape[2])


def _to_blocks(name, full):
    if name in _COL_SHARDED:
        r = full.shape[0]
        return jnp.transpose(full.reshape(r, N_DEV, -1), (1, 0, 2))
    return full.reshape(N_DEV, -1, full.shape[1])


def _grad_blocks(name, full):
    blocks = _to_blocks(name, full)
    return blocks if name == "conv_w" else blocks.astype(BF16)


def _step(args, tm, tq):
    w = {n: args[n] for n in _WEIGHTS}
    mom = {n: args["m_" + n] for n in _WEIGHTS}
    var = {n: args["v_" + n] for n in _WEIGHTS}
    x = args["x"][0]
    mem = args["mem"][0]
    target = args["loss_target"][0]

    def flat2(a):
        return a.reshape(a.shape[-2], a.shape[-1]) if a.ndim == 3 else a.reshape(1, -1)

    shards = {n: flat2(w[n]) if n == "conv_w" else flat2(w[n]).astype(BF16) for n in _SHARDED}
    (w_in_all,) = _all_gather([shards["w_in"]], "gather_w_in")
    p = {"w_in": _gathered_full("w_in", w_in_all)}
    for n in _REPLICATED:
        p[n] = flat2(w[n])

    loss_blk, grad_x, g = _local_step(x, mem, target, p, tm, tq, late={n: shards[n] for n in _LATE})

    parts = g["parts"]
    out = {}
    for n in _SHARDED:
        res = _adamw(flat2(w[n]), parts[n], flat2(mom[n]), flat2(var[n]), "adamw_" + n)
        out[n] = [r.reshape(w[n].shape) for r in res]

    shapes = [w[n].shape for n in _REPLICATED]
    rows = sum(_pack_rows(flat2(w[n]).shape[1]) for n in _REPLICATED) + 1
    rows = -(-rows // 8) * 8
    g_pack = _pack([g[n] for n in _REPLICATED] + [loss_blk[0:1, :]], rows)
    (g_all,) = _all_gather([g_pack], "gather_small")
    res = _adamw(_pack([w[n] for n in _REPLICATED], rows), g_all,
                 _pack([mom[n] for n in _REPLICATED], rows), _pack([var[n] for n in _REPLICATED], rows),
                 "adamw_small")
    n_rows_params = sum(_pack_rows(flat2(w[n]).shape[1]) for n in _REPLICATED)
    loss = res[0][n_rows_params, 0]
    unpacked = [_unpack(r, shapes) for r in res]
    for k, n in enumerate(_REPLICATED):
        out[n] = [unpacked[q][k] for q in range(4)]

    grads = [out[n][0] for n in _WEIGHTS]
    deltas = [out[n][1] for n in _WEIGHTS]
    new_m = [out[n][2] for n in _WEIGHTS]
    new_v = [out[n][3] for n in _WEIGHTS]
    return (loss, grad_x[None], *grads, *deltas, *new_m, *new_v)


def kernel(x, mem, attn_norm_g, w_in, b_forget, fox_out_g, sb_out_g, w_out, xattn_norm_g, mem_norm_g, w_mq, w_mkv, w_mo, ffn_norm_g, w_up, conv_w, conv_b, w_down, final_norm_g, loss_target, m_attn_norm_g, m_w_in, m_b_forget, m_fox_out_g, m_sb_out_g, m_w_out, m_xattn_norm_g, m_mem_norm_g, m_w_mq, m_w_mkv, m_w_mo, m_ffn_norm_g, m_w_up, m_conv_w, m_conv_b, m_w_down, m_final_norm_g, v_attn_norm_g, v_w_in, v_b_forget, v_fox_out_g, v_sb_out_g, v_w_out, v_xattn_norm_g, v_mem_norm_g, v_w_mq, v_w_mkv, v_w_mo, v_ffn_norm_g, v_w_up, v_conv_w, v_conv_b, v_w_down, v_final_norm_g):
    args = dict(locals())
    T = x.shape[1]
    return _step(args, tm=min(T, 512), tq=min(T, 256))
```

```python
import functools

import jax
import jax.numpy as jnp
from jax import lax
from jax.experimental import pallas as pl
from jax.experimental.pallas import tpu as pltpu

F32 = jnp.float32
BF16 = jnp.bfloat16
EPS = 1e-6
NEG = -1e30
LOG2E = 1.4426950408889634

HEAD_DIM = 64
N_FOX = 8
FOX_W = 512
QKV_W = 3072
N_MEM_HEADS = 4
MEM_HD = 256
D_FF = 2816
FF_CHUNK = 256
N_DEV = 8

ADAM_LR = 0.001
ADAM_B1 = 0.9
ADAM_B2 = 0.999
ADAM_EPS = 1e-08
ADAM_WD = 0.01
ADAM_STEP = 10

SB_SUM_TERMS = 1

VMEM_LIMIT = 56 * 1024 * 1024
MESH = pl.DeviceIdType.MESH


def _cparams(sem=None):
    return pltpu.CompilerParams(dimension_semantics=sem, vmem_limit_bytes=VMEM_LIMIT)


def _nt(a, b):
    return lax.dot_general(a, b, (((1,), (1,)), ((), ())), preferred_element_type=F32)


def _tn(a, b):
    return lax.dot_general(a, b, (((0,), (0,)), ((), ())), preferred_element_type=F32)


def _nn(a, b):
    return jnp.dot(a, b, preferred_element_type=F32)


def _split_dot(a, m01, terms):
    out = None
    r = a
    for t in range(terms):
        p = r.astype(BF16)
        d = _nn(p, m01)
        out = d if out is None else out + d
        if t + 1 < terms:
            r = r - p.astype(F32)
    return out


def _rstd(xv):
    return lax.rsqrt(jnp.mean(xv * xv, axis=-1, keepdims=True) + EPS)


def _norm_bwd(xv, g, dh):
    r = _rstd(xv)
    xhat = xv * r
    dxhat = dh * g
    dx = r * (dxhat - xhat * jnp.mean(dxhat * xhat, axis=-1, keepdims=True))
    dg = jnp.sum(dh * xhat, axis=0, keepdims=True)
    return dx, dg


def _tile_div(n, cap):
    best = None
    for d in range(128, min(n, cap) + 1, 128):
        if n % d == 0:
            best = d
    assert best is not None, n
    return best


def _inproj_fwd(x, g1, w_qkv, w_f_t, b_f, tm):
    T, D = x.shape
    N = w_qkv.shape[1]
    H = w_f_t.shape[0]

    def body(x_ref, g_ref, w_ref, wf_ref, b_ref, proj_ref, h_ref, xf_ref, c_ref, carry_ref):
        i = pl.program_id(0)

        @pl.when(i == 0)
        def _():
            carry_ref[...] = jnp.zeros_like(carry_ref)

        xv = x_ref[...]
        h = (xv * _rstd(xv) * g_ref[...]).astype(BF16)
        h_ref[...] = h
        for n0 in range(0, N, 512):
            proj_ref[:, n0:n0 + 512] = _nn(h, w_ref[:, n0:n0 + 512]).astype(BF16)
        xf = _nt(wf_ref[...], h) + b_ref[...]
        xf_ref[...] = xf
        logf = jnp.minimum(xf, 0.0) - jnp.log1p(jnp.exp(-jnp.abs(xf)))
        row = lax.broadcasted_iota(jnp.int32, (tm, tm), 0)
        col = lax.broadcasted_iota(jnp.int32, (tm, tm), 1)
        upper = jnp.where(row <= col, 1.0, 0.0).astype(BF16)
        c = _split_dot(logf, upper, 3) + carry_ref[...]
        c_ref[...] = c
        carry_ref[...] = c[:, tm - 1:tm]

    return pl.pallas_call(
        body,
        name="inproj_fwd",
        grid=(T // tm,),
        in_specs=[
            pl.BlockSpec((tm, D), lambda i: (i, 0)),
            pl.BlockSpec((1, D), lambda i: (0, 0)),
            pl.BlockSpec((D, N), lambda i: (0, 0)),
            pl.BlockSpec((H, D), lambda i: (0, 0)),
            pl.BlockSpec((H, 1), lambda i: (0, 0)),
        ],
        out_specs=[
            pl.BlockSpec((tm, N), lambda i: (i, 0)),
            pl.BlockSpec((tm, D), lambda i: (i, 0)),
            pl.BlockSpec((H, tm), lambda i: (0, i)),
            pl.BlockSpec((H, tm), lambda i: (0, i)),
        ],
        out_shape=[
            jax.ShapeDtypeStruct((T, N), BF16),
            jax.ShapeDtypeStruct((T, D), BF16),
            jax.ShapeDtypeStruct((H, T), F32),
            jax.ShapeDtypeStruct((H, T), F32),
        ],
        scratch_shapes=[pltpu.VMEM((H, 1), F32)],
        compiler_params=_cparams(("arbitrary",)),
    )(x, g1, w_qkv, w_f_t, b_f)


def _head_q(q, hh, lane):
    hmask = (lane >= HEAD_DIM * hh) & (lane < HEAD_DIM * (hh + 1))
    qh = jnp.where(hmask, q.astype(F32), 0.0) * (HEAD_DIM ** -0.5)
    return qh.astype(BF16), hmask


def _pipeline3(n, stage_a, stage_b, stage_c, diag_last, alive=None, a_first=False):
    stage_a(0, 0)
    if diag_last:
        @pl.when(n == 1)
        def _():
            stage_b(0, 0, True)

        @pl.when(n >= 2)
        def _():
            stage_b(0, 0, False)
    else:
        stage_b(0, 0, True)

    @pl.when(n >= 2)
    def _():
        stage_a(1, 1)

    def pair(m, carry):
        t = 2 + 2 * m
        if a_first:
            stage_a(t, 0)
            stage_b(t - 1, 1, False)
            stage_c(t - 2, 0)
            stage_a(t + 1, 1)
            stage_b(t, 0, False)
            stage_c(t - 1, 1)
        else:
            stage_c(t - 2, 0)
            stage_b(t - 1, 1, False)
            stage_a(t, 0)
            stage_c(t - 1, 1)
            stage_b(t, 0, False)
            stage_a(t + 1, 1)
        return carry

    pairs = (n - 2) // 2
    if alive is None:
        lax.fori_loop(0, pairs, pair, 0)
        go_on = True
        done = n
    else:
        def more(state):
            return (state[0] < pairs) & state[1]

        def step(state):
            pair(state[0], 0)
            return state[0] + 1, alive()

        m_end, go_on = lax.while_loop(more, step, (jnp.int32(0), jnp.bool_(True)))
        done = jnp.where(go_on, n, 2 * m_end)
    odd = n % 2 == 1

    @pl.when((n >= 3) & odd & go_on)
    def _():
        stage_c(n - 3, 0)
        stage_b(n - 2, 1, False)
        stage_a(n - 1, 0)

    @pl.when((n >= 2) & odd & go_on)
    def _():
        stage_c(n - 2, 1)
        stage_b(n - 1, 0, diag_last)

    @pl.when(odd & go_on)
    def _():
        stage_c(n - 1, 0)

    @pl.when(jnp.logical_not(odd) & go_on)
    def _():
        stage_c(n - 2, 0)
        stage_b(n - 1, 1, diag_last)
        stage_c(n - 1, 1)

    return done


def _lanes2(x):
    return jnp.concatenate([x, x], axis=1)


def _lanes_to_rows(vec, eye):
    return jnp.sum(jnp.where(eye, jnp.broadcast_to(vec, eye.shape), 0.0), axis=1, keepdims=True)


def _rows_to_lanes(rep, eye):
    return jnp.sum(jnp.where(eye, _lanes2(rep), 0.0), axis=0, keepdims=True)


FOX_DEAD = -110.0


def _fox_key_norms(k_ref, kn_s, lane):
    T = k_ref.shape[0]
    rows = min(T, 512)
    for hh in range(2):
        hmask = (lane >= HEAD_DIM * hh) & (lane < HEAD_DIM * (hh + 1))

        def chunk(n, best, hmask=hmask):
            kf = jnp.where(hmask, k_ref[pl.ds(pl.multiple_of(n * rows, rows), rows), :].astype(F32), 0.0)
            sq = jnp.sum(kf * kf, axis=1, keepdims=True)
            return jnp.maximum(best, jnp.max(sq, axis=0, keepdims=True))

        best = lax.fori_loop(0, T // rows, chunk, jnp.zeros((1, 1), F32))
        kn_s[hh] = jnp.broadcast_to(best, kn_s.shape[1:])


def _fox_live_blocks(i, qh_s, kn_s, cq_ref, cke_ref):
    nq = cke_ref.shape[-1]
    jj = lax.broadcasted_iota(jnp.int32, (1, nq), 1)
    first = None
    for hh in range(2):
        qf = qh_s[hh].astype(F32)
        qn = jnp.max(jnp.sum(qf * qf, axis=1, keepdims=True), axis=0, keepdims=True)
        zb = jnp.sqrt(qn * kn_s[hh][0:1, 0:1]) * 1.001
        bound = (2.0 * zb + cq_ref[hh][:, 0:1]) - cke_ref[hh]
        live = (bound >= FOX_DEAD) & (jj <= i)
        f = jnp.min(jnp.where(live, jj, i).astype(F32), axis=1, keepdims=True)
        first = f if first is None else jnp.minimum(first, f)
    return i + 1 - first[0, 0].astype(jnp.int32)


def _ride_along(exchange, at_start, at_middle, at_end):
    @pl.when(at_start)
    def _():
        exchange.start()

    if at_middle is not None:
        @pl.when(at_middle)
        def _():
            exchange.forward()

    def finish():
        @pl.when(at_end)
        def _():
            exchange.finish()

    return finish


def _fox_fwd(proj, c_col, c_row, c_ends, tq, gather=()):
    T = proj.shape[0]
    assert tq == 256
    nq = T // tq
    ng = len(gather)

    def body(*refs):
        q_ref, k_ref, v_ref, cq_ref, ck_ref, cke_ref = refs[:6]
        o_ref, lse_ref = refs[6 + ng:8 + ng]
        qh_s, cq_s, z_s, p_s, al_s, m_s, acc_s, kn_s = refs[8 + 2 * ng:16 + 2 * ng]
        i = pl.program_id(1)
        if ng:
            pair = pl.program_id(0)
            finish = _ride_along(_Gather(refs[6:6 + ng], refs[8 + ng:8 + 2 * ng], *refs[16 + 2 * ng:]),
                                 (pair == 0) & (i == 0), (pair == 1) & (i == 0), (pair == 3) & (i == nq - 1))
        lane = lax.broadcasted_iota(jnp.int32, (1, 128), 1)
        row = lax.broadcasted_iota(jnp.int32, (tq, tq), 0)
        col = lax.broadcasted_iota(jnp.int32, (tq, tq), 1)

        @pl.when(i == 0)
        def _():
            _fox_key_norms(k_ref, kn_s, lane)

        q = q_ref[...]
        for hh in range(2):
            qh_s[hh] = _head_q(q, hh, lane)[0]
            cq_s[hh] = jnp.broadcast_to(_lanes_to_rows(cq_ref[hh], row == col), (tq, tq))
        m_s[...] = jnp.full(m_s.shape, NEG, F32)
        acc_s[...] = jnp.zeros_like(acc_s)

        def rows(t):
            return pl.ds(pl.multiple_of((i - t) * tq, tq), tq)

        def stage_a(t, slot):
            k = k_ref[rows(t), :]
            for hh in range(2):
                z_s[slot, hh] = _nt(qh_s[hh], k)

        def stage_b(t, slot, diag):
            for hh in range(2):
                s = z_s[slot, hh] + cq_s[hh] - ck_ref[hh, :, rows(t)]
                if diag:
                    s = jnp.where(col <= row, s, NEG)
                m = m_s[hh]
                half = jnp.maximum(s[:, :128], s[:, 128:])
                m_new = jnp.maximum(m, jnp.max(half, axis=1, keepdims=True))
                m_s[hh] = m_new
                al_s[slot, hh] = jnp.exp(m - m_new)
                p_s[slot, hh] = jnp.exp(s - _lanes2(m_new)).astype(BF16)

        def stage_c(t, slot):
            v = v_ref[rows(t), :]
            for hh in range(2):
                own = (lane >= HEAD_DIM * hh) & (lane < HEAD_DIM * (hh + 1))
                acc_s[hh] = (al_s[slot, hh] * acc_s[hh]
                             + _nn(p_s[slot, hh], jnp.where(own, v, 1.0).astype(BF16)))

        _pipeline3(_fox_live_blocks(i, qh_s, kn_s, cq_ref, cke_ref), stage_a, stage_b, stage_c, False)
        halves = []
        for hh in range(2):
            acc = acc_s[hh]
            own = (lane >= HEAD_DIM * hh) & (lane < HEAD_DIM * (hh + 1))
            halves.append(jnp.where(own, pltpu.roll(acc, HEAD_DIM, axis=1), acc))
        l0, l1 = halves
        o_ref[...] = jnp.where(lane < HEAD_DIM, acc_s[0] / l0, acc_s[1] / l1)
        lse_ref[0] = _rows_to_lanes(m_s[0] + jnp.log(l0), row == col)
        lse_ref[1] = _rows_to_lanes(m_s[1] + jnp.log(l1), row == col)
        if ng:
            finish()

    res = pl.pallas_call(
        body,
        name="fox_fwd",
        grid=(4, nq),
        in_specs=[
            pl.BlockSpec((tq, 128), lambda p, i: (i, p)),
            pl.BlockSpec((T, 128), lambda p, i: (0, 4 + p)),
            pl.BlockSpec((T, 128), lambda p, i: (0, 8 + p)),
            pl.BlockSpec((2, 1, tq), lambda p, i: (p, 0, i)),
            pl.BlockSpec((2, 1, T), lambda p, i: (p, 0, 0)),
            pl.BlockSpec((2, 1, nq), lambda p, i: (p, 0, 0)),
        ] + [_ANY] * ng,
        out_specs=[
            pl.BlockSpec((tq, 128), lambda p, i: (i, p)),
            pl.BlockSpec((2, 1, tq), lambda p, i: (p, 0, i)),
        ] + [_ANY] * ng,
        out_shape=[
            jax.ShapeDtypeStruct((T, FOX_W), F32),
            jax.ShapeDtypeStruct((N_FOX, 1, T), F32),
        ] + _gathered_shapes(gather),
        scratch_shapes=[
            pltpu.VMEM((2, tq, 128), BF16),
            pltpu.VMEM((2, tq, tq), F32),
            pltpu.VMEM((2, 2, tq, tq), F32),
            pltpu.VMEM((2, 2, tq, tq), BF16),
            pltpu.VMEM((2, 2, tq, 128), F32),
            pltpu.VMEM((2, tq, 128), F32),
            pltpu.VMEM((2, tq, 128), F32),
            pltpu.VMEM((2, 8, 128), F32),
        ] + (_comm_sems(ng) if ng else []),
        compiler_params=_cparams(("arbitrary", "arbitrary")),
    )(proj, proj, proj, c_col, c_row, c_ends, *gather)
    res = list(res)
    return res[0], res[1], res[2:]


def _sb_logs(zn, strict):
    e = jnp.exp2(jnp.abs(zn) * (-LOG2E))
    L = jnp.minimum(zn, 0.0) - jnp.log(1.0 + e)
    G = L - zn
    if strict is not None:
        L = jnp.where(strict, L, 0.0)
    return L, G


SB_DEAD = -110.0


def _sb_fwd(proj, tq):
    T = proj.shape[0]
    nq = T // tq

    def body(q_ref, k_ref, v_ref, o_ref, ltot_ref, live_ref, qh_s, z_s, g_s, tot_s, run_s, acc_s):
        i = pl.program_id(1)
        lane = lax.broadcasted_iota(jnp.int32, (1, 128), 1)
        row = lax.broadcasted_iota(jnp.int32, (tq, tq), 0)
        col = lax.broadcasted_iota(jnp.int32, (tq, tq), 1)
        strict = col < row
        later = jnp.where(row > col, 1.0, 0.0).astype(BF16)
        q = q_ref[...]
        for hh in range(2):
            qh_s[hh] = -_head_q(q, hh, lane)[0]
        run_s[...] = jnp.zeros_like(run_s)
        acc_s[...] = jnp.zeros_like(acc_s)

        def rows(t):
            return pl.ds(pl.multiple_of((i - t) * tq, tq), tq)

        def stage_a(t, slot):
            k = k_ref[rows(t), :]
            for hh in range(2):
                z_s[slot, hh] = _nt(qh_s[hh], k)

        def stage_b(t, slot, diag):
            for hh in range(2):
                L, g = _sb_logs(z_s[slot, hh], strict if diag else None)
                if diag:
                    g = jnp.where(strict, g, NEG)
                after = _split_dot(L, later, SB_SUM_TERMS)
                g_s[slot, hh] = g + after
                first = L[:, 0:1]
                if SB_SUM_TERMS == 1:
                    first = first.astype(BF16).astype(F32)
                tot_s[slot, hh] = jnp.broadcast_to(after[:, 0:1] + first, (tq, 128))

        def stage_c(t, slot):
            v = v_ref[rows(t), :]
            for hh in range(2):
                run = run_s[hh]
                a = jnp.exp(g_s[slot, hh] + _lanes2(run))
                acc_s[hh] += _nn(a.astype(BF16), v)
                run_s[hh] = run + tot_s[slot, hh]

        def alive():
            return jnp.max(jnp.maximum(run_s[0], run_s[1])) > SB_DEAD

        done = _pipeline3(i + 1, stage_a, stage_b, stage_c, False, alive)
        ltot_ref[0] = _rows_to_lanes(run_s[0], row == col)
        ltot_ref[1] = _rows_to_lanes(run_s[1], row == col)
        o_ref[...] = jnp.where(lane < HEAD_DIM, acc_s[0], acc_s[1])
        at = lax.broadcasted_iota(jnp.int32, (1, nq), 1)

        @pl.when(i == 0)
        def _():
            live_ref[0] = jnp.zeros((1, nq), F32)

        live_ref[0] = jnp.where(at == i, done.astype(F32), live_ref[0])

    return pl.pallas_call(
        body,
        name="sb_fwd",
        grid=(4, nq),
        in_specs=[
            pl.BlockSpec((tq, 128), lambda p, i: (i, 12 + p)),
            pl.BlockSpec((T, 128), lambda p, i: (0, 16 + p)),
            pl.BlockSpec((T, 128), lambda p, i: (0, 20 + p)),
        ],
        out_specs=[
            pl.BlockSpec((tq, 128), lambda p, i: (i, p)),
            pl.BlockSpec((2, 1, tq), lambda p, i: (p, 0, i)),
            pl.BlockSpec((1, 1, nq), lambda p, i: (p, 0, 0)),
        ],
        out_shape=[
            jax.ShapeDtypeStruct((T, FOX_W), F32),
            jax.ShapeDtypeStruct((N_FOX, 1, T), F32),
            jax.ShapeDtypeStruct((N_FOX // 2, 1, nq), F32),
        ],
        scratch_shapes=[
            pltpu.VMEM((2, tq, 128), BF16),
            pltpu.VMEM((2, 2, tq, tq), F32),
            pltpu.VMEM((2, 2, tq, tq), F32),
            pltpu.VMEM((2, 2, tq, 128), F32),
            pltpu.VMEM((2, tq, 128), F32),
            pltpu.VMEM((2, tq, 128), F32),
        ],
        compiler_params=_cparams(("arbitrary", "arbitrary")),
    )(proj, proj, proj)


def _post_attn_fwd(fox_o, sb_o, gf, gs, w_out, x, tm):
    T, D = x.shape

    def body(f_ref, s_ref, gf_ref, gs_ref, w_ref, x_ref, x1_ref, mix_ref):
        f = f_ref[...]
        s = s_ref[...]
        mix_ref[:, :FOX_W] = (f * _rstd(f) * gf_ref[...]).astype(BF16)
        mix_ref[:, FOX_W:] = (s * _rstd(s) * gs_ref[...]).astype(BF16)
        x1_ref[...] = x_ref[...] + _nn(mix_ref[...], w_ref[...])

    return pl.pallas_call(
        body,
        name="post_attn_fwd",
        grid=(T // tm,),
        in_specs=[
            pl.BlockSpec((tm, FOX_W), lambda i: (i, 0)),
            pl.BlockSpec((tm, FOX_W), lambda i: (i, 0)),
            pl.BlockSpec((1, FOX_W), lambda i: (0, 0)),
            pl.BlockSpec((1, FOX_W), lambda i: (0, 0)),
            pl.BlockSpec((D, D), lambda i: (0, 0)),
            pl.BlockSpec((tm, D), lambda i: (i, 0)),
        ],
        out_specs=[
            pl.BlockSpec((tm, D), lambda i: (i, 0)),
            pl.BlockSpec((tm, D), lambda i: (i, 0)),
        ],
        out_shape=[jax.ShapeDtypeStruct((T, D), F32), jax.ShapeDtypeStruct((T, D), BF16)],
        compiler_params=_cparams(("arbitrary",)),
    )(fox_o, sb_o, gf, gs, w_out, x)


def _mem_kv_fwd(mem, gm, w_mkv):
    M, D = mem.shape
    N = w_mkv.shape[1]

    def body(mem_ref, g_ref, w_ref, m_ref, kv_ref):
        mv = mem_ref[...]
        m = (mv * _rstd(mv) * g_ref[...]).astype(BF16)
        m_ref[...] = m
        for n0 in range(0, N, 512):
            kv_ref[:, n0:n0 + 512] = _nn(m, w_ref[:, n0:n0 + 512]).astype(BF16)

    return pl.pallas_call(
        body,
        name="mem_kv_fwd",
        out_shape=[jax.ShapeDtypeStruct((M, D), BF16), jax.ShapeDtypeStruct((M, N), BF16)],
        compiler_params=_cparams(),
    )(mem, gm, w_mkv)


def _xattn_probs(qb, kv, h):
    k = kv[:, h * MEM_HD:(h + 1) * MEM_HD]
    s = _nt(qb[:, h * MEM_HD:(h + 1) * MEM_HD], k) * (MEM_HD ** -0.5)
    s = s - jnp.max(s, axis=1, keepdims=True)
    p = jnp.exp(s)
    return p / jnp.sum(p, axis=1, keepdims=True)


def _xattn_fwd(x1, g2, w_mq, kv, w_mo, tm):
    T, D = x1.shape
    M = kv.shape[0]

    def body(x_ref, g_ref, wq_ref, kv_ref, wo_ref, x2_ref, h_ref, q_ref, om_ref):
        xv = x_ref[...]
        h = (xv * _rstd(xv) * g_ref[...]).astype(BF16)
        h_ref[...] = h
        q_ref[...] = _nn(h, wq_ref[...]).astype(BF16)
        qb = q_ref[...]
        kvv = kv_ref[...]
        for hd in range(N_MEM_HEADS):
            p = _xattn_probs(qb, kvv, hd)
            v = kvv[:, D + hd * MEM_HD:D + (hd + 1) * MEM_HD]
            om_ref[:, hd * MEM_HD:(hd + 1) * MEM_HD] = _nn(p.astype(BF16), v).astype(BF16)
        x2_ref[...] = xv + _nn(om_ref[...], wo_ref[...])

    return pl.pallas_call(
        body,
        name="xattn_fwd",
        grid=(T // tm,),
        in_specs=[
            pl.BlockSpec((tm, D), lambda i: (i, 0)),
            pl.BlockSpec((1, D), lambda i: (0, 0)),
            pl.BlockSpec((D, D), lambda i: (0, 0)),
            pl.BlockSpec((M, 2 * D), lambda i: (0, 0)),
            pl.BlockSpec((D, D), lambda i: (0, 0)),
        ],
        out_specs=[pl.BlockSpec((tm, D), lambda i: (i, 0))] * 4,
        out_shape=[jax.ShapeDtypeStruct((T, D), F32)] + [jax.ShapeDtypeStruct((T, D), BF16)] * 3,
        compiler_params=_cparams(("arbitrary",)),
    )(x1, g2, w_mq, kv, w_mo)


def _conv_taps(ext_ref, tm, back):
    if back:
        return ext_ref[pl.ds(6, tm), :], ext_ref[pl.ds(7, tm), :], ext_ref[pl.ds(8, tm), :]
    return ext_ref[pl.ds(0, tm), :], ext_ref[pl.ds(1, tm), :], ext_ref[pl.ds(2, tm), :]


def _ffn_fwd(x2, g3, w_up, conv_w, conv_b, w_down, tm):
    T, D = x2.shape
    fc = FF_CHUNK
    nj = D_FF // fc

    def body(x_ref, g_ref, wg_ref, wv_ref, cwg_ref, cwv_ref, cbg_ref, cbv_ref, wd_ref,
             x3_ref, h_ref, ug_ref, uv_ref, yg_ref, yv_ref, a_ref, acc_ref, carry_ref, ext_ref):
        i = pl.program_id(0)
        j = pl.program_id(1)

        @pl.when(j == 0)
        def _():
            xv = x_ref[...]
            h_ref[...] = (xv * _rstd(xv) * g_ref[...]).astype(BF16)
            acc_ref[...] = xv

        @pl.when(i == 0)
        def _():
            carry_ref[j] = jnp.zeros((2, 8, fc), F32)

        h = h_ref[...]
        halves = []
        for part, (w_ref, cw_ref, cb_ref, u_ref, y_ref) in enumerate(
                ((wg_ref, cwg_ref, cbg_ref, ug_ref, yg_ref), (wv_ref, cwv_ref, cbv_ref, uv_ref, yv_ref))):
            u = _nn(h, w_ref[...])
            u_ref[...] = u.astype(BF16)
            ext = ext_ref.at[part]
            ext[pl.ds(0, 8), :] = carry_ref[j, part]
            ext[pl.ds(8, tm), :] = u
            carry_ref[j, part] = u[tm - 8:, :]
            u2, u1, u0 = _conv_taps(ext, tm, True)
            cw = cw_ref[...]
            y = cb_ref[...] + cw[0:1] * u2 + cw[1:2] * u1 + cw[2:3] * u0
            y_ref[...] = y.astype(BF16)
            halves.append(y)
        gate, val = halves
        a = (gate * jax.nn.sigmoid(gate) * val).astype(BF16)
        a_ref[...] = a
        acc_ref[...] += _nn(a, wd_ref[...])

        @pl.when(j == nj - 1)
        def _():
            x3_ref[...] = acc_ref[...]

    return pl.pallas_call(
        body,
        name="ffn_fwd",
        grid=(T // tm, nj),
        in_specs=[
            pl.BlockSpec((tm, D), lambda i, j: (i, 0)),
            pl.BlockSpec((1, D), lambda i, j: (0, 0)),
            pl.BlockSpec((D, fc), lambda i, j: (0, j)),
            pl.BlockSpec((D, fc), lambda i, j: (0, nj + j)),
            pl.BlockSpec((3, fc), lambda i, j: (0, j)),
            pl.BlockSpec((3, fc), lambda i, j: (0, nj + j)),
            pl.BlockSpec((1, fc), lambda i, j: (0, j)),
            pl.BlockSpec((1, fc), lambda i, j: (0, nj + j)),
            pl.BlockSpec((fc, D), lambda i, j: (j, 0)),
        ],
        out_specs=[
            pl.BlockSpec((tm, D), lambda i, j: (i, 0)),
            pl.BlockSpec((tm, D), lambda i, j: (i, 0)),
        ] + [pl.BlockSpec((tm, fc), lambda i, j: (i, j))] * 5,
        out_shape=[
            jax.ShapeDtypeStruct((T, D), F32),
            jax.ShapeDtypeStruct((T, D), BF16),
        ] + [jax.ShapeDtypeStruct((T, D_FF), BF16)] * 5,
        scratch_shapes=[
            pltpu.VMEM((tm, D), F32),
            pltpu.VMEM((nj, 2, 8, fc), F32),
            pltpu.VMEM((2, tm + 8, fc), F32),
        ],
        compiler_params=_cparams(("arbitrary", "arbitrary")),
    )(x2, g3, w_up, w_up, conv_w, conv_w, conv_b, conv_b, w_down)


def _loss_head(x3, gfin, target, tm):
    T, D = x3.shape

    def body(x_ref, g_ref, t_ref, dx_ref, loss_ref, dg_ref):
        i = pl.program_id(0)

        @pl.when(i == 0)
        def _():
            loss_ref[...] = jnp.zeros_like(loss_ref)
            dg_ref[...] = jnp.zeros_like(dg_ref)

        xv = x_ref[...]
        g = g_ref[...]
        r = _rstd(xv)
        xhat = xv * r
        err = xhat * g - t_ref[...]
        part = jnp.sum(jnp.sum(err * err, axis=1, keepdims=True), axis=0, keepdims=True) * (0.5 / D)
        loss_ref[...] += jnp.broadcast_to(part, loss_ref.shape)
        dy = err * (1.0 / D)
        dg_ref[...] += jnp.sum(dy * xhat, axis=0, keepdims=True)
        dxhat = dy * g
        dx_ref[...] = r * (dxhat - xhat * jnp.mean(dxhat * xhat, axis=-1, keepdims=True))

    return pl.pallas_call(
        body,
        name="loss_head",
        grid=(T // tm,),
        in_specs=[
            pl.BlockSpec((tm, D), lambda i: (i, 0)),
            pl.BlockSpec((1, D), lambda i: (0, 0)),
            pl.BlockSpec((tm, D), lambda i: (i, 0)),
        ],
        out_specs=[
            pl.BlockSpec((tm, D), lambda i: (i, 0)),
            pl.BlockSpec((8, 128), lambda i: (0, 0)),
            pl.BlockSpec((1, D), lambda i: (0, 0)),
        ],
        out_shape=[
            jax.ShapeDtypeStruct((T, D), F32),
            jax.ShapeDtypeStruct((8, 128), F32),
            jax.ShapeDtypeStruct((1, D), F32),
        ],
        compiler_params=_cparams(("arbitrary",)),
    )(x3, gfin, target)


def _ffn_bwd(dx3, x2, g3, ug, uv, yg, yv, conv_w, w_down, w_up, tm):
    T, D = x2.shape
    fc = FF_CHUNK
    nj = D_FF // fc
    nt = T // tm

    def rev(i):
        return nt - 1 - i

    def body(dx3_ref, x_ref, g_ref, ug_ref, uv_ref, yg_ref, yv_ref, cwg_ref, cwv_ref,
             wd_ref, wug_ref, wuv_ref,
             dx2_ref, dug_ref, duv_ref, dg_ref, dcg_ref, dcv_ref,
             acc_ref, carry_ref, ext_ref):
        i = pl.program_id(0)
        j = pl.program_id(1)
        cols = pl.ds(pl.multiple_of(j * fc, fc), fc)

        @pl.when(j == 0)
        def _():
            acc_ref[...] = jnp.zeros_like(acc_ref)

        @pl.when((i == 0) & (j == 0))
        def _():
            dg_ref[...] = jnp.zeros_like(dg_ref)
            dcg_ref[...] = jnp.zeros_like(dcg_ref)
            dcv_ref[...] = jnp.zeros_like(dcv_ref)

        @pl.when(i == 0)
        def _():
            carry_ref[j] = jnp.zeros((2, 8, fc), F32)

        da = _nt(dx3_ref[...].astype(BF16), wd_ref[...])
        gate = yg_ref[...].astype(F32)
        val = yv_ref[...].astype(F32)
        sig = jax.nn.sigmoid(gate)
        silu = gate * sig
        dys = (da * val * (sig * (1.0 + gate * (1.0 - sig))), da * silu)
        for part, (dy, u_ref, cw_ref, du_ref, wu_ref, dc_ref) in enumerate(
                ((dys[0], ug_ref, cwg_ref, dug_ref, wug_ref, dcg_ref),
                 (dys[1], uv_ref, cwv_ref, duv_ref, wuv_ref, dcv_ref))):
            ext = ext_ref.at[part]
            ext[pl.ds(0, tm), :] = dy
            ext[pl.ds(tm, 8), :] = carry_ref[j, part]
            carry_ref[j, part] = dy[:8, :]
            d0, d1, d2 = _conv_taps(ext, tm, False)
            u = u_ref[...].astype(F32)
            upd = jnp.concatenate([
                jnp.sum(u * d2, axis=0, keepdims=True),
                jnp.sum(u * d1, axis=0, keepdims=True),
                jnp.sum(u * d0, axis=0, keepdims=True),
                jnp.sum(d0, axis=0, keepdims=True),
                jnp.zeros((4, fc), F32)], axis=0)
            dc_ref[:, cols] += upd
            cw = cw_ref[...]
            du = (cw[2:3] * d0 + cw[1:2] * d1 + cw[0:1] * d2).astype(BF16)
            du_ref[...] = du
            acc_ref[...] += _nt(du, wu_ref[...])

        @pl.when(j == nj - 1)
        def _():
            dx, dg = _norm_bwd(x_ref[...], g_ref[...], acc_ref[...])
            dx2_ref[...] = dx3_ref[...] + dx
            dg_ref[...] += dg

    return pl.pallas_call(
        body,
        name="ffn_bwd",
        grid=(nt, nj),
        in_specs=[
            pl.BlockSpec((tm, D), lambda i, j: (rev(i), 0)),
            pl.BlockSpec((tm, D), lambda i, j: (rev(i), 0)),
            pl.BlockSpec((1, D), lambda i, j: (0, 0)),
            pl.BlockSpec((tm, fc), lambda i, j: (rev(i), j)),
            pl.BlockSpec((tm, fc), lambda i, j: (rev(i), j)),
            pl.BlockSpec((tm, fc), lambda i, j: (rev(i), j)),
            pl.BlockSpec((tm, fc), lambda i, j: (rev(i), j)),
            pl.BlockSpec((3, fc), lambda i, j: (0, j)),
            pl.BlockSpec((3, fc), lambda i, j: (0, nj + j)),
            pl.BlockSpec((fc, D), lambda i, j: (j, 0)),
            pl.BlockSpec((D, fc), lambda i, j: (0, j)),
            pl.BlockSpec((D, fc), lambda i, j: (0, nj + j)),
        ],
        out_specs=[
            pl.BlockSpec((tm, D), lambda i, j: (rev(i), 0)),
            pl.BlockSpec((tm, fc), lambda i, j: (rev(i), j)),
            pl.BlockSpec((tm, fc), lambda i, j: (rev(i), j)),
            pl.BlockSpec((1, D), lambda i, j: (0, 0)),
            pl.BlockSpec((8, D_FF), lambda i, j: (0, 0)),
            pl.BlockSpec((8, D_FF), lambda i, j: (0, 0)),
        ],
        out_shape=[
            jax.ShapeDtypeStruct((T, D), F32),
            jax.ShapeDtypeStruct((T, D_FF), BF16),
            jax.ShapeDtypeStruct((T, D_FF), BF16),
            jax.ShapeDtypeStruct((1, D), F32),
            jax.ShapeDtypeStruct((8, D_FF), F32),
            jax.ShapeDtypeStruct((8, D_FF), F32),
        ],
        scratch_shapes=[
            pltpu.VMEM((tm, D), F32),
            pltpu.VMEM((nj, 2, 8, fc), F32),
            pltpu.VMEM((2, tm + 8, fc), F32),
        ],
        compiler_params=_cparams(("arbitrary", "arbitrary")),
    )(dx3, x2, g3, ug, uv, yg, yv, conv_w, conv_w, w_down, w_up, w_up)


def _xattn_bwd(dx2, x1, g2, qb, kv, w_mo, w_mq, tm):
    T, D = x1.shape
    M = kv.shape[0]

    def body(dx2_ref, x_ref, g_ref, q_ref, kv_ref, wo_ref, wq_ref, dx1_ref, dq_ref, dkv_ref, dg_ref):
        i = pl.program_id(0)

        @pl.when(i == 0)
        def _():
            dkv_ref[...] = jnp.zeros_like(dkv_ref)
            dg_ref[...] = jnp.zeros_like(dg_ref)

        dxv = dx2_ref[...]
        dom = _nt(dxv.astype(BF16), wo_ref[...]).astype(BF16)
        qb_ = q_ref[...]
        kvv = kv_ref[...]
        for hd in range(N_MEM_HEADS):
            sl = slice(hd * MEM_HD, (hd + 1) * MEM_HD)
            vsl = slice(D + hd * MEM_HD, D + (hd + 1) * MEM_HD)
            p = _xattn_probs(qb_, kvv, hd)
            dp = _nt(dom[:, sl], kvv[:, vsl])
            ds = (p * (dp - jnp.sum(p * dp, axis=1, keepdims=True)) * (MEM_HD ** -0.5)).astype(BF16)
            dq_ref[:, sl] = _nn(ds, kvv[:, sl]).astype(BF16)
            dkv_ref[:, sl] += _tn(ds, qb_[:, sl])
            dkv_ref[:, vsl] += _tn(p.astype(BF16), dom[:, sl])
        dh = _nt(dq_ref[...], wq_ref[...])
        dx, dg = _norm_bwd(x_ref[...], g_ref[...], dh)
        dx1_ref[...] = dxv + dx
        dg_ref[...] += dg

    return pl.pallas_call(
        body,
        name="xattn_bwd",
        grid=(T // tm,),
        in_specs=[
            pl.BlockSpec((tm, D), lambda i: (i, 0)),
            pl.BlockSpec((tm, D), lambda i: (i, 0)),
            pl.BlockSpec((1, D), lambda i: (0, 0)),
            pl.BlockSpec((tm, D), lambda i: (i, 0)),
            pl.BlockSpec((M, 2 * D), lambda i: (0, 0)),
            pl.BlockSpec((D, D), lambda i: (0, 0)),
            pl.BlockSpec((D, D), lambda i: (0, 0)),
        ],
        out_specs=[
            pl.BlockSpec((tm, D), lambda i: (i, 0)),
            pl.BlockSpec((tm, D), lambda i: (i, 0)),
            pl.BlockSpec((M, 2 * D), lambda i: (0, 0)),
            pl.BlockSpec((1, D), lambda i: (0, 0)),
        ],
        out_shape=[
            jax.ShapeDtypeStruct((T, D), F32),
            jax.ShapeDtypeStruct((T, D), BF16),
            jax.ShapeDtypeStruct((M, 2 * D), F32),
            jax.ShapeDtypeStruct((1, D), F32),
        ],
        compiler_params=_cparams(("arbitrary",)),
    )(dx2, x1, g2, qb, kv, w_mo, w_mq)


def _mem_kv_bwd(mem, gm, mb, dkv, w_mkv):
    M, D = mem.shape
    N = dkv.shape[1]

    def body(mem_ref, g_ref, m_ref, dkv_ref, w_ref, dw_ref, dg_ref):
        dkvb = dkv_ref[...].astype(BF16)
        for n0 in range(0, N, 512):
            dw_ref[:, n0:n0 + 512] = _tn(m_ref[...], dkvb[:, n0:n0 + 512]).astype(BF16)
        dm = _nt(dkvb, w_ref[...])
        mv = mem_ref[...]
        dg_ref[...] = jnp.sum(dm * (mv * _rstd(mv)), axis=0, keepdims=True)

    return pl.pallas_call(
        body,
        name="mem_kv_bwd",
        out_shape=[jax.ShapeDtypeStruct((D, N), BF16), jax.ShapeDtypeStruct((1, D), F32)],
        compiler_params=_cparams(),
    )(mem, gm, mb, dkv, w_mkv)


def _post_attn_bwd(dx1, fox_o, sb_o, gf, gs, w_out, tm):
    T, D = dx1.shape

    def body(dx_ref, f_ref, s_ref, gf_ref, gs_ref, w_ref, df_ref, ds_ref, dgf_ref, dgs_ref):
        i = pl.program_id(0)

        @pl.when(i == 0)
        def _():
            dgf_ref[...] = jnp.zeros_like(dgf_ref)
            dgs_ref[...] = jnp.zeros_like(dgs_ref)

        dmix = _nt(dx_ref[...].astype(BF16), w_ref[...])
        d, dg = _norm_bwd(f_ref[...], gf_ref[...], dmix[:, :FOX_W])
        df_ref[...] = d
        dgf_ref[...] += dg
        d, dg = _norm_bwd(s_ref[...], gs_ref[...], dmix[:, FOX_W:])
        ds_ref[...] = d
        dgs_ref[...] += dg

    return pl.pallas_call(
        body,
        name="post_attn_bwd",
        grid=(T // tm,),
        in_specs=[
            pl.BlockSpec((tm, D), lambda i: (i, 0)),
            pl.BlockSpec((tm, FOX_W), lambda i: (i, 0)),
            pl.BlockSpec((tm, FOX_W), lambda i: (i, 0)),
            pl.BlockSpec((1, FOX_W), lambda i: (0, 0)),
            pl.BlockSpec((1, FOX_W), lambda i: (0, 0)),
            pl.BlockSpec((D, D), lambda i: (0, 0)),
        ],
        out_specs=[
            pl.BlockSpec((tm, FOX_W), lambda i: (i, 0)),
            pl.BlockSpec((tm, FOX_W), lambda i: (i, 0)),
            pl.BlockSpec((1, FOX_W), lambda i: (0, 0)),
            pl.BlockSpec((1, FOX_W), lambda i: (0, 0)),
        ],
        out_shape=[
            jax.ShapeDtypeStruct((T, FOX_W), F32),
            jax.ShapeDtypeStruct((T, FOX_W), F32),
            jax.ShapeDtypeStruct((1, FOX_W), F32),
            jax.ShapeDtypeStruct((1, FOX_W), F32),
        ],
        compiler_params=_cparams(("arbitrary",)),
    )(dx1, fox_o, sb_o, gf, gs, w_out)


def _sb_bwd(proj, ltot, live, d_o, tq):
    T = proj.shape[0]
    nq = T // tq

    def body(q_ref, k_ref, v_ref, lt_ref, live_ref, do_ref, dq_ref, dk_ref, dv_ref,
             qh_s, doh_s, lt_s, z_s, da_s, ab_s, dzb_s, run_s, runw_s, dq_s, qt_s, dot_s, dkt_s, dvt_s):
        i = pl.program_id(1)

        @pl.when(i == 0)
        def _():
            dkt_s[...] = jnp.zeros_like(dkt_s)
            dvt_s[...] = jnp.zeros_like(dvt_s)

        lane = lax.broadcasted_iota(jnp.int32, (1, 128), 1)
        row = lax.broadcasted_iota(jnp.int32, (tq, tq), 0)
        col = lax.broadcasted_iota(jnp.int32, (tq, tq), 1)
        strict = col < row
        upto = jnp.where(row <= col, 1.0, 0.0).astype(BF16)
        before = jnp.where(row < col, 1.0, 0.0).astype(BF16)
        q = q_ref[...]
        dov = do_ref[...]
        for hh in range(2):
            qh, hmask = _head_q(q, hh, lane)
            qh_s[hh] = -qh
            doh_s[hh] = jnp.where(hmask, dov, 0.0).astype(BF16)
            lt_s[hh] = jnp.broadcast_to(_lanes_to_rows(lt_ref[hh], row == col), (tq, 128))
        qt_s[...] = (q.astype(F32) * -(HEAD_DIM ** -0.5)).T.astype(BF16)
        dot_s[...] = dov.astype(F32).T.astype(BF16)
        run_s[...] = jnp.zeros_like(run_s)
        runw_s[...] = jnp.zeros_like(runw_s)
        dq_s[...] = jnp.zeros_like(dq_s)

        at = lax.broadcasted_iota(jnp.int32, (1, nq), 1)
        count = jnp.sum(jnp.where(at == i, live_ref[0], 0.0), axis=1, keepdims=True)[0, 0].astype(jnp.int32)
        n_live = jnp.clip(count, 1, i + 1)
        oldest = i + 1 - n_live

        def rows(t):
            return pl.ds(pl.multiple_of((oldest + t) * tq, tq), tq)

        def stage_a(t, slot):
            k = k_ref[rows(t), :]
            v = v_ref[rows(t), :]
            for hh in range(2):
                z_s[slot, hh] = _nt(qh_s[hh], k)
                da_s[slot, hh] = _nt(doh_s[hh], v)

        def stage_b(t, slot, diag):
            for hh in range(2):
                L, g = _sb_logs(z_s[slot, hh], strict if diag else None)
                upto_s = _split_dot(L, upto, SB_SUM_TERMS)
                run = run_s[hh]
                arg = (g + _lanes2(lt_s[hh] - run)) - upto_s
                if diag:
                    arg = jnp.where(strict, arg, NEG)
                a = jnp.exp(arg)
                w = a * da_s[slot, hh]
                w_before = _split_dot(w, before, SB_SUM_TERMS)
                run_w = runw_s[hh]
                d_keep = w_before + _lanes2(run_w)
                beta = jnp.exp(g)
                ndz = beta * (w + d_keep) - w
                if diag:
                    ndz = jnp.where(strict, ndz, 0.0)
                dzb_s[slot, hh] = ndz.astype(BF16)
                ab_s[slot, hh] = a.astype(BF16)
                run_s[hh] = run + jnp.broadcast_to(upto_s[:, tq - 1:tq], (tq, 128))
                runw_s[hh] = run_w + jnp.broadcast_to(w_before[:, tq - 1:tq] + w[:, tq - 1:tq], (tq, 128))

        def stage_c(t, slot):
            k = k_ref[rows(t), :]
            for hh in range(2):
                dzb = dzb_s[slot, hh]
                dq_s[hh] += _nn(dzb, k)
                dims = pl.ds(HEAD_DIM * hh, HEAD_DIM)
                dkt_s[oldest + t, dims, :] += _nn(qt_s[dims, :], dzb)
                dvt_s[oldest + t, dims, :] += _nn(dot_s[dims, :], ab_s[slot, hh])

        _pipeline3(n_live, stage_a, stage_b, stage_c, True)
        dq_ref[...] = (jnp.where(lane < HEAD_DIM, dq_s[0], dq_s[1]) * -(HEAD_DIM ** -0.5)).astype(BF16)

        @pl.when(i == nq - 1)
        def _():
            def flush(n, carry):
                keys = pl.ds(pl.multiple_of(n * tq, tq), tq)
                dk_ref[keys, :] = dkt_s[n].T
                dv_ref[keys, :] = dvt_s[n].T
                return carry

            lax.fori_loop(0, nq, flush, 0)

    return pl.pallas_call(
        body,
        name="sb_bwd",
        grid=(4, nq),
        in_specs=[
            pl.BlockSpec((tq, 128), lambda p, i: (i, 12 + p)),
            pl.BlockSpec((T, 128), lambda p, i: (0, 16 + p)),
            pl.BlockSpec((T, 128), lambda p, i: (0, 20 + p)),
            pl.BlockSpec((2, 1, tq), lambda p, i: (p, 0, i)),
            pl.BlockSpec((1, 1, nq), lambda p, i: (p, 0, 0)),
            pl.BlockSpec((tq, 128), lambda p, i: (i, p)),
        ],
        out_specs=[
            pl.BlockSpec((tq, 128), lambda p, i: (i, p)),
            pl.BlockSpec((T, 128), lambda p, i: (0, p)),
            pl.BlockSpec((T, 128), lambda p, i: (0, p)),
        ],
        out_shape=[
            jax.ShapeDtypeStruct((T, FOX_W), BF16),
            jax.ShapeDtypeStruct((T, FOX_W), F32),
            jax.ShapeDtypeStruct((T, FOX_W), F32),
        ],
        scratch_shapes=[
            pltpu.VMEM((2, tq, 128), BF16),
            pltpu.VMEM((2, tq, 128), BF16),
            pltpu.VMEM((2, tq, 128), F32),
            pltpu.VMEM((2, 2, tq, tq), F32),
            pltpu.VMEM((2, 2, tq, tq), F32),
            pltpu.VMEM((2, 2, tq, tq), BF16),
            pltpu.VMEM((2, 2, tq, tq), BF16),
            pltpu.VMEM((2, tq, 128), F32),
            pltpu.VMEM((2, tq, 128), F32),
            pltpu.VMEM((2, tq, 128), F32),
            pltpu.VMEM((128, tq), BF16),
            pltpu.VMEM((128, tq), BF16),
            pltpu.VMEM((nq, 128, tq), F32),
            pltpu.VMEM((nq, 128, tq), F32),
        ],
        compiler_params=_cparams(("arbitrary", "arbitrary")),
    )(proj, proj, proj, ltot, live, d_o)


def _fox_bwd(proj, c_col, c_row, c_ends, lse, d_o, o, tq, scatter=()):
    T = proj.shape[0]
    nq = T // tq
    ns = len(scatter)

    def body(*refs):
        q_ref, k_ref, v_ref, cq_ref, ck_ref, cke_ref, lse_ref, do_ref, o_ref = refs[:9]
        dq_ref, dk_ref, dv_ref, dck_ref, dcq_ref = refs[9 + ns:14 + ns]
        (qh_s, doh_s, delta_s, shift_s, z_s, dp_s, pb_s, dsb_s, rs_s, dq_s,
         kn_s, qt_s, dot_s, dkt_s, dvt_s, kt_s) = refs[14 + 2 * ns:30 + 2 * ns]
        i = pl.program_id(1)
        if ns:
            pair = pl.program_id(0)
            finish = _ride_along(_Scatter(refs[9:9 + ns], refs[14 + ns:14 + 2 * ns], *refs[30 + 2 * ns:]),
                                 (pair == 0) & (i == 0), None, (pair == 3) & (i == nq - 1))
        lane = lax.broadcasted_iota(jnp.int32, (1, 128), 1)

        @pl.when(i == 0)
        def _():
            dkt_s[...] = jnp.zeros_like(dkt_s)
            dvt_s[...] = jnp.zeros_like(dvt_s)
            dck_ref[...] = jnp.zeros_like(dck_ref)
            _fox_key_norms(k_ref, kn_s, lane)

            def turn(n, carry):
                kt_s[n] = k_ref[pl.ds(pl.multiple_of(n * tq, tq), tq), :].astype(F32).T.astype(BF16)
                return carry

            lax.fori_loop(0, nq, turn, 0)

        row = lax.broadcasted_iota(jnp.int32, (tq, tq), 0)
        col = lax.broadcasted_iota(jnp.int32, (tq, tq), 1)
        q = q_ref[...]
        dov = do_ref[...]
        ov = o_ref[...]
        qt_s[...] = (q.astype(F32) * (HEAD_DIM ** -0.5)).T.astype(BF16)
        dot_s[...] = dov.astype(F32).T.astype(BF16)
        for hh in range(2):
            qh, hmask = _head_q(q, hh, lane)
            dohb = jnp.where(hmask, dov, 0.0).astype(BF16)
            qh_s[hh] = qh
            doh_s[hh] = dohb
            delta_s[hh] = jnp.broadcast_to(jnp.sum(dohb.astype(F32) * ov, axis=1, keepdims=True), (tq, tq))
            shift_s[hh] = jnp.broadcast_to(_lanes_to_rows(cq_ref[hh] - lse_ref[hh], row == col), (tq, tq))
        rs_s[...] = jnp.zeros_like(rs_s)
        dq_s[...] = jnp.zeros_like(dq_s)

        def rows(t):
            return pl.ds(pl.multiple_of((i - t) * tq, tq), tq)

        def stage_a(t, slot):
            k = k_ref[rows(t), :]
            v = v_ref[rows(t), :]
            for hh in range(2):
                z_s[slot, hh] = _nt(qh_s[hh], k)
                dp_s[slot, hh] = _nt(doh_s[hh], v)

        def stage_b(t, slot, diag):
            for hh in range(2):
                s = z_s[slot, hh] + shift_s[hh] - ck_ref[hh, :, rows(t)]
                if diag:
                    s = jnp.where(col <= row, s, NEG)
                p = jnp.exp(s)
                ds = p * (dp_s[slot, hh] - delta_s[hh])
                pb_s[slot, hh] = p.astype(BF16)
                dsb_s[slot, hh] = ds.astype(BF16)
                dck_ref[hh, :, rows(t)] += jnp.sum(ds, axis=0, keepdims=True)
                rs_s[hh] += jnp.sum(ds, axis=1, keepdims=True)

        def stage_c(t, slot):
            for hh in range(2):
                dsb = dsb_s[slot, hh]
                dims = pl.ds(HEAD_DIM * hh, HEAD_DIM)
                dq_s[dims, :] += _nt(kt_s[i - t, dims, :], dsb)
                dkt_s[i - t, dims, :] += _nn(qt_s[dims, :], dsb)
                dvt_s[i - t, dims, :] += _nn(dot_s[dims, :], pb_s[slot, hh])

        _pipeline3(_fox_live_blocks(i, qh_s, kn_s, cq_ref, cke_ref), stage_a, stage_b, stage_c, False,
                   a_first=True)

        @pl.when(i == nq - 1)
        def _():
            def flush(n, carry):
                keys = pl.ds(pl.multiple_of(n * tq, tq), tq)
                dk_ref[keys, :] = dkt_s[n].T
                dv_ref[keys, :] = dvt_s[n].T
                return carry

            lax.fori_loop(0, nq, flush, 0)
        dcq_ref[0] = _rows_to_lanes(rs_s[0], row == col)
        dcq_ref[1] = _rows_to_lanes(rs_s[1], row == col)
        dq_ref[...] = (dq_s[...].T * (HEAD_DIM ** -0.5)).astype(BF16)
        if ns:
            finish()

    res = pl.pallas_call(
        body,
        name="fox_bwd",
        grid=(4, nq),
        in_specs=[
            pl.BlockSpec((tq, 128), lambda p, i: (i, p)),
            pl.BlockSpec((T, 128), lambda p, i: (0, 4 + p)),
            pl.BlockSpec((T, 128), lambda p, i: (0, 8 + p)),
            pl.BlockSpec((2, 1, tq), lambda p, i: (p, 0, i)),
            pl.BlockSpec((2, 1, T), lambda p, i: (p, 0, 0)),
            pl.BlockSpec((2, 1, nq), lambda p, i: (p, 0, 0)),
            pl.BlockSpec((2, 1, tq), lambda p, i: (p, 0, i)),
            pl.BlockSpec((tq, 128), lambda p, i: (i, p)),
            pl.BlockSpec((tq, 128), lambda p, i: (i, p)),
        ] + [_ANY] * ns,
        out_specs=[
            pl.BlockSpec((tq, 128), lambda p, i: (i, p)),
            pl.BlockSpec((T, 128), lambda p, i: (0, p)),
            pl.BlockSpec((T, 128), lambda p, i: (0, p)),
            pl.BlockSpec((2, 1, T), lambda p, i: (p, 0, 0)),
            pl.BlockSpec((2, 1, tq), lambda p, i: (p, 0, i)),
        ] + [_ANY] * ns,
        out_shape=[
            jax.ShapeDtypeStruct((T, FOX_W), BF16),
            jax.ShapeDtypeStruct((T, FOX_W), F32),
            jax.ShapeDtypeStruct((T, FOX_W), F32),
            jax.ShapeDtypeStruct((N_FOX, 1, T), F32),
            jax.ShapeDtypeStruct((N_FOX, 1, T), F32),
        ] + [jax.ShapeDtypeStruct(b.shape, b.dtype) for b in scatter],
        scratch_shapes=[
            pltpu.VMEM((2, tq, 128), BF16),
            pltpu.VMEM((2, tq, 128), BF16),
            pltpu.VMEM((2, tq, tq), F32),
            pltpu.VMEM((2, tq, tq), F32),
            pltpu.VMEM((2, 2, tq, tq), F32),
            pltpu.VMEM((2, 2, tq, tq), F32),
            pltpu.VMEM((2, 2, tq, tq), BF16),
            pltpu.VMEM((2, 2, tq, tq), BF16),
            pltpu.VMEM((2, tq, 128), F32),
            pltpu.VMEM((128, tq), F32),
            pltpu.VMEM((2, 8, 128), F32),
            pltpu.VMEM((128, tq), BF16),
            pltpu.VMEM((128, tq), BF16),
            pltpu.VMEM((nq, 128, tq), F32),
            pltpu.VMEM((nq, 128, tq), F32),
            pltpu.VMEM((nq, 128, tq), BF16),
        ] + (_comm_sems(ns) if ns else []),
        compiler_params=_cparams(("arbitrary", "arbitrary")),
    )(proj, proj, proj, c_col, c_row, c_ends, lse, d_o, o, *scatter)
    res = list(res)
    return (*res[:5], res[5:])


def _forget_bwd(dcq, dck, xf, h1, tc):
    H, T = xf.shape
    D = h1.shape[1]
    nc = T // tc

    def body(dcq_ref, dck_ref, xf_ref, h_ref, dxf_ref, db_ref, dwf_ref):
        row = lax.broadcasted_iota(jnp.int32, (tc, tc), 0)
        col = lax.broadcasted_iota(jnp.int32, (tc, tc), 1)
        from_here = jnp.where(row >= col, 1.0, 0.0).astype(BF16)

        def chunk(n, carry):
            run, db, dwf = carry
            cs = pl.multiple_of((nc - 1 - n) * tc, tc)
            dc = dcq_ref[:, pl.ds(cs, tc)] - dck_ref[:, pl.ds(cs, tc)]
            dlogf = _split_dot(dc, from_here, 3) + run
            xfv = xf_ref[:, pl.ds(cs, tc)]
            dxf = dlogf * jax.nn.sigmoid(-xfv)
            dxf_ref[:, pl.ds(cs, tc)] = dxf
            dwf = dwf + _nn(dxf.astype(BF16), h_ref[pl.ds(cs, tc), :])
            return dlogf[:, 0:1], db + jnp.sum(dxf, axis=1, keepdims=True), dwf

        zero = jnp.zeros((H, 1), F32)
        _, db, dwf = lax.fori_loop(0, nc, chunk, (zero, zero, jnp.zeros((H, D), F32)))
        db_ref[...] = db
        dwf_ref[...] = dwf

    return pl.pallas_call(
        body,
        name="forget_bwd",
        out_shape=[jax.ShapeDtypeStruct((H, T), F32), jax.ShapeDtypeStruct((H, 1), F32),
                   jax.ShapeDtypeStruct((H, D), F32)],
        compiler_params=_cparams(),
    )(dcq, dck, xf, h1)


def _inproj_bwd(pieces, dxf_t, w_in, w_f_t, x, g1, dx1, tm, scatter=()):
    T, D = x.shape
    N = w_in.shape[1]
    ns = len(scatter)
    nt = T // tm
    npc = len(pieces)

    def body(*refs):
        pc_refs = refs[:npc]
        dxf_ref, w_ref, wf_ref, x_ref, g_ref, dx1_ref = refs[npc:npc + 6]
        base = npc + 6
        dx_ref, dg_ref = refs[base + ns:base + 2 + ns]
        i = pl.program_id(0)
        if ns:
            exchange = _Scatter(refs[base:base + ns], refs[base + 2 + ns:base + 2 + 2 * ns],
                                *refs[base + 2 + 2 * ns:])

            @pl.when(i == 0)
            def _():
                exchange.start()

        @pl.when(i == 0)
        def _():
            dg_ref[...] = jnp.zeros_like(dg_ref)

        dh = _nn(dxf_ref[...], wf_ref[...].astype(F32))
        for k, pc_ref in enumerate(pc_refs):
            dh = dh + _nt(pc_ref[...].astype(BF16), w_ref[:, k * FOX_W:(k + 1) * FOX_W])
        dx, dg = _norm_bwd(x_ref[...], g_ref[...], dh)
        dx_ref[...] = dx1_ref[...] + dx
        dg_ref[...] += dg
        if ns:
            @pl.when(i == nt - 1)
            def _():
                exchange.finish()

    res = pl.pallas_call(
        body,
        name="inproj_bwd",
        grid=(nt,),
        in_specs=[pl.BlockSpec((tm, FOX_W), lambda i: (i, 0))] * npc + [
            pl.BlockSpec((tm, N_FOX), lambda i: (i, 0)),
            pl.BlockSpec((D, N), lambda i: (0, 0)),
            pl.BlockSpec((N_FOX, D), lambda i: (0, 0)),
            pl.BlockSpec((tm, D), lambda i: (i, 0)),
            pl.BlockSpec((1, D), lambda i: (0, 0)),
            pl.BlockSpec((tm, D), lambda i: (i, 0)),
        ] + [_ANY] * ns,
        out_specs=[
            pl.BlockSpec((tm, D), lambda i: (i, 0)),
            pl.BlockSpec((1, D), lambda i: (0, 0)),
        ] + [_ANY] * ns,
        out_shape=[jax.ShapeDtypeStruct((T, D), F32), jax.ShapeDtypeStruct((1, D), F32)]
        + [jax.ShapeDtypeStruct(b.shape, b.dtype) for b in scatter],
        scratch_shapes=_comm_sems(ns) if ns else [],
        compiler_params=_cparams(("arbitrary",)),
    )(*pieces, dxf_t, w_in, w_f_t, x, g1, dx1, *scatter)
    res = list(res)
    return res[0], res[1], res[2:]


def _dw_in(h1, pieces, name):
    T, K = h1.shape
    bt = min(T, 512)
    nt = T // bt
    npc = len(pieces)

    def body(*refs):
        a_ref = refs[0]
        pc_refs = refs[1:1 + npc]
        o_ref, acc_ref = refs[1 + npc:]
        t = pl.program_id(0)

        @pl.when(t == 0)
        def _():
            acc_ref[...] = jnp.zeros_like(acc_ref)

        a = a_ref[...]
        for k, pc_ref in enumerate(pc_refs):
            acc_ref[:, k * FOX_W:(k + 1) * FOX_W] += _tn(a, pc_ref[...].astype(BF16))

        @pl.when(t == nt - 1)
        def _():
            o_ref[...] = acc_ref[...].astype(BF16)

    return pl.pallas_call(
        body,
        name=name,
        grid=(nt,),
        in_specs=[pl.BlockSpec((bt, K), lambda t: (t, 0))] + [pl.BlockSpec((bt, FOX_W), lambda t: (t, 0))] * npc,
        out_specs=pl.BlockSpec((K, npc * FOX_W), lambda t: (0, 0)),
        out_shape=jax.ShapeDtypeStruct((K, npc * FOX_W), BF16),
        scratch_shapes=[pltpu.VMEM((K, npc * FOX_W), F32)],
        compiler_params=_cparams(("arbitrary",)),
    )(h1, *pieces)


def _matmul_tn(a, b, name, cast_b=False):
    T, K = a.shape
    N = b.shape[1]
    bt = min(T, 512)
    bk = _tile_div(K, 1536)
    bn = _tile_div(N, 1536)
    nt = T // bt

    def body(a_ref, b_ref, o_ref, acc_ref):
        t = pl.program_id(2)

        @pl.when(t == 0)
        def _():
            acc_ref[...] = jnp.zeros_like(acc_ref)

        bv = b_ref[...]
        if cast_b:
            bv = bv.astype(BF16)
        acc_ref[...] += _tn(a_ref[...], bv)

        @pl.when(t == nt - 1)
        def _():
            o_ref[...] = acc_ref[...].astype(BF16)

    return pl.pallas_call(
        body,
        name=name,
        grid=(K // bk, N // bn, nt),
        in_specs=[
            pl.BlockSpec((bt, bk), lambda k, n, t: (t, k)),
            pl.BlockSpec((bt, bn), lambda k, n, t: (t, n)),
        ],
        out_specs=pl.BlockSpec((bk, bn), lambda k, n, t: (k, n)),
        out_shape=jax.ShapeDtypeStruct((K, N), BF16),
        scratch_shapes=[pltpu.VMEM((bk, bn), F32)],
        compiler_params=_cparams(("arbitrary", "arbitrary", "arbitrary")),
    )(a, b)


def _local_step(x, mem, target, p, tm, tq, late=None):
    T, D = x.shape
    w_in = p["w_in"]
    w_qkv = w_in[:, :QKV_W]
    w_f_t = w_in[:, QKV_W:].T
    b_f = p["b_forget"].reshape(N_FOX, 1)

    proj, h1, xf, c = _inproj_fwd(x, p["attn_norm_g"], w_qkv, w_f_t, b_f, tm)
    c_col = c.reshape(N_FOX, 1, T)
    c_row = c.reshape(N_FOX, 1, T)
    c_ends = c[:, tq - 1::tq].reshape(N_FOX, 1, T // tq)
    fox_o, lse, gathered = _fox_fwd(proj, c_col, c_row, c_ends, tq, gather=[late[n] for n in _LATE] if late else ())
    if late:
        p = dict(p, **{n: _gathered_full(n, gv) for n, gv in zip(_LATE, gathered)})
    sb_o, sb_ltot, sb_live = _sb_fwd(proj, tq)
    x1, mixed = _post_attn_fwd(fox_o, sb_o, p["fox_out_g"], p["sb_out_g"], p["w_out"], x, tm)
    mb, kv = _mem_kv_fwd(mem, p["mem_norm_g"], p["w_mkv"])
    x2, h2, qb, om = _xattn_fwd(x1, p["xattn_norm_g"], p["w_mq"], kv, p["w_mo"], tm)
    tf = 2 * tm if T % (2 * tm) == 0 else tm
    x3, h3, ug, uv, yg, yv, a = _ffn_fwd(
        x2, p["ffn_norm_g"], p["w_up"], p["conv_w"], p["conv_b"], p["w_down"], tf)
    dx3, loss_blk, d_final_g = _loss_head(x3, p["final_norm_g"], target, tm)

    g = {"final_norm_g": d_final_g}
    dx2, du_g, du_v, g["ffn_norm_g"], dc_g, dc_v = _ffn_bwd(
        dx3, x2, p["ffn_norm_g"], ug, uv, yg, yv, p["conv_w"], p["w_down"], p["w_up"], tf)
    g["w_down"] = _matmul_tn(a, dx3, "dw_down", cast_b=True)
    g["w_up"] = jnp.concatenate([_matmul_tn(h3, du_g, "dw_up_gate"), _matmul_tn(h3, du_v, "dw_up_val")], axis=1)
    dconv = jnp.concatenate([dc_g, dc_v], axis=1)
    g["conv_w"] = dconv[0:3]
    g["conv_b"] = dconv[3:4]
    dx1, dq_m, dkv, g["xattn_norm_g"] = _xattn_bwd(dx2, x1, p["xattn_norm_g"], qb, kv, p["w_mo"], p["w_mq"], tm)
    g["w_mo"] = _matmul_tn(om, dx2, "dw_mo", cast_b=True)
    g["w_mq"] = _matmul_tn(h2, dq_m, "dw_mq")
    g["w_mkv"], g["mem_norm_g"] = _mem_kv_bwd(mem, p["mem_norm_g"], mb, dkv, p["w_mkv"])
    d_fox, d_sb, g["fox_out_g"], g["sb_out_g"] = _post_attn_bwd(
        dx1, fox_o, sb_o, p["fox_out_g"], p["sb_out_g"], p["w_out"], tm)
    g["w_out"] = _matmul_tn(mixed, dx1, "dw_out", cast_b=True)
    dq_s, dk_s, dv_s = _sb_bwd(proj, sb_ltot, sb_live, d_sb, tq)
    dq_f, dk_f, dv_f, dck, dcq, parts = _fox_bwd(
        proj, c_col, c_row, c_ends, lse, d_fox, fox_o, tq,
        scatter=[_grad_blocks(n, g[n]) for n in _LATE] if late else ())
    if late:
        g["parts"] = dict(zip(_LATE, parts))
    dxf, db, dwf_t = _forget_bwd(dcq.reshape(N_FOX, T), dck.reshape(N_FOX, T), xf, h1, min(T, 512))
    g["b_forget"] = db.reshape(1, N_FOX)
    pieces = [dq_f, dk_f, dv_f, dq_s, dk_s, dv_s]
    g["w_in"] = jnp.concatenate([_dw_in(h1, pieces, "dw_in"), dwf_t.T.astype(BF16)], axis=1)
    grad_x, g["attn_norm_g"], parts = _inproj_bwd(
        pieces, dxf.T, w_in, w_f_t, x, p["attn_norm_g"], dx1, tm,
        scatter=[_grad_blocks("w_in", g["w_in"])] if late else ())
    if late:
        (g["parts"]["w_in"],) = parts
    return loss_blk, grad_x, g


def _mesh_pos():
    return lax.axis_index("x"), lax.axis_index("y"), lax.axis_index("c")


def _flip(pos, k):
    return tuple(1 - v if (k >> b) & 1 else v for v, b in zip(pos, (2, 1, 0)))


def _slot(pos):
    return 4 * pos[0] + 2 * pos[1] + pos[2]


_CHIPS = (4, 2, 6)


def _comm_sems(n):
    return [pltpu.SemaphoreType.DMA((7 * n,)), pltpu.SemaphoreType.DMA((7 * n,)), pltpu.SemaphoreType.DMA((n,))]


class _Gather:
    def __init__(self, ins, outs, send_sems, recv_sems, local_sems):
        self.ins, self.outs, self.n = ins, outs, len(ins)
        self.send_sems, self.recv_sems, self.local_sems = send_sems, recv_sems, local_sems
        self.me = _mesh_pos()
        self.sibling = _flip(self.me, 1)

    def _copy(self, a, kk, block, to, src=None):
        rows = self.outs[a].at[_slot(block)]
        return pltpu.make_async_remote_copy(
            src_ref=rows if src is None else src, dst_ref=rows,
            send_sem=self.send_sems.at[7 * a + kk], recv_sem=self.recv_sems.at[7 * a + kk],
            device_id=to, device_id_type=MESH)

    def _mine(self):
        return [pltpu.make_async_copy(self.ins[a], self.outs[a].at[_slot(self.me)], self.local_sems.at[a])
                for a in range(self.n)]

    def _first(self):
        out = []
        for a in range(self.n):
            out.append(self._copy(a, 0, self.me, self.sibling, src=self.ins[a]))
            out += [self._copy(a, 1 + j, self.me, _flip(self.me, k), src=self.ins[a]) for j, k in enumerate(_CHIPS)]
        return out

    def _passed(self):
        return [self._copy(a, 4 + j, _flip(self.me, k), self.sibling)
                for j, k in enumerate(_CHIPS) for a in range(self.n)]

    def start(self):
        for cp in self._mine() + self._first():
            cp.start()

    def forward(self):
        for j, k in enumerate(_CHIPS):
            for a in range(self.n):
                self._copy(a, 1 + j, _flip(self.me, k), self.me).wait_recv()
                self._copy(a, 4 + j, _flip(self.me, k), self.sibling).start()

    def finish(self):
        for a in range(self.n):
            self._copy(a, 0, self.sibling, self.me).wait_recv()
            for j, k in enumerate(_CHIPS):
                self._copy(a, 4 + j, _flip(self.sibling, k), self.me).wait_recv()
        for cp in self._first() + self._passed():
            cp.wait_send()
        for cp in self._mine():
            cp.wait()


class _Scatter:
    def __init__(self, ins, outs, send_sems, recv_sems, local_sems):
        self.ins, self.outs, self.n = ins, outs, len(ins)
        self.send_sems, self.recv_sems, self.local_sems = send_sems, recv_sems, local_sems
        self.me = _mesh_pos()

    def _copy(self, a, k, landed=False):
        peer = _flip(self.me, k)
        return pltpu.make_async_remote_copy(
            src_ref=self.ins[a].at[_slot(peer)], dst_ref=self.outs[a].at[_slot(peer if landed else self.me)],
            send_sem=self.send_sems.at[7 * a + k - 1], recv_sem=self.recv_sems.at[7 * a + k - 1],
            device_id=peer, device_id_type=MESH)

    def _mine(self):
        s = _slot(self.me)
        return [pltpu.make_async_copy(self.ins[a].at[s], self.outs[a].at[s], self.local_sems.at[a])
                for a in range(self.n)]

    def start(self):
        for cp in self._mine() + [self._copy(a, k) for k in range(1, 8) for a in range(self.n)]:
            cp.start()

    def finish(self):
        for k in range(1, 8):
            for a in range(self.n):
                self._copy(a, k, landed=True).wait_recv()
        for k in range(1, 8):
            for a in range(self.n):
                self._copy(a, k).wait_send()
        for cp in self._mine():
            cp.wait()


_ANY = pl.BlockSpec(memory_space=pl.ANY)


def _gathered_shapes(shards):
    return [jax.ShapeDtypeStruct((N_DEV,) + s.shape, s.dtype) for s in shards]


def _all_gather(shards, name):
    n = len(shards)

    def body(*refs):
        g = _Gather(refs[:n], refs[n:2 * n], *refs[2 * n:])
        g.start()
        g.forward()
        g.finish()

    return pl.pallas_call(
        body, name=name, in_specs=[_ANY] * n, out_specs=[_ANY] * n,
        out_shape=_gathered_shapes(shards), scratch_shapes=_comm_sems(n),
    )(*shards)


def _adamw_math(w, g, m, v):
    m2 = ADAM_B1 * m + (1.0 - ADAM_B1) * g
    v2 = ADAM_B2 * v + (1.0 - ADAM_B2) * (g * g)
    m_hat = m2 / (1.0 - ADAM_B1 ** ADAM_STEP)
    v_hat = v2 / (1.0 - ADAM_B2 ** ADAM_STEP)
    delta = -ADAM_LR * (m_hat / (jnp.sqrt(v_hat) + ADAM_EPS) + ADAM_WD * w)
    return delta, m2, v2


def _adamw(w, parts, m, v, name):
    R, C = w.shape
    br = 128 if R % 128 == 0 else R

    def body(w_ref, p_ref, m_ref, v_ref, g_ref, d_ref, nm_ref, nv_ref):
        g = p_ref[0].astype(F32)
        for s in range(1, N_DEV):
            g = g + p_ref[s].astype(F32)
        g_ref[...] = g
        d_ref[...], nm_ref[...], nv_ref[...] = _adamw_math(w_ref[...], g, m_ref[...], v_ref[...])

    spec = pl.BlockSpec((br, C), lambda i: (i, 0))
    return pl.pallas_call(
        body,
        name=name,
        grid=(R // br,),
        in_specs=[spec, pl.BlockSpec((N_DEV, br, C), lambda i: (0, i, 0)), spec, spec],
        out_specs=[spec] * 4,
        out_shape=[jax.ShapeDtypeStruct((R, C), F32)] * 4,
        compiler_params=_cparams(("arbitrary",)),
    )(w, parts, m, v)


_SHARDED = ("w_in", "w_out", "w_mq", "w_mkv", "w_mo", "w_up", "conv_w", "w_down")
_LATE = _SHARDED[1:]
_COL_SHARDED = ("w_in", "w_mkv", "w_up", "conv_w")
_REPLICATED = ("attn_norm_g", "b_forget", "fox_out_g", "sb_out_g", "xattn_norm_g", "mem_norm_g",
               "ffn_norm_g", "conv_b", "final_norm_g")
_WEIGHTS = ("attn_norm_g", "w_in", "b_forget", "fox_out_g", "sb_out_g", "w_out", "xattn_norm_g", "mem_norm_g",
            "w_mq", "w_mkv", "w_mo", "ffn_norm_g", "w_up", "conv_w", "conv_b", "w_down", "final_norm_g")


def _pack_rows(n):
    return -(-n // 128)


def _pack(vals, rows_total):
    parts = []
    for v in vals:
        flat = v.reshape(-1)
        parts.append(jnp.pad(flat, (0, _pack_rows(flat.shape[0]) * 128 - flat.shape[0])))
    flat = jnp.concatenate(parts)
    return jnp.pad(flat, (0, rows_total * 128 - flat.shape[0])).reshape(rows_total, 128)


def _unpack(packed, shapes):
    out = []
    r = 0
    for shp in shapes:
        n = 1
        for d in shp:
            n *= d
        out.append(packed[r:r + _pack_rows(n)].reshape(-1)[:n].reshape(shp))
        r += _pack_rows(n)
    return out


def _gathered_full(name, gathered):
    if name in _COL_SHARDED:
        return jnp.transpose(gathered, (1, 0, 2)).reshape(gathered.shape[1], -1)
    return gathered.reshape(-1, gathered.shape[2])


def _to_blocks(name, full):
    if name in _COL_SHARDED:
        r = full.shape[0]
        return jnp.transpose(full.reshape(r, N_DEV, -1), (1, 0, 2))
    return full.reshape(N_DEV, -1, full.shape[1])


def _grad_blocks(name, full):
    blocks = _to_blocks(name, full)
    return blocks if name == "conv_w" else blocks.astype(BF16)


def _step(args, tm, tq):
    w = {n: args[n] for n in _WEIGHTS}
    mom = {n: args["m_" + n] for n in _WEIGHTS}
    var = {n: args["v_" + n] for n in _WEIGHTS}
    x = args["x"][0]
    mem = args["mem"][0]
    target = args["loss_target"][0]

    def flat2(a):
        return a.reshape(a.shape[-2], a.shape[-1]) if a.ndim == 3 else a.reshape(1, -1)

    shards = {n: flat2(w[n]) if n == "conv_w" else flat2(w[n]).astype(BF16) for n in _SHARDED}
    (w_in_all,) = _all_gather([shards["w_in"]], "gather_w_in")
    p = {"w_in": _gathered_full("w_in", w_in_all)}
    for n in _REPLICATED:
        p[n] = flat2(w[n])

    loss_blk, grad_x, g = _local_step(x, mem, target, p, tm, tq, late={n: shards[n] for n in _LATE})

    parts = g["parts"]
    out = {}
    for n in _SHARDED:
        res = _adamw(flat2(w[n]), parts[n], flat2(mom[n]), flat2(var[n]), "adamw_" + n)
        out[n] = [r.reshape(w[n].shape) for r in res]

    shapes = [w[n].shape for n in _REPLICATED]
    rows = sum(_pack_rows(flat2(w[n]).shape[1]) for n in _REPLICATED) + 1
    rows = -(-rows // 8) * 8
    g_pack = _pack([g[n] for n in _REPLICATED] + [loss_blk[0:1, :]], rows)
    (g_all,) = _all_gather([g_pack], "gather_small")
    res = _adamw(_pack([w[n] for n in _REPLICATED], rows), g_all,
                 _pack([mom[n] for n in _REPLICATED], rows), _pack([var[n] for n in _REPLICATED], rows),
                 "adamw_small")
    n_rows_params = sum(_pack_rows(flat2(w[n]).shape[1]) for n in _REPLICATED)
    loss = res[0][n_rows_params, 0]
    unpacked = [_unpack(r, shapes) for r in res]
    for k, n in enumerate(_REPLICATED):
        out[n] = [unpacked[q][k] for q in range(4)]

    grads = [out[n][0] for n in _WEIGHTS]
    deltas = [out[n][1] for n in _WEIGHTS]
    new_m = [out[n][2] for n in _WEIGHTS]
    new_v = [out[n][3] for n in _WEIGHTS]
    return (loss, grad_x[None], *grads, *deltas, *new_m, *new_v)


def kernel(x, mem, attn_norm_g, w_in, b_forget, fox_out_g, sb_out_g, w_out, xattn_norm_g, mem_norm_g, w_mq, w_mkv, w_mo, ffn_norm_g, w_up, conv_w, conv_b, w_down, final_norm_g, loss_target, m_attn_norm_g, m_w_in, m_b_forget, m_fox_out_g, m_sb_out_g, m_w_out, m_xattn_norm_g, m_mem_norm_g, m_w_mq, m_w_mkv, m_w_mo, m_ffn_norm_g, m_w_up, m_conv_w, m_conv_b, m_w_down, m_final_norm_g, v_attn_norm_g, v_w_in, v_b_forget, v_fox_out_g, v_sb_out_g, v_w_out, v_xattn_norm_g, v_mem_norm_g, v_w_mq, v_w_mkv, v_w_mo, v_ffn_norm_g, v_w_up, v_conv_w, v_conv_b, v_w_down, v_final_norm_g):
    args = dict(locals())
    T = x.shape[1]
    return _step(args, tm=min(T, 512), tq=min(T, 256))
```

```python
import functools

import jax
import jax.numpy as jnp
from jax import lax
from jax.experimental import pallas as pl
from jax.experimental.pallas import tpu as pltpu

F32 = jnp.float32
BF16 = jnp.bfloat16
EPS = 1e-6
NEG = -1e30
LOG2E = 1.4426950408889634

HEAD_DIM = 64
N_FOX = 8
FOX_W = 512
QKV_W = 3072
N_MEM_HEADS = 4
MEM_HD = 256
D_FF = 2816
FF_CHUNK = 256
N_DEV = 8

ADAM_LR = 0.001
ADAM_B1 = 0.9
ADAM_B2 = 0.999
ADAM_EPS = 1e-08
ADAM_WD = 0.01
ADAM_STEP = 10

SB_SUM_TERMS = 1

VMEM_LIMIT = 56 * 1024 * 1024
MESH = pl.DeviceIdType.MESH


def _cparams(sem=None):
    return pltpu.CompilerParams(dimension_semantics=sem, vmem_limit_bytes=VMEM_LIMIT)


def _nt(a, b):
    return lax.dot_general(a, b, (((1,), (1,)), ((), ())), preferred_element_type=F32)


def _tn(a, b):
    return lax.dot_general(a, b, (((0,), (0,)), ((), ())), preferred_element_type=F32)


def _nn(a, b):
    return jnp.dot(a, b, preferred_element_type=F32)


def _split_dot(a, m01, terms):
    out = None
    r = a
    for t in range(terms):
        p = r.astype(BF16)
        d = _nn(p, m01)
        out = d if out is None else out + d
        if t + 1 < terms:
            r = r - p.astype(F32)
    return out


def _rstd(xv):
    return lax.rsqrt(jnp.mean(xv * xv, axis=-1, keepdims=True) + EPS)


def _norm_bwd(xv, g, dh):
    r = _rstd(xv)
    xhat = xv * r
    dxhat = dh * g
    dx = r * (dxhat - xhat * jnp.mean(dxhat * xhat, axis=-1, keepdims=True))
    dg = jnp.sum(dh * xhat, axis=0, keepdims=True)
    return dx, dg


def _tile_div(n, cap):
    best = None
    for d in range(128, min(n, cap) + 1, 128):
        if n % d == 0:
            best = d
    assert best is not None, n
    return best


def _inproj_fwd(x, g1, w_qkv, w_f_t, b_f, tm):
    T, D = x.shape
    N = w_qkv.shape[1]
    H = w_f_t.shape[0]

    def body(x_ref, g_ref, w_ref, wf_ref, b_ref, proj_ref, h_ref, xf_ref, c_ref, carry_ref):
        i = pl.program_id(0)

        @pl.when(i == 0)
        def _():
            carry_ref[...] = jnp.zeros_like(carry_ref)

        xv = x_ref[...]
        h = (xv * _rstd(xv) * g_ref[...]).astype(BF16)
        h_ref[...] = h
        for n0 in range(0, N, 512):
            proj_ref[:, n0:n0 + 512] = _nn(h, w_ref[:, n0:n0 + 512]).astype(BF16)
        xf = _nt(wf_ref[...], h) + b_ref[...]
        xf_ref[...] = xf
        logf = jnp.minimum(xf, 0.0) - jnp.log1p(jnp.exp(-jnp.abs(xf)))
        row = lax.broadcasted_iota(jnp.int32, (tm, tm), 0)
        col = lax.broadcasted_iota(jnp.int32, (tm, tm), 1)
        upper = jnp.where(row <= col, 1.0, 0.0).astype(BF16)
        c = _split_dot(logf, upper, 3) + carry_ref[...]
        c_ref[...] = c
        carry_ref[...] = c[:, tm - 1:tm]

    return pl.pallas_call(
        body,
        name="inproj_fwd",
        grid=(T // tm,),
        in_specs=[
            pl.BlockSpec((tm, D), lambda i: (i, 0)),
            pl.BlockSpec((1, D), lambda i: (0, 0)),
            pl.BlockSpec((D, N), lambda i: (0, 0)),
            pl.BlockSpec((H, D), lambda i: (0, 0)),
            pl.BlockSpec((H, 1), lambda i: (0, 0)),
        ],
        out_specs=[
            pl.BlockSpec((tm, N), lambda i: (i, 0)),
            pl.BlockSpec((tm, D), lambda i: (i, 0)),
            pl.BlockSpec((H, tm), lambda i: (0, i)),
            pl.BlockSpec((H, tm), lambda i: (0, i)),
        ],
        out_shape=[
            jax.ShapeDtypeStruct((T, N), BF16),
            jax.ShapeDtypeStruct((T, D), BF16),
            jax.ShapeDtypeStruct((H, T), F32),
            jax.ShapeDtypeStruct((H, T), F32),
        ],
        scratch_shapes=[pltpu.VMEM((H, 1), F32)],
        compiler_params=_cparams(("arbitrary",)),
    )(x, g1, w_qkv, w_f_t, b_f)


def _head_q(q, hh, lane):
    hmask = (lane >= HEAD_DIM * hh) & (lane < HEAD_DIM * (hh + 1))
    qh = jnp.where(hmask, q.astype(F32), 0.0) * (HEAD_DIM ** -0.5)
    return qh.astype(BF16), hmask


def _pipeline3(n, stage_a, stage_b, stage_c, diag_last, alive=None, a_first=False):
    stage_a(0, 0)
    if diag_last:
        @pl.when(n == 1)
        def _():
            stage_b(0, 0, True)

        @pl.when(n >= 2)
        def _():
            stage_b(0, 0, False)
            stage_a(1, 1)
    else:
        stage_a(jnp.minimum(1, n - 1), 1)
        stage_b(0, 0, True)

    def pair(m, carry):
        t = 2 + 2 * m
        if a_first:
            stage_a(t, 0)
            stage_b(t - 1, 1, False)
            stage_c(t - 2, 0)
            stage_a(t + 1, 1)
            stage_b(t, 0, False)
            stage_c(t - 1, 1)
        else:
            stage_c(t - 2, 0)
            stage_b(t - 1, 1, False)
            stage_a(t, 0)
            stage_c(t - 1, 1)
            stage_b(t, 0, False)
            stage_a(t + 1, 1)
        return carry

    pairs = (n - 2) // 2
    if alive is None:
        lax.fori_loop(0, pairs, pair, 0)
        go_on = True
        done = n
    else:
        def more(state):
            return (state[0] < pairs) & state[1]

        def step(state):
            pair(state[0], 0)
            return state[0] + 1, alive()

        m_end, go_on = lax.while_loop(more, step, (jnp.int32(0), jnp.bool_(True)))
        done = jnp.where(go_on, n, 2 * m_end)
    odd = n % 2 == 1

    @pl.when((n >= 3) & odd & go_on)
    def _():
        stage_c(n - 3, 0)
        stage_b(n - 2, 1, False)
        stage_a(n - 1, 0)

    @pl.when((n >= 2) & odd & go_on)
    def _():
        stage_c(n - 2, 1)
        stage_b(n - 1, 0, diag_last)

    @pl.when(odd & go_on)
    def _():
        stage_c(n - 1, 0)

    @pl.when(jnp.logical_not(odd) & go_on)
    def _():
        stage_c(n - 2, 0)
        stage_b(n - 1, 1, diag_last)
        stage_c(n - 1, 1)

    return done


def _lanes2(x):
    return jnp.concatenate([x, x], axis=1)


def _lanes_to_rows(vec, eye):
    return jnp.sum(jnp.where(eye, jnp.broadcast_to(vec, eye.shape), 0.0), axis=1, keepdims=True)


def _rows_to_lanes(rep, eye):
    return jnp.sum(jnp.where(eye, _lanes2(rep), 0.0), axis=0, keepdims=True)


FOX_DEAD = -110.0


def _fox_key_norms(k_ref, kn_s, lane):
    T = k_ref.shape[0]
    rows = min(T, 512)
    for hh in range(2):
        hmask = (lane >= HEAD_DIM * hh) & (lane < HEAD_DIM * (hh + 1))

        def chunk(n, best, hmask=hmask):
            kf = jnp.where(hmask, k_ref[pl.ds(pl.multiple_of(n * rows, rows), rows), :].astype(F32), 0.0)
            sq = jnp.sum(kf * kf, axis=1, keepdims=True)
            return jnp.maximum(best, jnp.max(sq, axis=0, keepdims=True))

        best = lax.fori_loop(0, T // rows, chunk, jnp.zeros((1, 1), F32))
        kn_s[hh] = jnp.broadcast_to(best, kn_s.shape[1:])


def _fox_live_blocks(i, qh_s, kn_s, cq_ref, cke_ref):
    nq = cke_ref.shape[-1]
    jj = lax.broadcasted_iota(jnp.int32, (1, nq), 1)
    first = None
    for hh in range(2):
        qf = qh_s[hh].astype(F32)
        qn = jnp.max(jnp.sum(qf * qf, axis=1, keepdims=True), axis=0, keepdims=True)
        zb = jnp.sqrt(qn * kn_s[hh][0:1, 0:1]) * 1.001
        bound = (2.0 * zb + cq_ref[hh][:, 0:1]) - cke_ref[hh]
        live = (bound >= FOX_DEAD) & (jj <= i)
        f = jnp.min(jnp.where(live, jj, i).astype(F32), axis=1, keepdims=True)
        first = f if first is None else jnp.minimum(first, f)
    return i + 1 - first[0, 0].astype(jnp.int32)


def _ride_along(exchange, at_start, at_middle, at_end):
    @pl.when(at_start)
    def _():
        exchange.start()

    if at_middle is not None:
        @pl.when(at_middle)
        def _():
            exchange.forward()

    def finish():
        @pl.when(at_end)
        def _():
            exchange.finish()

    return finish


def _fox_fwd(proj, c_col, c_row, c_ends, tq, gather=()):
    T = proj.shape[0]
    assert tq == 256
    nq = T // tq
    ng = len(gather)

    def body(*refs):
        q_ref, k_ref, v_ref, cq_ref, ck_ref, cke_ref = refs[:6]
        o_ref, lse_ref = refs[6 + ng:8 + ng]
        qh_s, cq_s, z_s, p_s, al_s, m_s, acc_s, kn_s = refs[8 + 2 * ng:16 + 2 * ng]
        i = pl.program_id(1)
        if ng:
            pair = pl.program_id(0)
            finish = _ride_along(_Gather(refs[6:6 + ng], refs[8 + ng:8 + 2 * ng], *refs[16 + 2 * ng:]),
                                 (pair == 0) & (i == 0), (pair == 1) & (i == 0), (pair == 3) & (i == nq - 1))
        lane = lax.broadcasted_iota(jnp.int32, (1, 128), 1)
        row = lax.broadcasted_iota(jnp.int32, (tq, tq), 0)
        col = lax.broadcasted_iota(jnp.int32, (tq, tq), 1)

        @pl.when(i == 0)
        def _():
            _fox_key_norms(k_ref, kn_s, lane)

        q = q_ref[...]
        for hh in range(2):
            qh_s[hh] = _head_q(q, hh, lane)[0]
            cq_s[hh] = jnp.broadcast_to(_lanes_to_rows(cq_ref[hh], row == col), (tq, tq))
        m_s[...] = jnp.full(m_s.shape, NEG, F32)
        acc_s[...] = jnp.zeros_like(acc_s)

        def rows(t):
            return pl.ds(pl.multiple_of((i - t) * tq, tq), tq)

        def stage_a(t, slot):
            k = k_ref[rows(t), :]
            for hh in range(2):
                z_s[slot, hh] = _nt(qh_s[hh], k)

        def stage_b(t, slot, diag):
            for hh in range(2):
                s = z_s[slot, hh] + cq_s[hh] - ck_ref[hh, :, rows(t)]
                if diag:
                    s = jnp.where(col <= row, s, NEG)
                m = m_s[hh]
                half = jnp.maximum(s[:, :128], s[:, 128:])
                m_new = jnp.maximum(m, jnp.max(half, axis=1, keepdims=True))
                m_s[hh] = m_new
                al_s[slot, hh] = jnp.exp(m - m_new)
                p_s[slot, hh] = jnp.exp(s - _lanes2(m_new)).astype(BF16)

        def stage_c(t, slot):
            v = v_ref[rows(t), :]
            for hh in range(2):
                own = (lane >= HEAD_DIM * hh) & (lane < HEAD_DIM * (hh + 1))
                acc_s[hh] = (al_s[slot, hh] * acc_s[hh]
                             + _nn(p_s[slot, hh], jnp.where(own, v, 1.0).astype(BF16)))

        _pipeline3(_fox_live_blocks(i, qh_s, kn_s, cq_ref, cke_ref), stage_a, stage_b, stage_c, False)
        halves = []
        for hh in range(2):
            acc = acc_s[hh]
            own = (lane >= HEAD_DIM * hh) & (lane < HEAD_DIM * (hh + 1))
            halves.append(jnp.where(own, pltpu.roll(acc, HEAD_DIM, axis=1), acc))
        l0, l1 = halves
        o_ref[...] = jnp.where(lane < HEAD_DIM, acc_s[0] / l0, acc_s[1] / l1)
        lse_ref[0] = _rows_to_lanes(m_s[0] + jnp.log(l0), row == col)
        lse_ref[1] = _rows_to_lanes(m_s[1] + jnp.log(l1), row == col)
        if ng:
            finish()

    res = pl.pallas_call(
        body,
        name="fox_fwd",
        grid=(4, nq),
        in_specs=[
            pl.BlockSpec((tq, 128), lambda p, i: (i, p)),
            pl.BlockSpec((T, 128), lambda p, i: (0, 4 + p)),
            pl.BlockSpec((T, 128), lambda p, i: (0, 8 + p)),
            pl.BlockSpec((2, 1, tq), lambda p, i: (p, 0, i)),
            pl.BlockSpec((2, 1, T), lambda p, i: (p, 0, 0)),
            pl.BlockSpec((2, 1, nq), lambda p, i: (p, 0, 0)),
        ] + [_ANY] * ng,
        out_specs=[
            pl.BlockSpec((tq, 128), lambda p, i: (i, p)),
            pl.BlockSpec((2, 1, tq), lambda p, i: (p, 0, i)),
        ] + [_ANY] * ng,
        out_shape=[
            jax.ShapeDtypeStruct((T, FOX_W), F32),
            jax.ShapeDtypeStruct((N_FOX, 1, T), F32),
        ] + _gathered_shapes(gather),
        scratch_shapes=[
            pltpu.VMEM((2, tq, 128), BF16),
            pltpu.VMEM((2, tq, tq), F32),
            pltpu.VMEM((2, 2, tq, tq), F32),
            pltpu.VMEM((2, 2, tq, tq), BF16),
            pltpu.VMEM((2, 2, tq, 128), F32),
            pltpu.VMEM((2, tq, 128), F32),
            pltpu.VMEM((2, tq, 128), F32),
            pltpu.VMEM((2, 8, 128), F32),
        ] + (_comm_sems(ng) if ng else []),
        compiler_params=_cparams(("arbitrary", "arbitrary")),
    )(proj, proj, proj, c_col, c_row, c_ends, *gather)
    res = list(res)
    return res[0], res[1], res[2:]


def _sb_logs(zn, strict):
    e = jnp.exp2(jnp.abs(zn) * (-LOG2E))
    L = jnp.minimum(zn, 0.0) - jnp.log(1.0 + e)
    G = L - zn
    if strict is not None:
        L = jnp.where(strict, L, 0.0)
    return L, G


SB_DEAD = -110.0


def _sb_fwd(proj, tq):
    T = proj.shape[0]
    nq = T // tq

    def body(q_ref, k_ref, v_ref, o_ref, ltot_ref, live_ref, qh_s, z_s, g_s, tot_s, run_s, acc_s):
        i = pl.program_id(1)
        lane = lax.broadcasted_iota(jnp.int32, (1, 128), 1)
        row = lax.broadcasted_iota(jnp.int32, (tq, tq), 0)
        col = lax.broadcasted_iota(jnp.int32, (tq, tq), 1)
        strict = col < row
        later = jnp.where(row > col, 1.0, 0.0).astype(BF16)
        q = q_ref[...]
        for hh in range(2):
            qh_s[hh] = -_head_q(q, hh, lane)[0]
        run_s[...] = jnp.zeros_like(run_s)
        acc_s[...] = jnp.zeros_like(acc_s)

        def rows(t):
            return pl.ds(pl.multiple_of((i - t) * tq, tq), tq)

        def stage_a(t, slot):
            k = k_ref[rows(t), :]
            for hh in range(2):
                z_s[slot, hh] = _nt(qh_s[hh], k)

        def stage_b(t, slot, diag):
            for hh in range(2):
                L, g = _sb_logs(z_s[slot, hh], strict if diag else None)
                if diag:
                    g = jnp.where(strict, g, NEG)
                after = _split_dot(L, later, SB_SUM_TERMS)
                g_s[slot, hh] = g + after
                first = L[:, 0:1]
                if SB_SUM_TERMS == 1:
                    first = first.astype(BF16).astype(F32)
                tot_s[slot, hh] = jnp.broadcast_to(after[:, 0:1] + first, (tq, 128))

        def stage_c(t, slot):
            v = v_ref[rows(t), :]
            for hh in range(2):
                run = run_s[hh]
                a = jnp.exp(g_s[slot, hh] + _lanes2(run))
                acc_s[hh] += _nn(a.astype(BF16), v)
                run_s[hh] = run + tot_s[slot, hh]

        def alive():
            return jnp.max(jnp.maximum(run_s[0], run_s[1])) > SB_DEAD

        done = _pipeline3(i + 1, stage_a, stage_b, stage_c, False, alive)
        ltot_ref[0] = _rows_to_lanes(run_s[0], row == col)
        ltot_ref[1] = _rows_to_lanes(run_s[1], row == col)
        o_ref[...] = jnp.where(lane < HEAD_DIM, acc_s[0], acc_s[1])
        at = lax.broadcasted_iota(jnp.int32, (1, nq), 1)

        @pl.when(i == 0)
        def _():
            live_ref[0] = jnp.zeros((1, nq), F32)

        live_ref[0] = jnp.where(at == i, done.astype(F32), live_ref[0])

    return pl.pallas_call(
        body,
        name="sb_fwd",
        grid=(4, nq),
        in_specs=[
            pl.BlockSpec((tq, 128), lambda p, i: (i, 12 + p)),
            pl.BlockSpec((T, 128), lambda p, i: (0, 16 + p)),
            pl.BlockSpec((T, 128), lambda p, i: (0, 20 + p)),
        ],
        out_specs=[
            pl.BlockSpec((tq, 128), lambda p, i: (i, p)),
            pl.BlockSpec((2, 1, tq), lambda p, i: (p, 0, i)),
            pl.BlockSpec((1, 1, nq), lambda p, i: (p, 0, 0)),
        ],
        out_shape=[
            jax.ShapeDtypeStruct((T, FOX_W), F32),
            jax.ShapeDtypeStruct((N_FOX, 1, T), F32),
            jax.ShapeDtypeStruct((N_FOX // 2, 1, nq), F32),
        ],
        scratch_shapes=[
            pltpu.VMEM((2, tq, 128), BF16),
            pltpu.VMEM((2, 2, tq, tq), F32),
            pltpu.VMEM((2, 2, tq, tq), F32),
            pltpu.VMEM((2, 2, tq, 128), F32),
            pltpu.VMEM((2, tq, 128), F32),
            pltpu.VMEM((2, tq, 128), F32),
        ],
        compiler_params=_cparams(("arbitrary", "arbitrary")),
    )(proj, proj, proj)


def _post_attn_fwd(fox_o, sb_o, gf, gs, w_out, x, tm):
    T, D = x.shape

    def body(f_ref, s_ref, gf_ref, gs_ref, w_ref, x_ref, x1_ref, mix_ref):
        f = f_ref[...]
        s = s_ref[...]
        mix_ref[:, :FOX_W] = (f * _rstd(f) * gf_ref[...]).astype(BF16)
        mix_ref[:, FOX_W:] = (s * _rstd(s) * gs_ref[...]).astype(BF16)
        x1_ref[...] = x_ref[...] + _nn(mix_ref[...], w_ref[...])

    return pl.pallas_call(
        body,
        name="post_attn_fwd",
        grid=(T // tm,),
        in_specs=[
            pl.BlockSpec((tm, FOX_W), lambda i: (i, 0)),
            pl.BlockSpec((tm, FOX_W), lambda i: (i, 0)),
            pl.BlockSpec((1, FOX_W), lambda i: (0, 0)),
            pl.BlockSpec((1, FOX_W), lambda i: (0, 0)),
            pl.BlockSpec((D, D), lambda i: (0, 0)),
            pl.BlockSpec((tm, D), lambda i: (i, 0)),
        ],
        out_specs=[
            pl.BlockSpec((tm, D), lambda i: (i, 0)),
            pl.BlockSpec((tm, D), lambda i: (i, 0)),
        ],
        out_shape=[jax.ShapeDtypeStruct((T, D), F32), jax.ShapeDtypeStruct((T, D), BF16)],
        compiler_params=_cparams(("arbitrary",)),
    )(fox_o, sb_o, gf, gs, w_out, x)


def _mem_kv_fwd(mem, gm, w_mkv):
    M, D = mem.shape
    N = w_mkv.shape[1]

    def body(mem_ref, g_ref, w_ref, m_ref, kv_ref):
        mv = mem_ref[...]
        m = (mv * _rstd(mv) * g_ref[...]).astype(BF16)
        m_ref[...] = m
        for n0 in range(0, N, 512):
            kv_ref[:, n0:n0 + 512] = _nn(m, w_ref[:, n0:n0 + 512]).astype(BF16)

    return pl.pallas_call(
        body,
        name="mem_kv_fwd",
        out_shape=[jax.ShapeDtypeStruct((M, D), BF16), jax.ShapeDtypeStruct((M, N), BF16)],
        compiler_params=_cparams(),
    )(mem, gm, w_mkv)


def _xattn_probs(qb, kv, h):
    k = kv[:, h * MEM_HD:(h + 1) * MEM_HD]
    s = _nt(qb[:, h * MEM_HD:(h + 1) * MEM_HD], k) * (MEM_HD ** -0.5)
    s = s - jnp.max(s, axis=1, keepdims=True)
    p = jnp.exp(s)
    return p / jnp.sum(p, axis=1, keepdims=True)


def _xattn_fwd(x1, g2, w_mq, kv, w_mo, tm):
    T, D = x1.shape
    M = kv.shape[0]

    def body(x_ref, g_ref, wq_ref, kv_ref, wo_ref, x2_ref, h_ref, q_ref, om_ref):
        xv = x_ref[...]
        h = (xv * _rstd(xv) * g_ref[...]).astype(BF16)
        h_ref[...] = h
        q_ref[...] = _nn(h, wq_ref[...]).astype(BF16)
        qb = q_ref[...]
        kvv = kv_ref[...]
        for hd in range(N_MEM_HEADS):
            p = _xattn_probs(qb, kvv, hd)
            v = kvv[:, D + hd * MEM_HD:D + (hd + 1) * MEM_HD]
            om_ref[:, hd * MEM_HD:(hd + 1) * MEM_HD] = _nn(p.astype(BF16), v).astype(BF16)
        x2_ref[...] = xv + _nn(om_ref[...], wo_ref[...])

    return pl.pallas_call(
        body,
        name="xattn_fwd",
        grid=(T // tm,),
        in_specs=[
            pl.BlockSpec((tm, D), lambda i: (i, 0)),
            pl.BlockSpec((1, D), lambda i: (0, 0)),
            pl.BlockSpec((D, D), lambda i: (0, 0)),
            pl.BlockSpec((M, 2 * D), lambda i: (0, 0)),
            pl.BlockSpec((D, D), lambda i: (0, 0)),
        ],
        out_specs=[pl.BlockSpec((tm, D), lambda i: (i, 0))] * 4,
        out_shape=[jax.ShapeDtypeStruct((T, D), F32)] + [jax.ShapeDtypeStruct((T, D), BF16)] * 3,
        compiler_params=_cparams(("arbitrary",)),
    )(x1, g2, w_mq, kv, w_mo)


def _conv_taps(ext_ref, tm, back):
    if back:
        return ext_ref[pl.ds(6, tm), :], ext_ref[pl.ds(7, tm), :], ext_ref[pl.ds(8, tm), :]
    return ext_ref[pl.ds(0, tm), :], ext_ref[pl.ds(1, tm), :], ext_ref[pl.ds(2, tm), :]


def _ffn_fwd(x2, g3, w_up, conv_w, conv_b, w_down, tm):
    T, D = x2.shape
    fc = FF_CHUNK
    nj = D_FF // fc

    def body(x_ref, g_ref, wg_ref, wv_ref, cwg_ref, cwv_ref, cbg_ref, cbv_ref, wd_ref,
             x3_ref, h_ref, ug_ref, uv_ref, yg_ref, yv_ref, a_ref, acc_ref, carry_ref, ext_ref):
        i = pl.program_id(0)
        j = pl.program_id(1)

        @pl.when(j == 0)
        def _():
            xv = x_ref[...]
            h_ref[...] = (xv * _rstd(xv) * g_ref[...]).astype(BF16)
            acc_ref[...] = xv

        @pl.when(i == 0)
        def _():
            carry_ref[j] = jnp.zeros((2, 8, fc), F32)

        h = h_ref[...]
        halves = []
        for part, (w_ref, cw_ref, cb_ref, u_ref, y_ref) in enumerate(
                ((wg_ref, cwg_ref, cbg_ref, ug_ref, yg_ref), (wv_ref, cwv_ref, cbv_ref, uv_ref, yv_ref))):
            u = _nn(h, w_ref[...])
            u_ref[...] = u.astype(BF16)
            ext = ext_ref.at[part]
            ext[pl.ds(0, 8), :] = carry_ref[j, part]
            ext[pl.ds(8, tm), :] = u
            carry_ref[j, part] = u[tm - 8:, :]
            u2, u1, u0 = _conv_taps(ext, tm, True)
            cw = cw_ref[...]
            y = cb_ref[...] + cw[0:1] * u2 + cw[1:2] * u1 + cw[2:3] * u0
            y_ref[...] = y.astype(BF16)
            halves.append(y)
        gate, val = halves
        a = (gate * jax.nn.sigmoid(gate) * val).astype(BF16)
        a_ref[...] = a
        acc_ref[...] += _nn(a, wd_ref[...])

        @pl.when(j == nj - 1)
        def _():
            x3_ref[...] = acc_ref[...]

    return pl.pallas_call(
        body,
        name="ffn_fwd",
        grid=(T // tm, nj),
        in_specs=[
            pl.BlockSpec((tm, D), lambda i, j: (i, 0)),
            pl.BlockSpec((1, D), lambda i, j: (0, 0)),
            pl.BlockSpec((D, fc), lambda i, j: (0, j)),
            pl.BlockSpec((D, fc), lambda i, j: (0, nj + j)),
            pl.BlockSpec((3, fc), lambda i, j: (0, j)),
            pl.BlockSpec((3, fc), lambda i, j: (0, nj + j)),
            pl.BlockSpec((1, fc), lambda i, j: (0, j)),
            pl.BlockSpec((1, fc), lambda i, j: (0, nj + j)),
            pl.BlockSpec((fc, D), lambda i, j: (j, 0)),
        ],
        out_specs=[
            pl.BlockSpec((tm, D), lambda i, j: (i, 0)),
            pl.BlockSpec((tm, D), lambda i, j: (i, 0)),
        ] + [pl.BlockSpec((tm, fc), lambda i, j: (i, j))] * 5,
        out_shape=[
            jax.ShapeDtypeStruct((T, D), F32),
            jax.ShapeDtypeStruct((T, D), BF16),
        ] + [jax.ShapeDtypeStruct((T, D_FF), BF16)] * 5,
        scratch_shapes=[
            pltpu.VMEM((tm, D), F32),
            pltpu.VMEM((nj, 2, 8, fc), F32),
            pltpu.VMEM((2, tm + 8, fc), F32),
        ],
        compiler_params=_cparams(("arbitrary", "arbitrary")),
    )(x2, g3, w_up, w_up, conv_w, conv_w, conv_b, conv_b, w_down)


def _loss_head(x3, gfin, target, tm):
    T, D = x3.shape

    def body(x_ref, g_ref, t_ref, dx_ref, loss_ref, dg_ref):
        i = pl.program_id(0)

        @pl.when(i == 0)
        def _():
            loss_ref[...] = jnp.zeros_like(loss_ref)
            dg_ref[...] = jnp.zeros_like(dg_ref)

        xv = x_ref[...]
        g = g_ref[...]
        r = _rstd(xv)
        xhat = xv * r
        err = xhat * g - t_ref[...]
        part = jnp.sum(jnp.sum(err * err, axis=1, keepdims=True), axis=0, keepdims=True) * (0.5 / D)
        loss_ref[...] += jnp.broadcast_to(part, loss_ref.shape)
        dy = err * (1.0 / D)
        dg_ref[...] += jnp.sum(dy * xhat, axis=0, keepdims=True)
        dxhat = dy * g
        dx_ref[...] = r * (dxhat - xhat * jnp.mean(dxhat * xhat, axis=-1, keepdims=True))

    return pl.pallas_call(
        body,
        name="loss_head",
        grid=(T // tm,),
        in_specs=[
            pl.BlockSpec((tm, D), lambda i: (i, 0)),
            pl.BlockSpec((1, D), lambda i: (0, 0)),
            pl.BlockSpec((tm, D), lambda i: (i, 0)),
        ],
        out_specs=[
            pl.BlockSpec((tm, D), lambda i: (i, 0)),
            pl.BlockSpec((8, 128), lambda i: (0, 0)),
            pl.BlockSpec((1, D), lambda i: (0, 0)),
        ],
        out_shape=[
            jax.ShapeDtypeStruct((T, D), F32),
            jax.ShapeDtypeStruct((8, 128), F32),
            jax.ShapeDtypeStruct((1, D), F32),
        ],
        compiler_params=_cparams(("arbitrary",)),
    )(x3, gfin, target)


def _ffn_bwd(dx3, x2, g3, ug, uv, yg, yv, conv_w, w_down, w_up, tm):
    T, D = x2.shape
    fc = FF_CHUNK
    nj = D_FF // fc
    nt = T // tm

    def rev(i):
        return nt - 1 - i

    def body(dx3_ref, x_ref, g_ref, ug_ref, uv_ref, yg_ref, yv_ref, cwg_ref, cwv_ref,
             wd_ref, wug_ref, wuv_ref,
             dx2_ref, dug_ref, duv_ref, dg_ref, dcg_ref, dcv_ref,
             acc_ref, carry_ref, ext_ref):
        i = pl.program_id(0)
        j = pl.program_id(1)
        cols = pl.ds(pl.multiple_of(j * fc, fc), fc)

        @pl.when(j == 0)
        def _():
            acc_ref[...] = jnp.zeros_like(acc_ref)

        @pl.when((i == 0) & (j == 0))
        def _():
            dg_ref[...] = jnp.zeros_like(dg_ref)
            dcg_ref[...] = jnp.zeros_like(dcg_ref)
            dcv_ref[...] = jnp.zeros_like(dcv_ref)

        @pl.when(i == 0)
        def _():
            carry_ref[j] = jnp.zeros((2, 8, fc), F32)

        da = _nt(dx3_ref[...].astype(BF16), wd_ref[...])
        gate = yg_ref[...].astype(F32)
        val = yv_ref[...].astype(F32)
        sig = jax.nn.sigmoid(gate)
        silu = gate * sig
        dys = (da * val * (sig * (1.0 + gate * (1.0 - sig))), da * silu)
        for part, (dy, u_ref, cw_ref, du_ref, wu_ref, dc_ref) in enumerate(
                ((dys[0], ug_ref, cwg_ref, dug_ref, wug_ref, dcg_ref),
                 (dys[1], uv_ref, cwv_ref, duv_ref, wuv_ref, dcv_ref))):
            ext = ext_ref.at[part]
            ext[pl.ds(0, tm), :] = dy
            ext[pl.ds(tm, 8), :] = carry_ref[j, part]
            carry_ref[j, part] = dy[:8, :]
            d0, d1, d2 = _conv_taps(ext, tm, False)
            u = u_ref[...].astype(F32)
            upd = jnp.concatenate([
                jnp.sum(u * d2, axis=0, keepdims=True),
                jnp.sum(u * d1, axis=0, keepdims=True),
                jnp.sum(u * d0, axis=0, keepdims=True),
                jnp.sum(d0, axis=0, keepdims=True),
                jnp.zeros((4, fc), F32)], axis=0)
            dc_ref[:, cols] += upd
            cw = cw_ref[...]
            du = (cw[2:3] * d0 + cw[1:2] * d1 + cw[0:1] * d2).astype(BF16)
            du_ref[...] = du
            acc_ref[...] += _nt(du, wu_ref[...])

        @pl.when(j == nj - 1)
        def _():
            dx, dg = _norm_bwd(x_ref[...], g_ref[...], acc_ref[...])
            dx2_ref[...] = dx3_ref[...] + dx
            dg_ref[...] += dg

    return pl.pallas_call(
        body,
        name="ffn_bwd",
        grid=(nt, nj),
        in_specs=[
            pl.BlockSpec((tm, D), lambda i, j: (rev(i), 0)),
            pl.BlockSpec((tm, D), lambda i, j: (rev(i), 0)),
            pl.BlockSpec((1, D), lambda i, j: (0, 0)),
            pl.BlockSpec((tm, fc), lambda i, j: (rev(i), j)),
            pl.BlockSpec((tm, fc), lambda i, j: (rev(i), j)),
            pl.BlockSpec((tm, fc), lambda i, j: (rev(i), j)),
            pl.BlockSpec((tm, fc), lambda i, j: (rev(i), j)),
            pl.BlockSpec((3, fc), lambda i, j: (0, j)),
            pl.BlockSpec((3, fc), lambda i, j: (0, nj + j)),
            pl.BlockSpec((fc, D), lambda i, j: (j, 0)),
            pl.BlockSpec((D, fc), lambda i, j: (0, j)),
            pl.BlockSpec((D, fc), lambda i, j: (0, nj + j)),
        ],
        out_specs=[
            pl.BlockSpec((tm, D), lambda i, j: (rev(i), 0)),
            pl.BlockSpec((tm, fc), lambda i, j: (rev(i), j)),
            pl.BlockSpec((tm, fc), lambda i, j: (rev(i), j)),
            pl.BlockSpec((1, D), lambda i, j: (0, 0)),
            pl.BlockSpec((8, D_FF), lambda i, j: (0, 0)),
            pl.BlockSpec((8, D_FF), lambda i, j: (0, 0)),
        ],
        out_shape=[
            jax.ShapeDtypeStruct((T, D), F32),
            jax.ShapeDtypeStruct((T, D_FF), BF16),
            jax.ShapeDtypeStruct((T, D_FF), BF16),
            jax.ShapeDtypeStruct((1, D), F32),
            jax.ShapeDtypeStruct((8, D_FF), F32),
            jax.ShapeDtypeStruct((8, D_FF), F32),
        ],
        scratch_shapes=[
            pltpu.VMEM((tm, D), F32),
            pltpu.VMEM((nj, 2, 8, fc), F32),
            pltpu.VMEM((2, tm + 8, fc), F32),
        ],
        compiler_params=_cparams(("arbitrary", "arbitrary")),
    )(dx3, x2, g3, ug, uv, yg, yv, conv_w, conv_w, w_down, w_up, w_up)


def _xattn_bwd(dx2, x1, g2, qb, kv, w_mo, w_mq, tm):
    T, D = x1.shape
    M = kv.shape[0]

    def body(dx2_ref, x_ref, g_ref, q_ref, kv_ref, wo_ref, wq_ref, dx1_ref, dq_ref, dkv_ref, dg_ref):
        i = pl.program_id(0)

        @pl.when(i == 0)
        def _():
            dkv_ref[...] = jnp.zeros_like(dkv_ref)
            dg_ref[...] = jnp.zeros_like(dg_ref)

        dxv = dx2_ref[...]
        dom = _nt(dxv.astype(BF16), wo_ref[...]).astype(BF16)
        qb_ = q_ref[...]
        kvv = kv_ref[...]
        for hd in range(N_MEM_HEADS):
            sl = slice(hd * MEM_HD, (hd + 1) * MEM_HD)
            vsl = slice(D + hd * MEM_HD, D + (hd + 1) * MEM_HD)
            p = _xattn_probs(qb_, kvv, hd)
            dp = _nt(dom[:, sl], kvv[:, vsl])
            ds = (p * (dp - jnp.sum(p * dp, axis=1, keepdims=True)) * (MEM_HD ** -0.5)).astype(BF16)
            dq_ref[:, sl] = _nn(ds, kvv[:, sl]).astype(BF16)
            dkv_ref[:, sl] += _tn(ds, qb_[:, sl])
            dkv_ref[:, vsl] += _tn(p.astype(BF16), dom[:, sl])
        dh = _nt(dq_ref[...], wq_ref[...])
        dx, dg = _norm_bwd(x_ref[...], g_ref[...], dh)
        dx1_ref[...] = dxv + dx
        dg_ref[...] += dg

    return pl.pallas_call(
        body,
        name="xattn_bwd",
        grid=(T // tm,),
        in_specs=[
            pl.BlockSpec((tm, D), lambda i: (i, 0)),
            pl.BlockSpec((tm, D), lambda i: (i, 0)),
            pl.BlockSpec((1, D), lambda i: (0, 0)),
            pl.BlockSpec((tm, D), lambda i: (i, 0)),
            pl.BlockSpec((M, 2 * D), lambda i: (0, 0)),
            pl.BlockSpec((D, D), lambda i: (0, 0)),
            pl.BlockSpec((D, D), lambda i: (0, 0)),
        ],
        out_specs=[
            pl.BlockSpec((tm, D), lambda i: (i, 0)),
            pl.BlockSpec((tm, D), lambda i: (i, 0)),
            pl.BlockSpec((M, 2 * D), lambda i: (0, 0)),
            pl.BlockSpec((1, D), lambda i: (0, 0)),
        ],
        out_shape=[
            jax.ShapeDtypeStruct((T, D), F32),
            jax.ShapeDtypeStruct((T, D), BF16),
            jax.ShapeDtypeStruct((M, 2 * D), F32),
            jax.ShapeDtypeStruct((1, D), F32),
        ],
        compiler_params=_cparams(("arbitrary",)),
    )(dx2, x1, g2, qb, kv, w_mo, w_mq)


def _mem_kv_bwd(mem, gm, mb, dkv, w_mkv):
    M, D = mem.shape
    N = dkv.shape[1]

    def body(mem_ref, g_ref, m_ref, dkv_ref, w_ref, dw_ref, dg_ref):
        dkvb = dkv_ref[...].astype(BF16)
        for n0 in range(0, N, 512):
            dw_ref[:, n0:n0 + 512] = _tn(m_ref[...], dkvb[:, n0:n0 + 512]).astype(BF16)
        dm = _nt(dkvb, w_ref[...])
        mv = mem_ref[...]
        dg_ref[...] = jnp.sum(dm * (mv * _rstd(mv)), axis=0, keepdims=True)

    return pl.pallas_call(
        body,
        name="mem_kv_bwd",
        out_shape=[jax.ShapeDtypeStruct((D, N), BF16), jax.ShapeDtypeStruct((1, D), F32)],
        compiler_params=_cparams(),
    )(mem, gm, mb, dkv, w_mkv)


def _post_attn_bwd(dx1, fox_o, sb_o, gf, gs, w_out, tm):
    T, D = dx1.shape

    def body(dx_ref, f_ref, s_ref, gf_ref, gs_ref, w_ref, df_ref, ds_ref, dgf_ref, dgs_ref):
        i = pl.program_id(0)

        @pl.when(i == 0)
        def _():
            dgf_ref[...] = jnp.zeros_like(dgf_ref)
            dgs_ref[...] = jnp.zeros_like(dgs_ref)

        dmix = _nt(dx_ref[...].astype(BF16), w_ref[...])
        d, dg = _norm_bwd(f_ref[...], gf_ref[...], dmix[:, :FOX_W])
        df_ref[...] = d
        dgf_ref[...] += dg
        d, dg = _norm_bwd(s_ref[...], gs_ref[...], dmix[:, FOX_W:])
        ds_ref[...] = d
        dgs_ref[...] += dg

    return pl.pallas_call(
        body,
        name="post_attn_bwd",
        grid=(T // tm,),
        in_specs=[
            pl.BlockSpec((tm, D), lambda i: (i, 0)),
            pl.BlockSpec((tm, FOX_W), lambda i: (i, 0)),
            pl.BlockSpec((tm, FOX_W), lambda i: (i, 0)),
            pl.BlockSpec((1, FOX_W), lambda i: (0, 0)),
            pl.BlockSpec((1, FOX_W), lambda i: (0, 0)),
            pl.BlockSpec((D, D), lambda i: (0, 0)),
        ],
        out_specs=[
            pl.BlockSpec((tm, FOX_W), lambda i: (i, 0)),
            pl.BlockSpec((tm, FOX_W), lambda i: (i, 0)),
            pl.BlockSpec((1, FOX_W), lambda i: (0, 0)),
            pl.BlockSpec((1, FOX_W), lambda i: (0, 0)),
        ],
        out_shape=[
            jax.ShapeDtypeStruct((T, FOX_W), F32),
            jax.ShapeDtypeStruct((T, FOX_W), F32),
            jax.ShapeDtypeStruct((1, FOX_W), F32),
            jax.ShapeDtypeStruct((1, FOX_W), F32),
        ],
        compiler_params=_cparams(("arbitrary",)),
    )(dx1, fox_o, sb_o, gf, gs, w_out)


def _sb_bwd(proj, ltot, live, d_o, tq):
    T = proj.shape[0]
    nq = T // tq

    def body(q_ref, k_ref, v_ref, lt_ref, live_ref, do_ref, dq_ref, dk_ref, dv_ref,
             qh_s, doh_s, lt_s, z_s, da_s, ab_s, dzb_s, run_s, runw_s, dq_s, qt_s, dot_s, dkt_s, dvt_s):
        i = pl.program_id(1)

        @pl.when(i == 0)
        def _():
            dkt_s[...] = jnp.zeros_like(dkt_s)
            dvt_s[...] = jnp.zeros_like(dvt_s)

        lane = lax.broadcasted_iota(jnp.int32, (1, 128), 1)
        row = lax.broadcasted_iota(jnp.int32, (tq, tq), 0)
        col = lax.broadcasted_iota(jnp.int32, (tq, tq), 1)
        strict = col < row
        upto = jnp.where(row <= col, 1.0, 0.0).astype(BF16)
        before = jnp.where(row < col, 1.0, 0.0).astype(BF16)
        q = q_ref[...]
        dov = do_ref[...]
        for hh in range(2):
            qh, hmask = _head_q(q, hh, lane)
            qh_s[hh] = -qh
            doh_s[hh] = jnp.where(hmask, dov, 0.0).astype(BF16)
            lt_s[hh] = jnp.broadcast_to(_lanes_to_rows(lt_ref[hh], row == col), (tq, 128))
        qt_s[...] = (q.astype(F32) * -(HEAD_DIM ** -0.5)).T.astype(BF16)
        dot_s[...] = dov.astype(F32).T.astype(BF16)
        run_s[...] = jnp.zeros_like(run_s)
        runw_s[...] = jnp.zeros_like(runw_s)
        dq_s[...] = jnp.zeros_like(dq_s)

        at = lax.broadcasted_iota(jnp.int32, (1, nq), 1)
        count = jnp.sum(jnp.where(at == i, live_ref[0], 0.0), axis=1, keepdims=True)[0, 0].astype(jnp.int32)
        n_live = jnp.clip(count, 1, i + 1)
        oldest = i + 1 - n_live

        def rows(t):
            return pl.ds(pl.multiple_of((oldest + t) * tq, tq), tq)

        def stage_a(t, slot):
            k = k_ref[rows(t), :]
            v = v_ref[rows(t), :]
            for hh in range(2):
                z_s[slot, hh] = _nt(qh_s[hh], k)
                da_s[slot, hh] = _nt(doh_s[hh], v)

        def stage_b(t, slot, diag):
            for hh in range(2):
                L, g = _sb_logs(z_s[slot, hh], strict if diag else None)
                upto_s = _split_dot(L, upto, SB_SUM_TERMS)
                run = run_s[hh]
                arg = (g + _lanes2(lt_s[hh] - run)) - upto_s
                if diag:
                    arg = jnp.where(strict, arg, NEG)
                a = jnp.exp(arg)
                w = a * da_s[slot, hh]
                w_before = _split_dot(w, before, SB_SUM_TERMS)
                run_w = runw_s[hh]
                d_keep = w_before + _lanes2(run_w)
                beta = jnp.exp(g)
                ndz = beta * (w + d_keep) - w
                if diag:
                    ndz = jnp.where(strict, ndz, 0.0)
                dzb_s[slot, hh] = ndz.astype(BF16)
                ab_s[slot, hh] = a.astype(BF16)
                run_s[hh] = run + jnp.broadcast_to(upto_s[:, tq - 1:tq], (tq, 128))
                runw_s[hh] = run_w + jnp.broadcast_to(w_before[:, tq - 1:tq] + w[:, tq - 1:tq], (tq, 128))

        def stage_c(t, slot):
            k = k_ref[rows(t), :]
            for hh in range(2):
                dzb = dzb_s[slot, hh]
                dq_s[hh] += _nn(dzb, k)
                dims = pl.ds(HEAD_DIM * hh, HEAD_DIM)
                dkt_s[oldest + t, dims, :] += _nn(qt_s[dims, :], dzb)
                dvt_s[oldest + t, dims, :] += _nn(dot_s[dims, :], ab_s[slot, hh])

        _pipeline3(n_live, stage_a, stage_b, stage_c, True)
        dq_ref[...] = (jnp.where(lane < HEAD_DIM, dq_s[0], dq_s[1]) * -(HEAD_DIM ** -0.5)).astype(BF16)

        @pl.when(i == nq - 1)
        def _():
            def flush(n, carry):
                keys = pl.ds(pl.multiple_of(n * tq, tq), tq)
                dk_ref[keys, :] = dkt_s[n].T
                dv_ref[keys, :] = dvt_s[n].T
                return carry

            lax.fori_loop(0, nq, flush, 0)

    return pl.pallas_call(
        body,
        name="sb_bwd",
        grid=(4, nq),
        in_specs=[
            pl.BlockSpec((tq, 128), lambda p, i: (i, 12 + p)),
            pl.BlockSpec((T, 128), lambda p, i: (0, 16 + p)),
            pl.BlockSpec((T, 128), lambda p, i: (0, 20 + p)),
            pl.BlockSpec((2, 1, tq), lambda p, i: (p, 0, i)),
            pl.BlockSpec((1, 1, nq), lambda p, i: (p, 0, 0)),
            pl.BlockSpec((tq, 128), lambda p, i: (i, p)),
        ],
        out_specs=[
            pl.BlockSpec((tq, 128), lambda p, i: (i, p)),
            pl.BlockSpec((T, 128), lambda p, i: (0, p)),
            pl.BlockSpec((T, 128), lambda p, i: (0, p)),
        ],
        out_shape=[
            jax.ShapeDtypeStruct((T, FOX_W), BF16),
            jax.ShapeDtypeStruct((T, FOX_W), F32),
            jax.ShapeDtypeStruct((T, FOX_W), F32),
        ],
        scratch_shapes=[
            pltpu.VMEM((2, tq, 128), BF16),
            pltpu.VMEM((2, tq, 128), BF16),
            pltpu.VMEM((2, tq, 128), F32),
            pltpu.VMEM((2, 2, tq, tq), F32),
            pltpu.VMEM((2, 2, tq, tq), F32),
            pltpu.VMEM((2, 2, tq, tq), BF16),
            pltpu.VMEM((2, 2, tq, tq), BF16),
            pltpu.VMEM((2, tq, 128), F32),
            pltpu.VMEM((2, tq, 128), F32),
            pltpu.VMEM((2, tq, 128), F32),
            pltpu.VMEM((128, tq), BF16),
            pltpu.VMEM((128, tq), BF16),
            pltpu.VMEM((nq, 128, tq), F32),
            pltpu.VMEM((nq, 128, tq), F32),
        ],
        compiler_params=_cparams(("arbitrary", "arbitrary")),
    )(proj, proj, proj, ltot, live, d_o)


def _fox_bwd(proj, c_col, c_row, c_ends, lse, d_o, o, tq, scatter=()):
    T = proj.shape[0]
    nq = T // tq
    ns = len(scatter)

    def body(*refs):
        q_ref, k_ref, v_ref, cq_ref, ck_ref, cke_ref, lse_ref, do_ref, o_ref = refs[:9]
        dq_ref, dk_ref, dv_ref, dck_ref, dcq_ref = refs[9 + ns:14 + ns]
        (qh_s, doh_s, delta_s, shift_s, z_s, dp_s, pb_s, dsb_s, rs_s, dq_s,
         kn_s, qt_s, dot_s, dkt_s, dvt_s, kt_s) = refs[14 + 2 * ns:30 + 2 * ns]
        i = pl.program_id(1)
        if ns:
            pair = pl.program_id(0)
            finish = _ride_along(_Scatter(refs[9:9 + ns], refs[14 + ns:14 + 2 * ns], *refs[30 + 2 * ns:]),
                                 (pair == 0) & (i == 0), None, (pair == 3) & (i == nq - 1))
        lane = lax.broadcasted_iota(jnp.int32, (1, 128), 1)

        @pl.when(i == 0)
        def _():
            dkt_s[...] = jnp.zeros_like(dkt_s)
            dvt_s[...] = jnp.zeros_like(dvt_s)
            dck_ref[...] = jnp.zeros_like(dck_ref)
            _fox_key_norms(k_ref, kn_s, lane)

            def turn(n, carry):
                kt_s[n] = k_ref[pl.ds(pl.multiple_of(n * tq, tq), tq), :].astype(F32).T.astype(BF16)
                return carry

            lax.fori_loop(0, nq, turn, 0)

        row = lax.broadcasted_iota(jnp.int32, (tq, tq), 0)
        col = lax.broadcasted_iota(jnp.int32, (tq, tq), 1)
        q = q_ref[...]
        dov = do_ref[...]
        ov = o_ref[...]
        qt_s[...] = (q.astype(F32) * (HEAD_DIM ** -0.5)).T.astype(BF16)
        dot_s[...] = dov.astype(F32).T.astype(BF16)
        for hh in range(2):
            qh, hmask = _head_q(q, hh, lane)
            dohb = jnp.where(hmask, dov, 0.0).astype(BF16)
            qh_s[hh] = qh
            doh_s[hh] = dohb
            delta_s[hh] = jnp.broadcast_to(jnp.sum(dohb.astype(F32) * ov, axis=1, keepdims=True), (tq, tq))
            shift_s[hh] = jnp.broadcast_to(_lanes_to_rows(cq_ref[hh] - lse_ref[hh], row == col), (tq, tq))
        rs_s[...] = jnp.zeros_like(rs_s)
        dq_s[...] = jnp.zeros_like(dq_s)

        def rows(t):
            return pl.ds(pl.multiple_of((i - t) * tq, tq), tq)

        def stage_a(t, slot):
            k = k_ref[rows(t), :]
            v = v_ref[rows(t), :]
            for hh in range(2):
                z_s[slot, hh] = _nt(qh_s[hh], k)
                dp_s[slot, hh] = _nt(doh_s[hh], v)

        def stage_b(t, slot, diag):
            for hh in range(2):
                s = z_s[slot, hh] + shift_s[hh] - ck_ref[hh, :, rows(t)]
                if diag:
                    s = jnp.where(col <= row, s, NEG)
                p = jnp.exp(s)
                ds = p * (dp_s[slot, hh] - delta_s[hh])
                pb_s[slot, hh] = p.astype(BF16)
                dsb_s[slot, hh] = ds.astype(BF16)
                dck_ref[hh, :, rows(t)] += jnp.sum(ds, axis=0, keepdims=True)
                rs_s[hh] += jnp.sum(ds, axis=1, keepdims=True)

        def stage_c(t, slot):
            for hh in range(2):
                dsb = dsb_s[slot, hh]
                dims = pl.ds(HEAD_DIM * hh, HEAD_DIM)
                dq_s[dims, :] += _nt(kt_s[i - t, dims, :], dsb)
                dkt_s[i - t, dims, :] += _nn(qt_s[dims, :], dsb)
                dvt_s[i - t, dims, :] += _nn(dot_s[dims, :], pb_s[slot, hh])

        _pipeline3(_fox_live_blocks(i, qh_s, kn_s, cq_ref, cke_ref), stage_a, stage_b, stage_c, False,
                   a_first=True)

        @pl.when(i == nq - 1)
        def _():
            def flush(n, carry):
                keys = pl.ds(pl.multiple_of(n * tq, tq), tq)
                dk_ref[keys, :] = dkt_s[n].T
                dv_ref[keys, :] = dvt_s[n].T
                return carry

            lax.fori_loop(0, nq, flush, 0)
        dcq_ref[0] = _rows_to_lanes(rs_s[0], row == col)
        dcq_ref[1] = _rows_to_lanes(rs_s[1], row == col)
        dq_ref[...] = (dq_s[...].T * (HEAD_DIM ** -0.5)).astype(BF16)
        if ns:
            finish()

    res = pl.pallas_call(
        body,
        name="fox_bwd",
        grid=(4, nq),
        in_specs=[
            pl.BlockSpec((tq, 128), lambda p, i: (i, p)),
            pl.BlockSpec((T, 128), lambda p, i: (0, 4 + p)),
            pl.BlockSpec((T, 128), lambda p, i: (0, 8 + p)),
            pl.BlockSpec((2, 1, tq), lambda p, i: (p, 0, i)),
            pl.BlockSpec((2, 1, T), lambda p, i: (p, 0, 0)),
            pl.BlockSpec((2, 1, nq), lambda p, i: (p, 0, 0)),
            pl.BlockSpec((2, 1, tq), lambda p, i: (p, 0, i)),
            pl.BlockSpec((tq, 128), lambda p, i: (i, p)),
            pl.BlockSpec((tq, 128), lambda p, i: (i, p)),
        ] + [_ANY] * ns,
        out_specs=[
            pl.BlockSpec((tq, 128), lambda p, i: (i, p)),
            pl.BlockSpec((T, 128), lambda p, i: (0, p)),
            pl.BlockSpec((T, 128), lambda p, i: (0, p)),
            pl.BlockSpec((2, 1, T), lambda p, i: (p, 0, 0)),
            pl.BlockSpec((2, 1, tq), lambda p, i: (p, 0, i)),
        ] + [_ANY] * ns,
        out_shape=[
            jax.ShapeDtypeStruct((T, FOX_W), BF16),
            jax.ShapeDtypeStruct((T, FOX_W), F32),
            jax.ShapeDtypeStruct((T, FOX_W), F32),
            jax.ShapeDtypeStruct((N_FOX, 1, T), F32),
            jax.ShapeDtypeStruct((N_FOX, 1, T), F32),
        ] + [jax.ShapeDtypeStruct(b.shape, b.dtype) for b in scatter],
        scratch_shapes=[
            pltpu.VMEM((2, tq, 128), BF16),
            pltpu.VMEM((2, tq, 128), BF16),
            pltpu.VMEM((2, tq, tq), F32),
            pltpu.VMEM((2, tq, tq), F32),
            pltpu.VMEM((2, 2, tq, tq), F32),
            pltpu.VMEM((2, 2, tq, tq), F32),
            pltpu.VMEM((2, 2, tq, tq), BF16),
            pltpu.VMEM((2, 2, tq, tq), BF16),
            pltpu.VMEM((2, tq, 128), F32),
            pltpu.VMEM((128, tq), F32),
            pltpu.VMEM((2, 8, 128), F32),
            pltpu.VMEM((128, tq), BF16),
            pltpu.VMEM((128, tq), BF16),
            pltpu.VMEM((nq, 128, tq), F32),
            pltpu.VMEM((nq, 128, tq), F32),
            pltpu.VMEM((nq, 128, tq), BF16),
        ] + (_comm_sems(ns) if ns else []),
        compiler_params=_cparams(("arbitrary", "arbitrary")),
    )(proj, proj, proj, c_col, c_row, c_ends, lse, d_o, o, *scatter)
    res = list(res)
    return (*res[:5], res[5:])


def _forget_bwd(dcq, dck, xf, h1, tc):
    H, T = xf.shape
    D = h1.shape[1]
    nc = T // tc

    def body(dcq_ref, dck_ref, xf_ref, h_ref, dxf_ref, db_ref, dwf_ref):
        row = lax.broadcasted_iota(jnp.int32, (tc, tc), 0)
        col = lax.broadcasted_iota(jnp.int32, (tc, tc), 1)
        from_here = jnp.where(row >= col, 1.0, 0.0).astype(BF16)

        def chunk(n, carry):
            run, db, dwf = carry
            cs = pl.multiple_of((nc - 1 - n) * tc, tc)
            dc = dcq_ref[:, pl.ds(cs, tc)] - dck_ref[:, pl.ds(cs, tc)]
            dlogf = _split_dot(dc, from_here, 3) + run
            xfv = xf_ref[:, pl.ds(cs, tc)]
            dxf = dlogf * jax.nn.sigmoid(-xfv)
            dxf_ref[:, pl.ds(cs, tc)] = dxf
            dwf = dwf + _nn(dxf.astype(BF16), h_ref[pl.ds(cs, tc), :])
            return dlogf[:, 0:1], db + jnp.sum(dxf, axis=1, keepdims=True), dwf

        zero = jnp.zeros((H, 1), F32)
        _, db, dwf = lax.fori_loop(0, nc, chunk, (zero, zero, jnp.zeros((H, D), F32)))
        db_ref[...] = db
        dwf_ref[...] = dwf

    return pl.pallas_call(
        body,
        name="forget_bwd",
        out_shape=[jax.ShapeDtypeStruct((H, T), F32), jax.ShapeDtypeStruct((H, 1), F32),
                   jax.ShapeDtypeStruct((H, D), F32)],
        compiler_params=_cparams(),
    )(dcq, dck, xf, h1)


def _inproj_bwd(pieces, dxf_t, w_in, w_f_t, x, g1, dx1, tm, scatter=()):
    T, D = x.shape
    N = w_in.shape[1]
    ns = len(scatter)
    nt = T // tm
    npc = len(pieces)

    def body(*refs):
        pc_refs = refs[:npc]
        dxf_ref, w_ref, wf_ref, x_ref, g_ref, dx1_ref = refs[npc:npc + 6]
        base = npc + 6
        dx_ref, dg_ref = refs[base + ns:base + 2 + ns]
        i = pl.program_id(0)
        if ns:
            exchange = _Scatter(refs[base:base + ns], refs[base + 2 + ns:base + 2 + 2 * ns],
                                *refs[base + 2 + 2 * ns:])

            @pl.when(i == 0)
            def _():
                exchange.start()

        @pl.when(i == 0)
        def _():
            dg_ref[...] = jnp.zeros_like(dg_ref)

        dh = _nn(dxf_ref[...], wf_ref[...].astype(F32))
        for k, pc_ref in enumerate(pc_refs):
            dh = dh + _nt(pc_ref[...].astype(BF16), w_ref[:, k * FOX_W:(k + 1) * FOX_W])
        dx, dg = _norm_bwd(x_ref[...], g_ref[...], dh)
        dx_ref[...] = dx1_ref[...] + dx
        dg_ref[...] += dg
        if ns:
            @pl.when(i == nt - 1)
            def _():
                exchange.finish()

    res = pl.pallas_call(
        body,
        name="inproj_bwd",
        grid=(nt,),
        in_specs=[pl.BlockSpec((tm, FOX_W), lambda i: (i, 0))] * npc + [
            pl.BlockSpec((tm, N_FOX), lambda i: (i, 0)),
            pl.BlockSpec((D, N), lambda i: (0, 0)),
            pl.BlockSpec((N_FOX, D), lambda i: (0, 0)),
            pl.BlockSpec((tm, D), lambda i: (i, 0)),
            pl.BlockSpec((1, D), lambda i: (0, 0)),
            pl.BlockSpec((tm, D), lambda i: (i, 0)),
        ] + [_ANY] * ns,
        out_specs=[
            pl.BlockSpec((tm, D), lambda i: (i, 0)),
            pl.BlockSpec((1, D), lambda i: (0, 0)),
        ] + [_ANY] * ns,
        out_shape=[jax.ShapeDtypeStruct((T, D), F32), jax.ShapeDtypeStruct((1, D), F32)]
        + [jax.ShapeDtypeStruct(b.shape, b.dtype) for b in scatter],
        scratch_shapes=_comm_sems(ns) if ns else [],
        compiler_params=_cparams(("arbitrary",)),
    )(*pieces, dxf_t, w_in, w_f_t, x, g1, dx1, *scatter)
    res = list(res)
    return res[0], res[1], res[2:]


def _dw_in(h1, pieces, name):
    T, K = h1.shape
    bt = min(T, 512)
    nt = T // bt
    npc = len(pieces)

    def body(*refs):
        a_ref = refs[0]
        pc_refs = refs[1:1 + npc]
        o_ref, acc_ref = refs[1 + npc:]
        t = pl.program_id(0)

        @pl.when(t == 0)
        def _():
            acc_ref[...] = jnp.zeros_like(acc_ref)

        a = a_ref[...]
        for k, pc_ref in enumerate(pc_refs):
            acc_ref[:, k * FOX_W:(k + 1) * FOX_W] += _tn(a, pc_ref[...].astype(BF16))

        @pl.when(t == nt - 1)
        def _():
            o_ref[...] = acc_ref[...].astype(BF16)

    return pl.pallas_call(
        body,
        name=name,
        grid=(nt,),
        in_specs=[pl.BlockSpec((bt, K), lambda t: (t, 0))] + [pl.BlockSpec((bt, FOX_W), lambda t: (t, 0))] * npc,
        out_specs=pl.BlockSpec((K, npc * FOX_W), lambda t: (0, 0)),
        out_shape=jax.ShapeDtypeStruct((K, npc * FOX_W), BF16),
        scratch_shapes=[pltpu.VMEM((K, npc * FOX_W), F32)],
        compiler_params=_cparams(("arbitrary",)),
    )(h1, *pieces)


def _matmul_tn(a, b, name, cast_b=False):
    T, K = a.shape
    N = b.shape[1]
    bt = min(T, 512)
    bk = _tile_div(K, 1536)
    bn = _tile_div(N, 1536)
    nt = T // bt

    def body(a_ref, b_ref, o_ref, acc_ref):
        t = pl.program_id(2)

        @pl.when(t == 0)
        def _():
            acc_ref[...] = jnp.zeros_like(acc_ref)

        bv = b_ref[...]
        if cast_b:
            bv = bv.astype(BF16)
        acc_ref[...] += _tn(a_ref[...], bv)

        @pl.when(t == nt - 1)
        def _():
            o_ref[...] = acc_ref[...].astype(BF16)

    return pl.pallas_call(
        body,
        name=name,
        grid=(K // bk, N // bn, nt),
        in_specs=[
            pl.BlockSpec((bt, bk), lambda k, n, t: (t, k)),
            pl.BlockSpec((bt, bn), lambda k, n, t: (t, n)),
        ],
        out_specs=pl.BlockSpec((bk, bn), lambda k, n, t: (k, n)),
        out_shape=jax.ShapeDtypeStruct((K, N), BF16),
        scratch_shapes=[pltpu.VMEM((bk, bn), F32)],
        compiler_params=_cparams(("arbitrary", "arbitrary", "arbitrary")),
    )(a, b)


def _local_step(x, mem, target, p, tm, tq, late=None):
    T, D = x.shape
    w_in = p["w_in"]
    w_qkv = w_in[:, :QKV_W]
    w_f_t = w_in[:, QKV_W:].T
    b_f = p["b_forget"].reshape(N_FOX, 1)

    proj, h1, xf, c = _inproj_fwd(x, p["attn_norm_g"], w_qkv, w_f_t, b_f, tm)
    c_col = c.reshape(N_FOX, 1, T)
    c_row = c.reshape(N_FOX, 1, T)
    c_ends = c[:, tq - 1::tq].reshape(N_FOX, 1, T // tq)
    fox_o, lse, gathered = _fox_fwd(proj, c_col, c_row, c_ends, tq, gather=[late[n] for n in _LATE] if late else ())
    if late:
        p = dict(p, **{n: _gathered_full(n, gv) for n, gv in zip(_LATE, gathered)})
    sb_o, sb_ltot, sb_live = _sb_fwd(proj, tq)
    x1, mixed = _post_attn_fwd(fox_o, sb_o, p["fox_out_g"], p["sb_out_g"], p["w_out"], x, tm)
    mb, kv = _mem_kv_fwd(mem, p["mem_norm_g"], p["w_mkv"])
    x2, h2, qb, om = _xattn_fwd(x1, p["xattn_norm_g"], p["w_mq"], kv, p["w_mo"], tm)
    tf = 2 * tm if T % (2 * tm) == 0 else tm
    x3, h3, ug, uv, yg, yv, a = _ffn_fwd(
        x2, p["ffn_norm_g"], p["w_up"], p["conv_w"], p["conv_b"], p["w_down"], tf)
    dx3, loss_blk, d_final_g = _loss_head(x3, p["final_norm_g"], target, tm)

    g = {"final_norm_g": d_final_g}
    dx2, du_g, du_v, g["ffn_norm_g"], dc_g, dc_v = _ffn_bwd(
        dx3, x2, p["ffn_norm_g"], ug, uv, yg, yv, p["conv_w"], p["w_down"], p["w_up"], tf)
    g["w_down"] = _matmul_tn(a, dx3, "dw_down", cast_b=True)
    g["w_up"] = jnp.concatenate([_matmul_tn(h3, du_g, "dw_up_gate"), _matmul_tn(h3, du_v, "dw_up_val")], axis=1)
    dconv = jnp.concatenate([dc_g, dc_v], axis=1)
    g["conv_w"] = dconv[0:3]
    g["conv_b"] = dconv[3:4]
    dx1, dq_m, dkv, g["xattn_norm_g"] = _xattn_bwd(dx2, x1, p["xattn_norm_g"], qb, kv, p["w_mo"], p["w_mq"], tm)
    g["w_mo"] = _matmul_tn(om, dx2, "dw_mo", cast_b=True)
    g["w_mq"] = _matmul_tn(h2, dq_m, "dw_mq")
    g["w_mkv"], g["mem_norm_g"] = _mem_kv_bwd(mem, p["mem_norm_g"], mb, dkv, p["w_mkv"])
    d_fox, d_sb, g["fox_out_g"], g["sb_out_g"] = _post_attn_bwd(
        dx1, fox_o, sb_o, p["fox_out_g"], p["sb_out_g"], p["w_out"], tm)
    g["w_out"] = _matmul_tn(mixed, dx1, "dw_out", cast_b=True)
    dq_s, dk_s, dv_s = _sb_bwd(proj, sb_ltot, sb_live, d_sb, tq)
    dq_f, dk_f, dv_f, dck, dcq, parts = _fox_bwd(
        proj, c_col, c_row, c_ends, lse, d_fox, fox_o, tq,
        scatter=[_grad_blocks(n, g[n]) for n in _LATE] if late else ())
    if late:
        g["parts"] = dict(zip(_LATE, parts))
    dxf, db, dwf_t = _forget_bwd(dcq.reshape(N_FOX, T), dck.reshape(N_FOX, T), xf, h1, min(T, 512))
    g["b_forget"] = db.reshape(1, N_FOX)
    pieces = [dq_f, dk_f, dv_f, dq_s, dk_s, dv_s]
    g["w_in"] = jnp.concatenate([_dw_in(h1, pieces, "dw_in"), dwf_t.T.astype(BF16)], axis=1)
    grad_x, g["attn_norm_g"], parts = _inproj_bwd(
        pieces, dxf.T, w_in, w_f_t, x, p["attn_norm_g"], dx1, tm,
        scatter=[_grad_blocks("w_in", g["w_in"])] if late else ())
    if late:
        (g["parts"]["w_in"],) = parts
    return loss_blk, grad_x, g


def _mesh_pos():
    return lax.axis_index("x"), lax.axis_index("y"), lax.axis_index("c")


def _flip(pos, k):
    return tuple(1 - v if (k >> b) & 1 else v for v, b in zip(pos, (2, 1, 0)))


def _slot(pos):
    return 4 * pos[0] + 2 * pos[1] + pos[2]


_CHIPS = (4, 2, 6)


def _comm_sems(n):
    return [pltpu.SemaphoreType.DMA((7 * n,)), pltpu.SemaphoreType.DMA((7 * n,)), pltpu.SemaphoreType.DMA((n,))]


class _Gather:
    def __init__(self, ins, outs, send_sems, recv_sems, local_sems):
        self.ins, self.outs, self.n = ins, outs, len(ins)
        self.send_sems, self.recv_sems, self.local_sems = send_sems, recv_sems, local_sems
        self.me = _mesh_pos()
        self.sibling = _flip(self.me, 1)

    def _copy(self, a, kk, block, to, src=None):
        rows = self.outs[a].at[_slot(block)]
        return pltpu.make_async_remote_copy(
            src_ref=rows if src is None else src, dst_ref=rows,
            send_sem=self.send_sems.at[7 * a + kk], recv_sem=self.recv_sems.at[7 * a + kk],
            device_id=to, device_id_type=MESH)

    def _mine(self):
        return [pltpu.make_async_copy(self.ins[a], self.outs[a].at[_slot(self.me)], self.local_sems.at[a])
                for a in range(self.n)]

    def _first(self):
        out = []
        for a in range(self.n):
            out.append(self._copy(a, 0, self.me, self.sibling, src=self.ins[a]))
            out += [self._copy(a, 1 + j, self.me, _flip(self.me, k), src=self.ins[a]) for j, k in enumerate(_CHIPS)]
        return out

    def _passed(self):
        return [self._copy(a, 4 + j, _flip(self.me, k), self.sibling)
                for j, k in enumerate(_CHIPS) for a in range(self.n)]

    def start(self):
        for cp in self._mine() + self._first():
            cp.start()

    def forward(self):
        for j, k in enumerate(_CHIPS):
            for a in range(self.n):
                self._copy(a, 1 + j, _flip(self.me, k), self.me).wait_recv()
                self._copy(a, 4 + j, _flip(self.me, k), self.sibling).start()

    def finish(self):
        for a in range(self.n):
            self._copy(a, 0, self.sibling, self.me).wait_recv()
            for j, k in enumerate(_CHIPS):
                self._copy(a, 4 + j, _flip(self.sibling, k), self.me).wait_recv()
        for cp in self._first() + self._passed():
            cp.wait_send()
        for cp in self._mine():
            cp.wait()


class _Scatter:
    def __init__(self, ins, outs, send_sems, recv_sems, local_sems):
        self.ins, self.outs, self.n = ins, outs, len(ins)
        self.send_sems, self.recv_sems, self.local_sems = send_sems, recv_sems, local_sems
        self.me = _mesh_pos()

    def _copy(self, a, k, landed=False):
        peer = _flip(self.me, k)
        return pltpu.make_async_remote_copy(
            src_ref=self.ins[a].at[_slot(peer)], dst_ref=self.outs[a].at[_slot(peer if landed else self.me)],
            send_sem=self.send_sems.at[7 * a + k - 1], recv_sem=self.recv_sems.at[7 * a + k - 1],
            device_id=peer, device_id_type=MESH)

    def _mine(self):
        s = _slot(self.me)
        return [pltpu.make_async_copy(self.ins[a].at[s], self.outs[a].at[s], self.local_sems.at[a])
                for a in range(self.n)]

    def start(self):
        for cp in self._mine() + [self._copy(a, k) for k in range(1, 8) for a in range(self.n)]:
            cp.start()

    def finish(self):
        for k in range(1, 8):
            for a in range(self.n):
                self._copy(a, k, landed=True).wait_recv()
        for k in range(1, 8):
            for a in range(self.n):
                self._copy(a, k).wait_send()
        for cp in self._mine():
            cp.wait()


_ANY = pl.BlockSpec(memory_space=pl.ANY)


def _gathered_shapes(shards):
    return [jax.ShapeDtypeStruct((N_DEV,) + s.shape, s.dtype) for s in shards]


def _all_gather(shards, name):
    n = len(shards)

    def body(*refs):
        g = _Gather(refs[:n], refs[n:2 * n], *refs[2 * n:])
        g.start()
        g.forward()
        g.finish()

    return pl.pallas_call(
        body, name=name, in_specs=[_ANY] * n, out_specs=[_ANY] * n,
        out_shape=_gathered_shapes(shards), scratch_shapes=_comm_sems(n),
    )(*shards)


def _adamw_math(w, g, m, v):
    m2 = ADAM_B1 * m + (1.0 - ADAM_B1) * g
    v2 = ADAM_B2 * v + (1.0 - ADAM_B2) * (g * g)
    m_hat = m2 / (1.0 - ADAM_B1 ** ADAM_STEP)
    v_hat = v2 / (1.0 - ADAM_B2 ** ADAM_STEP)
    delta = -ADAM_LR * (m_hat / (jnp.sqrt(v_hat) + ADAM_EPS) + ADAM_WD * w)
    return delta, m2, v2


def _adamw(w, parts, m, v, name):
    R, C = w.shape
    br = 128 if R % 128 == 0 else R

    def body(w_ref, p_ref, m_ref, v_ref, g_ref, d_ref, nm_ref, nv_ref):
        g = p_ref[0].astype(F32)
        for s in range(1, N_DEV):
            g = g + p_ref[s].astype(F32)
        g_ref[...] = g
        d_ref[...], nm_ref[...], nv_ref[...] = _adamw_math(w_ref[...], g, m_ref[...], v_ref[...])

    spec = pl.BlockSpec((br, C), lambda i: (i, 0))
    return pl.pallas_call(
        body,
        name=name,
        grid=(R // br,),
        in_specs=[spec, pl.BlockSpec((N_DEV, br, C), lambda i: (0, i, 0)), spec, spec],
        out_specs=[spec] * 4,
        out_shape=[jax.ShapeDtypeStruct((R, C), F32)] * 4,
        compiler_params=_cparams(("arbitrary",)),
    )(w, parts, m, v)


_SHARDED = ("w_in", "w_out", "w_mq", "w_mkv", "w_mo", "w_up", "conv_w", "w_down")
_LATE = _SHARDED[1:]
_COL_SHARDED = ("w_in", "w_mkv", "w_up", "conv_w")
_REPLICATED = ("attn_norm_g", "b_forget", "fox_out_g", "sb_out_g", "xattn_norm_g", "mem_norm_g",
               "ffn_norm_g", "conv_b", "final_norm_g")
_WEIGHTS = ("attn_norm_g", "w_in", "b_forget", "fox_out_g", "sb_out_g", "w_out", "xattn_norm_g", "mem_norm_g",
            "w_mq", "w_mkv", "w_mo", "ffn_norm_g", "w_up", "conv_w", "conv_b", "w_down", "final_norm_g")


def _pack_rows(n):
    return -(-n // 128)


def _pack(vals, rows_total):
    parts = []
    for v in vals:
        flat = v.reshape(-1)
        parts.append(jnp.pad(flat, (0, _pack_rows(flat.shape[0]) * 128 - flat.shape[0])))
    flat = jnp.concatenate(parts)
    return jnp.pad(flat, (0, rows_total * 128 - flat.shape[0])).reshape(rows_total, 128)


def _unpack(packed, shapes):
    out = []
    r = 0
    for shp in shapes:
        n = 1
        for d in shp:
            n *= d
        out.append(packed[r:r + _pack_rows(n)].reshape(-1)[:n].reshape(shp))
        r += _pack_rows(n)
    return out


def _gathered_full(name, gathered):
    if name in _COL_SHARDED:
        return jnp.transpose(gathered, (1, 0, 2)).reshape(gathered.shape[1], -1)
    return gathered.reshape(-1, gathered.shape[2])


def _to_blocks(name, full):
    if name in _COL_SHARDED:
        r = full.shape[0]
        return jnp.transpose(full.reshape(r, N_DEV, -1), (1, 0, 2))
    return full.reshape(N_DEV, -1, full.shape[1])


def _grad_blocks(name, full):
    blocks = _to_blocks(name, full)
    return blocks if name == "conv_w" else blocks.astype(BF16)


def _step(args, tm, tq):
    w = {n: args[n] for n in _WEIGHTS}
    mom = {n: args["m_" + n] for n in _WEIGHTS}
    var = {n: args["v_" + n] for n in _WEIGHTS}
    x = args["x"][0]
    mem = args["mem"][0]
    target = args["loss_target"][0]

    def flat2(a):
        return a.reshape(a.shape[-2], a.shape[-1]) if a.ndim == 3 else a.reshape(1, -1)

    shards = {n: flat2(w[n]) if n == "conv_w" else flat2(w[n]).astype(BF16) for n in _SHARDED}
    (w_in_all,) = _all_gather([shards["w_in"]], "gather_w_in")
    p = {"w_in": _gathered_full("w_in", w_in_all)}
    for n in _REPLICATED:
        p[n] = flat2(w[n])

    loss_blk, grad_x, g = _local_step(x, mem, target, p, tm, tq, late={n: shards[n] for n in _LATE})

    parts = g["parts"]
    out = {}
    for n in _SHARDED:
        res = _adamw(flat2(w[n]), parts[n], flat2(mom[n]), flat2(var[n]), "adamw_" + n)
        out[n] = [r.reshape(w[n].shape) for r in res]

    shapes = [w[n].shape for n in _REPLICATED]
    rows = sum(_pack_rows(flat2(w[n]).shape[1]) for n in _REPLICATED) + 1
    rows = -(-rows // 8) * 8
    g_pack = _pack([g[n] for n in _REPLICATED] + [loss_blk[0:1, :]], rows)
    (g_all,) = _all_gather([g_pack], "gather_small")
    res = _adamw(_pack([w[n] for n in _REPLICATED], rows), g_all,
                 _pack([mom[n] for n in _REPLICATED], rows), _pack([var[n] for n in _REPLICATED], rows),
                 "adamw_small")
    n_rows_params = sum(_pack_rows(flat2(w[n]).shape[1]) for n in _REPLICATED)
    loss = res[0][n_rows_params, 0]
    unpacked = [_unpack(r, shapes) for r in res]
    for k, n in enumerate(_REPLICATED):
        out[n] = [unpacked[q][k] for q in range(4)]

    grads = [out[n][0] for n in _WEIGHTS]
    deltas = [out[n][1] for n in _WEIGHTS]
    new_m = [out[n][2] for n in _WEIGHTS]
    new_v = [out[n][3] for n in _WEIGHTS]
    return (loss, grad_x[None], *grads, *deltas, *new_m, *new_v)


def kernel(x, mem, attn_norm_g, w_in, b_forget, fox_out_g, sb_out_g, w_out, xattn_norm_g, mem_norm_g, w_mq, w_mkv, w_mo, ffn_norm_g, w_up, conv_w, conv_b, w_down, final_norm_g, loss_target, m_attn_norm_g, m_w_in, m_b_forget, m_fox_out_g, m_sb_out_g, m_w_out, m_xattn_norm_g, m_mem_norm_g, m_w_mq, m_w_mkv, m_w_mo, m_ffn_norm_g, m_w_up, m_conv_w, m_conv_b, m_w_down, m_final_norm_g, v_attn_norm_g, v_w_in, v_b_forget, v_fox_out_g, v_sb_out_g, v_w_out, v_xattn_norm_g, v_mem_norm_g, v_w_mq, v_w_mkv, v_w_mo, v_ffn_norm_g, v_w_up, v_conv_w, v_conv_b, v_w_down, v_final_norm_g):
    args = dict(locals())
    T = x.shape[1]
    return _step(args, tm=min(T, 512), tq=min(T, 256))
```

```python
import functools

import jax
import jax.numpy as jnp
from jax import lax
from jax.experimental import pallas as pl
from jax.experimental.pallas import tpu as pltpu

F32 = jnp.float32
BF16 = jnp.bfloat16
EPS = 1e-6
NEG = -1e30
LOG2E = 1.4426950408889634

HEAD_DIM = 64
N_FOX = 8
FOX_W = 512
QKV_W = 3072
N_MEM_HEADS = 4
MEM_HD = 256
D_FF = 2816
FF_CHUNK = 256
N_DEV = 8

ADAM_LR = 0.001
ADAM_B1 = 0.9
ADAM_B2 = 0.999
ADAM_EPS = 1e-08
ADAM_WD = 0.01
ADAM_STEP = 10

SB_SUM_TERMS = 1

VMEM_LIMIT = 56 * 1024 * 1024
MESH = pl.DeviceIdType.MESH


def _cparams(sem=None):
    return pltpu.CompilerParams(dimension_semantics=sem, vmem_limit_bytes=VMEM_LIMIT)


def _nt(a, b):
    return lax.dot_general(a, b, (((1,), (1,)), ((), ())), preferred_element_type=F32)


def _tn(a, b):
    return lax.dot_general(a, b, (((0,), (0,)), ((), ())), preferred_element_type=F32)


def _nn(a, b):
    return jnp.dot(a, b, preferred_element_type=F32)


def _split_dot(a, m01, terms):
    out = None
    r = a
    for t in range(terms):
        p = r.astype(BF16)
        d = _nn(p, m01)
        out = d if out is None else out + d
        if t + 1 < terms:
            r = r - p.astype(F32)
    return out


def _rstd(xv):
    return lax.rsqrt(jnp.mean(xv * xv, axis=-1, keepdims=True) + EPS)


def _norm_bwd(xv, g, dh):
    r = _rstd(xv)
    xhat = xv * r
    dxhat = dh * g
    dx = r * (dxhat - xhat * jnp.mean(dxhat * xhat, axis=-1, keepdims=True))
    dg = jnp.sum(dh * xhat, axis=0, keepdims=True)
    return dx, dg


def _tile_div(n, cap):
    best = None
    for d in range(128, min(n, cap) + 1, 128):
        if n % d == 0:
            best = d
    assert best is not None, n
    return best


def _inproj_fwd(x, g1, w_qkv, w_f_t, b_f, tm):
    T, D = x.shape
    N = w_qkv.shape[1]
    H = w_f_t.shape[0]

    def body(x_ref, g_ref, w_ref, wf_ref, b_ref, proj_ref, h_ref, xf_ref, c_ref, carry_ref):
        i = pl.program_id(0)

        @pl.when(i == 0)
        def _():
            carry_ref[...] = jnp.zeros_like(carry_ref)

        xv = x_ref[...]
        h = (xv * _rstd(xv) * g_ref[...]).astype(BF16)
        h_ref[...] = h
        for n0 in range(0, N, 512):
            proj_ref[:, n0:n0 + 512] = _nn(h, w_ref[:, n0:n0 + 512]).astype(BF16)
        xf = _nt(wf_ref[...], h) + b_ref[...]
        xf_ref[...] = xf
        logf = jnp.minimum(xf, 0.0) - jnp.log1p(jnp.exp(-jnp.abs(xf)))
        row = lax.broadcasted_iota(jnp.int32, (tm, tm), 0)
        col = lax.broadcasted_iota(jnp.int32, (tm, tm), 1)
        upper = jnp.where(row <= col, 1.0, 0.0).astype(BF16)
        c = _split_dot(logf, upper, 3) + carry_ref[...]
        c_ref[...] = c
        carry_ref[...] = c[:, tm - 1:tm]

    return pl.pallas_call(
        body,
        name="inproj_fwd",
        grid=(T // tm,),
        in_specs=[
            pl.BlockSpec((tm, D), lambda i: (i, 0)),
            pl.BlockSpec((1, D), lambda i: (0, 0)),
            pl.BlockSpec((D, N), lambda i: (0, 0)),
            pl.BlockSpec((H, D), lambda i: (0, 0)),
            pl.BlockSpec((H, 1), lambda i: (0, 0)),
        ],
        out_specs=[
            pl.BlockSpec((tm, N), lambda i: (i, 0)),
            pl.BlockSpec((tm, D), lambda i: (i, 0)),
            pl.BlockSpec((H, tm), lambda i: (0, i)),
            pl.BlockSpec((H, tm), lambda i: (0, i)),
        ],
        out_shape=[
            jax.ShapeDtypeStruct((T, N), BF16),
            jax.ShapeDtypeStruct((T, D), BF16),
            jax.ShapeDtypeStruct((H, T), F32),
            jax.ShapeDtypeStruct((H, T), F32),
        ],
        scratch_shapes=[pltpu.VMEM((H, 1), F32)],
        compiler_params=_cparams(("arbitrary",)),
    )(x, g1, w_qkv, w_f_t, b_f)


def _head_q(q, hh, lane):
    hmask = (lane >= HEAD_DIM * hh) & (lane < HEAD_DIM * (hh + 1))
    qh = jnp.where(hmask, q.astype(F32), 0.0) * (HEAD_DIM ** -0.5)
    return qh.astype(BF16), hmask


def _pipeline3(n, stage_a, stage_b, stage_c, diag_last, alive=None, a_first=False):
    stage_a(0, 0)
    if diag_last:
        @pl.when(n == 1)
        def _():
            stage_b(0, 0, True)

        @pl.when(n >= 2)
        def _():
            stage_b(0, 0, False)
            stage_a(1, 1)
    else:
        stage_a(jnp.minimum(1, n - 1), 1)
        stage_b(0, 0, True)

    def pair(m, carry):
        t = 2 + 2 * m
        if a_first:
            stage_a(t, 0)
            stage_b(t - 1, 1, False)
            stage_c(t - 2, 0)
            stage_a(t + 1, 1)
            stage_b(t, 0, False)
            stage_c(t - 1, 1)
        else:
            stage_c(t - 2, 0)
            stage_b(t - 1, 1, False)
            stage_a(t, 0)
            stage_c(t - 1, 1)
            stage_b(t, 0, False)
            stage_a(t + 1, 1)
        return carry

    pairs = (n - 2) // 2
    if alive is None:
        lax.fori_loop(0, pairs, pair, 0)
        go_on = True
        done = n
    else:
        def more(state):
            return (state[0] < pairs) & state[1]

        def step(state):
            pair(state[0], 0)
            return state[0] + 1, alive()

        m_end, go_on = lax.while_loop(more, step, (jnp.int32(0), jnp.bool_(True)))
        done = jnp.where(go_on, n, 2 * m_end)
    odd = n % 2 == 1

    @pl.when((n >= 3) & odd & go_on)
    def _():
        stage_a(n - 1, 0)
        stage_c(n - 3, 0)
        stage_b(n - 2, 1, False)
        stage_c(n - 2, 1)
        stage_b(n - 1, 0, diag_last)
        stage_c(n - 1, 0)

    @pl.when((n == 1) & go_on)
    def _():
        stage_c(0, 0)

    @pl.when(jnp.logical_not(odd) & go_on)
    def _():
        stage_c(n - 2, 0)
        stage_b(n - 1, 1, diag_last)
        stage_c(n - 1, 1)

    return done


def _lanes2(x):
    return jnp.concatenate([x, x], axis=1)


def _lanes_to_rows(vec, eye):
    return jnp.sum(jnp.where(eye, jnp.broadcast_to(vec, eye.shape), 0.0), axis=1, keepdims=True)


def _rows_to_lanes(rep, eye):
    return jnp.sum(jnp.where(eye, _lanes2(rep), 0.0), axis=0, keepdims=True)


FOX_DEAD = -110.0


def _fox_key_norms(k_ref, kn_s, lane):
    T = k_ref.shape[0]
    rows = min(T, 512)
    for hh in range(2):
        hmask = (lane >= HEAD_DIM * hh) & (lane < HEAD_DIM * (hh + 1))

        def chunk(n, best, hmask=hmask):
            kf = jnp.where(hmask, k_ref[pl.ds(pl.multiple_of(n * rows, rows), rows), :].astype(F32), 0.0)
            sq = jnp.sum(kf * kf, axis=1, keepdims=True)
            return jnp.maximum(best, jnp.max(sq, axis=0, keepdims=True))

        best = lax.fori_loop(0, T // rows, chunk, jnp.zeros((1, 1), F32))
        kn_s[hh] = jnp.broadcast_to(best, kn_s.shape[1:])


def _fox_live_blocks(i, qh_s, kn_s, cq_ref, cke_ref):
    nq = cke_ref.shape[-1]
    jj = lax.broadcasted_iota(jnp.int32, (1, nq), 1)
    first = None
    for hh in range(2):
        qf = qh_s[hh].astype(F32)
        qn = jnp.max(jnp.sum(qf * qf, axis=1, keepdims=True), axis=0, keepdims=True)
        zb = jnp.sqrt(qn * kn_s[hh][0:1, 0:1]) * 1.001
        bound = (2.0 * zb + cq_ref[hh][:, 0:1]) - cke_ref[hh]
        live = (bound >= FOX_DEAD) & (jj <= i)
        f = jnp.min(jnp.where(live, jj, i).astype(F32), axis=1, keepdims=True)
        first = f if first is None else jnp.minimum(first, f)
    return i + 1 - first[0, 0].astype(jnp.int32)


def _ride_along(exchange, at_start, at_middle, at_end):
    @pl.when(at_start)
    def _():
        exchange.start()

    if at_middle is not None:
        @pl.when(at_middle)
        def _():
            exchange.forward()

    def finish():
        @pl.when(at_end)
        def _():
            exchange.finish()

    return finish


def _fox_fwd(proj, c_col, c_row, c_ends, tq, gather=()):
    T = proj.shape[0]
    assert tq == 256
    nq = T // tq
    ng = len(gather)

    def body(*refs):
        q_ref, k_ref, v_ref, cq_ref, ck_ref, cke_ref = refs[:6]
        o_ref, lse_ref = refs[6 + ng:8 + ng]
        qh_s, cq_s, z_s, p_s, al_s, m_s, acc_s, kn_s = refs[8 + 2 * ng:16 + 2 * ng]
        i = pl.program_id(1)
        if ng:
            pair = pl.program_id(0)
            finish = _ride_along(_Gather(refs[6:6 + ng], refs[8 + ng:8 + 2 * ng], *refs[16 + 2 * ng:]),
                                 (pair == 0) & (i == 0), (pair == 1) & (i == 0), (pair == 3) & (i == nq - 1))
        lane = lax.broadcasted_iota(jnp.int32, (1, 128), 1)
        row = lax.broadcasted_iota(jnp.int32, (tq, tq), 0)
        col = lax.broadcasted_iota(jnp.int32, (tq, tq), 1)

        @pl.when(i == 0)
        def _():
            _fox_key_norms(k_ref, kn_s, lane)

        q = q_ref[...]
        for hh in range(2):
            qh_s[hh] = _head_q(q, hh, lane)[0]
            cq_s[hh] = jnp.broadcast_to(_lanes_to_rows(cq_ref[hh], row == col), (tq, tq))
        m_s[...] = jnp.full(m_s.shape, NEG, F32)
        acc_s[...] = jnp.zeros_like(acc_s)

        def rows(t):
            return pl.ds(pl.multiple_of((i - t) * tq, tq), tq)

        def stage_a(t, slot):
            k = k_ref[rows(t), :]
            for hh in range(2):
                z_s[slot, hh] = _nt(qh_s[hh], k)

        def stage_b(t, slot, diag):
            for hh in range(2):
                s = z_s[slot, hh] + cq_s[hh] - ck_ref[hh, :, rows(t)]
                if diag:
                    s = jnp.where(col <= row, s, NEG)
                m = m_s[hh]
                half = jnp.maximum(s[:, :128], s[:, 128:])
                m_new = jnp.maximum(m, jnp.max(half, axis=1, keepdims=True))
                m_s[hh] = m_new
                al_s[slot, hh] = jnp.exp(m - m_new)
                p_s[slot, hh] = jnp.exp(s - _lanes2(m_new)).astype(BF16)

        def stage_c(t, slot):
            v = v_ref[rows(t), :]
            for hh in range(2):
                own = (lane >= HEAD_DIM * hh) & (lane < HEAD_DIM * (hh + 1))
                acc_s[hh] = (al_s[slot, hh] * acc_s[hh]
                             + _nn(p_s[slot, hh], jnp.where(own, v, 1.0).astype(BF16)))

        _pipeline3(_fox_live_blocks(i, qh_s, kn_s, cq_ref, cke_ref), stage_a, stage_b, stage_c, False)
        halves = []
        for hh in range(2):
            acc = acc_s[hh]
            own = (lane >= HEAD_DIM * hh) & (lane < HEAD_DIM * (hh + 1))
            halves.append(jnp.where(own, pltpu.roll(acc, HEAD_DIM, axis=1), acc))
        l0, l1 = halves
        o_ref[...] = jnp.where(lane < HEAD_DIM, acc_s[0] / l0, acc_s[1] / l1)
        lse_ref[0] = _rows_to_lanes(m_s[0] + jnp.log(l0), row == col)
        lse_ref[1] = _rows_to_lanes(m_s[1] + jnp.log(l1), row == col)
        if ng:
            finish()

    res = pl.pallas_call(
        body,
        name="fox_fwd",
        grid=(4, nq),
        in_specs=[
            pl.BlockSpec((tq, 128), lambda p, i: (i, p)),
            pl.BlockSpec((T, 128), lambda p, i: (0, 4 + p)),
            pl.BlockSpec((T, 128), lambda p, i: (0, 8 + p)),
            pl.BlockSpec((2, 1, tq), lambda p, i: (p, 0, i)),
            pl.BlockSpec((2, 1, T), lambda p, i: (p, 0, 0)),
            pl.BlockSpec((2, 1, nq), lambda p, i: (p, 0, 0)),
        ] + [_ANY] * ng,
        out_specs=[
            pl.BlockSpec((tq, 128), lambda p, i: (i, p)),
            pl.BlockSpec((2, 1, tq), lambda p, i: (p, 0, i)),
        ] + [_ANY] * ng,
        out_shape=[
            jax.ShapeDtypeStruct((T, FOX_W), F32),
            jax.ShapeDtypeStruct((N_FOX, 1, T), F32),
        ] + _gathered_shapes(gather),
        scratch_shapes=[
            pltpu.VMEM((2, tq, 128), BF16),
            pltpu.VMEM((2, tq, tq), F32),
            pltpu.VMEM((2, 2, tq, tq), F32),
            pltpu.VMEM((2, 2, tq, tq), BF16),
            pltpu.VMEM((2, 2, tq, 128), F32),
            pltpu.VMEM((2, tq, 128), F32),
            pltpu.VMEM((2, tq, 128), F32),
            pltpu.VMEM((2, 8, 128), F32),
        ] + (_comm_sems(ng) if ng else []),
        compiler_params=_cparams(("arbitrary", "arbitrary")),
    )(proj, proj, proj, c_col, c_row, c_ends, *gather)
    res = list(res)
    return res[0], res[1], res[2:]


def _sb_logs(zn, strict):
    e = jnp.exp2(jnp.abs(zn) * (-LOG2E))
    L = jnp.minimum(zn, 0.0) - jnp.log(1.0 + e)
    G = L - zn
    if strict is not None:
        L = jnp.where(strict, L, 0.0)
    return L, G


SB_DEAD = -110.0


def _sb_fwd(proj, tq):
    T = proj.shape[0]
    nq = T // tq

    def body(q_ref, k_ref, v_ref, o_ref, ltot_ref, live_ref, qh_s, z_s, g_s, tot_s, run_s, acc_s):
        i = pl.program_id(1)
        lane = lax.broadcasted_iota(jnp.int32, (1, 128), 1)
        row = lax.broadcasted_iota(jnp.int32, (tq, tq), 0)
        col = lax.broadcasted_iota(jnp.int32, (tq, tq), 1)
        strict = col < row
        later = jnp.where(row > col, 1.0, 0.0).astype(BF16)
        q = q_ref[...]
        for hh in range(2):
            qh_s[hh] = -_head_q(q, hh, lane)[0]
        run_s[...] = jnp.zeros_like(run_s)
        acc_s[...] = jnp.zeros_like(acc_s)

        def rows(t):
            return pl.ds(pl.multiple_of((i - t) * tq, tq), tq)

        def stage_a(t, slot):
            k = k_ref[rows(t), :]
            for hh in range(2):
                z_s[slot, hh] = _nt(qh_s[hh], k)

        def stage_b(t, slot, diag):
            for hh in range(2):
                L, g = _sb_logs(z_s[slot, hh], strict if diag else None)
                if diag:
                    g = jnp.where(strict, g, NEG)
                after = _split_dot(L, later, SB_SUM_TERMS)
                g_s[slot, hh] = g + after
                first = L[:, 0:1]
                if SB_SUM_TERMS == 1:
                    first = first.astype(BF16).astype(F32)
                tot_s[slot, hh] = jnp.broadcast_to(after[:, 0:1] + first, (tq, 128))

        def stage_c(t, slot):
            v = v_ref[rows(t), :]
            for hh in range(2):
                run = run_s[hh]
                a = jnp.exp(g_s[slot, hh] + _lanes2(run))
                acc_s[hh] += _nn(a.astype(BF16), v)
                run_s[hh] = run + tot_s[slot, hh]

        def alive():
            return jnp.max(jnp.maximum(run_s[0], run_s[1])) > SB_DEAD

        done = _pipeline3(i + 1, stage_a, stage_b, stage_c, False, alive)
        ltot_ref[0] = _rows_to_lanes(run_s[0], row == col)
        ltot_ref[1] = _rows_to_lanes(run_s[1], row == col)
        o_ref[...] = jnp.where(lane < HEAD_DIM, acc_s[0], acc_s[1])
        at = lax.broadcasted_iota(jnp.int32, (1, nq), 1)

        @pl.when(i == 0)
        def _():
            live_ref[0] = jnp.zeros((1, nq), F32)

        live_ref[0] = jnp.where(at == i, done.astype(F32), live_ref[0])

    return pl.pallas_call(
        body,
        name="sb_fwd",
        grid=(4, nq),
        in_specs=[
            pl.BlockSpec((tq, 128), lambda p, i: (i, 12 + p)),
            pl.BlockSpec((T, 128), lambda p, i: (0, 16 + p)),
            pl.BlockSpec((T, 128), lambda p, i: (0, 20 + p)),
        ],
        out_specs=[
            pl.BlockSpec((tq, 128), lambda p, i: (i, p)),
            pl.BlockSpec((2, 1, tq), lambda p, i: (p, 0, i)),
            pl.BlockSpec((1, 1, nq), lambda p, i: (p, 0, 0)),
        ],
        out_shape=[
            jax.ShapeDtypeStruct((T, FOX_W), F32),
            jax.ShapeDtypeStruct((N_FOX, 1, T), F32),
            jax.ShapeDtypeStruct((N_FOX // 2, 1, nq), F32),
        ],
        scratch_shapes=[
            pltpu.VMEM((2, tq, 128), BF16),
            pltpu.VMEM((2, 2, tq, tq), F32),
            pltpu.VMEM((2, 2, tq, tq), F32),
            pltpu.VMEM((2, 2, tq, 128), F32),
            pltpu.VMEM((2, tq, 128), F32),
            pltpu.VMEM((2, tq, 128), F32),
        ],
        compiler_params=_cparams(("arbitrary", "arbitrary")),
    )(proj, proj, proj)


def _post_attn_fwd(fox_o, sb_o, gf, gs, w_out, x, tm):
    T, D = x.shape

    def body(f_ref, s_ref, gf_ref, gs_ref, w_ref, x_ref, x1_ref, mix_ref):
        f = f_ref[...]
        s = s_ref[...]
        mix_ref[:, :FOX_W] = (f * _rstd(f) * gf_ref[...]).astype(BF16)
        mix_ref[:, FOX_W:] = (s * _rstd(s) * gs_ref[...]).astype(BF16)
        x1_ref[...] = x_ref[...] + _nn(mix_ref[...], w_ref[...])

    return pl.pallas_call(
        body,
        name="post_attn_fwd",
        grid=(T // tm,),
        in_specs=[
            pl.BlockSpec((tm, FOX_W), lambda i: (i, 0)),
            pl.BlockSpec((tm, FOX_W), lambda i: (i, 0)),
            pl.BlockSpec((1, FOX_W), lambda i: (0, 0)),
            pl.BlockSpec((1, FOX_W), lambda i: (0, 0)),
            pl.BlockSpec((D, D), lambda i: (0, 0)),
            pl.BlockSpec((tm, D), lambda i: (i, 0)),
        ],
        out_specs=[
            pl.BlockSpec((tm, D), lambda i: (i, 0)),
            pl.BlockSpec((tm, D), lambda i: (i, 0)),
        ],
        out_shape=[jax.ShapeDtypeStruct((T, D), F32), jax.ShapeDtypeStruct((T, D), BF16)],
        compiler_params=_cparams(("arbitrary",)),
    )(fox_o, sb_o, gf, gs, w_out, x)


def _mem_kv_fwd(mem, gm, w_mkv):
    M, D = mem.shape
    N = w_mkv.shape[1]

    def body(mem_ref, g_ref, w_ref, m_ref, kv_ref):
        mv = mem_ref[...]
        m = (mv * _rstd(mv) * g_ref[...]).astype(BF16)
        m_ref[...] = m
        for n0 in range(0, N, 512):
            kv_ref[:, n0:n0 + 512] = _nn(m, w_ref[:, n0:n0 + 512]).astype(BF16)

    return pl.pallas_call(
        body,
        name="mem_kv_fwd",
        out_shape=[jax.ShapeDtypeStruct((M, D), BF16), jax.ShapeDtypeStruct((M, N), BF16)],
        compiler_params=_cparams(),
    )(mem, gm, w_mkv)


def _xattn_probs(qb, kv, h):
    k = kv[:, h * MEM_HD:(h + 1) * MEM_HD]
    s = _nt(qb[:, h * MEM_HD:(h + 1) * MEM_HD], k) * (MEM_HD ** -0.5)
    s = s - jnp.max(s, axis=1, keepdims=True)
    p = jnp.exp(s)
    return p / jnp.sum(p, axis=1, keepdims=True)


def _xattn_fwd(x1, g2, w_mq, kv, w_mo, tm):
    T, D = x1.shape
    M = kv.shape[0]

    def body(x_ref, g_ref, wq_ref, kv_ref, wo_ref, x2_ref, h_ref, q_ref, om_ref):
        xv = x_ref[...]
        h = (xv * _rstd(xv) * g_ref[...]).astype(BF16)
        h_ref[...] = h
        q_ref[...] = _nn(h, wq_ref[...]).astype(BF16)
        qb = q_ref[...]
        kvv = kv_ref[...]
        for hd in range(N_MEM_HEADS):
            p = _xattn_probs(qb, kvv, hd)
            v = kvv[:, D + hd * MEM_HD:D + (hd + 1) * MEM_HD]
            om_ref[:, hd * MEM_HD:(hd + 1) * MEM_HD] = _nn(p.astype(BF16), v).astype(BF16)
        x2_ref[...] = xv + _nn(om_ref[...], wo_ref[...])

    return pl.pallas_call(
        body,
        name="xattn_fwd",
        grid=(T // tm,),
        in_specs=[
            pl.BlockSpec((tm, D), lambda i: (i, 0)),
            pl.BlockSpec((1, D), lambda i: (0, 0)),
            pl.BlockSpec((D, D), lambda i: (0, 0)),
            pl.BlockSpec((M, 2 * D), lambda i: (0, 0)),
            pl.BlockSpec((D, D), lambda i: (0, 0)),
        ],
        out_specs=[pl.BlockSpec((tm, D), lambda i: (i, 0))] * 4,
        out_shape=[jax.ShapeDtypeStruct((T, D), F32)] + [jax.ShapeDtypeStruct((T, D), BF16)] * 3,
        compiler_params=_cparams(("arbitrary",)),
    )(x1, g2, w_mq, kv, w_mo)


def _conv_taps(ext_ref, tm, back):
    if back:
        return ext_ref[pl.ds(6, tm), :], ext_ref[pl.ds(7, tm), :], ext_ref[pl.ds(8, tm), :]
    return ext_ref[pl.ds(0, tm), :], ext_ref[pl.ds(1, tm), :], ext_ref[pl.ds(2, tm), :]


def _ffn_fwd(x2, g3, w_up, conv_w, conv_b, w_down, tm):
    T, D = x2.shape
    fc = FF_CHUNK
    nj = D_FF // fc

    def body(x_ref, g_ref, wg_ref, wv_ref, cwg_ref, cwv_ref, cbg_ref, cbv_ref, wd_ref,
             x3_ref, h_ref, ug_ref, uv_ref, yg_ref, yv_ref, a_ref, acc_ref, carry_ref, ext_ref):
        i = pl.program_id(0)
        j = pl.program_id(1)

        @pl.when(j == 0)
        def _():
            xv = x_ref[...]
            h_ref[...] = (xv * _rstd(xv) * g_ref[...]).astype(BF16)
            acc_ref[...] = xv

        @pl.when(i == 0)
        def _():
            carry_ref[j] = jnp.zeros((2, 8, fc), F32)

        h = h_ref[...]
        halves = []
        for part, (w_ref, cw_ref, cb_ref, u_ref, y_ref) in enumerate(
                ((wg_ref, cwg_ref, cbg_ref, ug_ref, yg_ref), (wv_ref, cwv_ref, cbv_ref, uv_ref, yv_ref))):
            u = _nn(h, w_ref[...])
            u_ref[...] = u.astype(BF16)
            ext = ext_ref.at[part]
            ext[pl.ds(0, 8), :] = carry_ref[j, part]
            ext[pl.ds(8, tm), :] = u
            carry_ref[j, part] = u[tm - 8:, :]
            u2, u1, u0 = _conv_taps(ext, tm, True)
            cw = cw_ref[...]
            y = cb_ref[...] + cw[0:1] * u2 + cw[1:2] * u1 + cw[2:3] * u0
            y_ref[...] = y.astype(BF16)
            halves.append(y)
        gate, val = halves
        a = (gate * jax.nn.sigmoid(gate) * val).astype(BF16)
        a_ref[...] = a
        acc_ref[...] += _nn(a, wd_ref[...])

        @pl.when(j == nj - 1)
        def _():
            x3_ref[...] = acc_ref[...]

    return pl.pallas_call(
        body,
        name="ffn_fwd",
        grid=(T // tm, nj),
        in_specs=[
            pl.BlockSpec((tm, D), lambda i, j: (i, 0)),
            pl.BlockSpec((1, D), lambda i, j: (0, 0)),
            pl.BlockSpec((D, fc), lambda i, j: (0, j)),
            pl.BlockSpec((D, fc), lambda i, j: (0, nj + j)),
            pl.BlockSpec((3, fc), lambda i, j: (0, j)),
            pl.BlockSpec((3, fc), lambda i, j: (0, nj + j)),
            pl.BlockSpec((1, fc), lambda i, j: (0, j)),
            pl.BlockSpec((1, fc), lambda i, j: (0, nj + j)),
            pl.BlockSpec((fc, D), lambda i, j: (j, 0)),
        ],
        out_specs=[
            pl.BlockSpec((tm, D), lambda i, j: (i, 0)),
            pl.BlockSpec((tm, D), lambda i, j: (i, 0)),
        ] + [pl.BlockSpec((tm, fc), lambda i, j: (i, j))] * 5,
        out_shape=[
            jax.ShapeDtypeStruct((T, D), F32),
            jax.ShapeDtypeStruct((T, D), BF16),
        ] + [jax.ShapeDtypeStruct((T, D_FF), BF16)] * 5,
        scratch_shapes=[
            pltpu.VMEM((tm, D), F32),
            pltpu.VMEM((nj, 2, 8, fc), F32),
            pltpu.VMEM((2, tm + 8, fc), F32),
        ],
        compiler_params=_cparams(("arbitrary", "arbitrary")),
    )(x2, g3, w_up, w_up, conv_w, conv_w, conv_b, conv_b, w_down)


def _loss_head(x3, gfin, target, tm):
    T, D = x3.shape

    def body(x_ref, g_ref, t_ref, dx_ref, loss_ref, dg_ref):
        i = pl.program_id(0)

        @pl.when(i == 0)
        def _():
            loss_ref[...] = jnp.zeros_like(loss_ref)
            dg_ref[...] = jnp.zeros_like(dg_ref)

        xv = x_ref[...]
        g = g_ref[...]
        r = _rstd(xv)
        xhat = xv * r
        err = xhat * g - t_ref[...]
        part = jnp.sum(jnp.sum(err * err, axis=1, keepdims=True), axis=0, keepdims=True) * (0.5 / D)
        loss_ref[...] += jnp.broadcast_to(part, loss_ref.shape)
        dy = err * (1.0 / D)
        dg_ref[...] += jnp.sum(dy * xhat, axis=0, keepdims=True)
        dxhat = dy * g
        dx_ref[...] = r * (dxhat - xhat * jnp.mean(dxhat * xhat, axis=-1, keepdims=True))

    return pl.pallas_call(
        body,
        name="loss_head",
        grid=(T // tm,),
        in_specs=[
            pl.BlockSpec((tm, D), lambda i: (i, 0)),
            pl.BlockSpec((1, D), lambda i: (0, 0)),
            pl.BlockSpec((tm, D), lambda i: (i, 0)),
        ],
        out_specs=[
            pl.BlockSpec((tm, D), lambda i: (i, 0)),
            pl.BlockSpec((8, 128), lambda i: (0, 0)),
            pl.BlockSpec((1, D), lambda i: (0, 0)),
        ],
        out_shape=[
            jax.ShapeDtypeStruct((T, D), F32),
            jax.ShapeDtypeStruct((8, 128), F32),
            jax.ShapeDtypeStruct((1, D), F32),
        ],
        compiler_params=_cparams(("arbitrary",)),
    )(x3, gfin, target)


def _ffn_bwd(dx3, x2, g3, ug, uv, yg, yv, conv_w, w_down, w_up, tm):
    T, D = x2.shape
    fc = FF_CHUNK
    nj = D_FF // fc
    nt = T // tm

    def rev(i):
        return nt - 1 - i

    def body(dx3_ref, x_ref, g_ref, ug_ref, uv_ref, yg_ref, yv_ref, cwg_ref, cwv_ref,
             wd_ref, wug_ref, wuv_ref,
             dx2_ref, dug_ref, duv_ref, dg_ref, dcg_ref, dcv_ref,
             acc_ref, carry_ref, ext_ref):
        i = pl.program_id(0)
        j = pl.program_id(1)
        cols = pl.ds(pl.multiple_of(j * fc, fc), fc)

        @pl.when(j == 0)
        def _():
            acc_ref[...] = jnp.zeros_like(acc_ref)

        @pl.when((i == 0) & (j == 0))
        def _():
            dg_ref[...] = jnp.zeros_like(dg_ref)
            dcg_ref[...] = jnp.zeros_like(dcg_ref)
            dcv_ref[...] = jnp.zeros_like(dcv_ref)

        @pl.when(i == 0)
        def _():
            carry_ref[j] = jnp.zeros((2, 8, fc), F32)

        da = _nt(dx3_ref[...].astype(BF16), wd_ref[...])
        gate = yg_ref[...].astype(F32)
        val = yv_ref[...].astype(F32)
        sig = jax.nn.sigmoid(gate)
        silu = gate * sig
        dys = (da * val * (sig * (1.0 + gate * (1.0 - sig))), da * silu)
        for part, (dy, u_ref, cw_ref, du_ref, wu_ref, dc_ref) in enumerate(
                ((dys[0], ug_ref, cwg_ref, dug_ref, wug_ref, dcg_ref),
                 (dys[1], uv_ref, cwv_ref, duv_ref, wuv_ref, dcv_ref))):
            ext = ext_ref.at[part]
            ext[pl.ds(0, tm), :] = dy
            ext[pl.ds(tm, 8), :] = carry_ref[j, part]
            carry_ref[j, part] = dy[:8, :]
            d0, d1, d2 = _conv_taps(ext, tm, False)
            u = u_ref[...].astype(F32)
            upd = jnp.concatenate([
                jnp.sum(u * d2, axis=0, keepdims=True),
                jnp.sum(u * d1, axis=0, keepdims=True),
                jnp.sum(u * d0, axis=0, keepdims=True),
                jnp.sum(d0, axis=0, keepdims=True),
                jnp.zeros((4, fc), F32)], axis=0)
            dc_ref[:, cols] += upd
            cw = cw_ref[...]
            du = (cw[2:3] * d0 + cw[1:2] * d1 + cw[0:1] * d2).astype(BF16)
            du_ref[...] = du
            acc_ref[...] += _nt(du, wu_ref[...])

        @pl.when(j == nj - 1)
        def _():
            dx, dg = _norm_bwd(x_ref[...], g_ref[...], acc_ref[...])
            dx2_ref[...] = dx3_ref[...] + dx
            dg_ref[...] += dg

    return pl.pallas_call(
        body,
        name="ffn_bwd",
        grid=(nt, nj),
        in_specs=[
            pl.BlockSpec((tm, D), lambda i, j: (rev(i), 0)),
            pl.BlockSpec((tm, D), lambda i, j: (rev(i), 0)),
            pl.BlockSpec((1, D), lambda i, j: (0, 0)),
            pl.BlockSpec((tm, fc), lambda i, j: (rev(i), j)),
            pl.BlockSpec((tm, fc), lambda i, j: (rev(i), j)),
            pl.BlockSpec((tm, fc), lambda i, j: (rev(i), j)),
            pl.BlockSpec((tm, fc), lambda i, j: (rev(i), j)),
            pl.BlockSpec((3, fc), lambda i, j: (0, j)),
            pl.BlockSpec((3, fc), lambda i, j: (0, nj + j)),
            pl.BlockSpec((fc, D), lambda i, j: (j, 0)),
            pl.BlockSpec((D, fc), lambda i, j: (0, j)),
            pl.BlockSpec((D, fc), lambda i, j: (0, nj + j)),
        ],
        out_specs=[
            pl.BlockSpec((tm, D), lambda i, j: (rev(i), 0)),
            pl.BlockSpec((tm, fc), lambda i, j: (rev(i), j)),
            pl.BlockSpec((tm, fc), lambda i, j: (rev(i), j)),
            pl.BlockSpec((1, D), lambda i, j: (0, 0)),
            pl.BlockSpec((8, D_FF), lambda i, j: (0, 0)),
            pl.BlockSpec((8, D_FF), lambda i, j: (0, 0)),
        ],
        out_shape=[
            jax.ShapeDtypeStruct((T, D), F32),
            jax.ShapeDtypeStruct((T, D_FF), BF16),
            jax.ShapeDtypeStruct((T, D_FF), BF16),
            jax.ShapeDtypeStruct((1, D), F32),
            jax.ShapeDtypeStruct((8, D_FF), F32),
            jax.ShapeDtypeStruct((8, D_FF), F32),
        ],
        scratch_shapes=[
            pltpu.VMEM((tm, D), F32),
            pltpu.VMEM((nj, 2, 8, fc), F32),
            pltpu.VMEM((2, tm + 8, fc), F32),
        ],
        compiler_params=_cparams(("arbitrary", "arbitrary")),
    )(dx3, x2, g3, ug, uv, yg, yv, conv_w, conv_w, w_down, w_up, w_up)


def _xattn_bwd(dx2, x1, g2, qb, kv, w_mo, w_mq, tm):
    T, D = x1.shape
    M = kv.shape[0]

    def body(dx2_ref, x_ref, g_ref, q_ref, kv_ref, wo_ref, wq_ref, dx1_ref, dq_ref, dkv_ref, dg_ref):
        i = pl.program_id(0)

        @pl.when(i == 0)
        def _():
            dkv_ref[...] = jnp.zeros_like(dkv_ref)
            dg_ref[...] = jnp.zeros_like(dg_ref)

        dxv = dx2_ref[...]
        dom = _nt(dxv.astype(BF16), wo_ref[...]).astype(BF16)
        qb_ = q_ref[...]
        kvv = kv_ref[...]
        for hd in range(N_MEM_HEADS):
            sl = slice(hd * MEM_HD, (hd + 1) * MEM_HD)
            vsl = slice(D + hd * MEM_HD, D + (hd + 1) * MEM_HD)
            p = _xattn_probs(qb_, kvv, hd)
            dp = _nt(dom[:, sl], kvv[:, vsl])
            ds = (p * (dp - jnp.sum(p * dp, axis=1, keepdims=True)) * (MEM_HD ** -0.5)).astype(BF16)
            dq_ref[:, sl] = _nn(ds, kvv[:, sl]).astype(BF16)
            dkv_ref[:, sl] += _tn(ds, qb_[:, sl])
            dkv_ref[:, vsl] += _tn(p.astype(BF16), dom[:, sl])
        dh = _nt(dq_ref[...], wq_ref[...])
        dx, dg = _norm_bwd(x_ref[...], g_ref[...], dh)
        dx1_ref[...] = dxv + dx
        dg_ref[...] += dg

    return pl.pallas_call(
        body,
        name="xattn_bwd",
        grid=(T // tm,),
        in_specs=[
            pl.BlockSpec((tm, D), lambda i: (i, 0)),
            pl.BlockSpec((tm, D), lambda i: (i, 0)),
            pl.BlockSpec((1, D), lambda i: (0, 0)),
            pl.BlockSpec((tm, D), lambda i: (i, 0)),
            pl.BlockSpec((M, 2 * D), lambda i: (0, 0)),
            pl.BlockSpec((D, D), lambda i: (0, 0)),
            pl.BlockSpec((D, D), lambda i: (0, 0)),
        ],
        out_specs=[
            pl.BlockSpec((tm, D), lambda i: (i, 0)),
            pl.BlockSpec((tm, D), lambda i: (i, 0)),
            pl.BlockSpec((M, 2 * D), lambda i: (0, 0)),
            pl.BlockSpec((1, D), lambda i: (0, 0)),
        ],
        out_shape=[
            jax.ShapeDtypeStruct((T, D), F32),
            jax.ShapeDtypeStruct((T, D), BF16),
            jax.ShapeDtypeStruct((M, 2 * D), F32),
            jax.ShapeDtypeStruct((1, D), F32),
        ],
        compiler_params=_cparams(("arbitrary",)),
    )(dx2, x1, g2, qb, kv, w_mo, w_mq)


def _mem_kv_bwd(mem, gm, mb, dkv, w_mkv):
    M, D = mem.shape
    N = dkv.shape[1]

    def body(mem_ref, g_ref, m_ref, dkv_ref, w_ref, dw_ref, dg_ref):
        dkvb = dkv_ref[...].astype(BF16)
        for n0 in range(0, N, 512):
            dw_ref[:, n0:n0 + 512] = _tn(m_ref[...], dkvb[:, n0:n0 + 512]).astype(BF16)
        dm = _nt(dkvb, w_ref[...])
        mv = mem_ref[...]
        dg_ref[...] = jnp.sum(dm * (mv * _rstd(mv)), axis=0, keepdims=True)

    return pl.pallas_call(
        body,
        name="mem_kv_bwd",
        out_shape=[jax.ShapeDtypeStruct((D, N), BF16), jax.ShapeDtypeStruct((1, D), F32)],
        compiler_params=_cparams(),
    )(mem, gm, mb, dkv, w_mkv)


def _post_attn_bwd(dx1, fox_o, sb_o, gf, gs, w_out, tm):
    T, D = dx1.shape

    def body(dx_ref, f_ref, s_ref, gf_ref, gs_ref, w_ref, df_ref, ds_ref, dgf_ref, dgs_ref):
        i = pl.program_id(0)

        @pl.when(i == 0)
        def _():
            dgf_ref[...] = jnp.zeros_like(dgf_ref)
            dgs_ref[...] = jnp.zeros_like(dgs_ref)

        dmix = _nt(dx_ref[...].astype(BF16), w_ref[...])
        d, dg = _norm_bwd(f_ref[...], gf_ref[...], dmix[:, :FOX_W])
        df_ref[...] = d
        dgf_ref[...] += dg
        d, dg = _norm_bwd(s_ref[...], gs_ref[...], dmix[:, FOX_W:])
        ds_ref[...] = d
        dgs_ref[...] += dg

    return pl.pallas_call(
        body,
        name="post_attn_bwd",
        grid=(T // tm,),
        in_specs=[
            pl.BlockSpec((tm, D), lambda i: (i, 0)),
            pl.BlockSpec((tm, FOX_W), lambda i: (i, 0)),
            pl.BlockSpec((tm, FOX_W), lambda i: (i, 0)),
            pl.BlockSpec((1, FOX_W), lambda i: (0, 0)),
            pl.BlockSpec((1, FOX_W), lambda i: (0, 0)),
            pl.BlockSpec((D, D), lambda i: (0, 0)),
        ],
        out_specs=[
            pl.BlockSpec((tm, FOX_W), lambda i: (i, 0)),
            pl.BlockSpec((tm, FOX_W), lambda i: (i, 0)),
            pl.BlockSpec((1, FOX_W), lambda i: (0, 0)),
            pl.BlockSpec((1, FOX_W), lambda i: (0, 0)),
        ],
        out_shape=[
            jax.ShapeDtypeStruct((T, FOX_W), F32),
            jax.ShapeDtypeStruct((T, FOX_W), F32),
            jax.ShapeDtypeStruct((1, FOX_W), F32),
            jax.ShapeDtypeStruct((1, FOX_W), F32),
        ],
        compiler_params=_cparams(("arbitrary",)),
    )(dx1, fox_o, sb_o, gf, gs, w_out)


def _sb_bwd(proj, ltot, live, d_o, tq):
    T = proj.shape[0]
    nq = T // tq

    def body(q_ref, k_ref, v_ref, lt_ref, live_ref, do_ref, dq_ref, dk_ref, dv_ref,
             qh_s, doh_s, lt_s, z_s, da_s, ab_s, dzb_s, run_s, runw_s, dq_s, qt_s, dot_s, dkt_s, dvt_s):
        i = pl.program_id(1)

        @pl.when(i == 0)
        def _():
            dkt_s[...] = jnp.zeros_like(dkt_s)
            dvt_s[...] = jnp.zeros_like(dvt_s)

        lane = lax.broadcasted_iota(jnp.int32, (1, 128), 1)
        row = lax.broadcasted_iota(jnp.int32, (tq, tq), 0)
        col = lax.broadcasted_iota(jnp.int32, (tq, tq), 1)
        strict = col < row
        upto = jnp.where(row <= col, 1.0, 0.0).astype(BF16)
        before = jnp.where(row < col, 1.0, 0.0).astype(BF16)
        q = q_ref[...]
        dov = do_ref[...]
        for hh in range(2):
            qh, hmask = _head_q(q, hh, lane)
            qh_s[hh] = -qh
            doh_s[hh] = jnp.where(hmask, dov, 0.0).astype(BF16)
            lt_s[hh] = jnp.broadcast_to(_lanes_to_rows(lt_ref[hh], row == col), (tq, 128))
        qt_s[...] = (q.astype(F32) * -(HEAD_DIM ** -0.5)).T.astype(BF16)
        dot_s[...] = dov.astype(F32).T.astype(BF16)
        run_s[...] = jnp.zeros_like(run_s)
        runw_s[...] = jnp.zeros_like(runw_s)
        dq_s[...] = jnp.zeros_like(dq_s)

        at = lax.broadcasted_iota(jnp.int32, (1, nq), 1)
        count = jnp.sum(jnp.where(at == i, live_ref[0], 0.0), axis=1, keepdims=True)[0, 0].astype(jnp.int32)
        n_live = jnp.clip(count, 1, i + 1)
        oldest = i + 1 - n_live

        def rows(t):
            return pl.ds(pl.multiple_of((oldest + t) * tq, tq), tq)

        def stage_a(t, slot):
            k = k_ref[rows(t), :]
            v = v_ref[rows(t), :]
            for hh in range(2):
                z_s[slot, hh] = _nt(qh_s[hh], k)
                da_s[slot, hh] = _nt(doh_s[hh], v)

        def stage_b(t, slot, diag):
            for hh in range(2):
                L, g = _sb_logs(z_s[slot, hh], strict if diag else None)
                upto_s = _split_dot(L, upto, SB_SUM_TERMS)
                run = run_s[hh]
                arg = (g + _lanes2(lt_s[hh] - run)) - upto_s
                if diag:
                    arg = jnp.where(strict, arg, NEG)
                a = jnp.exp(arg)
                w = a * da_s[slot, hh]
                w_before = _split_dot(w, before, SB_SUM_TERMS)
                run_w = runw_s[hh]
                d_keep = w_before + _lanes2(run_w)
                beta = jnp.exp(g)
                ndz = beta * (w + d_keep) - w
                if diag:
                    ndz = jnp.where(strict, ndz, 0.0)
                dzb_s[slot, hh] = ndz.astype(BF16)
                ab_s[slot, hh] = a.astype(BF16)
                run_s[hh] = run + jnp.broadcast_to(upto_s[:, tq - 1:tq], (tq, 128))
                runw_s[hh] = run_w + jnp.broadcast_to(w_before[:, tq - 1:tq] + w[:, tq - 1:tq], (tq, 128))

        def stage_c(t, slot):
            k = k_ref[rows(t), :]
            for hh in range(2):
                dzb = dzb_s[slot, hh]
                dq_s[hh] += _nn(dzb, k)
                dims = pl.ds(HEAD_DIM * hh, HEAD_DIM)
                dkt_s[oldest + t, dims, :] += _nn(qt_s[dims, :], dzb)
                dvt_s[oldest + t, dims, :] += _nn(dot_s[dims, :], ab_s[slot, hh])

        _pipeline3(n_live, stage_a, stage_b, stage_c, True)
        dq_ref[...] = (jnp.where(lane < HEAD_DIM, dq_s[0], dq_s[1]) * -(HEAD_DIM ** -0.5)).astype(BF16)

        @pl.when(i == nq - 1)
        def _():
            def flush(n, carry):
                keys = pl.ds(pl.multiple_of(n * tq, tq), tq)
                dk_ref[keys, :] = dkt_s[n].T
                dv_ref[keys, :] = dvt_s[n].T
                return carry

            lax.fori_loop(0, nq, flush, 0)

    return pl.pallas_call(
        body,
        name="sb_bwd",
        grid=(4, nq),
        in_specs=[
            pl.BlockSpec((tq, 128), lambda p, i: (i, 12 + p)),
            pl.BlockSpec((T, 128), lambda p, i: (0, 16 + p)),
            pl.BlockSpec((T, 128), lambda p, i: (0, 20 + p)),
            pl.BlockSpec((2, 1, tq), lambda p, i: (p, 0, i)),
            pl.BlockSpec((1, 1, nq), lambda p, i: (p, 0, 0)),
            pl.BlockSpec((tq, 128), lambda p, i: (i, p)),
        ],
        out_specs=[
            pl.BlockSpec((tq, 128), lambda p, i: (i, p)),
            pl.BlockSpec((T, 128), lambda p, i: (0, p)),
            pl.BlockSpec((T, 128), lambda p, i: (0, p)),
        ],
        out_shape=[
            jax.ShapeDtypeStruct((T, FOX_W), BF16),
            jax.ShapeDtypeStruct((T, FOX_W), F32),
            jax.ShapeDtypeStruct((T, FOX_W), F32),
        ],
        scratch_shapes=[
            pltpu.VMEM((2, tq, 128), BF16),
            pltpu.VMEM((2, tq, 128), BF16),
            pltpu.VMEM((2, tq, 128), F32),
            pltpu.VMEM((2, 2, tq, tq), F32),
            pltpu.VMEM((2, 2, tq, tq), F32),
            pltpu.VMEM((2, 2, tq, tq), BF16),
            pltpu.VMEM((2, 2, tq, tq), BF16),
            pltpu.VMEM((2, tq, 128), F32),
            pltpu.VMEM((2, tq, 128), F32),
            pltpu.VMEM((2, tq, 128), F32),
            pltpu.VMEM((128, tq), BF16),
            pltpu.VMEM((128, tq), BF16),
            pltpu.VMEM((nq, 128, tq), F32),
            pltpu.VMEM((nq, 128, tq), F32),
        ],
        compiler_params=_cparams(("arbitrary", "arbitrary")),
    )(proj, proj, proj, ltot, live, d_o)


def _fox_bwd(proj, c_col, c_row, c_ends, lse, d_o, o, tq, scatter=()):
    T = proj.shape[0]
    nq = T // tq
    ns = len(scatter)

    def body(*refs):
        q_ref, k_ref, v_ref, cq_ref, ck_ref, cke_ref, lse_ref, do_ref, o_ref = refs[:9]
        dq_ref, dk_ref, dv_ref, dck_ref, dcq_ref = refs[9 + ns:14 + ns]
        (qh_s, doh_s, delta_s, shift_s, z_s, dp_s, pb_s, dsb_s, rs_s, dq_s,
         kn_s, qt_s, dot_s, dkt_s, dvt_s, kt_s) = refs[14 + 2 * ns:30 + 2 * ns]
        i = pl.program_id(1)
        if ns:
            pair = pl.program_id(0)
            finish = _ride_along(_Scatter(refs[9:9 + ns], refs[14 + ns:14 + 2 * ns], *refs[30 + 2 * ns:]),
                                 (pair == 0) & (i == 0), None, (pair == 3) & (i == nq - 1))
        lane = lax.broadcasted_iota(jnp.int32, (1, 128), 1)

        @pl.when(i == 0)
        def _():
            dkt_s[...] = jnp.zeros_like(dkt_s)
            dvt_s[...] = jnp.zeros_like(dvt_s)
            dck_ref[...] = jnp.zeros_like(dck_ref)
            _fox_key_norms(k_ref, kn_s, lane)

            def turn(n, carry):
                kt_s[n] = k_ref[pl.ds(pl.multiple_of(n * tq, tq), tq), :].astype(F32).T.astype(BF16)
                return carry

            lax.fori_loop(0, nq, turn, 0)

        row = lax.broadcasted_iota(jnp.int32, (tq, tq), 0)
        col = lax.broadcasted_iota(jnp.int32, (tq, tq), 1)
        q = q_ref[...]
        dov = do_ref[...]
        ov = o_ref[...]
        qt_s[...] = (q.astype(F32) * (HEAD_DIM ** -0.5)).T.astype(BF16)
        dot_s[...] = dov.astype(F32).T.astype(BF16)
        for hh in range(2):
            qh, hmask = _head_q(q, hh, lane)
            dohb = jnp.where(hmask, dov, 0.0).astype(BF16)
            qh_s[hh] = qh
            doh_s[hh] = dohb
            delta_s[hh] = jnp.broadcast_to(jnp.sum(dohb.astype(F32) * ov, axis=1, keepdims=True), (tq, tq))
            shift_s[hh] = jnp.broadcast_to(_lanes_to_rows(cq_ref[hh] - lse_ref[hh], row == col), (tq, tq))
        rs_s[...] = jnp.zeros_like(rs_s)
        dq_s[...] = jnp.zeros_like(dq_s)

        def rows(t):
            return pl.ds(pl.multiple_of((i - t) * tq, tq), tq)

        def stage_a(t, slot):
            k = k_ref[rows(t), :]
            v = v_ref[rows(t), :]
            for hh in range(2):
                z_s[slot, hh] = _nt(qh_s[hh], k)
                dp_s[slot, hh] = _nt(doh_s[hh], v)

        def stage_b(t, slot, diag):
            for hh in range(2):
                s = z_s[slot, hh] + shift_s[hh] - ck_ref[hh, :, rows(t)]
                if diag:
                    s = jnp.where(col <= row, s, NEG)
                p = jnp.exp(s)
                ds = p * (dp_s[slot, hh] - delta_s[hh])
                pb_s[slot, hh] = p.astype(BF16)
                dsb_s[slot, hh] = ds.astype(BF16)
                dck_ref[hh, :, rows(t)] += jnp.sum(ds, axis=0, keepdims=True)
                rs_s[hh] += jnp.sum(ds, axis=1, keepdims=True)

        def stage_c(t, slot):
            for hh in range(2):
                dsb = dsb_s[slot, hh]
                dims = pl.ds(HEAD_DIM * hh, HEAD_DIM)
                dq_s[dims, :] += _nt(kt_s[i - t, dims, :], dsb)
                dkt_s[i - t, dims, :] += _nn(qt_s[dims, :], dsb)
                dvt_s[i - t, dims, :] += _nn(dot_s[dims, :], pb_s[slot, hh])

        _pipeline3(_fox_live_blocks(i, qh_s, kn_s, cq_ref, cke_ref), stage_a, stage_b, stage_c, False,
                   a_first=True)

        @pl.when(i == nq - 1)
        def _():
            def flush(n, carry):
                keys = pl.ds(pl.multiple_of(n * tq, tq), tq)
                dk_ref[keys, :] = dkt_s[n].T
                dv_ref[keys, :] = dvt_s[n].T
                return carry

            lax.fori_loop(0, nq, flush, 0)
        dcq_ref[0] = _rows_to_lanes(rs_s[0], row == col)
        dcq_ref[1] = _rows_to_lanes(rs_s[1], row == col)
        dq_ref[...] = (dq_s[...].T * (HEAD_DIM ** -0.5)).astype(BF16)
        if ns:
            finish()

    res = pl.pallas_call(
        body,
        name="fox_bwd",
        grid=(4, nq),
        in_specs=[
            pl.BlockSpec((tq, 128), lambda p, i: (i, p)),
            pl.BlockSpec((T, 128), lambda p, i: (0, 4 + p)),
            pl.BlockSpec((T, 128), lambda p, i: (0, 8 + p)),
            pl.BlockSpec((2, 1, tq), lambda p, i: (p, 0, i)),
            pl.BlockSpec((2, 1, T), lambda p, i: (p, 0, 0)),
            pl.BlockSpec((2, 1, nq), lambda p, i: (p, 0, 0)),
            pl.BlockSpec((2, 1, tq), lambda p, i: (p, 0, i)),
            pl.BlockSpec((tq, 128), lambda p, i: (i, p)),
            pl.BlockSpec((tq, 128), lambda p, i: (i, p)),
        ] + [_ANY] * ns,
        out_specs=[
            pl.BlockSpec((tq, 128), lambda p, i: (i, p)),
            pl.BlockSpec((T, 128), lambda p, i: (0, p)),
            pl.BlockSpec((T, 128), lambda p, i: (0, p)),
            pl.BlockSpec((2, 1, T), lambda p, i: (p, 0, 0)),
            pl.BlockSpec((2, 1, tq), lambda p, i: (p, 0, i)),
        ] + [_ANY] * ns,
        out_shape=[
            jax.ShapeDtypeStruct((T, FOX_W), BF16),
            jax.ShapeDtypeStruct((T, FOX_W), F32),
            jax.ShapeDtypeStruct((T, FOX_W), F32),
            jax.ShapeDtypeStruct((N_FOX, 1, T), F32),
            jax.ShapeDtypeStruct((N_FOX, 1, T), F32),
        ] + [jax.ShapeDtypeStruct(b.shape, b.dtype) for b in scatter],
        scratch_shapes=[
            pltpu.VMEM((2, tq, 128), BF16),
            pltpu.VMEM((2, tq, 128), BF16),
            pltpu.VMEM((2, tq, tq), F32),
            pltpu.VMEM((2, tq, tq), F32),
            pltpu.VMEM((2, 2, tq, tq), F32),
            pltpu.VMEM((2, 2, tq, tq), F32),
            pltpu.VMEM((2, 2, tq, tq), BF16),
            pltpu.VMEM((2, 2, tq, tq), BF16),
            pltpu.VMEM((2, tq, 128), F32),
            pltpu.VMEM((128, tq), F32),
            pltpu.VMEM((2, 8, 128), F32),
            pltpu.VMEM((128, tq), BF16),
            pltpu.VMEM((128, tq), BF16),
            pltpu.VMEM((nq, 128, tq), F32),
            pltpu.VMEM((nq, 128, tq), F32),
            pltpu.VMEM((nq, 128, tq), BF16),
        ] + (_comm_sems(ns) if ns else []),
        compiler_params=_cparams(("arbitrary", "arbitrary")),
    )(proj, proj, proj, c_col, c_row, c_ends, lse, d_o, o, *scatter)
    res = list(res)
    return (*res[:5], res[5:])


def _forget_bwd(dcq, dck, xf, h1, tc):
    H, T = xf.shape
    D = h1.shape[1]
    nc = T // tc

    def body(dcq_ref, dck_ref, xf_ref, h_ref, dxf_ref, db_ref, dwf_ref):
        row = lax.broadcasted_iota(jnp.int32, (tc, tc), 0)
        col = lax.broadcasted_iota(jnp.int32, (tc, tc), 1)
        from_here = jnp.where(row >= col, 1.0, 0.0).astype(BF16)

        def chunk(n, carry):
            run, db, dwf = carry
            cs = pl.multiple_of((nc - 1 - n) * tc, tc)
            dc = dcq_ref[:, pl.ds(cs, tc)] - dck_ref[:, pl.ds(cs, tc)]
            dlogf = _split_dot(dc, from_here, 3) + run
            xfv = xf_ref[:, pl.ds(cs, tc)]
            dxf = dlogf * jax.nn.sigmoid(-xfv)
            dxf_ref[:, pl.ds(cs, tc)] = dxf
            dwf = dwf + _nn(dxf.astype(BF16), h_ref[pl.ds(cs, tc), :])
            return dlogf[:, 0:1], db + jnp.sum(dxf, axis=1, keepdims=True), dwf

        zero = jnp.zeros((H, 1), F32)
        _, db, dwf = lax.fori_loop(0, nc, chunk, (zero, zero, jnp.zeros((H, D), F32)))
        db_ref[...] = db
        dwf_ref[...] = dwf

    return pl.pallas_call(
        body,
        name="forget_bwd",
        out_shape=[jax.ShapeDtypeStruct((H, T), F32), jax.ShapeDtypeStruct((H, 1), F32),
                   jax.ShapeDtypeStruct((H, D), F32)],
        compiler_params=_cparams(),
    )(dcq, dck, xf, h1)


def _inproj_bwd(pieces, dxf_t, w_in, w_f_t, x, g1, dx1, tm, scatter=()):
    T, D = x.shape
    N = w_in.shape[1]
    ns = len(scatter)
    nt = T // tm
    npc = len(pieces)

    def body(*refs):
        pc_refs = refs[:npc]
        dxf_ref, w_ref, wf_ref, x_ref, g_ref, dx1_ref = refs[npc:npc + 6]
        base = npc + 6
        dx_ref, dg_ref = refs[base + ns:base + 2 + ns]
        i = pl.program_id(0)
        if ns:
            exchange = _Scatter(refs[base:base + ns], refs[base + 2 + ns:base + 2 + 2 * ns],
                                *refs[base + 2 + 2 * ns:])

            @pl.when(i == 0)
            def _():
                exchange.start()

        @pl.when(i == 0)
        def _():
            dg_ref[...] = jnp.zeros_like(dg_ref)

        dh = _nn(dxf_ref[...], wf_ref[...].astype(F32))
        for k, pc_ref in enumerate(pc_refs):
            dh = dh + _nt(pc_ref[...].astype(BF16), w_ref[:, k * FOX_W:(k + 1) * FOX_W])
        dx, dg = _norm_bwd(x_ref[...], g_ref[...], dh)
        dx_ref[...] = dx1_ref[...] + dx
        dg_ref[...] += dg
        if ns:
            @pl.when(i == nt - 1)
            def _():
                exchange.finish()

    res = pl.pallas_call(
        body,
        name="inproj_bwd",
        grid=(nt,),
        in_specs=[pl.BlockSpec((tm, FOX_W), lambda i: (i, 0))] * npc + [
            pl.BlockSpec((tm, N_FOX), lambda i: (i, 0)),
            pl.BlockSpec((D, N), lambda i: (0, 0)),
            pl.BlockSpec((N_FOX, D), lambda i: (0, 0)),
            pl.BlockSpec((tm, D), lambda i: (i, 0)),
            pl.BlockSpec((1, D), lambda i: (0, 0)),
            pl.BlockSpec((tm, D), lambda i: (i, 0)),
        ] + [_ANY] * ns,
        out_specs=[
            pl.BlockSpec((tm, D), lambda i: (i, 0)),
            pl.BlockSpec((1, D), lambda i: (0, 0)),
        ] + [_ANY] * ns,
        out_shape=[jax.ShapeDtypeStruct((T, D), F32), jax.ShapeDtypeStruct((1, D), F32)]
        + [jax.ShapeDtypeStruct(b.shape, b.dtype) for b in scatter],
        scratch_shapes=_comm_sems(ns) if ns else [],
        compiler_params=_cparams(("arbitrary",)),
    )(*pieces, dxf_t, w_in, w_f_t, x, g1, dx1, *scatter)
    res = list(res)
    return res[0], res[1], res[2:]


def _dw_in(h1, pieces, name):
    T, K = h1.shape
    bt = min(T, 512)
    nt = T // bt
    npc = len(pieces)

    def body(*refs):
        a_ref = refs[0]
        pc_refs = refs[1:1 + npc]
        o_ref, acc_ref = refs[1 + npc:]
        t = pl.program_id(0)

        @pl.when(t == 0)
        def _():
            acc_ref[...] = jnp.zeros_like(acc_ref)

        a = a_ref[...]
        for k, pc_ref in enumerate(pc_refs):
            acc_ref[:, k * FOX_W:(k + 1) * FOX_W] += _tn(a, pc_ref[...].astype(BF16))

        @pl.when(t == nt - 1)
        def _():
            o_ref[...] = acc_ref[...].astype(BF16)

    return pl.pallas_call(
        body,
        name=name,
        grid=(nt,),
        in_specs=[pl.BlockSpec((bt, K), lambda t: (t, 0))] + [pl.BlockSpec((bt, FOX_W), lambda t: (t, 0))] * npc,
        out_specs=pl.BlockSpec((K, npc * FOX_W), lambda t: (0, 0)),
        out_shape=jax.ShapeDtypeStruct((K, npc * FOX_W), BF16),
        scratch_shapes=[pltpu.VMEM((K, npc * FOX_W), F32)],
        compiler_params=_cparams(("arbitrary",)),
    )(h1, *pieces)


def _matmul_tn(a, b, name, cast_b=False):
    T, K = a.shape
    N = b.shape[1]
    bt = min(T, 512)
    bk = _tile_div(K, 1536)
    bn = _tile_div(N, 1536)
    nt = T // bt

    def body(a_ref, b_ref, o_ref, acc_ref):
        t = pl.program_id(2)

        @pl.when(t == 0)
        def _():
            acc_ref[...] = jnp.zeros_like(acc_ref)

        bv = b_ref[...]
        if cast_b:
            bv = bv.astype(BF16)
        acc_ref[...] += _tn(a_ref[...], bv)

        @pl.when(t == nt - 1)
        def _():
            o_ref[...] = acc_ref[...].astype(BF16)

    return pl.pallas_call(
        body,
        name=name,
        grid=(K // bk, N // bn, nt),
        in_specs=[
            pl.BlockSpec((bt, bk), lambda k, n, t: (t, k)),
            pl.BlockSpec((bt, bn), lambda k, n, t: (t, n)),
        ],
        out_specs=pl.BlockSpec((bk, bn), lambda k, n, t: (k, n)),
        out_shape=jax.ShapeDtypeStruct((K, N), BF16),
        scratch_shapes=[pltpu.VMEM((bk, bn), F32)],
        compiler_params=_cparams(("arbitrary", "arbitrary", "arbitrary")),
    )(a, b)


def _local_step(x, mem, target, p, tm, tq, late=None):
    T, D = x.shape
    w_in = p["w_in"]
    w_qkv = w_in[:, :QKV_W]
    w_f_t = w_in[:, QKV_W:].T
    b_f = p["b_forget"].reshape(N_FOX, 1)

    proj, h1, xf, c = _inproj_fwd(x, p["attn_norm_g"], w_qkv, w_f_t, b_f, tm)
    c_col = c.reshape(N_FOX, 1, T)
    c_row = c.reshape(N_FOX, 1, T)
    c_ends = c[:, tq - 1::tq].reshape(N_FOX, 1, T // tq)
    fox_o, lse, gathered = _fox_fwd(proj, c_col, c_row, c_ends, tq, gather=[late[n] for n in _LATE] if late else ())
    if late:
        p = dict(p, **{n: _gathered_full(n, gv) for n, gv in zip(_LATE, gathered)})
    sb_o, sb_ltot, sb_live = _sb_fwd(proj, tq)
    x1, mixed = _post_attn_fwd(fox_o, sb_o, p["fox_out_g"], p["sb_out_g"], p["w_out"], x, tm)
    mb, kv = _mem_kv_fwd(mem, p["mem_norm_g"], p["w_mkv"])
    x2, h2, qb, om = _xattn_fwd(x1, p["xattn_norm_g"], p["w_mq"], kv, p["w_mo"], tm)
    tf = 2 * tm if T % (2 * tm) == 0 else tm
    x3, h3, ug, uv, yg, yv, a = _ffn_fwd(
        x2, p["ffn_norm_g"], p["w_up"], p["conv_w"], p["conv_b"], p["w_down"], tf)
    dx3, loss_blk, d_final_g = _loss_head(x3, p["final_norm_g"], target, tm)

    g = {"final_norm_g": d_final_g}
    dx2, du_g, du_v, g["ffn_norm_g"], dc_g, dc_v = _ffn_bwd(
        dx3, x2, p["ffn_norm_g"], ug, uv, yg, yv, p["conv_w"], p["w_down"], p["w_up"], tf)
    g["w_down"] = _matmul_tn(a, dx3, "dw_down", cast_b=True)
    g["w_up"] = jnp.concatenate([_matmul_tn(h3, du_g, "dw_up_gate"), _matmul_tn(h3, du_v, "dw_up_val")], axis=1)
    dconv = jnp.concatenate([dc_g, dc_v], axis=1)
    g["conv_w"] = dconv[0:3]
    g["conv_b"] = dconv[3:4]
    dx1, dq_m, dkv, g["xattn_norm_g"] = _xattn_bwd(dx2, x1, p["xattn_norm_g"], qb, kv, p["w_mo"], p["w_mq"], tm)
    g["w_mo"] = _matmul_tn(om, dx2, "dw_mo", cast_b=True)
    g["w_mq"] = _matmul_tn(h2, dq_m, "dw_mq")
    g["w_mkv"], g["mem_norm_g"] = _mem_kv_bwd(mem, p["mem_norm_g"], mb, dkv, p["w_mkv"])
    d_fox, d_sb, g["fox_out_g"], g["sb_out_g"] = _post_attn_bwd(
        dx1, fox_o, sb_o, p["fox_out_g"], p["sb_out_g"], p["w_out"], tm)
    g["w_out"] = _matmul_tn(mixed, dx1, "dw_out", cast_b=True)
    dq_s, dk_s, dv_s = _sb_bwd(proj, sb_ltot, sb_live, d_sb, tq)
    dq_f, dk_f, dv_f, dck, dcq, parts = _fox_bwd(
        proj, c_col, c_row, c_ends, lse, d_fox, fox_o, tq,
        scatter=[_grad_blocks(n, g[n]) for n in _LATE] if late else ())
    if late:
        g["parts"] = dict(zip(_LATE, parts))
    dxf, db, dwf_t = _forget_bwd(dcq.reshape(N_FOX, T), dck.reshape(N_FOX, T), xf, h1, min(T, 512))
    g["b_forget"] = db.reshape(1, N_FOX)
    pieces = [dq_f, dk_f, dv_f, dq_s, dk_s, dv_s]
    g["w_in"] = jnp.concatenate([_dw_in(h1, pieces, "dw_in"), dwf_t.T.astype(BF16)], axis=1)
    grad_x, g["attn_norm_g"], parts = _inproj_bwd(
        pieces, dxf.T, w_in, w_f_t, x, p["attn_norm_g"], dx1, tm,
        scatter=[_grad_blocks("w_in", g["w_in"])] if late else ())
    if late:
        (g["parts"]["w_in"],) = parts
    return loss_blk, grad_x, g


def _mesh_pos():
    return lax.axis_index("x"), lax.axis_index("y"), lax.axis_index("c")


def _flip(pos, k):
    return tuple(1 - v if (k >> b) & 1 else v for v, b in zip(pos, (2, 1, 0)))


def _slot(pos):
    return 4 * pos[0] + 2 * pos[1] + pos[2]


_CHIPS = (4, 2, 6)


def _comm_sems(n):
    return [pltpu.SemaphoreType.DMA((7 * n,)), pltpu.SemaphoreType.DMA((7 * n,)), pltpu.SemaphoreType.DMA((n,))]


class _Gather:
    def __init__(self, ins, outs, send_sems, recv_sems, local_sems):
        self.ins, self.outs, self.n = ins, outs, len(ins)
        self.send_sems, self.recv_sems, self.local_sems = send_sems, recv_sems, local_sems
        self.me = _mesh_pos()
        self.sibling = _flip(self.me, 1)

    def _copy(self, a, kk, block, to, src=None):
        rows = self.outs[a].at[_slot(block)]
        return pltpu.make_async_remote_copy(
            src_ref=rows if src is None else src, dst_ref=rows,
            send_sem=self.send_sems.at[7 * a + kk], recv_sem=self.recv_sems.at[7 * a + kk],
            device_id=to, device_id_type=MESH)

    def _mine(self):
        return [pltpu.make_async_copy(self.ins[a], self.outs[a].at[_slot(self.me)], self.local_sems.at[a])
                for a in range(self.n)]

    def _first(self):
        out = []
        for a in range(self.n):
            out.append(self._copy(a, 0, self.me, self.sibling, src=self.ins[a]))
            out += [self._copy(a, 1 + j, self.me, _flip(self.me, k), src=self.ins[a]) for j, k in enumerate(_CHIPS)]
        return out

    def _passed(self):
        return [self._copy(a, 4 + j, _flip(self.me, k), self.sibling)
                for j, k in enumerate(_CHIPS) for a in range(self.n)]

    def start(self):
        for cp in self._mine() + self._first():
            cp.start()

    def forward(self):
        for j, k in enumerate(_CHIPS):
            for a in range(self.n):
                self._copy(a, 1 + j, _flip(self.me, k), self.me).wait_recv()
                self._copy(a, 4 + j, _flip(self.me, k), self.sibling).start()

    def finish(self):
        for a in range(self.n):
            self._copy(a, 0, self.sibling, self.me).wait_recv()
            for j, k in enumerate(_CHIPS):
                self._copy(a, 4 + j, _flip(self.sibling, k), self.me).wait_recv()
        for cp in self._first() + self._passed():
            cp.wait_send()
        for cp in self._mine():
            cp.wait()


class _Scatter:
    def __init__(self, ins, outs, send_sems, recv_sems, local_sems):
        self.ins, self.outs, self.n = ins, outs, len(ins)
        self.send_sems, self.recv_sems, self.local_sems = send_sems, recv_sems, local_sems
        self.me = _mesh_pos()

    def _copy(self, a, k, landed=False):
        peer = _flip(self.me, k)
        return pltpu.make_async_remote_copy(
            src_ref=self.ins[a].at[_slot(peer)], dst_ref=self.outs[a].at[_slot(peer if landed else self.me)],
            send_sem=self.send_sems.at[7 * a + k - 1], recv_sem=self.recv_sems.at[7 * a + k - 1],
            device_id=peer, device_id_type=MESH)

    def _mine(self):
        s = _slot(self.me)
        return [pltpu.make_async_copy(self.ins[a].at[s], self.outs[a].at[s], self.local_sems.at[a])
                for a in range(self.n)]

    def start(self):
        for cp in self._mine() + [self._copy(a, k) for k in range(1, 8) for a in range(self.n)]:
            cp.start()

    def finish(self):
        for k in range(1, 8):
            for a in range(self.n):
                self._copy(a, k, landed=True).wait_recv()
        for k in range(1, 8):
            for a in range(self.n):
                self._copy(a, k).wait_send()
        for cp in self._mine():
            cp.wait()


_ANY = pl.BlockSpec(memory_space=pl.ANY)


def _gathered_shapes(shards):
    return [jax.ShapeDtypeStruct((N_DEV,) + s.shape, s.dtype) for s in shards]


def _all_gather(shards, name):
    n = len(shards)

    def body(*refs):
        g = _Gather(refs[:n], refs[n:2 * n], *refs[2 * n:])
        g.start()
        g.forward()
        g.finish()

    return pl.pallas_call(
        body, name=name, in_specs=[_ANY] * n, out_specs=[_ANY] * n,
        out_shape=_gathered_shapes(shards), scratch_shapes=_comm_sems(n),
    )(*shards)


def _adamw_math(w, g, m, v):
    m2 = ADAM_B1 * m + (1.0 - ADAM_B1) * g
    v2 = ADAM_B2 * v + (1.0 - ADAM_B2) * (g * g)
    m_hat = m2 / (1.0 - ADAM_B1 ** ADAM_STEP)
    v_hat = v2 / (1.0 - ADAM_B2 ** ADAM_STEP)
    delta = -ADAM_LR * (m_hat / (jnp.sqrt(v_hat) + ADAM_EPS) + ADAM_WD * w)
    return delta, m2, v2


def _adamw(w, parts, m, v, name):
    R, C = w.shape
    br = 128 if R % 128 == 0 else R

    def body(w_ref, p_ref, m_ref, v_ref, g_ref, d_ref, nm_ref, nv_ref):
        g = p_ref[0].astype(F32)
        for s in range(1, N_DEV):
            g = g + p_ref[s].astype(F32)
        g_ref[...] = g
        d_ref[...], nm_ref[...], nv_ref[...] = _adamw_math(w_ref[...], g, m_ref[...], v_ref[...])

    spec = pl.BlockSpec((br, C), lambda i: (i, 0))
    return pl.pallas_call(
        body,
        name=name,
        grid=(R // br,),
        in_specs=[spec, pl.BlockSpec((N_DEV, br, C), lambda i: (0, i, 0)), spec, spec],
        out_specs=[spec] * 4,
        out_shape=[jax.ShapeDtypeStruct((R, C), F32)] * 4,
        compiler_params=_cparams(("arbitrary",)),
    )(w, parts, m, v)


_SHARDED = ("w_in", "w_out", "w_mq", "w_mkv", "w_mo", "w_up", "conv_w", "w_down")
_LATE = _SHARDED[1:]
_COL_SHARDED = ("w_in", "w_mkv", "w_up", "conv_w")
_REPLICATED = ("attn_norm_g", "b_forget", "fox_out_g", "sb_out_g", "xattn_norm_g", "mem_norm_g",
               "ffn_norm_g", "conv_b", "final_norm_g")
_WEIGHTS = ("attn_norm_g", "w_in", "b_forget", "fox_out_g", "sb_out_g", "w_out", "xattn_norm_g", "mem_norm_g",
            "w_mq", "w_mkv", "w_mo", "ffn_norm_g", "w_up", "conv_w", "conv_b", "w_down", "final_norm_g")


def _pack_rows(n):
    return -(-n // 128)


def _pack(vals, rows_total):
    parts = []
    for v in vals:
        flat = v.reshape(-1)
        parts.append(jnp.pad(flat, (0, _pack_rows(flat.shape[0]) * 128 - flat.shape[0])))
    flat = jnp.concatenate(parts)
    return jnp.pad(flat, (0, rows_total * 128 - flat.shape[0])).reshape(rows_total, 128)


def _unpack(packed, shapes):
    out = []
    r = 0
    for shp in shapes:
        n = 1
        for d in shp:
            n *= d
        out.append(packed[r:r + _pack_rows(n)].reshape(-1)[:n].reshape(shp))
        r += _pack_rows(n)
    return out


def _gathered_full(name, gathered):
    if name in _COL_SHARDED:
        return jnp.transpose(gathered, (1, 0, 2)).reshape(gathered.shape[1], -1)
    return gathered.reshape(-1, gathered.shape[2])


def _to_blocks(name, full):
    if name in _COL_SHARDED:
        r = full.shape[0]
        return jnp.transpose(full.reshape(r, N_DEV, -1), (1, 0, 2))
    return full.reshape(N_DEV, -1, full.shape[1])


def _grad_blocks(name, full):
    blocks = _to_blocks(name, full)
    return blocks if name == "conv_w" else blocks.astype(BF16)


def _step(args, tm, tq):
    w = {n: args[n] for n in _WEIGHTS}
    mom = {n: args["m_" + n] for n in _WEIGHTS}
    var = {n: args["v_" + n] for n in _WEIGHTS}
    x = args["x"][0]
    mem = args["mem"][0]
    target = args["loss_target"][0]

    def flat2(a):
        return a.reshape(a.shape[-2], a.shape[-1]) if a.ndim == 3 else a.reshape(1, -1)

    shards = {n: flat2(w[n]) if n == "conv_w" else flat2(w[n]).astype(BF16) for n in _SHARDED}
    (w_in_all,) = _all_gather([shards["w_in"]], "gather_w_in")
    p = {"w_in": _gathered_full("w_in", w_in_all)}
    for n in _REPLICATED:
        p[n] = flat2(w[n])

    loss_blk, grad_x, g = _local_step(x, mem, target, p, tm, tq, late={n: shards[n] for n in _LATE})

    parts = g["parts"]
    out = {}
    for n in _SHARDED:
        res = _adamw(flat2(w[n]), parts[n], flat2(mom[n]), flat2(var[n]), "adamw_" + n)
        out[n] = [r.reshape(w[n].shape) for r in res]

    shapes = [w[n].shape for n in _REPLICATED]
    rows = sum(_pack_rows(flat2(w[n]).shape[1]) for n in _REPLICATED) + 1
    rows = -(-rows // 8) * 8
    g_pack = _pack([g[n] for n in _REPLICATED] + [loss_blk[0:1, :]], rows)
    (g_all,) = _all_gather([g_pack], "gather_small")
    res = _adamw(_pack([w[n] for n in _REPLICATED], rows), g_all,
                 _pack([mom[n] for n in _REPLICATED], rows), _pack([var[n] for n in _REPLICATED], rows),
                 "adamw_small")
    n_rows_params = sum(_pack_rows(flat2(w[n]).shape[1]) for n in _REPLICATED)
    loss = res[0][n_rows_params, 0]
    unpacked = [_unpack(r, shapes) for r in res]
    for k, n in enumerate(_REPLICATED):
        out[n] = [unpacked[q][k] for q in range(4)]

    grads = [out[n][0] for n in _WEIGHTS]
    deltas = [out[n][1] for n in _WEIGHTS]
    new_m = [out[n][2] for n in _WEIGHTS]
    new_v = [out[n][3] for n in _WEIGHTS]
    return (loss, grad_x[None], *grads, *deltas, *new_m, *new_v)


def kernel(x, mem, attn_norm_g, w_in, b_forget, fox_out_g, sb_out_g, w_out, xattn_norm_g, mem_norm_g, w_mq, w_mkv, w_mo, ffn_norm_g, w_up, conv_w, conv_b, w_down, final_norm_g, loss_target, m_attn_norm_g, m_w_in, m_b_forget, m_fox_out_g, m_sb_out_g, m_w_out, m_xattn_norm_g, m_mem_norm_g, m_w_mq, m_w_mkv, m_w_mo, m_ffn_norm_g, m_w_up, m_conv_w, m_conv_b, m_w_down, m_final_norm_g, v_attn_norm_g, v_w_in, v_b_forget, v_fox_out_g, v_sb_out_g, v_w_out, v_xattn_norm_g, v_mem_norm_g, v_w_mq, v_w_mkv, v_w_mo, v_ffn_norm_g, v_w_up, v_conv_w, v_conv_b, v_w_down, v_final_norm_g):
    args = dict(locals())
    T = x.shape[1]
    return _step(args, tm=min(T, 512), tq=min(T, 256))
```

```python
import functools

import jax
import jax.numpy as jnp
from jax import lax
from jax.experimental import pallas as pl
from jax.experimental.pallas import tpu as pltpu

F32 = jnp.float32
BF16 = jnp.bfloat16
EPS = 1e-6
NEG = -1e30
LOG2E = 1.4426950408889634

HEAD_DIM = 64
N_FOX = 8
FOX_W = 512
QKV_W = 3072
N_MEM_HEADS = 4
MEM_HD = 256
D_FF = 2816
FF_CHUNK = 256
N_DEV = 8

ADAM_LR = 0.001
ADAM_B1 = 0.9
ADAM_B2 = 0.999
ADAM_EPS = 1e-08
ADAM_WD = 0.01
ADAM_STEP = 10

SB_SUM_TERMS = 1

VMEM_LIMIT = 56 * 1024 * 1024
MESH = pl.DeviceIdType.MESH


def _cparams(sem=None):
    return pltpu.CompilerParams(dimension_semantics=sem, vmem_limit_bytes=VMEM_LIMIT)


def _nt(a, b):
    return lax.dot_general(a, b, (((1,), (1,)), ((), ())), preferred_element_type=F32)


def _tn(a, b):
    return lax.dot_general(a, b, (((0,), (0,)), ((), ())), preferred_element_type=F32)


def _nn(a, b):
    return jnp.dot(a, b, preferred_element_type=F32)


def _split_dot(a, m01, terms):
    out = None
    r = a
    for t in range(terms):
        p = r.astype(BF16)
        d = _nn(p, m01)
        out = d if out is None else out + d
        if t + 1 < terms:
            r = r - p.astype(F32)
    return out


def _rstd(xv):
    return lax.rsqrt(jnp.mean(xv * xv, axis=-1, keepdims=True) + EPS)


def _norm_bwd(xv, g, dh):
    r = _rstd(xv)
    xhat = xv * r
    dxhat = dh * g
    dx = r * (dxhat - xhat * jnp.mean(dxhat * xhat, axis=-1, keepdims=True))
    dg = jnp.sum(dh * xhat, axis=0, keepdims=True)
    return dx, dg


def _tile_div(n, cap):
    best = None
    for d in range(128, min(n, cap) + 1, 128):
        if n % d == 0:
            best = d
    assert best is not None, n
    return best


def _inproj_fwd(x, g1, w_qkv, w_f_t, b_f, tm):
    T, D = x.shape
    N = w_qkv.shape[1]
    H = w_f_t.shape[0]

    def body(x_ref, g_ref, w_ref, wf_ref, b_ref, proj_ref, h_ref, xf_ref, c_ref, carry_ref):
        i = pl.program_id(0)

        @pl.when(i == 0)
        def _():
            carry_ref[...] = jnp.zeros_like(carry_ref)

        xv = x_ref[...]
        h = (xv * _rstd(xv) * g_ref[...]).astype(BF16)
        h_ref[...] = h
        for n0 in range(0, N, 512):
            proj_ref[:, n0:n0 + 512] = _nn(h, w_ref[:, n0:n0 + 512]).astype(BF16)
        xf = _nt(wf_ref[...], h) + b_ref[...]
        xf_ref[...] = xf
        logf = jnp.minimum(xf, 0.0) - jnp.log1p(jnp.exp(-jnp.abs(xf)))
        row = lax.broadcasted_iota(jnp.int32, (tm, tm), 0)
        col = lax.broadcasted_iota(jnp.int32, (tm, tm), 1)
        upper = jnp.where(row <= col, 1.0, 0.0).astype(BF16)
        c = _split_dot(logf, upper, 3) + carry_ref[...]
        c_ref[...] = c
        carry_ref[...] = c[:, tm - 1:tm]

    return pl.pallas_call(
        body,
        name="inproj_fwd",
        grid=(T // tm,),
        in_specs=[
            pl.BlockSpec((tm, D), lambda i: (i, 0)),
            pl.BlockSpec((1, D), lambda i: (0, 0)),
            pl.BlockSpec((D, N), lambda i: (0, 0)),
            pl.BlockSpec((H, D), lambda i: (0, 0)),
            pl.BlockSpec((H, 1), lambda i: (0, 0)),
        ],
        out_specs=[
            pl.BlockSpec((tm, N), lambda i: (i, 0)),
            pl.BlockSpec((tm, D), lambda i: (i, 0)),
            pl.BlockSpec((H, tm), lambda i: (0, i)),
            pl.BlockSpec((H, tm), lambda i: (0, i)),
        ],
        out_shape=[
            jax.ShapeDtypeStruct((T, N), BF16),
            jax.ShapeDtypeStruct((T, D), BF16),
            jax.ShapeDtypeStruct((H, T), F32),
            jax.ShapeDtypeStruct((H, T), F32),
        ],
        scratch_shapes=[pltpu.VMEM((H, 1), F32)],
        compiler_params=_cparams(("arbitrary",)),
    )(x, g1, w_qkv, w_f_t, b_f)


def _head_q(q, hh, lane):
    hmask = (lane >= HEAD_DIM * hh) & (lane < HEAD_DIM * (hh + 1))
    qh = jnp.where(hmask, q.astype(F32), 0.0) * (HEAD_DIM ** -0.5)
    return qh.astype(BF16), hmask


def _pipeline3(n, stage_a, stage_b, stage_c, diag_last, alive=None, a_first=False):
    stage_a(0, 0)
    if diag_last:
        @pl.when(n == 1)
        def _():
            stage_b(0, 0, True)

        @pl.when(n >= 2)
        def _():
            stage_a(1, 1)
            stage_b(0, 0, False)
    else:
        stage_a(jnp.minimum(1, n - 1), 1)
        stage_b(0, 0, True)

    def pair(m, carry):
        t = 2 + 2 * m
        if a_first:
            stage_a(t, 0)
            stage_b(t - 1, 1, False)
            stage_c(t - 2, 0)
            stage_a(t + 1, 1)
            stage_b(t, 0, False)
            stage_c(t - 1, 1)
        else:
            stage_c(t - 2, 0)
            stage_b(t - 1, 1, False)
            stage_a(t, 0)
            stage_c(t - 1, 1)
            stage_b(t, 0, False)
            stage_a(t + 1, 1)
        return carry

    pairs = (n - 2) // 2
    if alive is None:
        lax.fori_loop(0, pairs, pair, 0)
        go_on = True
        done = n
    else:
        def more(state):
            return (state[0] < pairs) & state[1]

        def step(state):
            pair(state[0], 0)
            return state[0] + 1, alive()

        m_end, go_on = lax.while_loop(more, step, (jnp.int32(0), jnp.bool_(True)))
        done = jnp.where(go_on, n, 2 * m_end)
    odd = n % 2 == 1

    @pl.when((n >= 3) & odd & go_on)
    def _():
        stage_a(n - 1, 0)
        stage_c(n - 3, 0)
        stage_b(n - 2, 1, False)
        stage_c(n - 2, 1)
        stage_b(n - 1, 0, diag_last)
        stage_c(n - 1, 0)

    @pl.when((n == 1) & go_on)
    def _():
        stage_c(0, 0)

    @pl.when(jnp.logical_not(odd) & go_on)
    def _():
        stage_c(n - 2, 0)
        stage_b(n - 1, 1, diag_last)
        stage_c(n - 1, 1)

    return done


def _lanes2(x):
    return jnp.concatenate([x, x], axis=1)


def _lanes_to_rows(vec, eye):
    return jnp.sum(jnp.where(eye, jnp.broadcast_to(vec, eye.shape), 0.0), axis=1, keepdims=True)


def _rows_to_lanes(rep, eye):
    return jnp.sum(jnp.where(eye, _lanes2(rep), 0.0), axis=0, keepdims=True)


FOX_DEAD = -110.0


def _fox_key_norms(k_ref, kn_s, lane):
    T = k_ref.shape[0]
    rows = min(T, 512)
    for hh in range(2):
        hmask = (lane >= HEAD_DIM * hh) & (lane < HEAD_DIM * (hh + 1))

        def chunk(n, best, hmask=hmask):
            kf = jnp.where(hmask, k_ref[pl.ds(pl.multiple_of(n * rows, rows), rows), :].astype(F32), 0.0)
            sq = jnp.sum(kf * kf, axis=1, keepdims=True)
            return jnp.maximum(best, jnp.max(sq, axis=0, keepdims=True))

        best = lax.fori_loop(0, T // rows, chunk, jnp.zeros((1, 1), F32))
        kn_s[hh] = jnp.broadcast_to(best, kn_s.shape[1:])


def _fox_live_blocks(i, qh_s, kn_s, cq_ref, cke_ref):
    nq = cke_ref.shape[-1]
    jj = lax.broadcasted_iota(jnp.int32, (1, nq), 1)
    first = None
    for hh in range(2):
        qf = qh_s[hh].astype(F32)
        qn = jnp.max(jnp.sum(qf * qf, axis=1, keepdims=True), axis=0, keepdims=True)
        zb = jnp.sqrt(qn * kn_s[hh][0:1, 0:1]) * 1.001
        bound = (2.0 * zb + cq_ref[hh][:, 0:1]) - cke_ref[hh]
        live = (bound >= FOX_DEAD) & (jj <= i)
        f = jnp.min(jnp.where(live, jj, i).astype(F32), axis=1, keepdims=True)
        first = f if first is None else jnp.minimum(first, f)
    return i + 1 - first[0, 0].astype(jnp.int32)


def _ride_along(exchange, at_start, at_middle, at_end):
    @pl.when(at_start)
    def _():
        exchange.start()

    if at_middle is not None:
        @pl.when(at_middle)
        def _():
            exchange.forward()

    def finish():
        @pl.when(at_end)
        def _():
            exchange.finish()

    return finish


def _fox_fwd(proj, c_col, c_row, c_ends, tq, gather=()):
    T = proj.shape[0]
    assert tq == 256
    nq = T // tq
    ng = len(gather)

    def body(*refs):
        q_ref, k_ref, v_ref, cq_ref, ck_ref, cke_ref = refs[:6]
        o_ref, lse_ref = refs[6 + ng:8 + ng]
        qh_s, cq_s, z_s, p_s, al_s, m_s, acc_s, kn_s = refs[8 + 2 * ng:16 + 2 * ng]
        i = pl.program_id(1)
        if ng:
            pair = pl.program_id(0)
            finish = _ride_along(_Gather(refs[6:6 + ng], refs[8 + ng:8 + 2 * ng], *refs[16 + 2 * ng:]),
                                 (pair == 0) & (i == 0), (pair == 1) & (i == 0), (pair == 3) & (i == nq - 1))
        lane = lax.broadcasted_iota(jnp.int32, (1, 128), 1)
        row = lax.broadcasted_iota(jnp.int32, (tq, tq), 0)
        col = lax.broadcasted_iota(jnp.int32, (tq, tq), 1)

        @pl.when(i == 0)
        def _():
            _fox_key_norms(k_ref, kn_s, lane)

        q = q_ref[...]
        for hh in range(2):
            qh_s[hh] = _head_q(q, hh, lane)[0]
            cq_s[hh] = jnp.broadcast_to(_lanes_to_rows(cq_ref[hh], row == col), (tq, tq))
        m_s[...] = jnp.full(m_s.shape, NEG, F32)
        acc_s[...] = jnp.zeros_like(acc_s)

        def rows(t):
            return pl.ds(pl.multiple_of((i - t) * tq, tq), tq)

        def stage_a(t, slot):
            k = k_ref[rows(t), :]
            for hh in range(2):
                z_s[slot, hh] = _nt(qh_s[hh], k)

        def stage_b(t, slot, diag):
            for hh in range(2):
                s = z_s[slot, hh] + cq_s[hh] - ck_ref[hh, :, rows(t)]
                if diag:
                    s = jnp.where(col <= row, s, NEG)
                m = m_s[hh]
                half = jnp.maximum(s[:, :128], s[:, 128:])
                m_new = jnp.maximum(m, jnp.max(half, axis=1, keepdims=True))
                m_s[hh] = m_new
                al_s[slot, hh] = jnp.exp(m - m_new)
                p_s[slot, hh] = jnp.exp(s - _lanes2(m_new)).astype(BF16)

        def stage_c(t, slot):
            v = v_ref[rows(t), :]
            for hh in range(2):
                own = (lane >= HEAD_DIM * hh) & (lane < HEAD_DIM * (hh + 1))
                acc_s[hh] = (al_s[slot, hh] * acc_s[hh]
                             + _nn(p_s[slot, hh], jnp.where(own, v, 1.0).astype(BF16)))

        _pipeline3(_fox_live_blocks(i, qh_s, kn_s, cq_ref, cke_ref), stage_a, stage_b, stage_c, False)
        halves = []
        for hh in range(2):
            acc = acc_s[hh]
            own = (lane >= HEAD_DIM * hh) & (lane < HEAD_DIM * (hh + 1))
            halves.append(jnp.where(own, pltpu.roll(acc, HEAD_DIM, axis=1), acc))
        l0, l1 = halves
        o_ref[...] = jnp.where(lane < HEAD_DIM, acc_s[0] / l0, acc_s[1] / l1)
        lse_ref[0] = _rows_to_lanes(m_s[0] + jnp.log(l0), row == col)
        lse_ref[1] = _rows_to_lanes(m_s[1] + jnp.log(l1), row == col)
        if ng:
            finish()

    res = pl.pallas_call(
        body,
        name="fox_fwd",
        grid=(4, nq),
        in_specs=[
            pl.BlockSpec((tq, 128), lambda p, i: (i, p)),
            pl.BlockSpec((T, 128), lambda p, i: (0, 4 + p)),
            pl.BlockSpec((T, 128), lambda p, i: (0, 8 + p)),
            pl.BlockSpec((2, 1, tq), lambda p, i: (p, 0, i)),
            pl.BlockSpec((2, 1, T), lambda p, i: (p, 0, 0)),
            pl.BlockSpec((2, 1, nq), lambda p, i: (p, 0, 0)),
        ] + [_ANY] * ng,
        out_specs=[
            pl.BlockSpec((tq, 128), lambda p, i: (i, p)),
            pl.BlockSpec((2, 1, tq), lambda p, i: (p, 0, i)),
        ] + [_ANY] * ng,
        out_shape=[
            jax.ShapeDtypeStruct((T, FOX_W), F32),
            jax.ShapeDtypeStruct((N_FOX, 1, T), F32),
        ] + _gathered_shapes(gather),
        scratch_shapes=[
            pltpu.VMEM((2, tq, 128), BF16),
            pltpu.VMEM((2, tq, tq), F32),
            pltpu.VMEM((2, 2, tq, tq), F32),
            pltpu.VMEM((2, 2, tq, tq), BF16),
            pltpu.VMEM((2, 2, tq, 128), F32),
            pltpu.VMEM((2, tq, 128), F32),
            pltpu.VMEM((2, tq, 128), F32),
            pltpu.VMEM((2, 8, 128), F32),
        ] + (_comm_sems(ng) if ng else []),
        compiler_params=_cparams(("arbitrary", "arbitrary")),
    )(proj, proj, proj, c_col, c_row, c_ends, *gather)
    res = list(res)
    return res[0], res[1], res[2:]


def _sb_logs(zn, strict):
    e = jnp.exp2(jnp.abs(zn) * (-LOG2E))
    L = jnp.minimum(zn, 0.0) - jnp.log(1.0 + e)
    G = L - zn
    if strict is not None:
        L = jnp.where(strict, L, 0.0)
    return L, G


SB_DEAD = -110.0


def _sb_fwd(proj, tq):
    T = proj.shape[0]
    nq = T // tq

    def body(q_ref, k_ref, v_ref, o_ref, ltot_ref, live_ref, qh_s, z_s, g_s, tot_s, run_s, acc_s):
        i = pl.program_id(1)
        lane = lax.broadcasted_iota(jnp.int32, (1, 128), 1)
        row = lax.broadcasted_iota(jnp.int32, (tq, tq), 0)
        col = lax.broadcasted_iota(jnp.int32, (tq, tq), 1)
        strict = col < row
        later = jnp.where(row > col, 1.0, 0.0).astype(BF16)
        q = q_ref[...]
        for hh in range(2):
            qh_s[hh] = -_head_q(q, hh, lane)[0]
        run_s[...] = jnp.zeros_like(run_s)
        acc_s[...] = jnp.zeros_like(acc_s)

        def rows(t):
            return pl.ds(pl.multiple_of((i - t) * tq, tq), tq)

        def stage_a(t, slot):
            k = k_ref[rows(t), :]
            for hh in range(2):
                z_s[slot, hh] = _nt(qh_s[hh], k)

        def stage_b(t, slot, diag):
            for hh in range(2):
                L, g = _sb_logs(z_s[slot, hh], strict if diag else None)
                if diag:
                    g = jnp.where(strict, g, NEG)
                after = _split_dot(L, later, SB_SUM_TERMS)
                g_s[slot, hh] = g + after
                first = L[:, 0:1]
                if SB_SUM_TERMS == 1:
                    first = first.astype(BF16).astype(F32)
                tot_s[slot, hh] = jnp.broadcast_to(after[:, 0:1] + first, (tq, 128))

        def stage_c(t, slot):
            v = v_ref[rows(t), :]
            for hh in range(2):
                run = run_s[hh]
                a = jnp.exp(g_s[slot, hh] + _lanes2(run))
                acc_s[hh] += _nn(a.astype(BF16), v)
                run_s[hh] = run + tot_s[slot, hh]

        def alive():
            return jnp.max(jnp.maximum(run_s[0], run_s[1])) > SB_DEAD

        done = _pipeline3(i + 1, stage_a, stage_b, stage_c, False, alive)
        ltot_ref[0] = _rows_to_lanes(run_s[0], row == col)
        ltot_ref[1] = _rows_to_lanes(run_s[1], row == col)
        o_ref[...] = jnp.where(lane < HEAD_DIM, acc_s[0], acc_s[1])
        at = lax.broadcasted_iota(jnp.int32, (1, nq), 1)

        @pl.when(i == 0)
        def _():
            live_ref[0] = jnp.zeros((1, nq), F32)

        live_ref[0] = jnp.where(at == i, done.astype(F32), live_ref[0])

    return pl.pallas_call(
        body,
        name="sb_fwd",
        grid=(4, nq),
        in_specs=[
            pl.BlockSpec((tq, 128), lambda p, i: (i, 12 + p)),
            pl.BlockSpec((T, 128), lambda p, i: (0, 16 + p)),
            pl.BlockSpec((T, 128), lambda p, i: (0, 20 + p)),
        ],
        out_specs=[
            pl.BlockSpec((tq, 128), lambda p, i: (i, p)),
            pl.BlockSpec((2, 1, tq), lambda p, i: (p, 0, i)),
            pl.BlockSpec((1, 1, nq), lambda p, i: (p, 0, 0)),
        ],
        out_shape=[
            jax.ShapeDtypeStruct((T, FOX_W), F32),
            jax.ShapeDtypeStruct((N_FOX, 1, T), F32),
            jax.ShapeDtypeStruct((N_FOX // 2, 1, nq), F32),
        ],
        scratch_shapes=[
            pltpu.VMEM((2, tq, 128), BF16),
            pltpu.VMEM((2, 2, tq, tq), F32),
            pltpu.VMEM((2, 2, tq, tq), F32),
            pltpu.VMEM((2, 2, tq, 128), F32),
            pltpu.VMEM((2, tq, 128), F32),
            pltpu.VMEM((2, tq, 128), F32),
        ],
        compiler_params=_cparams(("arbitrary", "arbitrary")),
    )(proj, proj, proj)


def _post_attn_fwd(fox_o, sb_o, gf, gs, w_out, x, tm):
    T, D = x.shape

    def body(f_ref, s_ref, gf_ref, gs_ref, w_ref, x_ref, x1_ref, mix_ref):
        f = f_ref[...]
        s = s_ref[...]
        mix_ref[:, :FOX_W] = (f * _rstd(f) * gf_ref[...]).astype(BF16)
        mix_ref[:, FOX_W:] = (s * _rstd(s) * gs_ref[...]).astype(BF16)
        x1_ref[...] = x_ref[...] + _nn(mix_ref[...], w_ref[...])

    return pl.pallas_call(
        body,
        name="post_attn_fwd",
        grid=(T // tm,),
        in_specs=[
            pl.BlockSpec((tm, FOX_W), lambda i: (i, 0)),
            pl.BlockSpec((tm, FOX_W), lambda i: (i, 0)),
            pl.BlockSpec((1, FOX_W), lambda i: (0, 0)),
            pl.BlockSpec((1, FOX_W), lambda i: (0, 0)),
            pl.BlockSpec((D, D), lambda i: (0, 0)),
            pl.BlockSpec((tm, D), lambda i: (i, 0)),
        ],
        out_specs=[
            pl.BlockSpec((tm, D), lambda i: (i, 0)),
            pl.BlockSpec((tm, D), lambda i: (i, 0)),
        ],
        out_shape=[jax.ShapeDtypeStruct((T, D), F32), jax.ShapeDtypeStruct((T, D), BF16)],
        compiler_params=_cparams(("arbitrary",)),
    )(fox_o, sb_o, gf, gs, w_out, x)


def _mem_kv_fwd(mem, gm, w_mkv):
    M, D = mem.shape
    N = w_mkv.shape[1]

    def body(mem_ref, g_ref, w_ref, m_ref, kv_ref):
        mv = mem_ref[...]
        m = (mv * _rstd(mv) * g_ref[...]).astype(BF16)
        m_ref[...] = m
        for n0 in range(0, N, 512):
            kv_ref[:, n0:n0 + 512] = _nn(m, w_ref[:, n0:n0 + 512]).astype(BF16)

    return pl.pallas_call(
        body,
        name="mem_kv_fwd",
        out_shape=[jax.ShapeDtypeStruct((M, D), BF16), jax.ShapeDtypeStruct((M, N), BF16)],
        compiler_params=_cparams(),
    )(mem, gm, w_mkv)


def _xattn_probs(qb, kv, h):
    k = kv[:, h * MEM_HD:(h + 1) * MEM_HD]
    s = _nt(qb[:, h * MEM_HD:(h + 1) * MEM_HD], k) * (MEM_HD ** -0.5)
    s = s - jnp.max(s, axis=1, keepdims=True)
    p = jnp.exp(s)
    return p / jnp.sum(p, axis=1, keepdims=True)


def _xattn_fwd(x1, g2, w_mq, kv, w_mo, tm):
    T, D = x1.shape
    M = kv.shape[0]

    def body(x_ref, g_ref, wq_ref, kv_ref, wo_ref, x2_ref, h_ref, q_ref, om_ref):
        xv = x_ref[...]
        h = (xv * _rstd(xv) * g_ref[...]).astype(BF16)
        h_ref[...] = h
        q_ref[...] = _nn(h, wq_ref[...]).astype(BF16)
        qb = q_ref[...]
        kvv = kv_ref[...]
        for hd in range(N_MEM_HEADS):
            p = _xattn_probs(qb, kvv, hd)
            v = kvv[:, D + hd * MEM_HD:D + (hd + 1) * MEM_HD]
            om_ref[:, hd * MEM_HD:(hd + 1) * MEM_HD] = _nn(p.astype(BF16), v).astype(BF16)
        x2_ref[...] = xv + _nn(om_ref[...], wo_ref[...])

    return pl.pallas_call(
        body,
        name="xattn_fwd",
        grid=(T // tm,),
        in_specs=[
            pl.BlockSpec((tm, D), lambda i: (i, 0)),
            pl.BlockSpec((1, D), lambda i: (0, 0)),
            pl.BlockSpec((D, D), lambda i: (0, 0)),
            pl.BlockSpec((M, 2 * D), lambda i: (0, 0)),
            pl.BlockSpec((D, D), lambda i: (0, 0)),
        ],
        out_specs=[pl.BlockSpec((tm, D), lambda i: (i, 0))] * 4,
        out_shape=[jax.ShapeDtypeStruct((T, D), F32)] + [jax.ShapeDtypeStruct((T, D), BF16)] * 3,
        compiler_params=_cparams(("arbitrary",)),
    )(x1, g2, w_mq, kv, w_mo)


def _conv_taps(ext_ref, tm, back):
    if back:
        return ext_ref[pl.ds(6, tm), :], ext_ref[pl.ds(7, tm), :], ext_ref[pl.ds(8, tm), :]
    return ext_ref[pl.ds(0, tm), :], ext_ref[pl.ds(1, tm), :], ext_ref[pl.ds(2, tm), :]


def _ffn_fwd(x2, g3, w_up, conv_w, conv_b, w_down, tm):
    T, D = x2.shape
    fc = FF_CHUNK
    nj = D_FF // fc

    def body(x_ref, g_ref, wg_ref, wv_ref, cwg_ref, cwv_ref, cbg_ref, cbv_ref, wd_ref,
             x3_ref, h_ref, ug_ref, uv_ref, yg_ref, yv_ref, a_ref, acc_ref, carry_ref, ext_ref):
        i = pl.program_id(0)
        j = pl.program_id(1)

        @pl.when(j == 0)
        def _():
            xv = x_ref[...]
            h_ref[...] = (xv * _rstd(xv) * g_ref[...]).astype(BF16)
            acc_ref[...] = xv

        @pl.when(i == 0)
        def _():
            carry_ref[j] = jnp.zeros((2, 8, fc), F32)

        h = h_ref[...]
        halves = []
        for part, (w_ref, cw_ref, cb_ref, u_ref, y_ref) in enumerate(
                ((wg_ref, cwg_ref, cbg_ref, ug_ref, yg_ref), (wv_ref, cwv_ref, cbv_ref, uv_ref, yv_ref))):
            u = _nn(h, w_ref[...])
            u_ref[...] = u.astype(BF16)
            ext = ext_ref.at[part]
            ext[pl.ds(0, 8), :] = carry_ref[j, part]
            ext[pl.ds(8, tm), :] = u
            carry_ref[j, part] = u[tm - 8:, :]
            u2, u1, u0 = _conv_taps(ext, tm, True)
            cw = cw_ref[...]
            y = cb_ref[...] + cw[0:1] * u2 + cw[1:2] * u1 + cw[2:3] * u0
            y_ref[...] = y.astype(BF16)
            halves.append(y)
        gate, val = halves
        a = (gate * jax.nn.sigmoid(gate) * val).astype(BF16)
        a_ref[...] = a
        acc_ref[...] += _nn(a, wd_ref[...])

        @pl.when(j == nj - 1)
        def _():
            x3_ref[...] = acc_ref[...]

    return pl.pallas_call(
        body,
        name="ffn_fwd",
        grid=(T // tm, nj),
        in_specs=[
            pl.BlockSpec((tm, D), lambda i, j: (i, 0)),
            pl.BlockSpec((1, D), lambda i, j: (0, 0)),
            pl.BlockSpec((D, fc), lambda i, j: (0, j)),
            pl.BlockSpec((D, fc), lambda i, j: (0, nj + j)),
            pl.BlockSpec((3, fc), lambda i, j: (0, j)),
            pl.BlockSpec((3, fc), lambda i, j: (0, nj + j)),
            pl.BlockSpec((1, fc), lambda i, j: (0, j)),
            pl.BlockSpec((1, fc), lambda i, j: (0, nj + j)),
            pl.BlockSpec((fc, D), lambda i, j: (j, 0)),
        ],
        out_specs=[
            pl.BlockSpec((tm, D), lambda i, j: (i, 0)),
            pl.BlockSpec((tm, D), lambda i, j: (i, 0)),
        ] + [pl.BlockSpec((tm, fc), lambda i, j: (i, j))] * 5,
        out_shape=[
            jax.ShapeDtypeStruct((T, D), F32),
            jax.ShapeDtypeStruct((T, D), BF16),
        ] + [jax.ShapeDtypeStruct((T, D_FF), BF16)] * 5,
        scratch_shapes=[
            pltpu.VMEM((tm, D), F32),
            pltpu.VMEM((nj, 2, 8, fc), F32),
            pltpu.VMEM((2, tm + 8, fc), F32),
        ],
        compiler_params=_cparams(("arbitrary", "arbitrary")),
    )(x2, g3, w_up, w_up, conv_w, conv_w, conv_b, conv_b, w_down)


def _loss_head(x3, gfin, target, tm):
    T, D = x3.shape

    def body(x_ref, g_ref, t_ref, dx_ref, loss_ref, dg_ref):
        i = pl.program_id(0)

        @pl.when(i == 0)
        def _():
            loss_ref[...] = jnp.zeros_like(loss_ref)
            dg_ref[...] = jnp.zeros_like(dg_ref)

        xv = x_ref[...]
        g = g_ref[...]
        r = _rstd(xv)
        xhat = xv * r
        err = xhat * g - t_ref[...]
        part = jnp.sum(jnp.sum(err * err, axis=1, keepdims=True), axis=0, keepdims=True) * (0.5 / D)
        loss_ref[...] += jnp.broadcast_to(part, loss_ref.shape)
        dy = err * (1.0 / D)
        dg_ref[...] += jnp.sum(dy * xhat, axis=0, keepdims=True)
        dxhat = dy * g
        dx_ref[...] = r * (dxhat - xhat * jnp.mean(dxhat * xhat, axis=-1, keepdims=True))

    return pl.pallas_call(
        body,
        name="loss_head",
        grid=(T // tm,),
        in_specs=[
            pl.BlockSpec((tm, D), lambda i: (i, 0)),
            pl.BlockSpec((1, D), lambda i: (0, 0)),
            pl.BlockSpec((tm, D), lambda i: (i, 0)),
        ],
        out_specs=[
            pl.BlockSpec((tm, D), lambda i: (i, 0)),
            pl.BlockSpec((8, 128), lambda i: (0, 0)),
            pl.BlockSpec((1, D), lambda i: (0, 0)),
        ],
        out_shape=[
            jax.ShapeDtypeStruct((T, D), F32),
            jax.ShapeDtypeStruct((8, 128), F32),
            jax.ShapeDtypeStruct((1, D), F32),
        ],
        compiler_params=_cparams(("arbitrary",)),
    )(x3, gfin, target)


def _ffn_bwd(dx3, x2, g3, ug, uv, yg, yv, conv_w, w_down, w_up, tm):
    T, D = x2.shape
    fc = FF_CHUNK
    nj = D_FF // fc
    nt = T // tm

    def rev(i):
        return nt - 1 - i

    def body(dx3_ref, x_ref, g_ref, ug_ref, uv_ref, yg_ref, yv_ref, cwg_ref, cwv_ref,
             wd_ref, wug_ref, wuv_ref,
             dx2_ref, dug_ref, duv_ref, dg_ref, dcg_ref, dcv_ref,
             acc_ref, carry_ref, ext_ref):
        i = pl.program_id(0)
        j = pl.program_id(1)
        cols = pl.ds(pl.multiple_of(j * fc, fc), fc)

        @pl.when(j == 0)
        def _():
            acc_ref[...] = jnp.zeros_like(acc_ref)

        @pl.when((i == 0) & (j == 0))
        def _():
            dg_ref[...] = jnp.zeros_like(dg_ref)
            dcg_ref[...] = jnp.zeros_like(dcg_ref)
            dcv_ref[...] = jnp.zeros_like(dcv_ref)

        @pl.when(i == 0)
        def _():
            carry_ref[j] = jnp.zeros((2, 8, fc), F32)

        da = _nt(dx3_ref[...].astype(BF16), wd_ref[...])
        gate = yg_ref[...].astype(F32)
        val = yv_ref[...].astype(F32)
        sig = jax.nn.sigmoid(gate)
        silu = gate * sig
        dys = (da * val * (sig * (1.0 + gate * (1.0 - sig))), da * silu)
        for part, (dy, u_ref, cw_ref, du_ref, wu_ref, dc_ref) in enumerate(
                ((dys[0], ug_ref, cwg_ref, dug_ref, wug_ref, dcg_ref),
                 (dys[1], uv_ref, cwv_ref, duv_ref, wuv_ref, dcv_ref))):
            ext = ext_ref.at[part]
            ext[pl.ds(0, tm), :] = dy
            ext[pl.ds(tm, 8), :] = carry_ref[j, part]
            carry_ref[j, part] = dy[:8, :]
            d0, d1, d2 = _conv_taps(ext, tm, False)
            u = u_ref[...].astype(F32)
            upd = jnp.concatenate([
                jnp.sum(u * d2, axis=0, keepdims=True),
                jnp.sum(u * d1, axis=0, keepdims=True),
                jnp.sum(u * d0, axis=0, keepdims=True),
                jnp.sum(d0, axis=0, keepdims=True),
                jnp.zeros((4, fc), F32)], axis=0)
            dc_ref[:, cols] += upd
            cw = cw_ref[...]
            du = (cw[2:3] * d0 + cw[1:2] * d1 + cw[0:1] * d2).astype(BF16)
            du_ref[...] = du
            acc_ref[...] += _nt(du, wu_ref[...])

        @pl.when(j == nj - 1)
        def _():
            dx, dg = _norm_bwd(x_ref[...], g_ref[...], acc_ref[...])
            dx2_ref[...] = dx3_ref[...] + dx
            dg_ref[...] += dg

    return pl.pallas_call(
        body,
        name="ffn_bwd",
        grid=(nt, nj),
        in_specs=[
            pl.BlockSpec((tm, D), lambda i, j: (rev(i), 0)),
            pl.BlockSpec((tm, D), lambda i, j: (rev(i), 0)),
            pl.BlockSpec((1, D), lambda i, j: (0, 0)),
            pl.BlockSpec((tm, fc), lambda i, j: (rev(i), j)),
            pl.BlockSpec((tm, fc), lambda i, j: (rev(i), j)),
            pl.BlockSpec((tm, fc), lambda i, j: (rev(i), j)),
            pl.BlockSpec((tm, fc), lambda i, j: (rev(i), j)),
            pl.BlockSpec((3, fc), lambda i, j: (0, j)),
            pl.BlockSpec((3, fc), lambda i, j: (0, nj + j)),
            pl.BlockSpec((fc, D), lambda i, j: (j, 0)),
            pl.BlockSpec((D, fc), lambda i, j: (0, j)),
            pl.BlockSpec((D, fc), lambda i, j: (0, nj + j)),
        ],
        out_specs=[
            pl.BlockSpec((tm, D), lambda i, j: (rev(i), 0)),
            pl.BlockSpec((tm, fc), lambda i, j: (rev(i), j)),
            pl.BlockSpec((tm, fc), lambda i, j: (rev(i), j)),
            pl.BlockSpec((1, D), lambda i, j: (0, 0)),
            pl.BlockSpec((8, D_FF), lambda i, j: (0, 0)),
            pl.BlockSpec((8, D_FF), lambda i, j: (0, 0)),
        ],
        out_shape=[
            jax.ShapeDtypeStruct((T, D), F32),
            jax.ShapeDtypeStruct((T, D_FF), BF16),
            jax.ShapeDtypeStruct((T, D_FF), BF16),
            jax.ShapeDtypeStruct((1, D), F32),
            jax.ShapeDtypeStruct((8, D_FF), F32),
            jax.ShapeDtypeStruct((8, D_FF), F32),
        ],
        scratch_shapes=[
            pltpu.VMEM((tm, D), F32),
            pltpu.VMEM((nj, 2, 8, fc), F32),
            pltpu.VMEM((2, tm + 8, fc), F32),
        ],
        compiler_params=_cparams(("arbitrary", "arbitrary")),
    )(dx3, x2, g3, ug, uv, yg, yv, conv_w, conv_w, w_down, w_up, w_up)


def _xattn_bwd(dx2, x1, g2, qb, kv, w_mo, w_mq, tm):
    T, D = x1.shape
    M = kv.shape[0]

    def body(dx2_ref, x_ref, g_ref, q_ref, kv_ref, wo_ref, wq_ref, dx1_ref, dq_ref, dkv_ref, dg_ref):
        i = pl.program_id(0)

        @pl.when(i == 0)
        def _():
            dkv_ref[...] = jnp.zeros_like(dkv_ref)
            dg_ref[...] = jnp.zeros_like(dg_ref)

        dxv = dx2_ref[...]
        dom = _nt(dxv.astype(BF16), wo_ref[...]).astype(BF16)
        qb_ = q_ref[...]
        kvv = kv_ref[...]
        for hd in range(N_MEM_HEADS):
            sl = slice(hd * MEM_HD, (hd + 1) * MEM_HD)
            vsl = slice(D + hd * MEM_HD, D + (hd + 1) * MEM_HD)
            p = _xattn_probs(qb_, kvv, hd)
            dp = _nt(dom[:, sl], kvv[:, vsl])
            ds = (p * (dp - jnp.sum(p * dp, axis=1, keepdims=True)) * (MEM_HD ** -0.5)).astype(BF16)
            dq_ref[:, sl] = _nn(ds, kvv[:, sl]).astype(BF16)
            dkv_ref[:, sl] += _tn(ds, qb_[:, sl])
            dkv_ref[:, vsl] += _tn(p.astype(BF16), dom[:, sl])
        dh = _nt(dq_ref[...], wq_ref[...])
        dx, dg = _norm_bwd(x_ref[...], g_ref[...], dh)
        dx1_ref[...] = dxv + dx
        dg_ref[...] += dg

    return pl.pallas_call(
        body,
        name="xattn_bwd",
        grid=(T // tm,),
        in_specs=[
            pl.BlockSpec((tm, D), lambda i: (i, 0)),
            pl.BlockSpec((tm, D), lambda i: (i, 0)),
            pl.BlockSpec((1, D), lambda i: (0, 0)),
            pl.BlockSpec((tm, D), lambda i: (i, 0)),
            pl.BlockSpec((M, 2 * D), lambda i: (0, 0)),
            pl.BlockSpec((D, D), lambda i: (0, 0)),
            pl.BlockSpec((D, D), lambda i: (0, 0)),
        ],
        out_specs=[
            pl.BlockSpec((tm, D), lambda i: (i, 0)),
            pl.BlockSpec((tm, D), lambda i: (i, 0)),
            pl.BlockSpec((M, 2 * D), lambda i: (0, 0)),
            pl.BlockSpec((1, D), lambda i: (0, 0)),
        ],
        out_shape=[
            jax.ShapeDtypeStruct((T, D), F32),
            jax.ShapeDtypeStruct((T, D), BF16),
            jax.ShapeDtypeStruct((M, 2 * D), F32),
            jax.ShapeDtypeStruct((1, D), F32),
        ],
        compiler_params=_cparams(("arbitrary",)),
    )(dx2, x1, g2, qb, kv, w_mo, w_mq)


def _mem_kv_bwd(mem, gm, mb, dkv, w_mkv):
    M, D = mem.shape
    N = dkv.shape[1]

    def body(mem_ref, g_ref, m_ref, dkv_ref, w_ref, dw_ref, dg_ref):
        dkvb = dkv_ref[...].astype(BF16)
        for n0 in range(0, N, 512):
            dw_ref[:, n0:n0 + 512] = _tn(m_ref[...], dkvb[:, n0:n0 + 512]).astype(BF16)
        dm = _nt(dkvb, w_ref[...])
        mv = mem_ref[...]
        dg_ref[...] = jnp.sum(dm * (mv * _rstd(mv)), axis=0, keepdims=True)

    return pl.pallas_call(
        body,
        name="mem_kv_bwd",
        out_shape=[jax.ShapeDtypeStruct((D, N), BF16), jax.ShapeDtypeStruct((1, D), F32)],
        compiler_params=_cparams(),
    )(mem, gm, mb, dkv, w_mkv)


def _post_attn_bwd(dx1, fox_o, sb_o, gf, gs, w_out, tm):
    T, D = dx1.shape

    def body(dx_ref, f_ref, s_ref, gf_ref, gs_ref, w_ref, df_ref, ds_ref, dgf_ref, dgs_ref):
        i = pl.program_id(0)

        @pl.when(i == 0)
        def _():
            dgf_ref[...] = jnp.zeros_like(dgf_ref)
            dgs_ref[...] = jnp.zeros_like(dgs_ref)

        dmix = _nt(dx_ref[...].astype(BF16), w_ref[...])
        d, dg = _norm_bwd(f_ref[...], gf_ref[...], dmix[:, :FOX_W])
        df_ref[...] = d
        dgf_ref[...] += dg
        d, dg = _norm_bwd(s_ref[...], gs_ref[...], dmix[:, FOX_W:])
        ds_ref[...] = d
        dgs_ref[...] += dg

    return pl.pallas_call(
        body,
        name="post_attn_bwd",
        grid=(T // tm,),
        in_specs=[
            pl.BlockSpec((tm, D), lambda i: (i, 0)),
            pl.BlockSpec((tm, FOX_W), lambda i: (i, 0)),
            pl.BlockSpec((tm, FOX_W), lambda i: (i, 0)),
            pl.BlockSpec((1, FOX_W), lambda i: (0, 0)),
            pl.BlockSpec((1, FOX_W), lambda i: (0, 0)),
            pl.BlockSpec((D, D), lambda i: (0, 0)),
        ],
        out_specs=[
            pl.BlockSpec((tm, FOX_W), lambda i: (i, 0)),
            pl.BlockSpec((tm, FOX_W), lambda i: (i, 0)),
            pl.BlockSpec((1, FOX_W), lambda i: (0, 0)),
            pl.BlockSpec((1, FOX_W), lambda i: (0, 0)),
        ],
        out_shape=[
            jax.ShapeDtypeStruct((T, FOX_W), F32),
            jax.ShapeDtypeStruct((T, FOX_W), F32),
            jax.ShapeDtypeStruct((1, FOX_W), F32),
            jax.ShapeDtypeStruct((1, FOX_W), F32),
        ],
        compiler_params=_cparams(("arbitrary",)),
    )(dx1, fox_o, sb_o, gf, gs, w_out)


def _sb_bwd(proj, ltot, live, d_o, tq):
    T = proj.shape[0]
    nq = T // tq

    def body(q_ref, k_ref, v_ref, lt_ref, live_ref, do_ref, dq_ref, dk_ref, dv_ref,
             qh_s, doh_s, lt_s, z_s, da_s, ab_s, dzb_s, run_s, runw_s, dq_s, qt_s, dot_s, dkt_s, dvt_s):
        i = pl.program_id(1)

        @pl.when(i == 0)
        def _():
            dkt_s[...] = jnp.zeros_like(dkt_s)
            dvt_s[...] = jnp.zeros_like(dvt_s)

        lane = lax.broadcasted_iota(jnp.int32, (1, 128), 1)
        row = lax.broadcasted_iota(jnp.int32, (tq, tq), 0)
        col = lax.broadcasted_iota(jnp.int32, (tq, tq), 1)
        strict = col < row
        upto = jnp.where(row <= col, 1.0, 0.0).astype(BF16)
        before = jnp.where(row < col, 1.0, 0.0).astype(BF16)
        q = q_ref[...]
        dov = do_ref[...]
        for hh in range(2):
            qh, hmask = _head_q(q, hh, lane)
            qh_s[hh] = -qh
            doh_s[hh] = jnp.where(hmask, dov, 0.0).astype(BF16)
            lt_s[hh] = jnp.broadcast_to(_lanes_to_rows(lt_ref[hh], row == col), (tq, 128))
        qt_s[...] = (q.astype(F32) * -(HEAD_DIM ** -0.5)).T.astype(BF16)
        dot_s[...] = dov.astype(F32).T.astype(BF16)
        run_s[...] = jnp.zeros_like(run_s)
        runw_s[...] = jnp.zeros_like(runw_s)
        dq_s[...] = jnp.zeros_like(dq_s)

        at = lax.broadcasted_iota(jnp.int32, (1, nq), 1)
        count = jnp.sum(jnp.where(at == i, live_ref[0], 0.0), axis=1, keepdims=True)[0, 0].astype(jnp.int32)
        n_live = jnp.clip(count, 1, i + 1)
        oldest = i + 1 - n_live

        def rows(t):
            return pl.ds(pl.multiple_of((oldest + t) * tq, tq), tq)

        def stage_a(t, slot):
            k = k_ref[rows(t), :]
            v = v_ref[rows(t), :]
            for hh in range(2):
                z_s[slot, hh] = _nt(qh_s[hh], k)
                da_s[slot, hh] = _nt(doh_s[hh], v)

        def stage_b(t, slot, diag):
            for hh in range(2):
                L, g = _sb_logs(z_s[slot, hh], strict if diag else None)
                upto_s = _split_dot(L, upto, SB_SUM_TERMS)
                run = run_s[hh]
                arg = (g + _lanes2(lt_s[hh] - run)) - upto_s
                if diag:
                    arg = jnp.where(strict, arg, NEG)
                a = jnp.exp(arg)
                w = a * da_s[slot, hh]
                w_before = _split_dot(w, before, SB_SUM_TERMS)
                run_w = runw_s[hh]
                d_keep = w_before + _lanes2(run_w)
                beta = jnp.exp(g)
                ndz = beta * (w + d_keep) - w
                if diag:
                    ndz = jnp.where(strict, ndz, 0.0)
                dzb_s[slot, hh] = ndz.astype(BF16)
                ab_s[slot, hh] = a.astype(BF16)
                run_s[hh] = run + jnp.broadcast_to(upto_s[:, tq - 1:tq], (tq, 128))
                runw_s[hh] = run_w + jnp.broadcast_to(w_before[:, tq - 1:tq] + w[:, tq - 1:tq], (tq, 128))

        def stage_c(t, slot):
            k = k_ref[rows(t), :]
            for hh in range(2):
                dzb = dzb_s[slot, hh]
                dq_s[hh] += _nn(dzb, k)
                dims = pl.ds(HEAD_DIM * hh, HEAD_DIM)
                dkt_s[oldest + t, dims, :] += _nn(qt_s[dims, :], dzb)
                dvt_s[oldest + t, dims, :] += _nn(dot_s[dims, :], ab_s[slot, hh])

        _pipeline3(n_live, stage_a, stage_b, stage_c, True)
        dq_ref[...] = (jnp.where(lane < HEAD_DIM, dq_s[0], dq_s[1]) * -(HEAD_DIM ** -0.5)).astype(BF16)

        @pl.when(i == nq - 1)
        def _():
            def flush(n, carry):
                keys = pl.ds(pl.multiple_of(n * tq, tq), tq)
                dk_ref[keys, :] = dkt_s[n].T
                dv_ref[keys, :] = dvt_s[n].T
                return carry

            lax.fori_loop(0, nq, flush, 0)

    return pl.pallas_call(
        body,
        name="sb_bwd",
        grid=(4, nq),
        in_specs=[
            pl.BlockSpec((tq, 128), lambda p, i: (i, 12 + p)),
            pl.BlockSpec((T, 128), lambda p, i: (0, 16 + p)),
            pl.BlockSpec((T, 128), lambda p, i: (0, 20 + p)),
            pl.BlockSpec((2, 1, tq), lambda p, i: (p, 0, i)),
            pl.BlockSpec((1, 1, nq), lambda p, i: (p, 0, 0)),
            pl.BlockSpec((tq, 128), lambda p, i: (i, p)),
        ],
        out_specs=[
            pl.BlockSpec((tq, 128), lambda p, i: (i, p)),
            pl.BlockSpec((T, 128), lambda p, i: (0, p)),
            pl.BlockSpec((T, 128), lambda p, i: (0, p)),
        ],
        out_shape=[
            jax.ShapeDtypeStruct((T, FOX_W), BF16),
            jax.ShapeDtypeStruct((T, FOX_W), F32),
            jax.ShapeDtypeStruct((T, FOX_W), F32),
        ],
        scratch_shapes=[
            pltpu.VMEM((2, tq, 128), BF16),
            pltpu.VMEM((2, tq, 128), BF16),
            pltpu.VMEM((2, tq, 128), F32),
            pltpu.VMEM((2, 2, tq, tq), F32),
            pltpu.VMEM((2, 2, tq, tq), F32),
            pltpu.VMEM((2, 2, tq, tq), BF16),
            pltpu.VMEM((2, 2, tq, tq), BF16),
            pltpu.VMEM((2, tq, 128), F32),
            pltpu.VMEM((2, tq, 128), F32),
            pltpu.VMEM((2, tq, 128), F32),
            pltpu.VMEM((128, tq), BF16),
            pltpu.VMEM((128, tq), BF16),
            pltpu.VMEM((nq, 128, tq), F32),
            pltpu.VMEM((nq, 128, tq), F32),
        ],
        compiler_params=_cparams(("arbitrary", "arbitrary")),
    )(proj, proj, proj, ltot, live, d_o)


def _fox_bwd(proj, c_col, c_row, c_ends, lse, d_o, o, tq, scatter=()):
    T = proj.shape[0]
    nq = T // tq
    ns = len(scatter)

    def body(*refs):
        q_ref, k_ref, v_ref, cq_ref, ck_ref, cke_ref, lse_ref, do_ref, o_ref = refs[:9]
        dq_ref, dk_ref, dv_ref, dck_ref, dcq_ref = refs[9 + ns:14 + ns]
        (qh_s, doh_s, delta_s, shift_s, z_s, dp_s, pb_s, dsb_s, rs_s, dq_s,
         kn_s, qt_s, dot_s, dkt_s, dvt_s, kt_s) = refs[14 + 2 * ns:30 + 2 * ns]
        i = pl.program_id(1)
        if ns:
            pair = pl.program_id(0)
            finish = _ride_along(_Scatter(refs[9:9 + ns], refs[14 + ns:14 + 2 * ns], *refs[30 + 2 * ns:]),
                                 (pair == 0) & (i == 0), None, (pair == 3) & (i == nq - 1))
        lane = lax.broadcasted_iota(jnp.int32, (1, 128), 1)

        @pl.when(i == 0)
        def _():
            dkt_s[...] = jnp.zeros_like(dkt_s)
            dvt_s[...] = jnp.zeros_like(dvt_s)
            dck_ref[...] = jnp.zeros_like(dck_ref)
            _fox_key_norms(k_ref, kn_s, lane)

            def turn(n, carry):
                kt_s[n] = k_ref[pl.ds(pl.multiple_of(n * tq, tq), tq), :].astype(F32).T.astype(BF16)
                return carry

            lax.fori_loop(0, nq, turn, 0)

        row = lax.broadcasted_iota(jnp.int32, (tq, tq), 0)
        col = lax.broadcasted_iota(jnp.int32, (tq, tq), 1)
        q = q_ref[...]
        dov = do_ref[...]
        ov = o_ref[...]
        qt_s[...] = (q.astype(F32) * (HEAD_DIM ** -0.5)).T.astype(BF16)
        dot_s[...] = dov.astype(F32).T.astype(BF16)
        for hh in range(2):
            qh, hmask = _head_q(q, hh, lane)
            dohb = jnp.where(hmask, dov, 0.0).astype(BF16)
            qh_s[hh] = qh
            doh_s[hh] = dohb
            delta_s[hh] = jnp.broadcast_to(jnp.sum(dohb.astype(F32) * ov, axis=1, keepdims=True), (tq, 128))
            shift_s[hh] = jnp.broadcast_to(_lanes_to_rows(cq_ref[hh] - lse_ref[hh], row == col), (tq, 128))
        rs_s[...] = jnp.zeros_like(rs_s)
        dq_s[...] = jnp.zeros_like(dq_s)

        def rows(t):
            return pl.ds(pl.multiple_of((i - t) * tq, tq), tq)

        def stage_a(t, slot):
            k = k_ref[rows(t), :]
            v = v_ref[rows(t), :]
            for hh in range(2):
                z_s[slot, hh] = _nt(qh_s[hh], k)
                dp_s[slot, hh] = _nt(doh_s[hh], v)

        def stage_b(t, slot, diag):
            for hh in range(2):
                s = z_s[slot, hh] + _lanes2(shift_s[hh]) - ck_ref[hh, :, rows(t)]
                if diag:
                    s = jnp.where(col <= row, s, NEG)
                p = jnp.exp(s)
                ds = p * (dp_s[slot, hh] - _lanes2(delta_s[hh]))
                pb_s[slot, hh] = p.astype(BF16)
                dsb_s[slot, hh] = ds.astype(BF16)
                dck_ref[hh, :, rows(t)] += jnp.sum(ds, axis=0, keepdims=True)
                rs_s[hh] += jnp.sum(ds, axis=1, keepdims=True)

        def stage_c(t, slot):
            for hh in range(2):
                dsb = dsb_s[slot, hh]
                dims = pl.ds(HEAD_DIM * hh, HEAD_DIM)
                dq_s[dims, :] += _nt(kt_s[i - t, dims, :], dsb)
                dkt_s[i - t, dims, :] += _nn(qt_s[dims, :], dsb)
                dvt_s[i - t, dims, :] += _nn(dot_s[dims, :], pb_s[slot, hh])

        _pipeline3(_fox_live_blocks(i, qh_s, kn_s, cq_ref, cke_ref), stage_a, stage_b, stage_c, False,
                   a_first=True)

        @pl.when(i == nq - 1)
        def _():
            def flush(n, carry):
                keys = pl.ds(pl.multiple_of(n * tq, tq), tq)
                dk_ref[keys, :] = dkt_s[n].T
                dv_ref[keys, :] = dvt_s[n].T
                return carry

            lax.fori_loop(0, nq, flush, 0)
        dcq_ref[0] = _rows_to_lanes(rs_s[0], row == col)
        dcq_ref[1] = _rows_to_lanes(rs_s[1], row == col)
        dq_ref[...] = (dq_s[...].T * (HEAD_DIM ** -0.5)).astype(BF16)
        if ns:
            finish()

    res = pl.pallas_call(
        body,
        name="fox_bwd",
        grid=(4, nq),
        in_specs=[
            pl.BlockSpec((tq, 128), lambda p, i: (i, p)),
            pl.BlockSpec((T, 128), lambda p, i: (0, 4 + p)),
            pl.BlockSpec((T, 128), lambda p, i: (0, 8 + p)),
            pl.BlockSpec((2, 1, tq), lambda p, i: (p, 0, i)),
            pl.BlockSpec((2, 1, T), lambda p, i: (p, 0, 0)),
            pl.BlockSpec((2, 1, nq), lambda p, i: (p, 0, 0)),
            pl.BlockSpec((2, 1, tq), lambda p, i: (p, 0, i)),
            pl.BlockSpec((tq, 128), lambda p, i: (i, p)),
            pl.BlockSpec((tq, 128), lambda p, i: (i, p)),
        ] + [_ANY] * ns,
        out_specs=[
            pl.BlockSpec((tq, 128), lambda p, i: (i, p)),
            pl.BlockSpec((T, 128), lambda p, i: (0, p)),
            pl.BlockSpec((T, 128), lambda p, i: (0, p)),
            pl.BlockSpec((2, 1, T), lambda p, i: (p, 0, 0)),
            pl.BlockSpec((2, 1, tq), lambda p, i: (p, 0, i)),
        ] + [_ANY] * ns,
        out_shape=[
            jax.ShapeDtypeStruct((T, FOX_W), BF16),
            jax.ShapeDtypeStruct((T, FOX_W), F32),
            jax.ShapeDtypeStruct((T, FOX_W), F32),
            jax.ShapeDtypeStruct((N_FOX, 1, T), F32),
            jax.ShapeDtypeStruct((N_FOX, 1, T), F32),
        ] + [jax.ShapeDtypeStruct(b.shape, b.dtype) for b in scatter],
        scratch_shapes=[
            pltpu.VMEM((2, tq, 128), BF16),
            pltpu.VMEM((2, tq, 128), BF16),
            pltpu.VMEM((2, tq, 128), F32),
            pltpu.VMEM((2, tq, 128), F32),
            pltpu.VMEM((2, 2, tq, tq), F32),
            pltpu.VMEM((2, 2, tq, tq), F32),
            pltpu.VMEM((2, 2, tq, tq), BF16),
            pltpu.VMEM((2, 2, tq, tq), BF16),
            pltpu.VMEM((2, tq, 128), F32),
            pltpu.VMEM((128, tq), F32),
            pltpu.VMEM((2, 8, 128), F32),
            pltpu.VMEM((128, tq), BF16),
            pltpu.VMEM((128, tq), BF16),
            pltpu.VMEM((nq, 128, tq), F32),
            pltpu.VMEM((nq, 128, tq), F32),
            pltpu.VMEM((nq, 128, tq), BF16),
        ] + (_comm_sems(ns) if ns else []),
        compiler_params=_cparams(("arbitrary", "arbitrary")),
    )(proj, proj, proj, c_col, c_row, c_ends, lse, d_o, o, *scatter)
    res = list(res)
    return (*res[:5], res[5:])


def _forget_bwd(dcq, dck, xf, h1, tc):
    H, T = xf.shape
    D = h1.shape[1]
    nc = T // tc

    def body(dcq_ref, dck_ref, xf_ref, h_ref, dxf_ref, db_ref, dwf_ref):
        row = lax.broadcasted_iota(jnp.int32, (tc, tc), 0)
        col = lax.broadcasted_iota(jnp.int32, (tc, tc), 1)
        from_here = jnp.where(row >= col, 1.0, 0.0).astype(BF16)

        def chunk(n, carry):
            run, db, dwf = carry
            cs = pl.multiple_of((nc - 1 - n) * tc, tc)
            dc = dcq_ref[:, pl.ds(cs, tc)] - dck_ref[:, pl.ds(cs, tc)]
            dlogf = _split_dot(dc, from_here, 3) + run
            xfv = xf_ref[:, pl.ds(cs, tc)]
            dxf = dlogf * jax.nn.sigmoid(-xfv)
            dxf_ref[:, pl.ds(cs, tc)] = dxf
            dwf = dwf + _nn(dxf.astype(BF16), h_ref[pl.ds(cs, tc), :])
            return dlogf[:, 0:1], db + jnp.sum(dxf, axis=1, keepdims=True), dwf

        zero = jnp.zeros((H, 1), F32)
        _, db, dwf = lax.fori_loop(0, nc, chunk, (zero, zero, jnp.zeros((H, D), F32)))
        db_ref[...] = db
        dwf_ref[...] = dwf

    return pl.pallas_call(
        body,
        name="forget_bwd",
        out_shape=[jax.ShapeDtypeStruct((H, T), F32), jax.ShapeDtypeStruct((H, 1), F32),
                   jax.ShapeDtypeStruct((H, D), F32)],
        compiler_params=_cparams(),
    )(dcq, dck, xf, h1)


def _inproj_bwd(pieces, dxf_t, w_in, w_f_t, x, g1, dx1, tm, scatter=()):
    T, D = x.shape
    N = w_in.shape[1]
    ns = len(scatter)
    nt = T // tm
    npc = len(pieces)

    def body(*refs):
        pc_refs = refs[:npc]
        dxf_ref, w_ref, wf_ref, x_ref, g_ref, dx1_ref = refs[npc:npc + 6]
        base = npc + 6
        dx_ref, dg_ref = refs[base + ns:base + 2 + ns]
        i = pl.program_id(0)
        if ns:
            exchange = _Scatter(refs[base:base + ns], refs[base + 2 + ns:base + 2 + 2 * ns],
                                *refs[base + 2 + 2 * ns:])

            @pl.when(i == 0)
            def _():
                exchange.start()

        @pl.when(i == 0)
        def _():
            dg_ref[...] = jnp.zeros_like(dg_ref)

        dh = _nn(dxf_ref[...], wf_ref[...].astype(F32))
        for k, pc_ref in enumerate(pc_refs):
            dh = dh + _nt(pc_ref[...].astype(BF16), w_ref[:, k * FOX_W:(k + 1) * FOX_W])
        dx, dg = _norm_bwd(x_ref[...], g_ref[...], dh)
        dx_ref[...] = dx1_ref[...] + dx
        dg_ref[...] += dg
        if ns:
            @pl.when(i == nt - 1)
            def _():
                exchange.finish()

    res = pl.pallas_call(
        body,
        name="inproj_bwd",
        grid=(nt,),
        in_specs=[pl.BlockSpec((tm, FOX_W), lambda i: (i, 0))] * npc + [
            pl.BlockSpec((tm, N_FOX), lambda i: (i, 0)),
            pl.BlockSpec((D, N), lambda i: (0, 0)),
            pl.BlockSpec((N_FOX, D), lambda i: (0, 0)),
            pl.BlockSpec((tm, D), lambda i: (i, 0)),
            pl.BlockSpec((1, D), lambda i: (0, 0)),
            pl.BlockSpec((tm, D), lambda i: (i, 0)),
        ] + [_ANY] * ns,
        out_specs=[
            pl.BlockSpec((tm, D), lambda i: (i, 0)),
            pl.BlockSpec((1, D), lambda i: (0, 0)),
        ] + [_ANY] * ns,
        out_shape=[jax.ShapeDtypeStruct((T, D), F32), jax.ShapeDtypeStruct((1, D), F32)]
        + [jax.ShapeDtypeStruct(b.shape, b.dtype) for b in scatter],
        scratch_shapes=_comm_sems(ns) if ns else [],
        compiler_params=_cparams(("arbitrary",)),
    )(*pieces, dxf_t, w_in, w_f_t, x, g1, dx1, *scatter)
    res = list(res)
    return res[0], res[1], res[2:]


def _dw_in(h1, pieces, name):
    T, K = h1.shape
    bt = min(T, 512)
    nt = T // bt
    npc = len(pieces)

    def body(*refs):
        a_ref = refs[0]
        pc_refs = refs[1:1 + npc]
        o_ref, acc_ref = refs[1 + npc:]
        t = pl.program_id(0)

        @pl.when(t == 0)
        def _():
            acc_ref[...] = jnp.zeros_like(acc_ref)

        a = a_ref[...]
        for k, pc_ref in enumerate(pc_refs):
            acc_ref[:, k * FOX_W:(k + 1) * FOX_W] += _tn(a, pc_ref[...].astype(BF16))

        @pl.when(t == nt - 1)
        def _():
            o_ref[...] = acc_ref[...].astype(BF16)

    return pl.pallas_call(
        body,
        name=name,
        grid=(nt,),
        in_specs=[pl.BlockSpec((bt, K), lambda t: (t, 0))] + [pl.BlockSpec((bt, FOX_W), lambda t: (t, 0))] * npc,
        out_specs=pl.BlockSpec((K, npc * FOX_W), lambda t: (0, 0)),
        out_shape=jax.ShapeDtypeStruct((K, npc * FOX_W), BF16),
        scratch_shapes=[pltpu.VMEM((K, npc * FOX_W), F32)],
        compiler_params=_cparams(("arbitrary",)),
    )(h1, *pieces)


def _matmul_tn(a, b, name, cast_b=False):
    T, K = a.shape
    N = b.shape[1]
    bt = min(T, 512)
    bk = _tile_div(K, 1536)
    bn = _tile_div(N, 1536)
    nt = T // bt

    def body(a_ref, b_ref, o_ref, acc_ref):
        t = pl.program_id(2)

        @pl.when(t == 0)
        def _():
            acc_ref[...] = jnp.zeros_like(acc_ref)

        bv = b_ref[...]
        if cast_b:
            bv = bv.astype(BF16)
        acc_ref[...] += _tn(a_ref[...], bv)

        @pl.when(t == nt - 1)
        def _():
            o_ref[...] = acc_ref[...].astype(BF16)

    return pl.pallas_call(
        body,
        name=name,
        grid=(K // bk, N // bn, nt),
        in_specs=[
            pl.BlockSpec((bt, bk), lambda k, n, t: (t, k)),
            pl.BlockSpec((bt, bn), lambda k, n, t: (t, n)),
        ],
        out_specs=pl.BlockSpec((bk, bn), lambda k, n, t: (k, n)),
        out_shape=jax.ShapeDtypeStruct((K, N), BF16),
        scratch_shapes=[pltpu.VMEM((bk, bn), F32)],
        compiler_params=_cparams(("arbitrary", "arbitrary", "arbitrary")),
    )(a, b)


def _local_step(x, mem, target, p, tm, tq, late=None):
    T, D = x.shape
    w_in = p["w_in"]
    w_qkv = w_in[:, :QKV_W]
    w_f_t = w_in[:, QKV_W:].T
    b_f = p["b_forget"].reshape(N_FOX, 1)

    proj, h1, xf, c = _inproj_fwd(x, p["attn_norm_g"], w_qkv, w_f_t, b_f, tm)
    c_col = c.reshape(N_FOX, 1, T)
    c_row = c.reshape(N_FOX, 1, T)
    c_ends = c[:, tq - 1::tq].reshape(N_FOX, 1, T // tq)
    fox_o, lse, gathered = _fox_fwd(proj, c_col, c_row, c_ends, tq, gather=[late[n] for n in _LATE] if late else ())
    if late:
        p = dict(p, **{n: _gathered_full(n, gv) for n, gv in zip(_LATE, gathered)})
    sb_o, sb_ltot, sb_live = _sb_fwd(proj, tq)
    x1, mixed = _post_attn_fwd(fox_o, sb_o, p["fox_out_g"], p["sb_out_g"], p["w_out"], x, tm)
    mb, kv = _mem_kv_fwd(mem, p["mem_norm_g"], p["w_mkv"])
    x2, h2, qb, om = _xattn_fwd(x1, p["xattn_norm_g"], p["w_mq"], kv, p["w_mo"], tm)
    tf = 2 * tm if T % (2 * tm) == 0 else tm
    x3, h3, ug, uv, yg, yv, a = _ffn_fwd(
        x2, p["ffn_norm_g"], p["w_up"], p["conv_w"], p["conv_b"], p["w_down"], tf)
    dx3, loss_blk, d_final_g = _loss_head(x3, p["final_norm_g"], target, tm)

    g = {"final_norm_g": d_final_g}
    dx2, du_g, du_v, g["ffn_norm_g"], dc_g, dc_v = _ffn_bwd(
        dx3, x2, p["ffn_norm_g"], ug, uv, yg, yv, p["conv_w"], p["w_down"], p["w_up"], tf)
    g["w_down"] = _matmul_tn(a, dx3, "dw_down", cast_b=True)
    g["w_up"] = jnp.concatenate([_matmul_tn(h3, du_g, "dw_up_gate"), _matmul_tn(h3, du_v, "dw_up_val")], axis=1)
    dconv = jnp.concatenate([dc_g, dc_v], axis=1)
    g["conv_w"] = dconv[0:3]
    g["conv_b"] = dconv[3:4]
    dx1, dq_m, dkv, g["xattn_norm_g"] = _xattn_bwd(dx2, x1, p["xattn_norm_g"], qb, kv, p["w_mo"], p["w_mq"], tm)
    g["w_mo"] = _matmul_tn(om, dx2, "dw_mo", cast_b=True)
    g["w_mq"] = _matmul_tn(h2, dq_m, "dw_mq")
    g["w_mkv"], g["mem_norm_g"] = _mem_kv_bwd(mem, p["mem_norm_g"], mb, dkv, p["w_mkv"])
    d_fox, d_sb, g["fox_out_g"], g["sb_out_g"] = _post_attn_bwd(
        dx1, fox_o, sb_o, p["fox_out_g"], p["sb_out_g"], p["w_out"], tm)
    g["w_out"] = _matmul_tn(mixed, dx1, "dw_out", cast_b=True)
    dq_s, dk_s, dv_s = _sb_bwd(proj, sb_ltot, sb_live, d_sb, tq)
    dq_f, dk_f, dv_f, dck, dcq, parts = _fox_bwd(
        proj, c_col, c_row, c_ends, lse, d_fox, fox_o, tq,
        scatter=[_grad_blocks(n, g[n]) for n in _LATE] if late else ())
    if late:
        g["parts"] = dict(zip(_LATE, parts))
    dxf, db, dwf_t = _forget_bwd(dcq.reshape(N_FOX, T), dck.reshape(N_FOX, T), xf, h1, min(T, 512))
    g["b_forget"] = db.reshape(1, N_FOX)
    pieces = [dq_f, dk_f, dv_f, dq_s, dk_s, dv_s]
    g["w_in"] = jnp.concatenate([_dw_in(h1, pieces, "dw_in"), dwf_t.T.astype(BF16)], axis=1)
    grad_x, g["attn_norm_g"], parts = _inproj_bwd(
        pieces, dxf.T, w_in, w_f_t, x, p["attn_norm_g"], dx1, tm,
        scatter=[_grad_blocks("w_in", g["w_in"])] if late else ())
    if late:
        (g["parts"]["w_in"],) = parts
    return loss_blk, grad_x, g


def _mesh_pos():
    return lax.axis_index("x"), lax.axis_index("y"), lax.axis_index("c")


def _flip(pos, k):
    return tuple(1 - v if (k >> b) & 1 else v for v, b in zip(pos, (2, 1, 0)))


def _slot(pos):
    return 4 * pos[0] + 2 * pos[1] + pos[2]


_CHIPS = (4, 2, 6)


def _comm_sems(n):
    return [pltpu.SemaphoreType.DMA((7 * n,)), pltpu.SemaphoreType.DMA((7 * n,)), pltpu.SemaphoreType.DMA((n,))]


class _Gather:
    def __init__(self, ins, outs, send_sems, recv_sems, local_sems):
        self.ins, self.outs, self.n = ins, outs, len(ins)
        self.send_sems, self.recv_sems, self.local_sems = send_sems, recv_sems, local_sems
        self.me = _mesh_pos()
        self.sibling = _flip(self.me, 1)

    def _copy(self, a, kk, block, to, src=None):
        rows = self.outs[a].at[_slot(block)]
        return pltpu.make_async_remote_copy(
            src_ref=rows if src is None else src, dst_ref=rows,
            send_sem=self.send_sems.at[7 * a + kk], recv_sem=self.recv_sems.at[7 * a + kk],
            device_id=to, device_id_type=MESH)

    def _mine(self):
        return [pltpu.make_async_copy(self.ins[a], self.outs[a].at[_slot(self.me)], self.local_sems.at[a])
                for a in range(self.n)]

    def _first(self):
        out = []
        for a in range(self.n):
            out.append(self._copy(a, 0, self.me, self.sibling, src=self.ins[a]))
            out += [self._copy(a, 1 + j, self.me, _flip(self.me, k), src=self.ins[a]) for j, k in enumerate(_CHIPS)]
        return out

    def _passed(self):
        return [self._copy(a, 4 + j, _flip(self.me, k), self.sibling)
                for j, k in enumerate(_CHIPS) for a in range(self.n)]

    def start(self):
        for cp in self._mine() + self._first():
            cp.start()

    def forward(self):
        for j, k in enumerate(_CHIPS):
            for a in range(self.n):
                self._copy(a, 1 + j, _flip(self.me, k), self.me).wait_recv()
                self._copy(a, 4 + j, _flip(self.me, k), self.sibling).start()

    def finish(self):
        for a in range(self.n):
            self._copy(a, 0, self.sibling, self.me).wait_recv()
            for j, k in enumerate(_CHIPS):
                self._copy(a, 4 + j, _flip(self.sibling, k), self.me).wait_recv()
        for cp in self._first() + self._passed():
            cp.wait_send()
        for cp in self._mine():
            cp.wait()


class _Scatter:
    def __init__(self, ins, outs, send_sems, recv_sems, local_sems):
        self.ins, self.outs, self.n = ins, outs, len(ins)
        self.send_sems, self.recv_sems, self.local_sems = send_sems, recv_sems, local_sems
        self.me = _mesh_pos()

    def _copy(self, a, k, landed=False):
        peer = _flip(self.me, k)
        return pltpu.make_async_remote_copy(
            src_ref=self.ins[a].at[_slot(peer)], dst_ref=self.outs[a].at[_slot(peer if landed else self.me)],
            send_sem=self.send_sems.at[7 * a + k - 1], recv_sem=self.recv_sems.at[7 * a + k - 1],
            device_id=peer, device_id_type=MESH)

    def _mine(self):
        s = _slot(self.me)
        return [pltpu.make_async_copy(self.ins[a].at[s], self.outs[a].at[s], self.local_sems.at[a])
                for a in range(self.n)]

    def start(self):
        for cp in self._mine() + [self._copy(a, k) for k in range(1, 8) for a in range(self.n)]:
            cp.start()

    def finish(self):
        for k in range(1, 8):
            for a in range(self.n):
                self._copy(a, k, landed=True).wait_recv()
        for k in range(1, 8):
            for a in range(self.n):
                self._copy(a, k).wait_send()
        for cp in self._mine():
            cp.wait()


_ANY = pl.BlockSpec(memory_space=pl.ANY)


def _gathered_shapes(shards):
    return [jax.ShapeDtypeStruct((N_DEV,) + s.shape, s.dtype) for s in shards]


def _all_gather(shards, name):
    n = len(shards)

    def body(*refs):
        g = _Gather(refs[:n], refs[n:2 * n], *refs[2 * n:])
        g.start()
        g.forward()
        g.finish()

    return pl.pallas_call(
        body, name=name, in_specs=[_ANY] * n, out_specs=[_ANY] * n,
        out_shape=_gathered_shapes(shards), scratch_shapes=_comm_sems(n),
    )(*shards)


def _adamw_math(w, g, m, v):
    m2 = ADAM_B1 * m + (1.0 - ADAM_B1) * g
    v2 = ADAM_B2 * v + (1.0 - ADAM_B2) * (g * g)
    m_hat = m2 / (1.0 - ADAM_B1 ** ADAM_STEP)
    v_hat = v2 / (1.0 - ADAM_B2 ** ADAM_STEP)
    delta = -ADAM_LR * (m_hat / (jnp.sqrt(v_hat) + ADAM_EPS) + ADAM_WD * w)
    return delta, m2, v2


def _adamw(w, parts, m, v, name):
    R, C = w.shape
    br = 128 if R % 128 == 0 else R

    def body(w_ref, p_ref, m_ref, v_ref, g_ref, d_ref, nm_ref, nv_ref):
        g = p_ref[0].astype(F32)
        for s in range(1, N_DEV):
            g = g + p_ref[s].astype(F32)
        g_ref[...] = g
        d_ref[...], nm_ref[...], nv_ref[...] = _adamw_math(w_ref[...], g, m_ref[...], v_ref[...])

    spec = pl.BlockSpec((br, C), lambda i: (i, 0))
    return pl.pallas_call(
        body,
        name=name,
        grid=(R // br,),
        in_specs=[spec, pl.BlockSpec((N_DEV, br, C), lambda i: (0, i, 0)), spec, spec],
        out_specs=[spec] * 4,
        out_shape=[jax.ShapeDtypeStruct((R, C), F32)] * 4,
        compiler_params=_cparams(("arbitrary",)),
    )(w, parts, m, v)


_SHARDED = ("w_in", "w_out", "w_mq", "w_mkv", "w_mo", "w_up", "conv_w", "w_down")
_LATE = _SHARDED[1:]
_COL_SHARDED = ("w_in", "w_mkv", "w_up", "conv_w")
_REPLICATED = ("attn_norm_g", "b_forget", "fox_out_g", "sb_out_g", "xattn_norm_g", "mem_norm_g",
               "ffn_norm_g", "conv_b", "final_norm_g")
_WEIGHTS = ("attn_norm_g", "w_in", "b_forget", "fox_out_g", "sb_out_g", "w_out", "xattn_norm_g", "mem_norm_g",
            "w_mq", "w_mkv", "w_mo", "ffn_norm_g", "w_up", "conv_w", "conv_b", "w_down", "final_norm_g")


def _pack_rows(n):
    return -(-n // 128)


def _pack(vals, rows_total):
    parts = []
    for v in vals:
        flat = v.reshape(-1)
        parts.append(jnp.pad(flat, (0, _pack_rows(flat.shape[0]) * 128 - flat.shape[0])))
    flat = jnp.concatenate(parts)
    return jnp.pad(flat, (0, rows_total * 128 - flat.shape[0])).reshape(rows_total, 128)


def _unpack(packed, shapes):
    out = []
    r = 0
    for shp in shapes:
        n = 1
        for d in shp:
            n *= d
        out.append(packed[r:r + _pack_rows(n)].reshape(-1)[:n].reshape(shp))
        r += _pack_rows(n)
    return out


def _gathered_full(name, gathered):
    if name in _COL_SHARDED:
        return jnp.transpose(gathered, (1, 0, 2)).reshape(gathered.shape[1], -1)
    return gathered.reshape(-1, gathered.shape[2])


def _to_blocks(name, full):
    if name in _COL_SHARDED:
        r = full.shape[0]
        return jnp.transpose(full.reshape(r, N_DEV, -1), (1, 0, 2))
    return full.reshape(N_DEV, -1, full.shape[1])


def _grad_blocks(name, full):
    blocks = _to_blocks(name, full)
    return blocks if name == "conv_w" else blocks.astype(BF16)


def _step(args, tm, tq):
    w = {n: args[n] for n in _WEIGHTS}
    mom = {n: args["m_" + n] for n in _WEIGHTS}
    var = {n: args["v_" + n] for n in _WEIGHTS}
    x = args["x"][0]
    mem = args["mem"][0]
    target = args["loss_target"][0]

    def flat2(a):
        return a.reshape(a.shape[-2], a.shape[-1]) if a.ndim == 3 else a.reshape(1, -1)

    shards = {n: flat2(w[n]) if n == "conv_w" else flat2(w[n]).astype(BF16) for n in _SHARDED}
    (w_in_all,) = _all_gather([shards["w_in"]], "gather_w_in")
    p = {"w_in": _gathered_full("w_in", w_in_all)}
    for n in _REPLICATED:
        p[n] = flat2(w[n])

    loss_blk, grad_x, g = _local_step(x, mem, target, p, tm, tq, late={n: shards[n] for n in _LATE})

    parts = g["parts"]
    out = {}
    for n in _SHARDED:
        res = _adamw(flat2(w[n]), parts[n], flat2(mom[n]), flat2(var[n]), "adamw_" + n)
        out[n] = [r.reshape(w[n].shape) for r in res]

    shapes = [w[n].shape for n in _REPLICATED]
    rows = sum(_pack_rows(flat2(w[n]).shape[1]) for n in _REPLICATED) + 1
    rows = -(-rows // 8) * 8
    g_pack = _pack([g[n] for n in _REPLICATED] + [loss_blk[0:1, :]], rows)
    (g_all,) = _all_gather([g_pack], "gather_small")
    res = _adamw(_pack([w[n] for n in _REPLICATED], rows), g_all,
                 _pack([mom[n] for n in _REPLICATED], rows), _pack([var[n] for n in _REPLICATED], rows),
                 "adamw_small")
    n_rows_params = sum(_pack_rows(flat2(w[n]).shape[1]) for n in _REPLICATED)
    loss = res[0][n_rows_params, 0]
    unpacked = [_unpack(r, shapes) for r in res]
    for k, n in enumerate(_REPLICATED):
        out[n] = [unpacked[q][k] for q in range(4)]

    grads = [out[n][0] for n in _WEIGHTS]
    deltas = [out[n][1] for n in _WEIGHTS]
    new_m = [out[n][2] for n in _WEIGHTS]
    new_v = [out[n][3] for n in _WEIGHTS]
    return (loss, grad_x[None], *grads, *deltas, *new_m, *new_v)


def kernel(x, mem, attn_norm_g, w_in, b_forget, fox_out_g, sb_out_g, w_out, xattn_norm_g, mem_norm_g, w_mq, w_mkv, w_mo, ffn_norm_g, w_up, conv_w, conv_b, w_down, final_norm_g, loss_target, m_attn_norm_g, m_w_in, m_b_forget, m_fox_out_g, m_sb_out_g, m_w_out, m_xattn_norm_g, m_mem_norm_g, m_w_mq, m_w_mkv, m_w_mo, m_ffn_norm_g, m_w_up, m_conv_w, m_conv_b, m_w_down, m_final_norm_g, v_attn_norm_g, v_w_in, v_b_forget, v_fox_out_g, v_sb_out_g, v_w_out, v_xattn_norm_g, v_mem_norm_g, v_w_mq, v_w_mkv, v_w_mo, v_ffn_norm_g, v_w_up, v_conv_w, v_conv_b, v_w_down, v_final_norm_g):
    args = dict(locals())
    T = x.shape[1]
    return _step(args, tm=min(T, 512), tq=min(T, 256))
```

```python
import functools

import jax
import jax.numpy as jnp
from jax import lax
from jax.experimental import pallas as pl
from jax.experimental.pallas import tpu as pltpu

F32 = jnp.float32
BF16 = jnp.bfloat16
EPS = 1e-6
NEG = -1e30
LOG2E = 1.4426950408889634

HEAD_DIM = 64
N_FOX = 8
FOX_W = 512
QKV_W = 3072
N_MEM_HEADS = 4
MEM_HD = 256
D_FF = 2816
FF_CHUNK = 256
N_DEV = 8

ADAM_LR = 0.001
ADAM_B1 = 0.9
ADAM_B2 = 0.999
ADAM_EPS = 1e-08
ADAM_WD = 0.01
ADAM_STEP = 10

SB_SUM_TERMS = 1

VMEM_LIMIT = 56 * 1024 * 1024
MESH = pl.DeviceIdType.MESH


def _cparams(sem=None):
    return pltpu.CompilerParams(dimension_semantics=sem, vmem_limit_bytes=VMEM_LIMIT)


def _nt(a, b):
    return lax.dot_general(a, b, (((1,), (1,)), ((), ())), preferred_element_type=F32)


def _tn(a, b):
    return lax.dot_general(a, b, (((0,), (0,)), ((), ())), preferred_element_type=F32)


def _nn(a, b):
    return jnp.dot(a, b, preferred_element_type=F32)


def _split_dot(a, m01, terms):
    out = None
    r = a
    for t in range(terms):
        p = r.astype(BF16)
        d = _nn(p, m01)
        out = d if out is None else out + d
        if t + 1 < terms:
            r = r - p.astype(F32)
    return out


def _rstd(xv):
    return lax.rsqrt(jnp.mean(xv * xv, axis=-1, keepdims=True) + EPS)


def _norm_bwd(xv, g, dh):
    r = _rstd(xv)
    xhat = xv * r
    dxhat = dh * g
    dx = r * (dxhat - xhat * jnp.mean(dxhat * xhat, axis=-1, keepdims=True))
    dg = jnp.sum(dh * xhat, axis=0, keepdims=True)
    return dx, dg


def _tile_div(n, cap):
    best = None
    for d in range(128, min(n, cap) + 1, 128):
        if n % d == 0:
            best = d
    assert best is not None, n
    return best


def _inproj_fwd(x, g1, w_qkv, w_f_t, b_f, tm):
    T, D = x.shape
    N = w_qkv.shape[1]
    H = w_f_t.shape[0]

    def body(x_ref, g_ref, w_ref, wf_ref, b_ref, proj_ref, h_ref, xf_ref, c_ref, carry_ref):
        i = pl.program_id(0)

        @pl.when(i == 0)
        def _():
            carry_ref[...] = jnp.zeros_like(carry_ref)

        xv = x_ref[...]
        h = (xv * _rstd(xv) * g_ref[...]).astype(BF16)
        h_ref[...] = h
        for n0 in range(0, N, 512):
            proj_ref[:, n0:n0 + 512] = _nn(h, w_ref[:, n0:n0 + 512]).astype(BF16)
        xf = _nt(wf_ref[...], h) + b_ref[...]
        xf_ref[...] = xf
        logf = jnp.minimum(xf, 0.0) - jnp.log1p(jnp.exp(-jnp.abs(xf)))
        row = lax.broadcasted_iota(jnp.int32, (tm, tm), 0)
        col = lax.broadcasted_iota(jnp.int32, (tm, tm), 1)
        upper = jnp.where(row <= col, 1.0, 0.0).astype(BF16)
        c = _split_dot(logf, upper, 3) + carry_ref[...]
        c_ref[...] = c
        carry_ref[...] = c[:, tm - 1:tm]

    return pl.pallas_call(
        body,
        name="inproj_fwd",
        grid=(T // tm,),
        in_specs=[
            pl.BlockSpec((tm, D), lambda i: (i, 0)),
            pl.BlockSpec((1, D), lambda i: (0, 0)),
            pl.BlockSpec((D, N), lambda i: (0, 0)),
            pl.BlockSpec((H, D), lambda i: (0, 0)),
            pl.BlockSpec((H, 1), lambda i: (0, 0)),
        ],
        out_specs=[
            pl.BlockSpec((tm, N), lambda i: (i, 0)),
            pl.BlockSpec((tm, D), lambda i: (i, 0)),
            pl.BlockSpec((H, tm), lambda i: (0, i)),
            pl.BlockSpec((H, tm), lambda i: (0, i)),
        ],
        out_shape=[
            jax.ShapeDtypeStruct((T, N), BF16),
            jax.ShapeDtypeStruct((T, D), BF16),
            jax.ShapeDtypeStruct((H, T), F32),
            jax.ShapeDtypeStruct((H, T), F32),
        ],
        scratch_shapes=[pltpu.VMEM((H, 1), F32)],
        compiler_params=_cparams(("arbitrary",)),
    )(x, g1, w_qkv, w_f_t, b_f)


def _head_q(q, hh, lane):
    hmask = (lane >= HEAD_DIM * hh) & (lane < HEAD_DIM * (hh + 1))
    qh = jnp.where(hmask, q.astype(F32), 0.0) * (HEAD_DIM ** -0.5)
    return qh.astype(BF16), hmask


def _pipeline3(n, stage_a, stage_b, stage_c, diag_last, alive=None, a_first=False):
    stage_a(0, 0)
    if diag_last:
        @pl.when(n == 1)
        def _():
            stage_b(0, 0, True)

        @pl.when(n >= 2)
        def _():
            stage_b(0, 0, False)
            stage_a(1, 1)
    else:
        stage_a(jnp.minimum(1, n - 1), 1)
        stage_b(0, 0, True)

    def pair(m, carry):
        t = 2 + 2 * m
        if a_first:
            stage_a(t, 0)
            stage_b(t - 1, 1, False)
            stage_c(t - 2, 0)
            stage_a(t + 1, 1)
            stage_b(t, 0, False)
            stage_c(t - 1, 1)
        else:
            stage_c(t - 2, 0)
            stage_b(t - 1, 1, False)
            stage_a(t, 0)
            stage_c(t - 1, 1)
            stage_b(t, 0, False)
            stage_a(t + 1, 1)
        return carry

    pairs = (n - 2) // 2
    if alive is None:
        lax.fori_loop(0, pairs, pair, 0)
        go_on = True
        done = n
    else:
        def more(state):
            return (state[0] < pairs) & state[1]

        def step(state):
            pair(state[0], 0)
            return state[0] + 1, alive()

        m_end, go_on = lax.while_loop(more, step, (jnp.int32(0), jnp.bool_(True)))
        done = jnp.where(go_on, n, 2 * m_end)
    odd = n % 2 == 1

    @pl.when((n >= 3) & odd & go_on)
    def _():
        stage_a(n - 1, 0)
        stage_c(n - 3, 0)
        stage_b(n - 2, 1, False)
        stage_c(n - 2, 1)
        stage_b(n - 1, 0, diag_last)
        stage_c(n - 1, 0)

    @pl.when((n == 1) & go_on)
    def _():
        stage_c(0, 0)

    @pl.when(jnp.logical_not(odd) & go_on)
    def _():
        stage_c(n - 2, 0)
        stage_b(n - 1, 1, diag_last)
        stage_c(n - 1, 1)

    return done


def _lanes2(x):
    return jnp.concatenate([x, x], axis=1)


def _lanes_to_rows(vec, eye):
    return jnp.sum(jnp.where(eye, jnp.broadcast_to(vec, eye.shape), 0.0), axis=1, keepdims=True)


def _rows_to_lanes(rep, eye):
    return jnp.sum(jnp.where(eye, _lanes2(rep), 0.0), axis=0, keepdims=True)


FOX_DEAD = -110.0


def _fox_key_norms(k_ref, kn_s, lane):
    T = k_ref.shape[0]
    rows = min(T, 512)
    for hh in range(2):
        hmask = (lane >= HEAD_DIM * hh) & (lane < HEAD_DIM * (hh + 1))

        def chunk(n, best, hmask=hmask):
            kf = jnp.where(hmask, k_ref[pl.ds(pl.multiple_of(n * rows, rows), rows), :].astype(F32), 0.0)
            sq = jnp.sum(kf * kf, axis=1, keepdims=True)
            return jnp.maximum(best, jnp.max(sq, axis=0, keepdims=True))

        best = lax.fori_loop(0, T // rows, chunk, jnp.zeros((1, 1), F32))
        kn_s[hh] = jnp.broadcast_to(best, kn_s.shape[1:])


def _fox_live_blocks(i, qh_s, kn_s, cq_ref, cke_ref):
    nq = cke_ref.shape[-1]
    jj = lax.broadcasted_iota(jnp.int32, (1, nq), 1)
    first = None
    for hh in range(2):
        qf = qh_s[hh].astype(F32)
        qn = jnp.max(jnp.sum(qf * qf, axis=1, keepdims=True), axis=0, keepdims=True)
        zb = jnp.sqrt(qn * kn_s[hh][0:1, 0:1]) * 1.001
        bound = (2.0 * zb + cq_ref[hh][:, 0:1]) - cke_ref[hh]
        live = (bound >= FOX_DEAD) & (jj <= i)
        f = jnp.min(jnp.where(live, jj, i).astype(F32), axis=1, keepdims=True)
        first = f if first is None else jnp.minimum(first, f)
    return i + 1 - first[0, 0].astype(jnp.int32)


def _ride_along(exchange, at_start, at_middle, at_end):
    @pl.when(at_start)
    def _():
        exchange.start()

    if at_middle is not None:
        @pl.when(at_middle)
        def _():
            exchange.forward()

    def finish():
        @pl.when(at_end)
        def _():
            exchange.finish()

    return finish


def _fox_fwd(proj, c_col, c_row, c_ends, tq, gather=()):
    T = proj.shape[0]
    assert tq == 256
    nq = T // tq
    ng = len(gather)

    def body(*refs):
        q_ref, k_ref, v_ref, cq_ref, ck_ref, cke_ref = refs[:6]
        o_ref, lse_ref = refs[6 + ng:8 + ng]
        qh_s, cq_s, z_s, p_s, al_s, m_s, acc_s, kn_s = refs[8 + 2 * ng:16 + 2 * ng]
        i = pl.program_id(1)
        if ng:
            pair = pl.program_id(0)
            finish = _ride_along(_Gather(refs[6:6 + ng], refs[8 + ng:8 + 2 * ng], *refs[16 + 2 * ng:]),
                                 (pair == 0) & (i == 0), (pair == 1) & (i == 0), (pair == 3) & (i == nq - 1))
        lane = lax.broadcasted_iota(jnp.int32, (1, 128), 1)
        row = lax.broadcasted_iota(jnp.int32, (tq, tq), 0)
        col = lax.broadcasted_iota(jnp.int32, (tq, tq), 1)

        @pl.when(i == 0)
        def _():
            _fox_key_norms(k_ref, kn_s, lane)

        q = q_ref[...]
        for hh in range(2):
            qh_s[hh] = _head_q(q, hh, lane)[0]
            cq_s[hh] = jnp.broadcast_to(_lanes_to_rows(cq_ref[hh], row == col), (tq, tq))
        m_s[...] = jnp.full(m_s.shape, NEG, F32)
        acc_s[...] = jnp.zeros_like(acc_s)

        def rows(t):
            return pl.ds(pl.multiple_of((i - t) * tq, tq), tq)

        def stage_a(t, slot):
            k = k_ref[rows(t), :]
            for hh in range(2):
                z_s[slot, hh] = _nt(qh_s[hh], k)

        def stage_b(t, slot, diag):
            for hh in range(2):
                s = z_s[slot, hh] + cq_s[hh] - ck_ref[hh, :, rows(t)]
                if diag:
                    s = jnp.where(col <= row, s, NEG)
                m = m_s[hh]
                half = jnp.maximum(s[:, :128], s[:, 128:])
                m_new = jnp.maximum(m, jnp.max(half, axis=1, keepdims=True))
                m_s[hh] = m_new
                al_s[slot, hh] = jnp.exp(m - m_new)
                p_s[slot, hh] = jnp.exp(s - _lanes2(m_new)).astype(BF16)

        def stage_c(t, slot):
            v = v_ref[rows(t), :]
            for hh in range(2):
                own = (lane >= HEAD_DIM * hh) & (lane < HEAD_DIM * (hh + 1))
                acc_s[hh] = (al_s[slot, hh] * acc_s[hh]
                             + _nn(p_s[slot, hh], jnp.where(own, v, 1.0).astype(BF16)))

        _pipeline3(_fox_live_blocks(i, qh_s, kn_s, cq_ref, cke_ref), stage_a, stage_b, stage_c, False)
        halves = []
        for hh in range(2):
            acc = acc_s[hh]
            own = (lane >= HEAD_DIM * hh) & (lane < HEAD_DIM * (hh + 1))
            halves.append(jnp.where(own, pltpu.roll(acc, HEAD_DIM, axis=1), acc))
        l0, l1 = halves
        o_ref[...] = jnp.where(lane < HEAD_DIM, acc_s[0] / l0, acc_s[1] / l1)
        lse_ref[0] = _rows_to_lanes(m_s[0] + jnp.log(l0), row == col)
        lse_ref[1] = _rows_to_lanes(m_s[1] + jnp.log(l1), row == col)
        if ng:
            finish()

    res = pl.pallas_call(
        body,
        name="fox_fwd",
        grid=(4, nq),
        in_specs=[
            pl.BlockSpec((tq, 128), lambda p, i: (i, p)),
            pl.BlockSpec((T, 128), lambda p, i: (0, 4 + p)),
            pl.BlockSpec((T, 128), lambda p, i: (0, 8 + p)),
            pl.BlockSpec((2, 1, tq), lambda p, i: (p, 0, i)),
            pl.BlockSpec((2, 1, T), lambda p, i: (p, 0, 0)),
            pl.BlockSpec((2, 1, nq), lambda p, i: (p, 0, 0)),
        ] + [_ANY] * ng,
        out_specs=[
            pl.BlockSpec((tq, 128), lambda p, i: (i, p)),
            pl.BlockSpec((2, 1, tq), lambda p, i: (p, 0, i)),
        ] + [_ANY] * ng,
        out_shape=[
            jax.ShapeDtypeStruct((T, FOX_W), F32),
            jax.ShapeDtypeStruct((N_FOX, 1, T), F32),
        ] + _gathered_shapes(gather),
        scratch_shapes=[
            pltpu.VMEM((2, tq, 128), BF16),
            pltpu.VMEM((2, tq, tq), F32),
            pltpu.VMEM((2, 2, tq, tq), F32),
            pltpu.VMEM((2, 2, tq, tq), BF16),
            pltpu.VMEM((2, 2, tq, 128), F32),
            pltpu.VMEM((2, tq, 128), F32),
            pltpu.VMEM((2, tq, 128), F32),
            pltpu.VMEM((2, 8, 128), F32),
        ] + (_comm_sems(ng) if ng else []),
        compiler_params=_cparams(("arbitrary", "arbitrary")),
    )(proj, proj, proj, c_col, c_row, c_ends, *gather)
    res = list(res)
    return res[0], res[1], res[2:]


def _sb_logs(zn, strict):
    e = jnp.exp2(jnp.abs(zn) * (-LOG2E))
    L = jnp.minimum(zn, 0.0) - jnp.log(1.0 + e)
    G = L - zn
    if strict is not None:
        L = jnp.where(strict, L, 0.0)
    return L, G


SB_DEAD = -110.0


def _sb_fwd(proj, tq):
    T = proj.shape[0]
    nq = T // tq

    def body(q_ref, k_ref, v_ref, o_ref, ltot_ref, live_ref, qh_s, z_s, g_s, tot_s, run_s, acc_s):
        i = pl.program_id(1)
        lane = lax.broadcasted_iota(jnp.int32, (1, 128), 1)
        row = lax.broadcasted_iota(jnp.int32, (tq, tq), 0)
        col = lax.broadcasted_iota(jnp.int32, (tq, tq), 1)
        strict = col < row
        later = jnp.where(row > col, 1.0, 0.0).astype(BF16)
        q = q_ref[...]
        for hh in range(2):
            qh_s[hh] = -_head_q(q, hh, lane)[0]
        run_s[...] = jnp.zeros_like(run_s)
        acc_s[...] = jnp.zeros_like(acc_s)

        def rows(t):
            return pl.ds(pl.multiple_of((i - t) * tq, tq), tq)

        def stage_a(t, slot):
            k = k_ref[rows(t), :]
            for hh in range(2):
                z_s[slot, hh] = _nt(qh_s[hh], k)

        def stage_b(t, slot, diag):
            for hh in range(2):
                L, g = _sb_logs(z_s[slot, hh], strict if diag else None)
                if diag:
                    g = jnp.where(strict, g, NEG)
                after = _split_dot(L, later, SB_SUM_TERMS)
                g_s[slot, hh] = g + after
                first = L[:, 0:1]
                if SB_SUM_TERMS == 1:
                    first = first.astype(BF16).astype(F32)
                tot_s[slot, hh] = jnp.broadcast_to(after[:, 0:1] + first, (tq, 128))

        def stage_c(t, slot):
            v = v_ref[rows(t), :]
            for hh in range(2):
                run = run_s[hh]
                a = jnp.exp(g_s[slot, hh] + _lanes2(run))
                acc_s[hh] += _nn(a.astype(BF16), v)
                run_s[hh] = run + tot_s[slot, hh]

        def alive():
            return jnp.max(jnp.maximum(run_s[0], run_s[1])) > SB_DEAD

        done = _pipeline3(i + 1, stage_a, stage_b, stage_c, False, alive)
        ltot_ref[0] = _rows_to_lanes(run_s[0], row == col)
        ltot_ref[1] = _rows_to_lanes(run_s[1], row == col)
        o_ref[...] = jnp.where(lane < HEAD_DIM, acc_s[0], acc_s[1])
        at = lax.broadcasted_iota(jnp.int32, (1, nq), 1)

        @pl.when(i == 0)
        def _():
            live_ref[0] = jnp.zeros((1, nq), F32)

        live_ref[0] = jnp.where(at == i, done.astype(F32), live_ref[0])

    return pl.pallas_call(
        body,
        name="sb_fwd",
        grid=(4, nq),
        in_specs=[
            pl.BlockSpec((tq, 128), lambda p, i: (i, 12 + p)),
            pl.BlockSpec((T, 128), lambda p, i: (0, 16 + p)),
            pl.BlockSpec((T, 128), lambda p, i: (0, 20 + p)),
        ],
        out_specs=[
            pl.BlockSpec((tq, 128), lambda p, i: (i, p)),
            pl.BlockSpec((2, 1, tq), lambda p, i: (p, 0, i)),
            pl.BlockSpec((1, 1, nq), lambda p, i: (p, 0, 0)),
        ],
        out_shape=[
            jax.ShapeDtypeStruct((T, FOX_W), F32),
            jax.ShapeDtypeStruct((N_FOX, 1, T), F32),
            jax.ShapeDtypeStruct((N_FOX // 2, 1, nq), F32),
        ],
        scratch_shapes=[
            pltpu.VMEM((2, tq, 128), BF16),
            pltpu.VMEM((2, 2, tq, tq), F32),
            pltpu.VMEM((2, 2, tq, tq), F32),
            pltpu.VMEM((2, 2, tq, 128), F32),
            pltpu.VMEM((2, tq, 128), F32),
            pltpu.VMEM((2, tq, 128), F32),
        ],
        compiler_params=_cparams(("arbitrary", "arbitrary")),
    )(proj, proj, proj)


def _post_attn_fwd(fox_o, sb_o, gf, gs, w_out, x, tm):
    T, D = x.shape

    def body(f_ref, s_ref, gf_ref, gs_ref, w_ref, x_ref, x1_ref, mix_ref):
        f = f_ref[...]
        s = s_ref[...]
        mix_ref[:, :FOX_W] = (f * _rstd(f) * gf_ref[...]).astype(BF16)
        mix_ref[:, FOX_W:] = (s * _rstd(s) * gs_ref[...]).astype(BF16)
        x1_ref[...] = x_ref[...] + _nn(mix_ref[...], w_ref[...])

    return pl.pallas_call(
        body,
        name="post_attn_fwd",
        grid=(T // tm,),
        in_specs=[
            pl.BlockSpec((tm, FOX_W), lambda i: (i, 0)),
            pl.BlockSpec((tm, FOX_W), lambda i: (i, 0)),
            pl.BlockSpec((1, FOX_W), lambda i: (0, 0)),
            pl.BlockSpec((1, FOX_W), lambda i: (0, 0)),
            pl.BlockSpec((D, D), lambda i: (0, 0)),
            pl.BlockSpec((tm, D), lambda i: (i, 0)),
        ],
        out_specs=[
            pl.BlockSpec((tm, D), lambda i: (i, 0)),
            pl.BlockSpec((tm, D), lambda i: (i, 0)),
        ],
        out_shape=[jax.ShapeDtypeStruct((T, D), F32), jax.ShapeDtypeStruct((T, D), BF16)],
        compiler_params=_cparams(("arbitrary",)),
    )(fox_o, sb_o, gf, gs, w_out, x)


def _mem_kv_fwd(mem, gm, w_mkv):
    M, D = mem.shape
    N = w_mkv.shape[1]

    def body(mem_ref, g_ref, w_ref, m_ref, kv_ref):
        mv = mem_ref[...]
        m = (mv * _rstd(mv) * g_ref[...]).astype(BF16)
        m_ref[...] = m
        for n0 in range(0, N, 512):
            kv_ref[:, n0:n0 + 512] = _nn(m, w_ref[:, n0:n0 + 512]).astype(BF16)

    return pl.pallas_call(
        body,
        name="mem_kv_fwd",
        out_shape=[jax.ShapeDtypeStruct((M, D), BF16), jax.ShapeDtypeStruct((M, N), BF16)],
        compiler_params=_cparams(),
    )(mem, gm, w_mkv)


def _xattn_probs(qb, kv, h):
    k = kv[:, h * MEM_HD:(h + 1) * MEM_HD]
    s = _nt(qb[:, h * MEM_HD:(h + 1) * MEM_HD], k) * (MEM_HD ** -0.5)
    s = s - jnp.max(s, axis=1, keepdims=True)
    p = jnp.exp(s)
    return p / jnp.sum(p, axis=1, keepdims=True)


def _xattn_fwd(x1, g2, w_mq, kv, w_mo, tm):
    T, D = x1.shape
    M = kv.shape[0]

    def body(x_ref, g_ref, wq_ref, kv_ref, wo_ref, x2_ref, h_ref, q_ref, om_ref):
        xv = x_ref[...]
        h = (xv * _rstd(xv) * g_ref[...]).astype(BF16)
        h_ref[...] = h
        q_ref[...] = _nn(h, wq_ref[...]).astype(BF16)
        qb = q_ref[...]
        kvv = kv_ref[...]
        for hd in range(N_MEM_HEADS):
            p = _xattn_probs(qb, kvv, hd)
            v = kvv[:, D + hd * MEM_HD:D + (hd + 1) * MEM_HD]
            om_ref[:, hd * MEM_HD:(hd + 1) * MEM_HD] = _nn(p.astype(BF16), v).astype(BF16)
        x2_ref[...] = xv + _nn(om_ref[...], wo_ref[...])

    return pl.pallas_call(
        body,
        name="xattn_fwd",
        grid=(T // tm,),
        in_specs=[
            pl.BlockSpec((tm, D), lambda i: (i, 0)),
            pl.BlockSpec((1, D), lambda i: (0, 0)),
            pl.BlockSpec((D, D), lambda i: (0, 0)),
            pl.BlockSpec((M, 2 * D), lambda i: (0, 0)),
            pl.BlockSpec((D, D), lambda i: (0, 0)),
        ],
        out_specs=[pl.BlockSpec((tm, D), lambda i: (i, 0))] * 4,
        out_shape=[jax.ShapeDtypeStruct((T, D), F32)] + [jax.ShapeDtypeStruct((T, D), BF16)] * 3,
        compiler_params=_cparams(("arbitrary",)),
    )(x1, g2, w_mq, kv, w_mo)


def _conv_taps(ext_ref, tm, back):
    if back:
        return ext_ref[pl.ds(6, tm), :], ext_ref[pl.ds(7, tm), :], ext_ref[pl.ds(8, tm), :]
    return ext_ref[pl.ds(0, tm), :], ext_ref[pl.ds(1, tm), :], ext_ref[pl.ds(2, tm), :]


def _ffn_fwd(x2, g3, w_up, conv_w, conv_b, w_down, tm):
    T, D = x2.shape
    fc = FF_CHUNK
    nj = D_FF // fc

    def body(x_ref, g_ref, wg_ref, wv_ref, cwg_ref, cwv_ref, cbg_ref, cbv_ref, wd_ref,
             x3_ref, h_ref, ug_ref, uv_ref, yg_ref, yv_ref, a_ref, acc_ref, carry_ref, ext_ref):
        i = pl.program_id(0)
        j = pl.program_id(1)

        @pl.when(j == 0)
        def _():
            xv = x_ref[...]
            h_ref[...] = (xv * _rstd(xv) * g_ref[...]).astype(BF16)
            acc_ref[...] = xv

        @pl.when(i == 0)
        def _():
            carry_ref[j] = jnp.zeros((2, 8, fc), F32)

        h = h_ref[...]
        halves = []
        for part, (w_ref, cw_ref, cb_ref, u_ref, y_ref) in enumerate(
                ((wg_ref, cwg_ref, cbg_ref, ug_ref, yg_ref), (wv_ref, cwv_ref, cbv_ref, uv_ref, yv_ref))):
            u = _nn(h, w_ref[...])
            u_ref[...] = u.astype(BF16)
            ext = ext_ref.at[part]
            ext[pl.ds(0, 8), :] = carry_ref[j, part]
            ext[pl.ds(8, tm), :] = u
            carry_ref[j, part] = u[tm - 8:, :]
            u2, u1, u0 = _conv_taps(ext, tm, True)
            cw = cw_ref[...]
            y = cb_ref[...] + cw[0:1] * u2 + cw[1:2] * u1 + cw[2:3] * u0
            y_ref[...] = y.astype(BF16)
            halves.append(y)
        gate, val = halves
        a = (gate * jax.nn.sigmoid(gate) * val).astype(BF16)
        a_ref[...] = a
        acc_ref[...] += _nn(a, wd_ref[...])

        @pl.when(j == nj - 1)
        def _():
            x3_ref[...] = acc_ref[...]

    return pl.pallas_call(
        body,
        name="ffn_fwd",
        grid=(T // tm, nj),
        in_specs=[
            pl.BlockSpec((tm, D), lambda i, j: (i, 0)),
            pl.BlockSpec((1, D), lambda i, j: (0, 0)),
            pl.BlockSpec((D, fc), lambda i, j: (0, j)),
            pl.BlockSpec((D, fc), lambda i, j: (0, nj + j)),
            pl.BlockSpec((3, fc), lambda i, j: (0, j)),
            pl.BlockSpec((3, fc), lambda i, j: (0, nj + j)),
            pl.BlockSpec((1, fc), lambda i, j: (0, j)),
            pl.BlockSpec((1, fc), lambda i, j: (0, nj + j)),
            pl.BlockSpec((fc, D), lambda i, j: (j, 0)),
        ],
        out_specs=[
            pl.BlockSpec((tm, D), lambda i, j: (i, 0)),
            pl.BlockSpec((tm, D), lambda i, j: (i, 0)),
        ] + [pl.BlockSpec((tm, fc), lambda i, j: (i, j))] * 5,
        out_shape=[
            jax.ShapeDtypeStruct((T, D), F32),
            jax.ShapeDtypeStruct((T, D), BF16),
        ] + [jax.ShapeDtypeStruct((T, D_FF), BF16)] * 5,
        scratch_shapes=[
            pltpu.VMEM((tm, D), F32),
            pltpu.VMEM((nj, 2, 8, fc), F32),
            pltpu.VMEM((2, tm + 8, fc), F32),
        ],
        compiler_params=_cparams(("arbitrary", "arbitrary")),
    )(x2, g3, w_up, w_up, conv_w, conv_w, conv_b, conv_b, w_down)


def _loss_head(x3, gfin, target, tm):
    T, D = x3.shape

    def body(x_ref, g_ref, t_ref, dx_ref, loss_ref, dg_ref):
        i = pl.program_id(0)

        @pl.when(i == 0)
        def _():
            loss_ref[...] = jnp.zeros_like(loss_ref)
            dg_ref[...] = jnp.zeros_like(dg_ref)

        xv = x_ref[...]
        g = g_ref[...]
        r = _rstd(xv)
        xhat = xv * r
        err = xhat * g - t_ref[...]
        part = jnp.sum(jnp.sum(err * err, axis=1, keepdims=True), axis=0, keepdims=True) * (0.5 / D)
        loss_ref[...] += jnp.broadcast_to(part, loss_ref.shape)
        dy = err * (1.0 / D)
        dg_ref[...] += jnp.sum(dy * xhat, axis=0, keepdims=True)
        dxhat = dy * g
        dx_ref[...] = r * (dxhat - xhat * jnp.mean(dxhat * xhat, axis=-1, keepdims=True))

    return pl.pallas_call(
        body,
        name="loss_head",
        grid=(T // tm,),
        in_specs=[
            pl.BlockSpec((tm, D), lambda i: (i, 0)),
            pl.BlockSpec((1, D), lambda i: (0, 0)),
            pl.BlockSpec((tm, D), lambda i: (i, 0)),
        ],
        out_specs=[
            pl.BlockSpec((tm, D), lambda i: (i, 0)),
            pl.BlockSpec((8, 128), lambda i: (0, 0)),
            pl.BlockSpec((1, D), lambda i: (0, 0)),
        ],
        out_shape=[
            jax.ShapeDtypeStruct((T, D), F32),
            jax.ShapeDtypeStruct((8, 128), F32),
            jax.ShapeDtypeStruct((1, D), F32),
        ],
        compiler_params=_cparams(("arbitrary",)),
    )(x3, gfin, target)


def _ffn_bwd(dx3, x2, g3, ug, uv, yg, yv, conv_w, w_down, w_up, tm):
    T, D = x2.shape
    fc = FF_CHUNK
    nj = D_FF // fc
    nt = T // tm

    def rev(i):
        return nt - 1 - i

    def body(dx3_ref, x_ref, g_ref, ug_ref, uv_ref, yg_ref, yv_ref, cwg_ref, cwv_ref,
             wd_ref, wug_ref, wuv_ref,
             dx2_ref, dug_ref, duv_ref, dg_ref, dcg_ref, dcv_ref,
             acc_ref, carry_ref, ext_ref):
        i = pl.program_id(0)
        j = pl.program_id(1)
        cols = pl.ds(pl.multiple_of(j * fc, fc), fc)

        @pl.when(j == 0)
        def _():
            acc_ref[...] = jnp.zeros_like(acc_ref)

        @pl.when((i == 0) & (j == 0))
        def _():
            dg_ref[...] = jnp.zeros_like(dg_ref)
            dcg_ref[...] = jnp.zeros_like(dcg_ref)
            dcv_ref[...] = jnp.zeros_like(dcv_ref)

        @pl.when(i == 0)
        def _():
            carry_ref[j] = jnp.zeros((2, 8, fc), F32)

        da = _nt(dx3_ref[...].astype(BF16), wd_ref[...])
        gate = yg_ref[...].astype(F32)
        val = yv_ref[...].astype(F32)
        sig = jax.nn.sigmoid(gate)
        silu = gate * sig
        dys = (da * val * (sig * (1.0 + gate * (1.0 - sig))), da * silu)
        for part, (dy, u_ref, cw_ref, du_ref, wu_ref, dc_ref) in enumerate(
                ((dys[0], ug_ref, cwg_ref, dug_ref, wug_ref, dcg_ref),
                 (dys[1], uv_ref, cwv_ref, duv_ref, wuv_ref, dcv_ref))):
            ext = ext_ref.at[part]
            ext[pl.ds(0, tm), :] = dy
            ext[pl.ds(tm, 8), :] = carry_ref[j, part]
            carry_ref[j, part] = dy[:8, :]
            d0, d1, d2 = _conv_taps(ext, tm, False)
            u = u_ref[...].astype(F32)
            upd = jnp.concatenate([
                jnp.sum(u * d2, axis=0, keepdims=True),
                jnp.sum(u * d1, axis=0, keepdims=True),
                jnp.sum(u * d0, axis=0, keepdims=True),
                jnp.sum(d0, axis=0, keepdims=True),
                jnp.zeros((4, fc), F32)], axis=0)
            dc_ref[:, cols] += upd
            cw = cw_ref[...]
            du = (cw[2:3] * d0 + cw[1:2] * d1 + cw[0:1] * d2).astype(BF16)
            du_ref[...] = du
            acc_ref[...] += _nt(du, wu_ref[...])

        @pl.when(j == nj - 1)
        def _():
            dx, dg = _norm_bwd(x_ref[...], g_ref[...], acc_ref[...])
            dx2_ref[...] = dx3_ref[...] + dx
            dg_ref[...] += dg

    return pl.pallas_call(
        body,
        name="ffn_bwd",
        grid=(nt, nj),
        in_specs=[
            pl.BlockSpec((tm, D), lambda i, j: (rev(i), 0)),
            pl.BlockSpec((tm, D), lambda i, j: (rev(i), 0)),
            pl.BlockSpec((1, D), lambda i, j: (0, 0)),
            pl.BlockSpec((tm, fc), lambda i, j: (rev(i), j)),
            pl.BlockSpec((tm, fc), lambda i, j: (rev(i), j)),
            pl.BlockSpec((tm, fc), lambda i, j: (rev(i), j)),
            pl.BlockSpec((tm, fc), lambda i, j: (rev(i), j)),
            pl.BlockSpec((3, fc), lambda i, j: (0, j)),
            pl.BlockSpec((3, fc), lambda i, j: (0, nj + j)),
            pl.BlockSpec((fc, D), lambda i, j: (j, 0)),
            pl.BlockSpec((D, fc), lambda i, j: (0, j)),
            pl.BlockSpec((D, fc), lambda i, j: (0, nj + j)),
        ],
        out_specs=[
            pl.BlockSpec((tm, D), lambda i, j: (rev(i), 0)),
            pl.BlockSpec((tm, fc), lambda i, j: (rev(i), j)),
            pl.BlockSpec((tm, fc), lambda i, j: (rev(i), j)),
            pl.BlockSpec((1, D), lambda i, j: (0, 0)),
            pl.BlockSpec((8, D_FF), lambda i, j: (0, 0)),
            pl.BlockSpec((8, D_FF), lambda i, j: (0, 0)),
        ],
        out_shape=[
            jax.ShapeDtypeStruct((T, D), F32),
            jax.ShapeDtypeStruct((T, D_FF), BF16),
            jax.ShapeDtypeStruct((T, D_FF), BF16),
            jax.ShapeDtypeStruct((1, D), F32),
            jax.ShapeDtypeStruct((8, D_FF), F32),
            jax.ShapeDtypeStruct((8, D_FF), F32),
        ],
        scratch_shapes=[
            pltpu.VMEM((tm, D), F32),
            pltpu.VMEM((nj, 2, 8, fc), F32),
            pltpu.VMEM((2, tm + 8, fc), F32),
        ],
        compiler_params=_cparams(("arbitrary", "arbitrary")),
    )(dx3, x2, g3, ug, uv, yg, yv, conv_w, conv_w, w_down, w_up, w_up)


def _xattn_bwd(dx2, x1, g2, qb, kv, w_mo, w_mq, tm):
    T, D = x1.shape
    M = kv.shape[0]

    def body(dx2_ref, x_ref, g_ref, q_ref, kv_ref, wo_ref, wq_ref, dx1_ref, dq_ref, dkv_ref, dg_ref):
        i = pl.program_id(0)

        @pl.when(i == 0)
        def _():
            dkv_ref[...] = jnp.zeros_like(dkv_ref)
            dg_ref[...] = jnp.zeros_like(dg_ref)

        dxv = dx2_ref[...]
        dom = _nt(dxv.astype(BF16), wo_ref[...]).astype(BF16)
        qb_ = q_ref[...]
        kvv = kv_ref[...]
        for hd in range(N_MEM_HEADS):
            sl = slice(hd * MEM_HD, (hd + 1) * MEM_HD)
            vsl = slice(D + hd * MEM_HD, D + (hd + 1) * MEM_HD)
            p = _xattn_probs(qb_, kvv, hd)
            dp = _nt(dom[:, sl], kvv[:, vsl])
            ds = (p * (dp - jnp.sum(p * dp, axis=1, keepdims=True)) * (MEM_HD ** -0.5)).astype(BF16)
            dq_ref[:, sl] = _nn(ds, kvv[:, sl]).astype(BF16)
            dkv_ref[:, sl] += _tn(ds, qb_[:, sl])
            dkv_ref[:, vsl] += _tn(p.astype(BF16), dom[:, sl])
        dh = _nt(dq_ref[...], wq_ref[...])
        dx, dg = _norm_bwd(x_ref[...], g_ref[...], dh)
        dx1_ref[...] = dxv + dx
        dg_ref[...] += dg

    return pl.pallas_call(
        body,
        name="xattn_bwd",
        grid=(T // tm,),
        in_specs=[
            pl.BlockSpec((tm, D), lambda i: (i, 0)),
            pl.BlockSpec((tm, D), lambda i: (i, 0)),
            pl.BlockSpec((1, D), lambda i: (0, 0)),
            pl.BlockSpec((tm, D), lambda i: (i, 0)),
            pl.BlockSpec((M, 2 * D), lambda i: (0, 0)),
            pl.BlockSpec((D, D), lambda i: (0, 0)),
            pl.BlockSpec((D, D), lambda i: (0, 0)),
        ],
        out_specs=[
            pl.BlockSpec((tm, D), lambda i: (i, 0)),
            pl.BlockSpec((tm, D), lambda i: (i, 0)),
            pl.BlockSpec((M, 2 * D), lambda i: (0, 0)),
            pl.BlockSpec((1, D), lambda i: (0, 0)),
        ],
        out_shape=[
            jax.ShapeDtypeStruct((T, D), F32),
            jax.ShapeDtypeStruct((T, D), BF16),
            jax.ShapeDtypeStruct((M, 2 * D), F32),
            jax.ShapeDtypeStruct((1, D), F32),
        ],
        compiler_params=_cparams(("arbitrary",)),
    )(dx2, x1, g2, qb, kv, w_mo, w_mq)


def _mem_kv_bwd(mem, gm, mb, dkv, w_mkv):
    M, D = mem.shape
    N = dkv.shape[1]

    def body(mem_ref, g_ref, m_ref, dkv_ref, w_ref, dw_ref, dg_ref):
        dkvb = dkv_ref[...].astype(BF16)
        for n0 in range(0, N, 512):
            dw_ref[:, n0:n0 + 512] = _tn(m_ref[...], dkvb[:, n0:n0 + 512]).astype(BF16)
        dm = _nt(dkvb, w_ref[...])
        mv = mem_ref[...]
        dg_ref[...] = jnp.sum(dm * (mv * _rstd(mv)), axis=0, keepdims=True)

    return pl.pallas_call(
        body,
        name="mem_kv_bwd",
        out_shape=[jax.ShapeDtypeStruct((D, N), BF16), jax.ShapeDtypeStruct((1, D), F32)],
        compiler_params=_cparams(),
    )(mem, gm, mb, dkv, w_mkv)


def _post_attn_bwd(dx1, fox_o, sb_o, gf, gs, w_out, tm):
    T, D = dx1.shape

    def body(dx_ref, f_ref, s_ref, gf_ref, gs_ref, w_ref, df_ref, ds_ref, dgf_ref, dgs_ref):
        i = pl.program_id(0)

        @pl.when(i == 0)
        def _():
            dgf_ref[...] = jnp.zeros_like(dgf_ref)
            dgs_ref[...] = jnp.zeros_like(dgs_ref)

        dmix = _nt(dx_ref[...].astype(BF16), w_ref[...])
        d, dg = _norm_bwd(f_ref[...], gf_ref[...], dmix[:, :FOX_W])
        df_ref[...] = d
        dgf_ref[...] += dg
        d, dg = _norm_bwd(s_ref[...], gs_ref[...], dmix[:, FOX_W:])
        ds_ref[...] = d
        dgs_ref[...] += dg

    return pl.pallas_call(
        body,
        name="post_attn_bwd",
        grid=(T // tm,),
        in_specs=[
            pl.BlockSpec((tm, D), lambda i: (i, 0)),
            pl.BlockSpec((tm, FOX_W), lambda i: (i, 0)),
            pl.BlockSpec((tm, FOX_W), lambda i: (i, 0)),
            pl.BlockSpec((1, FOX_W), lambda i: (0, 0)),
            pl.BlockSpec((1, FOX_W), lambda i: (0, 0)),
            pl.BlockSpec((D, D), lambda i: (0, 0)),
        ],
        out_specs=[
            pl.BlockSpec((tm, FOX_W), lambda i: (i, 0)),
            pl.BlockSpec((tm, FOX_W), lambda i: (i, 0)),
            pl.BlockSpec((1, FOX_W), lambda i: (0, 0)),
            pl.BlockSpec((1, FOX_W), lambda i: (0, 0)),
        ],
        out_shape=[
            jax.ShapeDtypeStruct((T, FOX_W), F32),
            jax.ShapeDtypeStruct((T, FOX_W), F32),
            jax.ShapeDtypeStruct((1, FOX_W), F32),
            jax.ShapeDtypeStruct((1, FOX_W), F32),
        ],
        compiler_params=_cparams(("arbitrary",)),
    )(dx1, fox_o, sb_o, gf, gs, w_out)


def _sb_bwd(proj, ltot, live, d_o, tq):
    T = proj.shape[0]
    nq = T // tq

    def body(q_ref, k_ref, v_ref, lt_ref, live_ref, do_ref, dq_ref, dk_ref, dv_ref,
             qh_s, doh_s, lt_s, z_s, da_s, ab_s, dzb_s, run_s, runw_s, dq_s, qt_s, dot_s, dkt_s, dvt_s):
        i = pl.program_id(1)

        @pl.when(i == 0)
        def _():
            dkt_s[...] = jnp.zeros_like(dkt_s)
            dvt_s[...] = jnp.zeros_like(dvt_s)

        lane = lax.broadcasted_iota(jnp.int32, (1, 128), 1)
        row = lax.broadcasted_iota(jnp.int32, (tq, tq), 0)
        col = lax.broadcasted_iota(jnp.int32, (tq, tq), 1)
        strict = col < row
        upto = jnp.where(row <= col, 1.0, 0.0).astype(BF16)
        before = jnp.where(row < col, 1.0, 0.0).astype(BF16)
        q = q_ref[...]
        dov = do_ref[...]
        for hh in range(2):
            qh, hmask = _head_q(q, hh, lane)
            qh_s[hh] = -qh
            doh_s[hh] = jnp.where(hmask, dov, 0.0).astype(BF16)
            lt_s[hh] = jnp.broadcast_to(_lanes_to_rows(lt_ref[hh], row == col), (tq, 128))
        qt_s[...] = (q.astype(F32) * -(HEAD_DIM ** -0.5)).T.astype(BF16)
        dot_s[...] = dov.astype(F32).T.astype(BF16)
        run_s[...] = jnp.zeros_like(run_s)
        runw_s[...] = jnp.zeros_like(runw_s)
        dq_s[...] = jnp.zeros_like(dq_s)

        at = lax.broadcasted_iota(jnp.int32, (1, nq), 1)
        count = jnp.sum(jnp.where(at == i, live_ref[0], 0.0), axis=1, keepdims=True)[0, 0].astype(jnp.int32)
        n_live = jnp.clip(count, 1, i + 1)
        oldest = i + 1 - n_live

        def rows(t):
            return pl.ds(pl.multiple_of((oldest + t) * tq, tq), tq)

        def stage_a(t, slot):
            k = k_ref[rows(t), :]
            v = v_ref[rows(t), :]
            for hh in range(2):
                z_s[slot, hh] = _nt(qh_s[hh], k)
                da_s[slot, hh] = _nt(doh_s[hh], v)

        def stage_b(t, slot, diag):
            for hh in range(2):
                L, g = _sb_logs(z_s[slot, hh], strict if diag else None)
                upto_s = _split_dot(L, upto, SB_SUM_TERMS)
                run = run_s[hh]
                arg = (g + _lanes2(lt_s[hh] - run)) - upto_s
                if diag:
                    arg = jnp.where(strict, arg, NEG)
                a = jnp.exp(arg)
                w = a * da_s[slot, hh]
                w_before = _split_dot(w, before, SB_SUM_TERMS)
                run_w = runw_s[hh]
                d_keep = w_before + _lanes2(run_w)
                beta = jnp.exp(g)
                ndz = beta * (w + d_keep) - w
                if diag:
                    ndz = jnp.where(strict, ndz, 0.0)
                dzb_s[slot, hh] = ndz.astype(BF16)
                ab_s[slot, hh] = a.astype(BF16)
                run_s[hh] = run + jnp.broadcast_to(upto_s[:, tq - 1:tq], (tq, 128))
                runw_s[hh] = run_w + jnp.broadcast_to(w_before[:, tq - 1:tq] + w[:, tq - 1:tq], (tq, 128))

        def stage_c(t, slot):
            k = k_ref[rows(t), :]
            for hh in range(2):
                dzb = dzb_s[slot, hh]
                dq_s[hh] += _nn(dzb, k)
                dims = pl.ds(HEAD_DIM * hh, HEAD_DIM)
                dkt_s[oldest + t, dims, :] += _nn(qt_s[dims, :], dzb)
                dvt_s[oldest + t, dims, :] += _nn(dot_s[dims, :], ab_s[slot, hh])

        _pipeline3(n_live, stage_a, stage_b, stage_c, True)
        dq_ref[...] = (jnp.where(lane < HEAD_DIM, dq_s[0], dq_s[1]) * -(HEAD_DIM ** -0.5)).astype(BF16)

        @pl.when(i == nq - 1)
        def _():
            def flush(n, carry):
                keys = pl.ds(pl.multiple_of(n * tq, tq), tq)
                dk_ref[keys, :] = dkt_s[n].T
                dv_ref[keys, :] = dvt_s[n].T
                return carry

            lax.fori_loop(0, nq, flush, 0)

    return pl.pallas_call(
        body,
        name="sb_bwd",
        grid=(4, nq),
        in_specs=[
            pl.BlockSpec((tq, 128), lambda p, i: (i, 12 + p)),
            pl.BlockSpec((T, 128), lambda p, i: (0, 16 + p)),
            pl.BlockSpec((T, 128), lambda p, i: (0, 20 + p)),
            pl.BlockSpec((2, 1, tq), lambda p, i: (p, 0, i)),
            pl.BlockSpec((1, 1, nq), lambda p, i: (p, 0, 0)),
            pl.BlockSpec((tq, 128), lambda p, i: (i, p)),
        ],
        out_specs=[
            pl.BlockSpec((tq, 128), lambda p, i: (i, p)),
            pl.BlockSpec((T, 128), lambda p, i: (0, p)),
            pl.BlockSpec((T, 128), lambda p, i: (0, p)),
        ],
        out_shape=[
            jax.ShapeDtypeStruct((T, FOX_W), BF16),
            jax.ShapeDtypeStruct((T, FOX_W), F32),
            jax.ShapeDtypeStruct((T, FOX_W), F32),
        ],
        scratch_shapes=[
            pltpu.VMEM((2, tq, 128), BF16),
            pltpu.VMEM((2, tq, 128), BF16),
            pltpu.VMEM((2, tq, 128), F32),
            pltpu.VMEM((2, 2, tq, tq), F32),
            pltpu.VMEM((2, 2, tq, tq), F32),
            pltpu.VMEM((2, 2, tq, tq), BF16),
            pltpu.VMEM((2, 2, tq, tq), BF16),
            pltpu.VMEM((2, tq, 128), F32),
            pltpu.VMEM((2, tq, 128), F32),
            pltpu.VMEM((2, tq, 128), F32),
            pltpu.VMEM((128, tq), BF16),
            pltpu.VMEM((128, tq), BF16),
            pltpu.VMEM((nq, 128, tq), F32),
            pltpu.VMEM((nq, 128, tq), F32),
        ],
        compiler_params=_cparams(("arbitrary", "arbitrary")),
    )(proj, proj, proj, ltot, live, d_o)


def _fox_bwd(proj, c_col, c_row, c_ends, lse, d_o, o, tq, scatter=()):
    T = proj.shape[0]
    nq = T // tq
    ns = len(scatter)

    def body(*refs):
        q_ref, k_ref, v_ref, cq_ref, ck_ref, cke_ref, lse_ref, do_ref, o_ref = refs[:9]
        dq_ref, dk_ref, dv_ref, dck_ref, dcq_ref = refs[9 + ns:14 + ns]
        (qh_s, doh_s, delta_s, shift_s, z_s, dp_s, pb_s, dsb_s, rs_s, dq_s,
         kn_s, qt_s, dot_s, dkt_s, dvt_s, kt_s) = refs[14 + 2 * ns:30 + 2 * ns]
        i = pl.program_id(1)
        if ns:
            pair = pl.program_id(0)
            finish = _ride_along(_Scatter(refs[9:9 + ns], refs[14 + ns:14 + 2 * ns], *refs[30 + 2 * ns:]),
                                 (pair == 0) & (i == 0), None, (pair == 3) & (i == nq - 1))
        lane = lax.broadcasted_iota(jnp.int32, (1, 128), 1)

        @pl.when(i == 0)
        def _():
            dkt_s[...] = jnp.zeros_like(dkt_s)
            dvt_s[...] = jnp.zeros_like(dvt_s)
            dck_ref[...] = jnp.zeros_like(dck_ref)
            _fox_key_norms(k_ref, kn_s, lane)

            def turn(n, carry):
                kt_s[n] = k_ref[pl.ds(pl.multiple_of(n * tq, tq), tq), :].astype(F32).T.astype(BF16)
                return carry

            lax.fori_loop(0, nq, turn, 0)

        row = lax.broadcasted_iota(jnp.int32, (tq, tq), 0)
        col = lax.broadcasted_iota(jnp.int32, (tq, tq), 1)
        q = q_ref[...]
        dov = do_ref[...]
        ov = o_ref[...]
        qt_s[...] = (q.astype(F32) * (HEAD_DIM ** -0.5)).T.astype(BF16)
        dot_s[...] = dov.astype(F32).T.astype(BF16)
        for hh in range(2):
            qh, hmask = _head_q(q, hh, lane)
            dohb = jnp.where(hmask, dov, 0.0).astype(BF16)
            qh_s[hh] = qh
            doh_s[hh] = dohb
            delta_s[hh] = jnp.broadcast_to(jnp.sum(dohb.astype(F32) * ov, axis=1, keepdims=True), (tq, 128))
            shift_s[hh] = jnp.broadcast_to(_lanes_to_rows(cq_ref[hh] - lse_ref[hh], row == col), (tq, 128))
        rs_s[...] = jnp.zeros_like(rs_s)
        dq_s[...] = jnp.zeros_like(dq_s)

        def rows(t):
            return pl.ds(pl.multiple_of((i - t) * tq, tq), tq)

        def stage_a(t, slot):
            k = k_ref[rows(t), :]
            v = v_ref[rows(t), :]
            for hh in range(2):
                z_s[slot, hh] = _nt(qh_s[hh], k)
                dp_s[slot, hh] = _nt(doh_s[hh], v)

        def stage_b(t, slot, diag):
            for hh in range(2):
                s = z_s[slot, hh] + _lanes2(shift_s[hh]) - ck_ref[hh, :, rows(t)]
                if diag:
                    s = jnp.where(col <= row, s, NEG)
                p = jnp.exp(s)
                ds = p * (dp_s[slot, hh] - _lanes2(delta_s[hh]))
                pb_s[slot, hh] = p.astype(BF16)
                dsb_s[slot, hh] = ds.astype(BF16)
                dck_ref[hh, :, rows(t)] += jnp.sum(ds, axis=0, keepdims=True)
                rs_s[hh] += jnp.sum(ds, axis=1, keepdims=True)

        def stage_c(t, slot):
            for hh in range(2):
                dsb = dsb_s[slot, hh]
                dims = pl.ds(HEAD_DIM * hh, HEAD_DIM)
                dq_s[dims, :] += _nt(kt_s[i - t, dims, :], dsb)
                dkt_s[i - t, dims, :] += _nn(qt_s[dims, :], dsb)
                dvt_s[i - t, dims, :] += _nn(dot_s[dims, :], pb_s[slot, hh])

        _pipeline3(_fox_live_blocks(i, qh_s, kn_s, cq_ref, cke_ref), stage_a, stage_b, stage_c, False,
                   a_first=True)

        @pl.when(i == nq - 1)
        def _():
            def flush(n, carry):
                keys = pl.ds(pl.multiple_of(n * tq, tq), tq)
                dk_ref[keys, :] = dkt_s[n].T
                dv_ref[keys, :] = dvt_s[n].T
                return carry

            lax.fori_loop(0, nq, flush, 0)
        dcq_ref[0] = _rows_to_lanes(rs_s[0], row == col)
        dcq_ref[1] = _rows_to_lanes(rs_s[1], row == col)
        dq_ref[...] = (dq_s[...].T * (HEAD_DIM ** -0.5)).astype(BF16)
        if ns:
            finish()

    res = pl.pallas_call(
        body,
        name="fox_bwd",
        grid=(4, nq),
        in_specs=[
            pl.BlockSpec((tq, 128), lambda p, i: (i, p)),
            pl.BlockSpec((T, 128), lambda p, i: (0, 4 + p)),
            pl.BlockSpec((T, 128), lambda p, i: (0, 8 + p)),
            pl.BlockSpec((2, 1, tq), lambda p, i: (p, 0, i)),
            pl.BlockSpec((2, 1, T), lambda p, i: (p, 0, 0)),
            pl.BlockSpec((2, 1, nq), lambda p, i: (p, 0, 0)),
            pl.BlockSpec((2, 1, tq), lambda p, i: (p, 0, i)),
            pl.BlockSpec((tq, 128), lambda p, i: (i, p)),
            pl.BlockSpec((tq, 128), lambda p, i: (i, p)),
        ] + [_ANY] * ns,
        out_specs=[
            pl.BlockSpec((tq, 128), lambda p, i: (i, p)),
            pl.BlockSpec((T, 128), lambda p, i: (0, p)),
            pl.BlockSpec((T, 128), lambda p, i: (0, p)),
            pl.BlockSpec((2, 1, T), lambda p, i: (p, 0, 0)),
            pl.BlockSpec((2, 1, tq), lambda p, i: (p, 0, i)),
        ] + [_ANY] * ns,
        out_shape=[
            jax.ShapeDtypeStruct((T, FOX_W), BF16),
            jax.ShapeDtypeStruct((T, FOX_W), F32),
            jax.ShapeDtypeStruct((T, FOX_W), F32),
            jax.ShapeDtypeStruct((N_FOX, 1, T), F32),
            jax.ShapeDtypeStruct((N_FOX, 1, T), F32),
        ] + [jax.ShapeDtypeStruct(b.shape, b.dtype) for b in scatter],
        scratch_shapes=[
            pltpu.VMEM((2, tq, 128), BF16),
            pltpu.VMEM((2, tq, 128), BF16),
            pltpu.VMEM((2, tq, 128), F32),
            pltpu.VMEM((2, tq, 128), F32),
            pltpu.VMEM((2, 2, tq, tq), F32),
            pltpu.VMEM((2, 2, tq, tq), F32),
            pltpu.VMEM((2, 2, tq, tq), BF16),
            pltpu.VMEM((2, 2, tq, tq), BF16),
            pltpu.VMEM((2, tq, 128), F32),
            pltpu.VMEM((128, tq), F32),
            pltpu.VMEM((2, 8, 128), F32),
            pltpu.VMEM((128, tq), BF16),
            pltpu.VMEM((128, tq), BF16),
            pltpu.VMEM((nq, 128, tq), F32),
            pltpu.VMEM((nq, 128, tq), F32),
            pltpu.VMEM((nq, 128, tq), BF16),
        ] + (_comm_sems(ns) if ns else []),
        compiler_params=_cparams(("arbitrary", "arbitrary")),
    )(proj, proj, proj, c_col, c_row, c_ends, lse, d_o, o, *scatter)
    res = list(res)
    return (*res[:5], res[5:])


def _forget_bwd(dcq, dck, xf, h1, tc):
    H, T = xf.shape
    D = h1.shape[1]
    nc = T // tc

    def body(dcq_ref, dck_ref, xf_ref, h_ref, dxf_ref, db_ref, dwf_ref):
        row = lax.broadcasted_iota(jnp.int32, (tc, tc), 0)
        col = lax.broadcasted_iota(jnp.int32, (tc, tc), 1)
        from_here = jnp.where(row >= col, 1.0, 0.0).astype(BF16)

        def chunk(n, carry):
            run, db, dwf = carry
            cs = pl.multiple_of((nc - 1 - n) * tc, tc)
            dc = dcq_ref[:, pl.ds(cs, tc)] - dck_ref[:, pl.ds(cs, tc)]
            dlogf = _split_dot(dc, from_here, 3) + run
            xfv = xf_ref[:, pl.ds(cs, tc)]
            dxf = dlogf * jax.nn.sigmoid(-xfv)
            dxf_ref[:, pl.ds(cs, tc)] = dxf
            dwf = dwf + _nn(dxf.astype(BF16), h_ref[pl.ds(cs, tc), :])
            return dlogf[:, 0:1], db + jnp.sum(dxf, axis=1, keepdims=True), dwf

        zero = jnp.zeros((H, 1), F32)
        _, db, dwf = lax.fori_loop(0, nc, chunk, (zero, zero, jnp.zeros((H, D), F32)))
        db_ref[...] = db
        dwf_ref[...] = dwf

    return pl.pallas_call(
        body,
        name="forget_bwd",
        out_shape=[jax.ShapeDtypeStruct((H, T), F32), jax.ShapeDtypeStruct((H, 1), F32),
                   jax.ShapeDtypeStruct((H, D), F32)],
        compiler_params=_cparams(),
    )(dcq, dck, xf, h1)


def _inproj_bwd(pieces, dxf_t, w_in, w_f_t, x, g1, dx1, tm, scatter=()):
    T, D = x.shape
    N = w_in.shape[1]
    ns = len(scatter)
    nt = T // tm
    npc = len(pieces)

    def body(*refs):
        pc_refs = refs[:npc]
        dxf_ref, w_ref, wf_ref, x_ref, g_ref, dx1_ref = refs[npc:npc + 6]
        base = npc + 6
        dx_ref, dg_ref = refs[base + ns:base + 2 + ns]
        i = pl.program_id(0)
        if ns:
            exchange = _Scatter(refs[base:base + ns], refs[base + 2 + ns:base + 2 + 2 * ns],
                                *refs[base + 2 + 2 * ns:])

            @pl.when(i == 0)
            def _():
                exchange.start()

        @pl.when(i == 0)
        def _():
            dg_ref[...] = jnp.zeros_like(dg_ref)

        dh = _nn(dxf_ref[...], wf_ref[...].astype(F32))
        for k, pc_ref in enumerate(pc_refs):
            dh = dh + _nt(pc_ref[...].astype(BF16), w_ref[:, k * FOX_W:(k + 1) * FOX_W])
        dx, dg = _norm_bwd(x_ref[...], g_ref[...], dh)
        dx_ref[...] = dx1_ref[...] + dx
        dg_ref[...] += dg
        if ns:
            @pl.when(i == nt - 1)
            def _():
                exchange.finish()

    res = pl.pallas_call(
        body,
        name="inproj_bwd",
        grid=(nt,),
        in_specs=[pl.BlockSpec((tm, FOX_W), lambda i: (i, 0))] * npc + [
            pl.BlockSpec((tm, N_FOX), lambda i: (i, 0)),
            pl.BlockSpec((D, N), lambda i: (0, 0)),
            pl.BlockSpec((N_FOX, D), lambda i: (0, 0)),
            pl.BlockSpec((tm, D), lambda i: (i, 0)),
            pl.BlockSpec((1, D), lambda i: (0, 0)),
            pl.BlockSpec((tm, D), lambda i: (i, 0)),
        ] + [_ANY] * ns,
        out_specs=[
            pl.BlockSpec((tm, D), lambda i: (i, 0)),
            pl.BlockSpec((1, D), lambda i: (0, 0)),
        ] + [_ANY] * ns,
        out_shape=[jax.ShapeDtypeStruct((T, D), F32), jax.ShapeDtypeStruct((1, D), F32)]
        + [jax.ShapeDtypeStruct(b.shape, b.dtype) for b in scatter],
        scratch_shapes=_comm_sems(ns) if ns else [],
        compiler_params=_cparams(("arbitrary",)),
    )(*pieces, dxf_t, w_in, w_f_t, x, g1, dx1, *scatter)
    res = list(res)
    return res[0], res[1], res[2:]


def _dw_in(h1, pieces, name):
    T, K = h1.shape
    bt = min(T, 512)
    nt = T // bt
    npc = len(pieces)

    def body(*refs):
        a_ref = refs[0]
        pc_refs = refs[1:1 + npc]
        o_ref, acc_ref = refs[1 + npc:]
        t = pl.program_id(0)

        @pl.when(t == 0)
        def _():
            acc_ref[...] = jnp.zeros_like(acc_ref)

        a = a_ref[...]
        for k, pc_ref in enumerate(pc_refs):
            acc_ref[:, k * FOX_W:(k + 1) * FOX_W] += _tn(a, pc_ref[...].astype(BF16))

        @pl.when(t == nt - 1)
        def _():
            o_ref[...] = acc_ref[...].astype(BF16)

    return pl.pallas_call(
        body,
        name=name,
        grid=(nt,),
        in_specs=[pl.BlockSpec((bt, K), lambda t: (t, 0))] + [pl.BlockSpec((bt, FOX_W), lambda t: (t, 0))] * npc,
        out_specs=pl.BlockSpec((K, npc * FOX_W), lambda t: (0, 0)),
        out_shape=jax.ShapeDtypeStruct((K, npc * FOX_W), BF16),
        scratch_shapes=[pltpu.VMEM((K, npc * FOX_W), F32)],
        compiler_params=_cparams(("arbitrary",)),
    )(h1, *pieces)


def _matmul_tn(a, b, name, cast_b=False):
    T, K = a.shape
    N = b.shape[1]
    bt = min(T, 512)
    bk = _tile_div(K, 1536)
    bn = _tile_div(N, 1536)
    nt = T // bt

    def body(a_ref, b_ref, o_ref, acc_ref):
        t = pl.program_id(2)

        @pl.when(t == 0)
        def _():
            acc_ref[...] = jnp.zeros_like(acc_ref)

        bv = b_ref[...]
        if cast_b:
            bv = bv.astype(BF16)
        acc_ref[...] += _tn(a_ref[...], bv)

        @pl.when(t == nt - 1)
        def _():
            o_ref[...] = acc_ref[...].astype(BF16)

    return pl.pallas_call(
        body,
        name=name,
        grid=(K // bk, N // bn, nt),
        in_specs=[
            pl.BlockSpec((bt, bk), lambda k, n, t: (t, k)),
            pl.BlockSpec((bt, bn), lambda k, n, t: (t, n)),
        ],
        out_specs=pl.BlockSpec((bk, bn), lambda k, n, t: (k, n)),
        out_shape=jax.ShapeDtypeStruct((K, N), BF16),
        scratch_shapes=[pltpu.VMEM((bk, bn), F32)],
        compiler_params=_cparams(("arbitrary", "arbitrary", "arbitrary")),
    )(a, b)


def _local_step(x, mem, target, p, tm, tq, late=None):
    T, D = x.shape
    w_in = p["w_in"]
    w_qkv = w_in[:, :QKV_W]
    w_f_t = w_in[:, QKV_W:].T
    b_f = p["b_forget"].reshape(N_FOX, 1)

    proj, h1, xf, c = _inproj_fwd(x, p["attn_norm_g"], w_qkv, w_f_t, b_f, tm)
    c_col = c.reshape(N_FOX, 1, T)
    c_row = c.reshape(N_FOX, 1, T)
    c_ends = c[:, tq - 1::tq].reshape(N_FOX, 1, T // tq)
    fox_o, lse, gathered = _fox_fwd(proj, c_col, c_row, c_ends, tq, gather=[late[n] for n in _LATE] if late else ())
    if late:
        p = dict(p, **{n: _gathered_full(n, gv) for n, gv in zip(_LATE, gathered)})
    sb_o, sb_ltot, sb_live = _sb_fwd(proj, tq)
    x1, mixed = _post_attn_fwd(fox_o, sb_o, p["fox_out_g"], p["sb_out_g"], p["w_out"], x, tm)
    mb, kv = _mem_kv_fwd(mem, p["mem_norm_g"], p["w_mkv"])
    x2, h2, qb, om = _xattn_fwd(x1, p["xattn_norm_g"], p["w_mq"], kv, p["w_mo"], tm)
    tf = 2 * tm if T % (2 * tm) == 0 else tm
    x3, h3, ug, uv, yg, yv, a = _ffn_fwd(
        x2, p["ffn_norm_g"], p["w_up"], p["conv_w"], p["conv_b"], p["w_down"], tf)
    dx3, loss_blk, d_final_g = _loss_head(x3, p["final_norm_g"], target, tm)

    g = {"final_norm_g": d_final_g}
    dx2, du_g, du_v, g["ffn_norm_g"], dc_g, dc_v = _ffn_bwd(
        dx3, x2, p["ffn_norm_g"], ug, uv, yg, yv, p["conv_w"], p["w_down"], p["w_up"], tf)
    g["w_down"] = _matmul_tn(a, dx3, "dw_down", cast_b=True)
    g["w_up"] = jnp.concatenate([_matmul_tn(h3, du_g, "dw_up_gate"), _matmul_tn(h3, du_v, "dw_up_val")], axis=1)
    dconv = jnp.concatenate([dc_g, dc_v], axis=1)
    g["conv_w"] = dconv[0:3]
    g["conv_b"] = dconv[3:4]
    dx1, dq_m, dkv, g["xattn_norm_g"] = _xattn_bwd(dx2, x1, p["xattn_norm_g"], qb, kv, p["w_mo"], p["w_mq"], tm)
    g["w_mo"] = _matmul_tn(om, dx2, "dw_mo", cast_b=True)
    g["w_mq"] = _matmul_tn(h2, dq_m, "dw_mq")
    g["w_mkv"], g["mem_norm_g"] = _mem_kv_bwd(mem, p["mem_norm_g"], mb, dkv, p["w_mkv"])
    d_fox, d_sb, g["fox_out_g"], g["sb_out_g"] = _post_attn_bwd(
        dx1, fox_o, sb_o, p["fox_out_g"], p["sb_out_g"], p["w_out"], tm)
    g["w_out"] = _matmul_tn(mixed, dx1, "dw_out", cast_b=True)
    dq_s, dk_s, dv_s = _sb_bwd(proj, sb_ltot, sb_live, d_sb, tq)
    dq_f, dk_f, dv_f, dck, dcq, parts = _fox_bwd(
        proj, c_col, c_row, c_ends, lse, d_fox, fox_o, tq,
        scatter=[_grad_blocks(n, g[n]) for n in _LATE] if late else ())
    if late:
        g["parts"] = dict(zip(_LATE, parts))
    dxf, db, dwf_t = _forget_bwd(dcq.reshape(N_FOX, T), dck.reshape(N_FOX, T), xf, h1, min(T, 512))
    g["b_forget"] = db.reshape(1, N_FOX)
    pieces = [dq_f, dk_f, dv_f, dq_s, dk_s, dv_s]
    g["w_in"] = jnp.concatenate([_dw_in(h1, pieces, "dw_in"), dwf_t.T.astype(BF16)], axis=1)
    grad_x, g["attn_norm_g"], parts = _inproj_bwd(
        pieces, dxf.T, w_in, w_f_t, x, p["attn_norm_g"], dx1, tm,
        scatter=[_grad_blocks("w_in", g["w_in"])] if late else ())
    if late:
        (g["parts"]["w_in"],) = parts
    return loss_blk, grad_x, g


def _mesh_pos():
    return lax.axis_index("x"), lax.axis_index("y"), lax.axis_index("c")


def _flip(pos, k):
    return tuple(1 - v if (k >> b) & 1 else v for v, b in zip(pos, (2, 1, 0)))


def _slot(pos):
    return 4 * pos[0] + 2 * pos[1] + pos[2]


_CHIPS = (4, 2, 6)


def _comm_sems(n):
    return [pltpu.SemaphoreType.DMA((7 * n,)), pltpu.SemaphoreType.DMA((7 * n,)), pltpu.SemaphoreType.DMA((n,))]


class _Gather:
    def __init__(self, ins, outs, send_sems, recv_sems, local_sems):
        self.ins, self.outs, self.n = ins, outs, len(ins)
        self.send_sems, self.recv_sems, self.local_sems = send_sems, recv_sems, local_sems
        self.me = _mesh_pos()
        self.sibling = _flip(self.me, 1)

    def _copy(self, a, kk, block, to, src=None):
        rows = self.outs[a].at[_slot(block)]
        return pltpu.make_async_remote_copy(
            src_ref=rows if src is None else src, dst_ref=rows,
            send_sem=self.send_sems.at[7 * a + kk], recv_sem=self.recv_sems.at[7 * a + kk],
            device_id=to, device_id_type=MESH)

    def _mine(self):
        return [pltpu.make_async_copy(self.ins[a], self.outs[a].at[_slot(self.me)], self.local_sems.at[a])
                for a in range(self.n)]

    def _first(self):
        out = []
        for a in range(self.n):
            out.append(self._copy(a, 0, self.me, self.sibling, src=self.ins[a]))
            out += [self._copy(a, 1 + j, self.me, _flip(self.me, k), src=self.ins[a]) for j, k in enumerate(_CHIPS)]
        return out

    def _passed(self):
        return [self._copy(a, 4 + j, _flip(self.me, k), self.sibling)
                for j, k in enumerate(_CHIPS) for a in range(self.n)]

    def start(self):
        for cp in self._mine() + self._first():
            cp.start()

    def forward(self):
        for j, k in enumerate(_CHIPS):
            for a in range(self.n):
                self._copy(a, 1 + j, _flip(self.me, k), self.me).wait_recv()
                self._copy(a, 4 + j, _flip(self.me, k), self.sibling).start()

    def finish(self):
        for a in range(self.n):
            self._copy(a, 0, self.sibling, self.me).wait_recv()
            for j, k in enumerate(_CHIPS):
                self._copy(a, 4 + j, _flip(self.sibling, k), self.me).wait_recv()
        for cp in self._first() + self._passed():
            cp.wait_send()
        for cp in self._mine():
            cp.wait()


class _Scatter:
    def __init__(self, ins, outs, send_sems, recv_sems, local_sems):
        self.ins, self.outs, self.n = ins, outs, len(ins)
        self.send_sems, self.recv_sems, self.local_sems = send_sems, recv_sems, local_sems
        self.me = _mesh_pos()

    def _copy(self, a, k, landed=False):
        peer = _flip(self.me, k)
        return pltpu.make_async_remote_copy(
            src_ref=self.ins[a].at[_slot(peer)], dst_ref=self.outs[a].at[_slot(peer if landed else self.me)],
            send_sem=self.send_sems.at[7 * a + k - 1], recv_sem=self.recv_sems.at[7 * a + k - 1],
            device_id=peer, device_id_type=MESH)

    def _mine(self):
        s = _slot(self.me)
        return [pltpu.make_async_copy(self.ins[a].at[s], self.outs[a].at[s], self.local_sems.at[a])
                for a in range(self.n)]

    def start(self):
        for cp in self._mine() + [self._copy(a, k) for k in range(1, 8) for a in range(self.n)]:
            cp.start()

    def finish(self):
        for k in range(1, 8):
            for a in range(self.n):
                self._copy(a, k, landed=True).wait_recv()
        for k in range(1, 8):
            for a in range(self.n):
                self._copy(a, k).wait_send()
        for cp in self._mine():
            cp.wait()


_ANY = pl.BlockSpec(memory_space=pl.ANY)


def _gathered_shapes(shards):
    return [jax.ShapeDtypeStruct((N_DEV,) + s.shape, s.dtype) for s in shards]


def _all_gather(shards, name):
    n = len(shards)

    def body(*refs):
        g = _Gather(refs[:n], refs[n:2 * n], *refs[2 * n:])
        g.start()
        g.forward()
        g.finish()

    return pl.pallas_call(
        body, name=name, in_specs=[_ANY] * n, out_specs=[_ANY] * n,
        out_shape=_gathered_shapes(shards), scratch_shapes=_comm_sems(n),
    )(*shards)


def _adamw_math(w, g, m, v):
    m2 = ADAM_B1 * m + (1.0 - ADAM_B1) * g
    v2 = ADAM_B2 * v + (1.0 - ADAM_B2) * (g * g)
    m_hat = m2 / (1.0 - ADAM_B1 ** ADAM_STEP)
    v_hat = v2 / (1.0 - ADAM_B2 ** ADAM_STEP)
    delta = -ADAM_LR * (m_hat / (jnp.sqrt(v_hat) + ADAM_EPS) + ADAM_WD * w)
    return delta, m2, v2


def _adamw(w, parts, m, v, name):
    R, C = w.shape
    br = 128 if R % 128 == 0 else R

    def body(w_ref, p_ref, m_ref, v_ref, g_ref, d_ref, nm_ref, nv_ref):
        g = p_ref[0].astype(F32)
        for s in range(1, N_DEV):
            g = g + p_ref[s].astype(F32)
        g_ref[...] = g
        d_ref[...], nm_ref[...], nv_ref[...] = _adamw_math(w_ref[...], g, m_ref[...], v_ref[...])

    spec = pl.BlockSpec((br, C), lambda i: (i, 0))
    return pl.pallas_call(
        body,
        name=name,
        grid=(R // br,),
        in_specs=[spec, pl.BlockSpec((N_DEV, br, C), lambda i: (0, i, 0)), spec, spec],
        out_specs=[spec] * 4,
        out_shape=[jax.ShapeDtypeStruct((R, C), F32)] * 4,
        compiler_params=_cparams(("arbitrary",)),
    )(w, parts, m, v)


_SHARDED = ("w_in", "w_out", "w_mq", "w_mkv", "w_mo", "w_up", "conv_w", "w_down")
_LATE = _SHARDED[1:]
_COL_SHARDED = ("w_in", "w_mkv", "w_up", "conv_w")
_REPLICATED = ("attn_norm_g", "b_forget", "fox_out_g", "sb_out_g", "xattn_norm_g", "mem_norm_g",
               "ffn_norm_g", "conv_b", "final_norm_g")
_WEIGHTS = ("attn_norm_g", "w_in", "b_forget", "fox_out_g", "sb_out_g", "w_out", "xattn_norm_g", "mem_norm_g",
            "w_mq", "w_mkv", "w_mo", "ffn_norm_g", "w_up", "conv_w", "conv_b", "w_down", "final_norm_g")


def _pack_rows(n):
    return -(-n // 128)


def _pack(vals, rows_total):
    parts = []
    for v in vals:
        flat = v.reshape(-1)
        parts.append(jnp.pad(flat, (0, _pack_rows(flat.shape[0]) * 128 - flat.shape[0])))
    flat = jnp.concatenate(parts)
    return jnp.pad(flat, (0, rows_total * 128 - flat.shape[0])).reshape(rows_total, 128)


def _unpack(packed, shapes):
    out = []
    r = 0
    for shp in shapes:
        n = 1
        for d in shp:
            n *= d
        out.append(packed[r:r + _pack_rows(n)].reshape(-1)[:n].reshape(shp))
        r += _pack_rows(n)
    return out


def _gathered_full(name, gathered):
    if name in _COL_SHARDED:
        return jnp.transpose(gathered, (1, 0, 2)).reshape(gathered.shape[1], -1)
    return gathered.reshape(-1, gathered.shape[2])


def _to_blocks(name, full):
    if name in _COL_SHARDED:
        r = full.shape[0]
        return jnp.transpose(full.reshape(r, N_DEV, -1), (1, 0, 2))
    return full.reshape(N_DEV, -1, full.shape[1])


def _grad_blocks(name, full):
    blocks = _to_blocks(name, full)
    return blocks if name == "conv_w" else blocks.astype(BF16)


def _step(args, tm, tq):
    w = {n: args[n] for n in _WEIGHTS}
    mom = {n: args["m_" + n] for n in _WEIGHTS}
    var = {n: args["v_" + n] for n in _WEIGHTS}
    x = args["x"][0]
    mem = args["mem"][0]
    target = args["loss_target"][0]

    def flat2(a):
        return a.reshape(a.shape[-2], a.shape[-1]) if a.ndim == 3 else a.reshape(1, -1)

    shards = {n: flat2(w[n]) if n == "conv_w" else flat2(w[n]).astype(BF16) for n in _SHARDED}
    (w_in_all,) = _all_gather([shards["w_in"]], "gather_w_in")
    p = {"w_in": _gathered_full("w_in", w_in_all)}
    for n in _REPLICATED:
        p[n] = flat2(w[n])

    loss_blk, grad_x, g = _local_step(x, mem, target, p, tm, tq, late={n: shards[n] for n in _LATE})

    parts = g["parts"]
    out = {}
    for n in _SHARDED:
        res = _adamw(flat2(w[n]), parts[n], flat2(mom[n]), flat2(var[n]), "adamw_" + n)
        out[n] = [r.reshape(w[n].shape) for r in res]

    shapes = [w[n].shape for n in _REPLICATED]
    rows = sum(_pack_rows(flat2(w[n]).shape[1]) for n in _REPLICATED) + 1
    rows = -(-rows // 8) * 8
    g_pack = _pack([g[n] for n in _REPLICATED] + [loss_blk[0:1, :]], rows)
    (g_all,) = _all_gather([g_pack], "gather_small")
    res = _adamw(_pack([w[n] for n in _REPLICATED], rows), g_all,
                 _pack([mom[n] for n in _REPLICATED], rows), _pack([var[n] for n in _REPLICATED], rows),
                 "adamw_small")
    n_rows_params = sum(_pack_rows(flat2(w[n]).shape[1]) for n in _REPLICATED)
    loss = res[0][n_rows_params, 0]
    unpacked = [_unpack(r, shapes) for r in res]
    for k, n in enumerate(_REPLICATED):
        out[n] = [unpacked[q][k] for q in range(4)]

    grads = [out[n][0] for n in _WEIGHTS]
    deltas = [out[n][1] for n in _WEIGHTS]
    new_m = [out[n][2] for n in _WEIGHTS]
    new_v = [out[n][3] for n in _WEIGHTS]
    return (loss, grad_x[None], *grads, *deltas, *new_m, *new_v)


def kernel(x, mem, attn_norm_g, w_in, b_forget, fox_out_g, sb_out_g, w_out, xattn_norm_g, mem_norm_g, w_mq, w_mkv, w_mo, ffn_norm_g, w_up, conv_w, conv_b, w_down, final_norm_g, loss_target, m_attn_norm_g, m_w_in, m_b_forget, m_fox_out_g, m_sb_out_g, m_w_out, m_xattn_norm_g, m_mem_norm_g, m_w_mq, m_w_mkv, m_w_mo, m_ffn_norm_g, m_w_up, m_conv_w, m_conv_b, m_w_down, m_final_norm_g, v_attn_norm_g, v_w_in, v_b_forget, v_fox_out_g, v_sb_out_g, v_w_out, v_xattn_norm_g, v_mem_norm_g, v_w_mq, v_w_mkv, v_w_mo, v_ffn_norm_g, v_w_up, v_conv_w, v_conv_b, v_w_down, v_final_norm_g):
    args = dict(locals())
    T = x.shape[1]
    return _step(args, tm=min(T, 512), tq=min(T, 256))
```

```python
import functools

import jax
import jax.numpy as jnp
from jax import lax
from jax.experimental import pallas as pl
from jax.experimental.pallas import tpu as pltpu

F32 = jnp.float32
BF16 = jnp.bfloat16
EPS = 1e-6
NEG = -1e30
LOG2E = 1.4426950408889634

HEAD_DIM = 64
N_FOX = 8
FOX_W = 512
QKV_W = 3072
N_MEM_HEADS = 4
MEM_HD = 256
D_FF = 2816
FF_CHUNK = 256
N_DEV = 8

ADAM_LR = 0.001
ADAM_B1 = 0.9
ADAM_B2 = 0.999
ADAM_EPS = 1e-08
ADAM_WD = 0.01
ADAM_STEP = 10

SB_SUM_TERMS = 1

VMEM_LIMIT = 56 * 1024 * 1024
MESH = pl.DeviceIdType.MESH


def _cparams(sem=None):
    return pltpu.CompilerParams(dimension_semantics=sem, vmem_limit_bytes=VMEM_LIMIT)


def _nt(a, b):
    return lax.dot_general(a, b, (((1,), (1,)), ((), ())), preferred_element_type=F32)


def _tn(a, b):
    return lax.dot_general(a, b, (((0,), (0,)), ((), ())), preferred_element_type=F32)


def _nn(a, b):
    return jnp.dot(a, b, preferred_element_type=F32)


def _split_dot(a, m01, terms):
    out = None
    r = a
    for t in range(terms):
        p = r.astype(BF16)
        d = _nn(p, m01)
        out = d if out is None else out + d
        if t + 1 < terms:
            r = r - p.astype(F32)
    return out


def _rstd(xv):
    return lax.rsqrt(jnp.mean(xv * xv, axis=-1, keepdims=True) + EPS)


def _norm_bwd(xv, g, dh):
    r = _rstd(xv)
    xhat = xv * r
    dxhat = dh * g
    dx = r * (dxhat - xhat * jnp.mean(dxhat * xhat, axis=-1, keepdims=True))
    dg = jnp.sum(dh * xhat, axis=0, keepdims=True)
    return dx, dg


def _tile_div(n, cap):
    best = None
    for d in range(128, min(n, cap) + 1, 128):
        if n % d == 0:
            best = d
    assert best is not None, n
    return best


def _inproj_fwd(x, g1, w_qkv, w_f_t, b_f, tm):
    T, D = x.shape
    N = w_qkv.shape[1]
    H = w_f_t.shape[0]

    def body(x_ref, g_ref, w_ref, wf_ref, b_ref, proj_ref, h_ref, xf_ref, c_ref, carry_ref):
        i = pl.program_id(0)

        @pl.when(i == 0)
        def _():
            carry_ref[...] = jnp.zeros_like(carry_ref)

        xv = x_ref[...]
        h = (xv * _rstd(xv) * g_ref[...]).astype(BF16)
        h_ref[...] = h
        for n0 in range(0, N, 512):
            proj_ref[:, n0:n0 + 512] = _nn(h, w_ref[:, n0:n0 + 512]).astype(BF16)
        xf = _nt(wf_ref[...], h) + b_ref[...]
        xf_ref[...] = xf
        logf = jnp.minimum(xf, 0.0) - jnp.log1p(jnp.exp(-jnp.abs(xf)))
        row = lax.broadcasted_iota(jnp.int32, (tm, tm), 0)
        col = lax.broadcasted_iota(jnp.int32, (tm, tm), 1)
        upper = jnp.where(row <= col, 1.0, 0.0).astype(BF16)
        c = _split_dot(logf, upper, 3) + carry_ref[...]
        c_ref[...] = c
        carry_ref[...] = c[:, tm - 1:tm]

    return pl.pallas_call(
        body,
        name="inproj_fwd",
        grid=(T // tm,),
        in_specs=[
            pl.BlockSpec((tm, D), lambda i: (i, 0)),
            pl.BlockSpec((1, D), lambda i: (0, 0)),
            pl.BlockSpec((D, N), lambda i: (0, 0)),
            pl.BlockSpec((H, D), lambda i: (0, 0)),
            pl.BlockSpec((H, 1), lambda i: (0, 0)),
        ],
        out_specs=[
            pl.BlockSpec((tm, N), lambda i: (i, 0)),
            pl.BlockSpec((tm, D), lambda i: (i, 0)),
            pl.BlockSpec((H, tm), lambda i: (0, i)),
            pl.BlockSpec((H, tm), lambda i: (0, i)),
        ],
        out_shape=[
            jax.ShapeDtypeStruct((T, N), BF16),
            jax.ShapeDtypeStruct((T, D), BF16),
            jax.ShapeDtypeStruct((H, T), F32),
            jax.ShapeDtypeStruct((H, T), F32),
        ],
        scratch_shapes=[pltpu.VMEM((H, 1), F32)],
        compiler_params=_cparams(("arbitrary",)),
    )(x, g1, w_qkv, w_f_t, b_f)


def _head_q(q, hh, lane):
    hmask = (lane >= HEAD_DIM * hh) & (lane < HEAD_DIM * (hh + 1))
    qh = jnp.where(hmask, q.astype(F32), 0.0) * (HEAD_DIM ** -0.5)
    return qh.astype(BF16), hmask


def _pipeline3(n, stage_a, stage_b, stage_c, diag_last, alive=None, a_first=False):
    stage_a(0, 0)
    if diag_last:
        @pl.when(n == 1)
        def _():
            stage_b(0, 0, True)

        @pl.when(n >= 2)
        def _():
            stage_b(0, 0, False)
            stage_a(1, 1)
    else:
        stage_a(jnp.minimum(1, n - 1), 1)
        stage_b(0, 0, True)

    def pair(m, carry):
        t = 2 + 2 * m
        if a_first:
            stage_a(t, 0)
            stage_b(t - 1, 1, False)
            stage_c(t - 2, 0)
            stage_a(t + 1, 1)
            stage_b(t, 0, False)
            stage_c(t - 1, 1)
        else:
            stage_c(t - 2, 0)
            stage_b(t - 1, 1, False)
            stage_a(t, 0)
            stage_c(t - 1, 1)
            stage_b(t, 0, False)
            stage_a(t + 1, 1)
        return carry

    pairs = (n - 2) // 2
    if alive is None:
        lax.fori_loop(0, pairs, pair, 0)
        go_on = True
        done = n
    else:
        def more(state):
            return (state[0] < pairs) & state[1]

        def step(state):
            pair(state[0], 0)
            return state[0] + 1, alive()

        m_end, go_on = lax.while_loop(more, step, (jnp.int32(0), jnp.bool_(True)))
        done = jnp.where(go_on, n, 2 * m_end)
    odd = n % 2 == 1

    @pl.when((n >= 3) & odd & go_on)
    def _():
        stage_a(n - 1, 0)
        stage_c(n - 3, 0)
        stage_b(n - 2, 1, False)
        stage_c(n - 2, 1)
        stage_b(n - 1, 0, diag_last)
        stage_c(n - 1, 0)

    @pl.when((n == 1) & go_on)
    def _():
        stage_c(0, 0)

    @pl.when(jnp.logical_not(odd) & go_on)
    def _():
        stage_c(n - 2, 0)
        stage_b(n - 1, 1, diag_last)
        stage_c(n - 1, 1)

    return done


def _lanes2(x):
    return jnp.concatenate([x, x], axis=1)


def _lanes_to_rows(vec, eye):
    return jnp.sum(jnp.where(eye, jnp.broadcast_to(vec, eye.shape), 0.0), axis=1, keepdims=True)


def _rows_to_lanes(rep, eye):
    return jnp.sum(jnp.where(eye, _lanes2(rep), 0.0), axis=0, keepdims=True)


FOX_DEAD = -110.0


def _fox_key_norms(k_ref, kn_s, lane):
    T = k_ref.shape[0]
    rows = min(T, 512)
    for hh in range(2):
        hmask = (lane >= HEAD_DIM * hh) & (lane < HEAD_DIM * (hh + 1))

        def chunk(n, best, hmask=hmask):
            kf = jnp.where(hmask, k_ref[pl.ds(pl.multiple_of(n * rows, rows), rows), :].astype(F32), 0.0)
            sq = jnp.sum(kf * kf, axis=1, keepdims=True)
            return jnp.maximum(best, jnp.max(sq, axis=0, keepdims=True))

        best = lax.fori_loop(0, T // rows, chunk, jnp.zeros((1, 1), F32))
        kn_s[hh] = jnp.broadcast_to(best, kn_s.shape[1:])


def _fox_live_blocks(i, qh_s, kn_s, cq_ref, cke_ref):
    nq = cke_ref.shape[-1]
    jj = lax.broadcasted_iota(jnp.int32, (1, nq), 1)
    first = None
    for hh in range(2):
        qf = qh_s[hh].astype(F32)
        qn = jnp.max(jnp.sum(qf * qf, axis=1, keepdims=True), axis=0, keepdims=True)
        zb = jnp.sqrt(qn * kn_s[hh][0:1, 0:1]) * 1.001
        bound = (2.0 * zb + cq_ref[hh][:, 0:1]) - cke_ref[hh]
        live = (bound >= FOX_DEAD) & (jj <= i)
        f = jnp.min(jnp.where(live, jj, i).astype(F32), axis=1, keepdims=True)
        first = f if first is None else jnp.minimum(first, f)
    return i + 1 - first[0, 0].astype(jnp.int32)


def _ride_along(exchange, at_start, at_middle, at_end):
    @pl.when(at_start)
    def _():
        exchange.start()

    if at_middle is not None:
        @pl.when(at_middle)
        def _():
            exchange.forward()

    def finish():
        @pl.when(at_end)
        def _():
            exchange.finish()

    return finish


def _fox_fwd(proj, c_col, c_row, c_ends, tq, gather=()):
    T = proj.shape[0]
    assert tq == 256
    nq = T // tq
    ng = len(gather)

    def body(*refs):
        q_ref, k_ref, v_ref, cq_ref, ck_ref, cke_ref = refs[:6]
        o_ref, lse_ref = refs[6 + ng:8 + ng]
        qh_s, cq_s, z_s, p_s, al_s, m_s, acc_s, kn_s = refs[8 + 2 * ng:16 + 2 * ng]
        i = pl.program_id(1)
        if ng:
            pair = pl.program_id(0)
            finish = _ride_along(_Gather(refs[6:6 + ng], refs[8 + ng:8 + 2 * ng], *refs[16 + 2 * ng:]),
                                 (pair == 0) & (i == 0), (pair == 1) & (i == 0), (pair == 3) & (i == nq - 1))
        lane = lax.broadcasted_iota(jnp.int32, (1, 128), 1)
        row = lax.broadcasted_iota(jnp.int32, (tq, tq), 0)
        col = lax.broadcasted_iota(jnp.int32, (tq, tq), 1)

        @pl.when(i == 0)
        def _():
            _fox_key_norms(k_ref, kn_s, lane)

        q = q_ref[...]
        for hh in range(2):
            qh_s[hh] = _head_q(q, hh, lane)[0]
            cq_s[hh] = jnp.broadcast_to(_lanes_to_rows(cq_ref[hh], row == col), (tq, 128))
        m_s[...] = jnp.full(m_s.shape, NEG, F32)
        acc_s[...] = jnp.zeros_like(acc_s)

        def rows(t):
            return pl.ds(pl.multiple_of((i - t) * tq, tq), tq)

        def stage_a(t, slot):
            k = k_ref[rows(t), :]
            for hh in range(2):
                z_s[slot, hh] = _nt(qh_s[hh], k)

        def stage_b(t, slot, diag):
            for hh in range(2):
                s = z_s[slot, hh] + _lanes2(cq_s[hh]) - ck_ref[hh, :, rows(t)]
                if diag:
                    s = jnp.where(col <= row, s, NEG)
                m = m_s[hh]
                half = jnp.maximum(s[:, :128], s[:, 128:])
                m_new = jnp.maximum(m, jnp.max(half, axis=1, keepdims=True))
                m_s[hh] = m_new
                al_s[slot, hh] = jnp.exp(m - m_new)
                p_s[slot, hh] = jnp.exp(s - _lanes2(m_new)).astype(BF16)

        def stage_c(t, slot):
            v = v_ref[rows(t), :]
            for hh in range(2):
                own = (lane >= HEAD_DIM * hh) & (lane < HEAD_DIM * (hh + 1))
                acc_s[hh] = (al_s[slot, hh] * acc_s[hh]
                             + _nn(p_s[slot, hh], jnp.where(own, v, 1.0).astype(BF16)))

        _pipeline3(_fox_live_blocks(i, qh_s, kn_s, cq_ref, cke_ref), stage_a, stage_b, stage_c, False)
        halves = []
        for hh in range(2):
            acc = acc_s[hh]
            own = (lane >= HEAD_DIM * hh) & (lane < HEAD_DIM * (hh + 1))
            halves.append(jnp.where(own, pltpu.roll(acc, HEAD_DIM, axis=1), acc))
        l0, l1 = halves
        o_ref[...] = jnp.where(lane < HEAD_DIM, acc_s[0] / l0, acc_s[1] / l1)
        lse_ref[0] = _rows_to_lanes(m_s[0] + jnp.log(l0), row == col)
        lse_ref[1] = _rows_to_lanes(m_s[1] + jnp.log(l1), row == col)
        if ng:
            finish()

    res = pl.pallas_call(
        body,
        name="fox_fwd",
        grid=(4, nq),
        in_specs=[
            pl.BlockSpec((tq, 128), lambda p, i: (i, p)),
            pl.BlockSpec((T, 128), lambda p, i: (0, 4 + p)),
            pl.BlockSpec((T, 128), lambda p, i: (0, 8 + p)),
            pl.BlockSpec((2, 1, tq), lambda p, i: (p, 0, i)),
            pl.BlockSpec((2, 1, T), lambda p, i: (p, 0, 0)),
            pl.BlockSpec((2, 1, nq), lambda p, i: (p, 0, 0)),
        ] + [_ANY] * ng,
        out_specs=[
            pl.BlockSpec((tq, 128), lambda p, i: (i, p)),
            pl.BlockSpec((2, 1, tq), lambda p, i: (p, 0, i)),
        ] + [_ANY] * ng,
        out_shape=[
            jax.ShapeDtypeStruct((T, FOX_W), F32),
            jax.ShapeDtypeStruct((N_FOX, 1, T), F32),
        ] + _gathered_shapes(gather),
        scratch_shapes=[
            pltpu.VMEM((2, tq, 128), BF16),
            pltpu.VMEM((2, tq, 128), F32),
            pltpu.VMEM((2, 2, tq, tq), F32),
            pltpu.VMEM((2, 2, tq, tq), BF16),
            pltpu.VMEM((2, 2, tq, 128), F32),
            pltpu.VMEM((2, tq, 128), F32),
            pltpu.VMEM((2, tq, 128), F32),
            pltpu.VMEM((2, 8, 128), F32),
        ] + (_comm_sems(ng) if ng else []),
        compiler_params=_cparams(("arbitrary", "arbitrary")),
    )(proj, proj, proj, c_col, c_row, c_ends, *gather)
    res = list(res)
    return res[0], res[1], res[2:]


def _sb_logs(zn, strict):
    e = jnp.exp2(jnp.abs(zn) * (-LOG2E))
    L = jnp.minimum(zn, 0.0) - jnp.log(1.0 + e)
    G = L - zn
    if strict is not None:
        L = jnp.where(strict, L, 0.0)
    return L, G


SB_DEAD = -110.0


def _sb_fwd(proj, tq):
    T = proj.shape[0]
    nq = T // tq

    def body(q_ref, k_ref, v_ref, o_ref, ltot_ref, live_ref, qh_s, z_s, g_s, tot_s, run_s, acc_s):
        i = pl.program_id(1)
        lane = lax.broadcasted_iota(jnp.int32, (1, 128), 1)
        row = lax.broadcasted_iota(jnp.int32, (tq, tq), 0)
        col = lax.broadcasted_iota(jnp.int32, (tq, tq), 1)
        strict = col < row
        later = jnp.where(row > col, 1.0, 0.0).astype(BF16)
        q = q_ref[...]
        for hh in range(2):
            qh_s[hh] = -_head_q(q, hh, lane)[0]
        run_s[...] = jnp.zeros_like(run_s)
        acc_s[...] = jnp.zeros_like(acc_s)

        def rows(t):
            return pl.ds(pl.multiple_of((i - t) * tq, tq), tq)

        def stage_a(t, slot):
            k = k_ref[rows(t), :]
            for hh in range(2):
                z_s[slot, hh] = _nt(qh_s[hh], k)

        def stage_b(t, slot, diag):
            for hh in range(2):
                L, g = _sb_logs(z_s[slot, hh], strict if diag else None)
                if diag:
                    g = jnp.where(strict, g, NEG)
                after = _split_dot(L, later, SB_SUM_TERMS)
                g_s[slot, hh] = g + after
                first = L[:, 0:1]
                if SB_SUM_TERMS == 1:
                    first = first.astype(BF16).astype(F32)
                tot_s[slot, hh] = jnp.broadcast_to(after[:, 0:1] + first, (tq, 128))

        def stage_c(t, slot):
            v = v_ref[rows(t), :]
            for hh in range(2):
                run = run_s[hh]
                a = jnp.exp(g_s[slot, hh] + _lanes2(run))
                acc_s[hh] += _nn(a.astype(BF16), v)
                run_s[hh] = run + tot_s[slot, hh]

        def alive():
            return jnp.max(jnp.maximum(run_s[0], run_s[1])) > SB_DEAD

        done = _pipeline3(i + 1, stage_a, stage_b, stage_c, False, alive)
        ltot_ref[0] = _rows_to_lanes(run_s[0], row == col)
        ltot_ref[1] = _rows_to_lanes(run_s[1], row == col)
        o_ref[...] = jnp.where(lane < HEAD_DIM, acc_s[0], acc_s[1])
        at = lax.broadcasted_iota(jnp.int32, (1, nq), 1)

        @pl.when(i == 0)
        def _():
            live_ref[0] = jnp.zeros((1, nq), F32)

        live_ref[0] = jnp.where(at == i, done.astype(F32), live_ref[0])

    return pl.pallas_call(
        body,
        name="sb_fwd",
        grid=(4, nq),
        in_specs=[
            pl.BlockSpec((tq, 128), lambda p, i: (i, 12 + p)),
            pl.BlockSpec((T, 128), lambda p, i: (0, 16 + p)),
            pl.BlockSpec((T, 128), lambda p, i: (0, 20 + p)),
        ],
        out_specs=[
            pl.BlockSpec((tq, 128), lambda p, i: (i, p)),
            pl.BlockSpec((2, 1, tq), lambda p, i: (p, 0, i)),
            pl.BlockSpec((1, 1, nq), lambda p, i: (p, 0, 0)),
        ],
        out_shape=[
            jax.ShapeDtypeStruct((T, FOX_W), F32),
            jax.ShapeDtypeStruct((N_FOX, 1, T), F32),
            jax.ShapeDtypeStruct((N_FOX // 2, 1, nq), F32),
        ],
        scratch_shapes=[
            pltpu.VMEM((2, tq, 128), BF16),
            pltpu.VMEM((2, 2, tq, tq), F32),
            pltpu.VMEM((2, 2, tq, tq), F32),
            pltpu.VMEM((2, 2, tq, 128), F32),
            pltpu.VMEM((2, tq, 128), F32),
            pltpu.VMEM((2, tq, 128), F32),
        ],
        compiler_params=_cparams(("arbitrary", "arbitrary")),
    )(proj, proj, proj)


def _post_attn_fwd(fox_o, sb_o, gf, gs, w_out, x, tm):
    T, D = x.shape

    def body(f_ref, s_ref, gf_ref, gs_ref, w_ref, x_ref, x1_ref, mix_ref):
        f = f_ref[...]
        s = s_ref[...]
        mix_ref[:, :FOX_W] = (f * _rstd(f) * gf_ref[...]).astype(BF16)
        mix_ref[:, FOX_W:] = (s * _rstd(s) * gs_ref[...]).astype(BF16)
        x1_ref[...] = x_ref[...] + _nn(mix_ref[...], w_ref[...])

    return pl.pallas_call(
        body,
        name="post_attn_fwd",
        grid=(T // tm,),
        in_specs=[
            pl.BlockSpec((tm, FOX_W), lambda i: (i, 0)),
            pl.BlockSpec((tm, FOX_W), lambda i: (i, 0)),
            pl.BlockSpec((1, FOX_W), lambda i: (0, 0)),
            pl.BlockSpec((1, FOX_W), lambda i: (0, 0)),
            pl.BlockSpec((D, D), lambda i: (0, 0)),
            pl.BlockSpec((tm, D), lambda i: (i, 0)),
        ],
        out_specs=[
            pl.BlockSpec((tm, D), lambda i: (i, 0)),
            pl.BlockSpec((tm, D), lambda i: (i, 0)),
        ],
        out_shape=[jax.ShapeDtypeStruct((T, D), F32), jax.ShapeDtypeStruct((T, D), BF16)],
        compiler_params=_cparams(("arbitrary",)),
    )(fox_o, sb_o, gf, gs, w_out, x)


def _mem_kv_fwd(mem, gm, w_mkv):
    M, D = mem.shape
    N = w_mkv.shape[1]

    def body(mem_ref, g_ref, w_ref, m_ref, kv_ref):
        mv = mem_ref[...]
        m = (mv * _rstd(mv) * g_ref[...]).astype(BF16)
        m_ref[...] = m
        for n0 in range(0, N, 512):
            kv_ref[:, n0:n0 + 512] = _nn(m, w_ref[:, n0:n0 + 512]).astype(BF16)

    return pl.pallas_call(
        body,
        name="mem_kv_fwd",
        out_shape=[jax.ShapeDtypeStruct((M, D), BF16), jax.ShapeDtypeStruct((M, N), BF16)],
        compiler_params=_cparams(),
    )(mem, gm, w_mkv)


def _xattn_probs(qb, kv, h):
    k = kv[:, h * MEM_HD:(h + 1) * MEM_HD]
    s = _nt(qb[:, h * MEM_HD:(h + 1) * MEM_HD], k) * (MEM_HD ** -0.5)
    s = s - jnp.max(s, axis=1, keepdims=True)
    p = jnp.exp(s)
    return p / jnp.sum(p, axis=1, keepdims=True)


def _xattn_fwd(x1, g2, w_mq, kv, w_mo, tm):
    T, D = x1.shape
    M = kv.shape[0]

    def body(x_ref, g_ref, wq_ref, kv_ref, wo_ref, x2_ref, h_ref, q_ref, om_ref):
        xv = x_ref[...]
        h = (xv * _rstd(xv) * g_ref[...]).astype(BF16)
        h_ref[...] = h
        q_ref[...] = _nn(h, wq_ref[...]).astype(BF16)
        qb = q_ref[...]
        kvv = kv_ref[...]
        for hd in range(N_MEM_HEADS):
            p = _xattn_probs(qb, kvv, hd)
            v = kvv[:, D + hd * MEM_HD:D + (hd + 1) * MEM_HD]
            om_ref[:, hd * MEM_HD:(hd + 1) * MEM_HD] = _nn(p.astype(BF16), v).astype(BF16)
        x2_ref[...] = xv + _nn(om_ref[...], wo_ref[...])

    return pl.pallas_call(
        body,
        name="xattn_fwd",
        grid=(T // tm,),
        in_specs=[
            pl.BlockSpec((tm, D), lambda i: (i, 0)),
            pl.BlockSpec((1, D), lambda i: (0, 0)),
            pl.BlockSpec((D, D), lambda i: (0, 0)),
            pl.BlockSpec((M, 2 * D), lambda i: (0, 0)),
            pl.BlockSpec((D, D), lambda i: (0, 0)),
        ],
        out_specs=[pl.BlockSpec((tm, D), lambda i: (i, 0))] * 4,
        out_shape=[jax.ShapeDtypeStruct((T, D), F32)] + [jax.ShapeDtypeStruct((T, D), BF16)] * 3,
        compiler_params=_cparams(("arbitrary",)),
    )(x1, g2, w_mq, kv, w_mo)


def _conv_taps(ext_ref, tm, back):
    if back:
        return ext_ref[pl.ds(6, tm), :], ext_ref[pl.ds(7, tm), :], ext_ref[pl.ds(8, tm), :]
    return ext_ref[pl.ds(0, tm), :], ext_ref[pl.ds(1, tm), :], ext_ref[pl.ds(2, tm), :]


def _ffn_fwd(x2, g3, w_up, conv_w, conv_b, w_down, tm):
    T, D = x2.shape
    fc = FF_CHUNK
    nj = D_FF // fc

    def body(x_ref, g_ref, wg_ref, wv_ref, cwg_ref, cwv_ref, cbg_ref, cbv_ref, wd_ref,
             x3_ref, h_ref, ug_ref, uv_ref, yg_ref, yv_ref, a_ref, acc_ref, carry_ref, ext_ref):
        i = pl.program_id(0)
        j = pl.program_id(1)

        @pl.when(j == 0)
        def _():
            xv = x_ref[...]
            h_ref[...] = (xv * _rstd(xv) * g_ref[...]).astype(BF16)
            acc_ref[...] = xv

        @pl.when(i == 0)
        def _():
            carry_ref[j] = jnp.zeros((2, 8, fc), F32)

        h = h_ref[...]
        halves = []
        for part, (w_ref, cw_ref, cb_ref, u_ref, y_ref) in enumerate(
                ((wg_ref, cwg_ref, cbg_ref, ug_ref, yg_ref), (wv_ref, cwv_ref, cbv_ref, uv_ref, yv_ref))):
            u = _nn(h, w_ref[...])
            u_ref[...] = u.astype(BF16)
            ext = ext_ref.at[part]
            ext[pl.ds(0, 8), :] = carry_ref[j, part]
            ext[pl.ds(8, tm), :] = u
            carry_ref[j, part] = u[tm - 8:, :]
            u2, u1, u0 = _conv_taps(ext, tm, True)
            cw = cw_ref[...]
            y = cb_ref[...] + cw[0:1] * u2 + cw[1:2] * u1 + cw[2:3] * u0
            y_ref[...] = y.astype(BF16)
            halves.append(y)
        gate, val = halves
        a = (gate * jax.nn.sigmoid(gate) * val).astype(BF16)
        a_ref[...] = a
        acc_ref[...] += _nn(a, wd_ref[...])

        @pl.when(j == nj - 1)
        def _():
            x3_ref[...] = acc_ref[...]

    return pl.pallas_call(
        body,
        name="ffn_fwd",
        grid=(T // tm, nj),
        in_specs=[
            pl.BlockSpec((tm, D), lambda i, j: (i, 0)),
            pl.BlockSpec((1, D), lambda i, j: (0, 0)),
            pl.BlockSpec((D, fc), lambda i, j: (0, j)),
            pl.BlockSpec((D, fc), lambda i, j: (0, nj + j)),
            pl.BlockSpec((3, fc), lambda i, j: (0, j)),
            pl.BlockSpec((3, fc), lambda i, j: (0, nj + j)),
            pl.BlockSpec((1, fc), lambda i, j: (0, j)),
            pl.BlockSpec((1, fc), lambda i, j: (0, nj + j)),
            pl.BlockSpec((fc, D), lambda i, j: (j, 0)),
        ],
        out_specs=[
            pl.BlockSpec((tm, D), lambda i, j: (i, 0)),
            pl.BlockSpec((tm, D), lambda i, j: (i, 0)),
        ] + [pl.BlockSpec((tm, fc), lambda i, j: (i, j))] * 5,
        out_shape=[
            jax.ShapeDtypeStruct((T, D), F32),
            jax.ShapeDtypeStruct((T, D), BF16),
        ] + [jax.ShapeDtypeStruct((T, D_FF), BF16)] * 5,
        scratch_shapes=[
            pltpu.VMEM((tm, D), F32),
            pltpu.VMEM((nj, 2, 8, fc), F32),
            pltpu.VMEM((2, tm + 8, fc), F32),
        ],
        compiler_params=_cparams(("arbitrary", "arbitrary")),
    )(x2, g3, w_up, w_up, conv_w, conv_w, conv_b, conv_b, w_down)


def _loss_head(x3, gfin, target, tm):
    T, D = x3.shape

    def body(x_ref, g_ref, t_ref, dx_ref, loss_ref, dg_ref):
        i = pl.program_id(0)

        @pl.when(i == 0)
        def _():
            loss_ref[...] = jnp.zeros_like(loss_ref)
            dg_ref[...] = jnp.zeros_like(dg_ref)

        xv = x_ref[...]
        g = g_ref[...]
        r = _rstd(xv)
        xhat = xv * r
        err = xhat * g - t_ref[...]
        part = jnp.sum(jnp.sum(err * err, axis=1, keepdims=True), axis=0, keepdims=True) * (0.5 / D)
        loss_ref[...] += jnp.broadcast_to(part, loss_ref.shape)
        dy = err * (1.0 / D)
        dg_ref[...] += jnp.sum(dy * xhat, axis=0, keepdims=True)
        dxhat = dy * g
        dx_ref[...] = r * (dxhat - xhat * jnp.mean(dxhat * xhat, axis=-1, keepdims=True))

    return pl.pallas_call(
        body,
        name="loss_head",
        grid=(T // tm,),
        in_specs=[
            pl.BlockSpec((tm, D), lambda i: (i, 0)),
            pl.BlockSpec((1, D), lambda i: (0, 0)),
            pl.BlockSpec((tm, D), lambda i: (i, 0)),
        ],
        out_specs=[
            pl.BlockSpec((tm, D), lambda i: (i, 0)),
            pl.BlockSpec((8, 128), lambda i: (0, 0)),
            pl.BlockSpec((1, D), lambda i: (0, 0)),
        ],
        out_shape=[
            jax.ShapeDtypeStruct((T, D), F32),
            jax.ShapeDtypeStruct((8, 128), F32),
            jax.ShapeDtypeStruct((1, D), F32),
        ],
        compiler_params=_cparams(("arbitrary",)),
    )(x3, gfin, target)


def _ffn_bwd(dx3, x2, g3, ug, uv, yg, yv, conv_w, w_down, w_up, tm):
    T, D = x2.shape
    fc = FF_CHUNK
    nj = D_FF // fc
    nt = T // tm

    def rev(i):
        return nt - 1 - i

    def body(dx3_ref, x_ref, g_ref, ug_ref, uv_ref, yg_ref, yv_ref, cwg_ref, cwv_ref,
             wd_ref, wug_ref, wuv_ref,
             dx2_ref, dug_ref, duv_ref, dg_ref, dcg_ref, dcv_ref,
             acc_ref, carry_ref, ext_ref):
        i = pl.program_id(0)
        j = pl.program_id(1)
        cols = pl.ds(pl.multiple_of(j * fc, fc), fc)

        @pl.when(j == 0)
        def _():
            acc_ref[...] = jnp.zeros_like(acc_ref)

        @pl.when((i == 0) & (j == 0))
        def _():
            dg_ref[...] = jnp.zeros_like(dg_ref)
            dcg_ref[...] = jnp.zeros_like(dcg_ref)
            dcv_ref[...] = jnp.zeros_like(dcv_ref)

        @pl.when(i == 0)
        def _():
            carry_ref[j] = jnp.zeros((2, 8, fc), F32)

        da = _nt(dx3_ref[...].astype(BF16), wd_ref[...])
        gate = yg_ref[...].astype(F32)
        val = yv_ref[...].astype(F32)
        sig = jax.nn.sigmoid(gate)
        silu = gate * sig
        dys = (da * val * (sig * (1.0 + gate * (1.0 - sig))), da * silu)
        for part, (dy, u_ref, cw_ref, du_ref, wu_ref, dc_ref) in enumerate(
                ((dys[0], ug_ref, cwg_ref, dug_ref, wug_ref, dcg_ref),
                 (dys[1], uv_ref, cwv_ref, duv_ref, wuv_ref, dcv_ref))):
            ext = ext_ref.at[part]
            ext[pl.ds(0, tm), :] = dy
            ext[pl.ds(tm, 8), :] = carry_ref[j, part]
            carry_ref[j, part] = dy[:8, :]
            d0, d1, d2 = _conv_taps(ext, tm, False)
            u = u_ref[...].astype(F32)
            upd = jnp.concatenate([
                jnp.sum(u * d2, axis=0, keepdims=True),
                jnp.sum(u * d1, axis=0, keepdims=True),
                jnp.sum(u * d0, axis=0, keepdims=True),
                jnp.sum(d0, axis=0, keepdims=True),
                jnp.zeros((4, fc), F32)], axis=0)
            dc_ref[:, cols] += upd
            cw = cw_ref[...]
            du = (cw[2:3] * d0 + cw[1:2] * d1 + cw[0:1] * d2).astype(BF16)
            du_ref[...] = du
            acc_ref[...] += _nt(du, wu_ref[...])

        @pl.when(j == nj - 1)
        def _():
            dx, dg = _norm_bwd(x_ref[...], g_ref[...], acc_ref[...])
            dx2_ref[...] = dx3_ref[...] + dx
            dg_ref[...] += dg

    return pl.pallas_call(
        body,
        name="ffn_bwd",
        grid=(nt, nj),
        in_specs=[
            pl.BlockSpec((tm, D), lambda i, j: (rev(i), 0)),
            pl.BlockSpec((tm, D), lambda i, j: (rev(i), 0)),
            pl.BlockSpec((1, D), lambda i, j: (0, 0)),
            pl.BlockSpec((tm, fc), lambda i, j: (rev(i), j)),
            pl.BlockSpec((tm, fc), lambda i, j: (rev(i), j)),
            pl.BlockSpec((tm, fc), lambda i, j: (rev(i), j)),
            pl.BlockSpec((tm, fc), lambda i, j: (rev(i), j)),
            pl.BlockSpec((3, fc), lambda i, j: (0, j)),
            pl.BlockSpec((3, fc), lambda i, j: (0, nj + j)),
            pl.BlockSpec((fc, D), lambda i, j: (j, 0)),
            pl.BlockSpec((D, fc), lambda i, j: (0, j)),
            pl.BlockSpec((D, fc), lambda i, j: (0, nj + j)),
        ],
        out_specs=[
            pl.BlockSpec((tm, D), lambda i, j: (rev(i), 0)),
            pl.BlockSpec((tm, fc), lambda i, j: (rev(i), j)),
            pl.BlockSpec((tm, fc), lambda i, j: (rev(i), j)),
            pl.BlockSpec((1, D), lambda i, j: (0, 0)),
            pl.BlockSpec((8, D_FF), lambda i, j: (0, 0)),
            pl.BlockSpec((8, D_FF), lambda i, j: (0, 0)),
        ],
        out_shape=[
            jax.ShapeDtypeStruct((T, D), F32),
            jax.ShapeDtypeStruct((T, D_FF), BF16),
            jax.ShapeDtypeStruct((T, D_FF), BF16),
            jax.ShapeDtypeStruct((1, D), F32),
            jax.ShapeDtypeStruct((8, D_FF), F32),
            jax.ShapeDtypeStruct((8, D_FF), F32),
        ],
        scratch_shapes=[
            pltpu.VMEM((tm, D), F32),
            pltpu.VMEM((nj, 2, 8, fc), F32),
            pltpu.VMEM((2, tm + 8, fc), F32),
        ],
        compiler_params=_cparams(("arbitrary", "arbitrary")),
    )(dx3, x2, g3, ug, uv, yg, yv, conv_w, conv_w, w_down, w_up, w_up)


def _xattn_bwd(dx2, x1, g2, qb, kv, w_mo, w_mq, tm):
    T, D = x1.shape
    M = kv.shape[0]

    def body(dx2_ref, x_ref, g_ref, q_ref, kv_ref, wo_ref, wq_ref, dx1_ref, dq_ref, dkv_ref, dg_ref):
        i = pl.program_id(0)

        @pl.when(i == 0)
        def _():
            dkv_ref[...] = jnp.zeros_like(dkv_ref)
            dg_ref[...] = jnp.zeros_like(dg_ref)

        dxv = dx2_ref[...]
        dom = _nt(dxv.astype(BF16), wo_ref[...]).astype(BF16)
        qb_ = q_ref[...]
        kvv = kv_ref[...]
        for hd in range(N_MEM_HEADS):
            sl = slice(hd * MEM_HD, (hd + 1) * MEM_HD)
            vsl = slice(D + hd * MEM_HD, D + (hd + 1) * MEM_HD)
            p = _xattn_probs(qb_, kvv, hd)
            dp = _nt(dom[:, sl], kvv[:, vsl])
            ds = (p * (dp - jnp.sum(p * dp, axis=1, keepdims=True)) * (MEM_HD ** -0.5)).astype(BF16)
            dq_ref[:, sl] = _nn(ds, kvv[:, sl]).astype(BF16)
            dkv_ref[:, sl] += _tn(ds, qb_[:, sl])
            dkv_ref[:, vsl] += _tn(p.astype(BF16), dom[:, sl])
        dh = _nt(dq_ref[...], wq_ref[...])
        dx, dg = _norm_bwd(x_ref[...], g_ref[...], dh)
        dx1_ref[...] = dxv + dx
        dg_ref[...] += dg

    return pl.pallas_call(
        body,
        name="xattn_bwd",
        grid=(T // tm,),
        in_specs=[
            pl.BlockSpec((tm, D), lambda i: (i, 0)),
            pl.BlockSpec((tm, D), lambda i: (i, 0)),
            pl.BlockSpec((1, D), lambda i: (0, 0)),
            pl.BlockSpec((tm, D), lambda i: (i, 0)),
            pl.BlockSpec((M, 2 * D), lambda i: (0, 0)),
            pl.BlockSpec((D, D), lambda i: (0, 0)),
            pl.BlockSpec((D, D), lambda i: (0, 0)),
        ],
        out_specs=[
            pl.BlockSpec((tm, D), lambda i: (i, 0)),
            pl.BlockSpec((tm, D), lambda i: (i, 0)),
            pl.BlockSpec((M, 2 * D), lambda i: (0, 0)),
            pl.BlockSpec((1, D), lambda i: (0, 0)),
        ],
        out_shape=[
            jax.ShapeDtypeStruct((T, D), F32),
            jax.ShapeDtypeStruct((T, D), BF16),
            jax.ShapeDtypeStruct((M, 2 * D), F32),
            jax.ShapeDtypeStruct((1, D), F32),
        ],
        compiler_params=_cparams(("arbitrary",)),
    )(dx2, x1, g2, qb, kv, w_mo, w_mq)


def _mem_kv_bwd(mem, gm, mb, dkv, w_mkv):
    M, D = mem.shape
    N = dkv.shape[1]

    def body(mem_ref, g_ref, m_ref, dkv_ref, w_ref, dw_ref, dg_ref):
        dkvb = dkv_ref[...].astype(BF16)
        for n0 in range(0, N, 512):
            dw_ref[:, n0:n0 + 512] = _tn(m_ref[...], dkvb[:, n0:n0 + 512]).astype(BF16)
        dm = _nt(dkvb, w_ref[...])
        mv = mem_ref[...]
        dg_ref[...] = jnp.sum(dm * (mv * _rstd(mv)), axis=0, keepdims=True)

    return pl.pallas_call(
        body,
        name="mem_kv_bwd",
        out_shape=[jax.ShapeDtypeStruct((D, N), BF16), jax.ShapeDtypeStruct((1, D), F32)],
        compiler_params=_cparams(),
    )(mem, gm, mb, dkv, w_mkv)


def _post_attn_bwd(dx1, fox_o, sb_o, gf, gs, w_out, tm):
    T, D = dx1.shape

    def body(dx_ref, f_ref, s_ref, gf_ref, gs_ref, w_ref, df_ref, ds_ref, dgf_ref, dgs_ref):
        i = pl.program_id(0)

        @pl.when(i == 0)
        def _():
            dgf_ref[...] = jnp.zeros_like(dgf_ref)
            dgs_ref[...] = jnp.zeros_like(dgs_ref)

        dmix = _nt(dx_ref[...].astype(BF16), w_ref[...])
        d, dg = _norm_bwd(f_ref[...], gf_ref[...], dmix[:, :FOX_W])
        df_ref[...] = d
        dgf_ref[...] += dg
        d, dg = _norm_bwd(s_ref[...], gs_ref[...], dmix[:, FOX_W:])
        ds_ref[...] = d
        dgs_ref[...] += dg

    return pl.pallas_call(
        body,
        name="post_attn_bwd",
        grid=(T // tm,),
        in_specs=[
            pl.BlockSpec((tm, D), lambda i: (i, 0)),
            pl.BlockSpec((tm, FOX_W), lambda i: (i, 0)),
            pl.BlockSpec((tm, FOX_W), lambda i: (i, 0)),
            pl.BlockSpec((1, FOX_W), lambda i: (0, 0)),
            pl.BlockSpec((1, FOX_W), lambda i: (0, 0)),
            pl.BlockSpec((D, D), lambda i: (0, 0)),
        ],
        out_specs=[
            pl.BlockSpec((tm, FOX_W), lambda i: (i, 0)),
            pl.BlockSpec((tm, FOX_W), lambda i: (i, 0)),
            pl.BlockSpec((1, FOX_W), lambda i: (0, 0)),
            pl.BlockSpec((1, FOX_W), lambda i: (0, 0)),
        ],
        out_shape=[
            jax.ShapeDtypeStruct((T, FOX_W), F32),
            jax.ShapeDtypeStruct((T, FOX_W), F32),
            jax.ShapeDtypeStruct((1, FOX_W), F32),
            jax.ShapeDtypeStruct((1, FOX_W), F32),
        ],
        compiler_params=_cparams(("arbitrary",)),
    )(dx1, fox_o, sb_o, gf, gs, w_out)


def _sb_bwd(proj, ltot, live, d_o, tq):
    T = proj.shape[0]
    nq = T // tq

    def body(q_ref, k_ref, v_ref, lt_ref, live_ref, do_ref, dq_ref, dk_ref, dv_ref,
             qh_s, doh_s, lt_s, z_s, da_s, ab_s, dzb_s, run_s, runw_s, dq_s, qt_s, dot_s, dkt_s, dvt_s):
        i = pl.program_id(1)

        @pl.when(i == 0)
        def _():
            dkt_s[...] = jnp.zeros_like(dkt_s)
            dvt_s[...] = jnp.zeros_like(dvt_s)

        lane = lax.broadcasted_iota(jnp.int32, (1, 128), 1)
        row = lax.broadcasted_iota(jnp.int32, (tq, tq), 0)
        col = lax.broadcasted_iota(jnp.int32, (tq, tq), 1)
        strict = col < row
        upto = jnp.where(row <= col, 1.0, 0.0).astype(BF16)
        before = jnp.where(row < col, 1.0, 0.0).astype(BF16)
        q = q_ref[...]
        dov = do_ref[...]
        for hh in range(2):
            qh, hmask = _head_q(q, hh, lane)
            qh_s[hh] = -qh
            doh_s[hh] = jnp.where(hmask, dov, 0.0).astype(BF16)
            lt_s[hh] = jnp.broadcast_to(_lanes_to_rows(lt_ref[hh], row == col), (tq, 128))
        qt_s[...] = (q.astype(F32) * -(HEAD_DIM ** -0.5)).T.astype(BF16)
        dot_s[...] = dov.astype(F32).T.astype(BF16)
        run_s[...] = jnp.zeros_like(run_s)
        runw_s[...] = jnp.zeros_like(runw_s)
        dq_s[...] = jnp.zeros_like(dq_s)

        at = lax.broadcasted_iota(jnp.int32, (1, nq), 1)
        count = jnp.sum(jnp.where(at == i, live_ref[0], 0.0), axis=1, keepdims=True)[0, 0].astype(jnp.int32)
        n_live = jnp.clip(count, 1, i + 1)
        oldest = i + 1 - n_live

        def rows(t):
            return pl.ds(pl.multiple_of((oldest + t) * tq, tq), tq)

        def stage_a(t, slot):
            k = k_ref[rows(t), :]
            v = v_ref[rows(t), :]
            for hh in range(2):
                z_s[slot, hh] = _nt(qh_s[hh], k)
                da_s[slot, hh] = _nt(doh_s[hh], v)

        def stage_b(t, slot, diag):
            for hh in range(2):
                L, g = _sb_logs(z_s[slot, hh], strict if diag else None)
                upto_s = _split_dot(L, upto, SB_SUM_TERMS)
                run = run_s[hh]
                arg = (g + _lanes2(lt_s[hh] - run)) - upto_s
                if diag:
                    arg = jnp.where(strict, arg, NEG)
                a = jnp.exp(arg)
                w = a * da_s[slot, hh]
                w_before = _split_dot(w, before, SB_SUM_TERMS)
                run_w = runw_s[hh]
                d_keep = w_before + _lanes2(run_w)
                beta = jnp.exp(g)
                ndz = beta * (w + d_keep) - w
                if diag:
                    ndz = jnp.where(strict, ndz, 0.0)
                dzb_s[slot, hh] = ndz.astype(BF16)
                ab_s[slot, hh] = a.astype(BF16)
                run_s[hh] = run + jnp.broadcast_to(upto_s[:, tq - 1:tq], (tq, 128))
                runw_s[hh] = run_w + jnp.broadcast_to(w_before[:, tq - 1:tq] + w[:, tq - 1:tq], (tq, 128))

        def stage_c(t, slot):
            k = k_ref[rows(t), :]
            for hh in range(2):
                dzb = dzb_s[slot, hh]
                dq_s[hh] += _nn(dzb, k)
                dims = pl.ds(HEAD_DIM * hh, HEAD_DIM)
                dkt_s[oldest + t, dims, :] += _nn(qt_s[dims, :], dzb)
                dvt_s[oldest + t, dims, :] += _nn(dot_s[dims, :], ab_s[slot, hh])

        _pipeline3(n_live, stage_a, stage_b, stage_c, True)
        dq_ref[...] = (jnp.where(lane < HEAD_DIM, dq_s[0], dq_s[1]) * -(HEAD_DIM ** -0.5)).astype(BF16)

        @pl.when(i == nq - 1)
        def _():
            def flush(n, carry):
                keys = pl.ds(pl.multiple_of(n * tq, tq), tq)
                dk_ref[keys, :] = dkt_s[n].T
                dv_ref[keys, :] = dvt_s[n].T
                return carry

            lax.fori_loop(0, nq, flush, 0)

    return pl.pallas_call(
        body,
        name="sb_bwd",
        grid=(4, nq),
        in_specs=[
            pl.BlockSpec((tq, 128), lambda p, i: (i, 12 + p)),
            pl.BlockSpec((T, 128), lambda p, i: (0, 16 + p)),
            pl.BlockSpec((T, 128), lambda p, i: (0, 20 + p)),
            pl.BlockSpec((2, 1, tq), lambda p, i: (p, 0, i)),
            pl.BlockSpec((1, 1, nq), lambda p, i: (p, 0, 0)),
            pl.BlockSpec((tq, 128), lambda p, i: (i, p)),
        ],
        out_specs=[
            pl.BlockSpec((tq, 128), lambda p, i: (i, p)),
            pl.BlockSpec((T, 128), lambda p, i: (0, p)),
            pl.BlockSpec((T, 128), lambda p, i: (0, p)),
        ],
        out_shape=[
            jax.ShapeDtypeStruct((T, FOX_W), BF16),
            jax.ShapeDtypeStruct((T, FOX_W), F32),
            jax.ShapeDtypeStruct((T, FOX_W), F32),
        ],
        scratch_shapes=[
            pltpu.VMEM((2, tq, 128), BF16),
            pltpu.VMEM((2, tq, 128), BF16),
            pltpu.VMEM((2, tq, 128), F32),
            pltpu.VMEM((2, 2, tq, tq), F32),
            pltpu.VMEM((2, 2, tq, tq), F32),
            pltpu.VMEM((2, 2, tq, tq), BF16),
            pltpu.VMEM((2, 2, tq, tq), BF16),
            pltpu.VMEM((2, tq, 128), F32),
            pltpu.VMEM((2, tq, 128), F32),
            pltpu.VMEM((2, tq, 128), F32),
            pltpu.VMEM((128, tq), BF16),
            pltpu.VMEM((128, tq), BF16),
            pltpu.VMEM((nq, 128, tq), F32),
            pltpu.VMEM((nq, 128, tq), F32),
        ],
        compiler_params=_cparams(("arbitrary", "arbitrary")),
    )(proj, proj, proj, ltot, live, d_o)


def _fox_bwd(proj, c_col, c_row, c_ends, lse, d_o, o, tq, scatter=()):
    T = proj.shape[0]
    nq = T // tq
    ns = len(scatter)

    def body(*refs):
        q_ref, k_ref, v_ref, cq_ref, ck_ref, cke_ref, lse_ref, do_ref, o_ref = refs[:9]
        dq_ref, dk_ref, dv_ref, dck_ref, dcq_ref = refs[9 + ns:14 + ns]
        (qh_s, doh_s, delta_s, shift_s, z_s, dp_s, pb_s, dsb_s, rs_s, dq_s,
         kn_s, qt_s, dot_s, dkt_s, dvt_s, kt_s) = refs[14 + 2 * ns:30 + 2 * ns]
        i = pl.program_id(1)
        if ns:
            pair = pl.program_id(0)
            finish = _ride_along(_Scatter(refs[9:9 + ns], refs[14 + ns:14 + 2 * ns], *refs[30 + 2 * ns:]),
                                 (pair == 0) & (i == 0), None, (pair == 3) & (i == nq - 1))
        lane = lax.broadcasted_iota(jnp.int32, (1, 128), 1)

        @pl.when(i == 0)
        def _():
            dkt_s[...] = jnp.zeros_like(dkt_s)
            dvt_s[...] = jnp.zeros_like(dvt_s)
            dck_ref[...] = jnp.zeros_like(dck_ref)
            _fox_key_norms(k_ref, kn_s, lane)

            def turn(n, carry):
                kt_s[n] = k_ref[pl.ds(pl.multiple_of(n * tq, tq), tq), :].astype(F32).T.astype(BF16)
                return carry

            lax.fori_loop(0, nq, turn, 0)

        row = lax.broadcasted_iota(jnp.int32, (tq, tq), 0)
        col = lax.broadcasted_iota(jnp.int32, (tq, tq), 1)
        q = q_ref[...]
        dov = do_ref[...]
        ov = o_ref[...]
        qt_s[...] = (q.astype(F32) * (HEAD_DIM ** -0.5)).T.astype(BF16)
        dot_s[...] = dov.astype(F32).T.astype(BF16)
        for hh in range(2):
            qh, hmask = _head_q(q, hh, lane)
            dohb = jnp.where(hmask, dov, 0.0).astype(BF16)
            qh_s[hh] = qh
            doh_s[hh] = dohb
            delta_s[hh] = jnp.broadcast_to(jnp.sum(dohb.astype(F32) * ov, axis=1, keepdims=True), (tq, 128))
            shift_s[hh] = jnp.broadcast_to(_lanes_to_rows(cq_ref[hh] - lse_ref[hh], row == col), (tq, 128))
        rs_s[...] = jnp.zeros_like(rs_s)
        dq_s[...] = jnp.zeros_like(dq_s)

        def rows(t):
            return pl.ds(pl.multiple_of((i - t) * tq, tq), tq)

        def stage_a(t, slot):
            k = k_ref[rows(t), :]
            v = v_ref[rows(t), :]
            for hh in range(2):
                z_s[slot, hh] = _nt(qh_s[hh], k)
                dp_s[slot, hh] = _nt(doh_s[hh], v)

        def stage_b(t, slot, diag):
            for hh in range(2):
                s = z_s[slot, hh] + _lanes2(shift_s[hh]) - ck_ref[hh, :, rows(t)]
                if diag:
                    s = jnp.where(col <= row, s, NEG)
                p = jnp.exp(s)
                ds = p * (dp_s[slot, hh] - _lanes2(delta_s[hh]))
                pb_s[slot, hh] = p.astype(BF16)
                dsb_s[slot, hh] = ds.astype(BF16)
                dck_ref[hh, :, rows(t)] += jnp.sum(ds, axis=0, keepdims=True)
                rs_s[hh] += jnp.sum(ds, axis=1, keepdims=True)

        def stage_c(t, slot):
            for hh in range(2):
                dsb = dsb_s[slot, hh]
                dims = pl.ds(HEAD_DIM * hh, HEAD_DIM)
                dq_s[dims, :] += _nt(kt_s[i - t, dims, :], dsb)
                dkt_s[i - t, dims, :] += _nn(qt_s[dims, :], dsb)
                dvt_s[i - t, dims, :] += _nn(dot_s[dims, :], pb_s[slot, hh])

        _pipeline3(_fox_live_blocks(i, qh_s, kn_s, cq_ref, cke_ref), stage_a, stage_b, stage_c, False,
                   a_first=True)

        @pl.when(i == nq - 1)
        def _():
            def flush(n, carry):
                keys = pl.ds(pl.multiple_of(n * tq, tq), tq)
                dk_ref[keys, :] = dkt_s[n].T
                dv_ref[keys, :] = dvt_s[n].T
                return carry

            lax.fori_loop(0, nq, flush, 0)
        dcq_ref[0] = _rows_to_lanes(rs_s[0], row == col)
        dcq_ref[1] = _rows_to_lanes(rs_s[1], row == col)
        dq_ref[...] = (dq_s[...].T * (HEAD_DIM ** -0.5)).astype(BF16)
        if ns:
            finish()

    res = pl.pallas_call(
        body,
        name="fox_bwd",
        grid=(4, nq),
        in_specs=[
            pl.BlockSpec((tq, 128), lambda p, i: (i, p)),
            pl.BlockSpec((T, 128), lambda p, i: (0, 4 + p)),
            pl.BlockSpec((T, 128), lambda p, i: (0, 8 + p)),
            pl.BlockSpec((2, 1, tq), lambda p, i: (p, 0, i)),
            pl.BlockSpec((2, 1, T), lambda p, i: (p, 0, 0)),
            pl.BlockSpec((2, 1, nq), lambda p, i: (p, 0, 0)),
            pl.BlockSpec((2, 1, tq), lambda p, i: (p, 0, i)),
            pl.BlockSpec((tq, 128), lambda p, i: (i, p)),
            pl.BlockSpec((tq, 128), lambda p, i: (i, p)),
        ] + [_ANY] * ns,
        out_specs=[
            pl.BlockSpec((tq, 128), lambda p, i: (i, p)),
            pl.BlockSpec((T, 128), lambda p, i: (0, p)),
            pl.BlockSpec((T, 128), lambda p, i: (0, p)),
            pl.BlockSpec((2, 1, T), lambda p, i: (p, 0, 0)),
            pl.BlockSpec((2, 1, tq), lambda p, i: (p, 0, i)),
        ] + [_ANY] * ns,
        out_shape=[
            jax.ShapeDtypeStruct((T, FOX_W), BF16),
            jax.ShapeDtypeStruct((T, FOX_W), F32),
            jax.ShapeDtypeStruct((T, FOX_W), F32),
            jax.ShapeDtypeStruct((N_FOX, 1, T), F32),
            jax.ShapeDtypeStruct((N_FOX, 1, T), F32),
        ] + [jax.ShapeDtypeStruct(b.shape, b.dtype) for b in scatter],
        scratch_shapes=[
            pltpu.VMEM((2, tq, 128), BF16),
            pltpu.VMEM((2, tq, 128), BF16),
            pltpu.VMEM((2, tq, 128), F32),
            pltpu.VMEM((2, tq, 128), F32),
            pltpu.VMEM((2, 2, tq, tq), F32),
            pltpu.VMEM((2, 2, tq, tq), F32),
            pltpu.VMEM((2, 2, tq, tq), BF16),
            pltpu.VMEM((2, 2, tq, tq), BF16),
            pltpu.VMEM((2, tq, 128), F32),
            pltpu.VMEM((128, tq), F32),
            pltpu.VMEM((2, 8, 128), F32),
            pltpu.VMEM((128, tq), BF16),
            pltpu.VMEM((128, tq), BF16),
            pltpu.VMEM((nq, 128, tq), F32),
            pltpu.VMEM((nq, 128, tq), F32),
            pltpu.VMEM((nq, 128, tq), BF16),
        ] + (_comm_sems(ns) if ns else []),
        compiler_params=_cparams(("arbitrary", "arbitrary")),
    )(proj, proj, proj, c_col, c_row, c_ends, lse, d_o, o, *scatter)
    res = list(res)
    return (*res[:5], res[5:])


def _forget_bwd(dcq, dck, xf, h1, tc):
    H, T = xf.shape
    D = h1.shape[1]
    nc = T // tc

    def body(dcq_ref, dck_ref, xf_ref, h_ref, dxf_ref, db_ref, dwf_ref):
        row = lax.broadcasted_iota(jnp.int32, (tc, tc), 0)
        col = lax.broadcasted_iota(jnp.int32, (tc, tc), 1)
        from_here = jnp.where(row >= col, 1.0, 0.0).astype(BF16)

        def chunk(n, carry):
            run, db, dwf = carry
            cs = pl.multiple_of((nc - 1 - n) * tc, tc)
            dc = dcq_ref[:, pl.ds(cs, tc)] - dck_ref[:, pl.ds(cs, tc)]
            dlogf = _split_dot(dc, from_here, 3) + run
            xfv = xf_ref[:, pl.ds(cs, tc)]
            dxf = dlogf * jax.nn.sigmoid(-xfv)
            dxf_ref[:, pl.ds(cs, tc)] = dxf
            dwf = dwf + _nn(dxf.astype(BF16), h_ref[pl.ds(cs, tc), :])
            return dlogf[:, 0:1], db + jnp.sum(dxf, axis=1, keepdims=True), dwf

        zero = jnp.zeros((H, 1), F32)
        _, db, dwf = lax.fori_loop(0, nc, chunk, (zero, zero, jnp.zeros((H, D), F32)))
        db_ref[...] = db
        dwf_ref[...] = dwf

    return pl.pallas_call(
        body,
        name="forget_bwd",
        out_shape=[jax.ShapeDtypeStruct((H, T), F32), jax.ShapeDtypeStruct((H, 1), F32),
                   jax.ShapeDtypeStruct((H, D), F32)],
        compiler_params=_cparams(),
    )(dcq, dck, xf, h1)


def _inproj_bwd(pieces, dxf_t, w_in, w_f_t, x, g1, dx1, tm, scatter=()):
    T, D = x.shape
    N = w_in.shape[1]
    ns = len(scatter)
    nt = T // tm
    npc = len(pieces)

    def body(*refs):
        pc_refs = refs[:npc]
        dxf_ref, w_ref, wf_ref, x_ref, g_ref, dx1_ref = refs[npc:npc + 6]
        base = npc + 6
        dx_ref, dg_ref = refs[base + ns:base + 2 + ns]
        i = pl.program_id(0)
        if ns:
            exchange = _Scatter(refs[base:base + ns], refs[base + 2 + ns:base + 2 + 2 * ns],
                                *refs[base + 2 + 2 * ns:])

            @pl.when(i == 0)
            def _():
                exchange.start()

        @pl.when(i == 0)
        def _():
            dg_ref[...] = jnp.zeros_like(dg_ref)

        dh = _nn(dxf_ref[...], wf_ref[...].astype(F32))
        for k, pc_ref in enumerate(pc_refs):
            dh = dh + _nt(pc_ref[...].astype(BF16), w_ref[:, k * FOX_W:(k + 1) * FOX_W])
        dx, dg = _norm_bwd(x_ref[...], g_ref[...], dh)
        dx_ref[...] = dx1_ref[...] + dx
        dg_ref[...] += dg
        if ns:
            @pl.when(i == nt - 1)
            def _():
                exchange.finish()

    res = pl.pallas_call(
        body,
        name="inproj_bwd",
        grid=(nt,),
        in_specs=[pl.BlockSpec((tm, FOX_W), lambda i: (i, 0))] * npc + [
            pl.BlockSpec((tm, N_FOX), lambda i: (i, 0)),
            pl.BlockSpec((D, N), lambda i: (0, 0)),
            pl.BlockSpec((N_FOX, D), lambda i: (0, 0)),
            pl.BlockSpec((tm, D), lambda i: (i, 0)),
            pl.BlockSpec((1, D), lambda i: (0, 0)),
            pl.BlockSpec((tm, D), lambda i: (i, 0)),
        ] + [_ANY] * ns,
        out_specs=[
            pl.BlockSpec((tm, D), lambda i: (i, 0)),
            pl.BlockSpec((1, D), lambda i: (0, 0)),
        ] + [_ANY] * ns,
        out_shape=[jax.ShapeDtypeStruct((T, D), F32), jax.ShapeDtypeStruct((1, D), F32)]
        + [jax.ShapeDtypeStruct(b.shape, b.dtype) for b in scatter],
        scratch_shapes=_comm_sems(ns) if ns else [],
        compiler_params=_cparams(("arbitrary",)),
    )(*pieces, dxf_t, w_in, w_f_t, x, g1, dx1, *scatter)
    res = list(res)
    return res[0], res[1], res[2:]


def _dw_in(h1, pieces, name):
    T, K = h1.shape
    bt = min(T, 512)
    nt = T // bt
    npc = len(pieces)

    def body(*refs):
        a_ref = refs[0]
        pc_refs = refs[1:1 + npc]
        o_ref, acc_ref = refs[1 + npc:]
        t = pl.program_id(0)

        @pl.when(t == 0)
        def _():
            acc_ref[...] = jnp.zeros_like(acc_ref)

        a = a_ref[...]
        for k, pc_ref in enumerate(pc_refs):
            acc_ref[:, k * FOX_W:(k + 1) * FOX_W] += _tn(a, pc_ref[...].astype(BF16))

        @pl.when(t == nt - 1)
        def _():
            o_ref[...] = acc_ref[...].astype(BF16)

    return pl.pallas_call(
        body,
        name=name,
        grid=(nt,),
        in_specs=[pl.BlockSpec((bt, K), lambda t: (t, 0))] + [pl.BlockSpec((bt, FOX_W), lambda t: (t, 0))] * npc,
        out_specs=pl.BlockSpec((K, npc * FOX_W), lambda t: (0, 0)),
        out_shape=jax.ShapeDtypeStruct((K, npc * FOX_W), BF16),
        scratch_shapes=[pltpu.VMEM((K, npc * FOX_W), F32)],
        compiler_params=_cparams(("arbitrary",)),
    )(h1, *pieces)


def _matmul_tn(a, b, name, cast_b=False):
    T, K = a.shape
    N = b.shape[1]
    bt = min(T, 512)
    bk = _tile_div(K, 1536)
    bn = _tile_div(N, 1536)
    nt = T // bt

    def body(a_ref, b_ref, o_ref, acc_ref):
        t = pl.program_id(2)

        @pl.when(t == 0)
        def _():
            acc_ref[...] = jnp.zeros_like(acc_ref)

        bv = b_ref[...]
        if cast_b:
            bv = bv.astype(BF16)
        acc_ref[...] += _tn(a_ref[...], bv)

        @pl.when(t == nt - 1)
        def _():
            o_ref[...] = acc_ref[...].astype(BF16)

    return pl.pallas_call(
        body,
        name=name,
        grid=(K // bk, N // bn, nt),
        in_specs=[
            pl.BlockSpec((bt, bk), lambda k, n, t: (t, k)),
            pl.BlockSpec((bt, bn), lambda k, n, t: (t, n)),
        ],
        out_specs=pl.BlockSpec((bk, bn), lambda k, n, t: (k, n)),
        out_shape=jax.ShapeDtypeStruct((K, N), BF16),
        scratch_shapes=[pltpu.VMEM((bk, bn), F32)],
        compiler_params=_cparams(("arbitrary", "arbitrary", "arbitrary")),
    )(a, b)


def _local_step(x, mem, target, p, tm, tq, late=None):
    T, D = x.shape
    w_in = p["w_in"]
    w_qkv = w_in[:, :QKV_W]
    w_f_t = w_in[:, QKV_W:].T
    b_f = p["b_forget"].reshape(N_FOX, 1)

    proj, h1, xf, c = _inproj_fwd(x, p["attn_norm_g"], w_qkv, w_f_t, b_f, tm)
    c_col = c.reshape(N_FOX, 1, T)
    c_row = c.reshape(N_FOX, 1, T)
    c_ends = c[:, tq - 1::tq].reshape(N_FOX, 1, T // tq)
    fox_o, lse, gathered = _fox_fwd(proj, c_col, c_row, c_ends, tq, gather=[late[n] for n in _LATE] if late else ())
    if late:
        p = dict(p, **{n: _gathered_full(n, gv) for n, gv in zip(_LATE, gathered)})
    sb_o, sb_ltot, sb_live = _sb_fwd(proj, tq)
    x1, mixed = _post_attn_fwd(fox_o, sb_o, p["fox_out_g"], p["sb_out_g"], p["w_out"], x, tm)
    mb, kv = _mem_kv_fwd(mem, p["mem_norm_g"], p["w_mkv"])
    x2, h2, qb, om = _xattn_fwd(x1, p["xattn_norm_g"], p["w_mq"], kv, p["w_mo"], tm)
    tf = 2 * tm if T % (2 * tm) == 0 else tm
    x3, h3, ug, uv, yg, yv, a = _ffn_fwd(
        x2, p["ffn_norm_g"], p["w_up"], p["conv_w"], p["conv_b"], p["w_down"], tf)
    dx3, loss_blk, d_final_g = _loss_head(x3, p["final_norm_g"], target, tm)

    g = {"final_norm_g": d_final_g}
    dx2, du_g, du_v, g["ffn_norm_g"], dc_g, dc_v = _ffn_bwd(
        dx3, x2, p["ffn_norm_g"], ug, uv, yg, yv, p["conv_w"], p["w_down"], p["w_up"], tf)
    g["w_down"] = _matmul_tn(a, dx3, "dw_down", cast_b=True)
    g["w_up"] = jnp.concatenate([_matmul_tn(h3, du_g, "dw_up_gate"), _matmul_tn(h3, du_v, "dw_up_val")], axis=1)
    dconv = jnp.concatenate([dc_g, dc_v], axis=1)
    g["conv_w"] = dconv[0:3]
    g["conv_b"] = dconv[3:4]
    dx1, dq_m, dkv, g["xattn_norm_g"] = _xattn_bwd(dx2, x1, p["xattn_norm_g"], qb, kv, p["w_mo"], p["w_mq"], tm)
    g["w_mo"] = _matmul_tn(om, dx2, "dw_mo", cast_b=True)
    g["w_mq"] = _matmul_tn(h2, dq_m, "dw_mq")
    g["w_mkv"], g["mem_norm_g"] = _mem_kv_bwd(mem, p["mem_norm_g"], mb, dkv, p["w_mkv"])
    d_fox, d_sb, g["fox_out_g"], g["sb_out_g"] = _post_attn_bwd(
        dx1, fox_o, sb_o, p["fox_out_g"], p["sb_out_g"], p["w_out"], tm)
    g["w_out"] = _matmul_tn(mixed, dx1, "dw_out", cast_b=True)
    dq_s, dk_s, dv_s = _sb_bwd(proj, sb_ltot, sb_live, d_sb, tq)
    dq_f, dk_f, dv_f, dck, dcq, parts = _fox_bwd(
        proj, c_col, c_row, c_ends, lse, d_fox, fox_o, tq,
        scatter=[_grad_blocks(n, g[n]) for n in _LATE] if late else ())
    if late:
        g["parts"] = dict(zip(_LATE, parts))
    dxf, db, dwf_t = _forget_bwd(dcq.reshape(N_FOX, T), dck.reshape(N_FOX, T), xf, h1, min(T, 512))
    g["b_forget"] = db.reshape(1, N_FOX)
    pieces = [dq_f, dk_f, dv_f, dq_s, dk_s, dv_s]
    g["w_in"] = jnp.concatenate([_dw_in(h1, pieces, "dw_in"), dwf_t.T.astype(BF16)], axis=1)
    grad_x, g["attn_norm_g"], parts = _inproj_bwd(
        pieces, dxf.T, w_in, w_f_t, x, p["attn_norm_g"], dx1, tm,
        scatter=[_grad_blocks("w_in", g["w_in"])] if late else ())
    if late:
        (g["parts"]["w_in"],) = parts
    return loss_blk, grad_x, g


def _mesh_pos():
    return lax.axis_index("x"), lax.axis_index("y"), lax.axis_index("c")


def _flip(pos, k):
    return tuple(1 - v if (k >> b) & 1 else v for v, b in zip(pos, (2, 1, 0)))


def _slot(pos):
    return 4 * pos[0] + 2 * pos[1] + pos[2]


_CHIPS = (4, 2, 6)


def _comm_sems(n):
    return [pltpu.SemaphoreType.DMA((7 * n,)), pltpu.SemaphoreType.DMA((7 * n,)), pltpu.SemaphoreType.DMA((n,))]


class _Gather:
    def __init__(self, ins, outs, send_sems, recv_sems, local_sems):
        self.ins, self.outs, self.n = ins, outs, len(ins)
        self.send_sems, self.recv_sems, self.local_sems = send_sems, recv_sems, local_sems
        self.me = _mesh_pos()
        self.sibling = _flip(self.me, 1)

    def _copy(self, a, kk, block, to, src=None):
        rows = self.outs[a].at[_slot(block)]
        return pltpu.make_async_remote_copy(
            src_ref=rows if src is None else src, dst_ref=rows,
            send_sem=self.send_sems.at[7 * a + kk], recv_sem=self.recv_sems.at[7 * a + kk],
            device_id=to, device_id_type=MESH)

    def _mine(self):
        return [pltpu.make_async_copy(self.ins[a], self.outs[a].at[_slot(self.me)], self.local_sems.at[a])
                for a in range(self.n)]

    def _first(self):
        out = []
        for a in range(self.n):
            out.append(self._copy(a, 0, self.me, self.sibling, src=self.ins[a]))
            out += [self._copy(a, 1 + j, self.me, _flip(self.me, k), src=self.ins[a]) for j, k in enumerate(_CHIPS)]
        return out

    def _passed(self):
        return [self._copy(a, 4 + j, _flip(self.me, k), self.sibling)
                for j, k in enumerate(_CHIPS) for a in range(self.n)]

    def start(self):
        for cp in self._mine() + self._first():
            cp.start()

    def forward(self):
        for j, k in enumerate(_CHIPS):
            for a in range(self.n):
                self._copy(a, 1 + j, _flip(self.me, k), self.me).wait_recv()
                self._copy(a, 4 + j, _flip(self.me, k), self.sibling).start()

    def finish(self):
        for a in range(self.n):
            self._copy(a, 0, self.sibling, self.me).wait_recv()
            for j, k in enumerate(_CHIPS):
                self._copy(a, 4 + j, _flip(self.sibling, k), self.me).wait_recv()
        for cp in self._first() + self._passed():
            cp.wait_send()
        for cp in self._mine():
            cp.wait()


class _Scatter:
    def __init__(self, ins, outs, send_sems, recv_sems, local_sems):
        self.ins, self.outs, self.n = ins, outs, len(ins)
        self.send_sems, self.recv_sems, self.local_sems = send_sems, recv_sems, local_sems
        self.me = _mesh_pos()

    def _copy(self, a, k, landed=False):
        peer = _flip(self.me, k)
        return pltpu.make_async_remote_copy(
            src_ref=self.ins[a].at[_slot(peer)], dst_ref=self.outs[a].at[_slot(peer if landed else self.me)],
            send_sem=self.send_sems.at[7 * a + k - 1], recv_sem=self.recv_sems.at[7 * a + k - 1],
            device_id=peer, device_id_type=MESH)

    def _mine(self):
        s = _slot(self.me)
        return [pltpu.make_async_copy(self.ins[a].at[s], self.outs[a].at[s], self.local_sems.at[a])
                for a in range(self.n)]

    def start(self):
        for cp in self._mine() + [self._copy(a, k) for k in range(1, 8) for a in range(self.n)]:
            cp.start()

    def finish(self):
        for k in range(1, 8):
            for a in range(self.n):
                self._copy(a, k, landed=True).wait_recv()
        for k in range(1, 8):
            for a in range(self.n):
                self._copy(a, k).wait_send()
        for cp in self._mine():
            cp.wait()


_ANY = pl.BlockSpec(memory_space=pl.ANY)


def _gathered_shapes(shards):
    return [jax.ShapeDtypeStruct((N_DEV,) + s.shape, s.dtype) for s in shards]


def _all_gather(shards, name):
    n = len(shards)

    def body(*refs):
        g = _Gather(refs[:n], refs[n:2 * n], *refs[2 * n:])
        g.start()
        g.forward()
        g.finish()

    return pl.pallas_call(
        body, name=name, in_specs=[_ANY] * n, out_specs=[_ANY] * n,
        out_shape=_gathered_shapes(shards), scratch_shapes=_comm_sems(n),
    )(*shards)


def _adamw_math(w, g, m, v):
    m2 = ADAM_B1 * m + (1.0 - ADAM_B1) * g
    v2 = ADAM_B2 * v + (1.0 - ADAM_B2) * (g * g)
    m_hat = m2 / (1.0 - ADAM_B1 ** ADAM_STEP)
    v_hat = v2 / (1.0 - ADAM_B2 ** ADAM_STEP)
    delta = -ADAM_LR * (m_hat / (jnp.sqrt(v_hat) + ADAM_EPS) + ADAM_WD * w)
    return delta, m2, v2


def _adamw(w, parts, m, v, name):
    R, C = w.shape
    br = 128 if R % 128 == 0 else R

    def body(w_ref, p_ref, m_ref, v_ref, g_ref, d_ref, nm_ref, nv_ref):
        g = p_ref[0].astype(F32)
        for s in range(1, N_DEV):
            g = g + p_ref[s].astype(F32)
        g_ref[...] = g
        d_ref[...], nm_ref[...], nv_ref[...] = _adamw_math(w_ref[...], g, m_ref[...], v_ref[...])

    spec = pl.BlockSpec((br, C), lambda i: (i, 0))
    return pl.pallas_call(
        body,
        name=name,
        grid=(R // br,),
        in_specs=[spec, pl.BlockSpec((N_DEV, br, C), lambda i: (0, i, 0)), spec, spec],
        out_specs=[spec] * 4,
        out_shape=[jax.ShapeDtypeStruct((R, C), F32)] * 4,
        compiler_params=_cparams(("arbitrary",)),
    )(w, parts, m, v)


_SHARDED = ("w_in", "w_out", "w_mq", "w_mkv", "w_mo", "w_up", "conv_w", "w_down")
_LATE = _SHARDED[1:]
_COL_SHARDED = ("w_in", "w_mkv", "w_up", "conv_w")
_REPLICATED = ("attn_norm_g", "b_forget", "fox_out_g", "sb_out_g", "xattn_norm_g", "mem_norm_g",
               "ffn_norm_g", "conv_b", "final_norm_g")
_WEIGHTS = ("attn_norm_g", "w_in", "b_forget", "fox_out_g", "sb_out_g", "w_out", "xattn_norm_g", "mem_norm_g",
            "w_mq", "w_mkv", "w_mo", "ffn_norm_g", "w_up", "conv_w", "conv_b", "w_down", "final_norm_g")


def _pack_rows(n):
    return -(-n // 128)


def _pack(vals, rows_total):
    parts = []
    for v in vals:
        flat = v.reshape(-1)
        parts.append(jnp.pad(flat, (0, _pack_rows(flat.shape[0]) * 128 - flat.shape[0])))
    flat = jnp.concatenate(parts)
    return jnp.pad(flat, (0, rows_total * 128 - flat.shape[0])).reshape(rows_total, 128)


def _unpack(packed, shapes):
    out = []
    r = 0
    for shp in shapes:
        n = 1
        for d in shp:
            n *= d
        out.append(packed[r:r + _pack_rows(n)].reshape(-1)[:n].reshape(shp))
        r += _pack_rows(n)
    return out


def _gathered_full(name, gathered):
    if name in _COL_SHARDED:
        return jnp.transpose(gathered, (1, 0, 2)).reshape(gathered.shape[1], -1)
    return gathered.reshape(-1, gathered.shape[2])


def _to_blocks(name, full):
    if name in _COL_SHARDED:
        r = full.shape[0]
        return jnp.transpose(full.reshape(r, N_DEV, -1), (1, 0, 2))
    return full.reshape(N_DEV, -1, full.shape[1])


def _grad_blocks(name, full):
    blocks = _to_blocks(name, full)
    return blocks if name == "conv_w" else blocks.astype(BF16)


def _step(args, tm, tq):
    w = {n: args[n] for n in _WEIGHTS}
    mom = {n: args["m_" + n] for n in _WEIGHTS}
    var = {n: args["v_" + n] for n in _WEIGHTS}
    x = args["x"][0]
    mem = args["mem"][0]
    target = args["loss_target"][0]

    def flat2(a):
        return a.reshape(a.shape[-2], a.shape[-1]) if a.ndim == 3 else a.reshape(1, -1)

    shards = {n: flat2(w[n]) if n == "conv_w" else flat2(w[n]).astype(BF16) for n in _SHARDED}
    (w_in_all,) = _all_gather([shards["w_in"]], "gather_w_in")
    p = {"w_in": _gathered_full("w_in", w_in_all)}
    for n in _REPLICATED:
        p[n] = flat2(w[n])

    loss_blk, grad_x, g = _local_step(x, mem, target, p, tm, tq, late={n: shards[n] for n in _LATE})

    parts = g["parts"]
    out = {}
    for n in _SHARDED:
        res = _adamw(flat2(w[n]), parts[n], flat2(mom[n]), flat2(var[n]), "adamw_" + n)
        out[n] = [r.reshape(w[n].shape) for r in res]

    shapes = [w[n].shape for n in _REPLICATED]
    rows = sum(_pack_rows(flat2(w[n]).shape[1]) for n in _REPLICATED) + 1
    rows = -(-rows // 8) * 8
    g_pack = _pack([g[n] for n in _REPLICATED] + [loss_blk[0:1, :]], rows)
    (g_all,) = _all_gather([g_pack], "gather_small")
    res = _adamw(_pack([w[n] for n in _REPLICATED], rows), g_all,
                 _pack([mom[n] for n in _REPLICATED], rows), _pack([var[n] for n in _REPLICATED], rows),
                 "adamw_small")
    n_rows_params = sum(_pack_rows(flat2(w[n]).shape[1]) for n in _REPLICATED)
    loss = res[0][n_rows_params, 0]
    unpacked = [_unpack(r, shapes) for r in res]
    for k, n in enumerate(_REPLICATED):
        out[n] = [unpacked[q][k] for q in range(4)]

    grads = [out[n][0] for n in _WEIGHTS]
    deltas = [out[n][1] for n in _WEIGHTS]
    new_m = [out[n][2] for n in _WEIGHTS]
    new_v = [out[n][3] for n in _WEIGHTS]
    return (loss, grad_x[None], *grads, *deltas, *new_m, *new_v)


def kernel(x, mem, attn_norm_g, w_in, b_forget, fox_out_g, sb_out_g, w_out, xattn_norm_g, mem_norm_g, w_mq, w_mkv, w_mo, ffn_norm_g, w_up, conv_w, conv_b, w_down, final_norm_g, loss_target, m_attn_norm_g, m_w_in, m_b_forget, m_fox_out_g, m_sb_out_g, m_w_out, m_xattn_norm_g, m_mem_norm_g, m_w_mq, m_w_mkv, m_w_mo, m_ffn_norm_g, m_w_up, m_conv_w, m_conv_b, m_w_down, m_final_norm_g, v_attn_norm_g, v_w_in, v_b_forget, v_fox_out_g, v_sb_out_g, v_w_out, v_xattn_norm_g, v_mem_norm_g, v_w_mq, v_w_mkv, v_w_mo, v_ffn_norm_g, v_w_up, v_conv_w, v_conv_b, v_w_down, v_final_norm_g):
    args = dict(locals())
    T = x.shape[1]
    return _step(args, tm=min(T, 512), tq=min(T, 256))
```
